```python
import jax, jax.numpy as jnp
from jax import lax
import numpy as np

D_MODEL = 1024
BATCH = 8
SEQ = 4096
DEPTH = 4

CHUNK = 64
N_MIXERS = 2
EPS = 1e-6

GDN_QK_HEADS = 8
GDN_V_HEADS = 16
GDN_HEAD_DIM = 128
GDN_QK_WIDTH = GDN_QK_HEADS * GDN_HEAD_DIM
GDN_V_WIDTH = GDN_V_HEADS * GDN_HEAD_DIM
GDN_CONV_CH = 2 * GDN_QK_WIDTH + GDN_V_WIDTH
GDN_IN_WIDTH = GDN_CONV_CH + GDN_V_WIDTH + 2 * GDN_V_HEADS
CONV_WIDTH = 4

FOX_HEADS = 16
FOX_HEAD_DIM = 64
FOX_WIDTH = FOX_HEADS * FOX_HEAD_DIM
FOX_IN_WIDTH = 4 * FOX_WIDTH + FOX_HEADS
Q_BLOCK = 128

N_LAYERS_A = (DEPTH + 1) // 2
N_LAYERS_B = DEPTH // 2

kernel_name = "hybrid_gdn_fox_adaln_trunk"


def rmsnorm(x, w):
    xf = x.astype(jnp.float32)
    y = xf * lax.rsqrt(jnp.mean(xf * xf, axis=-1, keepdims=True) + EPS)
    return (y * w.astype(jnp.float32)).astype(x.dtype)


def l2norm(x):
    return x * lax.rsqrt(jnp.sum(x * x, axis=-1, keepdims=True) + EPS)


def causal_depthwise_conv(x, w):
    C = x.shape[-1]
    return lax.conv_general_dilated(
        x, w[:, None, :].astype(x.dtype), window_strides=(1,),
        padding=[(CONV_WIDTH - 1, 0)], dimension_numbers=('NWC', 'WIO', 'NWC'),
        feature_group_count=C)


def gated_delta_rule(q, k, v, g, beta):
    B, S, H, DK = q.shape
    DV = v.shape[-1]
    N = S // CHUNK

    def to_chunks(t):
        t = jnp.moveaxis(t, 2, 1)
        return t.reshape(t.shape[:2] + (N, CHUNK) + t.shape[3:])

    q, k, v, g, beta = (to_chunks(t) for t in (q, k, v, g, beta))
    g = jnp.cumsum(g, axis=-1)
    idx = jnp.arange(CHUNK)
    lower = idx[:, None] >= idx[None, :]
    strict = idx[:, None] > idx[None, :]
    decay = jnp.exp(jnp.where(lower, g[..., :, None] - g[..., None, :], -jnp.inf))
    kb = k * beta[..., None]
    vb = v * beta[..., None]
    L = jnp.where(strict, jnp.einsum('bhncd,bhnsd->bhncs', kb, k) * decay, 0.0)
    eye = jnp.eye(CHUNK, dtype=q.dtype)
    T = lax.linalg.triangular_solve(eye + L, jnp.broadcast_to(eye, L.shape),
                                    left_side=True, lower=True, unit_diagonal=True)
    u = T @ vb
    w = T @ (kb * jnp.exp(g)[..., None])
    attn = jnp.einsum('bhncd,bhnsd->bhncs', q, k) * decay
    q_dec = q * jnp.exp(g)[..., None]
    k_dec = k * jnp.exp(g[..., -1:] - g)[..., None]
    g_last = jnp.exp(g[..., -1])
    xs = tuple(jnp.moveaxis(t, 2, 0) for t in (q_dec, k_dec, u, w, attn, g_last))

    def step(state, inp):
        q_n, k_n, u_n, w_n, a_n, gl_n = inp
        v_new = u_n - jnp.einsum('bhcd,bhde->bhce', w_n, state)
        o = (jnp.einsum('bhcd,bhde->bhce', q_n, state)
             + jnp.einsum('bhcs,bhse->bhce', a_n, v_new))
        state = state * gl_n[..., None, None] + jnp.einsum('bhcd,bhce->bhde', k_n, v_new)
        return state, o

    state0 = jnp.zeros((B, H, DK, DV), q.dtype)
    _, o = lax.scan(step, state0, xs)
    o = jnp.moveaxis(o, 0, 2).reshape(B, H, S, DV)
    return jnp.moveaxis(o, 1, 2)


def gdn_mixer(h, w_in, conv_w, A_log, dt_bias, norm_w, w_out):
    B, S, _ = h.shape
    proj = h @ w_in
    qkv, z, b, a = jnp.split(proj, [GDN_CONV_CH, GDN_CONV_CH + GDN_V_WIDTH,
                                    GDN_CONV_CH + GDN_V_WIDTH + GDN_V_HEADS], axis=-1)
    qkv = jax.nn.silu(causal_depthwise_conv(qkv, conv_w))
    q, k, v = jnp.split(qkv, [GDN_QK_WIDTH, 2 * GDN_QK_WIDTH], axis=-1)
    rep = GDN_V_HEADS // GDN_QK_HEADS
    q = l2norm(q.reshape(B, S, GDN_QK_HEADS, GDN_HEAD_DIM).astype(jnp.float32)) * GDN_HEAD_DIM ** -0.5
    k = l2norm(k.reshape(B, S, GDN_QK_HEADS, GDN_HEAD_DIM).astype(jnp.float32))
    q = jnp.repeat(q, rep, axis=2)
    k = jnp.repeat(k, rep, axis=2)
    v = v.reshape(B, S, GDN_V_HEADS, GDN_HEAD_DIM).astype(jnp.float32)
    beta = jax.nn.sigmoid(b.astype(jnp.float32))
    g = -jnp.exp(A_log.astype(jnp.float32)) * jax.nn.softplus(
        a.astype(jnp.float32) + dt_bias.astype(jnp.float32))
    o = gated_delta_rule(q, k, v, g, beta)
    zg = jax.nn.silu(z.reshape(B, S, GDN_V_HEADS, GDN_HEAD_DIM).astype(jnp.float32))
    o = rmsnorm(o, norm_w) * zg
    return o.reshape(B, S, GDN_V_WIDTH).astype(h.dtype) @ w_out


def forgetting_attention(q, k, v, cum):
    B, S, H, DH = q.shape
    NB = S // Q_BLOCK
    cum_k = jnp.moveaxis(cum, 1, 2)
    q_blocks = jnp.moveaxis(q.reshape(B, NB, Q_BLOCK, H, DH), 1, 0)
    c_blocks = jnp.moveaxis(cum_k.reshape(B, H, NB, Q_BLOCK), 2, 0)
    key_pos = jnp.arange(S)

    def block(inp):
        q_b, c_b, start = inp
        logits = jnp.einsum('bqhd,bkhd->bhqk', q_b, k)
        logits = logits + (c_b[..., :, None] - cum_k[..., None, :])
        q_pos = start + jnp.arange(Q_BLOCK)
        mask = key_pos[None, :] <= q_pos[:, None]
        p = jax.nn.softmax(jnp.where(mask, logits, -jnp.inf), axis=-1)
        return jnp.einsum('bhqk,bkhd->bqhd', p, v)

    o = lax.map(block, (q_blocks, c_blocks, jnp.arange(NB) * Q_BLOCK))
    return jnp.moveaxis(o, 0, 1).reshape(B, S, H, DH)


def fox_mixer(h, w_in, f_bias, qn_w, kn_w, w_out):
    B, S, _ = h.shape
    proj = h @ w_in
    q, k, v, z, f = jnp.split(proj, [FOX_WIDTH, 2 * FOX_WIDTH, 3 * FOX_WIDTH, 4 * FOX_WIDTH], axis=-1)
    shp = (B, S, FOX_HEADS, FOX_HEAD_DIM)
    q = rmsnorm(q.reshape(shp).astype(jnp.float32), qn_w) * FOX_HEAD_DIM ** -0.5
    k = rmsnorm(k.reshape(shp).astype(jnp.float32), kn_w)
    v = v.reshape(shp).astype(jnp.float32)
    log_f = jax.nn.log_sigmoid(f.astype(jnp.float32) + f_bias.astype(jnp.float32))
    cum = jnp.cumsum(log_f, axis=1)
    o = forgetting_attention(q, k, v, cum).reshape(B, S, FOX_WIDTH)
    o = o * jax.nn.silu(z.astype(jnp.float32))
    return o.astype(h.dtype) @ w_out


def _fwd_setup_inputs(seed: int = 0) -> dict:
    key = jax.random.key(seed)
    ks = jax.random.split(key, 20)
    nrm = jax.random.normal
    D = D_MODEL
    dt = jnp.exp(jax.random.uniform(ks[7], (N_LAYERS_A, GDN_V_HEADS),
                                    minval=np.log(1e-3), maxval=np.log(1e-1)))
    return {
        "x": nrm(ks[0], (BATCH, SEQ, D), jnp.float32),
        "c": nrm(ks[1], (BATCH, D), jnp.float32),
        "norm_w": 1.0 + 0.1 * nrm(ks[2], (DEPTH, D), jnp.float32),
        "ada_w": 0.5 * D ** -0.5 * nrm(ks[3], (DEPTH, D, 3 * D), jnp.float32),
        "ada_b": 0.02 * nrm(ks[4], (DEPTH, 3 * D), jnp.float32),
        "a_w_in": D ** -0.5 * nrm(ks[5], (N_LAYERS_A, D, GDN_IN_WIDTH), jnp.float32),
        "a_conv_w": CONV_WIDTH ** -0.5 * nrm(ks[6], (N_LAYERS_A, CONV_WIDTH, GDN_CONV_CH), jnp.float32),
        "a_A_log": jnp.log(jax.random.uniform(ks[8], (N_LAYERS_A, GDN_V_HEADS), minval=1.0, maxval=16.0)),
        "a_dt_bias": dt + jnp.log(-jnp.expm1(-dt)),
        "a_norm_w": 1.0 + 0.1 * nrm(ks[9], (N_LAYERS_A, GDN_HEAD_DIM), jnp.float32),
        "a_w_out": GDN_V_WIDTH ** -0.5 * nrm(ks[10], (N_LAYERS_A, GDN_V_WIDTH, D), jnp.float32),
        "b_w_in": D ** -0.5 * nrm(ks[11], (N_LAYERS_B, D, FOX_IN_WIDTH), jnp.float32),
        "b_f_bias": jax.random.uniform(ks[12], (N_LAYERS_B, FOX_HEADS), minval=1.0, maxval=5.0),
        "b_qn_w": 1.0 + 0.1 * nrm(ks[13], (N_LAYERS_B, FOX_HEAD_DIM), jnp.float32),
        "b_kn_w": 1.0 + 0.1 * nrm(ks[14], (N_LAYERS_B, FOX_HEAD_DIM), jnp.float32),
        "b_w_out": FOX_WIDTH ** -0.5 * nrm(ks[15], (N_LAYERS_B, FOX_WIDTH, D), jnp.float32),
        "final_norm_w": 1.0 + 0.1 * nrm(ks[16], (D,), jnp.float32),
    }


def _fwd_reference(x, c, norm_w, ada_w, ada_b, a_w_in, a_conv_w, a_A_log, a_dt_bias, a_norm_w,
              a_w_out, b_w_in, b_f_bias, b_qn_w, b_kn_w, b_w_out, final_norm_w):
    cond = jax.nn.silu(c)
    for i in range(DEPTH):
        mod = cond @ ada_w[i] + ada_b[i]
        shift, scale, gate = jnp.split(mod[:, None, :], 3, axis=-1)
        h = rmsnorm(x, norm_w[i]) * (1 + scale) + shift
        j = i // N_MIXERS
        if i % N_MIXERS == 0:
            y = gdn_mixer(h, a_w_in[j], a_conv_w[j], a_A_log[j], a_dt_bias[j], a_norm_w[j], a_w_out[j])
        else:
            y = fox_mixer(h, b_w_in[j], b_f_bias[j], b_qn_w[j], b_kn_w[j], b_w_out[j])
        x = x + gate * y
    return rmsnorm(x, final_norm_w)


import jax as _jax
import jax.numpy as _jnp

TWIN_FORMAT = 'train_step'
FWD_PARAMS = ['x', 'c', 'norm_w', 'ada_w', 'ada_b', 'a_w_in', 'a_conv_w', 'a_A_log', 'a_dt_bias', 'a_norm_w', 'a_w_out', 'b_w_in', 'b_f_bias', 'b_qn_w', 'b_kn_w', 'b_w_out', 'final_norm_w']
TWIN_WEIGHTS = ['norm_w', 'ada_w', 'ada_b', 'a_w_in', 'a_conv_w', 'a_A_log', 'a_dt_bias', 'a_norm_w', 'a_w_out', 'b_w_in', 'b_f_bias', 'b_qn_w', 'b_kn_w', 'b_w_out', 'final_norm_w']
TWIN_DIFF_INPUT = 'x'
TWIN_INPUTS = ['x', 'c', 'norm_w', 'ada_w', 'ada_b', 'a_w_in', 'a_conv_w', 'a_A_log', 'a_dt_bias', 'a_norm_w', 'a_w_out', 'b_w_in', 'b_f_bias', 'b_qn_w', 'b_kn_w', 'b_w_out', 'final_norm_w', 'loss_target', 'm_norm_w', 'm_ada_w', 'm_ada_b', 'm_a_w_in', 'm_a_conv_w', 'm_a_A_log', 'm_a_dt_bias', 'm_a_norm_w', 'm_a_w_out', 'm_b_w_in', 'm_b_f_bias', 'm_b_qn_w', 'm_b_kn_w', 'm_b_w_out', 'm_final_norm_w', 'v_norm_w', 'v_ada_w', 'v_ada_b', 'v_a_w_in', 'v_a_conv_w', 'v_a_A_log', 'v_a_dt_bias', 'v_a_norm_w', 'v_a_w_out', 'v_b_w_in', 'v_b_f_bias', 'v_b_qn_w', 'v_b_kn_w', 'v_b_w_out', 'v_final_norm_w']
TWIN_OUTPUTS = ['loss', 'grad_x', 'grad_norm_w', 'grad_ada_w', 'grad_ada_b', 'grad_a_w_in', 'grad_a_conv_w', 'grad_a_A_log', 'grad_a_dt_bias', 'grad_a_norm_w', 'grad_a_w_out', 'grad_b_w_in', 'grad_b_f_bias', 'grad_b_qn_w', 'grad_b_kn_w', 'grad_b_w_out', 'grad_final_norm_w', 'delta_norm_w', 'delta_ada_w', 'delta_ada_b', 'delta_a_w_in', 'delta_a_conv_w', 'delta_a_A_log', 'delta_a_dt_bias', 'delta_a_norm_w', 'delta_a_w_out', 'delta_b_w_in', 'delta_b_f_bias', 'delta_b_qn_w', 'delta_b_kn_w', 'delta_b_w_out', 'delta_final_norm_w', 'new_m_norm_w', 'new_m_ada_w', 'new_m_ada_b', 'new_m_a_w_in', 'new_m_a_conv_w', 'new_m_a_A_log', 'new_m_a_dt_bias', 'new_m_a_norm_w', 'new_m_a_w_out', 'new_m_b_w_in', 'new_m_b_f_bias', 'new_m_b_qn_w', 'new_m_b_kn_w', 'new_m_b_w_out', 'new_m_final_norm_w', 'new_v_norm_w', 'new_v_ada_w', 'new_v_ada_b', 'new_v_a_w_in', 'new_v_a_conv_w', 'new_v_a_A_log', 'new_v_a_dt_bias', 'new_v_a_norm_w', 'new_v_a_w_out', 'new_v_b_w_in', 'new_v_b_f_bias', 'new_v_b_qn_w', 'new_v_b_kn_w', 'new_v_b_w_out', 'new_v_final_norm_w']
TWIN_LEAF_KINDS = {'loss': 'loss', 'grad_x': 'grad_x', 'grad_norm_w': 'grad_w', 'grad_ada_w': 'grad_w', 'grad_ada_b': 'grad_w', 'grad_a_w_in': 'grad_w', 'grad_a_conv_w': 'grad_w', 'grad_a_A_log': 'grad_w', 'grad_a_dt_bias': 'grad_w', 'grad_a_norm_w': 'grad_w', 'grad_a_w_out': 'grad_w', 'grad_b_w_in': 'grad_w', 'grad_b_f_bias': 'grad_w', 'grad_b_qn_w': 'grad_w', 'grad_b_kn_w': 'grad_w', 'grad_b_w_out': 'grad_w', 'grad_final_norm_w': 'grad_w', 'delta_norm_w': 'delta_w', 'delta_ada_w': 'delta_w', 'delta_ada_b': 'delta_w', 'delta_a_w_in': 'delta_w', 'delta_a_conv_w': 'delta_w', 'delta_a_A_log': 'delta_w', 'delta_a_dt_bias': 'delta_w', 'delta_a_norm_w': 'delta_w', 'delta_a_w_out': 'delta_w', 'delta_b_w_in': 'delta_w', 'delta_b_f_bias': 'delta_w', 'delta_b_qn_w': 'delta_w', 'delta_b_kn_w': 'delta_w', 'delta_b_w_out': 'delta_w', 'delta_final_norm_w': 'delta_w', 'new_m_norm_w': 'new_m', 'new_m_ada_w': 'new_m', 'new_m_ada_b': 'new_m', 'new_m_a_w_in': 'new_m', 'new_m_a_conv_w': 'new_m', 'new_m_a_A_log': 'new_m', 'new_m_a_dt_bias': 'new_m', 'new_m_a_norm_w': 'new_m', 'new_m_a_w_out': 'new_m', 'new_m_b_w_in': 'new_m', 'new_m_b_f_bias': 'new_m', 'new_m_b_qn_w': 'new_m', 'new_m_b_kn_w': 'new_m', 'new_m_b_w_out': 'new_m', 'new_m_final_norm_w': 'new_m', 'new_v_norm_w': 'new_v', 'new_v_ada_w': 'new_v', 'new_v_ada_b': 'new_v', 'new_v_a_w_in': 'new_v', 'new_v_a_conv_w': 'new_v', 'new_v_a_A_log': 'new_v', 'new_v_a_dt_bias': 'new_v', 'new_v_a_norm_w': 'new_v', 'new_v_a_w_out': 'new_v', 'new_v_b_w_in': 'new_v', 'new_v_b_f_bias': 'new_v', 'new_v_b_qn_w': 'new_v', 'new_v_b_kn_w': 'new_v', 'new_v_b_w_out': 'new_v', 'new_v_final_norm_w': 'new_v'}


def _forward(args):
    return _fwd_reference(*[args[k] for k in FWD_PARAMS])


def _output_shape():
    out = _jax.eval_shape(lambda: _forward(_fwd_setup_inputs(0)))
    return out.shape, out.dtype

N_MICROBATCH = 1
ADAM_LR = 0.001
ADAM_B1 = 0.9
ADAM_B2 = 0.999
ADAM_EPS = 1e-08
ADAM_WD = 0.01
ADAM_STEP = 10
PER_EXAMPLE_BATCH_AXIS = {'x': 0, 'c': 0, 'loss_target': 0}
SHARED_INPUTS = []
_WEIGHT_DTYPES = {'norm_w': _jnp.float32, 'ada_w': _jnp.float32, 'ada_b': _jnp.float32, 'a_w_in': _jnp.float32, 'a_conv_w': _jnp.float32, 'a_A_log': _jnp.float32, 'a_dt_bias': _jnp.float32, 'a_norm_w': _jnp.float32, 'a_w_out': _jnp.float32, 'b_w_in': _jnp.float32, 'b_f_bias': _jnp.float32, 'b_qn_w': _jnp.float32, 'b_kn_w': _jnp.float32, 'b_w_out': _jnp.float32, 'final_norm_w': _jnp.float32}
MOMENT_SCALE = {'norm_w': 4.615187e-02, 'ada_w': 7.430760e-02, 'ada_b': 1.338369e-01, 'a_w_in': 2.570693e-02, 'a_conv_w': 2.555366e-02, 'a_A_log': 1.090362e-01, 'a_dt_bias': 1.034445e-01, 'a_norm_w': 1.070862e-01, 'a_w_out': 3.950158e-02, 'b_w_in': 1.542340e-02, 'b_f_bias': 6.879674e-02, 'b_qn_w': 4.021647e-02, 'b_kn_w': 4.125638e-02, 'b_w_out': 1.871412e-02, 'final_norm_w': 3.226961e+01}


def _to_microbatches(a, axis):
    t = _jnp.moveaxis(a, axis, 0)
    t = t.reshape((N_MICROBATCH, t.shape[0] // N_MICROBATCH) + t.shape[1:])
    return _jnp.moveaxis(t, 1, axis + 1)


def setup_inputs(seed: int = 0) -> dict:
    inp = _fwd_setup_inputs(seed)
    key = _jax.random.fold_in(_jax.random.key(seed), 7919)
    shape, _ = _output_shape()
    out = dict(inp)
    out["loss_target"] = _jax.random.normal(_jax.random.fold_in(key, 0), shape, _jnp.float32)
    for i, name in enumerate(TWIN_WEIGHTS):
        w = inp[name].astype(_jnp.float32)
        if MOMENT_SCALE is None:
            s = _jnp.sqrt(_jnp.mean(_jnp.square(w)) + 1e-30)
        else:
            s = MOMENT_SCALE[name]
        km, kv = _jax.random.split(_jax.random.fold_in(key, i + 1))
        out[name] = w
        out["m_" + name] = s * _jax.random.normal(km, w.shape, _jnp.float32)
        out["v_" + name] = (s * s) * _jax.random.uniform(kv, w.shape, _jnp.float32, 0.5, 1.5)
    if N_MICROBATCH > 1:
        for name, axis in PER_EXAMPLE_BATCH_AXIS.items():
            out[name] = _to_microbatches(out[name], axis)
    return {'x': out['x'], 'c': out['c'], 'norm_w': out['norm_w'], 'ada_w': out['ada_w'], 'ada_b': out['ada_b'], 'a_w_in': out['a_w_in'], 'a_conv_w': out['a_conv_w'], 'a_A_log': out['a_A_log'], 'a_dt_bias': out['a_dt_bias'], 'a_norm_w': out['a_norm_w'], 'a_w_out': out['a_w_out'], 'b_w_in': out['b_w_in'], 'b_f_bias': out['b_f_bias'], 'b_qn_w': out['b_qn_w'], 'b_kn_w': out['b_kn_w'], 'b_w_out': out['b_w_out'], 'final_norm_w': out['final_norm_w'], 'loss_target': out['loss_target'], 'm_norm_w': out['m_norm_w'], 'm_ada_w': out['m_ada_w'], 'm_ada_b': out['m_ada_b'], 'm_a_w_in': out['m_a_w_in'], 'm_a_conv_w': out['m_a_conv_w'], 'm_a_A_log': out['m_a_A_log'], 'm_a_dt_bias': out['m_a_dt_bias'], 'm_a_norm_w': out['m_a_norm_w'], 'm_a_w_out': out['m_a_w_out'], 'm_b_w_in': out['m_b_w_in'], 'm_b_f_bias': out['m_b_f_bias'], 'm_b_qn_w': out['m_b_qn_w'], 'm_b_kn_w': out['m_b_kn_w'], 'm_b_w_out': out['m_b_w_out'], 'm_final_norm_w': out['m_final_norm_w'], 'v_norm_w': out['v_norm_w'], 'v_ada_w': out['v_ada_w'], 'v_ada_b': out['v_ada_b'], 'v_a_w_in': out['v_a_w_in'], 'v_a_conv_w': out['v_a_conv_w'], 'v_a_A_log': out['v_a_A_log'], 'v_a_dt_bias': out['v_a_dt_bias'], 'v_a_norm_w': out['v_a_norm_w'], 'v_a_w_out': out['v_a_w_out'], 'v_b_w_in': out['v_b_w_in'], 'v_b_f_bias': out['v_b_f_bias'], 'v_b_qn_w': out['v_b_qn_w'], 'v_b_kn_w': out['v_b_kn_w'], 'v_b_w_out': out['v_b_w_out'], 'v_final_norm_w': out['v_final_norm_w']}


def _loss(weights, diff, rest, loss_target):
    with _jax.named_scope("forward"):
        args = {**rest, TWIN_DIFF_INPUT: diff, **{k: w.astype(_WEIGHT_DTYPES[k]) for k, w in weights.items()}}
        y = _forward(args)
    with _jax.named_scope("loss_head"):
        err = _jnp.square(y.astype(_jnp.float32) - loss_target)
        return 0.5 * _jnp.sum(_jnp.mean(err, axis=-1)) if err.ndim else 0.5 * err


def _adamw(w, g, m, v):
    m = ADAM_B1 * m + (1.0 - ADAM_B1) * g
    v = ADAM_B2 * v + (1.0 - ADAM_B2) * _jnp.square(g)
    m_hat = m / (1.0 - ADAM_B1 ** ADAM_STEP)
    v_hat = v / (1.0 - ADAM_B2 ** ADAM_STEP)
    delta = -ADAM_LR * (m_hat / (_jnp.sqrt(v_hat) + ADAM_EPS) + ADAM_WD * w)
    return delta, m, v


def reference(x, c, norm_w, ada_w, ada_b, a_w_in, a_conv_w, a_A_log, a_dt_bias, a_norm_w, a_w_out, b_w_in, b_f_bias, b_qn_w, b_kn_w, b_w_out, final_norm_w, loss_target, m_norm_w, m_ada_w, m_ada_b, m_a_w_in, m_a_conv_w, m_a_A_log, m_a_dt_bias, m_a_norm_w, m_a_w_out, m_b_w_in, m_b_f_bias, m_b_qn_w, m_b_kn_w, m_b_w_out, m_final_norm_w, v_norm_w, v_ada_w, v_ada_b, v_a_w_in, v_a_conv_w, v_a_A_log, v_a_dt_bias, v_a_norm_w, v_a_w_out, v_b_w_in, v_b_f_bias, v_b_qn_w, v_b_kn_w, v_b_w_out, v_final_norm_w):
    given = dict(x=x, c=c, norm_w=norm_w, ada_w=ada_w, ada_b=ada_b, a_w_in=a_w_in, a_conv_w=a_conv_w, a_A_log=a_A_log, a_dt_bias=a_dt_bias, a_norm_w=a_norm_w, a_w_out=a_w_out, b_w_in=b_w_in, b_f_bias=b_f_bias, b_qn_w=b_qn_w, b_kn_w=b_kn_w, b_w_out=b_w_out, final_norm_w=final_norm_w, loss_target=loss_target, m_norm_w=m_norm_w, m_ada_w=m_ada_w, m_ada_b=m_ada_b, m_a_w_in=m_a_w_in, m_a_conv_w=m_a_conv_w, m_a_A_log=m_a_A_log, m_a_dt_bias=m_a_dt_bias, m_a_norm_w=m_a_norm_w, m_a_w_out=m_a_w_out, m_b_w_in=m_b_w_in, m_b_f_bias=m_b_f_bias, m_b_qn_w=m_b_qn_w, m_b_kn_w=m_b_kn_w, m_b_w_out=m_b_w_out, m_final_norm_w=m_final_norm_w, v_norm_w=v_norm_w, v_ada_w=v_ada_w, v_ada_b=v_ada_b, v_a_w_in=v_a_w_in, v_a_conv_w=v_a_conv_w, v_a_A_log=v_a_A_log, v_a_dt_bias=v_a_dt_bias, v_a_norm_w=v_a_norm_w, v_a_w_out=v_a_w_out, v_b_w_in=v_b_w_in, v_b_f_bias=v_b_f_bias, v_b_qn_w=v_b_qn_w, v_b_kn_w=v_b_kn_w, v_b_w_out=v_b_w_out, v_final_norm_w=v_final_norm_w)
    weights = {n: given[n] for n in TWIN_WEIGHTS}
    shared = {n: given[n] for n in SHARED_INPUTS}
    per_example = {n: given[n] for n in ['x', 'c']}
    grad_fn = _jax.value_and_grad(_loss, argnums=(0, 1))

    def one_microbatch(ex, loss_target):
        ex = dict(ex)
        diff = ex.pop(TWIN_DIFF_INPUT)
        return grad_fn(weights, diff, {**shared, **ex}, loss_target)

    if N_MICROBATCH == 1:
        loss, (grad_w, grad_x) = one_microbatch(per_example, given["loss_target"])
    else:
        def body(carry, xs):
            loss_sum, grad_sum = carry
            l_k, (gw_k, gx_k) = one_microbatch(xs[0], xs[1])
            with _jax.named_scope("update"):
                return (loss_sum + l_k, _jax.tree.map(_jnp.add, grad_sum, gw_k)), gx_k

        init = (_jnp.zeros((), _jnp.float32), _jax.tree.map(_jnp.zeros_like, weights))
        (loss, grad_w), grad_x = _jax.lax.scan(body, init, (per_example, given["loss_target"]))
    with _jax.named_scope("update"):
        delta_w, new_m, new_v = {}, {}, {}
        for n in TWIN_WEIGHTS:
            delta_w[n], new_m[n], new_v[n] = _adamw(weights[n], grad_w[n], given["m_" + n], given["v_" + n])
    return (loss, grad_x, *[grad_w[n] for n in TWIN_WEIGHTS], *[delta_w[n] for n in TWIN_WEIGHTS],
            *[new_m[n] for n in TWIN_WEIGHTS], *[new_v[n] for n in TWIN_WEIGHTS])
```

```python
import functools

import jax
import jax.numpy as jnp
from jax import lax
from jax.experimental import pallas as pl
from jax.experimental.pallas import tpu as pltpu

f32, bf16 = jnp.float32, jnp.bfloat16
HI = lax.Precision.HIGHEST
MESH = pl.DeviceIdType.MESH

EPS = 1e-6
D = 1024
CHUNK = 64
GQK_H, GV_H, GHD = 8, 16, 128
G_CONV = 4096
G_Z0 = 4096
G_BA0 = 6144
G_IN, G_INP = 6176, 6272
CONV_K = 4
F_H, F_HD = 16, 64
F_W = 1024
F_F0 = 4096
F_IN, F_INP = 4112, 4224
LR, B1, B2, AEPS, WD, STEP = 0.001, 0.9, 0.999, 1e-08, 0.01, 10
NEG = -1e30
VMEM_LIMIT = 56 * 1024 * 1024


def _nn(a, b, prec=None):
    return lax.dot_general(a, b, (((1,), (0,)), ((), ())), preferred_element_type=f32, precision=prec)


def _nt(a, b, prec=None):
    return lax.dot_general(a, b, (((1,), (1,)), ((), ())), preferred_element_type=f32, precision=prec)


def _tn(a, b, prec=None):
    return lax.dot_general(a, b, (((0,), (0,)), ((), ())), preferred_element_type=f32, precision=prec)


def _iota(shape, axis):
    return lax.broadcasted_iota(jnp.int32, shape, axis)


def _sigmoid(x):
    return 1.0 / (1.0 + jnp.exp(-x))


def _softplus(x):
    return jnp.maximum(x, 0.0) + jnp.log(1.0 + jnp.exp(-jnp.abs(x)))


def _silu(x):
    return x * _sigmoid(x)


def _dsilu(x):
    s = _sigmoid(x)
    return s * (1.0 + x * (1.0 - s))


def _params(sem=None, vmem=VMEM_LIMIT):
    return pltpu.CompilerParams(dimension_semantics=sem, vmem_limit_bytes=vmem)


def _row(v):
    return v.reshape(1, -1)


def _pick(n, pref):
    for t in pref:
        if n % t == 0:
            return t
    return n


def matmul(a, b, mode, name, out_dtype=f32):
    if mode == "nn":
        (M, K), (_, N) = a.shape, b.shape
    elif mode == "nt":
        (M, K), (N, _) = a.shape, b.shape
    else:
        (K, M), (_, N) = a.shape, b.shape
    tm = _pick(M, (512, 256, 128))
    tn = _pick(N, (1408, 1024, 896, 512, 384, 256, 128))
    tk = K if K <= 2048 else _pick(K, (1408, 1024, 896, 512, 384, 256, 128))
    nk = K // tk
    dot = {"nn": _nn, "nt": _nt, "tn": _tn}[mode]

    def body(a_ref, b_ref, o_ref, *acc):
        k = pl.program_id(2)
        part = dot(a_ref[...], b_ref[...])
        if nk == 1:
            o_ref[...] = part.astype(out_dtype)
        else:
            acc_ref = acc[0]

            @pl.when(k == 0)
            def _():
                acc_ref[...] = part

            @pl.when(k > 0)
            def _():
                acc_ref[...] += part

            @pl.when(k == nk - 1)
            def _():
                o_ref[...] = acc_ref[...].astype(out_dtype)

    a_spec = pl.BlockSpec((tk, tm), lambda i, j, k: (k, i)) if mode == "tn" else pl.BlockSpec((tm, tk), lambda i, j, k: (i, k))
    b_spec = pl.BlockSpec((tn, tk), lambda i, j, k: (j, k)) if mode == "nt" else pl.BlockSpec((tk, tn), lambda i, j, k: (k, j))
    return pl.pallas_call(
        body, name=name, grid=(M // tm, N // tn, nk),
        in_specs=[a_spec, b_spec], out_specs=pl.BlockSpec((tm, tn), lambda i, j, k: (i, j)),
        out_shape=jax.ShapeDtypeStruct((M, N), out_dtype),
        scratch_shapes=[] if nk == 1 else [pltpu.VMEM((tm, tn), f32)],
        compiler_params=_params(("parallel", "parallel", "arbitrary")),
    )(a, b)


def out_proj(o2, w, x, gate, name):
    S, K = o2.shape
    N = w.shape[1]
    tm, tn = 512, 512

    def body(a_ref, b_ref, x_ref, g_ref, y_ref, xn_ref):
        y = _nn(a_ref[...], b_ref[...])
        y_ref[...] = y
        xn_ref[...] = x_ref[...] + g_ref[...] * y

    return pl.pallas_call(
        body, name=name, grid=(S // tm, N // tn),
        in_specs=[pl.BlockSpec((tm, K), lambda i, j: (i, 0)), pl.BlockSpec((K, tn), lambda i, j: (0, j)),
                  pl.BlockSpec((tm, tn), lambda i, j: (i, j)), pl.BlockSpec((1, tn), lambda i, j: (0, j))],
        out_specs=[pl.BlockSpec((tm, tn), lambda i, j: (i, j))] * 2,
        out_shape=[jax.ShapeDtypeStruct((S, N), f32)] * 2,
        compiler_params=_params(("parallel", "parallel")),
    )(o2, w, x, gate)


def ln_mod(x, nw, scale, shift):
    S = x.shape[0]
    tm = 512

    def body(x_ref, nw_ref, sc_ref, sh_ref, h_ref):
        xv = x_ref[...]
        r = lax.rsqrt(jnp.mean(xv * xv, axis=-1, keepdims=True) + EPS)
        h_ref[...] = ((xv * r) * nw_ref[...] * (1.0 + sc_ref[...]) + sh_ref[...]).astype(bf16)

    vec = pl.BlockSpec((1, D), lambda i: (0, 0))
    return pl.pallas_call(
        body, name="ln_mod", grid=(S // tm,),
        in_specs=[pl.BlockSpec((tm, D), lambda i: (i, 0)), vec, vec, vec],
        out_specs=pl.BlockSpec((tm, D), lambda i: (i, 0)),
        out_shape=jax.ShapeDtypeStruct((S, D), bf16),
        compiler_params=_params(("parallel",)),
    )(x, nw, scale, shift)


def ln_mod_bwd(x, nw, scale, dh, dxres):
    S = x.shape[0]
    tm = 512
    nb = S // tm

    def body(x_ref, nw_ref, sc_ref, dh_ref, dr_ref, dx_ref, st_ref):
        i = pl.program_id(0)
        xv = x_ref[...]
        r = lax.rsqrt(jnp.mean(xv * xv, axis=-1, keepdims=True) + EPS)
        xn = xv * r
        dh = dh_ref[...]
        dxn = dh * (nw_ref[...] * (1.0 + sc_ref[...]))
        dx_ref[...] = dr_ref[...] + r * (dxn - xn * jnp.mean(dxn * xn, axis=-1, keepdims=True))
        p1 = jnp.sum(dh * xn, axis=0, keepdims=True)
        p2 = jnp.sum(dh, axis=0, keepdims=True)
        upd = jnp.concatenate([p1, p1, p2, jnp.zeros((5, D), f32)], axis=0)

        @pl.when(i == 0)
        def _():
            st_ref[...] = upd

        @pl.when(i > 0)
        def _():
            st_ref[...] += upd

        @pl.when(i == nb - 1)
        def _():
            st_ref[0:1, :] = st_ref[0:1, :] * (1.0 + sc_ref[...])
            st_ref[1:2, :] = st_ref[1:2, :] * nw_ref[...]

    vec = pl.BlockSpec((1, D), lambda i: (0, 0))
    tile = pl.BlockSpec((tm, D), lambda i: (i, 0))
    return pl.pallas_call(
        body, name="ln_mod_bwd", grid=(S // tm,),
        in_specs=[tile, vec, vec, tile, tile],
        out_specs=[tile, pl.BlockSpec((8, D), lambda i: (0, 0))],
        out_shape=[jax.ShapeDtypeStruct((S, D), f32), jax.ShapeDtypeStruct((8, D), f32)],
        compiler_params=_params(("arbitrary",)),
    )(x, nw, scale, dh, dxres)


def final_loss(x, fw, tgt):
    S = x.shape[0]
    tm = 512

    def body(x_ref, w_ref, t_ref, dx_ref, st_ref):
        i = pl.program_id(0)
        xv = x_ref[...]
        r = lax.rsqrt(jnp.mean(xv * xv, axis=-1, keepdims=True) + EPS)
        xn = xv * r
        err = xn * w_ref[...] - t_ref[...]
        dy = err * (1.0 / D)
        dxn = dy * w_ref[...]
        dx_ref[...] = r * (dxn - xn * jnp.mean(dxn * xn, axis=-1, keepdims=True))
        p1 = jnp.sum(dy * xn, axis=0, keepdims=True)
        p2 = jnp.sum(err * err, axis=0, keepdims=True) * (0.5 / D)
        upd = jnp.concatenate([p1, p2, jnp.zeros((6, D), f32)], axis=0)

        @pl.when(i == 0)
        def _():
            st_ref[...] = upd

        @pl.when(i > 0)
        def _():
            st_ref[...] += upd

    tile = pl.BlockSpec((tm, D), lambda i: (i, 0))
    return pl.pallas_call(
        body, name="final_loss", grid=(S // tm,),
        in_specs=[tile, pl.BlockSpec((1, D), lambda i: (0, 0)), tile],
        out_specs=[tile, pl.BlockSpec((8, D), lambda i: (0, 0))],
        out_shape=[jax.ShapeDtypeStruct((S, D), f32), jax.ShapeDtypeStruct((8, D), f32)],
        compiler_params=_params(("arbitrary",)),
    )(x, fw, tgt)


def gate_bwd(dx, y, gate):
    S = dx.shape[0]
    tm = 512

    def body(dx_ref, y_ref, g_ref, dy_ref, st_ref):
        i = pl.program_id(0)
        dxv = dx_ref[...]
        dy_ref[...] = (g_ref[...] * dxv).astype(bf16)
        upd = jnp.concatenate([jnp.sum(dxv * y_ref[...], axis=0, keepdims=True), jnp.zeros((7, D), f32)], axis=0)

        @pl.when(i == 0)
        def _():
            st_ref[...] = upd

        @pl.when(i > 0)
        def _():
            st_ref[...] += upd

    tile = pl.BlockSpec((tm, D), lambda i: (i, 0))
    return pl.pallas_call(
        body, name="gate_bwd", grid=(S // tm,),
        in_specs=[tile, tile, pl.BlockSpec((1, D), lambda i: (0, 0))],
        out_specs=[tile, pl.BlockSpec((8, D), lambda i: (0, 0))],
        out_shape=[jax.ShapeDtypeStruct((S, D), bf16), jax.ShapeDtypeStruct((8, D), f32)],
        compiler_params=_params(("arbitrary",)),
    )(dx, y, gate)


def _chunk_mats(tm):
    r, c = _iota((tm, tm), 0), _iota((tm, tm), 1)
    same = jnp.right_shift(r, 6) == jnp.right_shift(c, 6)
    ltri = jnp.where(same & (c <= r), 1.0, 0.0).astype(f32)
    utri = jnp.where(same & (c >= r), 1.0, 0.0).astype(f32)
    bsame = jnp.where(same, 1.0, 0.0).astype(f32)
    return ltri, utri, bsame


def _gdn_scalars(ba, alog, dtb, ltri, bsame):
    beta = _sigmoid(ba[:, 0:16])
    u = ba[:, 16:32] + dtb
    neg_a = -jnp.exp(alog)
    g = neg_a * _softplus(u)
    gc = _nn(ltri, g, HI)
    glast = _nn(bsame, g, HI)
    return beta, u, neg_a, g, gc, glast


def _conv_taps(p_ref, halo_ref, first, gi):
    cs = slice(gi * 128, (gi + 1) * 128)
    cur = p_ref[:, cs]
    hal = jnp.where(first, 0.0, halo_ref[:, cs])
    ext = jnp.concatenate([hal, cur], axis=0)
    return [cur] + [pltpu.roll(ext, s, 0)[8:] for s in range(1, CONV_K)]


def _conv(taps, w):
    cv = taps[0] * w[3:4]
    for s in range(1, CONV_K):
        cv = cv + taps[s] * w[3 - s:4 - s]
    return cv


def _l2n(x):
    return x * lax.rsqrt(jnp.sum(x * x, axis=-1, keepdims=True) + EPS)


def _gdn_in_specs(tm, S):
    nb8 = tm // 8
    return [pl.BlockSpec((tm, G_CONV), lambda i: (i, 0)),
            pl.BlockSpec((8, G_CONV), lambda i: (jnp.maximum(i * nb8 - 1, 0), 0)),
            pl.BlockSpec((tm, 128), lambda i: (i, G_BA0 // 128))]


def gdn_pre(proj, conv_w, alog, dtb):
    S = proj.shape[0]
    tm = 256
    nch = tm // CHUNK

    def body(p_ref, halo_ref, ba_ref, w_ref, al_ref, dt_ref,
             q_ref, k_ref, kb_ref, kbg_ref, vb_ref, qd_ref, kd_ref, d_ref, gl_ref):
        first = pl.program_id(0) == 0
        ltri, _, bsame = _chunk_mats(tm)
        beta, _, _, _, gc, glast = _gdn_scalars(ba_ref[...], al_ref[...], dt_ref[...], ltri, bsame)
        eg, ek, egl = jnp.exp(gc), jnp.exp(glast - gc), jnp.exp(glast)
        eye = jnp.where(_iota((16, 16), 0) == _iota((16, 16), 1), 1.0, 0.0).astype(f32)
        gct = _nt(eye, gc, HI)
        low = _iota((CHUNK, CHUNK), 0) >= _iota((CHUNK, CHUNK), 1)

        def act(gi):
            return _silu(_conv(_conv_taps(p_ref, halo_ref, first, gi), w_ref[:, gi * 128:(gi + 1) * 128]))

        for j in range(GQK_H):
            js = slice(j * 128, (j + 1) * 128)
            qn = _l2n(act(j)) * (GHD ** -0.5)
            kn = _l2n(act(GQK_H + j))
            q_ref[:, js] = qn.astype(bf16)
            k_ref[:, js] = kn.astype(bf16)
            for e in range(2):
                h = 2 * j + e
                hs = slice(h * 128, (h + 1) * 128)
                v = act(2 * GQK_H + h)
                bh, egh, ekh = beta[:, h:h + 1], eg[:, h:h + 1], ek[:, h:h + 1]
                kbv = kn * bh
                kb_ref[:, hs] = kbv.astype(bf16)
                kbg_ref[:, hs] = (kbv * egh).astype(bf16)
                vb_ref[:, hs] = (v * bh).astype(bf16)
                qd_ref[:, hs] = (qn * egh).astype(bf16)
                kd_ref[:, hs] = (kn * ekh).astype(bf16)
                for c in range(nch):
                    rs = slice(c * CHUNK, (c + 1) * CHUNK)
                    diff = gc[rs, h:h + 1] - gct[h:h + 1, rs]
                    d_ref[rs, h * CHUNK:(h + 1) * CHUNK] = jnp.where(low, jnp.exp(jnp.where(low, diff, 0.0)), 0.0)
                    gl_ref[c * 8:(c + 1) * 8, hs] = jnp.broadcast_to(egl[c * CHUNK:c * CHUNK + 8, h:h + 1], (8, 128))

    full = lambda shape: pl.BlockSpec(shape, lambda i: (0, 0))
    t1 = pl.BlockSpec((tm, 1024), lambda i: (i, 0))
    t2 = pl.BlockSpec((tm, 2048), lambda i: (i, 0))
    sd = jax.ShapeDtypeStruct
    return pl.pallas_call(
        body, name="gdn_pre", grid=(S // tm,),
        in_specs=_gdn_in_specs(tm, S) + [full((CONV_K, G_CONV)), full((1, 16)), full((1, 16))],
        out_specs=[t1, t1, t2, t2, t2, t2, t2, t1, pl.BlockSpec((tm // 8, 2048), lambda i: (i, 0))],
        out_shape=[sd((S, 1024), bf16)] * 2 + [sd((S, 2048), bf16)] * 5 + [sd((S, 1024), f32), sd((S // 8, 2048), f32)],
        compiler_params=_params(("parallel",)),
    )(proj, proj, proj, conv_w, alog, dtb)


def _tri_inv(L):
    eye = jnp.where(_iota(L.shape, 0) == _iota(L.shape, 1), 1.0, 0.0).astype(f32)
    P = -L
    T = eye + P
    for _ in range(5):
        P = _nn(P, P, HI)
        T = T + _nn(T, P, HI)
    return T


GTB = 512


def gdn_fwd(q, k, kb, kbg, vb, qd, kd, dm, gl8):
    S = q.shape[0]
    nb, ncb = S // GTB, GTB // CHUNK

    def body(q_ref, k_ref, kb_ref, kbg_ref, vb_ref, qd_ref, kd_ref, d_ref, gl_ref,
             o_ref, w_ref, at_ref, t_ref, vn_ref, st_ref, state):
        @pl.when(pl.program_id(1) == 0)
        def _():
            state[...] = jnp.zeros_like(state)

        strict = _iota((CHUNK, CHUNK), 0) > _iota((CHUNK, CHUNK), 1)
        for c in range(ncb):
            rs = slice(c * CHUNK, (c + 1) * CHUNK)
            kc, qc = k_ref[rs, :], q_ref[rs, :]
            qk = _nt(qc, kc)
            for e in range(2):
                cs = slice(e * 128, (e + 1) * 128)
                ds_ = slice(e * CHUNK, (e + 1) * CHUNK)
                dmat = d_ref[rs, ds_]
                L = jnp.where(strict, _nt(kb_ref[rs, cs], kc) * dmat, 0.0)
                T = _tri_inv(L)
                tb = T.astype(bf16)
                u = _nn(tb, vb_ref[rs, cs])
                wb = _nn(tb, kbg_ref[rs, cs]).astype(bf16)
                atb = (qk * dmat).astype(bf16)
                sb = state[e].astype(bf16)
                vnb = (u - _nn(wb, sb)).astype(bf16)
                o_ref[rs, cs] = _nn(qd_ref[rs, cs], sb) + _nn(atb, vnb)
                st_ref[c * 128:(c + 1) * 128, cs] = sb
                state[e] = state[e] * gl_ref[c * 8:c * 8 + 1, cs] + _tn(kd_ref[rs, cs], vnb)
                w_ref[rs, cs] = wb
                at_ref[rs, ds_] = atb
                t_ref[rs, ds_] = T
                vn_ref[rs, cs] = vnb

    b1 = pl.BlockSpec((GTB, 128), lambda j, i: (i, j))
    b2 = pl.BlockSpec((GTB, 256), lambda j, i: (i, j))
    sd = jax.ShapeDtypeStruct
    return pl.pallas_call(
        body, name="gdn_fwd", grid=(GQK_H, nb),
        in_specs=[b1, b1, b2, b2, b2, b2, b2, b1, pl.BlockSpec((GTB // 8, 256), lambda j, i: (i, j))],
        out_specs=[b2, b2, b1, b1, b2, pl.BlockSpec((ncb * 128, 256), lambda j, i: (i, j))],
        out_shape=[sd((S, 2048), f32), sd((S, 2048), bf16), sd((S, 1024), bf16), sd((S, 1024), f32),
                   sd((S, 2048), bf16), sd((S // CHUNK * 128, 2048), bf16)],
        scratch_shapes=[pltpu.VMEM((2, 128, 128), f32)],
        compiler_params=_params(("parallel", "arbitrary")),
    )(q, k, kb, kbg, vb, qd, kd, dm, gl8)


def gdn_bwd(do, q, k, kb, kbg, vb, qd, kd, dm, gl8, w, at, T, vn, st):
    S = q.shape[0]
    nb, ncb = S // GTB, GTB // CHUNK

    def body(do_ref, q_ref, k_ref, kb_ref, kbg_ref, vb_ref, qd_ref, kd_ref, d_ref, gl_ref, w_ref, at_ref, t_ref, vn_ref, st_ref,
             dq_ref, dk_ref, dkb_ref, dkbg_ref, dvb_ref, dqd_ref, dkd_ref, dgc_ref, dstate):
        @pl.when(pl.program_id(1) == 0)
        def _():
            dstate[...] = jnp.zeros_like(dstate)

        ri, ci = _iota((CHUNK, CHUNK), 0), _iota((CHUNK, CHUNK), 1)
        strict = ri > ci
        lastrow = _iota((CHUNK, 128), 0) == CHUNK - 1
        ones = jnp.ones((CHUNK, 128), f32)
        for c in reversed(range(ncb)):
            rs = slice(c * CHUNK, (c + 1) * CHUNK)
            kc, qc = k_ref[rs, :], q_ref[rs, :]
            qk = _nt(qc, kc)
            dq_acc = jnp.zeros((CHUNK, 128), f32)
            dk_acc = jnp.zeros((CHUNK, 128), f32)
            for e in range(2):
                cs = slice(e * 128, (e + 1) * 128)
                ds_ = slice(e * CHUNK, (e + 1) * CHUNK)
                dob = do_ref[rs, cs].astype(bf16)
                sb = st_ref[c * 128:(c + 1) * 128, cs]
                vnb, atb, wb = vn_ref[rs, cs], at_ref[rs, ds_], w_ref[rs, cs]
                qdb, kdb, kbb, kbgb, vbb = qd_ref[rs, cs], kd_ref[rs, cs], kb_ref[rs, cs], kbg_ref[rs, cs], vb_ref[rs, cs]
                gl = gl_ref[c * 8:c * 8 + 1, cs]
                dS = dstate[e]
                dsb = dS.astype(bf16)
                dvnb = (_tn(atb, dob) + _nn(kdb, dsb)).astype(bf16)
                dat = _nt(dob, vnb)
                dqd_ref[rs, cs] = _nt(dob, sb)
                dkd_ref[rs, cs] = _nt(vnb, dsb)
                dwb = (-_nt(dvnb, sb)).astype(bf16)
                dgl = jnp.sum(jnp.sum(dS * sb.astype(f32), axis=1, keepdims=True), axis=0, keepdims=True)
                dstate[e] = gl * dS + _tn(qdb, dob) - _tn(wb, dvnb)
                Tm = t_ref[rs, ds_]
                tb = Tm.astype(bf16)
                dT = _nt(dvnb, vbb) + _nt(dwb, kbgb)
                dvb_ref[rs, cs] = _tn(tb, dvnb)
                dkbg_ref[rs, cs] = _tn(tb, dwb)
                dL = -_tn(Tm, _nt(dT, Tm, HI), HI)
                dmat = d_ref[rs, ds_]
                A = _nt(kbb, kc)
                dA = jnp.where(strict, dL * dmat, 0.0)
                dB = dat * dmat
                dAb, dBb = dA.astype(bf16), dB.astype(bf16)
                dkb_ref[rs, cs] = _nn(dAb, kc)
                dk_acc = dk_acc + _tn(dAb, kbb) + _tn(dBb, qc)
                dq_acc = dq_acc + _nn(dBb, kc)
                M = dA * A + dB * qk
                dgc = jnp.sum(M, axis=1, keepdims=True) - _tn(M, ones, HI)
                dgc_ref[rs, cs] = dgc + jnp.where(lastrow, dgl * gl, 0.0)
            dq_ref[rs, :] = dq_acc
            dk_ref[rs, :] = dk_acc

    b1 = pl.BlockSpec((GTB, 128), lambda j, i: (nb - 1 - i, j))
    b2 = pl.BlockSpec((GTB, 256), lambda j, i: (nb - 1 - i, j))
    sd = jax.ShapeDtypeStruct
    return pl.pallas_call(
        body, name="gdn_bwd", grid=(GQK_H, nb),
        in_specs=[b2, b1, b1, b2, b2, b2, b2, b2, b1, pl.BlockSpec((GTB // 8, 256), lambda j, i: (nb - 1 - i, j)),
                  b2, b1, b1, b2, pl.BlockSpec((ncb * 128, 256), lambda j, i: (nb - 1 - i, j))],
        out_specs=[b1, b1, b2, b2, b2, b2, b2, b2],
        out_shape=[sd((S, 1024), f32)] * 2 + [sd((S, 2048), f32)] * 6,
        scratch_shapes=[pltpu.VMEM((2, 128, 128), f32)],
        compiler_params=_params(("parallel", "arbitrary")),
    )(do, q, k, kb, kbg, vb, qd, kd, dm, gl8, w, at, T, vn, st)


def gdn_onorm(o, proj, nw):
    S = o.shape[0]
    tm = 256

    def body(o_ref, z_ref, nw_ref, o2_ref):
        for h in range(GV_H):
            hs = slice(h * 128, (h + 1) * 128)
            oh = o_ref[:, hs]
            r = lax.rsqrt(jnp.mean(oh * oh, axis=-1, keepdims=True) + EPS)
            o2_ref[:, hs] = (((oh * r) * nw_ref[...]) * _silu(z_ref[:, hs])).astype(bf16)

    t2 = pl.BlockSpec((tm, 2048), lambda i: (i, 0))
    return pl.pallas_call(
        body, name="gdn_onorm", grid=(S // tm,),
        in_specs=[t2, pl.BlockSpec((tm, 2048), lambda i: (i, G_Z0 // 2048)), pl.BlockSpec((1, 128), lambda i: (0, 0))],
        out_specs=t2, out_shape=jax.ShapeDtypeStruct((S, 2048), bf16),
        compiler_params=_params(("parallel",)),
    )(o, proj, nw)


def gdn_onorm_bwd(do2, o, proj, nw):
    S = o.shape[0]
    tm = 256

    def body(d_ref, o_ref, z_ref, nw_ref, do_ref, dz_ref, st_ref):
        i = pl.program_id(0)
        acc = jnp.zeros((1, 128), f32)
        for h in range(GV_H):
            hs = slice(h * 128, (h + 1) * 128)
            oh, z, d2 = o_ref[:, hs], z_ref[:, hs], d_ref[:, hs]
            r = lax.rsqrt(jnp.mean(oh * oh, axis=-1, keepdims=True) + EPS)
            on = oh * r
            dt = d2 * _silu(z)
            dz_ref[:, hs] = (d2 * (on * nw_ref[...]) * _dsilu(z)).astype(bf16)
            don = dt * nw_ref[...]
            acc = acc + jnp.sum(dt * on, axis=0, keepdims=True)
            do_ref[:, hs] = r * (don - on * jnp.mean(don * on, axis=-1, keepdims=True))
        upd = jnp.concatenate([acc, jnp.zeros((7, 128), f32)], axis=0)

        @pl.when(i == 0)
        def _():
            st_ref[...] = upd

        @pl.when(i > 0)
        def _():
            st_ref[...] += upd

    t2 = pl.BlockSpec((tm, 2048), lambda i: (i, 0))
    sd = jax.ShapeDtypeStruct
    return pl.pallas_call(
        body, name="gdn_onorm_bwd", grid=(S // tm,),
        in_specs=[t2, t2, pl.BlockSpec((tm, 2048), lambda i: (i, G_Z0 // 2048)), pl.BlockSpec((1, 128), lambda i: (0, 0))],
        out_specs=[t2, t2, pl.BlockSpec((8, 128), lambda i: (0, 0))],
        out_shape=[sd((S, 2048), f32), sd((S, 2048), bf16), sd((8, 128), f32)],
        compiler_params=_params(("arbitrary",)),
    )(do2, o, proj, nw)


def gdn_pre_bwd(proj, conv_w, alog, dtb, dq, dk, dkb, dkbg, dvb, dqd, dkd, dgcd):
    S = proj.shape[0]
    tm = 128

    def body(p_ref, halo_ref, ba_ref, w_ref, al_ref, dt_ref, dq_ref, dk_ref, dkb_ref, dkbg_ref, dvb_ref, dqd_ref, dkd_ref, dgc_ref,
             dcv_ref, dba_ref, st_ref):
        i = pl.program_id(0)
        first = i == 0
        ltri, utri, bsame = _chunk_mats(tm)
        beta, u, neg_a, g, gc, glast = _gdn_scalars(ba_ref[...], al_ref[...], dt_ref[...], ltri, bsame)
        eg, ek = jnp.exp(gc), jnp.exp(glast - gc)
        lane16 = _iota((tm, 16), 1)
        dgc_all = jnp.zeros((tm, 16), f32)
        rkd_all = jnp.zeros((tm, 16), f32)
        dbeta_all = jnp.zeros((tm, 16), f32)

        def pre(gi):
            return _conv(_conv_taps(p_ref, halo_ref, first, gi), w_ref[:, gi * 128:(gi + 1) * 128])

        def l2n_bwd(xt, dy):
            r = lax.rsqrt(jnp.sum(xt * xt, axis=-1, keepdims=True) + EPS)
            y = xt * r
            return r * (dy - y * jnp.sum(dy * y, axis=-1, keepdims=True))

        for j in range(GQK_H):
            js = slice(j * 128, (j + 1) * 128)
            cvq, cvk = pre(j), pre(GQK_H + j)
            qt, kt = _silu(cvq), _silu(cvk)
            qn = _l2n(qt) * (GHD ** -0.5)
            kn = _l2n(kt)
            dq_tot, dk_tot = dq_ref[:, js], dk_ref[:, js]
            for e in range(2):
                h = 2 * j + e
                hs = slice(h * 128, (h + 1) * 128)
                gv = 2 * GQK_H + h
                cvv = pre(gv)
                v = _silu(cvv)
                bh, egh, ekh = beta[:, h:h + 1], eg[:, h:h + 1], ek[:, h:h + 1]
                dkbg, dkd, dqd, dvb = dkbg_ref[:, hs], dkd_ref[:, hs], dqd_ref[:, hs], dvb_ref[:, hs]
                dkb_t = dkb_ref[:, hs] + dkbg * egh
                dk_tot = dk_tot + dkb_t * bh + dkd * ekh
                dq_tot = dq_tot + dqd * egh
                dcv_ref[:, gv * 128:(gv + 1) * 128] = (dvb * bh) * _dsilu(cvv)
                dbeta = jnp.sum(dkb_t * kn, axis=-1, keepdims=True) + jnp.sum(dvb * v, axis=-1, keepdims=True)
                rkd = jnp.sum(dkd * (kn * ekh), axis=-1, keepdims=True)
                dgc = (dgc_ref[:, hs][:, 0:1] + jnp.sum(dkbg * (kn * bh * egh), axis=-1, keepdims=True)
                       + jnp.sum(dqd * (qn * egh), axis=-1, keepdims=True) - rkd)
                sel = lane16 == h
                dgc_all = dgc_all + jnp.where(sel, dgc, 0.0)
                rkd_all = rkd_all + jnp.where(sel, rkd, 0.0)
                dbeta_all = dbeta_all + jnp.where(sel, dbeta, 0.0)
            dcv_ref[:, js] = l2n_bwd(qt, dq_tot * (GHD ** -0.5)) * _dsilu(cvq)
            ks = slice((GQK_H + j) * 128, (GQK_H + j + 1) * 128)
            dcv_ref[:, ks] = l2n_bwd(kt, dk_tot) * _dsilu(cvk)

        islast = jnp.bitwise_and(_iota((tm, 16), 0), CHUNK - 1) == CHUNK - 1
        dgc_all = dgc_all + jnp.where(islast, _nn(bsame, rkd_all, HI), 0.0)
        dg = _nn(utri, dgc_all, HI)
        da = dg * neg_a * _sigmoid(u)
        db = dbeta_all * beta * (1.0 - beta)
        r16, c128 = _iota((16, 128), 0), _iota((16, 128), 1)
        pb = jnp.where(c128 == r16, 1.0, 0.0).astype(f32)
        pa = jnp.where(c128 == r16 + 16, 1.0, 0.0).astype(f32)
        dba_ref[...] = _nn(db, pb, HI) + _nn(da, pa, HI)
        upd = jnp.concatenate([jnp.sum(dg * g, axis=0, keepdims=True), jnp.sum(da, axis=0, keepdims=True),
                               jnp.zeros((6, 16), f32)], axis=0)

        @pl.when(i == 0)
        def _():
            st_ref[...] = upd

        @pl.when(i > 0)
        def _():
            st_ref[...] += upd

    full = lambda shape: pl.BlockSpec(shape, lambda i: (0, 0))
    t1 = pl.BlockSpec((tm, 1024), lambda i: (i, 0))
    t2 = pl.BlockSpec((tm, 2048), lambda i: (i, 0))
    sd = jax.ShapeDtypeStruct
    return pl.pallas_call(
        body, name="gdn_pre_bwd", grid=(S // tm,),
        in_specs=_gdn_in_specs(tm, S) + [full((CONV_K, G_CONV)), full((1, 16)), full((1, 16)), t1, t1] + [t2] * 6,
        out_specs=[pl.BlockSpec((tm, G_CONV), lambda i: (i, 0)), pl.BlockSpec((tm, 128), lambda i: (i, 0)), full((8, 16))],
        out_shape=[sd((S, G_CONV), f32), sd((S, 128), f32), sd((8, 16), f32)],
        compiler_params=_params(("arbitrary",)),
    )(proj, proj, proj, conv_w, alog, dtb, dq, dk, dkb, dkbg, dvb, dqd, dkd, dgcd)


def gdn_conv_bwd(proj, conv_w, dcv, dz, dba):
    S = proj.shape[0]
    tm = 256
    nb, nb8 = S // tm, tm // 8

    def body(p_ref, halo_ref, w_ref, dcv_ref, nxt_ref, dz_ref, dba_ref, dp_ref, dw_ref):
        i = pl.program_id(0)
        first, last = i == 0, i == nb - 1
        for gi in range(G_CONV // 128):
            cs = slice(gi * 128, (gi + 1) * 128)
            taps = _conv_taps(p_ref, halo_ref, first, gi)
            cur = dcv_ref[:, cs]
            ext = jnp.concatenate([cur, jnp.where(last, 0.0, nxt_ref[:, cs])], axis=0)
            w = w_ref[:, cs]
            dp = cur * w[3:4]
            rows = [jnp.sum(cur * taps[3 - kk], axis=0, keepdims=True) for kk in range(CONV_K)]
            for s in range(1, CONV_K):
                dp = dp + pltpu.roll(ext, tm + 8 - s, 0)[:tm] * w[3 - s:4 - s]
            dp_ref[:, cs] = dp.astype(bf16)
            upd = jnp.concatenate(rows + [jnp.zeros((4, 128), f32)], axis=0)

            @pl.when(first)
            def _():
                dw_ref[:, cs] = upd

            @pl.when(i > 0)
            def _():
                dw_ref[:, cs] += upd

        dp_ref[:, G_Z0:G_BA0] = dz_ref[...]
        dp_ref[:, G_BA0:G_INP] = dba_ref[...].astype(bf16)

    sd = jax.ShapeDtypeStruct
    return pl.pallas_call(
        body, name="gdn_conv_bwd", grid=(nb,),
        in_specs=[pl.BlockSpec((tm, G_CONV), lambda i: (i, 0)),
                  pl.BlockSpec((8, G_CONV), lambda i: (jnp.maximum(i * nb8 - 1, 0), 0)),
                  pl.BlockSpec((CONV_K, G_CONV), lambda i: (0, 0)),
                  pl.BlockSpec((tm, G_CONV), lambda i: (i, 0)),
                  pl.BlockSpec((8, G_CONV), lambda i: (jnp.minimum((i + 1) * nb8, S // 8 - 1), 0)),
                  pl.BlockSpec((tm, 2048), lambda i: (i, 0)), pl.BlockSpec((tm, 128), lambda i: (i, 0))],
        out_specs=[pl.BlockSpec((tm, G_INP), lambda i: (i, 0)), pl.BlockSpec((8, G_CONV), lambda i: (0, 0))],
        out_shape=[sd((S, G_INP), bf16), sd((8, G_CONV), f32)],
        compiler_params=_params(("arbitrary",)),
    )(proj, proj, conv_w, dcv, dcv, dz, dba)


def _half_mean(t, lo_half):
    m0 = jnp.sum(jnp.where(lo_half, t, 0.0), axis=-1, keepdims=True)
    m1 = jnp.sum(jnp.where(lo_half, 0.0, t), axis=-1, keepdims=True)
    return jnp.where(lo_half, m0, m1) * (1.0 / F_HD)


def _split3(c):
    hi = c.astype(bf16).astype(f32)
    mid = (c - hi).astype(bf16).astype(f32)
    lo = (c - hi - mid).astype(bf16).astype(f32)
    return hi, mid, lo


def fox_pre(proj, fbias, qw2, kw2):
    S = proj.shape[0]
    tm = 256

    def body(q_ref, k_ref, v_ref, f_ref, fb_ref, qw_ref, kw_ref, qa_ref, ka_ref, vb_ref, carry):
        @pl.when(pl.program_id(0) == 0)
        def _():
            carry[...] = jnp.zeros_like(carry)

        logf = -_softplus(-(f_ref[:, 0:16] + fb_ref[...]))
        ltri = jnp.where(_iota((tm, tm), 1) <= _iota((tm, tm), 0), 1.0, 0.0).astype(f32)
        cum = _nn(ltri, logf, HI) + carry[0:1, :]
        carry[0:1, :] = cum[tm - 1:tm, :]
        lane = _iota((tm, 128), 1)
        lo_half = lane < F_HD
        for p in range(F_H // 2):
            ps = slice(p * 128, (p + 1) * 128)
            for src, w_ref, dst, is_q in ((q_ref, qw_ref, qa_ref, True), (k_ref, kw_ref, ka_ref, False)):
                x = src[:, ps]
                xn = x * lax.rsqrt(_half_mean(x * x, lo_half) + EPS) * w_ref[...]
                if is_q:
                    xn = xn * (F_HD ** -0.5)
                for e in range(2):
                    h = 2 * p + e
                    base = xn if e == 0 else pltpu.roll(xn, F_HD, 1)
                    hi, mid, lo = _split3(cum[:, h:h + 1])
                    pieces = jnp.where(lane == 64, hi, 0.0) + jnp.where(lane == 65, mid, 0.0) + jnp.where(lane == 66, lo, 0.0)
                    if is_q:
                        ext = pieces + jnp.where((lane >= 67) & (lane <= 69), 1.0, 0.0)
                    else:
                        ext = jnp.where((lane >= 64) & (lane <= 66), 1.0, 0.0) - pltpu.roll(pieces, 3, 1)
                    dst[:, h * 128:(h + 1) * 128] = jnp.where(lo_half, base, ext).astype(bf16)
        vb_ref[...] = v_ref[...].astype(bf16)

    t1 = lambda c: pl.BlockSpec((tm, 1024), lambda i: (i, c))
    vec = lambda n: pl.BlockSpec((1, n), lambda i: (0, 0))
    sd = jax.ShapeDtypeStruct
    return pl.pallas_call(
        body, name="fox_pre", grid=(S // tm,),
        in_specs=[t1(0), t1(1), t1(2), pl.BlockSpec((tm, 128), lambda i: (i, F_F0 // 128)), vec(16), vec(128), vec(128)],
        out_specs=[pl.BlockSpec((tm, 2048), lambda i: (i, 0))] * 2 + [pl.BlockSpec((tm, 1024), lambda i: (i, 0))],
        out_shape=[sd((S, 2048), bf16), sd((S, 2048), bf16), sd((S, 1024), bf16)],
        scratch_shapes=[pltpu.VMEM((8, 16), f32)],
        compiler_params=_params(("arbitrary",)),
    )(proj, proj, proj, proj, fbias, qw2, kw2)


FTQ = 512


def fox_attn(qa, ka, v):
    S = qa.shape[0]
    nq = S // FTQ

    def body(q_ref, k_ref, v_ref, o_ref, lse_ref, m_scr, l_scr, acc_scr):
        i, j = pl.program_id(1), pl.program_id(2)

        @pl.when(j == 0)
        def _():
            m_scr[...] = jnp.full_like(m_scr, NEG)
            l_scr[...] = jnp.zeros_like(l_scr)
            acc_scr[...] = jnp.zeros_like(acc_scr)

        @pl.when(j <= i)
        def _():
            keep = (j < i) | (_iota((FTQ, FTQ), 0) >= _iota((FTQ, FTQ), 1))
            for e in range(2):
                es, vs = slice(e * 128, (e + 1) * 128), slice(e * F_HD, (e + 1) * F_HD)
                s = jnp.where(keep, _nt(q_ref[:, es], k_ref[:, es]), NEG)
                m_old = m_scr[e]
                m_new = jnp.maximum(m_old, jnp.max(s, axis=-1, keepdims=True))
                alpha = jnp.exp(m_old - m_new)
                p = jnp.exp(s - m_new[:, 0:1])
                l_scr[e] = alpha * l_scr[e] + jnp.sum(p, axis=-1, keepdims=True)
                acc_scr[e] = acc_scr[e] * alpha[:, 0:F_HD] + _nn(p.astype(bf16), v_ref[:, vs])
                m_scr[e] = m_new

        @pl.when(j == nq - 1)
        def _():
            for e in range(2):
                vs = slice(e * F_HD, (e + 1) * F_HD)
                l = l_scr[e][:, 0:F_HD]
                o_ref[:, vs] = acc_scr[e] / l
                lse_ref[:, vs] = m_scr[e][:, 0:F_HD] + jnp.log(l)

    sd = jax.ShapeDtypeStruct
    qo = pl.BlockSpec((FTQ, 128), lambda p, i, j: (i, p))
    return pl.pallas_call(
        body, name="fox_attn", grid=(F_H // 2, nq, nq),
        in_specs=[pl.BlockSpec((FTQ, 256), lambda p, i, j: (i, p)),
                  pl.BlockSpec((FTQ, 256), lambda p, i, j: (jnp.minimum(j, i), p)),
                  pl.BlockSpec((FTQ, 128), lambda p, i, j: (jnp.minimum(j, i), p))],
        out_specs=[qo, qo],
        out_shape=[sd((S, 1024), f32), sd((S, 1024), f32)],
        scratch_shapes=[pltpu.VMEM((2, FTQ, 128), f32), pltpu.VMEM((2, FTQ, 128), f32), pltpu.VMEM((2, FTQ, F_HD), f32)],
        compiler_params=_params(("parallel", "parallel", "arbitrary")),
    )(qa, ka, v)


def fox_attn_bwd(qa, ka, v, do, lse, delta):
    S = qa.shape[0]
    nq = S // FTQ

    def body(q_ref, k_ref, v_ref, do_ref, lse_ref, dl_ref, dq_ref, dk_ref, dv_ref, dk_scr, dv_scr):
        j, i = pl.program_id(1), pl.program_id(2)

        @pl.when((j == 0) & (i == 0))
        def _():
            dq_ref[...] = jnp.zeros_like(dq_ref)

        @pl.when(i == 0)
        def _():
            dk_scr[...] = jnp.zeros_like(dk_scr)
            dv_scr[...] = jnp.zeros_like(dv_scr)

        @pl.when(i >= j)
        def _():
            keep = (i > j) | (_iota((FTQ, FTQ), 0) >= _iota((FTQ, FTQ), 1))
            rows = pl.ds(pl.multiple_of(i * FTQ, FTQ), FTQ)
            for e in range(2):
                es, vs = slice(e * 128, (e + 1) * 128), slice(e * F_HD, (e + 1) * F_HD)
                qe, ke = q_ref[:, es], k_ref[:, es]
                dob = do_ref[:, vs].astype(bf16)
                s = jnp.where(keep, _nt(qe, ke), NEG)
                p = jnp.exp(s - lse_ref[:, e * F_HD:e * F_HD + 1])
                ds = p * (_nt(dob, v_ref[:, vs]) - dl_ref[:, e * F_HD:e * F_HD + 1])
                dsb = ds.astype(bf16)
                dv_scr[e] += _tn(p.astype(bf16), dob)
                dk_scr[e] += _tn(dsb, qe)
                dq_ref[rows, es] += _nn(dsb, ke)

        @pl.when(i == nq - 1)
        def _():
            for e in range(2):
                dk_ref[:, e * 128:(e + 1) * 128] = dk_scr[e]
                dv_ref[:, e * F_HD:(e + 1) * F_HD] = dv_scr[e]

    sd = jax.ShapeDtypeStruct
    qi = lambda w: pl.BlockSpec((FTQ, w), lambda p, j, i: (jnp.maximum(i, j), p))
    kj = lambda w: pl.BlockSpec((FTQ, w), lambda p, j, i: (j, p))
    return pl.pallas_call(
        body, name="fox_attn_bwd", grid=(F_H // 2, nq, nq),
        in_specs=[qi(256), kj(256), kj(128), qi(128), qi(128), qi(128)],
        out_specs=[pl.BlockSpec((S, 256), lambda p, j, i: (0, p)), kj(256), kj(128)],
        out_shape=[sd((S, 2048), f32), sd((S, 2048), f32), sd((S, 1024), f32)],
        scratch_shapes=[pltpu.VMEM((2, FTQ, 128), f32), pltpu.VMEM((2, FTQ, F_HD), f32)],
        compiler_params=_params(("parallel", "arbitrary", "arbitrary")),
    )(qa, ka, v, do, lse, delta)


def fox_gate(o, proj):
    S = o.shape[0]
    tm = 512

    def body(o_ref, z_ref, o2_ref):
        o2_ref[...] = (o_ref[...] * _silu(z_ref[...])).astype(bf16)

    t = pl.BlockSpec((tm, 1024), lambda i: (i, 0))
    return pl.pallas_call(
        body, name="fox_gate", grid=(S // tm,),
        in_specs=[t, pl.BlockSpec((tm, 1024), lambda i: (i, 3))], out_specs=t,
        out_shape=jax.ShapeDtypeStruct((S, 1024), bf16),
        compiler_params=_params(("parallel",)),
    )(o, proj)


def fox_gate_bwd(do2, o, proj):
    S = o.shape[0]
    tm = 256

    def body(d_ref, o_ref, z_ref, do_ref, dz_ref, dl_ref):
        lo_half = _iota((tm, 128), 1) < F_HD
        for p in range(F_H // 2):
            ps = slice(p * 128, (p + 1) * 128)
            d2, ov, z = d_ref[:, ps], o_ref[:, ps], z_ref[:, ps]
            dov = d2 * _silu(z)
            do_ref[:, ps] = dov
            dz_ref[:, ps] = (d2 * ov * _dsilu(z)).astype(bf16)
            dl_ref[:, ps] = _half_mean(dov * ov, lo_half) * float(F_HD)

    t = pl.BlockSpec((tm, 1024), lambda i: (i, 0))
    sd = jax.ShapeDtypeStruct
    return pl.pallas_call(
        body, name="fox_gate_bwd", grid=(S // tm,),
        in_specs=[t, t, pl.BlockSpec((tm, 1024), lambda i: (i, 3))], out_specs=[t, t, t],
        out_shape=[sd((S, 1024), f32), sd((S, 1024), bf16), sd((S, 1024), f32)],
        compiler_params=_params(("parallel",)),
    )(do2, o, proj)


def fox_pre_bwd(proj, fbias, qw2, kw2, dqa, dka, dv, dz):
    S = proj.shape[0]
    tm = 256
    nb = S // tm

    def body(q_ref, k_ref, f_ref, fb_ref, qw_ref, kw_ref, dqa_ref, dka_ref, dv_ref, dz_ref, dp_ref, st_ref, carry):
        i = pl.program_id(0)

        @pl.when(i == 0)
        def _():
            carry[...] = jnp.zeros_like(carry)

        lane = _iota((tm, 128), 1)
        lo_half = lane < F_HD
        lane16 = _iota((tm, 16), 1)
        dcum = jnp.zeros((tm, 16), f32)
        dws = []
        for src, w_ref, dsrc, is_q, col0 in ((q_ref, qw_ref, dqa_ref, True, 0), (k_ref, kw_ref, dka_ref, False, 1024)):
            dw = jnp.zeros((1, 128), f32)
            for p in range(F_H // 2):
                ps = slice(p * 128, (p + 1) * 128)
                x = src[:, ps]
                r = lax.rsqrt(_half_mean(x * x, lo_half) + EPS)
                xh = x * r
                d0 = dsrc[:, (2 * p) * 128:(2 * p + 1) * 128]
                d1 = dsrc[:, (2 * p + 1) * 128:(2 * p + 2) * 128]
                dy = jnp.where(lo_half, d0, pltpu.roll(d1, F_HD, 1))
                if is_q:
                    dy = dy * (F_HD ** -0.5)
                dxh = dy * w_ref[...]
                dw = dw + jnp.sum(dy * xh, axis=0, keepdims=True)
                dp_ref[:, col0 + p * 128:col0 + (p + 1) * 128] = (r * (dxh - xh * _half_mean(dxh * xh, lo_half))).astype(bf16)
                for e, de in ((0, d0), (1, d1)):
                    col = de[:, 64:65] if is_q else -de[:, 67:68]
                    dcum = dcum + jnp.where(lane16 == 2 * p + e, col, 0.0)
            dws.append(dw)
        dp_ref[:, 2048:3072] = dv_ref[...].astype(bf16)
        dp_ref[:, 3072:4096] = dz_ref[...]
        utri = jnp.where(_iota((tm, tm), 1) >= _iota((tm, tm), 0), 1.0, 0.0).astype(f32)
        dlogf = _nn(utri, dcum, HI) + carry[0:1, :]
        carry[0:1, :] = dlogf[0:1, :]
        fl = f_ref[:, 0:16] + fb_ref[...]
        df = dlogf * _sigmoid(-fl)
        place = jnp.where(_iota((16, 128), 1) == _iota((16, 128), 0), 1.0, 0.0).astype(f32)
        dfw = _nn(df, place, HI)
        dp_ref[:, F_F0:F_INP] = dfw.astype(bf16)
        upd = jnp.concatenate(dws + [jnp.sum(dfw, axis=0, keepdims=True), jnp.zeros((5, 128), f32)], axis=0)

        @pl.when(i == 0)
        def _():
            st_ref[...] = upd

        @pl.when(i > 0)
        def _():
            st_ref[...] += upd

    rev = lambda w, c: pl.BlockSpec((tm, w), lambda i: (nb - 1 - i, c))
    vec = lambda n: pl.BlockSpec((1, n), lambda i: (0, 0))
    sd = jax.ShapeDtypeStruct
    return pl.pallas_call(
        body, name="fox_pre_bwd", grid=(nb,),
        in_specs=[rev(1024, 0), rev(1024, 1), rev(128, F_F0 // 128), vec(16), vec(128), vec(128),
                  rev(2048, 0), rev(2048, 0), rev(1024, 0), rev(1024, 0)],
        out_specs=[rev(F_INP, 0), pl.BlockSpec((8, 128), lambda i: (0, 0))],
        out_shape=[sd((S, F_INP), bf16), sd((8, 128), f32)],
        scratch_shapes=[pltpu.VMEM((8, 16), f32)],
        compiler_params=_params(("arbitrary",)),
    )(proj, proj, proj, fbias, qw2, kw2, dqa, dka, dv, dz)


def _me():
    return lax.axis_index("x"), lax.axis_index("y"), lax.axis_index("c")


def _other_chips(x, y):
    return [(1 - x, y), (x, 1 - y), (1 - x, 1 - y)]


def ag_small(xs):
    m_per, n = xs.shape

    def body(x_ref, out_ref, send_sems, recv_sems, local_sem):
        x, y, c = _me()
        me, sibling = (x, y, c), (x, y, 1 - c)
        chips = _other_chips(x, y)

        def rows(px, py, pc):
            return out_ref.at[pl.ds((4 * px + 2 * py + pc) * m_per, m_per), :]

        def copy(k, block, to, src=None):
            return pltpu.make_async_remote_copy(
                src_ref=rows(*block) if src is None else src, dst_ref=rows(*block),
                send_sem=send_sems.at[k], recv_sem=recv_sems.at[k], device_id=to, device_id_type=MESH)

        mine = pltpu.make_async_copy(x_ref, rows(*me), local_sem)
        mine.start()
        first = [copy(0, me, sibling, src=x_ref)]
        first += [copy(1 + j, me, (*chip, c), src=x_ref) for j, chip in enumerate(chips)]
        for cp in first:
            cp.start()
        passed = [copy(4 + j, (*chip, c), sibling) for j, chip in enumerate(chips)]
        for j, chip in enumerate(chips):
            copy(1 + j, (*chip, c), me).wait_recv()
            passed[j].start()
        copy(0, sibling, me).wait_recv()
        for j, chip in enumerate(chips):
            copy(4 + j, (*chip, 1 - c), me).wait_recv()
        for cp in first + passed:
            cp.wait_send()
        mine.wait()

    return pl.pallas_call(
        body, name="ag_small",
        out_shape=jax.ShapeDtypeStruct((8 * m_per, n), xs.dtype),
        in_specs=[pl.BlockSpec(memory_space=pltpu.VMEM)], out_specs=pl.BlockSpec(memory_space=pltpu.VMEM),
        scratch_shapes=[pltpu.SemaphoreType.DMA((7,)), pltpu.SemaphoreType.DMA((7,)), pltpu.SemaphoreType.DMA],
        compiler_params=pltpu.CompilerParams(vmem_limit_bytes=VMEM_LIMIT),
    )(xs)


_ANY = pl.BlockSpec(memory_space=pl.ANY)


def ag_chips(arrs):
    n = len(arrs)

    def body(*refs):
        ins, outs = refs[:n], refs[n:2 * n]
        send_sems, recv_sems, local_sems = refs[2 * n:]
        x, y, c = _me()
        me = 2 * x + y
        chips = _other_chips(x, y)
        started = []
        for a in range(n):
            cp = pltpu.make_async_copy(ins[a], outs[a].at[me], local_sems.at[a])
            cp.start()
            started.append(cp)
        sends = []
        for a in range(n):
            for j, (px, py) in enumerate(chips):
                r = pltpu.make_async_remote_copy(
                    src_ref=ins[a], dst_ref=outs[a].at[me], send_sem=send_sems.at[3 * a + j], recv_sem=recv_sems.at[3 * a + j],
                    device_id=(px, py, c), device_id_type=MESH)
                r.start()
                sends.append(r)
        for a in range(n):
            for j, (px, py) in enumerate(chips):
                pltpu.make_async_remote_copy(
                    src_ref=ins[a], dst_ref=outs[a].at[2 * px + py], send_sem=send_sems.at[3 * a + j],
                    recv_sem=recv_sems.at[3 * a + j], device_id=(px, py, c), device_id_type=MESH).wait_recv()
        for r in sends:
            r.wait_send()
        for cp in started:
            cp.wait()

    return pl.pallas_call(
        body, name="ag_chips",
        out_shape=[jax.ShapeDtypeStruct((4,) + a.shape, a.dtype) for a in arrs],
        in_specs=[_ANY] * n, out_specs=[_ANY] * n,
        scratch_shapes=[pltpu.SemaphoreType.DMA((3 * n,)), pltpu.SemaphoreType.DMA((3 * n,)), pltpu.SemaphoreType.DMA((n,))],
    )(*arrs)


def rs_swap_halves(gs):
    n = len(gs)

    def body(*refs):
        ins, outs = refs[:n], refs[n:2 * n]
        send_sems, recv_sems = refs[2 * n:]
        x, y, c = _me()
        cps = []
        for a in range(n):
            rh = ins[a].shape[1] // 2
            cp = pltpu.make_async_remote_copy(
                src_ref=ins[a].at[:, pl.ds((1 - c) * rh, rh), :], dst_ref=outs[a],
                send_sem=send_sems.at[a], recv_sem=recv_sems.at[a], device_id=(x, y, 1 - c), device_id_type=MESH)
            cp.start()
            cps.append(cp)
        for cp in cps:
            cp.wait()

    return pl.pallas_call(
        body, name="rs_swap_halves",
        out_shape=[jax.ShapeDtypeStruct((4, g.shape[1] // 2, g.shape[2]), g.dtype) for g in gs],
        in_specs=[_ANY] * n, out_specs=[_ANY] * n,
        scratch_shapes=[pltpu.SemaphoreType.DMA((n,)), pltpu.SemaphoreType.DMA((n,))],
    )(*gs)


def rs_add_halves(g, r, cidx):
    _, rh, C = r.shape
    tr = _pick(rh, (256, 128))
    nrb = rh // tr

    def body(c_ref, g_ref, r_ref, o_ref):
        o_ref[...] = g_ref[...] + r_ref[...]

    return pl.pallas_call(
        body, name="rs_add_halves",
        grid_spec=pltpu.PrefetchScalarGridSpec(
            num_scalar_prefetch=1, grid=(4, nrb),
            in_specs=[pl.BlockSpec((1, tr, C), lambda k, i, c_ref: (k, c_ref[0] * nrb + i, 0)),
                      pl.BlockSpec((1, tr, C), lambda k, i, c_ref: (k, i, 0))],
            out_specs=pl.BlockSpec((1, tr, C), lambda k, i, c_ref: (k, i, 0))),
        out_shape=jax.ShapeDtypeStruct(r.shape, f32),
        compiler_params=_params(("parallel", "parallel")),
    )(cidx, g, r)


def rs_exchange_chips(ps):
    n = len(ps)

    def body(*refs):
        ins, outs = refs[:n], refs[n:2 * n]
        send_sems, recv_sems, local_sems = refs[2 * n:]
        x, y, c = _me()
        me = 2 * x + y
        chips = _other_chips(x, y)
        started = []
        for a in range(n):
            cp = pltpu.make_async_copy(ins[a].at[me], outs[a].at[me], local_sems.at[a])
            cp.start()
            started.append(cp)
        sends = []
        for a in range(n):
            for j, (px, py) in enumerate(chips):
                r = pltpu.make_async_remote_copy(
                    src_ref=ins[a].at[2 * px + py], dst_ref=outs[a].at[me], send_sem=send_sems.at[3 * a + j],
                    recv_sem=recv_sems.at[3 * a + j], device_id=(px, py, c), device_id_type=MESH)
                r.start()
                sends.append(r)
        for a in range(n):
            for j, (px, py) in enumerate(chips):
                pltpu.make_async_remote_copy(
                    src_ref=ins[a].at[me], dst_ref=outs[a].at[2 * px + py], send_sem=send_sems.at[3 * a + j],
                    recv_sem=recv_sems.at[3 * a + j], device_id=(px, py, c), device_id_type=MESH).wait_recv()
        for r in sends:
            r.wait_send()
        for cp in started:
            cp.wait()

    return pl.pallas_call(
        body, name="rs_exchange_chips",
        out_shape=[jax.ShapeDtypeStruct(p.shape, p.dtype) for p in ps],
        in_specs=[_ANY] * n, out_specs=[_ANY] * n,
        scratch_shapes=[pltpu.SemaphoreType.DMA((3 * n,)), pltpu.SemaphoreType.DMA((3 * n,)), pltpu.SemaphoreType.DMA((n,))],
    )(*ps)


def sum_leading(q, name):
    K, R, C = q.shape
    tr = _pick(R, (256, 128, 64, 32, 16, 8))

    def body(q_ref, o_ref):
        acc = q_ref[0]
        for k in range(1, K):
            acc = acc + q_ref[k]
        o_ref[...] = acc

    return pl.pallas_call(
        body, name=name, grid=(R // tr,),
        in_specs=[pl.BlockSpec((K, tr, C), lambda i: (0, i, 0))], out_specs=pl.BlockSpec((tr, C), lambda i: (i, 0)),
        out_shape=jax.ShapeDtypeStruct((R, C), f32),
        compiler_params=_params(("parallel",)),
    )(q)


def rs_share_halves(rs):
    n = len(rs)

    def body(*refs):
        ins, outs = refs[:n], refs[n:2 * n]
        send_sems, recv_sems, local_sems = refs[2 * n:]
        x, y, c = _me()
        cps = []
        for a in range(n):
            lc = pltpu.make_async_copy(ins[a], outs[a].at[c], local_sems.at[a])
            lc.start()
            cp = pltpu.make_async_remote_copy(
                src_ref=ins[a], dst_ref=outs[a].at[c], send_sem=send_sems.at[a], recv_sem=recv_sems.at[a],
                device_id=(x, y, 1 - c), device_id_type=MESH)
            cp.start()
            cps.append((lc, cp))
        for a, (lc, cp) in enumerate(cps):
            pltpu.make_async_remote_copy(
                src_ref=ins[a], dst_ref=outs[a].at[1 - c], send_sem=send_sems.at[a], recv_sem=recv_sems.at[a],
                device_id=(x, y, 1 - c), device_id_type=MESH).wait_recv()
            cp.wait_send()
            lc.wait()

    return pl.pallas_call(
        body, name="rs_share_halves",
        out_shape=[jax.ShapeDtypeStruct((2,) + r.shape, r.dtype) for r in rs],
        in_specs=[_ANY] * n, out_specs=[_ANY] * n,
        scratch_shapes=[pltpu.SemaphoreType.DMA((n,)), pltpu.SemaphoreType.DMA((n,)), pltpu.SemaphoreType.DMA((n,))],
    )(*rs)


def reduce_scatter(gs, cidx):
    got = rs_swap_halves(gs)
    ps = [rs_add_halves(g, r, cidx) for g, r in zip(gs, got)]
    qs = rs_exchange_chips(ps)
    rs = [sum_leading(q, "rs_sum_chips") for q in qs]
    fs = rs_share_halves(rs)
    return [f.reshape(-1, f.shape[-1]) for f in fs]


def ada_mod(c_all, ada_w):
    L, _, n = ada_w.shape

    def body(c_ref, w_ref, o_ref):
        o_ref[0] = _nn(_silu(c_ref[...]), w_ref[0], HI)

    return pl.pallas_call(
        body, name="ada_mod", grid=(L,),
        in_specs=[pl.BlockSpec((8, D), lambda l: (0, 0)), pl.BlockSpec((1, D, n), lambda l: (l, 0, 0))],
        out_specs=pl.BlockSpec((1, 8, n), lambda l: (l, 0, 0)),
        out_shape=jax.ShapeDtypeStruct((L, 8, n), f32),
        compiler_params=_params(("parallel",)),
    )(c_all, ada_w)


def ada_w_grad(c_all, dmod):
    L, _, n = dmod.shape

    def body(c_ref, d_ref, o_ref):
        o_ref[0] = _tn(_silu(c_ref[...]), d_ref[0], HI)

    return pl.pallas_call(
        body, name="ada_w_grad", grid=(L,),
        in_specs=[pl.BlockSpec((8, D), lambda l: (0, 0)), pl.BlockSpec((1, 8, n), lambda l: (l, 0, 0))],
        out_specs=pl.BlockSpec((1, D, n), lambda l: (l, 0, 0)),
        out_shape=jax.ShapeDtypeStruct((L, D, n), f32),
        compiler_params=_params(("parallel",)),
    )(c_all, dmod)


def adamw(w, g, m, v, name):
    shp = w.shape
    two = lambda a: a.reshape(-1, shp[-1])
    R, C = two(w).shape
    tr = _pick(R, (256, 128, 64, 32, 16, 8))
    bc1, bc2 = 1.0 - B1 ** STEP, 1.0 - B2 ** STEP

    def body(w_ref, g_ref, m_ref, v_ref, d_ref, mo_ref, vo_ref):
        gv = g_ref[...]
        mn = B1 * m_ref[...] + (1.0 - B1) * gv
        vn = B2 * v_ref[...] + (1.0 - B2) * (gv * gv)
        d_ref[...] = -LR * ((mn / bc1) / (jnp.sqrt(vn / bc2) + AEPS) + WD * w_ref[...])
        mo_ref[...] = mn
        vo_ref[...] = vn

    t = pl.BlockSpec((tr, C), lambda i: (i, 0))
    outs = pl.pallas_call(
        body, name=name, grid=(R // tr,),
        in_specs=[t] * 4, out_specs=[t] * 3, out_shape=[jax.ShapeDtypeStruct((R, C), f32)] * 3,
        compiler_params=_params(("parallel",)),
    )(two(w), two(g), two(m), two(v))
    return [o.reshape(shp) for o in outs]


def _pack(arrs):
    parts, offs, r0 = [], [], 0
    for a in arrs:
        n = a.size
        rows = -(-n // 1024) * 8
        parts.append(jnp.pad(a.reshape(-1), (0, rows * 128 - n)).reshape(rows, 128))
        offs.append((r0, rows))
        r0 += rows
    return jnp.concatenate(parts, axis=0), offs


def _unpack(buf, offs, shapes):
    out = []
    for (r0, rows), shp in zip(offs, shapes):
        n = 1
        for d in shp:
            n *= d
        out.append(buf[..., r0:r0 + rows, :].reshape(buf.shape[:-2] + (rows * 128,))[..., :n].reshape(buf.shape[:-2] + tuple(shp)))
    return out


def kernel(x, c, norm_w, ada_w, ada_b, a_w_in, a_conv_w, a_A_log, a_dt_bias, a_norm_w, a_w_out, b_w_in, b_f_bias, b_qn_w, b_kn_w, b_w_out, final_norm_w, loss_target, m_norm_w, m_ada_w, m_ada_b, m_a_w_in, m_a_conv_w, m_a_A_log, m_a_dt_bias, m_a_norm_w, m_a_w_out, m_b_w_in, m_b_f_bias, m_b_qn_w, m_b_kn_w, m_b_w_out, m_final_norm_w, v_norm_w, v_ada_w, v_ada_b, v_a_w_in, v_a_conv_w, v_a_A_log, v_a_dt_bias, v_a_norm_w, v_a_w_out, v_b_w_in, v_b_f_bias, v_b_qn_w, v_b_kn_w, v_b_w_out, v_final_norm_w):
    weights = dict(norm_w=norm_w, ada_w=ada_w, ada_b=ada_b, a_w_in=a_w_in, a_conv_w=a_conv_w, a_A_log=a_A_log,
                   a_dt_bias=a_dt_bias, a_norm_w=a_norm_w, a_w_out=a_w_out, b_w_in=b_w_in, b_f_bias=b_f_bias,
                   b_qn_w=b_qn_w, b_kn_w=b_kn_w, b_w_out=b_w_out, final_norm_w=final_norm_w)
    m_in = dict(norm_w=m_norm_w, ada_w=m_ada_w, ada_b=m_ada_b, a_w_in=m_a_w_in, a_conv_w=m_a_conv_w, a_A_log=m_a_A_log,
                a_dt_bias=m_a_dt_bias, a_norm_w=m_a_norm_w, a_w_out=m_a_w_out, b_w_in=m_b_w_in, b_f_bias=m_b_f_bias,
                b_qn_w=m_b_qn_w, b_kn_w=m_b_kn_w, b_w_out=m_b_w_out, final_norm_w=m_final_norm_w)
    v_in = dict(norm_w=v_norm_w, ada_w=v_ada_w, ada_b=v_ada_b, a_w_in=v_a_w_in, a_conv_w=v_a_conv_w, a_A_log=v_a_A_log,
                a_dt_bias=v_a_dt_bias, a_norm_w=v_a_norm_w, a_w_out=v_a_w_out, b_w_in=v_b_w_in, b_f_bias=v_b_f_bias,
                b_qn_w=v_b_qn_w, b_kn_w=v_b_kn_w, b_w_out=v_b_w_out, final_norm_w=v_final_norm_w)
    xi, yi, ci = _me()
    me_b, me_k = 4 * xi + 2 * yi + ci, 2 * xi + yi
    cidx = ci.astype(jnp.int32).reshape(1)
    S = x.shape[1]
    depth, n_a, n_b = norm_w.shape[0], a_w_in.shape[0], b_w_in.shape[0]
    x0, tgt = x.reshape(S, D), loss_target.reshape(S, D)

    c_all = ag_small(jnp.pad(c, ((0, 7), (0, 0)))).reshape(8, 8, D)[:, 0]
    nloc = ada_w.shape[2]
    parts = ag_small(ada_mod(c_all, ada_w).reshape(depth * 8, nloc)).reshape(4, 2, depth, 8, nloc)[:, 0]
    mine = lax.dynamic_index_in_dim(parts, me_b, axis=2, keepdims=False)
    mod = jnp.transpose(mine, (1, 0, 2)).reshape(depth, 4 * nloc) + ada_b
    shift, scale, gate = (mod[:, k * D:(k + 1) * D] for k in range(3))

    g_ain, g_aout, g_bin, g_bout, g_conv = ag_chips(
        [a_w_in.astype(bf16), a_w_out.astype(bf16), b_w_in.astype(bf16), b_w_out.astype(bf16), a_conv_w])

    def cols(g, l, pad):
        w = jnp.transpose(g[:, l], (1, 0, 2)).reshape(g.shape[2], -1)
        return jnp.pad(w, ((0, 0), (0, pad)))

    w_ain = [cols(g_ain, l, G_INP - G_IN) for l in range(n_a)]
    w_bin = [cols(g_bin, l, F_INP - F_IN) for l in range(n_b)]
    w_aout = [g_aout[:, l].reshape(-1, D) for l in range(n_a)]
    w_bout = [g_bout[:, l].reshape(-1, D) for l in range(n_b)]
    conv = [cols(g_conv, l, 0) for l in range(n_a)]
    qw2 = [_row(jnp.tile(b_qn_w[l], 2)) for l in range(n_b)]
    kw2 = [_row(jnp.tile(b_kn_w[l], 2)) for l in range(n_b)]

    saved, xc = [], x0
    for i in range(depth):
        l = i // 2
        h = ln_mod(xc, _row(norm_w[i]), _row(scale[i]), _row(shift[i]))
        if i % 2 == 0:
            proj = matmul(h, w_ain[l], "nn", "mm_a_in")
            pre = gdn_pre(proj, conv[l], _row(a_A_log[l]), _row(a_dt_bias[l]))
            o, wv, at, tinv, vn, st = gdn_fwd(*pre)
            o2 = gdn_onorm(o, proj, _row(a_norm_w[l]))
            y, xn = out_proj(o2, w_aout[l], xc, _row(gate[i]), "out_proj_a")
            saved.append((xc, h, proj, o2, y, pre, (o, wv, at, tinv, vn, st)))
        else:
            proj = matmul(h, w_bin[l], "nn", "mm_b_in")
            qa, ka, vb = fox_pre(proj, _row(b_f_bias[l]), qw2[l], kw2[l])
            o, lse = fox_attn(qa, ka, vb)
            o2 = fox_gate(o, proj)
            y, xn = out_proj(o2, w_bout[l], xc, _row(gate[i]), "out_proj_b")
            saved.append((xc, h, proj, o2, y, (qa, ka, vb), (o, lse)))
        xc = xn
    dx, st_f = final_loss(xc, _row(final_norm_w), tgt)

    d_norm, d_mod = [None] * depth, [None] * depth
    d_ain, d_aout, d_bin, d_bout = [None] * n_a, [None] * n_a, [None] * n_b, [None] * n_b
    d_conv, d_alog, d_dtb, d_anw = [None] * n_a, [None] * n_a, [None] * n_a, [None] * n_a
    d_fb, d_qn, d_kn = [None] * n_b, [None] * n_b, [None] * n_b
    for i in reversed(range(depth)):
        l = i // 2
        xin, h, proj, o2, y, pre, res = saved[i]
        dy, st_g = gate_bwd(dx, y, _row(gate[i]))
        if i % 2 == 0:
            o, wv, at, tinv, vn, st = res
            do2 = matmul(dy, w_aout[l], "nt", "mm_a_do2")
            d_aout[l] = matmul(o2, dy, "tn", "mm_a_dwo")
            do, dz, st_o = gdn_onorm_bwd(do2, o, proj, _row(a_norm_w[l]))
            grads = gdn_bwd(do, *pre, wv, at, tinv, vn, st)
            dcv, dba, st_s = gdn_pre_bwd(proj, conv[l], _row(a_A_log[l]), _row(a_dt_bias[l]), *grads)
            dproj, dcw = gdn_conv_bwd(proj, conv[l], dcv, dz, dba)
            dh = matmul(dproj, w_ain[l], "nt", "mm_a_dh")
            d_ain[l] = matmul(h, dproj, "tn", "mm_a_dw")
            d_conv[l], d_alog[l], d_dtb[l], d_anw[l] = dcw[:CONV_K], st_s[0], st_s[1], st_o[0]
        else:
            qa, ka, vb = pre
            o, lse = res
            do2 = matmul(dy, w_bout[l], "nt", "mm_b_do2")
            d_bout[l] = matmul(o2, dy, "tn", "mm_b_dwo")
            do, dz, delta = fox_gate_bwd(do2, o, proj)
            dqa, dka, dv = fox_attn_bwd(qa, ka, vb, do, lse, delta)
            dproj, st_b = fox_pre_bwd(proj, _row(b_f_bias[l]), qw2[l], kw2[l], dqa, dka, dv, dz)
            dh = matmul(dproj, w_bin[l], "nt", "mm_b_dh")
            d_bin[l] = matmul(h, dproj, "tn", "mm_b_dw")
            d_fb[l], d_qn[l], d_kn[l] = st_b[2, :F_H], st_b[0, :F_HD] + st_b[0, F_HD:], st_b[1, :F_HD] + st_b[1, F_HD:]
        dx, st_n = ln_mod_bwd(xin, _row(norm_w[i]), _row(scale[i]), dh, dx)
        d_norm[i] = st_n[0]
        d_mod[i] = jnp.concatenate([st_n[2], st_n[1], st_g[0]])

    small = [jnp.stack(d_norm), jnp.stack(d_mod), jnp.stack(d_conv), jnp.stack(d_alog), jnp.stack(d_dtb), jnp.stack(d_anw),
             jnp.stack(d_fb), jnp.stack(d_qn), jnp.stack(d_kn), st_f[0], jnp.sum(st_f[1]).reshape(1)]
    shapes = [a.shape for a in small]
    buf, offs = _pack(small)
    gathered = ag_small(buf).reshape(8, buf.shape[0], 128)
    tot = _unpack(sum_leading(gathered, "sum_devices"), offs, shapes)
    g_norm, g_adab, g_convf, g_alog, g_dtb, g_anw, g_fb, g_qn, g_kn, g_fin, loss = tot
    dmod_all = _unpack(gathered, offs[1:2], shapes[1:2])[0]
    dmod_loc = lax.dynamic_slice_in_dim(dmod_all, me_k * nloc, nloc, axis=2)
    g_adaw = ada_w_grad(c_all, jnp.transpose(dmod_loc, (1, 0, 2)))
    g_conv_loc = lax.dynamic_slice_in_dim(g_convf, me_k * a_conv_w.shape[2], a_conv_w.shape[2], axis=2)

    def col_blocks(ds, width):
        per = [jnp.transpose(d[:, :width].reshape(d.shape[0], 4, width // 4), (1, 0, 2)) for d in ds]
        return jnp.concatenate(per, axis=1)

    def row_blocks(ds):
        per = [d.reshape(4, d.shape[0] // 4, D) for d in ds]
        return jnp.concatenate(per, axis=1)

    r_ain, r_aout, r_bin, r_bout = reduce_scatter(
        [col_blocks(d_ain, G_IN), row_blocks(d_aout), col_blocks(d_bin, F_IN), row_blocks(d_bout)], cidx)

    grads = dict(norm_w=g_norm, ada_w=g_adaw, ada_b=g_adab, a_w_in=r_ain.reshape(a_w_in.shape), a_conv_w=g_conv_loc,
                 a_A_log=g_alog, a_dt_bias=g_dtb, a_norm_w=g_anw, a_w_out=r_aout.reshape(a_w_out.shape),
                 b_w_in=r_bin.reshape(b_w_in.shape), b_f_bias=g_fb, b_qn_w=g_qn, b_kn_w=g_kn,
                 b_w_out=r_bout.reshape(b_w_out.shape), final_norm_w=g_fin)
    names = list(weights)
    upd = {n: adamw(weights[n], grads[n], m_in[n], v_in[n], "adamw_" + n) for n in names}
    return (loss.reshape(()), dx.reshape(x.shape), *[grads[n] for n in names], *[upd[n][0] for n in names],
            *[upd[n][1] for n in names], *[upd[n][2] for n in names])
```

```python
import functools

import jax
import jax.numpy as jnp
from jax import lax
from jax.experimental import pallas as pl
from jax.experimental.pallas import tpu as pltpu

f32, bf16 = jnp.float32, jnp.bfloat16
HI = lax.Precision.HIGHEST
MESH = pl.DeviceIdType.MESH

EPS = 1e-6
D = 1024
CHUNK = 64
GQK_H, GV_H, GHD = 8, 16, 128
G_CONV = 4096
G_Z0 = 4096
G_BA0 = 6144
G_IN, G_INP = 6176, 6272
CONV_K = 4
F_H, F_HD = 16, 64
F_W = 1024
F_F0 = 4096
F_IN, F_INP = 4112, 4224
LR, B1, B2, AEPS, WD, STEP = 0.001, 0.9, 0.999, 1e-08, 0.01, 10
NEG = -1e30
VMEM_LIMIT = 56 * 1024 * 1024


def _nn(a, b, prec=None):
    return lax.dot_general(a, b, (((1,), (0,)), ((), ())), preferred_element_type=f32, precision=prec)


def _nt(a, b, prec=None):
    return lax.dot_general(a, b, (((1,), (1,)), ((), ())), preferred_element_type=f32, precision=prec)


def _tn(a, b, prec=None):
    return lax.dot_general(a, b, (((0,), (0,)), ((), ())), preferred_element_type=f32, precision=prec)


def _iota(shape, axis):
    return lax.broadcasted_iota(jnp.int32, shape, axis)


def _sigmoid(x):
    return 1.0 / (1.0 + jnp.exp(-x))


def _softplus(x):
    return jnp.maximum(x, 0.0) + jnp.log(1.0 + jnp.exp(-jnp.abs(x)))


def _silu(x):
    return x * _sigmoid(x)


def _dsilu(x):
    s = _sigmoid(x)
    return s * (1.0 + x * (1.0 - s))


def _params(sem=None, vmem=VMEM_LIMIT):
    return pltpu.CompilerParams(dimension_semantics=sem, vmem_limit_bytes=vmem)


def _row(v):
    return v.reshape(1, -1)


def _pick(n, pref):
    for t in pref:
        if n % t == 0:
            return t
    return n


def matmul(a, b, mode, name, out_dtype=f32):
    if mode == "nn":
        (M, K), (_, N) = a.shape, b.shape
    elif mode == "nt":
        (M, K), (N, _) = a.shape, b.shape
    else:
        (K, M), (_, N) = a.shape, b.shape
    tm = _pick(M, (512, 256, 128))
    tn = _pick(N, (1408, 1024, 896, 512, 384, 256, 128))
    tk = K if K <= 2048 else _pick(K, (1408, 1024, 896, 512, 384, 256, 128))
    nk = K // tk
    dot = {"nn": _nn, "nt": _nt, "tn": _tn}[mode]

    def body(a_ref, b_ref, o_ref, *acc):
        k = pl.program_id(2)
        part = dot(a_ref[...], b_ref[...])
        if nk == 1:
            o_ref[...] = part.astype(out_dtype)
        else:
            acc_ref = acc[0]

            @pl.when(k == 0)
            def _():
                acc_ref[...] = part

            @pl.when(k > 0)
            def _():
                acc_ref[...] += part

            @pl.when(k == nk - 1)
            def _():
                o_ref[...] = acc_ref[...].astype(out_dtype)

    a_spec = pl.BlockSpec((tk, tm), lambda i, j, k: (k, i)) if mode == "tn" else pl.BlockSpec((tm, tk), lambda i, j, k: (i, k))
    b_spec = pl.BlockSpec((tn, tk), lambda i, j, k: (j, k)) if mode == "nt" else pl.BlockSpec((tk, tn), lambda i, j, k: (k, j))
    return pl.pallas_call(
        body, name=name, grid=(M // tm, N // tn, nk),
        in_specs=[a_spec, b_spec], out_specs=pl.BlockSpec((tm, tn), lambda i, j, k: (i, j)),
        out_shape=jax.ShapeDtypeStruct((M, N), out_dtype),
        scratch_shapes=[] if nk == 1 else [pltpu.VMEM((tm, tn), f32)],
        compiler_params=_params(("parallel", "parallel", "arbitrary")),
    )(a, b)


def out_proj(o2, w, x, gate, name):
    S, K = o2.shape
    N = w.shape[1]
    tm, tn = 512, 512

    def body(a_ref, b_ref, x_ref, g_ref, y_ref, xn_ref):
        y = _nn(a_ref[...], b_ref[...])
        y_ref[...] = y
        xn_ref[...] = x_ref[...] + g_ref[...] * y

    return pl.pallas_call(
        body, name=name, grid=(S // tm, N // tn),
        in_specs=[pl.BlockSpec((tm, K), lambda i, j: (i, 0)), pl.BlockSpec((K, tn), lambda i, j: (0, j)),
                  pl.BlockSpec((tm, tn), lambda i, j: (i, j)), pl.BlockSpec((1, tn), lambda i, j: (0, j))],
        out_specs=[pl.BlockSpec((tm, tn), lambda i, j: (i, j))] * 2,
        out_shape=[jax.ShapeDtypeStruct((S, N), f32)] * 2,
        compiler_params=_params(("parallel", "parallel")),
    )(o2, w, x, gate)


def ln_mod(x, nw, scale, shift):
    S = x.shape[0]
    tm = 512

    def body(x_ref, nw_ref, sc_ref, sh_ref, h_ref):
        xv = x_ref[...]
        r = lax.rsqrt(jnp.mean(xv * xv, axis=-1, keepdims=True) + EPS)
        h_ref[...] = ((xv * r) * nw_ref[...] * (1.0 + sc_ref[...]) + sh_ref[...]).astype(bf16)

    vec = pl.BlockSpec((1, D), lambda i: (0, 0))
    return pl.pallas_call(
        body, name="ln_mod", grid=(S // tm,),
        in_specs=[pl.BlockSpec((tm, D), lambda i: (i, 0)), vec, vec, vec],
        out_specs=pl.BlockSpec((tm, D), lambda i: (i, 0)),
        out_shape=jax.ShapeDtypeStruct((S, D), bf16),
        compiler_params=_params(("parallel",)),
    )(x, nw, scale, shift)


def ln_mod_bwd(x, nw, scale, dh, dxres):
    S = x.shape[0]
    tm = 512
    nb = S // tm

    def body(x_ref, nw_ref, sc_ref, dh_ref, dr_ref, dx_ref, st_ref):
        i = pl.program_id(0)
        xv = x_ref[...]
        r = lax.rsqrt(jnp.mean(xv * xv, axis=-1, keepdims=True) + EPS)
        xn = xv * r
        dh = dh_ref[...]
        dxn = dh * (nw_ref[...] * (1.0 + sc_ref[...]))
        dx_ref[...] = dr_ref[...] + r * (dxn - xn * jnp.mean(dxn * xn, axis=-1, keepdims=True))
        p1 = jnp.sum(dh * xn, axis=0, keepdims=True)
        p2 = jnp.sum(dh, axis=0, keepdims=True)
        upd = jnp.concatenate([p1, p1, p2, jnp.zeros((5, D), f32)], axis=0)

        @pl.when(i == 0)
        def _():
            st_ref[...] = upd

        @pl.when(i > 0)
        def _():
            st_ref[...] += upd

        @pl.when(i == nb - 1)
        def _():
            st_ref[0:1, :] = st_ref[0:1, :] * (1.0 + sc_ref[...])
            st_ref[1:2, :] = st_ref[1:2, :] * nw_ref[...]

    vec = pl.BlockSpec((1, D), lambda i: (0, 0))
    tile = pl.BlockSpec((tm, D), lambda i: (i, 0))
    return pl.pallas_call(
        body, name="ln_mod_bwd", grid=(S // tm,),
        in_specs=[tile, vec, vec, tile, tile],
        out_specs=[tile, pl.BlockSpec((8, D), lambda i: (0, 0))],
        out_shape=[jax.ShapeDtypeStruct((S, D), f32), jax.ShapeDtypeStruct((8, D), f32)],
        compiler_params=_params(("arbitrary",)),
    )(x, nw, scale, dh, dxres)


def final_loss(x, fw, tgt):
    S = x.shape[0]
    tm = 512

    def body(x_ref, w_ref, t_ref, dx_ref, st_ref):
        i = pl.program_id(0)
        xv = x_ref[...]
        r = lax.rsqrt(jnp.mean(xv * xv, axis=-1, keepdims=True) + EPS)
        xn = xv * r
        err = xn * w_ref[...] - t_ref[...]
        dy = err * (1.0 / D)
        dxn = dy * w_ref[...]
        dx_ref[...] = r * (dxn - xn * jnp.mean(dxn * xn, axis=-1, keepdims=True))
        p1 = jnp.sum(dy * xn, axis=0, keepdims=True)
        p2 = jnp.sum(err * err, axis=0, keepdims=True) * (0.5 / D)
        upd = jnp.concatenate([p1, p2, jnp.zeros((6, D), f32)], axis=0)

        @pl.when(i == 0)
        def _():
            st_ref[...] = upd

        @pl.when(i > 0)
        def _():
            st_ref[...] += upd

    tile = pl.BlockSpec((tm, D), lambda i: (i, 0))
    return pl.pallas_call(
        body, name="final_loss", grid=(S // tm,),
        in_specs=[tile, pl.BlockSpec((1, D), lambda i: (0, 0)), tile],
        out_specs=[tile, pl.BlockSpec((8, D), lambda i: (0, 0))],
        out_shape=[jax.ShapeDtypeStruct((S, D), f32), jax.ShapeDtypeStruct((8, D), f32)],
        compiler_params=_params(("arbitrary",)),
    )(x, fw, tgt)


def gate_bwd(dx, y, gate):
    S = dx.shape[0]
    tm = 512

    def body(dx_ref, y_ref, g_ref, dy_ref, st_ref):
        i = pl.program_id(0)
        dxv = dx_ref[...]
        dy_ref[...] = (g_ref[...] * dxv).astype(bf16)
        upd = jnp.concatenate([jnp.sum(dxv * y_ref[...], axis=0, keepdims=True), jnp.zeros((7, D), f32)], axis=0)

        @pl.when(i == 0)
        def _():
            st_ref[...] = upd

        @pl.when(i > 0)
        def _():
            st_ref[...] += upd

    tile = pl.BlockSpec((tm, D), lambda i: (i, 0))
    return pl.pallas_call(
        body, name="gate_bwd", grid=(S // tm,),
        in_specs=[tile, tile, pl.BlockSpec((1, D), lambda i: (0, 0))],
        out_specs=[tile, pl.BlockSpec((8, D), lambda i: (0, 0))],
        out_shape=[jax.ShapeDtypeStruct((S, D), bf16), jax.ShapeDtypeStruct((8, D), f32)],
        compiler_params=_params(("arbitrary",)),
    )(dx, y, gate)


def _chunk_mats(tm):
    r, c = _iota((tm, tm), 0), _iota((tm, tm), 1)
    same = jnp.right_shift(r, 6) == jnp.right_shift(c, 6)
    ltri = jnp.where(same & (c <= r), 1.0, 0.0).astype(f32)
    utri = jnp.where(same & (c >= r), 1.0, 0.0).astype(f32)
    bsame = jnp.where(same, 1.0, 0.0).astype(f32)
    return ltri, utri, bsame


def _gdn_scalars(ba, alog, dtb, ltri, bsame):
    beta = _sigmoid(ba[:, 0:16])
    u = ba[:, 16:32] + dtb
    neg_a = -jnp.exp(alog)
    g = neg_a * _softplus(u)
    gc = _nn(ltri, g, HI)
    glast = _nn(bsame, g, HI)
    return beta, u, neg_a, g, gc, glast


def _conv_taps(p_ref, halo_ref, first, gi):
    cs = slice(gi * 128, (gi + 1) * 128)
    cur = p_ref[:, cs]
    hal = jnp.where(first, 0.0, halo_ref[:, cs])
    ext = jnp.concatenate([hal, cur], axis=0)
    return [cur] + [pltpu.roll(ext, s, 0)[8:] for s in range(1, CONV_K)]


def _conv(taps, w):
    cv = taps[0] * w[3:4]
    for s in range(1, CONV_K):
        cv = cv + taps[s] * w[3 - s:4 - s]
    return cv


def _l2n(x):
    return x * lax.rsqrt(jnp.sum(x * x, axis=-1, keepdims=True) + EPS)


def _gdn_in_specs(tm, S):
    nb8 = tm // 8
    return [pl.BlockSpec((tm, G_CONV), lambda i: (i, 0)),
            pl.BlockSpec((8, G_CONV), lambda i: (jnp.maximum(i * nb8 - 1, 0), 0)),
            pl.BlockSpec((tm, 128), lambda i: (i, G_BA0 // 128))]


def gdn_pre(proj, conv_w, alog, dtb):
    S = proj.shape[0]
    tm = 256
    nch = tm // CHUNK

    def body(p_ref, halo_ref, ba_ref, w_ref, al_ref, dt_ref,
             q_ref, k_ref, kb_ref, kbg_ref, vb_ref, qd_ref, kd_ref, d_ref, gl_ref):
        first = pl.program_id(0) == 0
        ltri, _, bsame = _chunk_mats(tm)
        beta, _, _, _, gc, glast = _gdn_scalars(ba_ref[...], al_ref[...], dt_ref[...], ltri, bsame)
        eg, ek, egl = jnp.exp(gc), jnp.exp(glast - gc), jnp.exp(glast)
        eye = jnp.where(_iota((16, 16), 0) == _iota((16, 16), 1), 1.0, 0.0).astype(f32)
        gct = _nt(eye, gc, HI)
        low = _iota((CHUNK, CHUNK), 0) >= _iota((CHUNK, CHUNK), 1)

        def act(gi):
            return _silu(_conv(_conv_taps(p_ref, halo_ref, first, gi), w_ref[:, gi * 128:(gi + 1) * 128]))

        for j in range(GQK_H):
            js = slice(j * 128, (j + 1) * 128)
            qn = _l2n(act(j)) * (GHD ** -0.5)
            kn = _l2n(act(GQK_H + j))
            q_ref[:, js] = qn.astype(bf16)
            k_ref[:, js] = kn.astype(bf16)
            for e in range(2):
                h = 2 * j + e
                hs = slice(h * 128, (h + 1) * 128)
                v = act(2 * GQK_H + h)
                bh, egh, ekh = beta[:, h:h + 1], eg[:, h:h + 1], ek[:, h:h + 1]
                kbv = kn * bh
                kb_ref[:, hs] = kbv.astype(bf16)
                kbg_ref[:, hs] = (kbv * egh).astype(bf16)
                vb_ref[:, hs] = (v * bh).astype(bf16)
                qd_ref[:, hs] = (qn * egh).astype(bf16)
                kd_ref[:, hs] = (kn * ekh).astype(bf16)
                for c in range(nch):
                    rs = slice(c * CHUNK, (c + 1) * CHUNK)
                    diff = gc[rs, h:h + 1] - gct[h:h + 1, rs]
                    d_ref[rs, h * CHUNK:(h + 1) * CHUNK] = jnp.where(low, jnp.exp(jnp.where(low, diff, 0.0)), 0.0)
                    gl_ref[c * 8:(c + 1) * 8, hs] = jnp.broadcast_to(egl[c * CHUNK:c * CHUNK + 8, h:h + 1], (8, 128))

    full = lambda shape: pl.BlockSpec(shape, lambda i: (0, 0))
    t1 = pl.BlockSpec((tm, 1024), lambda i: (i, 0))
    t2 = pl.BlockSpec((tm, 2048), lambda i: (i, 0))
    sd = jax.ShapeDtypeStruct
    return pl.pallas_call(
        body, name="gdn_pre", grid=(S // tm,),
        in_specs=_gdn_in_specs(tm, S) + [full((CONV_K, G_CONV)), full((1, 16)), full((1, 16))],
        out_specs=[t1, t1, t2, t2, t2, t2, t2, t1, pl.BlockSpec((tm // 8, 2048), lambda i: (i, 0))],
        out_shape=[sd((S, 1024), bf16)] * 2 + [sd((S, 2048), bf16)] * 5 + [sd((S, 1024), f32), sd((S // 8, 2048), f32)],
        compiler_params=_params(("parallel",)),
    )(proj, proj, proj, conv_w, alog, dtb)


def _bnn(a, b):
    return lax.dot_general(a, b, (((2,), (1,)), ((0,), (0,))), preferred_element_type=f32)


def _bnt(a, b):
    return lax.dot_general(a, b, (((2,), (2,)), ((0,), (0,))), preferred_element_type=f32)


def _btn(a, b):
    return lax.dot_general(a, b, (((1,), (1,)), ((0,), (0,))), preferred_element_type=f32)


def _split(a):
    hi = a.astype(bf16)
    return hi, (a - hi.astype(f32)).astype(bf16)


def _cat3(h, l, axis, lhs):
    return jnp.concatenate([h, h, l] if lhs else [h, l, h], axis=axis)


def _tri_inv_b(L):
    eye = jnp.where(_iota((1, CHUNK, CHUNK), 1) == _iota((1, CHUNK, CHUNK), 2), 1.0, 0.0).astype(f32)
    P = -L
    T = eye + P
    ph, pl_ = _split(P)
    for _ in range(5):
        P = _bnn(_cat3(ph, pl_, 2, True), _cat3(ph, pl_, 1, False))
        ph, pl_ = _split(P)
        th, tl = _split(T)
        T = T + _bnn(_cat3(th, tl, 2, True), _cat3(ph, pl_, 1, False))
    return T


GTB = 512


def _gdn_slices(ncb):
    pairs = [(c, e) for c in range(ncb) for e in range(2)]
    rs = lambda c: slice(c * CHUNK, (c + 1) * CHUNK)
    cs = lambda e: slice(e * 128, (e + 1) * 128)
    ds_ = lambda e: slice(e * CHUNK, (e + 1) * CHUNK)
    return pairs, rs, cs, ds_


def gdn_fwd(q, k, kb, kbg, vb, qd, kd, dm, gl8):
    S = q.shape[0]
    nb, ncb = S // GTB, GTB // CHUNK
    pairs, rs, cs, ds_ = _gdn_slices(ncb)

    def body(q_ref, k_ref, kb_ref, kbg_ref, vb_ref, qd_ref, kd_ref, d_ref, gl_ref,
             o_ref, w_ref, at_ref, t_ref, vn_ref, st_ref, state, u_scr):
        @pl.when(pl.program_id(1) == 0)
        def _():
            state[...] = jnp.zeros_like(state)

        stk = lambda ref, lanes: jnp.stack([ref[rs(c), lanes(e)] for c, e in pairs])
        kq = jnp.stack([k_ref[rs(c), :] for c, _ in pairs])
        dmat = stk(d_ref, ds_)
        strict = _iota((1, CHUNK, CHUNK), 1) > _iota((1, CHUNK, CHUNK), 2)
        T = _tri_inv_b(jnp.where(strict, _bnt(stk(kb_ref, cs), kq) * dmat, 0.0))
        tb = T.astype(bf16)
        u_scr[...] = _bnn(tb, stk(vb_ref, cs))
        wb = _bnn(tb, stk(kbg_ref, cs)).astype(bf16)
        qk = _bnt(jnp.stack([q_ref[rs(c), :] for c in range(ncb)]), jnp.stack([k_ref[rs(c), :] for c in range(ncb)]))
        for b, (c, e) in enumerate(pairs):
            w_ref[rs(c), cs(e)] = wb[b]
            at_ref[rs(c), ds_(e)] = (qk[c] * dmat[b]).astype(bf16)
            t_ref[rs(c), ds_(e)] = T[b]
        for b, (c, e) in enumerate(pairs):
            sb = state[e].astype(bf16)
            vnb = (u_scr[b] - _nn(w_ref[rs(c), cs(e)], sb)).astype(bf16)
            o_ref[rs(c), cs(e)] = _nn(qd_ref[rs(c), cs(e)], sb) + _nn(at_ref[rs(c), ds_(e)], vnb)
            st_ref[c * 128:(c + 1) * 128, cs(e)] = sb
            state[e] = state[e] * gl_ref[c * 8:c * 8 + 1, cs(e)] + _tn(kd_ref[rs(c), cs(e)], vnb)
            vn_ref[rs(c), cs(e)] = vnb

    b1 = pl.BlockSpec((GTB, 128), lambda j, i: (i, j))
    b2 = pl.BlockSpec((GTB, 256), lambda j, i: (i, j))
    sd = jax.ShapeDtypeStruct
    return pl.pallas_call(
        body, name="gdn_fwd", grid=(GQK_H, nb),
        in_specs=[b1, b1, b2, b2, b2, b2, b2, b1, pl.BlockSpec((GTB // 8, 256), lambda j, i: (i, j))],
        out_specs=[b2, b2, b1, b1, b2, pl.BlockSpec((ncb * 128, 256), lambda j, i: (i, j))],
        out_shape=[sd((S, 2048), f32), sd((S, 2048), bf16), sd((S, 1024), bf16), sd((S, 1024), f32),
                   sd((S, 2048), bf16), sd((S // CHUNK * 128, 2048), bf16)],
        scratch_shapes=[pltpu.VMEM((2, 128, 128), f32), pltpu.VMEM((2 * ncb, CHUNK, 128), f32)],
        compiler_params=_params(("parallel", "arbitrary")),
    )(q, k, kb, kbg, vb, qd, kd, dm, gl8)


def gdn_bwd(do, q, k, kb, kbg, vb, qd, kd, dm, gl8, w, at, T, vn, st):
    S = q.shape[0]
    nb, ncb = S // GTB, GTB // CHUNK
    pairs, rs, cs, ds_ = _gdn_slices(ncb)

    def body(do_ref, q_ref, k_ref, kb_ref, kbg_ref, vb_ref, qd_ref, kd_ref, d_ref, gl_ref, w_ref, at_ref, t_ref, vn_ref, st_ref,
             dq_ref, dk_ref, dkb_ref, dkbg_ref, dvb_ref, dqd_ref, dkd_ref, dgc_ref, dstate, dvn_scr, dw_scr, dat_scr, dgl_scr):
        @pl.when(pl.program_id(1) == 0)
        def _():
            dstate[...] = jnp.zeros_like(dstate)

        for b, (c, e) in reversed(list(enumerate(pairs))):
            dob = do_ref[rs(c), cs(e)].astype(bf16)
            sb = st_ref[c * 128:(c + 1) * 128, cs(e)]
            vnb = vn_ref[rs(c), cs(e)]
            gl = gl_ref[c * 8:c * 8 + 1, cs(e)]
            dS = dstate[e]
            dsb = dS.astype(bf16)
            dvnb = (_tn(at_ref[rs(c), ds_(e)], dob) + _nn(kd_ref[rs(c), cs(e)], dsb)).astype(bf16)
            dvn_scr[b] = dvnb
            dat_scr[b] = _nt(dob, vnb)
            dqd_ref[rs(c), cs(e)] = _nt(dob, sb)
            dkd_ref[rs(c), cs(e)] = _nt(vnb, dsb)
            dw_scr[b] = (-_nt(dvnb, sb)).astype(bf16)
            dgl = jnp.sum(jnp.sum(dS * sb.astype(f32), axis=1, keepdims=True), axis=0, keepdims=True)
            dgl_scr[b] = jnp.broadcast_to(dgl * gl, (8, 128))
            dstate[e] = gl * dS + _tn(qd_ref[rs(c), cs(e)], dob) - _tn(w_ref[rs(c), cs(e)], dvnb)

        stk = lambda ref, lanes: jnp.stack([ref[rs(c), lanes(e)] for c, e in pairs])
        kq = jnp.stack([k_ref[rs(c), :] for c, _ in pairs])
        qq = jnp.stack([q_ref[rs(c), :] for c, _ in pairs])
        kbb = stk(kb_ref, cs)
        Tm = stk(t_ref, ds_)
        tb = Tm.astype(bf16)
        dvn, dw = dvn_scr[...], dw_scr[...]
        dT = _bnt(dvn, stk(vb_ref, cs)) + _bnt(dw, stk(kbg_ref, cs))
        dvb, dkbg = _btn(tb, dvn), _btn(tb, dw)
        th, tl = _split(Tm)
        xh, xl = _split(_bnt(_cat3(*_split(dT), 2, True), _cat3(th, tl, 2, False)))
        dL = -_btn(_cat3(th, tl, 1, True), _cat3(xh, xl, 1, False))
        dmat = stk(d_ref, ds_)
        strict = _iota((1, CHUNK, CHUNK), 1) > _iota((1, CHUNK, CHUNK), 2)
        dA = jnp.where(strict, dL * dmat, 0.0)
        dB = dat_scr[...] * dmat
        dAb, dBb = dA.astype(bf16), dB.astype(bf16)
        dkb = _bnn(dAb, kq)
        dkc = _btn(dAb, kbb) + _btn(dBb, qq)
        dqc = _bnn(dBb, kq)
        M = dA * _bnt(kbb, kq) + dB * _bnt(qq, kq)
        mh, ml = _split(M)
        colsum = _btn(jnp.concatenate([mh, ml], axis=1), jnp.ones((2 * ncb, 2 * CHUNK, 128), bf16))
        lastrow = _iota((1, CHUNK, 128), 1) == CHUNK - 1
        for b, (c, e) in enumerate(pairs):
            dvb_ref[rs(c), cs(e)] = dvb[b]
            dkbg_ref[rs(c), cs(e)] = dkbg[b]
            dkb_ref[rs(c), cs(e)] = dkb[b]
            dgc_ref[rs(c), cs(e)] = (jnp.sum(M[b], axis=1, keepdims=True) - colsum[b]
                                     + jnp.where(lastrow[0], dgl_scr[b][0:1, :], 0.0))
        for c in range(ncb):
            dq_ref[rs(c), :] = dqc[2 * c] + dqc[2 * c + 1]
            dk_ref[rs(c), :] = dkc[2 * c] + dkc[2 * c + 1]

    b1 = pl.BlockSpec((GTB, 128), lambda j, i: (nb - 1 - i, j))
    b2 = pl.BlockSpec((GTB, 256), lambda j, i: (nb - 1 - i, j))
    sd = jax.ShapeDtypeStruct
    return pl.pallas_call(
        body, name="gdn_bwd", grid=(GQK_H, nb),
        in_specs=[b2, b1, b1, b2, b2, b2, b2, b2, b1, pl.BlockSpec((GTB // 8, 256), lambda j, i: (nb - 1 - i, j)),
                  b2, b1, b1, b2, pl.BlockSpec((ncb * 128, 256), lambda j, i: (nb - 1 - i, j))],
        out_specs=[b1, b1, b2, b2, b2, b2, b2, b2],
        out_shape=[sd((S, 1024), f32)] * 2 + [sd((S, 2048), f32)] * 6,
        scratch_shapes=[pltpu.VMEM((2, 128, 128), f32), pltpu.VMEM((2 * ncb, CHUNK, 128), bf16),
                        pltpu.VMEM((2 * ncb, CHUNK, 128), bf16), pltpu.VMEM((2 * ncb, CHUNK, CHUNK), f32),
                        pltpu.VMEM((2 * ncb, 8, 128), f32)],
        compiler_params=_params(("parallel", "arbitrary")),
    )(do, q, k, kb, kbg, vb, qd, kd, dm, gl8, w, at, T, vn, st)


def gdn_onorm(o, proj, nw):
    S = o.shape[0]
    tm = 256

    def body(o_ref, z_ref, nw_ref, o2_ref):
        for h in range(GV_H):
            hs = slice(h * 128, (h + 1) * 128)
            oh = o_ref[:, hs]
            r = lax.rsqrt(jnp.mean(oh * oh, axis=-1, keepdims=True) + EPS)
            o2_ref[:, hs] = (((oh * r) * nw_ref[...]) * _silu(z_ref[:, hs])).astype(bf16)

    t2 = pl.BlockSpec((tm, 2048), lambda i: (i, 0))
    return pl.pallas_call(
        body, name="gdn_onorm", grid=(S // tm,),
        in_specs=[t2, pl.BlockSpec((tm, 2048), lambda i: (i, G_Z0 // 2048)), pl.BlockSpec((1, 128), lambda i: (0, 0))],
        out_specs=t2, out_shape=jax.ShapeDtypeStruct((S, 2048), bf16),
        compiler_params=_params(("parallel",)),
    )(o, proj, nw)


def gdn_onorm_bwd(do2, o, proj, nw):
    S = o.shape[0]
    tm = 256

    def body(d_ref, o_ref, z_ref, nw_ref, do_ref, dz_ref, st_ref):
        i = pl.program_id(0)
        acc = jnp.zeros((1, 128), f32)
        for h in range(GV_H):
            hs = slice(h * 128, (h + 1) * 128)
            oh, z, d2 = o_ref[:, hs], z_ref[:, hs], d_ref[:, hs]
            r = lax.rsqrt(jnp.mean(oh * oh, axis=-1, keepdims=True) + EPS)
            on = oh * r
            dt = d2 * _silu(z)
            dz_ref[:, hs] = (d2 * (on * nw_ref[...]) * _dsilu(z)).astype(bf16)
            don = dt * nw_ref[...]
            acc = acc + jnp.sum(dt * on, axis=0, keepdims=True)
            do_ref[:, hs] = r * (don - on * jnp.mean(don * on, axis=-1, keepdims=True))
        upd = jnp.concatenate([acc, jnp.zeros((7, 128), f32)], axis=0)

        @pl.when(i == 0)
        def _():
            st_ref[...] = upd

        @pl.when(i > 0)
        def _():
            st_ref[...] += upd

    t2 = pl.BlockSpec((tm, 2048), lambda i: (i, 0))
    sd = jax.ShapeDtypeStruct
    return pl.pallas_call(
        body, name="gdn_onorm_bwd", grid=(S // tm,),
        in_specs=[t2, t2, pl.BlockSpec((tm, 2048), lambda i: (i, G_Z0 // 2048)), pl.BlockSpec((1, 128), lambda i: (0, 0))],
        out_specs=[t2, t2, pl.BlockSpec((8, 128), lambda i: (0, 0))],
        out_shape=[sd((S, 2048), f32), sd((S, 2048), bf16), sd((8, 128), f32)],
        compiler_params=_params(("arbitrary",)),
    )(do2, o, proj, nw)


def gdn_pre_bwd(proj, conv_w, alog, dtb, dq, dk, dkb, dkbg, dvb, dqd, dkd, dgcd):
    S = proj.shape[0]
    tm = 128

    def body(p_ref, halo_ref, ba_ref, w_ref, al_ref, dt_ref, dq_ref, dk_ref, dkb_ref, dkbg_ref, dvb_ref, dqd_ref, dkd_ref, dgc_ref,
             dcv_ref, dba_ref, st_ref):
        i = pl.program_id(0)
        first = i == 0
        ltri, utri, bsame = _chunk_mats(tm)
        beta, u, neg_a, g, gc, glast = _gdn_scalars(ba_ref[...], al_ref[...], dt_ref[...], ltri, bsame)
        eg, ek = jnp.exp(gc), jnp.exp(glast - gc)
        lane16 = _iota((tm, 16), 1)
        dgc_all = jnp.zeros((tm, 16), f32)
        rkd_all = jnp.zeros((tm, 16), f32)
        dbeta_all = jnp.zeros((tm, 16), f32)

        def pre(gi):
            return _conv(_conv_taps(p_ref, halo_ref, first, gi), w_ref[:, gi * 128:(gi + 1) * 128])

        def l2n_bwd(xt, dy):
            r = lax.rsqrt(jnp.sum(xt * xt, axis=-1, keepdims=True) + EPS)
            y = xt * r
            return r * (dy - y * jnp.sum(dy * y, axis=-1, keepdims=True))

        for j in range(GQK_H):
            js = slice(j * 128, (j + 1) * 128)
            cvq, cvk = pre(j), pre(GQK_H + j)
            qt, kt = _silu(cvq), _silu(cvk)
            qn = _l2n(qt) * (GHD ** -0.5)
            kn = _l2n(kt)
            dq_tot, dk_tot = dq_ref[:, js], dk_ref[:, js]
            for e in range(2):
                h = 2 * j + e
                hs = slice(h * 128, (h + 1) * 128)
                gv = 2 * GQK_H + h
                cvv = pre(gv)
                v = _silu(cvv)
                bh, egh, ekh = beta[:, h:h + 1], eg[:, h:h + 1], ek[:, h:h + 1]
                dkbg, dkd, dqd, dvb = dkbg_ref[:, hs], dkd_ref[:, hs], dqd_ref[:, hs], dvb_ref[:, hs]
                dkb_t = dkb_ref[:, hs] + dkbg * egh
                dk_tot = dk_tot + dkb_t * bh + dkd * ekh
                dq_tot = dq_tot + dqd * egh
                dcv_ref[:, gv * 128:(gv + 1) * 128] = (dvb * bh) * _dsilu(cvv)
                dbeta = jnp.sum(dkb_t * kn, axis=-1, keepdims=True) + jnp.sum(dvb * v, axis=-1, keepdims=True)
                rkd = jnp.sum(dkd * (kn * ekh), axis=-1, keepdims=True)
                dgc = (dgc_ref[:, hs][:, 0:1] + jnp.sum(dkbg * (kn * bh * egh), axis=-1, keepdims=True)
                       + jnp.sum(dqd * (qn * egh), axis=-1, keepdims=True) - rkd)
                sel = lane16 == h
                dgc_all = dgc_all + jnp.where(sel, dgc, 0.0)
                rkd_all = rkd_all + jnp.where(sel, rkd, 0.0)
                dbeta_all = dbeta_all + jnp.where(sel, dbeta, 0.0)
            dcv_ref[:, js] = l2n_bwd(qt, dq_tot * (GHD ** -0.5)) * _dsilu(cvq)
            ks = slice((GQK_H + j) * 128, (GQK_H + j + 1) * 128)
            dcv_ref[:, ks] = l2n_bwd(kt, dk_tot) * _dsilu(cvk)

        islast = jnp.bitwise_and(_iota((tm, 16), 0), CHUNK - 1) == CHUNK - 1
        dgc_all = dgc_all + jnp.where(islast, _nn(bsame, rkd_all, HI), 0.0)
        dg = _nn(utri, dgc_all, HI)
        da = dg * neg_a * _sigmoid(u)
        db = dbeta_all * beta * (1.0 - beta)
        r16, c128 = _iota((16, 128), 0), _iota((16, 128), 1)
        pb = jnp.where(c128 == r16, 1.0, 0.0).astype(f32)
        pa = jnp.where(c128 == r16 + 16, 1.0, 0.0).astype(f32)
        dba_ref[...] = _nn(db, pb, HI) + _nn(da, pa, HI)
        upd = jnp.concatenate([jnp.sum(dg * g, axis=0, keepdims=True), jnp.sum(da, axis=0, keepdims=True),
                               jnp.zeros((6, 16), f32)], axis=0)

        @pl.when(i == 0)
        def _():
            st_ref[...] = upd

        @pl.when(i > 0)
        def _():
            st_ref[...] += upd

    full = lambda shape: pl.BlockSpec(shape, lambda i: (0, 0))
    t1 = pl.BlockSpec((tm, 1024), lambda i: (i, 0))
    t2 = pl.BlockSpec((tm, 2048), lambda i: (i, 0))
    sd = jax.ShapeDtypeStruct
    return pl.pallas_call(
        body, name="gdn_pre_bwd", grid=(S // tm,),
        in_specs=_gdn_in_specs(tm, S) + [full((CONV_K, G_CONV)), full((1, 16)), full((1, 16)), t1, t1] + [t2] * 6,
        out_specs=[pl.BlockSpec((tm, G_CONV), lambda i: (i, 0)), pl.BlockSpec((tm, 128), lambda i: (i, 0)), full((8, 16))],
        out_shape=[sd((S, G_CONV), f32), sd((S, 128), f32), sd((8, 16), f32)],
        compiler_params=_params(("arbitrary",)),
    )(proj, proj, proj, conv_w, alog, dtb, dq, dk, dkb, dkbg, dvb, dqd, dkd, dgcd)


def gdn_conv_bwd(proj, conv_w, dcv, dz, dba):
    S = proj.shape[0]
    tm = 256
    nb, nb8 = S // tm, tm // 8

    def body(p_ref, halo_ref, w_ref, dcv_ref, nxt_ref, dz_ref, dba_ref, dp_ref, dw_ref):
        i = pl.program_id(0)
        first, last = i == 0, i == nb - 1
        for gi in range(G_CONV // 128):
            cs = slice(gi * 128, (gi + 1) * 128)
            taps = _conv_taps(p_ref, halo_ref, first, gi)
            cur = dcv_ref[:, cs]
            ext = jnp.concatenate([cur, jnp.where(last, 0.0, nxt_ref[:, cs])], axis=0)
            w = w_ref[:, cs]
            dp = cur * w[3:4]
            rows = [jnp.sum(cur * taps[3 - kk], axis=0, keepdims=True) for kk in range(CONV_K)]
            for s in range(1, CONV_K):
                dp = dp + pltpu.roll(ext, tm + 8 - s, 0)[:tm] * w[3 - s:4 - s]
            dp_ref[:, cs] = dp.astype(bf16)
            upd = jnp.concatenate(rows + [jnp.zeros((4, 128), f32)], axis=0)

            @pl.when(first)
            def _():
                dw_ref[:, cs] = upd

            @pl.when(i > 0)
            def _():
                dw_ref[:, cs] += upd

        dp_ref[:, G_Z0:G_BA0] = dz_ref[...]
        dp_ref[:, G_BA0:G_INP] = dba_ref[...].astype(bf16)

    sd = jax.ShapeDtypeStruct
    return pl.pallas_call(
        body, name="gdn_conv_bwd", grid=(nb,),
        in_specs=[pl.BlockSpec((tm, G_CONV), lambda i: (i, 0)),
                  pl.BlockSpec((8, G_CONV), lambda i: (jnp.maximum(i * nb8 - 1, 0), 0)),
                  pl.BlockSpec((CONV_K, G_CONV), lambda i: (0, 0)),
                  pl.BlockSpec((tm, G_CONV), lambda i: (i, 0)),
                  pl.BlockSpec((8, G_CONV), lambda i: (jnp.minimum((i + 1) * nb8, S // 8 - 1), 0)),
                  pl.BlockSpec((tm, 2048), lambda i: (i, 0)), pl.BlockSpec((tm, 128), lambda i: (i, 0))],
        out_specs=[pl.BlockSpec((tm, G_INP), lambda i: (i, 0)), pl.BlockSpec((8, G_CONV), lambda i: (0, 0))],
        out_shape=[sd((S, G_INP), bf16), sd((8, G_CONV), f32)],
        compiler_params=_params(("arbitrary",)),
    )(proj, proj, conv_w, dcv, dcv, dz, dba)


def _half_mean(t, lo_half):
    m0 = jnp.sum(jnp.where(lo_half, t, 0.0), axis=-1, keepdims=True)
    m1 = jnp.sum(jnp.where(lo_half, 0.0, t), axis=-1, keepdims=True)
    return jnp.where(lo_half, m0, m1) * (1.0 / F_HD)


def _split3(c):
    hi = c.astype(bf16).astype(f32)
    mid = (c - hi).astype(bf16).astype(f32)
    lo = (c - hi - mid).astype(bf16).astype(f32)
    return hi, mid, lo


def fox_pre(proj, fbias, qw2, kw2):
    S = proj.shape[0]
    tm = 256

    def body(q_ref, k_ref, v_ref, f_ref, fb_ref, qw_ref, kw_ref, qa_ref, ka_ref, vb_ref, carry):
        @pl.when(pl.program_id(0) == 0)
        def _():
            carry[...] = jnp.zeros_like(carry)

        logf = -_softplus(-(f_ref[:, 0:16] + fb_ref[...]))
        ltri = jnp.where(_iota((tm, tm), 1) <= _iota((tm, tm), 0), 1.0, 0.0).astype(f32)
        cum = _nn(ltri, logf, HI) + carry[0:1, :]
        carry[0:1, :] = cum[tm - 1:tm, :]
        lane = _iota((tm, 128), 1)
        lo_half = lane < F_HD
        for p in range(F_H // 2):
            ps = slice(p * 128, (p + 1) * 128)
            for src, w_ref, dst, is_q in ((q_ref, qw_ref, qa_ref, True), (k_ref, kw_ref, ka_ref, False)):
                x = src[:, ps]
                xn = x * lax.rsqrt(_half_mean(x * x, lo_half) + EPS) * w_ref[...]
                if is_q:
                    xn = xn * (F_HD ** -0.5)
                for e in range(2):
                    h = 2 * p + e
                    base = xn if e == 0 else pltpu.roll(xn, F_HD, 1)
                    hi, mid, lo = _split3(cum[:, h:h + 1])
                    pieces = jnp.where(lane == 64, hi, 0.0) + jnp.where(lane == 65, mid, 0.0) + jnp.where(lane == 66, lo, 0.0)
                    if is_q:
                        ext = pieces + jnp.where((lane >= 67) & (lane <= 69), 1.0, 0.0)
                    else:
                        ext = jnp.where((lane >= 64) & (lane <= 66), 1.0, 0.0) - pltpu.roll(pieces, 3, 1)
                    dst[:, h * 128:(h + 1) * 128] = jnp.where(lo_half, base, ext).astype(bf16)
        vb_ref[...] = v_ref[...].astype(bf16)

    t1 = lambda c: pl.BlockSpec((tm, 1024), lambda i: (i, c))
    vec = lambda n: pl.BlockSpec((1, n), lambda i: (0, 0))
    sd = jax.ShapeDtypeStruct
    return pl.pallas_call(
        body, name="fox_pre", grid=(S // tm,),
        in_specs=[t1(0), t1(1), t1(2), pl.BlockSpec((tm, 128), lambda i: (i, F_F0 // 128)), vec(16), vec(128), vec(128)],
        out_specs=[pl.BlockSpec((tm, 2048), lambda i: (i, 0))] * 2 + [pl.BlockSpec((tm, 1024), lambda i: (i, 0))],
        out_shape=[sd((S, 2048), bf16), sd((S, 2048), bf16), sd((S, 1024), bf16)],
        scratch_shapes=[pltpu.VMEM((8, 16), f32)],
        compiler_params=_params(("arbitrary",)),
    )(proj, proj, proj, proj, fbias, qw2, kw2)


FTQ = 512


def fox_attn(qa, ka, v):
    S = qa.shape[0]
    nq = S // FTQ

    def body(q_ref, k_ref, v_ref, o_ref, lse_ref, m_scr, l_scr, acc_scr):
        i, j = pl.program_id(1), pl.program_id(2)

        @pl.when(j == 0)
        def _():
            m_scr[...] = jnp.full_like(m_scr, NEG)
            l_scr[...] = jnp.zeros_like(l_scr)
            acc_scr[...] = jnp.zeros_like(acc_scr)

        @pl.when(j <= i)
        def _():
            keep = (j < i) | (_iota((FTQ, FTQ), 0) >= _iota((FTQ, FTQ), 1))
            for e in range(2):
                es, vs = slice(e * 128, (e + 1) * 128), slice(e * F_HD, (e + 1) * F_HD)
                s = jnp.where(keep, _nt(q_ref[:, es], k_ref[:, es]), NEG)
                m_old = m_scr[e]
                m_new = jnp.maximum(m_old, jnp.max(s, axis=-1, keepdims=True))
                alpha = jnp.exp(m_old - m_new)
                p = jnp.exp(s - m_new[:, 0:1])
                l_scr[e] = alpha * l_scr[e] + jnp.sum(p, axis=-1, keepdims=True)
                acc_scr[e] = acc_scr[e] * alpha[:, 0:F_HD] + _nn(p.astype(bf16), v_ref[:, vs])
                m_scr[e] = m_new

        @pl.when(j == nq - 1)
        def _():
            for e in range(2):
                vs = slice(e * F_HD, (e + 1) * F_HD)
                l = l_scr[e][:, 0:F_HD]
                o_ref[:, vs] = acc_scr[e] / l
                lse_ref[:, vs] = m_scr[e][:, 0:F_HD] + jnp.log(l)

    sd = jax.ShapeDtypeStruct
    qo = pl.BlockSpec((FTQ, 128), lambda p, i, j: (i, p))
    return pl.pallas_call(
        body, name="fox_attn", grid=(F_H // 2, nq, nq),
        in_specs=[pl.BlockSpec((FTQ, 256), lambda p, i, j: (i, p)),
                  pl.BlockSpec((FTQ, 256), lambda p, i, j: (jnp.minimum(j, i), p)),
                  pl.BlockSpec((FTQ, 128), lambda p, i, j: (jnp.minimum(j, i), p))],
        out_specs=[qo, qo],
        out_shape=[sd((S, 1024), f32), sd((S, 1024), f32)],
        scratch_shapes=[pltpu.VMEM((2, FTQ, 128), f32), pltpu.VMEM((2, FTQ, 128), f32), pltpu.VMEM((2, FTQ, F_HD), f32)],
        compiler_params=_params(("parallel", "parallel", "arbitrary")),
    )(qa, ka, v)


def fox_attn_bwd(qa, ka, v, do, lse, delta):
    S = qa.shape[0]
    nq = S // FTQ

    def body(q_ref, k_ref, v_ref, do_ref, lse_ref, dl_ref, dq_ref, dk_ref, dv_ref, dk_scr, dv_scr):
        j, i = pl.program_id(1), pl.program_id(2)

        @pl.when((j == 0) & (i == 0))
        def _():
            dq_ref[...] = jnp.zeros_like(dq_ref)

        @pl.when(i == 0)
        def _():
            dk_scr[...] = jnp.zeros_like(dk_scr)
            dv_scr[...] = jnp.zeros_like(dv_scr)

        @pl.when(i >= j)
        def _():
            keep = (i > j) | (_iota((FTQ, FTQ), 0) >= _iota((FTQ, FTQ), 1))
            rows = pl.ds(pl.multiple_of(i * FTQ, FTQ), FTQ)
            for e in range(2):
                es, vs = slice(e * 128, (e + 1) * 128), slice(e * F_HD, (e + 1) * F_HD)
                qe, ke = q_ref[:, es], k_ref[:, es]
                dob = do_ref[:, vs].astype(bf16)
                s = jnp.where(keep, _nt(qe, ke), NEG)
                p = jnp.exp(s - lse_ref[:, e * F_HD:e * F_HD + 1])
                ds = p * (_nt(dob, v_ref[:, vs]) - dl_ref[:, e * F_HD:e * F_HD + 1])
                dsb = ds.astype(bf16)
                dv_scr[e] += _tn(p.astype(bf16), dob)
                dk_scr[e] += _tn(dsb, qe)
                dq_ref[rows, es] += _nn(dsb, ke)

        @pl.when(i == nq - 1)
        def _():
            for e in range(2):
                dk_ref[:, e * 128:(e + 1) * 128] = dk_scr[e]
                dv_ref[:, e * F_HD:(e + 1) * F_HD] = dv_scr[e]

    sd = jax.ShapeDtypeStruct
    qi = lambda w: pl.BlockSpec((FTQ, w), lambda p, j, i: (jnp.maximum(i, j), p))
    kj = lambda w: pl.BlockSpec((FTQ, w), lambda p, j, i: (j, p))
    return pl.pallas_call(
        body, name="fox_attn_bwd", grid=(F_H // 2, nq, nq),
        in_specs=[qi(256), kj(256), kj(128), qi(128), qi(128), qi(128)],
        out_specs=[pl.BlockSpec((S, 256), lambda p, j, i: (0, p)), kj(256), kj(128)],
        out_shape=[sd((S, 2048), f32), sd((S, 2048), f32), sd((S, 1024), f32)],
        scratch_shapes=[pltpu.VMEM((2, FTQ, 128), f32), pltpu.VMEM((2, FTQ, F_HD), f32)],
        compiler_params=_params(("parallel", "arbitrary", "arbitrary")),
    )(qa, ka, v, do, lse, delta)


def fox_gate(o, proj):
    S = o.shape[0]
    tm = 512

    def body(o_ref, z_ref, o2_ref):
        o2_ref[...] = (o_ref[...] * _silu(z_ref[...])).astype(bf16)

    t = pl.BlockSpec((tm, 1024), lambda i: (i, 0))
    return pl.pallas_call(
        body, name="fox_gate", grid=(S // tm,),
        in_specs=[t, pl.BlockSpec((tm, 1024), lambda i: (i, 3))], out_specs=t,
        out_shape=jax.ShapeDtypeStruct((S, 1024), bf16),
        compiler_params=_params(("parallel",)),
    )(o, proj)


def fox_gate_bwd(do2, o, proj):
    S = o.shape[0]
    tm = 256

    def body(d_ref, o_ref, z_ref, do_ref, dz_ref, dl_ref):
        lo_half = _iota((tm, 128), 1) < F_HD
        for p in range(F_H // 2):
            ps = slice(p * 128, (p + 1) * 128)
            d2, ov, z = d_ref[:, ps], o_ref[:, ps], z_ref[:, ps]
            dov = d2 * _silu(z)
            do_ref[:, ps] = dov
            dz_ref[:, ps] = (d2 * ov * _dsilu(z)).astype(bf16)
            dl_ref[:, ps] = _half_mean(dov * ov, lo_half) * float(F_HD)

    t = pl.BlockSpec((tm, 1024), lambda i: (i, 0))
    sd = jax.ShapeDtypeStruct
    return pl.pallas_call(
        body, name="fox_gate_bwd", grid=(S // tm,),
        in_specs=[t, t, pl.BlockSpec((tm, 1024), lambda i: (i, 3))], out_specs=[t, t, t],
        out_shape=[sd((S, 1024), f32), sd((S, 1024), bf16), sd((S, 1024), f32)],
        compiler_params=_params(("parallel",)),
    )(do2, o, proj)


def fox_pre_bwd(proj, fbias, qw2, kw2, dqa, dka, dv, dz):
    S = proj.shape[0]
    tm = 256
    nb = S // tm

    def body(q_ref, k_ref, f_ref, fb_ref, qw_ref, kw_ref, dqa_ref, dka_ref, dv_ref, dz_ref, dp_ref, st_ref, carry):
        i = pl.program_id(0)

        @pl.when(i == 0)
        def _():
            carry[...] = jnp.zeros_like(carry)

        lane = _iota((tm, 128), 1)
        lo_half = lane < F_HD
        lane16 = _iota((tm, 16), 1)
        dcum = jnp.zeros((tm, 16), f32)
        dws = []
        for src, w_ref, dsrc, is_q, col0 in ((q_ref, qw_ref, dqa_ref, True, 0), (k_ref, kw_ref, dka_ref, False, 1024)):
            dw = jnp.zeros((1, 128), f32)
            for p in range(F_H // 2):
                ps = slice(p * 128, (p + 1) * 128)
                x = src[:, ps]
                r = lax.rsqrt(_half_mean(x * x, lo_half) + EPS)
                xh = x * r
                d0 = dsrc[:, (2 * p) * 128:(2 * p + 1) * 128]
                d1 = dsrc[:, (2 * p + 1) * 128:(2 * p + 2) * 128]
                dy = jnp.where(lo_half, d0, pltpu.roll(d1, F_HD, 1))
                if is_q:
                    dy = dy * (F_HD ** -0.5)
                dxh = dy * w_ref[...]
                dw = dw + jnp.sum(dy * xh, axis=0, keepdims=True)
                dp_ref[:, col0 + p * 128:col0 + (p + 1) * 128] = (r * (dxh - xh * _half_mean(dxh * xh, lo_half))).astype(bf16)
                for e, de in ((0, d0), (1, d1)):
                    col = de[:, 64:65] if is_q else -de[:, 67:68]
                    dcum = dcum + jnp.where(lane16 == 2 * p + e, col, 0.0)
            dws.append(dw)
        dp_ref[:, 2048:3072] = dv_ref[...].astype(bf16)
        dp_ref[:, 3072:4096] = dz_ref[...]
        utri = jnp.where(_iota((tm, tm), 1) >= _iota((tm, tm), 0), 1.0, 0.0).astype(f32)
        dlogf = _nn(utri, dcum, HI) + carry[0:1, :]
        carry[0:1, :] = dlogf[0:1, :]
        fl = f_ref[:, 0:16] + fb_ref[...]
        df = dlogf * _sigmoid(-fl)
        place = jnp.where(_iota((16, 128), 1) == _iota((16, 128), 0), 1.0, 0.0).astype(f32)
        dfw = _nn(df, place, HI)
        dp_ref[:, F_F0:F_INP] = dfw.astype(bf16)
        upd = jnp.concatenate(dws + [jnp.sum(dfw, axis=0, keepdims=True), jnp.zeros((5, 128), f32)], axis=0)

        @pl.when(i == 0)
        def _():
            st_ref[...] = upd

        @pl.when(i > 0)
        def _():
            st_ref[...] += upd

    rev = lambda w, c: pl.BlockSpec((tm, w), lambda i: (nb - 1 - i, c))
    vec = lambda n: pl.BlockSpec((1, n), lambda i: (0, 0))
    sd = jax.ShapeDtypeStruct
    return pl.pallas_call(
        body, name="fox_pre_bwd", grid=(nb,),
        in_specs=[rev(1024, 0), rev(1024, 1), rev(128, F_F0 // 128), vec(16), vec(128), vec(128),
                  rev(2048, 0), rev(2048, 0), rev(1024, 0), rev(1024, 0)],
        out_specs=[rev(F_INP, 0), pl.BlockSpec((8, 128), lambda i: (0, 0))],
        out_shape=[sd((S, F_INP), bf16), sd((8, 128), f32)],
        scratch_shapes=[pltpu.VMEM((8, 16), f32)],
        compiler_params=_params(("arbitrary",)),
    )(proj, proj, proj, fbias, qw2, kw2, dqa, dka, dv, dz)


def _me():
    return lax.axis_index("x"), lax.axis_index("y"), lax.axis_index("c")


def _other_chips(x, y):
    return [(1 - x, y), (x, 1 - y), (1 - x, 1 - y)]


def ag_small(xs):
    m_per, n = xs.shape

    def body(x_ref, out_ref, send_sems, recv_sems, local_sem):
        x, y, c = _me()
        me, sibling = (x, y, c), (x, y, 1 - c)
        chips = _other_chips(x, y)

        def rows(px, py, pc):
            return out_ref.at[pl.ds((4 * px + 2 * py + pc) * m_per, m_per), :]

        def copy(k, block, to, src=None):
            return pltpu.make_async_remote_copy(
                src_ref=rows(*block) if src is None else src, dst_ref=rows(*block),
                send_sem=send_sems.at[k], recv_sem=recv_sems.at[k], device_id=to, device_id_type=MESH)

        mine = pltpu.make_async_copy(x_ref, rows(*me), local_sem)
        mine.start()
        first = [copy(0, me, sibling, src=x_ref)]
        first += [copy(1 + j, me, (*chip, c), src=x_ref) for j, chip in enumerate(chips)]
        for cp in first:
            cp.start()
        passed = [copy(4 + j, (*chip, c), sibling) for j, chip in enumerate(chips)]
        for j, chip in enumerate(chips):
            copy(1 + j, (*chip, c), me).wait_recv()
            passed[j].start()
        copy(0, sibling, me).wait_recv()
        for j, chip in enumerate(chips):
            copy(4 + j, (*chip, 1 - c), me).wait_recv()
        for cp in first + passed:
            cp.wait_send()
        mine.wait()

    return pl.pallas_call(
        body, name="ag_small",
        out_shape=jax.ShapeDtypeStruct((8 * m_per, n), xs.dtype),
        in_specs=[pl.BlockSpec(memory_space=pltpu.VMEM)], out_specs=pl.BlockSpec(memory_space=pltpu.VMEM),
        scratch_shapes=[pltpu.SemaphoreType.DMA((7,)), pltpu.SemaphoreType.DMA((7,)), pltpu.SemaphoreType.DMA],
        compiler_params=pltpu.CompilerParams(vmem_limit_bytes=VMEM_LIMIT),
    )(xs)


_ANY = pl.BlockSpec(memory_space=pl.ANY)


def ag_chips(arrs):
    n = len(arrs)
    assert all(a.shape[0] == 2 for a in arrs)

    def body(*refs):
        ins, outs = refs[:n], refs[n:2 * n]
        send_sems, recv_sems, fwd_send, fwd_recv, local_sems = refs[2 * n:]
        x, y, c = _me()
        me = 2 * x + y
        chips = _other_chips(x, y)
        started = []
        for a in range(n):
            cp = pltpu.make_async_copy(ins[a], outs[a].at[me], local_sems.at[a])
            cp.start()
            started.append(cp)
        sends = []
        for a in range(n):
            for j, (px, py) in enumerate(chips):
                r = pltpu.make_async_remote_copy(
                    src_ref=ins[a].at[c], dst_ref=outs[a].at[me, c], send_sem=send_sems.at[3 * a + j],
                    recv_sem=recv_sems.at[3 * a + j], device_id=(px, py, c), device_id_type=MESH)
                r.start()
                sends.append(r)
        for a in range(n):
            for j, (px, py) in enumerate(chips):
                got = outs[a].at[2 * px + py, c]
                pltpu.make_async_remote_copy(
                    src_ref=ins[a].at[c], dst_ref=got, send_sem=send_sems.at[3 * a + j],
                    recv_sem=recv_sems.at[3 * a + j], device_id=(px, py, c), device_id_type=MESH).wait_recv()
                f = pltpu.make_async_remote_copy(
                    src_ref=got, dst_ref=got, send_sem=fwd_send.at[3 * a + j], recv_sem=fwd_recv.at[3 * a + j],
                    device_id=(x, y, 1 - c), device_id_type=MESH)
                f.start()
                sends.append(f)
        for a in range(n):
            for j, (px, py) in enumerate(chips):
                theirs = outs[a].at[2 * px + py, 1 - c]
                pltpu.make_async_remote_copy(
                    src_ref=theirs, dst_ref=theirs, send_sem=fwd_send.at[3 * a + j], recv_sem=fwd_recv.at[3 * a + j],
                    device_id=(x, y, 1 - c), device_id_type=MESH).wait_recv()
        for r in sends:
            r.wait_send()
        for cp in started:
            cp.wait()

    sems = pltpu.SemaphoreType.DMA((3 * n,))
    return pl.pallas_call(
        body, name="ag_chips",
        out_shape=[jax.ShapeDtypeStruct((4,) + a.shape, a.dtype) for a in arrs],
        in_specs=[_ANY] * n, out_specs=[_ANY] * n,
        scratch_shapes=[sems, sems, sems, sems, pltpu.SemaphoreType.DMA((n,))],
    )(*arrs)


def rs_swap_halves(gs):
    n = len(gs)

    def body(*refs):
        ins, outs = refs[:n], refs[n:2 * n]
        send_sems, recv_sems = refs[2 * n:]
        x, y, c = _me()
        cps = []
        for a in range(n):
            rh = ins[a].shape[1] // 2
            cp = pltpu.make_async_remote_copy(
                src_ref=ins[a].at[:, pl.ds((1 - c) * rh, rh), :], dst_ref=outs[a],
                send_sem=send_sems.at[a], recv_sem=recv_sems.at[a], device_id=(x, y, 1 - c), device_id_type=MESH)
            cp.start()
            cps.append(cp)
        for cp in cps:
            cp.wait()

    return pl.pallas_call(
        body, name="rs_swap_halves",
        out_shape=[jax.ShapeDtypeStruct((4, g.shape[1] // 2, g.shape[2]), g.dtype) for g in gs],
        in_specs=[_ANY] * n, out_specs=[_ANY] * n,
        scratch_shapes=[pltpu.SemaphoreType.DMA((n,)), pltpu.SemaphoreType.DMA((n,))],
    )(*gs)


def rs_add_halves(g, r, cidx):
    _, rh, C = r.shape
    tr = _pick(rh, (256, 128))
    nrb = rh // tr

    def body(c_ref, g_ref, r_ref, o_ref):
        o_ref[...] = (g_ref[...] + r_ref[...]).astype(bf16)

    return pl.pallas_call(
        body, name="rs_add_halves",
        grid_spec=pltpu.PrefetchScalarGridSpec(
            num_scalar_prefetch=1, grid=(4, nrb),
            in_specs=[pl.BlockSpec((1, tr, C), lambda k, i, c_ref: (k, c_ref[0] * nrb + i, 0)),
                      pl.BlockSpec((1, tr, C), lambda k, i, c_ref: (k, i, 0))],
            out_specs=pl.BlockSpec((1, tr, C), lambda k, i, c_ref: (k, i, 0))),
        out_shape=jax.ShapeDtypeStruct(r.shape, bf16),
        compiler_params=_params(("parallel", "parallel")),
    )(cidx, g, r)


def rs_exchange_chips(ps):
    n = len(ps)

    def body(*refs):
        ins, outs = refs[:n], refs[n:2 * n]
        send_sems, recv_sems, local_sems = refs[2 * n:]
        x, y, c = _me()
        me = 2 * x + y
        chips = _other_chips(x, y)
        started = []
        for a in range(n):
            cp = pltpu.make_async_copy(ins[a].at[me], outs[a].at[me], local_sems.at[a])
            cp.start()
            started.append(cp)
        sends = []
        for a in range(n):
            for j, (px, py) in enumerate(chips):
                r = pltpu.make_async_remote_copy(
                    src_ref=ins[a].at[2 * px + py], dst_ref=outs[a].at[me], send_sem=send_sems.at[3 * a + j],
                    recv_sem=recv_sems.at[3 * a + j], device_id=(px, py, c), device_id_type=MESH)
                r.start()
                sends.append(r)
        for a in range(n):
            for j, (px, py) in enumerate(chips):
                pltpu.make_async_remote_copy(
                    src_ref=ins[a].at[me], dst_ref=outs[a].at[2 * px + py], send_sem=send_sems.at[3 * a + j],
                    recv_sem=recv_sems.at[3 * a + j], device_id=(px, py, c), device_id_type=MESH).wait_recv()
        for r in sends:
            r.wait_send()
        for cp in started:
            cp.wait()

    return pl.pallas_call(
        body, name="rs_exchange_chips",
        out_shape=[jax.ShapeDtypeStruct(p.shape, p.dtype) for p in ps],
        in_specs=[_ANY] * n, out_specs=[_ANY] * n,
        scratch_shapes=[pltpu.SemaphoreType.DMA((3 * n,)), pltpu.SemaphoreType.DMA((3 * n,)), pltpu.SemaphoreType.DMA((n,))],
    )(*ps)


def sum_leading(q, name):
    K, R, C = q.shape
    tr = _pick(R, (256, 128, 64, 32, 16, 8))

    def body(q_ref, o_ref):
        acc = q_ref[0]
        for k in range(1, K):
            acc = acc + q_ref[k]
        o_ref[...] = acc

    return pl.pallas_call(
        body, name=name, grid=(R // tr,),
        in_specs=[pl.BlockSpec((K, tr, C), lambda i: (0, i, 0))], out_specs=pl.BlockSpec((tr, C), lambda i: (i, 0)),
        out_shape=jax.ShapeDtypeStruct((R, C), f32),
        compiler_params=_params(("parallel",)),
    )(q)


def rs_sum_chips(q, cidx):
    K, R, C = q.shape
    tr = _pick(R, (256, 128))

    def body(c_ref, q_ref, o_ref):
        acc = q_ref[0].astype(f32)
        for k in range(1, K):
            acc = acc + q_ref[k].astype(f32)
        o_ref[0] = acc

    return pl.pallas_call(
        body, name="rs_sum_chips",
        grid_spec=pltpu.PrefetchScalarGridSpec(
            num_scalar_prefetch=1, grid=(R // tr,),
            in_specs=[pl.BlockSpec((K, tr, C), lambda i, c_ref: (0, i, 0))],
            out_specs=pl.BlockSpec((1, tr, C), lambda i, c_ref: (c_ref[0], i, 0))),
        out_shape=jax.ShapeDtypeStruct((2, R, C), f32),
        compiler_params=_params(("parallel",)),
    )(cidx, q)


def rs_share_halves(rs):
    n = len(rs)

    def body(*refs):
        bufs = refs[n:2 * n]
        send_sems, recv_sems = refs[2 * n:]
        x, y, c = _me()
        cps = []
        for a in range(n):
            cp = pltpu.make_async_remote_copy(
                src_ref=bufs[a].at[c], dst_ref=bufs[a].at[c], send_sem=send_sems.at[a], recv_sem=recv_sems.at[a],
                device_id=(x, y, 1 - c), device_id_type=MESH)
            cp.start()
            cps.append(cp)
        for a, cp in enumerate(cps):
            pltpu.make_async_remote_copy(
                src_ref=bufs[a].at[c], dst_ref=bufs[a].at[1 - c], send_sem=send_sems.at[a], recv_sem=recv_sems.at[a],
                device_id=(x, y, 1 - c), device_id_type=MESH).wait_recv()
            cp.wait_send()

    return pl.pallas_call(
        body, name="rs_share_halves",
        out_shape=[jax.ShapeDtypeStruct(r.shape, r.dtype) for r in rs],
        in_specs=[_ANY] * n, out_specs=[_ANY] * n, input_output_aliases={a: a for a in range(n)},
        scratch_shapes=[pltpu.SemaphoreType.DMA((n,)), pltpu.SemaphoreType.DMA((n,))],
    )(*rs)


def reduce_scatter(gs, cidx):
    got = rs_swap_halves(gs)
    ps = [rs_add_halves(g, r, cidx) for g, r in zip(gs, got)]
    qs = rs_exchange_chips(ps)
    fs = rs_share_halves([rs_sum_chips(q, cidx) for q in qs])
    return [f.reshape(-1, f.shape[-1]) for f in fs]


def ada_mod(c_all, ada_w):
    L, _, n = ada_w.shape

    def body(c_ref, w_ref, o_ref):
        o_ref[0] = _nn(_silu(c_ref[...]), w_ref[0], HI)

    return pl.pallas_call(
        body, name="ada_mod", grid=(L,),
        in_specs=[pl.BlockSpec((8, D), lambda l: (0, 0)), pl.BlockSpec((1, D, n), lambda l: (l, 0, 0))],
        out_specs=pl.BlockSpec((1, 8, n), lambda l: (l, 0, 0)),
        out_shape=jax.ShapeDtypeStruct((L, 8, n), f32),
        compiler_params=_params(("parallel",)),
    )(c_all, ada_w)


def ada_w_grad(c_all, dmod):
    L, _, n = dmod.shape

    def body(c_ref, d_ref, o_ref):
        o_ref[0] = _tn(_silu(c_ref[...]), d_ref[0], HI)

    return pl.pallas_call(
        body, name="ada_w_grad", grid=(L,),
        in_specs=[pl.BlockSpec((8, D), lambda l: (0, 0)), pl.BlockSpec((1, 8, n), lambda l: (l, 0, 0))],
        out_specs=pl.BlockSpec((1, D, n), lambda l: (l, 0, 0)),
        out_shape=jax.ShapeDtypeStruct((L, D, n), f32),
        compiler_params=_params(("parallel",)),
    )(c_all, dmod)


def adamw(w, g, m, v, name):
    shp = w.shape
    two = lambda a: a.reshape(-1, shp[-1])
    R, C = two(w).shape
    tr = _pick(R, (256, 128, 64, 32, 16, 8))
    bc1, bc2 = 1.0 - B1 ** STEP, 1.0 - B2 ** STEP

    def body(w_ref, g_ref, m_ref, v_ref, d_ref, mo_ref, vo_ref):
        gv = g_ref[...]
        mn = B1 * m_ref[...] + (1.0 - B1) * gv
        vn = B2 * v_ref[...] + (1.0 - B2) * (gv * gv)
        d_ref[...] = -LR * ((mn / bc1) / (jnp.sqrt(vn / bc2) + AEPS) + WD * w_ref[...])
        mo_ref[...] = mn
        vo_ref[...] = vn

    t = pl.BlockSpec((tr, C), lambda i: (i, 0))
    outs = pl.pallas_call(
        body, name=name, grid=(R // tr,),
        in_specs=[t] * 4, out_specs=[t] * 3, out_shape=[jax.ShapeDtypeStruct((R, C), f32)] * 3,
        compiler_params=_params(("parallel",)),
    )(two(w), two(g), two(m), two(v))
    return [o.reshape(shp) for o in outs]


def _pack(arrs):
    parts, offs, r0 = [], [], 0
    for a in arrs:
        n = a.size
        rows = -(-n // 1024) * 8
        parts.append(jnp.pad(a.reshape(-1), (0, rows * 128 - n)).reshape(rows, 128))
        offs.append((r0, rows))
        r0 += rows
    return jnp.concatenate(parts, axis=0), offs


def _unpack(buf, offs, shapes):
    out = []
    for (r0, rows), shp in zip(offs, shapes):
        n = 1
        for d in shp:
            n *= d
        out.append(buf[..., r0:r0 + rows, :].reshape(buf.shape[:-2] + (rows * 128,))[..., :n].reshape(buf.shape[:-2] + tuple(shp)))
    return out


def kernel(x, c, norm_w, ada_w, ada_b, a_w_in, a_conv_w, a_A_log, a_dt_bias, a_norm_w, a_w_out, b_w_in, b_f_bias, b_qn_w, b_kn_w, b_w_out, final_norm_w, loss_target, m_norm_w, m_ada_w, m_ada_b, m_a_w_in, m_a_conv_w, m_a_A_log, m_a_dt_bias, m_a_norm_w, m_a_w_out, m_b_w_in, m_b_f_bias, m_b_qn_w, m_b_kn_w, m_b_w_out, m_final_norm_w, v_norm_w, v_ada_w, v_ada_b, v_a_w_in, v_a_conv_w, v_a_A_log, v_a_dt_bias, v_a_norm_w, v_a_w_out, v_b_w_in, v_b_f_bias, v_b_qn_w, v_b_kn_w, v_b_w_out, v_final_norm_w):
    weights = dict(norm_w=norm_w, ada_w=ada_w, ada_b=ada_b, a_w_in=a_w_in, a_conv_w=a_conv_w, a_A_log=a_A_log,
                   a_dt_bias=a_dt_bias, a_norm_w=a_norm_w, a_w_out=a_w_out, b_w_in=b_w_in, b_f_bias=b_f_bias,
                   b_qn_w=b_qn_w, b_kn_w=b_kn_w, b_w_out=b_w_out, final_norm_w=final_norm_w)
    m_in = dict(norm_w=m_norm_w, ada_w=m_ada_w, ada_b=m_ada_b, a_w_in=m_a_w_in, a_conv_w=m_a_conv_w, a_A_log=m_a_A_log,
                a_dt_bias=m_a_dt_bias, a_norm_w=m_a_norm_w, a_w_out=m_a_w_out, b_w_in=m_b_w_in, b_f_bias=m_b_f_bias,
                b_qn_w=m_b_qn_w, b_kn_w=m_b_kn_w, b_w_out=m_b_w_out, final_norm_w=m_final_norm_w)
    v_in = dict(norm_w=v_norm_w, ada_w=v_ada_w, ada_b=v_ada_b, a_w_in=v_a_w_in, a_conv_w=v_a_conv_w, a_A_log=v_a_A_log,
                a_dt_bias=v_a_dt_bias, a_norm_w=v_a_norm_w, a_w_out=v_a_w_out, b_w_in=v_b_w_in, b_f_bias=v_b_f_bias,
                b_qn_w=v_b_qn_w, b_kn_w=v_b_kn_w, b_w_out=v_b_w_out, final_norm_w=v_final_norm_w)
    xi, yi, ci = _me()
    me_b, me_k = 4 * xi + 2 * yi + ci, 2 * xi + yi
    cidx = ci.astype(jnp.int32).reshape(1)
    S = x.shape[1]
    depth, n_a, n_b = norm_w.shape[0], a_w_in.shape[0], b_w_in.shape[0]
    x0, tgt = x.reshape(S, D), loss_target.reshape(S, D)

    c_all = ag_small(jnp.pad(c, ((0, 7), (0, 0)))).reshape(8, 8, D)[:, 0]
    nloc = ada_w.shape[2]
    parts = ag_small(ada_mod(c_all, ada_w).reshape(depth * 8, nloc)).reshape(4, 2, depth, 8, nloc)[:, 0]
    mine = lax.dynamic_index_in_dim(parts, me_b, axis=2, keepdims=False)
    mod = jnp.transpose(mine, (1, 0, 2)).reshape(depth, 4 * nloc) + ada_b
    shift, scale, gate = (mod[:, k * D:(k + 1) * D] for k in range(3))

    g_ain, g_aout, g_bin, g_bout, g_conv = ag_chips(
        [a_w_in.astype(bf16), a_w_out.astype(bf16), b_w_in.astype(bf16), b_w_out.astype(bf16), a_conv_w])

    def cols(g, l, pad):
        w = jnp.transpose(g[:, l], (1, 0, 2)).reshape(g.shape[2], -1)
        return jnp.pad(w, ((0, 0), (0, pad)))

    w_ain = [cols(g_ain, l, G_INP - G_IN) for l in range(n_a)]
    w_bin = [cols(g_bin, l, F_INP - F_IN) for l in range(n_b)]
    w_aout = [g_aout[:, l].reshape(-1, D) for l in range(n_a)]
    w_bout = [g_bout[:, l].reshape(-1, D) for l in range(n_b)]
    conv = [cols(g_conv, l, 0) for l in range(n_a)]
    qw2 = [_row(jnp.tile(b_qn_w[l], 2)) for l in range(n_b)]
    kw2 = [_row(jnp.tile(b_kn_w[l], 2)) for l in range(n_b)]

    saved, xc = [], x0
    for i in range(depth):
        l = i // 2
        h = ln_mod(xc, _row(norm_w[i]), _row(scale[i]), _row(shift[i]))
        if i % 2 == 0:
            proj = matmul(h, w_ain[l], "nn", "mm_a_in")
            pre = gdn_pre(proj, conv[l], _row(a_A_log[l]), _row(a_dt_bias[l]))
            o, wv, at, tinv, vn, st = gdn_fwd(*pre)
            o2 = gdn_onorm(o, proj, _row(a_norm_w[l]))
            y, xn = out_proj(o2, w_aout[l], xc, _row(gate[i]), "out_proj_a")
            saved.append((xc, h, proj, o2, y, pre, (o, wv, at, tinv, vn, st)))
        else:
            proj = matmul(h, w_bin[l], "nn", "mm_b_in")
            qa, ka, vb = fox_pre(proj, _row(b_f_bias[l]), qw2[l], kw2[l])
            o, lse = fox_attn(qa, ka, vb)
            o2 = fox_gate(o, proj)
            y, xn = out_proj(o2, w_bout[l], xc, _row(gate[i]), "out_proj_b")
            saved.append((xc, h, proj, o2, y, (qa, ka, vb), (o, lse)))
        xc = xn
    dx, st_f = final_loss(xc, _row(final_norm_w), tgt)

    d_norm, d_mod = [None] * depth, [None] * depth
    d_ain, d_aout, d_bin, d_bout = [None] * n_a, [None] * n_a, [None] * n_b, [None] * n_b
    d_conv, d_alog, d_dtb, d_anw = [None] * n_a, [None] * n_a, [None] * n_a, [None] * n_a
    d_fb, d_qn, d_kn = [None] * n_b, [None] * n_b, [None] * n_b
    for i in reversed(range(depth)):
        l = i // 2
        xin, h, proj, o2, y, pre, res = saved[i]
        dy, st_g = gate_bwd(dx, y, _row(gate[i]))
        if i % 2 == 0:
            o, wv, at, tinv, vn, st = res
            do2 = matmul(dy, w_aout[l], "nt", "mm_a_do2")
            d_aout[l] = matmul(o2, dy, "tn", "mm_a_dwo")
            do, dz, st_o = gdn_onorm_bwd(do2, o, proj, _row(a_norm_w[l]))
            grads = gdn_bwd(do, *pre, wv, at, tinv, vn, st)
            dcv, dba, st_s = gdn_pre_bwd(proj, conv[l], _row(a_A_log[l]), _row(a_dt_bias[l]), *grads)
            dproj, dcw = gdn_conv_bwd(proj, conv[l], dcv, dz, dba)
            dh = matmul(dproj, w_ain[l], "nt", "mm_a_dh")
            d_ain[l] = matmul(h, dproj, "tn", "mm_a_dw")
            d_conv[l], d_alog[l], d_dtb[l], d_anw[l] = dcw[:CONV_K], st_s[0], st_s[1], st_o[0]
        else:
            qa, ka, vb = pre
            o, lse = res
            do2 = matmul(dy, w_bout[l], "nt", "mm_b_do2")
            d_bout[l] = matmul(o2, dy, "tn", "mm_b_dwo")
            do, dz, delta = fox_gate_bwd(do2, o, proj)
            dqa, dka, dv = fox_attn_bwd(qa, ka, vb, do, lse, delta)
            dproj, st_b = fox_pre_bwd(proj, _row(b_f_bias[l]), qw2[l], kw2[l], dqa, dka, dv, dz)
            dh = matmul(dproj, w_bin[l], "nt", "mm_b_dh")
            d_bin[l] = matmul(h, dproj, "tn", "mm_b_dw")
            d_fb[l], d_qn[l], d_kn[l] = st_b[2, :F_H], st_b[0, :F_HD] + st_b[0, F_HD:], st_b[1, :F_HD] + st_b[1, F_HD:]
        dx, st_n = ln_mod_bwd(xin, _row(norm_w[i]), _row(scale[i]), dh, dx)
        d_norm[i] = st_n[0]
        d_mod[i] = jnp.concatenate([st_n[2], st_n[1], st_g[0]])

    small = [jnp.stack(d_norm), jnp.stack(d_mod), jnp.stack(d_conv), jnp.stack(d_alog), jnp.stack(d_dtb), jnp.stack(d_anw),
             jnp.stack(d_fb), jnp.stack(d_qn), jnp.stack(d_kn), st_f[0], jnp.sum(st_f[1]).reshape(1)]
    shapes = [a.shape for a in small]
    buf, offs = _pack(small)
    gathered = ag_small(buf).reshape(8, buf.shape[0], 128)
    tot = _unpack(sum_leading(gathered, "sum_devices"), offs, shapes)
    g_norm, g_adab, g_convf, g_alog, g_dtb, g_anw, g_fb, g_qn, g_kn, g_fin, loss = tot
    dmod_all = _unpack(gathered, offs[1:2], shapes[1:2])[0]
    dmod_loc = lax.dynamic_slice_in_dim(dmod_all, me_k * nloc, nloc, axis=2)
    g_adaw = ada_w_grad(c_all, jnp.transpose(dmod_loc, (1, 0, 2)))
    g_conv_loc = lax.dynamic_slice_in_dim(g_convf, me_k * a_conv_w.shape[2], a_conv_w.shape[2], axis=2)

    def col_blocks(ds, width):
        per = [jnp.transpose(d[:, :width].reshape(d.shape[0], 4, width // 4), (1, 0, 2)) for d in ds]
        return jnp.concatenate(per, axis=1)

    def row_blocks(ds):
        per = [d.reshape(4, d.shape[0] // 4, D) for d in ds]
        return jnp.concatenate(per, axis=1)

    r_ain, r_aout, r_bin, r_bout = reduce_scatter(
        [col_blocks(d_ain, G_IN), row_blocks(d_aout), col_blocks(d_bin, F_IN), row_blocks(d_bout)], cidx)

    grads = dict(norm_w=g_norm, ada_w=g_adaw, ada_b=g_adab, a_w_in=r_ain.reshape(a_w_in.shape), a_conv_w=g_conv_loc,
                 a_A_log=g_alog, a_dt_bias=g_dtb, a_norm_w=g_anw, a_w_out=r_aout.reshape(a_w_out.shape),
                 b_w_in=r_bin.reshape(b_w_in.shape), b_f_bias=g_fb, b_qn_w=g_qn, b_kn_w=g_kn,
                 b_w_out=r_bout.reshape(b_w_out.shape), final_norm_w=g_fin)
    names = list(weights)
    upd = {n: adamw(weights[n], grads[n], m_in[n], v_in[n], "adamw_" + n) for n in names}
    return (loss.reshape(()), dx.reshape(x.shape), *[grads[n] for n in names], *[upd[n][0] for n in names],
            *[upd[n][1] for n in names], *[upd[n][2] for n in names])
```

```python
import functools

import jax
import jax.numpy as jnp
from jax import lax
from jax.experimental import pallas as pl
from jax.experimental.pallas import tpu as pltpu

f32, bf16 = jnp.float32, jnp.bfloat16
HI = lax.Precision.HIGHEST
MESH = pl.DeviceIdType.MESH

EPS = 1e-6
D = 1024
CHUNK = 64
GQK_H, GV_H, GHD = 8, 16, 128
G_CONV = 4096
G_Z0 = 4096
G_BA0 = 6144
G_IN, G_INP = 6176, 6272
CONV_K = 4
F_H, F_HD = 16, 64
F_W = 1024
F_F0 = 4096
F_IN, F_INP = 4112, 4224
LR, B1, B2, AEPS, WD, STEP = 0.001, 0.9, 0.999, 1e-08, 0.01, 10
NEG = -1e30
VMEM_LIMIT = 56 * 1024 * 1024


def _nn(a, b, prec=None):
    return lax.dot_general(a, b, (((1,), (0,)), ((), ())), preferred_element_type=f32, precision=prec)


def _nt(a, b, prec=None):
    return lax.dot_general(a, b, (((1,), (1,)), ((), ())), preferred_element_type=f32, precision=prec)


def _tn(a, b, prec=None):
    return lax.dot_general(a, b, (((0,), (0,)), ((), ())), preferred_element_type=f32, precision=prec)


def _iota(shape, axis):
    return lax.broadcasted_iota(jnp.int32, shape, axis)


def _sigmoid(x):
    return 0.5 * jnp.tanh(0.5 * x) + 0.5


def _softplus(x):
    return jnp.maximum(x, 0.0) + jnp.log(1.0 + jnp.exp(-jnp.abs(x)))


def _silu(x):
    return x * _sigmoid(x)


def _dsilu(x):
    s = _sigmoid(x)
    return s * (1.0 + x * (1.0 - s))


def _params(sem=None, vmem=VMEM_LIMIT):
    return pltpu.CompilerParams(dimension_semantics=sem, vmem_limit_bytes=vmem)


def _row(v):
    return v.reshape(1, -1)


def _pick(n, pref):
    for t in pref:
        if n % t == 0:
            return t
    return n


MM_VMEM_BUDGET = 44 * 1024 * 1024


def _mm_tiles(M, N, K):
    best = None
    for tk in [K] + [t for t in (2048, 1408, 1024, 896, 512, 384, 256, 128) if K % t == 0 and t < K]:
        for tm in (2048, 1024, 512, 256, 128):
            for tn in (1408, 1024, 896, 512, 384, 256, 128):
                if M % tm or N % tn:
                    continue
                nk = K // tk
                need = 2 * 2 * (tm * tk + tk * tn) + 2 * 4 * tm * tn + (4 * tm * tn if nk > 1 else 0)
                if need <= MM_VMEM_BUDGET:
                    cand = ((nk, -tm * tn), (tm, tn, tk))
                    best = cand if best is None or cand[0] < best[0] else best
    return best[1]


def matmul(a, b, mode, name, out_dtype=f32):
    if mode == "nn":
        (M, K), (_, N) = a.shape, b.shape
    elif mode == "nt":
        (M, K), (N, _) = a.shape, b.shape
    else:
        (K, M), (_, N) = a.shape, b.shape
    tm, tn, tk = _mm_tiles(M, N, K)
    nk = K // tk
    dot = {"nn": _nn, "nt": _nt, "tn": _tn}[mode]

    def body(a_ref, b_ref, o_ref, *acc):
        k = pl.program_id(2)
        part = dot(a_ref[...], b_ref[...])
        if nk == 1:
            o_ref[...] = part.astype(out_dtype)
        else:
            acc_ref = acc[0]

            @pl.when(k == 0)
            def _():
                acc_ref[...] = part

            @pl.when(k > 0)
            def _():
                acc_ref[...] += part

            @pl.when(k == nk - 1)
            def _():
                o_ref[...] = acc_ref[...].astype(out_dtype)

    a_spec = pl.BlockSpec((tk, tm), lambda i, j, k: (k, i)) if mode == "tn" else pl.BlockSpec((tm, tk), lambda i, j, k: (i, k))
    b_spec = pl.BlockSpec((tn, tk), lambda i, j, k: (j, k)) if mode == "nt" else pl.BlockSpec((tk, tn), lambda i, j, k: (k, j))
    return pl.pallas_call(
        body, name=name, grid=(M // tm, N // tn, nk),
        in_specs=[a_spec, b_spec], out_specs=pl.BlockSpec((tm, tn), lambda i, j, k: (i, j)),
        out_shape=jax.ShapeDtypeStruct((M, N), out_dtype),
        scratch_shapes=[] if nk == 1 else [pltpu.VMEM((tm, tn), f32)],
        compiler_params=_params(("parallel", "parallel", "arbitrary")),
    )(a, b)


def out_proj(o2, w, x, gate, name):
    S, K = o2.shape
    N = w.shape[1]
    tm, tn = 512, 512

    def body(a_ref, b_ref, x_ref, g_ref, y_ref, xn_ref):
        y = _nn(a_ref[...], b_ref[...])
        y_ref[...] = y
        xn_ref[...] = x_ref[...] + g_ref[...] * y

    return pl.pallas_call(
        body, name=name, grid=(S // tm, N // tn),
        in_specs=[pl.BlockSpec((tm, K), lambda i, j: (i, 0)), pl.BlockSpec((K, tn), lambda i, j: (0, j)),
                  pl.BlockSpec((tm, tn), lambda i, j: (i, j)), pl.BlockSpec((1, tn), lambda i, j: (0, j))],
        out_specs=[pl.BlockSpec((tm, tn), lambda i, j: (i, j))] * 2,
        out_shape=[jax.ShapeDtypeStruct((S, N), f32)] * 2,
        compiler_params=_params(("parallel", "parallel")),
    )(o2, w, x, gate)


def ln_mod(x, nw, scale, shift):
    S = x.shape[0]
    tm = 512

    def body(x_ref, nw_ref, sc_ref, sh_ref, h_ref):
        xv = x_ref[...]
        r = lax.rsqrt(jnp.mean(xv * xv, axis=-1, keepdims=True) + EPS)
        h_ref[...] = ((xv * r) * nw_ref[...] * (1.0 + sc_ref[...]) + sh_ref[...]).astype(bf16)

    vec = pl.BlockSpec((1, D), lambda i: (0, 0))
    return pl.pallas_call(
        body, name="ln_mod", grid=(S // tm,),
        in_specs=[pl.BlockSpec((tm, D), lambda i: (i, 0)), vec, vec, vec],
        out_specs=pl.BlockSpec((tm, D), lambda i: (i, 0)),
        out_shape=jax.ShapeDtypeStruct((S, D), bf16),
        compiler_params=_params(("parallel",)),
    )(x, nw, scale, shift)


def ln_mod_bwd(x, nw, scale, dh, dxres):
    S = x.shape[0]
    tm = 512
    nb = S // tm

    def body(x_ref, nw_ref, sc_ref, dh_ref, dr_ref, dx_ref, st_ref):
        i = pl.program_id(0)
        xv = x_ref[...]
        r = lax.rsqrt(jnp.mean(xv * xv, axis=-1, keepdims=True) + EPS)
        xn = xv * r
        dh = dh_ref[...]
        dxn = dh * (nw_ref[...] * (1.0 + sc_ref[...]))
        dx_ref[...] = dr_ref[...] + r * (dxn - xn * jnp.mean(dxn * xn, axis=-1, keepdims=True))
        p1 = jnp.sum(dh * xn, axis=0, keepdims=True)
        p2 = jnp.sum(dh, axis=0, keepdims=True)
        upd = jnp.concatenate([p1, p1, p2, jnp.zeros((5, D), f32)], axis=0)

        @pl.when(i == 0)
        def _():
            st_ref[...] = upd

        @pl.when(i > 0)
        def _():
            st_ref[...] += upd

        @pl.when(i == nb - 1)
        def _():
            st_ref[0:1, :] = st_ref[0:1, :] * (1.0 + sc_ref[...])
            st_ref[1:2, :] = st_ref[1:2, :] * nw_ref[...]

    vec = pl.BlockSpec((1, D), lambda i: (0, 0))
    tile = pl.BlockSpec((tm, D), lambda i: (i, 0))
    return pl.pallas_call(
        body, name="ln_mod_bwd", grid=(S // tm,),
        in_specs=[tile, vec, vec, tile, tile],
        out_specs=[tile, pl.BlockSpec((8, D), lambda i: (0, 0))],
        out_shape=[jax.ShapeDtypeStruct((S, D), f32), jax.ShapeDtypeStruct((8, D), f32)],
        compiler_params=_params(("arbitrary",)),
    )(x, nw, scale, dh, dxres)


def final_loss(x, fw, tgt):
    S = x.shape[0]
    tm = 512

    def body(x_ref, w_ref, t_ref, dx_ref, st_ref):
        i = pl.program_id(0)
        xv = x_ref[...]
        r = lax.rsqrt(jnp.mean(xv * xv, axis=-1, keepdims=True) + EPS)
        xn = xv * r
        err = xn * w_ref[...] - t_ref[...]
        dy = err * (1.0 / D)
        dxn = dy * w_ref[...]
        dx_ref[...] = r * (dxn - xn * jnp.mean(dxn * xn, axis=-1, keepdims=True))
        p1 = jnp.sum(dy * xn, axis=0, keepdims=True)
        p2 = jnp.sum(err * err, axis=0, keepdims=True) * (0.5 / D)
        upd = jnp.concatenate([p1, p2, jnp.zeros((6, D), f32)], axis=0)

        @pl.when(i == 0)
        def _():
            st_ref[...] = upd

        @pl.when(i > 0)
        def _():
            st_ref[...] += upd

    tile = pl.BlockSpec((tm, D), lambda i: (i, 0))
    return pl.pallas_call(
        body, name="final_loss", grid=(S // tm,),
        in_specs=[tile, pl.BlockSpec((1, D), lambda i: (0, 0)), tile],
        out_specs=[tile, pl.BlockSpec((8, D), lambda i: (0, 0))],
        out_shape=[jax.ShapeDtypeStruct((S, D), f32), jax.ShapeDtypeStruct((8, D), f32)],
        compiler_params=_params(("arbitrary",)),
    )(x, fw, tgt)


def gate_bwd(dx, y, gate):
    S = dx.shape[0]
    tm = 512

    def body(dx_ref, y_ref, g_ref, dy_ref, st_ref):
        i = pl.program_id(0)
        dxv = dx_ref[...]
        dy_ref[...] = (g_ref[...] * dxv).astype(bf16)
        upd = jnp.concatenate([jnp.sum(dxv * y_ref[...], axis=0, keepdims=True), jnp.zeros((7, D), f32)], axis=0)

        @pl.when(i == 0)
        def _():
            st_ref[...] = upd

        @pl.when(i > 0)
        def _():
            st_ref[...] += upd

    tile = pl.BlockSpec((tm, D), lambda i: (i, 0))
    return pl.pallas_call(
        body, name="gate_bwd", grid=(S // tm,),
        in_specs=[tile, tile, pl.BlockSpec((1, D), lambda i: (0, 0))],
        out_specs=[tile, pl.BlockSpec((8, D), lambda i: (0, 0))],
        out_shape=[jax.ShapeDtypeStruct((S, D), bf16), jax.ShapeDtypeStruct((8, D), f32)],
        compiler_params=_params(("arbitrary",)),
    )(dx, y, gate)


def _chunk_mats(tm):
    r, c = _iota((tm, tm), 0), _iota((tm, tm), 1)
    same = jnp.right_shift(r, 6) == jnp.right_shift(c, 6)
    ltri = jnp.where(same & (c <= r), 1.0, 0.0).astype(f32)
    utri = jnp.where(same & (c >= r), 1.0, 0.0).astype(f32)
    bsame = jnp.where(same, 1.0, 0.0).astype(f32)
    return ltri, utri, bsame


def _gdn_scalars(ba, alog, dtb, ltri, bsame):
    beta = _sigmoid(ba[:, 0:16])
    u = ba[:, 16:32] + dtb
    neg_a = -jnp.exp(alog)
    g = neg_a * _softplus(u)
    gc = _nn(ltri, g, HI)
    glast = _nn(bsame, g, HI)
    return beta, u, neg_a, g, gc, glast


def _conv_taps(p_ref, halo_ref, first, gi):
    cs = slice(gi * 128, (gi + 1) * 128)
    cur = p_ref[:, cs]
    hal = jnp.where(first, 0.0, halo_ref[:, cs])
    ext = jnp.concatenate([hal, cur], axis=0)
    return [cur] + [pltpu.roll(ext, s, 0)[8:] for s in range(1, CONV_K)]


def _conv(taps, w):
    cv = taps[0] * w[3:4]
    for s in range(1, CONV_K):
        cv = cv + taps[s] * w[3 - s:4 - s]
    return cv


def _l2n(x):
    return x * lax.rsqrt(jnp.sum(x * x, axis=-1, keepdims=True) + EPS)


def _gdn_in_specs(tm, S):
    nb8 = tm // 8
    return [pl.BlockSpec((tm, G_CONV), lambda i: (i, 0)),
            pl.BlockSpec((8, G_CONV), lambda i: (jnp.maximum(i * nb8 - 1, 0), 0)),
            pl.BlockSpec((tm, 128), lambda i: (i, G_BA0 // 128))]


def gdn_pre(proj, conv_w, alog, dtb):
    S = proj.shape[0]
    tm = 256
    nch = tm // CHUNK

    def body(p_ref, halo_ref, ba_ref, w_ref, al_ref, dt_ref,
             q_ref, k_ref, kb_ref, kbg_ref, vb_ref, qd_ref, kd_ref, d_ref, gl_ref):
        first = pl.program_id(0) == 0
        ltri, _, bsame = _chunk_mats(tm)
        beta, _, _, _, gc, glast = _gdn_scalars(ba_ref[...], al_ref[...], dt_ref[...], ltri, bsame)
        eg, ek, egl = jnp.exp(gc), jnp.exp(glast - gc), jnp.exp(glast)
        eye = jnp.where(_iota((16, 16), 0) == _iota((16, 16), 1), 1.0, 0.0).astype(f32)
        gct = _nt(eye, gc, HI)
        low = _iota((CHUNK, CHUNK), 0) >= _iota((CHUNK, CHUNK), 1)

        def act(gi):
            return _silu(_conv(_conv_taps(p_ref, halo_ref, first, gi), w_ref[:, gi * 128:(gi + 1) * 128]))

        for j in range(GQK_H):
            js = slice(j * 128, (j + 1) * 128)
            qn = _l2n(act(j)) * (GHD ** -0.5)
            kn = _l2n(act(GQK_H + j))
            q_ref[:, js] = qn.astype(bf16)
            k_ref[:, js] = kn.astype(bf16)
            for e in range(2):
                h = 2 * j + e
                hs = slice(h * 128, (h + 1) * 128)
                v = act(2 * GQK_H + h)
                bh, egh, ekh = beta[:, h:h + 1], eg[:, h:h + 1], ek[:, h:h + 1]
                kbv = kn * bh
                kb_ref[:, hs] = kbv.astype(bf16)
                kbg_ref[:, hs] = (kbv * egh).astype(bf16)
                vb_ref[:, hs] = (v * bh).astype(bf16)
                qd_ref[:, hs] = (qn * egh).astype(bf16)
                kd_ref[:, hs] = (kn * ekh).astype(bf16)
                for c in range(nch):
                    rs = slice(c * CHUNK, (c + 1) * CHUNK)
                    diff = gc[rs, h:h + 1] - gct[h:h + 1, rs]
                    d_ref[rs, h * CHUNK:(h + 1) * CHUNK] = jnp.where(low, jnp.exp(jnp.where(low, diff, 0.0)), 0.0)
                    gl_ref[c * 8:(c + 1) * 8, hs] = jnp.broadcast_to(egl[c * CHUNK:c * CHUNK + 8, h:h + 1], (8, 128))

    full = lambda shape: pl.BlockSpec(shape, lambda i: (0, 0))
    t1 = pl.BlockSpec((tm, 1024), lambda i: (i, 0))
    t2 = pl.BlockSpec((tm, 2048), lambda i: (i, 0))
    sd = jax.ShapeDtypeStruct
    return pl.pallas_call(
        body, name="gdn_pre", grid=(S // tm,),
        in_specs=_gdn_in_specs(tm, S) + [full((CONV_K, G_CONV)), full((1, 16)), full((1, 16))],
        out_specs=[t1, t1, t2, t2, t2, t2, t2, t1, pl.BlockSpec((tm // 8, 2048), lambda i: (i, 0))],
        out_shape=[sd((S, 1024), bf16)] * 2 + [sd((S, 2048), bf16)] * 5 + [sd((S, 1024), f32), sd((S // 8, 2048), f32)],
        compiler_params=_params(("parallel",)),
    )(proj, proj, proj, conv_w, alog, dtb)


def _bnn(a, b):
    return lax.dot_general(a, b, (((2,), (1,)), ((0,), (0,))), preferred_element_type=f32)


def _bnt(a, b):
    return lax.dot_general(a, b, (((2,), (2,)), ((0,), (0,))), preferred_element_type=f32)


def _btn(a, b):
    return lax.dot_general(a, b, (((1,), (1,)), ((0,), (0,))), preferred_element_type=f32)


def _split(a):
    hi = a.astype(bf16)
    return hi, (a - hi.astype(f32)).astype(bf16)


def _cat3(h, l, axis, lhs):
    return jnp.concatenate([h, h, l] if lhs else [h, l, h], axis=axis)


def _tri_inv_b(L):
    eye = jnp.where(_iota((1, CHUNK, CHUNK), 1) == _iota((1, CHUNK, CHUNK), 2), 1.0, 0.0).astype(f32)
    P = -L
    T = eye + P
    ph, pl_ = _split(P)
    for _ in range(5):
        P = _bnn(_cat3(ph, pl_, 2, True), _cat3(ph, pl_, 1, False))
        ph, pl_ = _split(P)
        th, tl = _split(T)
        T = T + _bnn(_cat3(th, tl, 2, True), _cat3(ph, pl_, 1, False))
    return T


GTB = 512


def _gdn_slices(ncb):
    pairs = [(c, e) for c in range(ncb) for e in range(2)]
    rs = lambda c: slice(c * CHUNK, (c + 1) * CHUNK)
    cs = lambda e: slice(e * 128, (e + 1) * 128)
    ds_ = lambda e: slice(e * CHUNK, (e + 1) * CHUNK)
    return pairs, rs, cs, ds_


def gdn_fwd(q, k, kb, kbg, vb, qd, kd, dm, gl8):
    S = q.shape[0]
    nb, ncb = S // GTB, GTB // CHUNK
    pairs, rs, cs, ds_ = _gdn_slices(ncb)

    def body(q_ref, k_ref, kb_ref, kbg_ref, vb_ref, qd_ref, kd_ref, d_ref, gl_ref,
             o_ref, w_ref, at_ref, t_ref, vn_ref, st_ref, state, u_scr):
        @pl.when(pl.program_id(1) == 0)
        def _():
            state[...] = jnp.zeros_like(state)

        stk = lambda ref, lanes: jnp.stack([ref[rs(c), lanes(e)] for c, e in pairs])
        kq = jnp.stack([k_ref[rs(c), :] for c, _ in pairs])
        dmat = stk(d_ref, ds_)
        strict = _iota((1, CHUNK, CHUNK), 1) > _iota((1, CHUNK, CHUNK), 2)
        T = _tri_inv_b(jnp.where(strict, _bnt(stk(kb_ref, cs), kq) * dmat, 0.0))
        tb = T.astype(bf16)
        u_scr[...] = _bnn(tb, stk(vb_ref, cs))
        wb = _bnn(tb, stk(kbg_ref, cs)).astype(bf16)
        qk = _bnt(jnp.stack([q_ref[rs(c), :] for c in range(ncb)]), jnp.stack([k_ref[rs(c), :] for c in range(ncb)]))
        for b, (c, e) in enumerate(pairs):
            w_ref[rs(c), cs(e)] = wb[b]
            at_ref[rs(c), ds_(e)] = (qk[c] * dmat[b]).astype(bf16)
            t_ref[rs(c), ds_(e)] = T[b]
        for b, (c, e) in enumerate(pairs):
            sb = state[e].astype(bf16)
            vnb = (u_scr[b] - _nn(w_ref[rs(c), cs(e)], sb)).astype(bf16)
            o_ref[rs(c), cs(e)] = _nn(qd_ref[rs(c), cs(e)], sb) + _nn(at_ref[rs(c), ds_(e)], vnb)
            st_ref[c * 128:(c + 1) * 128, cs(e)] = sb
            state[e] = state[e] * gl_ref[c * 8:c * 8 + 1, cs(e)] + _tn(kd_ref[rs(c), cs(e)], vnb)
            vn_ref[rs(c), cs(e)] = vnb

    b1 = pl.BlockSpec((GTB, 128), lambda j, i: (i, j))
    b2 = pl.BlockSpec((GTB, 256), lambda j, i: (i, j))
    sd = jax.ShapeDtypeStruct
    return pl.pallas_call(
        body, name="gdn_fwd", grid=(GQK_H, nb),
        in_specs=[b1, b1, b2, b2, b2, b2, b2, b1, pl.BlockSpec((GTB // 8, 256), lambda j, i: (i, j))],
        out_specs=[b2, b2, b1, b1, b2, pl.BlockSpec((ncb * 128, 256), lambda j, i: (i, j))],
        out_shape=[sd((S, 2048), f32), sd((S, 2048), bf16), sd((S, 1024), bf16), sd((S, 1024), f32),
                   sd((S, 2048), bf16), sd((S // CHUNK * 128, 2048), bf16)],
        scratch_shapes=[pltpu.VMEM((2, 128, 128), f32), pltpu.VMEM((2 * ncb, CHUNK, 128), f32)],
        compiler_params=_params(("parallel", "arbitrary")),
    )(q, k, kb, kbg, vb, qd, kd, dm, gl8)


def gdn_bwd(do, q, k, kb, kbg, vb, qd, kd, dm, gl8, w, at, T, vn, st):
    S = q.shape[0]
    nb, ncb = S // GTB, GTB // CHUNK
    pairs, rs, cs, ds_ = _gdn_slices(ncb)

    def body(do_ref, q_ref, k_ref, kb_ref, kbg_ref, vb_ref, qd_ref, kd_ref, d_ref, gl_ref, w_ref, at_ref, t_ref, vn_ref, st_ref,
             dq_ref, dk_ref, dkb_ref, dkbg_ref, dvb_ref, dqd_ref, dkd_ref, dgc_ref, dstate, dvn_scr, dw_scr, dat_scr, dgl_scr):
        @pl.when(pl.program_id(1) == 0)
        def _():
            dstate[...] = jnp.zeros_like(dstate)

        for b, (c, e) in reversed(list(enumerate(pairs))):
            dob = do_ref[rs(c), cs(e)].astype(bf16)
            sb = st_ref[c * 128:(c + 1) * 128, cs(e)]
            vnb = vn_ref[rs(c), cs(e)]
            gl = gl_ref[c * 8:c * 8 + 1, cs(e)]
            dS = dstate[e]
            dsb = dS.astype(bf16)
            dvnb = (_tn(at_ref[rs(c), ds_(e)], dob) + _nn(kd_ref[rs(c), cs(e)], dsb)).astype(bf16)
            dvn_scr[b] = dvnb
            dat_scr[b] = _nt(dob, vnb)
            dqd_ref[rs(c), cs(e)] = _nt(dob, sb)
            dkd_ref[rs(c), cs(e)] = _nt(vnb, dsb)
            dw_scr[b] = (-_nt(dvnb, sb)).astype(bf16)
            dgl = jnp.sum(jnp.sum(dS * sb.astype(f32), axis=1, keepdims=True), axis=0, keepdims=True)
            dgl_scr[b] = jnp.broadcast_to(dgl * gl, (8, 128))
            dstate[e] = gl * dS + _tn(qd_ref[rs(c), cs(e)], dob) - _tn(w_ref[rs(c), cs(e)], dvnb)

        stk = lambda ref, lanes: jnp.stack([ref[rs(c), lanes(e)] for c, e in pairs])
        kq = jnp.stack([k_ref[rs(c), :] for c, _ in pairs])
        qq = jnp.stack([q_ref[rs(c), :] for c, _ in pairs])
        kbb = stk(kb_ref, cs)
        Tm = stk(t_ref, ds_)
        tb = Tm.astype(bf16)
        dvn, dw = dvn_scr[...], dw_scr[...]
        dT = _bnt(dvn, stk(vb_ref, cs)) + _bnt(dw, stk(kbg_ref, cs))
        dvb, dkbg = _btn(tb, dvn), _btn(tb, dw)
        th, tl = _split(Tm)
        xh, xl = _split(_bnt(_cat3(*_split(dT), 2, True), _cat3(th, tl, 2, False)))
        dL = -_btn(_cat3(th, tl, 1, True), _cat3(xh, xl, 1, False))
        dmat = stk(d_ref, ds_)
        strict = _iota((1, CHUNK, CHUNK), 1) > _iota((1, CHUNK, CHUNK), 2)
        dA = jnp.where(strict, dL * dmat, 0.0)
        dB = dat_scr[...] * dmat
        dAb, dBb = dA.astype(bf16), dB.astype(bf16)
        dkb = _bnn(dAb, kq)
        dkc = _btn(dAb, kbb) + _btn(dBb, qq)
        dqc = _bnn(dBb, kq)
        M = dA * _bnt(kbb, kq) + dB * _bnt(qq, kq)
        mh, ml = _split(M)
        colsum = _btn(jnp.concatenate([mh, ml], axis=1), jnp.ones((2 * ncb, 2 * CHUNK, 128), bf16))
        lastrow = _iota((1, CHUNK, 128), 1) == CHUNK - 1
        for b, (c, e) in enumerate(pairs):
            dvb_ref[rs(c), cs(e)] = dvb[b]
            dkbg_ref[rs(c), cs(e)] = dkbg[b]
            dkb_ref[rs(c), cs(e)] = dkb[b]
            dgc_ref[rs(c), cs(e)] = (jnp.sum(M[b], axis=1, keepdims=True) - colsum[b]
                                     + jnp.where(lastrow[0], dgl_scr[b][0:1, :], 0.0))
        for c in range(ncb):
            dq_ref[rs(c), :] = dqc[2 * c] + dqc[2 * c + 1]
            dk_ref[rs(c), :] = dkc[2 * c] + dkc[2 * c + 1]

    b1 = pl.BlockSpec((GTB, 128), lambda j, i: (nb - 1 - i, j))
    b2 = pl.BlockSpec((GTB, 256), lambda j, i: (nb - 1 - i, j))
    sd = jax.ShapeDtypeStruct
    return pl.pallas_call(
        body, name="gdn_bwd", grid=(GQK_H, nb),
        in_specs=[b2, b1, b1, b2, b2, b2, b2, b2, b1, pl.BlockSpec((GTB // 8, 256), lambda j, i: (nb - 1 - i, j)),
                  b2, b1, b1, b2, pl.BlockSpec((ncb * 128, 256), lambda j, i: (nb - 1 - i, j))],
        out_specs=[b1, b1, b2, b2, b2, b2, b2, b2],
        out_shape=[sd((S, 1024), f32)] * 2 + [sd((S, 2048), f32)] * 6,
        scratch_shapes=[pltpu.VMEM((2, 128, 128), f32), pltpu.VMEM((2 * ncb, CHUNK, 128), bf16),
                        pltpu.VMEM((2 * ncb, CHUNK, 128), bf16), pltpu.VMEM((2 * ncb, CHUNK, CHUNK), f32),
                        pltpu.VMEM((2 * ncb, 8, 128), f32)],
        compiler_params=_params(("parallel", "arbitrary")),
    )(do, q, k, kb, kbg, vb, qd, kd, dm, gl8, w, at, T, vn, st)


def gdn_onorm(o, proj, nw):
    S = o.shape[0]
    tm = 256

    def body(o_ref, z_ref, nw_ref, o2_ref):
        for h in range(GV_H):
            hs = slice(h * 128, (h + 1) * 128)
            oh = o_ref[:, hs]
            r = lax.rsqrt(jnp.mean(oh * oh, axis=-1, keepdims=True) + EPS)
            o2_ref[:, hs] = (((oh * r) * nw_ref[...]) * _silu(z_ref[:, hs])).astype(bf16)

    t2 = pl.BlockSpec((tm, 2048), lambda i: (i, 0))
    return pl.pallas_call(
        body, name="gdn_onorm", grid=(S // tm,),
        in_specs=[t2, pl.BlockSpec((tm, 2048), lambda i: (i, G_Z0 // 2048)), pl.BlockSpec((1, 128), lambda i: (0, 0))],
        out_specs=t2, out_shape=jax.ShapeDtypeStruct((S, 2048), bf16),
        compiler_params=_params(("parallel",)),
    )(o, proj, nw)


def gdn_onorm_bwd(do2, o, proj, nw):
    S = o.shape[0]
    tm = 256

    def body(d_ref, o_ref, z_ref, nw_ref, do_ref, dz_ref, st_ref):
        i = pl.program_id(0)
        acc = jnp.zeros((1, 128), f32)
        for h in range(GV_H):
            hs = slice(h * 128, (h + 1) * 128)
            oh, z, d2 = o_ref[:, hs], z_ref[:, hs], d_ref[:, hs]
            r = lax.rsqrt(jnp.mean(oh * oh, axis=-1, keepdims=True) + EPS)
            on = oh * r
            dt = d2 * _silu(z)
            dz_ref[:, hs] = (d2 * (on * nw_ref[...]) * _dsilu(z)).astype(bf16)
            don = dt * nw_ref[...]
            acc = acc + jnp.sum(dt * on, axis=0, keepdims=True)
            do_ref[:, hs] = r * (don - on * jnp.mean(don * on, axis=-1, keepdims=True))
        upd = jnp.concatenate([acc, jnp.zeros((7, 128), f32)], axis=0)

        @pl.when(i == 0)
        def _():
            st_ref[...] = upd

        @pl.when(i > 0)
        def _():
            st_ref[...] += upd

    t2 = pl.BlockSpec((tm, 2048), lambda i: (i, 0))
    sd = jax.ShapeDtypeStruct
    return pl.pallas_call(
        body, name="gdn_onorm_bwd", grid=(S // tm,),
        in_specs=[t2, t2, pl.BlockSpec((tm, 2048), lambda i: (i, G_Z0 // 2048)), pl.BlockSpec((1, 128), lambda i: (0, 0))],
        out_specs=[t2, t2, pl.BlockSpec((8, 128), lambda i: (0, 0))],
        out_shape=[sd((S, 2048), f32), sd((S, 2048), bf16), sd((8, 128), f32)],
        compiler_params=_params(("arbitrary",)),
    )(do2, o, proj, nw)


def gdn_pre_bwd(proj, conv_w, alog, dtb, dq, dk, dkb, dkbg, dvb, dqd, dkd, dgcd):
    S = proj.shape[0]
    tm = 128

    def body(p_ref, halo_ref, ba_ref, w_ref, al_ref, dt_ref, dq_ref, dk_ref, dkb_ref, dkbg_ref, dvb_ref, dqd_ref, dkd_ref, dgc_ref,
             dcv_ref, dba_ref, st_ref):
        i = pl.program_id(0)
        first = i == 0
        ltri, utri, bsame = _chunk_mats(tm)
        beta, u, neg_a, g, gc, glast = _gdn_scalars(ba_ref[...], al_ref[...], dt_ref[...], ltri, bsame)
        eg, ek = jnp.exp(gc), jnp.exp(glast - gc)
        lane16 = _iota((tm, 16), 1)
        dgc_all = jnp.zeros((tm, 16), f32)
        rkd_all = jnp.zeros((tm, 16), f32)
        dbeta_all = jnp.zeros((tm, 16), f32)

        def pre(gi):
            return _conv(_conv_taps(p_ref, halo_ref, first, gi), w_ref[:, gi * 128:(gi + 1) * 128])

        def l2n_bwd(xt, dy):
            r = lax.rsqrt(jnp.sum(xt * xt, axis=-1, keepdims=True) + EPS)
            y = xt * r
            return r * (dy - y * jnp.sum(dy * y, axis=-1, keepdims=True))

        for j in range(GQK_H):
            js = slice(j * 128, (j + 1) * 128)
            cvq, cvk = pre(j), pre(GQK_H + j)
            qt, kt = _silu(cvq), _silu(cvk)
            qn = _l2n(qt) * (GHD ** -0.5)
            kn = _l2n(kt)
            dq_tot, dk_tot = dq_ref[:, js], dk_ref[:, js]
            for e in range(2):
                h = 2 * j + e
                hs = slice(h * 128, (h + 1) * 128)
                gv = 2 * GQK_H + h
                cvv = pre(gv)
                v = _silu(cvv)
                bh, egh, ekh = beta[:, h:h + 1], eg[:, h:h + 1], ek[:, h:h + 1]
                dkbg, dkd, dqd, dvb = dkbg_ref[:, hs], dkd_ref[:, hs], dqd_ref[:, hs], dvb_ref[:, hs]
                dkb_t = dkb_ref[:, hs] + dkbg * egh
                dk_tot = dk_tot + dkb_t * bh + dkd * ekh
                dq_tot = dq_tot + dqd * egh
                dcv_ref[:, gv * 128:(gv + 1) * 128] = (dvb * bh) * _dsilu(cvv)
                dbeta = jnp.sum(dkb_t * kn, axis=-1, keepdims=True) + jnp.sum(dvb * v, axis=-1, keepdims=True)
                rkd = jnp.sum(dkd * (kn * ekh), axis=-1, keepdims=True)
                dgc = (dgc_ref[:, hs][:, 0:1] + jnp.sum(dkbg * (kn * bh * egh), axis=-1, keepdims=True)
                       + jnp.sum(dqd * (qn * egh), axis=-1, keepdims=True) - rkd)
                sel = lane16 == h
                dgc_all = dgc_all + jnp.where(sel, dgc, 0.0)
                rkd_all = rkd_all + jnp.where(sel, rkd, 0.0)
                dbeta_all = dbeta_all + jnp.where(sel, dbeta, 0.0)
            dcv_ref[:, js] = l2n_bwd(qt, dq_tot * (GHD ** -0.5)) * _dsilu(cvq)
            ks = slice((GQK_H + j) * 128, (GQK_H + j + 1) * 128)
            dcv_ref[:, ks] = l2n_bwd(kt, dk_tot) * _dsilu(cvk)

        islast = jnp.bitwise_and(_iota((tm, 16), 0), CHUNK - 1) == CHUNK - 1
        dgc_all = dgc_all + jnp.where(islast, _nn(bsame, rkd_all, HI), 0.0)
        dg = _nn(utri, dgc_all, HI)
        da = dg * neg_a * _sigmoid(u)
        db = dbeta_all * beta * (1.0 - beta)
        r16, c128 = _iota((16, 128), 0), _iota((16, 128), 1)
        pb = jnp.where(c128 == r16, 1.0, 0.0).astype(f32)
        pa = jnp.where(c128 == r16 + 16, 1.0, 0.0).astype(f32)
        dba_ref[...] = _nn(db, pb, HI) + _nn(da, pa, HI)
        upd = jnp.concatenate([jnp.sum(dg * g, axis=0, keepdims=True), jnp.sum(da, axis=0, keepdims=True),
                               jnp.zeros((6, 16), f32)], axis=0)

        @pl.when(i == 0)
        def _():
            st_ref[...] = upd

        @pl.when(i > 0)
        def _():
            st_ref[...] += upd

    full = lambda shape: pl.BlockSpec(shape, lambda i: (0, 0))
    t1 = pl.BlockSpec((tm, 1024), lambda i: (i, 0))
    t2 = pl.BlockSpec((tm, 2048), lambda i: (i, 0))
    sd = jax.ShapeDtypeStruct
    return pl.pallas_call(
        body, name="gdn_pre_bwd", grid=(S // tm,),
        in_specs=_gdn_in_specs(tm, S) + [full((CONV_K, G_CONV)), full((1, 16)), full((1, 16)), t1, t1] + [t2] * 6,
        out_specs=[pl.BlockSpec((tm, G_CONV), lambda i: (i, 0)), pl.BlockSpec((tm, 128), lambda i: (i, 0)), full((8, 16))],
        out_shape=[sd((S, G_CONV), f32), sd((S, 128), f32), sd((8, 16), f32)],
        compiler_params=_params(("arbitrary",)),
    )(proj, proj, proj, conv_w, alog, dtb, dq, dk, dkb, dkbg, dvb, dqd, dkd, dgcd)


def gdn_conv_bwd(proj, conv_w, dcv, dz, dba):
    S = proj.shape[0]
    tm = 256
    nb, nb8 = S // tm, tm // 8

    def body(p_ref, halo_ref, w_ref, dcv_ref, nxt_ref, dz_ref, dba_ref, dp_ref, dw_ref):
        i = pl.program_id(0)
        first, last = i == 0, i == nb - 1
        for gi in range(G_CONV // 128):
            cs = slice(gi * 128, (gi + 1) * 128)
            taps = _conv_taps(p_ref, halo_ref, first, gi)
            cur = dcv_ref[:, cs]
            ext = jnp.concatenate([cur, jnp.where(last, 0.0, nxt_ref[:, cs])], axis=0)
            w = w_ref[:, cs]
            dp = cur * w[3:4]
            rows = [jnp.sum(cur * taps[3 - kk], axis=0, keepdims=True) for kk in range(CONV_K)]
            for s in range(1, CONV_K):
                dp = dp + pltpu.roll(ext, tm + 8 - s, 0)[:tm] * w[3 - s:4 - s]
            dp_ref[:, cs] = dp.astype(bf16)
            upd = jnp.concatenate(rows + [jnp.zeros((4, 128), f32)], axis=0)

            @pl.when(first)
            def _():
                dw_ref[:, cs] = upd

            @pl.when(i > 0)
            def _():
                dw_ref[:, cs] += upd

        dp_ref[:, G_Z0:G_BA0] = dz_ref[...]
        dp_ref[:, G_BA0:G_INP] = dba_ref[...].astype(bf16)

    sd = jax.ShapeDtypeStruct
    return pl.pallas_call(
        body, name="gdn_conv_bwd", grid=(nb,),
        in_specs=[pl.BlockSpec((tm, G_CONV), lambda i: (i, 0)),
                  pl.BlockSpec((8, G_CONV), lambda i: (jnp.maximum(i * nb8 - 1, 0), 0)),
                  pl.BlockSpec((CONV_K, G_CONV), lambda i: (0, 0)),
                  pl.BlockSpec((tm, G_CONV), lambda i: (i, 0)),
                  pl.BlockSpec((8, G_CONV), lambda i: (jnp.minimum((i + 1) * nb8, S // 8 - 1), 0)),
                  pl.BlockSpec((tm, 2048), lambda i: (i, 0)), pl.BlockSpec((tm, 128), lambda i: (i, 0))],
        out_specs=[pl.BlockSpec((tm, G_INP), lambda i: (i, 0)), pl.BlockSpec((8, G_CONV), lambda i: (0, 0))],
        out_shape=[sd((S, G_INP), bf16), sd((8, G_CONV), f32)],
        compiler_params=_params(("arbitrary",)),
    )(proj, proj, conv_w, dcv, dcv, dz, dba)


def _half_mean(t, lo_half):
    m0 = jnp.sum(jnp.where(lo_half, t, 0.0), axis=-1, keepdims=True)
    m1 = jnp.sum(jnp.where(lo_half, 0.0, t), axis=-1, keepdims=True)
    return jnp.where(lo_half, m0, m1) * (1.0 / F_HD)


def _split3(c):
    hi = c.astype(bf16).astype(f32)
    mid = (c - hi).astype(bf16).astype(f32)
    lo = (c - hi - mid).astype(bf16).astype(f32)
    return hi, mid, lo


def fox_pre(proj, fbias, qw2, kw2):
    S = proj.shape[0]
    tm = 256

    def body(q_ref, k_ref, v_ref, f_ref, fb_ref, qw_ref, kw_ref, qa_ref, ka_ref, vb_ref, carry):
        @pl.when(pl.program_id(0) == 0)
        def _():
            carry[...] = jnp.zeros_like(carry)

        logf = -_softplus(-(f_ref[:, 0:16] + fb_ref[...]))
        ltri = jnp.where(_iota((tm, tm), 1) <= _iota((tm, tm), 0), 1.0, 0.0).astype(f32)
        cum = _nn(ltri, logf, HI) + carry[0:1, :]
        carry[0:1, :] = cum[tm - 1:tm, :]
        lane = _iota((tm, 128), 1)
        lo_half = lane < F_HD
        for p in range(F_H // 2):
            ps = slice(p * 128, (p + 1) * 128)
            for src, w_ref, dst, is_q in ((q_ref, qw_ref, qa_ref, True), (k_ref, kw_ref, ka_ref, False)):
                x = src[:, ps]
                xn = x * lax.rsqrt(_half_mean(x * x, lo_half) + EPS) * w_ref[...]
                if is_q:
                    xn = xn * (F_HD ** -0.5)
                for e in range(2):
                    h = 2 * p + e
                    base = xn if e == 0 else pltpu.roll(xn, F_HD, 1)
                    hi, mid, lo = _split3(cum[:, h:h + 1])
                    pieces = jnp.where(lane == 64, hi, 0.0) + jnp.where(lane == 65, mid, 0.0) + jnp.where(lane == 66, lo, 0.0)
                    if is_q:
                        ext = pieces + jnp.where((lane >= 67) & (lane <= 69), 1.0, 0.0)
                    else:
                        ext = jnp.where((lane >= 64) & (lane <= 66), 1.0, 0.0) - pltpu.roll(pieces, 3, 1)
                    dst[:, h * 128:(h + 1) * 128] = jnp.where(lo_half, base, ext).astype(bf16)
        one = jnp.where(lane == F_HD, 1.0, 0.0)
        for p in range(F_H // 2):
            vv = v_ref[:, p * 128:(p + 1) * 128]
            vb_ref[:, (2 * p) * 128:(2 * p + 1) * 128] = jnp.where(lo_half, vv, one).astype(bf16)
            vb_ref[:, (2 * p + 1) * 128:(2 * p + 2) * 128] = jnp.where(lo_half, pltpu.roll(vv, F_HD, 1), one).astype(bf16)

    t1 = lambda c: pl.BlockSpec((tm, 1024), lambda i: (i, c))
    vec = lambda n: pl.BlockSpec((1, n), lambda i: (0, 0))
    sd = jax.ShapeDtypeStruct
    return pl.pallas_call(
        body, name="fox_pre", grid=(S // tm,),
        in_specs=[t1(0), t1(1), t1(2), pl.BlockSpec((tm, 128), lambda i: (i, F_F0 // 128)), vec(16), vec(128), vec(128)],
        out_specs=[pl.BlockSpec((tm, 2048), lambda i: (i, 0))] * 3,
        out_shape=[sd((S, 2048), bf16)] * 3,
        scratch_shapes=[pltpu.VMEM((8, 16), f32)],
        compiler_params=_params(("arbitrary",)),
    )(proj, proj, proj, proj, fbias, qw2, kw2)


FTQ = 512


def fox_attn(qa, ka, v):
    S = qa.shape[0]
    nq = S // FTQ

    def body(q_ref, k_ref, v_ref, o_ref, lse_ref, m_scr, acc_scr):
        i, j = pl.program_id(1), pl.program_id(2)

        @pl.when(j == 0)
        def _():
            m_scr[...] = jnp.full_like(m_scr, NEG)
            acc_scr[...] = jnp.zeros_like(acc_scr)

        def step(diagonal):
            for e in range(2):
                es = slice(e * 128, (e + 1) * 128)
                s = _nt(q_ref[:, es], k_ref[:, es])
                if diagonal:
                    s = jnp.where(_iota((FTQ, FTQ), 0) >= _iota((FTQ, FTQ), 1), s, NEG)
                m_old = m_scr[e]
                m_new = jnp.maximum(m_old, jnp.max(s, axis=-1, keepdims=True))
                p = jnp.exp(s - m_new[:, 0:1])
                acc_scr[e] = acc_scr[e] * jnp.exp(m_old - m_new) + _nn(p.astype(bf16), v_ref[:, es])
                m_scr[e] = m_new

        pl.when(j < i)(functools.partial(step, False))
        pl.when(j == i)(functools.partial(step, True))

        @pl.when(j == nq - 1)
        def _():
            for e in range(2):
                vs = slice(e * F_HD, (e + 1) * F_HD)
                acc = acc_scr[e]
                l = acc[:, F_HD:F_HD + 1]
                o_ref[:, vs] = acc[:, 0:F_HD] / l
                lse_ref[:, vs] = m_scr[e][:, 0:F_HD] + jnp.log(l)

    sd = jax.ShapeDtypeStruct
    qo = pl.BlockSpec((FTQ, 128), lambda p, i, j: (i, p))
    kv = pl.BlockSpec((FTQ, 256), lambda p, i, j: (jnp.minimum(j, i), p))
    return pl.pallas_call(
        body, name="fox_attn", grid=(F_H // 2, nq, nq),
        in_specs=[pl.BlockSpec((FTQ, 256), lambda p, i, j: (i, p)), kv, kv],
        out_specs=[qo, qo],
        out_shape=[sd((S, 1024), f32), sd((S, 1024), f32)],
        scratch_shapes=[pltpu.VMEM((2, FTQ, 128), f32), pltpu.VMEM((2, FTQ, 128), f32)],
        compiler_params=_params(("parallel", "parallel", "arbitrary")),
    )(qa, ka, v)


def fox_attn_bwd(qa, ka, v, do, lse, delta):
    S = qa.shape[0]
    nq = S // FTQ

    def body(q_ref, k_ref, v_ref, do_ref, lse_ref, dl_ref, dq_ref, dk_ref, dv_ref, dk_scr, dv_scr):
        j, i = pl.program_id(1), pl.program_id(2)

        @pl.when((j == 0) & (i == 0))
        def _():
            dq_ref[...] = jnp.zeros_like(dq_ref)

        @pl.when(i == 0)
        def _():
            dk_scr[...] = jnp.zeros_like(dk_scr)
            dv_scr[...] = jnp.zeros_like(dv_scr)

        def step(diagonal):
            rows = pl.ds(pl.multiple_of(i * FTQ, FTQ), FTQ)
            for e in range(2):
                es, vs = slice(e * 128, (e + 1) * 128), slice(e * F_HD, (e + 1) * F_HD)
                qe, ke = q_ref[:, es], k_ref[:, es]
                dob = do_ref[:, vs].astype(bf16)
                s = _nt(qe, ke)
                if diagonal:
                    s = jnp.where(_iota((FTQ, FTQ), 0) >= _iota((FTQ, FTQ), 1), s, NEG)
                p = jnp.exp(s - lse_ref[:, e * F_HD:e * F_HD + 1])
                ds = p * (_nt(dob, v_ref[:, e * 128:e * 128 + F_HD]) - dl_ref[:, e * F_HD:e * F_HD + 1])
                dsb = ds.astype(bf16)
                dv_scr[e] += _tn(p.astype(bf16), dob)
                dk_scr[e] += _tn(dsb, qe)
                dq_ref[rows, es] += _nn(dsb, ke)

        pl.when(i > j)(functools.partial(step, False))
        pl.when(i == j)(functools.partial(step, True))

        @pl.when(i == nq - 1)
        def _():
            for e in range(2):
                dk_ref[:, e * 128:(e + 1) * 128] = dk_scr[e]
                dv_ref[:, e * F_HD:(e + 1) * F_HD] = dv_scr[e]

    sd = jax.ShapeDtypeStruct
    qi = lambda w: pl.BlockSpec((FTQ, w), lambda p, j, i: (jnp.maximum(i, j), p))
    kj = lambda w: pl.BlockSpec((FTQ, w), lambda p, j, i: (j, p))
    return pl.pallas_call(
        body, name="fox_attn_bwd", grid=(F_H // 2, nq, nq),
        in_specs=[qi(256), kj(256), kj(256), qi(128), qi(128), qi(128)],
        out_specs=[pl.BlockSpec((S, 256), lambda p, j, i: (0, p)), kj(256), kj(128)],
        out_shape=[sd((S, 2048), f32), sd((S, 2048), f32), sd((S, 1024), f32)],
        scratch_shapes=[pltpu.VMEM((2, FTQ, 128), f32), pltpu.VMEM((2, FTQ, F_HD), f32)],
        compiler_params=_params(("parallel", "arbitrary", "arbitrary")),
    )(qa, ka, v, do, lse, delta)


def fox_gate(o, proj):
    S = o.shape[0]
    tm = 512

    def body(o_ref, z_ref, o2_ref):
        o2_ref[...] = (o_ref[...] * _silu(z_ref[...])).astype(bf16)

    t = pl.BlockSpec((tm, 1024), lambda i: (i, 0))
    return pl.pallas_call(
        body, name="fox_gate", grid=(S // tm,),
        in_specs=[t, pl.BlockSpec((tm, 1024), lambda i: (i, 3))], out_specs=t,
        out_shape=jax.ShapeDtypeStruct((S, 1024), bf16),
        compiler_params=_params(("parallel",)),
    )(o, proj)


def fox_gate_bwd(do2, o, proj):
    S = o.shape[0]
    tm = 256

    def body(d_ref, o_ref, z_ref, do_ref, dz_ref, dl_ref):
        lo_half = _iota((tm, 128), 1) < F_HD
        for p in range(F_H // 2):
            ps = slice(p * 128, (p + 1) * 128)
            d2, ov, z = d_ref[:, ps], o_ref[:, ps], z_ref[:, ps]
            dov = d2 * _silu(z)
            do_ref[:, ps] = dov
            dz_ref[:, ps] = (d2 * ov * _dsilu(z)).astype(bf16)
            dl_ref[:, ps] = _half_mean(dov * ov, lo_half) * float(F_HD)

    t = pl.BlockSpec((tm, 1024), lambda i: (i, 0))
    sd = jax.ShapeDtypeStruct
    return pl.pallas_call(
        body, name="fox_gate_bwd", grid=(S // tm,),
        in_specs=[t, t, pl.BlockSpec((tm, 1024), lambda i: (i, 3))], out_specs=[t, t, t],
        out_shape=[sd((S, 1024), f32), sd((S, 1024), bf16), sd((S, 1024), f32)],
        compiler_params=_params(("parallel",)),
    )(do2, o, proj)


def fox_pre_bwd(proj, fbias, qw2, kw2, dqa, dka, dv, dz):
    S = proj.shape[0]
    tm = 256
    nb = S // tm

    def body(q_ref, k_ref, f_ref, fb_ref, qw_ref, kw_ref, dqa_ref, dka_ref, dv_ref, dz_ref, dp_ref, st_ref, carry):
        i = pl.program_id(0)

        @pl.when(i == 0)
        def _():
            carry[...] = jnp.zeros_like(carry)

        lane = _iota((tm, 128), 1)
        lo_half = lane < F_HD
        lane16 = _iota((tm, 16), 1)
        dcum = jnp.zeros((tm, 16), f32)
        dws = []
        for src, w_ref, dsrc, is_q, col0 in ((q_ref, qw_ref, dqa_ref, True, 0), (k_ref, kw_ref, dka_ref, False, 1024)):
            dw = jnp.zeros((1, 128), f32)
            for p in range(F_H // 2):
                ps = slice(p * 128, (p + 1) * 128)
                x = src[:, ps]
                r = lax.rsqrt(_half_mean(x * x, lo_half) + EPS)
                xh = x * r
                d0 = dsrc[:, (2 * p) * 128:(2 * p + 1) * 128]
                d1 = dsrc[:, (2 * p + 1) * 128:(2 * p + 2) * 128]
                dy = jnp.where(lo_half, d0, pltpu.roll(d1, F_HD, 1))
                if is_q:
                    dy = dy * (F_HD ** -0.5)
                dxh = dy * w_ref[...]
                dw = dw + jnp.sum(dy * xh, axis=0, keepdims=True)
                dp_ref[:, col0 + p * 128:col0 + (p + 1) * 128] = (r * (dxh - xh * _half_mean(dxh * xh, lo_half))).astype(bf16)
                for e, de in ((0, d0), (1, d1)):
                    col = de[:, 64:65] if is_q else -de[:, 67:68]
                    dcum = dcum + jnp.where(lane16 == 2 * p + e, col, 0.0)
            dws.append(dw)
        dp_ref[:, 2048:3072] = dv_ref[...].astype(bf16)
        dp_ref[:, 3072:4096] = dz_ref[...]
        utri = jnp.where(_iota((tm, tm), 1) >= _iota((tm, tm), 0), 1.0, 0.0).astype(f32)
        dlogf = _nn(utri, dcum, HI) + carry[0:1, :]
        carry[0:1, :] = dlogf[0:1, :]
        fl = f_ref[:, 0:16] + fb_ref[...]
        df = dlogf * _sigmoid(-fl)
        place = jnp.where(_iota((16, 128), 1) == _iota((16, 128), 0), 1.0, 0.0).astype(f32)
        dfw = _nn(df, place, HI)
        dp_ref[:, F_F0:F_INP] = dfw.astype(bf16)
        upd = jnp.concatenate(dws + [jnp.sum(dfw, axis=0, keepdims=True), jnp.zeros((5, 128), f32)], axis=0)

        @pl.when(i == 0)
        def _():
            st_ref[...] = upd

        @pl.when(i > 0)
        def _():
            st_ref[...] += upd

    rev = lambda w, c: pl.BlockSpec((tm, w), lambda i: (nb - 1 - i, c))
    vec = lambda n: pl.BlockSpec((1, n), lambda i: (0, 0))
    sd = jax.ShapeDtypeStruct
    return pl.pallas_call(
        body, name="fox_pre_bwd", grid=(nb,),
        in_specs=[rev(1024, 0), rev(1024, 1), rev(128, F_F0 // 128), vec(16), vec(128), vec(128),
                  rev(2048, 0), rev(2048, 0), rev(1024, 0), rev(1024, 0)],
        out_specs=[rev(F_INP, 0), pl.BlockSpec((8, 128), lambda i: (0, 0))],
        out_shape=[sd((S, F_INP), bf16), sd((8, 128), f32)],
        scratch_shapes=[pltpu.VMEM((8, 16), f32)],
        compiler_params=_params(("arbitrary",)),
    )(proj, proj, proj, fbias, qw2, kw2, dqa, dka, dv, dz)


def _me():
    return lax.axis_index("x"), lax.axis_index("y"), lax.axis_index("c")


def _other_chips(x, y):
    return [(1 - x, y), (x, 1 - y), (1 - x, 1 - y)]


def ag_small(xs):
    m_per, n = xs.shape

    def body(x_ref, out_ref, send_sems, recv_sems, local_sem):
        x, y, c = _me()
        me, sibling = (x, y, c), (x, y, 1 - c)
        chips = _other_chips(x, y)

        def rows(px, py, pc):
            return out_ref.at[pl.ds((4 * px + 2 * py + pc) * m_per, m_per), :]

        def copy(k, block, to, src=None):
            return pltpu.make_async_remote_copy(
                src_ref=rows(*block) if src is None else src, dst_ref=rows(*block),
                send_sem=send_sems.at[k], recv_sem=recv_sems.at[k], device_id=to, device_id_type=MESH)

        mine = pltpu.make_async_copy(x_ref, rows(*me), local_sem)
        mine.start()
        first = [copy(0, me, sibling, src=x_ref)]
        first += [copy(1 + j, me, (*chip, c), src=x_ref) for j, chip in enumerate(chips)]
        for cp in first:
            cp.start()
        passed = [copy(4 + j, (*chip, c), sibling) for j, chip in enumerate(chips)]
        for j, chip in enumerate(chips):
            copy(1 + j, (*chip, c), me).wait_recv()
            passed[j].start()
        copy(0, sibling, me).wait_recv()
        for j, chip in enumerate(chips):
            copy(4 + j, (*chip, 1 - c), me).wait_recv()
        for cp in first + passed:
            cp.wait_send()
        mine.wait()

    return pl.pallas_call(
        body, name="ag_small",
        out_shape=jax.ShapeDtypeStruct((8 * m_per, n), xs.dtype),
        in_specs=[pl.BlockSpec(memory_space=pltpu.VMEM)], out_specs=pl.BlockSpec(memory_space=pltpu.VMEM),
        scratch_shapes=[pltpu.SemaphoreType.DMA((7,)), pltpu.SemaphoreType.DMA((7,)), pltpu.SemaphoreType.DMA],
        compiler_params=pltpu.CompilerParams(vmem_limit_bytes=VMEM_LIMIT),
    )(xs)


_ANY = pl.BlockSpec(memory_space=pl.ANY)


def ag_chips(arrs):
    n = len(arrs)
    assert all(a.shape[0] == 2 for a in arrs)

    def body(*refs):
        ins, outs = refs[:n], refs[n:2 * n]
        send_sems, recv_sems, fwd_send, fwd_recv, local_sems = refs[2 * n:]
        x, y, c = _me()
        me = 2 * x + y
        chips = _other_chips(x, y)
        started = []
        for a in range(n):
            cp = pltpu.make_async_copy(ins[a], outs[a].at[me], local_sems.at[a])
            cp.start()
            started.append(cp)
        sends = []
        for a in range(n):
            for j, (px, py) in enumerate(chips):
                r = pltpu.make_async_remote_copy(
                    src_ref=ins[a].at[c], dst_ref=outs[a].at[me, c], send_sem=send_sems.at[3 * a + j],
                    recv_sem=recv_sems.at[3 * a + j], device_id=(px, py, c), device_id_type=MESH)
                r.start()
                sends.append(r)
        for a in range(n):
            for j, (px, py) in enumerate(chips):
                got = outs[a].at[2 * px + py, c]
                pltpu.make_async_remote_copy(
                    src_ref=ins[a].at[c], dst_ref=got, send_sem=send_sems.at[3 * a + j],
                    recv_sem=recv_sems.at[3 * a + j], device_id=(px, py, c), device_id_type=MESH).wait_recv()
                f = pltpu.make_async_remote_copy(
                    src_ref=got, dst_ref=got, send_sem=fwd_send.at[3 * a + j], recv_sem=fwd_recv.at[3 * a + j],
                    device_id=(x, y, 1 - c), device_id_type=MESH)
                f.start()
                sends.append(f)
        for a in range(n):
            for j, (px, py) in enumerate(chips):
                theirs = outs[a].at[2 * px + py, 1 - c]
                pltpu.make_async_remote_copy(
                    src_ref=theirs, dst_ref=theirs, send_sem=fwd_send.at[3 * a + j], recv_sem=fwd_recv.at[3 * a + j],
                    device_id=(x, y, 1 - c), device_id_type=MESH).wait_recv()
        for r in sends:
            r.wait_send()
        for cp in started:
            cp.wait()

    sems = pltpu.SemaphoreType.DMA((3 * n,))
    return pl.pallas_call(
        body, name="ag_chips",
        out_shape=[jax.ShapeDtypeStruct((4,) + a.shape, a.dtype) for a in arrs],
        in_specs=[_ANY] * n, out_specs=[_ANY] * n,
        scratch_shapes=[sems, sems, sems, sems, pltpu.SemaphoreType.DMA((n,))],
    )(*arrs)


def rs_swap_halves(gs):
    n = len(gs)

    def body(*refs):
        ins, outs = refs[:n], refs[n:2 * n]
        send_sems, recv_sems = refs[2 * n:]
        x, y, c = _me()
        cps = []
        for a in range(n):
            rh = ins[a].shape[1] // 2
            cp = pltpu.make_async_remote_copy(
                src_ref=ins[a].at[:, pl.ds((1 - c) * rh, rh), :], dst_ref=outs[a],
                send_sem=send_sems.at[a], recv_sem=recv_sems.at[a], device_id=(x, y, 1 - c), device_id_type=MESH)
            cp.start()
            cps.append(cp)
        for cp in cps:
            cp.wait()

    return pl.pallas_call(
        body, name="rs_swap_halves",
        out_shape=[jax.ShapeDtypeStruct((4, g.shape[1] // 2, g.shape[2]), g.dtype) for g in gs],
        in_specs=[_ANY] * n, out_specs=[_ANY] * n,
        scratch_shapes=[pltpu.SemaphoreType.DMA((n,)), pltpu.SemaphoreType.DMA((n,))],
    )(*gs)


def rs_add_halves(g, r, cidx):
    _, rh, C = r.shape
    tr = _pick(rh, (256, 128))
    nrb = rh // tr

    def body(c_ref, g_ref, r_ref, o_ref):
        o_ref[...] = (g_ref[...] + r_ref[...]).astype(bf16)

    return pl.pallas_call(
        body, name="rs_add_halves",
        grid_spec=pltpu.PrefetchScalarGridSpec(
            num_scalar_prefetch=1, grid=(4, nrb),
            in_specs=[pl.BlockSpec((1, tr, C), lambda k, i, c_ref: (k, c_ref[0] * nrb + i, 0)),
                      pl.BlockSpec((1, tr, C), lambda k, i, c_ref: (k, i, 0))],
            out_specs=pl.BlockSpec((1, tr, C), lambda k, i, c_ref: (k, i, 0))),
        out_shape=jax.ShapeDtypeStruct(r.shape, bf16),
        compiler_params=_params(("parallel", "parallel")),
    )(cidx, g, r)


def rs_exchange_chips(ps):
    n = len(ps)

    def body(*refs):
        ins, outs = refs[:n], refs[n:2 * n]
        send_sems, recv_sems, local_sems = refs[2 * n:]
        x, y, c = _me()
        me = 2 * x + y
        chips = _other_chips(x, y)
        started = []
        for a in range(n):
            cp = pltpu.make_async_copy(ins[a].at[me], outs[a].at[me], local_sems.at[a])
            cp.start()
            started.append(cp)
        sends = []
        for a in range(n):
            for j, (px, py) in enumerate(chips):
                r = pltpu.make_async_remote_copy(
                    src_ref=ins[a].at[2 * px + py], dst_ref=outs[a].at[me], send_sem=send_sems.at[3 * a + j],
                    recv_sem=recv_sems.at[3 * a + j], device_id=(px, py, c), device_id_type=MESH)
                r.start()
                sends.append(r)
        for a in range(n):
            for j, (px, py) in enumerate(chips):
                pltpu.make_async_remote_copy(
                    src_ref=ins[a].at[me], dst_ref=outs[a].at[2 * px + py], send_sem=send_sems.at[3 * a + j],
                    recv_sem=recv_sems.at[3 * a + j], device_id=(px, py, c), device_id_type=MESH).wait_recv()
        for r in sends:
            r.wait_send()
        for cp in started:
            cp.wait()

    return pl.pallas_call(
        body, name="rs_exchange_chips",
        out_shape=[jax.ShapeDtypeStruct(p.shape, p.dtype) for p in ps],
        in_specs=[_ANY] * n, out_specs=[_ANY] * n,
        scratch_shapes=[pltpu.SemaphoreType.DMA((3 * n,)), pltpu.SemaphoreType.DMA((3 * n,)), pltpu.SemaphoreType.DMA((n,))],
    )(*ps)


def sum_leading(q, name):
    K, R, C = q.shape
    tr = _pick(R, (256, 128, 64, 32, 16, 8))

    def body(q_ref, o_ref):
        acc = q_ref[0]
        for k in range(1, K):
            acc = acc + q_ref[k]
        o_ref[...] = acc

    return pl.pallas_call(
        body, name=name, grid=(R // tr,),
        in_specs=[pl.BlockSpec((K, tr, C), lambda i: (0, i, 0))], out_specs=pl.BlockSpec((tr, C), lambda i: (i, 0)),
        out_shape=jax.ShapeDtypeStruct((R, C), f32),
        compiler_params=_params(("parallel",)),
    )(q)


def rs_sum_chips(q, cidx):
    K, R, C = q.shape
    tr = _pick(R, (256, 128))

    def body(c_ref, q_ref, o_ref):
        acc = q_ref[0].astype(f32)
        for k in range(1, K):
            acc = acc + q_ref[k].astype(f32)
        o_ref[0] = acc

    return pl.pallas_call(
        body, name="rs_sum_chips",
        grid_spec=pltpu.PrefetchScalarGridSpec(
            num_scalar_prefetch=1, grid=(R // tr,),
            in_specs=[pl.BlockSpec((K, tr, C), lambda i, c_ref: (0, i, 0))],
            out_specs=pl.BlockSpec((1, tr, C), lambda i, c_ref: (c_ref[0], i, 0))),
        out_shape=jax.ShapeDtypeStruct((2, R, C), f32),
        compiler_params=_params(("parallel",)),
    )(cidx, q)


def rs_share_halves(rs):
    n = len(rs)

    def body(*refs):
        bufs = refs[n:2 * n]
        send_sems, recv_sems = refs[2 * n:]
        x, y, c = _me()
        cps = []
        for a in range(n):
            cp = pltpu.make_async_remote_copy(
                src_ref=bufs[a].at[c], dst_ref=bufs[a].at[c], send_sem=send_sems.at[a], recv_sem=recv_sems.at[a],
                device_id=(x, y, 1 - c), device_id_type=MESH)
            cp.start()
            cps.append(cp)
        for a, cp in enumerate(cps):
            pltpu.make_async_remote_copy(
                src_ref=bufs[a].at[c], dst_ref=bufs[a].at[1 - c], send_sem=send_sems.at[a], recv_sem=recv_sems.at[a],
                device_id=(x, y, 1 - c), device_id_type=MESH).wait_recv()
            cp.wait_send()

    return pl.pallas_call(
        body, name="rs_share_halves",
        out_shape=[jax.ShapeDtypeStruct(r.shape, r.dtype) for r in rs],
        in_specs=[_ANY] * n, out_specs=[_ANY] * n, input_output_aliases={a: a for a in range(n)},
        scratch_shapes=[pltpu.SemaphoreType.DMA((n,)), pltpu.SemaphoreType.DMA((n,))],
    )(*rs)


def reduce_scatter(gs, cidx):
    got = rs_swap_halves(gs)
    ps = [rs_add_halves(g, r, cidx) for g, r in zip(gs, got)]
    qs = rs_exchange_chips(ps)
    fs = rs_share_halves([rs_sum_chips(q, cidx) for q in qs])
    return [f.reshape(-1, f.shape[-1]) for f in fs]


def ada_mod(c_all, ada_w):
    L, _, n = ada_w.shape

    def body(c_ref, w_ref, o_ref):
        o_ref[0] = _nn(_silu(c_ref[...]), w_ref[0], HI)

    return pl.pallas_call(
        body, name="ada_mod", grid=(L,),
        in_specs=[pl.BlockSpec((8, D), lambda l: (0, 0)), pl.BlockSpec((1, D, n), lambda l: (l, 0, 0))],
        out_specs=pl.BlockSpec((1, 8, n), lambda l: (l, 0, 0)),
        out_shape=jax.ShapeDtypeStruct((L, 8, n), f32),
        compiler_params=_params(("parallel",)),
    )(c_all, ada_w)


def ada_w_grad(c_all, dmod):
    L, _, n = dmod.shape

    def body(c_ref, d_ref, o_ref):
        o_ref[0] = _tn(_silu(c_ref[...]), d_ref[0], HI)

    return pl.pallas_call(
        body, name="ada_w_grad", grid=(L,),
        in_specs=[pl.BlockSpec((8, D), lambda l: (0, 0)), pl.BlockSpec((1, 8, n), lambda l: (l, 0, 0))],
        out_specs=pl.BlockSpec((1, D, n), lambda l: (l, 0, 0)),
        out_shape=jax.ShapeDtypeStruct((L, D, n), f32),
        compiler_params=_params(("parallel",)),
    )(c_all, dmod)


def adamw(w, g, m, v, name):
    shp = w.shape
    two = lambda a: a.reshape(-1, shp[-1])
    R, C = two(w).shape
    tr = _pick(R, (256, 128, 64, 32, 16, 8))
    bc1, bc2 = 1.0 - B1 ** STEP, 1.0 - B2 ** STEP

    def body(w_ref, g_ref, m_ref, v_ref, d_ref, mo_ref, vo_ref):
        gv = g_ref[...]
        mn = B1 * m_ref[...] + (1.0 - B1) * gv
        vn = B2 * v_ref[...] + (1.0 - B2) * (gv * gv)
        d_ref[...] = -LR * ((mn / bc1) / (jnp.sqrt(vn / bc2) + AEPS) + WD * w_ref[...])
        mo_ref[...] = mn
        vo_ref[...] = vn

    t = pl.BlockSpec((tr, C), lambda i: (i, 0))
    outs = pl.pallas_call(
        body, name=name, grid=(R // tr,),
        in_specs=[t] * 4, out_specs=[t] * 3, out_shape=[jax.ShapeDtypeStruct((R, C), f32)] * 3,
        compiler_params=_params(("parallel",)),
    )(two(w), two(g), two(m), two(v))
    return [o.reshape(shp) for o in outs]


def _pack(arrs):
    parts, offs, r0 = [], [], 0
    for a in arrs:
        n = a.size
        rows = -(-n // 1024) * 8
        parts.append(jnp.pad(a.reshape(-1), (0, rows * 128 - n)).reshape(rows, 128))
        offs.append((r0, rows))
        r0 += rows
    return jnp.concatenate(parts, axis=0), offs


def _unpack(buf, offs, shapes):
    out = []
    for (r0, rows), shp in zip(offs, shapes):
        n = 1
        for d in shp:
            n *= d
        out.append(buf[..., r0:r0 + rows, :].reshape(buf.shape[:-2] + (rows * 128,))[..., :n].reshape(buf.shape[:-2] + tuple(shp)))
    return out


def kernel(x, c, norm_w, ada_w, ada_b, a_w_in, a_conv_w, a_A_log, a_dt_bias, a_norm_w, a_w_out, b_w_in, b_f_bias, b_qn_w, b_kn_w, b_w_out, final_norm_w, loss_target, m_norm_w, m_ada_w, m_ada_b, m_a_w_in, m_a_conv_w, m_a_A_log, m_a_dt_bias, m_a_norm_w, m_a_w_out, m_b_w_in, m_b_f_bias, m_b_qn_w, m_b_kn_w, m_b_w_out, m_final_norm_w, v_norm_w, v_ada_w, v_ada_b, v_a_w_in, v_a_conv_w, v_a_A_log, v_a_dt_bias, v_a_norm_w, v_a_w_out, v_b_w_in, v_b_f_bias, v_b_qn_w, v_b_kn_w, v_b_w_out, v_final_norm_w):
    weights = dict(norm_w=norm_w, ada_w=ada_w, ada_b=ada_b, a_w_in=a_w_in, a_conv_w=a_conv_w, a_A_log=a_A_log,
                   a_dt_bias=a_dt_bias, a_norm_w=a_norm_w, a_w_out=a_w_out, b_w_in=b_w_in, b_f_bias=b_f_bias,
                   b_qn_w=b_qn_w, b_kn_w=b_kn_w, b_w_out=b_w_out, final_norm_w=final_norm_w)
    m_in = dict(norm_w=m_norm_w, ada_w=m_ada_w, ada_b=m_ada_b, a_w_in=m_a_w_in, a_conv_w=m_a_conv_w, a_A_log=m_a_A_log,
                a_dt_bias=m_a_dt_bias, a_norm_w=m_a_norm_w, a_w_out=m_a_w_out, b_w_in=m_b_w_in, b_f_bias=m_b_f_bias,
                b_qn_w=m_b_qn_w, b_kn_w=m_b_kn_w, b_w_out=m_b_w_out, final_norm_w=m_final_norm_w)
    v_in = dict(norm_w=v_norm_w, ada_w=v_ada_w, ada_b=v_ada_b, a_w_in=v_a_w_in, a_conv_w=v_a_conv_w, a_A_log=v_a_A_log,
                a_dt_bias=v_a_dt_bias, a_norm_w=v_a_norm_w, a_w_out=v_a_w_out, b_w_in=v_b_w_in, b_f_bias=v_b_f_bias,
                b_qn_w=v_b_qn_w, b_kn_w=v_b_kn_w, b_w_out=v_b_w_out, final_norm_w=v_final_norm_w)
    xi, yi, ci = _me()
    me_b, me_k = 4 * xi + 2 * yi + ci, 2 * xi + yi
    cidx = ci.astype(jnp.int32).reshape(1)
    S = x.shape[1]
    depth, n_a, n_b = norm_w.shape[0], a_w_in.shape[0], b_w_in.shape[0]
    x0, tgt = x.reshape(S, D), loss_target.reshape(S, D)

    c_all = ag_small(jnp.pad(c, ((0, 7), (0, 0)))).reshape(8, 8, D)[:, 0]
    nloc = ada_w.shape[2]
    parts = ag_small(ada_mod(c_all, ada_w).reshape(depth * 8, nloc)).reshape(4, 2, depth, 8, nloc)[:, 0]
    mine = lax.dynamic_index_in_dim(parts, me_b, axis=2, keepdims=False)
    mod = jnp.transpose(mine, (1, 0, 2)).reshape(depth, 4 * nloc) + ada_b
    shift, scale, gate = (mod[:, k * D:(k + 1) * D] for k in range(3))

    g_ain, g_aout, g_bin, g_bout, g_conv = ag_chips(
        [a_w_in.astype(bf16), a_w_out.astype(bf16), b_w_in.astype(bf16), b_w_out.astype(bf16), a_conv_w])

    def cols(g, l, pad):
        w = jnp.transpose(g[:, l], (1, 0, 2)).reshape(g.shape[2], -1)
        return jnp.pad(w, ((0, 0), (0, pad)))

    w_ain = [cols(g_ain, l, G_INP - G_IN) for l in range(n_a)]
    w_bin = [cols(g_bin, l, F_INP - F_IN) for l in range(n_b)]
    w_aout = [g_aout[:, l].reshape(-1, D) for l in range(n_a)]
    w_bout = [g_bout[:, l].reshape(-1, D) for l in range(n_b)]
    conv = [cols(g_conv, l, 0) for l in range(n_a)]
    qw2 = [_row(jnp.tile(b_qn_w[l], 2)) for l in range(n_b)]
    kw2 = [_row(jnp.tile(b_kn_w[l], 2)) for l in range(n_b)]

    saved, xc = [], x0
    for i in range(depth):
        l = i // 2
        h = ln_mod(xc, _row(norm_w[i]), _row(scale[i]), _row(shift[i]))
        if i % 2 == 0:
            proj = matmul(h, w_ain[l], "nn", "mm_a_in")
            pre = gdn_pre(proj, conv[l], _row(a_A_log[l]), _row(a_dt_bias[l]))
            o, wv, at, tinv, vn, st = gdn_fwd(*pre)
            o2 = gdn_onorm(o, proj, _row(a_norm_w[l]))
            y, xn = out_proj(o2, w_aout[l], xc, _row(gate[i]), "out_proj_a")
            saved.append((xc, h, proj, o2, y, pre, (o, wv, at, tinv, vn, st)))
        else:
            proj = matmul(h, w_bin[l], "nn", "mm_b_in")
            qa, ka, vb = fox_pre(proj, _row(b_f_bias[l]), qw2[l], kw2[l])
            o, lse = fox_attn(qa, ka, vb)
            o2 = fox_gate(o, proj)
            y, xn = out_proj(o2, w_bout[l], xc, _row(gate[i]), "out_proj_b")
            saved.append((xc, h, proj, o2, y, (qa, ka, vb), (o, lse)))
        xc = xn
    dx, st_f = final_loss(xc, _row(final_norm_w), tgt)

    d_norm, d_mod = [None] * depth, [None] * depth
    d_ain, d_aout, d_bin, d_bout = [None] * n_a, [None] * n_a, [None] * n_b, [None] * n_b
    d_conv, d_alog, d_dtb, d_anw = [None] * n_a, [None] * n_a, [None] * n_a, [None] * n_a
    d_fb, d_qn, d_kn = [None] * n_b, [None] * n_b, [None] * n_b
    for i in reversed(range(depth)):
        l = i // 2
        xin, h, proj, o2, y, pre, res = saved[i]
        dy, st_g = gate_bwd(dx, y, _row(gate[i]))
        if i % 2 == 0:
            o, wv, at, tinv, vn, st = res
            do2 = matmul(dy, w_aout[l], "nt", "mm_a_do2")
            d_aout[l] = matmul(o2, dy, "tn", "mm_a_dwo")
            do, dz, st_o = gdn_onorm_bwd(do2, o, proj, _row(a_norm_w[l]))
            grads = gdn_bwd(do, *pre, wv, at, tinv, vn, st)
            dcv, dba, st_s = gdn_pre_bwd(proj, conv[l], _row(a_A_log[l]), _row(a_dt_bias[l]), *grads)
            dproj, dcw = gdn_conv_bwd(proj, conv[l], dcv, dz, dba)
            dh = matmul(dproj, w_ain[l], "nt", "mm_a_dh")
            d_ain[l] = matmul(h, dproj, "tn", "mm_a_dw")
            d_conv[l], d_alog[l], d_dtb[l], d_anw[l] = dcw[:CONV_K], st_s[0], st_s[1], st_o[0]
        else:
            qa, ka, vb = pre
            o, lse = res
            do2 = matmul(dy, w_bout[l], "nt", "mm_b_do2")
            d_bout[l] = matmul(o2, dy, "tn", "mm_b_dwo")
            do, dz, delta = fox_gate_bwd(do2, o, proj)
            dqa, dka, dv = fox_attn_bwd(qa, ka, vb, do, lse, delta)
            dproj, st_b = fox_pre_bwd(proj, _row(b_f_bias[l]), qw2[l], kw2[l], dqa, dka, dv, dz)
            dh = matmul(dproj, w_bin[l], "nt", "mm_b_dh")
            d_bin[l] = matmul(h, dproj, "tn", "mm_b_dw")
            d_fb[l], d_qn[l], d_kn[l] = st_b[2, :F_H], st_b[0, :F_HD] + st_b[0, F_HD:], st_b[1, :F_HD] + st_b[1, F_HD:]
        dx, st_n = ln_mod_bwd(xin, _row(norm_w[i]), _row(scale[i]), dh, dx)
        d_norm[i] = st_n[0]
        d_mod[i] = jnp.concatenate([st_n[2], st_n[1], st_g[0]])

    small = [jnp.stack(d_norm), jnp.stack(d_mod), jnp.stack(d_conv), jnp.stack(d_alog), jnp.stack(d_dtb), jnp.stack(d_anw),
             jnp.stack(d_fb), jnp.stack(d_qn), jnp.stack(d_kn), st_f[0], jnp.sum(st_f[1]).reshape(1)]
    shapes = [a.shape for a in small]
    buf, offs = _pack(small)
    gathered = ag_small(buf).reshape(8, buf.shape[0], 128)
    tot = _unpack(sum_leading(gathered, "sum_devices"), offs, shapes)
    g_norm, g_adab, g_convf, g_alog, g_dtb, g_anw, g_fb, g_qn, g_kn, g_fin, loss = tot
    dmod_all = _unpack(gathered, offs[1:2], shapes[1:2])[0]
    dmod_loc = lax.dynamic_slice_in_dim(dmod_all, me_k * nloc, nloc, axis=2)
    g_adaw = ada_w_grad(c_all, jnp.transpose(dmod_loc, (1, 0, 2)))
    g_conv_loc = lax.dynamic_slice_in_dim(g_convf, me_k * a_conv_w.shape[2], a_conv_w.shape[2], axis=2)

    def col_blocks(ds, width):
        per = [jnp.transpose(d[:, :width].reshape(d.shape[0], 4, width // 4), (1, 0, 2)) for d in ds]
        return jnp.concatenate(per, axis=1)

    def row_blocks(ds):
        per = [d.reshape(4, d.shape[0] // 4, D) for d in ds]
        return jnp.concatenate(per, axis=1)

    r_ain, r_aout, r_bin, r_bout = reduce_scatter(
        [col_blocks(d_ain, G_IN), row_blocks(d_aout), col_blocks(d_bin, F_IN), row_blocks(d_bout)], cidx)

    grads = dict(norm_w=g_norm, ada_w=g_adaw, ada_b=g_adab, a_w_in=r_ain.reshape(a_w_in.shape), a_conv_w=g_conv_loc,
                 a_A_log=g_alog, a_dt_bias=g_dtb, a_norm_w=g_anw, a_w_out=r_aout.reshape(a_w_out.shape),
                 b_w_in=r_bin.reshape(b_w_in.shape), b_f_bias=g_fb, b_qn_w=g_qn, b_kn_w=g_kn,
                 b_w_out=r_bout.reshape(b_w_out.shape), final_norm_w=g_fin)
    names = list(weights)
    upd = {n: adamw(weights[n], grads[n], m_in[n], v_in[n], "adamw_" + n) for n in names}
    return (loss.reshape(()), dx.reshape(x.shape), *[grads[n] for n in names], *[upd[n][0] for n in names],
            *[upd[n][1] for n in names], *[upd[n][2] for n in names])
```

```python
import functools

import jax
import jax.numpy as jnp
from jax import lax
from jax.experimental import pallas as pl
from jax.experimental.pallas import tpu as pltpu

f32, bf16 = jnp.float32, jnp.bfloat16
HI = lax.Precision.HIGHEST
MESH = pl.DeviceIdType.MESH

EPS = 1e-6
D = 1024
CHUNK = 64
GQK_H, GV_H, GHD = 8, 16, 128
G_CONV = 4096
G_Z0 = 4096
G_BA0 = 6144
G_IN, G_INP = 6176, 6272
CONV_K = 4
F_H, F_HD = 16, 64
F_W = 1024
F_F0 = 4096
F_IN, F_INP = 4112, 4224
LR, B1, B2, AEPS, WD, STEP = 0.001, 0.9, 0.999, 1e-08, 0.01, 10
NEG = -1e30
VMEM_LIMIT = 56 * 1024 * 1024


def _nn(a, b, prec=None):
    return lax.dot_general(a, b, (((1,), (0,)), ((), ())), preferred_element_type=f32, precision=prec)


def _nt(a, b, prec=None):
    return lax.dot_general(a, b, (((1,), (1,)), ((), ())), preferred_element_type=f32, precision=prec)


def _tn(a, b, prec=None):
    return lax.dot_general(a, b, (((0,), (0,)), ((), ())), preferred_element_type=f32, precision=prec)


def _iota(shape, axis):
    return lax.broadcasted_iota(jnp.int32, shape, axis)


def _sigmoid(x):
    return 0.5 * jnp.tanh(0.5 * x) + 0.5


def _softplus(x):
    return jnp.maximum(x, 0.0) + jnp.log(1.0 + jnp.exp(-jnp.abs(x)))


def _silu(x):
    return x * _sigmoid(x)


def _dsilu(x):
    s = _sigmoid(x)
    return s * (1.0 + x * (1.0 - s))


def _params(sem=None, vmem=VMEM_LIMIT):
    return pltpu.CompilerParams(dimension_semantics=sem, vmem_limit_bytes=vmem)


def _row(v):
    return v.reshape(1, -1)


class _Comm:
    def __init__(self, ins, out_shapes, sems, start, wait):
        self.ins, self.out_shapes, self.sems, self.start, self.wait = list(ins), list(out_shapes), list(sems), start, wait


def _call(body, *, name, grid, in_specs, out_specs, out_shape, scratch_shapes, sem, args, comm=None):
    if comm is None:
        outs = pl.pallas_call(body, name=name, grid=grid, in_specs=in_specs, out_specs=out_specs, out_shape=out_shape,
                              scratch_shapes=scratch_shapes, compiler_params=_params(sem))(*args)
        return outs, []
    n_in, n_out, n_s = len(in_specs), len(out_specs), len(scratch_shapes)
    n_ci, n_co = len(comm.ins), len(comm.out_shapes)

    def wrapped(*refs):
        core_in, c_in = refs[:n_in], refs[n_in:n_in + n_ci]
        o0 = n_in + n_ci
        core_out, c_out = refs[o0:o0 + n_out], refs[o0 + n_out:o0 + n_out + n_co]
        s0 = o0 + n_out + n_co
        core_s, c_sem = refs[s0:s0 + n_s], refs[s0 + n_s:]
        first = functools.reduce(jnp.logical_and, [pl.program_id(d) == 0 for d in range(len(grid))])
        last = functools.reduce(jnp.logical_and, [pl.program_id(d) == grid[d] - 1 for d in range(len(grid))])
        pl.when(first)(functools.partial(comm.start, c_in, c_out, c_sem))
        body(*core_in, *core_out, *core_s)
        pl.when(last)(functools.partial(comm.wait, c_in, c_out, c_sem))

    outs = pl.pallas_call(
        wrapped, name=name + "_x", grid=grid, in_specs=list(in_specs) + [_ANY] * n_ci,
        out_specs=list(out_specs) + [_ANY] * n_co, out_shape=list(out_shape) + comm.out_shapes,
        scratch_shapes=list(scratch_shapes) + comm.sems, compiler_params=_params(("arbitrary",) * len(grid)),
    )(*args, *comm.ins)
    return outs[:n_out], outs[n_out:]


def _comm_alone(comm, name):
    n_ci, n_co = len(comm.ins), len(comm.out_shapes)

    def body(*refs):
        c_in, c_out, c_sem = refs[:n_ci], refs[n_ci:n_ci + n_co], refs[n_ci + n_co:]
        comm.start(c_in, c_out, c_sem)
        comm.wait(c_in, c_out, c_sem)

    return pl.pallas_call(body, name=name, in_specs=[_ANY] * n_ci, out_specs=[_ANY] * n_co, out_shape=comm.out_shapes,
                          scratch_shapes=comm.sems)(*comm.ins)


def _pick(n, pref):
    for t in pref:
        if n % t == 0:
            return t
    return n


MM_VMEM_BUDGET = 44 * 1024 * 1024


def _mm_tiles(M, N, K):
    best = None
    for tk in [K] + [t for t in (2048, 1408, 1024, 896, 512, 384, 256, 128) if K % t == 0 and t < K]:
        for tm in (2048, 1024, 512, 256, 128):
            for tn in (1408, 1024, 896, 512, 384, 256, 128):
                if M % tm or N % tn:
                    continue
                nk = K // tk
                need = 2 * 2 * (tm * tk + tk * tn) + 2 * 4 * tm * tn + (4 * tm * tn if nk > 1 else 0)
                if need <= MM_VMEM_BUDGET:
                    cand = ((nk, -tm * tn), (tm, tn, tk))
                    best = cand if best is None or cand[0] < best[0] else best
    return best[1]


def matmul(a, b, mode, name, out_dtype=f32):
    if mode == "nn":
        (M, K), (_, N) = a.shape, b.shape
    elif mode == "nt":
        (M, K), (N, _) = a.shape, b.shape
    else:
        (K, M), (_, N) = a.shape, b.shape
    tm, tn, tk = _mm_tiles(M, N, K)
    nk = K // tk
    dot = {"nn": _nn, "nt": _nt, "tn": _tn}[mode]

    def body(a_ref, b_ref, o_ref, *acc):
        k = pl.program_id(2)
        part = dot(a_ref[...], b_ref[...])
        if nk == 1:
            o_ref[...] = part.astype(out_dtype)
        else:
            acc_ref = acc[0]

            @pl.when(k == 0)
            def _():
                acc_ref[...] = part

            @pl.when(k > 0)
            def _():
                acc_ref[...] += part

            @pl.when(k == nk - 1)
            def _():
                o_ref[...] = acc_ref[...].astype(out_dtype)

    a_spec = pl.BlockSpec((tk, tm), lambda i, j, k: (k, i)) if mode == "tn" else pl.BlockSpec((tm, tk), lambda i, j, k: (i, k))
    b_spec = pl.BlockSpec((tn, tk), lambda i, j, k: (j, k)) if mode == "nt" else pl.BlockSpec((tk, tn), lambda i, j, k: (k, j))
    return pl.pallas_call(
        body, name=name, grid=(M // tm, N // tn, nk),
        in_specs=[a_spec, b_spec], out_specs=pl.BlockSpec((tm, tn), lambda i, j, k: (i, j)),
        out_shape=jax.ShapeDtypeStruct((M, N), out_dtype),
        scratch_shapes=[] if nk == 1 else [pltpu.VMEM((tm, tn), f32)],
        compiler_params=_params(("parallel", "parallel", "arbitrary")),
    )(a, b)


def out_proj(o2, w, x, gate, name):
    S, K = o2.shape
    N = w.shape[1]
    tm, tn = 512, 512

    def body(a_ref, b_ref, x_ref, g_ref, y_ref, xn_ref):
        y = _nn(a_ref[...], b_ref[...])
        y_ref[...] = y
        xn_ref[...] = x_ref[...] + g_ref[...] * y

    return pl.pallas_call(
        body, name=name, grid=(S // tm, N // tn),
        in_specs=[pl.BlockSpec((tm, K), lambda i, j: (i, 0)), pl.BlockSpec((K, tn), lambda i, j: (0, j)),
                  pl.BlockSpec((tm, tn), lambda i, j: (i, j)), pl.BlockSpec((1, tn), lambda i, j: (0, j))],
        out_specs=[pl.BlockSpec((tm, tn), lambda i, j: (i, j))] * 2,
        out_shape=[jax.ShapeDtypeStruct((S, N), f32)] * 2,
        compiler_params=_params(("parallel", "parallel")),
    )(o2, w, x, gate)


def ln_mod(x, nw, scale, shift):
    S = x.shape[0]
    tm = 512

    def body(x_ref, nw_ref, sc_ref, sh_ref, h_ref):
        xv = x_ref[...]
        r = lax.rsqrt(jnp.mean(xv * xv, axis=-1, keepdims=True) + EPS)
        h_ref[...] = ((xv * r) * nw_ref[...] * (1.0 + sc_ref[...]) + sh_ref[...]).astype(bf16)

    vec = pl.BlockSpec((1, D), lambda i: (0, 0))
    return pl.pallas_call(
        body, name="ln_mod", grid=(S // tm,),
        in_specs=[pl.BlockSpec((tm, D), lambda i: (i, 0)), vec, vec, vec],
        out_specs=pl.BlockSpec((tm, D), lambda i: (i, 0)),
        out_shape=jax.ShapeDtypeStruct((S, D), bf16),
        compiler_params=_params(("parallel",)),
    )(x, nw, scale, shift)


def ln_mod_bwd(x, nw, scale, dh, dxres):
    S = x.shape[0]
    tm = 512
    nb = S // tm

    def body(x_ref, nw_ref, sc_ref, dh_ref, dr_ref, dx_ref, st_ref):
        i = pl.program_id(0)
        xv = x_ref[...]
        r = lax.rsqrt(jnp.mean(xv * xv, axis=-1, keepdims=True) + EPS)
        xn = xv * r
        dh = dh_ref[...]
        dxn = dh * (nw_ref[...] * (1.0 + sc_ref[...]))
        dx_ref[...] = dr_ref[...] + r * (dxn - xn * jnp.mean(dxn * xn, axis=-1, keepdims=True))
        p1 = jnp.sum(dh * xn, axis=0, keepdims=True)
        p2 = jnp.sum(dh, axis=0, keepdims=True)
        upd = jnp.concatenate([p1, p1, p2, jnp.zeros((5, D), f32)], axis=0)

        @pl.when(i == 0)
        def _():
            st_ref[...] = upd

        @pl.when(i > 0)
        def _():
            st_ref[...] += upd

        @pl.when(i == nb - 1)
        def _():
            st_ref[0:1, :] = st_ref[0:1, :] * (1.0 + sc_ref[...])
            st_ref[1:2, :] = st_ref[1:2, :] * nw_ref[...]

    vec = pl.BlockSpec((1, D), lambda i: (0, 0))
    tile = pl.BlockSpec((tm, D), lambda i: (i, 0))
    return pl.pallas_call(
        body, name="ln_mod_bwd", grid=(S // tm,),
        in_specs=[tile, vec, vec, tile, tile],
        out_specs=[tile, pl.BlockSpec((8, D), lambda i: (0, 0))],
        out_shape=[jax.ShapeDtypeStruct((S, D), f32), jax.ShapeDtypeStruct((8, D), f32)],
        compiler_params=_params(("arbitrary",)),
    )(x, nw, scale, dh, dxres)


def final_loss(x, fw, tgt):
    S = x.shape[0]
    tm = 512

    def body(x_ref, w_ref, t_ref, dx_ref, st_ref):
        i = pl.program_id(0)
        xv = x_ref[...]
        r = lax.rsqrt(jnp.mean(xv * xv, axis=-1, keepdims=True) + EPS)
        xn = xv * r
        err = xn * w_ref[...] - t_ref[...]
        dy = err * (1.0 / D)
        dxn = dy * w_ref[...]
        dx_ref[...] = r * (dxn - xn * jnp.mean(dxn * xn, axis=-1, keepdims=True))
        p1 = jnp.sum(dy * xn, axis=0, keepdims=True)
        p2 = jnp.sum(err * err, axis=0, keepdims=True) * (0.5 / D)
        upd = jnp.concatenate([p1, p2, jnp.zeros((6, D), f32)], axis=0)

        @pl.when(i == 0)
        def _():
            st_ref[...] = upd

        @pl.when(i > 0)
        def _():
            st_ref[...] += upd

    tile = pl.BlockSpec((tm, D), lambda i: (i, 0))
    return pl.pallas_call(
        body, name="final_loss", grid=(S // tm,),
        in_specs=[tile, pl.BlockSpec((1, D), lambda i: (0, 0)), tile],
        out_specs=[tile, pl.BlockSpec((8, D), lambda i: (0, 0))],
        out_shape=[jax.ShapeDtypeStruct((S, D), f32), jax.ShapeDtypeStruct((8, D), f32)],
        compiler_params=_params(("arbitrary",)),
    )(x, fw, tgt)


def gate_bwd(dx, y, gate):
    S = dx.shape[0]
    tm = 512

    def body(dx_ref, y_ref, g_ref, dy_ref, st_ref):
        i = pl.program_id(0)
        dxv = dx_ref[...]
        dy_ref[...] = (g_ref[...] * dxv).astype(bf16)
        upd = jnp.concatenate([jnp.sum(dxv * y_ref[...], axis=0, keepdims=True), jnp.zeros((7, D), f32)], axis=0)

        @pl.when(i == 0)
        def _():
            st_ref[...] = upd

        @pl.when(i > 0)
        def _():
            st_ref[...] += upd

    tile = pl.BlockSpec((tm, D), lambda i: (i, 0))
    return pl.pallas_call(
        body, name="gate_bwd", grid=(S // tm,),
        in_specs=[tile, tile, pl.BlockSpec((1, D), lambda i: (0, 0))],
        out_specs=[tile, pl.BlockSpec((8, D), lambda i: (0, 0))],
        out_shape=[jax.ShapeDtypeStruct((S, D), bf16), jax.ShapeDtypeStruct((8, D), f32)],
        compiler_params=_params(("arbitrary",)),
    )(dx, y, gate)


def _chunk_mats(tm):
    r, c = _iota((tm, tm), 0), _iota((tm, tm), 1)
    same = jnp.right_shift(r, 6) == jnp.right_shift(c, 6)
    ltri = jnp.where(same & (c <= r), 1.0, 0.0).astype(f32)
    utri = jnp.where(same & (c >= r), 1.0, 0.0).astype(f32)
    bsame = jnp.where(same, 1.0, 0.0).astype(f32)
    return ltri, utri, bsame


def _gdn_scalars(ba, alog, dtb, ltri, bsame):
    beta = _sigmoid(ba[:, 0:16])
    u = ba[:, 16:32] + dtb
    neg_a = -jnp.exp(alog)
    g = neg_a * _softplus(u)
    gc = _nn(ltri, g, HI)
    glast = _nn(bsame, g, HI)
    return beta, u, neg_a, g, gc, glast


def _conv_taps(p_ref, halo_ref, first, gi):
    cs = slice(gi * 128, (gi + 1) * 128)
    cur = p_ref[:, cs]
    hal = jnp.where(first, 0.0, halo_ref[:, cs])
    ext = jnp.concatenate([hal, cur], axis=0)
    return [cur] + [pltpu.roll(ext, s, 0)[8:] for s in range(1, CONV_K)]


def _conv(taps, w):
    cv = taps[0] * w[3:4]
    for s in range(1, CONV_K):
        cv = cv + taps[s] * w[3 - s:4 - s]
    return cv


def _l2n(x):
    return x * lax.rsqrt(jnp.sum(x * x, axis=-1, keepdims=True) + EPS)


def _gdn_in_specs(tm, S):
    nb8 = tm // 8
    return [pl.BlockSpec((tm, G_CONV), lambda i: (i, 0)),
            pl.BlockSpec((8, G_CONV), lambda i: (jnp.maximum(i * nb8 - 1, 0), 0)),
            pl.BlockSpec((tm, 128), lambda i: (i, G_BA0 // 128))]


def gdn_pre(proj, conv_w, alog, dtb):
    S = proj.shape[0]
    tm = 256
    nch = tm // CHUNK

    def body(p_ref, halo_ref, ba_ref, w_ref, al_ref, dt_ref,
             q_ref, k_ref, kb_ref, kbg_ref, vb_ref, qd_ref, kd_ref, d_ref, gl_ref):
        first = pl.program_id(0) == 0
        ltri, _, bsame = _chunk_mats(tm)
        beta, _, _, _, gc, glast = _gdn_scalars(ba_ref[...], al_ref[...], dt_ref[...], ltri, bsame)
        eg, ek, egl = jnp.exp(gc), jnp.exp(glast - gc), jnp.exp(glast)
        eye = jnp.where(_iota((16, 16), 0) == _iota((16, 16), 1), 1.0, 0.0).astype(f32)
        gct = _nt(eye, gc, HI)
        low = _iota((CHUNK, CHUNK), 0) >= _iota((CHUNK, CHUNK), 1)

        def act(gi):
            return _silu(_conv(_conv_taps(p_ref, halo_ref, first, gi), w_ref[:, gi * 128:(gi + 1) * 128]))

        for j in range(GQK_H):
            js = slice(j * 128, (j + 1) * 128)
            qn = _l2n(act(j)) * (GHD ** -0.5)
            kn = _l2n(act(GQK_H + j))
            q_ref[:, js] = qn.astype(bf16)
            k_ref[:, js] = kn.astype(bf16)
            for e in range(2):
                h = 2 * j + e
                hs = slice(h * 128, (h + 1) * 128)
                v = act(2 * GQK_H + h)
                bh, egh, ekh = beta[:, h:h + 1], eg[:, h:h + 1], ek[:, h:h + 1]
                kbv = kn * bh
                kb_ref[:, hs] = kbv.astype(bf16)
                kbg_ref[:, hs] = (kbv * egh).astype(bf16)
                vb_ref[:, hs] = (v * bh).astype(bf16)
                qd_ref[:, hs] = (qn * egh).astype(bf16)
                kd_ref[:, hs] = (kn * ekh).astype(bf16)
                for c in range(nch):
                    rs = slice(c * CHUNK, (c + 1) * CHUNK)
                    diff = gc[rs, h:h + 1] - gct[h:h + 1, rs]
                    d_ref[rs, h * CHUNK:(h + 1) * CHUNK] = jnp.where(low, jnp.exp(jnp.where(low, diff, 0.0)), 0.0)
                    gl_ref[c * 8:(c + 1) * 8, hs] = jnp.broadcast_to(egl[c * CHUNK:c * CHUNK + 8, h:h + 1], (8, 128))

    full = lambda shape: pl.BlockSpec(shape, lambda i: (0, 0))
    t1 = pl.BlockSpec((tm, 1024), lambda i: (i, 0))
    t2 = pl.BlockSpec((tm, 2048), lambda i: (i, 0))
    sd = jax.ShapeDtypeStruct
    return pl.pallas_call(
        body, name="gdn_pre", grid=(S // tm,),
        in_specs=_gdn_in_specs(tm, S) + [full((CONV_K, G_CONV)), full((1, 16)), full((1, 16))],
        out_specs=[t1, t1, t2, t2, t2, t2, t2, t1, pl.BlockSpec((tm // 8, 2048), lambda i: (i, 0))],
        out_shape=[sd((S, 1024), bf16)] * 2 + [sd((S, 2048), bf16)] * 5 + [sd((S, 1024), f32), sd((S // 8, 2048), f32)],
        compiler_params=_params(("parallel",)),
    )(proj, proj, proj, conv_w, alog, dtb)


def _bnn(a, b):
    return lax.dot_general(a, b, (((2,), (1,)), ((0,), (0,))), preferred_element_type=f32)


def _bnt(a, b):
    return lax.dot_general(a, b, (((2,), (2,)), ((0,), (0,))), preferred_element_type=f32)


def _btn(a, b):
    return lax.dot_general(a, b, (((1,), (1,)), ((0,), (0,))), preferred_element_type=f32)


def _split(a):
    hi = a.astype(bf16)
    return hi, (a - hi.astype(f32)).astype(bf16)


def _cat3(h, l, axis, lhs):
    return jnp.concatenate([h, h, l] if lhs else [h, l, h], axis=axis)


def _tri_inv_b(L):
    eye = jnp.where(_iota((1, CHUNK, CHUNK), 1) == _iota((1, CHUNK, CHUNK), 2), 1.0, 0.0).astype(f32)
    P = -L
    T = eye + P
    ph, pl_ = _split(P)
    for _ in range(5):
        P = _bnn(_cat3(ph, pl_, 2, True), _cat3(ph, pl_, 1, False))
        ph, pl_ = _split(P)
        th, tl = _split(T)
        T = T + _bnn(_cat3(th, tl, 2, True), _cat3(ph, pl_, 1, False))
    return T


GTB = 512


def _gdn_slices(ncb):
    pairs = [(c, e) for c in range(ncb) for e in range(2)]
    rs = lambda c: slice(c * CHUNK, (c + 1) * CHUNK)
    cs = lambda e: slice(e * 128, (e + 1) * 128)
    ds_ = lambda e: slice(e * CHUNK, (e + 1) * CHUNK)
    return pairs, rs, cs, ds_


def gdn_fwd(q, k, kb, kbg, vb, qd, kd, dm, gl8, comm=None):
    S = q.shape[0]
    nb, ncb = S // GTB, GTB // CHUNK
    pairs, rs, cs, ds_ = _gdn_slices(ncb)

    def body(q_ref, k_ref, kb_ref, kbg_ref, vb_ref, qd_ref, kd_ref, d_ref, gl_ref,
             o_ref, w_ref, at_ref, t_ref, vn_ref, st_ref, state, u_scr):
        @pl.when(pl.program_id(1) == 0)
        def _():
            state[...] = jnp.zeros_like(state)

        stk = lambda ref, lanes: jnp.stack([ref[rs(c), lanes(e)] for c, e in pairs])
        kq = jnp.stack([k_ref[rs(c), :] for c, _ in pairs])
        dmat = stk(d_ref, ds_)
        strict = _iota((1, CHUNK, CHUNK), 1) > _iota((1, CHUNK, CHUNK), 2)
        T = _tri_inv_b(jnp.where(strict, _bnt(stk(kb_ref, cs), kq) * dmat, 0.0))
        tb = T.astype(bf16)
        u_scr[...] = _bnn(tb, stk(vb_ref, cs))
        wb = _bnn(tb, stk(kbg_ref, cs)).astype(bf16)
        qk = _bnt(jnp.stack([q_ref[rs(c), :] for c in range(ncb)]), jnp.stack([k_ref[rs(c), :] for c in range(ncb)]))
        for b, (c, e) in enumerate(pairs):
            w_ref[rs(c), cs(e)] = wb[b]
            at_ref[rs(c), ds_(e)] = (qk[c] * dmat[b]).astype(bf16)
            t_ref[rs(c), ds_(e)] = T[b]
        for b, (c, e) in enumerate(pairs):
            sb = state[e].astype(bf16)
            vnb = (u_scr[b] - _nn(w_ref[rs(c), cs(e)], sb)).astype(bf16)
            o_ref[rs(c), cs(e)] = _nn(qd_ref[rs(c), cs(e)], sb) + _nn(at_ref[rs(c), ds_(e)], vnb)
            st_ref[c * 128:(c + 1) * 128, cs(e)] = sb
            state[e] = state[e] * gl_ref[c * 8:c * 8 + 1, cs(e)] + _tn(kd_ref[rs(c), cs(e)], vnb)
            vn_ref[rs(c), cs(e)] = vnb

    b1 = pl.BlockSpec((GTB, 128), lambda j, i: (i, j))
    b2 = pl.BlockSpec((GTB, 256), lambda j, i: (i, j))
    sd = jax.ShapeDtypeStruct
    return _call(
        body, name="gdn_fwd", grid=(GQK_H, nb),
        in_specs=[b1, b1, b2, b2, b2, b2, b2, b1, pl.BlockSpec((GTB // 8, 256), lambda j, i: (i, j))],
        out_specs=[b2, b2, b1, b1, b2, pl.BlockSpec((ncb * 128, 256), lambda j, i: (i, j))],
        out_shape=[sd((S, 2048), f32), sd((S, 2048), bf16), sd((S, 1024), bf16), sd((S, 1024), f32),
                   sd((S, 2048), bf16), sd((S // CHUNK * 128, 2048), bf16)],
        scratch_shapes=[pltpu.VMEM((2, 128, 128), f32), pltpu.VMEM((2 * ncb, CHUNK, 128), f32)],
        sem=("parallel", "arbitrary"), args=(q, k, kb, kbg, vb, qd, kd, dm, gl8), comm=comm)


def gdn_bwd(do, q, k, kb, kbg, vb, qd, kd, dm, gl8, w, at, T, vn, st, comm=None):
    S = q.shape[0]
    nb, ncb = S // GTB, GTB // CHUNK
    pairs, rs, cs, ds_ = _gdn_slices(ncb)

    def body(do_ref, q_ref, k_ref, kb_ref, kbg_ref, vb_ref, qd_ref, kd_ref, d_ref, gl_ref, w_ref, at_ref, t_ref, vn_ref, st_ref,
             dq_ref, dk_ref, dkb_ref, dkbg_ref, dvb_ref, dqd_ref, dkd_ref, dgc_ref, dstate, dvn_scr, dw_scr, dat_scr, dgl_scr):
        @pl.when(pl.program_id(1) == 0)
        def _():
            dstate[...] = jnp.zeros_like(dstate)

        for b, (c, e) in reversed(list(enumerate(pairs))):
            dob = do_ref[rs(c), cs(e)].astype(bf16)
            sb = st_ref[c * 128:(c + 1) * 128, cs(e)]
            vnb = vn_ref[rs(c), cs(e)]
            gl = gl_ref[c * 8:c * 8 + 1, cs(e)]
            dS = dstate[e]
            dsb = dS.astype(bf16)
            dvnb = (_tn(at_ref[rs(c), ds_(e)], dob) + _nn(kd_ref[rs(c), cs(e)], dsb)).astype(bf16)
            dvn_scr[b] = dvnb
            dat_scr[b] = _nt(dob, vnb)
            dqd_ref[rs(c), cs(e)] = _nt(dob, sb)
            dkd_ref[rs(c), cs(e)] = _nt(vnb, dsb)
            dw_scr[b] = (-_nt(dvnb, sb)).astype(bf16)
            dgl = jnp.sum(jnp.sum(dS * sb.astype(f32), axis=1, keepdims=True), axis=0, keepdims=True)
            dgl_scr[b] = jnp.broadcast_to(dgl * gl, (8, 128))
            dstate[e] = gl * dS + _tn(qd_ref[rs(c), cs(e)], dob) - _tn(w_ref[rs(c), cs(e)], dvnb)

        stk = lambda ref, lanes: jnp.stack([ref[rs(c), lanes(e)] for c, e in pairs])
        kq = jnp.stack([k_ref[rs(c), :] for c, _ in pairs])
        qq = jnp.stack([q_ref[rs(c), :] for c, _ in pairs])
        kbb = stk(kb_ref, cs)
        Tm = stk(t_ref, ds_)
        tb = Tm.astype(bf16)
        dvn, dw = dvn_scr[...], dw_scr[...]
        dT = _bnt(dvn, stk(vb_ref, cs)) + _bnt(dw, stk(kbg_ref, cs))
        dvb, dkbg = _btn(tb, dvn), _btn(tb, dw)
        th, tl = _split(Tm)
        xh, xl = _split(_bnt(_cat3(*_split(dT), 2, True), _cat3(th, tl, 2, False)))
        dL = -_btn(_cat3(th, tl, 1, True), _cat3(xh, xl, 1, False))
        dmat = stk(d_ref, ds_)
        strict = _iota((1, CHUNK, CHUNK), 1) > _iota((1, CHUNK, CHUNK), 2)
        dA = jnp.where(strict, dL * dmat, 0.0)
        dB = dat_scr[...] * dmat
        dAb, dBb = dA.astype(bf16), dB.astype(bf16)
        dkb = _bnn(dAb, kq)
        dkc = _btn(dAb, kbb) + _btn(dBb, qq)
        dqc = _bnn(dBb, kq)
        M = dA * _bnt(kbb, kq) + dB * _bnt(qq, kq)
        mh, ml = _split(M)
        colsum = _btn(jnp.concatenate([mh, ml], axis=1), jnp.ones((2 * ncb, 2 * CHUNK, 128), bf16))
        lastrow = _iota((1, CHUNK, 128), 1) == CHUNK - 1
        for b, (c, e) in enumerate(pairs):
            dvb_ref[rs(c), cs(e)] = dvb[b]
            dkbg_ref[rs(c), cs(e)] = dkbg[b]
            dkb_ref[rs(c), cs(e)] = dkb[b]
            dgc_ref[rs(c), cs(e)] = (jnp.sum(M[b], axis=1, keepdims=True) - colsum[b]
                                     + jnp.where(lastrow[0], dgl_scr[b][0:1, :], 0.0))
        for c in range(ncb):
            dq_ref[rs(c), :] = dqc[2 * c] + dqc[2 * c + 1]
            dk_ref[rs(c), :] = dkc[2 * c] + dkc[2 * c + 1]

    b1 = pl.BlockSpec((GTB, 128), lambda j, i: (nb - 1 - i, j))
    b2 = pl.BlockSpec((GTB, 256), lambda j, i: (nb - 1 - i, j))
    sd = jax.ShapeDtypeStruct
    return _call(
        body, name="gdn_bwd", grid=(GQK_H, nb),
        in_specs=[b2, b1, b1, b2, b2, b2, b2, b2, b1, pl.BlockSpec((GTB // 8, 256), lambda j, i: (nb - 1 - i, j)),
                  b2, b1, b1, b2, pl.BlockSpec((ncb * 128, 256), lambda j, i: (nb - 1 - i, j))],
        out_specs=[b1, b1, b2, b2, b2, b2, b2, b2],
        out_shape=[sd((S, 1024), f32)] * 2 + [sd((S, 2048), f32)] * 6,
        scratch_shapes=[pltpu.VMEM((2, 128, 128), f32), pltpu.VMEM((2 * ncb, CHUNK, 128), bf16),
                        pltpu.VMEM((2 * ncb, CHUNK, 128), bf16), pltpu.VMEM((2 * ncb, CHUNK, CHUNK), f32),
                        pltpu.VMEM((2 * ncb, 8, 128), f32)],
        sem=("parallel", "arbitrary"), args=(do, q, k, kb, kbg, vb, qd, kd, dm, gl8, w, at, T, vn, st), comm=comm)


def gdn_onorm(o, proj, nw):
    S = o.shape[0]
    tm = 256

    def body(o_ref, z_ref, nw_ref, o2_ref):
        for h in range(GV_H):
            hs = slice(h * 128, (h + 1) * 128)
            oh = o_ref[:, hs]
            r = lax.rsqrt(jnp.mean(oh * oh, axis=-1, keepdims=True) + EPS)
            o2_ref[:, hs] = (((oh * r) * nw_ref[...]) * _silu(z_ref[:, hs])).astype(bf16)

    t2 = pl.BlockSpec((tm, 2048), lambda i: (i, 0))
    return pl.pallas_call(
        body, name="gdn_onorm", grid=(S // tm,),
        in_specs=[t2, pl.BlockSpec((tm, 2048), lambda i: (i, G_Z0 // 2048)), pl.BlockSpec((1, 128), lambda i: (0, 0))],
        out_specs=t2, out_shape=jax.ShapeDtypeStruct((S, 2048), bf16),
        compiler_params=_params(("parallel",)),
    )(o, proj, nw)


def gdn_onorm_bwd(do2, o, proj, nw):
    S = o.shape[0]
    tm = 256

    def body(d_ref, o_ref, z_ref, nw_ref, do_ref, dz_ref, st_ref):
        i = pl.program_id(0)
        acc = jnp.zeros((1, 128), f32)
        for h in range(GV_H):
            hs = slice(h * 128, (h + 1) * 128)
            oh, z, d2 = o_ref[:, hs], z_ref[:, hs], d_ref[:, hs]
            r = lax.rsqrt(jnp.mean(oh * oh, axis=-1, keepdims=True) + EPS)
            on = oh * r
            dt = d2 * _silu(z)
            dz_ref[:, hs] = (d2 * (on * nw_ref[...]) * _dsilu(z)).astype(bf16)
            don = dt * nw_ref[...]
            acc = acc + jnp.sum(dt * on, axis=0, keepdims=True)
            do_ref[:, hs] = r * (don - on * jnp.mean(don * on, axis=-1, keepdims=True))
        upd = jnp.concatenate([acc, jnp.zeros((7, 128), f32)], axis=0)

        @pl.when(i == 0)
        def _():
            st_ref[...] = upd

        @pl.when(i > 0)
        def _():
            st_ref[...] += upd

    t2 = pl.BlockSpec((tm, 2048), lambda i: (i, 0))
    sd = jax.ShapeDtypeStruct
    return pl.pallas_call(
        body, name="gdn_onorm_bwd", grid=(S // tm,),
        in_specs=[t2, t2, pl.BlockSpec((tm, 2048), lambda i: (i, G_Z0 // 2048)), pl.BlockSpec((1, 128), lambda i: (0, 0))],
        out_specs=[t2, t2, pl.BlockSpec((8, 128), lambda i: (0, 0))],
        out_shape=[sd((S, 2048), f32), sd((S, 2048), bf16), sd((8, 128), f32)],
        compiler_params=_params(("arbitrary",)),
    )(do2, o, proj, nw)


def gdn_pre_bwd(proj, conv_w, alog, dtb, dq, dk, dkb, dkbg, dvb, dqd, dkd, dgcd):
    S = proj.shape[0]
    tm = 128

    def body(p_ref, halo_ref, ba_ref, w_ref, al_ref, dt_ref, dq_ref, dk_ref, dkb_ref, dkbg_ref, dvb_ref, dqd_ref, dkd_ref, dgc_ref,
             dcv_ref, dba_ref, st_ref):
        i = pl.program_id(0)
        first = i == 0
        ltri, utri, bsame = _chunk_mats(tm)
        beta, u, neg_a, g, gc, glast = _gdn_scalars(ba_ref[...], al_ref[...], dt_ref[...], ltri, bsame)
        eg, ek = jnp.exp(gc), jnp.exp(glast - gc)
        lane16 = _iota((tm, 16), 1)
        dgc_all = jnp.zeros((tm, 16), f32)
        rkd_all = jnp.zeros((tm, 16), f32)
        dbeta_all = jnp.zeros((tm, 16), f32)

        def pre(gi):
            return _conv(_conv_taps(p_ref, halo_ref, first, gi), w_ref[:, gi * 128:(gi + 1) * 128])

        def l2n_bwd(xt, dy):
            r = lax.rsqrt(jnp.sum(xt * xt, axis=-1, keepdims=True) + EPS)
            y = xt * r
            return r * (dy - y * jnp.sum(dy * y, axis=-1, keepdims=True))

        for j in range(GQK_H):
            js = slice(j * 128, (j + 1) * 128)
            cvq, cvk = pre(j), pre(GQK_H + j)
            qt, kt = _silu(cvq), _silu(cvk)
            qn = _l2n(qt) * (GHD ** -0.5)
            kn = _l2n(kt)
            dq_tot, dk_tot = dq_ref[:, js], dk_ref[:, js]
            for e in range(2):
                h = 2 * j + e
                hs = slice(h * 128, (h + 1) * 128)
                gv = 2 * GQK_H + h
                cvv = pre(gv)
                v = _silu(cvv)
                bh, egh, ekh = beta[:, h:h + 1], eg[:, h:h + 1], ek[:, h:h + 1]
                dkbg, dkd, dqd, dvb = dkbg_ref[:, hs], dkd_ref[:, hs], dqd_ref[:, hs], dvb_ref[:, hs]
                dkb_t = dkb_ref[:, hs] + dkbg * egh
                dk_tot = dk_tot + dkb_t * bh + dkd * ekh
                dq_tot = dq_tot + dqd * egh
                dcv_ref[:, gv * 128:(gv + 1) * 128] = (dvb * bh) * _dsilu(cvv)
                dbeta = jnp.sum(dkb_t * kn, axis=-1, keepdims=True) + jnp.sum(dvb * v, axis=-1, keepdims=True)
                rkd = jnp.sum(dkd * (kn * ekh), axis=-1, keepdims=True)
                dgc = (dgc_ref[:, hs][:, 0:1] + jnp.sum(dkbg * (kn * bh * egh), axis=-1, keepdims=True)
                       + jnp.sum(dqd * (qn * egh), axis=-1, keepdims=True) - rkd)
                sel = lane16 == h
                dgc_all = dgc_all + jnp.where(sel, dgc, 0.0)
                rkd_all = rkd_all + jnp.where(sel, rkd, 0.0)
                dbeta_all = dbeta_all + jnp.where(sel, dbeta, 0.0)
            dcv_ref[:, js] = l2n_bwd(qt, dq_tot * (GHD ** -0.5)) * _dsilu(cvq)
            ks = slice((GQK_H + j) * 128, (GQK_H + j + 1) * 128)
            dcv_ref[:, ks] = l2n_bwd(kt, dk_tot) * _dsilu(cvk)

        islast = jnp.bitwise_and(_iota((tm, 16), 0), CHUNK - 1) == CHUNK - 1
        dgc_all = dgc_all + jnp.where(islast, _nn(bsame, rkd_all, HI), 0.0)
        dg = _nn(utri, dgc_all, HI)
        da = dg * neg_a * _sigmoid(u)
        db = dbeta_all * beta * (1.0 - beta)
        r16, c128 = _iota((16, 128), 0), _iota((16, 128), 1)
        pb = jnp.where(c128 == r16, 1.0, 0.0).astype(f32)
        pa = jnp.where(c128 == r16 + 16, 1.0, 0.0).astype(f32)
        dba_ref[...] = _nn(db, pb, HI) + _nn(da, pa, HI)
        upd = jnp.concatenate([jnp.sum(dg * g, axis=0, keepdims=True), jnp.sum(da, axis=0, keepdims=True),
                               jnp.zeros((6, 16), f32)], axis=0)

        @pl.when(i == 0)
        def _():
            st_ref[...] = upd

        @pl.when(i > 0)
        def _():
            st_ref[...] += upd

    full = lambda shape: pl.BlockSpec(shape, lambda i: (0, 0))
    t1 = pl.BlockSpec((tm, 1024), lambda i: (i, 0))
    t2 = pl.BlockSpec((tm, 2048), lambda i: (i, 0))
    sd = jax.ShapeDtypeStruct
    return pl.pallas_call(
        body, name="gdn_pre_bwd", grid=(S // tm,),
        in_specs=_gdn_in_specs(tm, S) + [full((CONV_K, G_CONV)), full((1, 16)), full((1, 16)), t1, t1] + [t2] * 6,
        out_specs=[pl.BlockSpec((tm, G_CONV), lambda i: (i, 0)), pl.BlockSpec((tm, 128), lambda i: (i, 0)), full((8, 16))],
        out_shape=[sd((S, G_CONV), f32), sd((S, 128), f32), sd((8, 16), f32)],
        compiler_params=_params(("arbitrary",)),
    )(proj, proj, proj, conv_w, alog, dtb, dq, dk, dkb, dkbg, dvb, dqd, dkd, dgcd)


def gdn_conv_bwd(proj, conv_w, dcv, dz, dba):
    S = proj.shape[0]
    tm = 256
    nb, nb8 = S // tm, tm // 8

    def body(p_ref, halo_ref, w_ref, dcv_ref, nxt_ref, dz_ref, dba_ref, dp_ref, dw_ref):
        i = pl.program_id(0)
        first, last = i == 0, i == nb - 1
        for gi in range(G_CONV // 128):
            cs = slice(gi * 128, (gi + 1) * 128)
            taps = _conv_taps(p_ref, halo_ref, first, gi)
            cur = dcv_ref[:, cs]
            ext = jnp.concatenate([cur, jnp.where(last, 0.0, nxt_ref[:, cs])], axis=0)
            w = w_ref[:, cs]
            dp = cur * w[3:4]
            rows = [jnp.sum(cur * taps[3 - kk], axis=0, keepdims=True) for kk in range(CONV_K)]
            for s in range(1, CONV_K):
                dp = dp + pltpu.roll(ext, tm + 8 - s, 0)[:tm] * w[3 - s:4 - s]
            dp_ref[:, cs] = dp.astype(bf16)
            upd = jnp.concatenate(rows + [jnp.zeros((4, 128), f32)], axis=0)

            @pl.when(first)
            def _():
                dw_ref[:, cs] = upd

            @pl.when(i > 0)
            def _():
                dw_ref[:, cs] += upd

        dp_ref[:, G_Z0:G_BA0] = dz_ref[...]
        dp_ref[:, G_BA0:G_INP] = dba_ref[...].astype(bf16)

    sd = jax.ShapeDtypeStruct
    return pl.pallas_call(
        body, name="gdn_conv_bwd", grid=(nb,),
        in_specs=[pl.BlockSpec((tm, G_CONV), lambda i: (i, 0)),
                  pl.BlockSpec((8, G_CONV), lambda i: (jnp.maximum(i * nb8 - 1, 0), 0)),
                  pl.BlockSpec((CONV_K, G_CONV), lambda i: (0, 0)),
                  pl.BlockSpec((tm, G_CONV), lambda i: (i, 0)),
                  pl.BlockSpec((8, G_CONV), lambda i: (jnp.minimum((i + 1) * nb8, S // 8 - 1), 0)),
                  pl.BlockSpec((tm, 2048), lambda i: (i, 0)), pl.BlockSpec((tm, 128), lambda i: (i, 0))],
        out_specs=[pl.BlockSpec((tm, G_INP), lambda i: (i, 0)), pl.BlockSpec((8, G_CONV), lambda i: (0, 0))],
        out_shape=[sd((S, G_INP), bf16), sd((8, G_CONV), f32)],
        compiler_params=_params(("arbitrary",)),
    )(proj, proj, conv_w, dcv, dcv, dz, dba)


def _half_mean(t, lo_half):
    m0 = jnp.sum(jnp.where(lo_half, t, 0.0), axis=-1, keepdims=True)
    m1 = jnp.sum(jnp.where(lo_half, 0.0, t), axis=-1, keepdims=True)
    return jnp.where(lo_half, m0, m1) * (1.0 / F_HD)


def _split3(c):
    hi = c.astype(bf16).astype(f32)
    mid = (c - hi).astype(bf16).astype(f32)
    lo = (c - hi - mid).astype(bf16).astype(f32)
    return hi, mid, lo


def fox_pre(proj, fbias, qw2, kw2):
    S = proj.shape[0]
    tm = 256

    def body(q_ref, k_ref, v_ref, f_ref, fb_ref, qw_ref, kw_ref, qa_ref, ka_ref, vb_ref, carry):
        @pl.when(pl.program_id(0) == 0)
        def _():
            carry[...] = jnp.zeros_like(carry)

        logf = -_softplus(-(f_ref[:, 0:16] + fb_ref[...]))
        ltri = jnp.where(_iota((tm, tm), 1) <= _iota((tm, tm), 0), 1.0, 0.0).astype(f32)
        cum = _nn(ltri, logf, HI) + carry[0:1, :]
        carry[0:1, :] = cum[tm - 1:tm, :]
        lane = _iota((tm, 128), 1)
        lo_half = lane < F_HD
        for p in range(F_H // 2):
            ps = slice(p * 128, (p + 1) * 128)
            for src, w_ref, dst, is_q in ((q_ref, qw_ref, qa_ref, True), (k_ref, kw_ref, ka_ref, False)):
                x = src[:, ps]
                xn = x * lax.rsqrt(_half_mean(x * x, lo_half) + EPS) * w_ref[...]
                if is_q:
                    xn = xn * (F_HD ** -0.5)
                for e in range(2):
                    h = 2 * p + e
                    base = xn if e == 0 else pltpu.roll(xn, F_HD, 1)
                    hi, mid, lo = _split3(cum[:, h:h + 1])
                    pieces = jnp.where(lane == 64, hi, 0.0) + jnp.where(lane == 65, mid, 0.0) + jnp.where(lane == 66, lo, 0.0)
                    if is_q:
                        ext = pieces + jnp.where((lane >= 67) & (lane <= 69), 1.0, 0.0)
                    else:
                        ext = jnp.where((lane >= 64) & (lane <= 66), 1.0, 0.0) - pltpu.roll(pieces, 3, 1)
                    dst[:, h * 128:(h + 1) * 128] = jnp.where(lo_half, base, ext).astype(bf16)
        one = jnp.where(lane == F_HD, 1.0, 0.0)
        for p in range(F_H // 2):
            vv = v_ref[:, p * 128:(p + 1) * 128]
            vb_ref[:, (2 * p) * 128:(2 * p + 1) * 128] = jnp.where(lo_half, vv, one).astype(bf16)
            vb_ref[:, (2 * p + 1) * 128:(2 * p + 2) * 128] = jnp.where(lo_half, pltpu.roll(vv, F_HD, 1), one).astype(bf16)

    t1 = lambda c: pl.BlockSpec((tm, 1024), lambda i: (i, c))
    vec = lambda n: pl.BlockSpec((1, n), lambda i: (0, 0))
    sd = jax.ShapeDtypeStruct
    return pl.pallas_call(
        body, name="fox_pre", grid=(S // tm,),
        in_specs=[t1(0), t1(1), t1(2), pl.BlockSpec((tm, 128), lambda i: (i, F_F0 // 128)), vec(16), vec(128), vec(128)],
        out_specs=[pl.BlockSpec((tm, 2048), lambda i: (i, 0))] * 3,
        out_shape=[sd((S, 2048), bf16)] * 3,
        scratch_shapes=[pltpu.VMEM((8, 16), f32)],
        compiler_params=_params(("arbitrary",)),
    )(proj, proj, proj, proj, fbias, qw2, kw2)


FTQ = 512


def fox_attn(qa, ka, v, comm=None):
    S = qa.shape[0]
    nq = S // FTQ

    def body(q_ref, k_ref, v_ref, o_ref, lse_ref, m_scr, acc_scr):
        i, j = pl.program_id(1), pl.program_id(2)

        @pl.when(j == 0)
        def _():
            m_scr[...] = jnp.full_like(m_scr, NEG)
            acc_scr[...] = jnp.zeros_like(acc_scr)

        def step(diagonal):
            for e in range(2):
                es = slice(e * 128, (e + 1) * 128)
                s = _nt(q_ref[:, es], k_ref[:, es])
                if diagonal:
                    s = jnp.where(_iota((FTQ, FTQ), 0) >= _iota((FTQ, FTQ), 1), s, NEG)
                m_old = m_scr[e]
                m_new = jnp.maximum(m_old, jnp.max(s, axis=-1, keepdims=True))
                p = jnp.exp(s - m_new[:, 0:1])
                acc_scr[e] = acc_scr[e] * jnp.exp(m_old - m_new) + _nn(p.astype(bf16), v_ref[:, es])
                m_scr[e] = m_new

        pl.when(j < i)(functools.partial(step, False))
        pl.when(j == i)(functools.partial(step, True))

        @pl.when(j == nq - 1)
        def _():
            for e in range(2):
                vs = slice(e * F_HD, (e + 1) * F_HD)
                acc = acc_scr[e]
                l = acc[:, F_HD:F_HD + 1]
                o_ref[:, vs] = acc[:, 0:F_HD] / l
                lse_ref[:, vs] = m_scr[e][:, 0:F_HD] + jnp.log(l)

    sd = jax.ShapeDtypeStruct
    qo = pl.BlockSpec((FTQ, 128), lambda p, i, j: (i, p))
    kv = pl.BlockSpec((FTQ, 256), lambda p, i, j: (jnp.minimum(j, i), p))
    return _call(
        body, name="fox_attn", grid=(F_H // 2, nq, nq),
        in_specs=[pl.BlockSpec((FTQ, 256), lambda p, i, j: (i, p)), kv, kv],
        out_specs=[qo, qo],
        out_shape=[sd((S, 1024), f32), sd((S, 1024), f32)],
        scratch_shapes=[pltpu.VMEM((2, FTQ, 128), f32), pltpu.VMEM((2, FTQ, 128), f32)],
        sem=("parallel", "parallel", "arbitrary"), args=(qa, ka, v), comm=comm)


def fox_attn_bwd(qa, ka, v, do, lse, delta, comm=None):
    S = qa.shape[0]
    nq = S // FTQ

    def body(q_ref, k_ref, v_ref, do_ref, lse_ref, dl_ref, dq_ref, dk_ref, dv_ref, dk_scr, dv_scr):
        j, i = pl.program_id(1), pl.program_id(2)

        @pl.when((j == 0) & (i == 0))
        def _():
            dq_ref[...] = jnp.zeros_like(dq_ref)

        @pl.when(i == 0)
        def _():
            dk_scr[...] = jnp.zeros_like(dk_scr)
            dv_scr[...] = jnp.zeros_like(dv_scr)

        def step(diagonal):
            rows = pl.ds(pl.multiple_of(i * FTQ, FTQ), FTQ)
            for e in range(2):
                es, vs = slice(e * 128, (e + 1) * 128), slice(e * F_HD, (e + 1) * F_HD)
                qe, ke = q_ref[:, es], k_ref[:, es]
                dob = do_ref[:, vs].astype(bf16)
                s = _nt(qe, ke)
                if diagonal:
                    s = jnp.where(_iota((FTQ, FTQ), 0) >= _iota((FTQ, FTQ), 1), s, NEG)
                p = jnp.exp(s - lse_ref[:, e * F_HD:e * F_HD + 1])
                ds = p * (_nt(dob, v_ref[:, e * 128:e * 128 + F_HD]) - dl_ref[:, e * F_HD:e * F_HD + 1])
                dsb = ds.astype(bf16)
                dv_scr[e] += _tn(p.astype(bf16), dob)
                dk_scr[e] += _tn(dsb, qe)
                dq_ref[rows, es] += _nn(dsb, ke)

        pl.when(i > j)(functools.partial(step, False))
        pl.when(i == j)(functools.partial(step, True))

        @pl.when(i == nq - 1)
        def _():
            for e in range(2):
                dk_ref[:, e * 128:(e + 1) * 128] = dk_scr[e]
                dv_ref[:, e * F_HD:(e + 1) * F_HD] = dv_scr[e]

    sd = jax.ShapeDtypeStruct
    qi = lambda w: pl.BlockSpec((FTQ, w), lambda p, j, i: (jnp.maximum(i, j), p))
    kj = lambda w: pl.BlockSpec((FTQ, w), lambda p, j, i: (j, p))
    return _call(
        body, name="fox_attn_bwd", grid=(F_H // 2, nq, nq),
        in_specs=[qi(256), kj(256), kj(256), qi(128), qi(128), qi(128)],
        out_specs=[pl.BlockSpec((S, 256), lambda p, j, i: (0, p)), kj(256), kj(128)],
        out_shape=[sd((S, 2048), f32), sd((S, 2048), f32), sd((S, 1024), f32)],
        scratch_shapes=[pltpu.VMEM((2, FTQ, 128), f32), pltpu.VMEM((2, FTQ, F_HD), f32)],
        sem=("parallel", "arbitrary", "arbitrary"), args=(qa, ka, v, do, lse, delta), comm=comm)


def fox_gate(o, proj):
    S = o.shape[0]
    tm = 512

    def body(o_ref, z_ref, o2_ref):
        o2_ref[...] = (o_ref[...] * _silu(z_ref[...])).astype(bf16)

    t = pl.BlockSpec((tm, 1024), lambda i: (i, 0))
    return pl.pallas_call(
        body, name="fox_gate", grid=(S // tm,),
        in_specs=[t, pl.BlockSpec((tm, 1024), lambda i: (i, 3))], out_specs=t,
        out_shape=jax.ShapeDtypeStruct((S, 1024), bf16),
        compiler_params=_params(("parallel",)),
    )(o, proj)


def fox_gate_bwd(do2, o, proj):
    S = o.shape[0]
    tm = 256

    def body(d_ref, o_ref, z_ref, do_ref, dz_ref, dl_ref):
        lo_half = _iota((tm, 128), 1) < F_HD
        for p in range(F_H // 2):
            ps = slice(p * 128, (p + 1) * 128)
            d2, ov, z = d_ref[:, ps], o_ref[:, ps], z_ref[:, ps]
            dov = d2 * _silu(z)
            do_ref[:, ps] = dov
            dz_ref[:, ps] = (d2 * ov * _dsilu(z)).astype(bf16)
            dl_ref[:, ps] = _half_mean(dov * ov, lo_half) * float(F_HD)

    t = pl.BlockSpec((tm, 1024), lambda i: (i, 0))
    sd = jax.ShapeDtypeStruct
    return pl.pallas_call(
        body, name="fox_gate_bwd", grid=(S // tm,),
        in_specs=[t, t, pl.BlockSpec((tm, 1024), lambda i: (i, 3))], out_specs=[t, t, t],
        out_shape=[sd((S, 1024), f32), sd((S, 1024), bf16), sd((S, 1024), f32)],
        compiler_params=_params(("parallel",)),
    )(do2, o, proj)


def fox_pre_bwd(proj, fbias, qw2, kw2, dqa, dka, dv, dz):
    S = proj.shape[0]
    tm = 256
    nb = S // tm

    def body(q_ref, k_ref, f_ref, fb_ref, qw_ref, kw_ref, dqa_ref, dka_ref, dv_ref, dz_ref, dp_ref, st_ref, carry):
        i = pl.program_id(0)

        @pl.when(i == 0)
        def _():
            carry[...] = jnp.zeros_like(carry)

        lane = _iota((tm, 128), 1)
        lo_half = lane < F_HD
        lane16 = _iota((tm, 16), 1)
        dcum = jnp.zeros((tm, 16), f32)
        dws = []
        for src, w_ref, dsrc, is_q, col0 in ((q_ref, qw_ref, dqa_ref, True, 0), (k_ref, kw_ref, dka_ref, False, 1024)):
            dw = jnp.zeros((1, 128), f32)
            for p in range(F_H // 2):
                ps = slice(p * 128, (p + 1) * 128)
                x = src[:, ps]
                r = lax.rsqrt(_half_mean(x * x, lo_half) + EPS)
                xh = x * r
                d0 = dsrc[:, (2 * p) * 128:(2 * p + 1) * 128]
                d1 = dsrc[:, (2 * p + 1) * 128:(2 * p + 2) * 128]
                dy = jnp.where(lo_half, d0, pltpu.roll(d1, F_HD, 1))
                if is_q:
                    dy = dy * (F_HD ** -0.5)
                dxh = dy * w_ref[...]
                dw = dw + jnp.sum(dy * xh, axis=0, keepdims=True)
                dp_ref[:, col0 + p * 128:col0 + (p + 1) * 128] = (r * (dxh - xh * _half_mean(dxh * xh, lo_half))).astype(bf16)
                for e, de in ((0, d0), (1, d1)):
                    col = de[:, 64:65] if is_q else -de[:, 67:68]
                    dcum = dcum + jnp.where(lane16 == 2 * p + e, col, 0.0)
            dws.append(dw)
        dp_ref[:, 2048:3072] = dv_ref[...].astype(bf16)
        dp_ref[:, 3072:4096] = dz_ref[...]
        utri = jnp.where(_iota((tm, tm), 1) >= _iota((tm, tm), 0), 1.0, 0.0).astype(f32)
        dlogf = _nn(utri, dcum, HI) + carry[0:1, :]
        carry[0:1, :] = dlogf[0:1, :]
        fl = f_ref[:, 0:16] + fb_ref[...]
        df = dlogf * _sigmoid(-fl)
        place = jnp.where(_iota((16, 128), 1) == _iota((16, 128), 0), 1.0, 0.0).astype(f32)
        dfw = _nn(df, place, HI)
        dp_ref[:, F_F0:F_INP] = dfw.astype(bf16)
        upd = jnp.concatenate(dws + [jnp.sum(dfw, axis=0, keepdims=True), jnp.zeros((5, 128), f32)], axis=0)

        @pl.when(i == 0)
        def _():
            st_ref[...] = upd

        @pl.when(i > 0)
        def _():
            st_ref[...] += upd

    rev = lambda w, c: pl.BlockSpec((tm, w), lambda i: (nb - 1 - i, c))
    vec = lambda n: pl.BlockSpec((1, n), lambda i: (0, 0))
    sd = jax.ShapeDtypeStruct
    return pl.pallas_call(
        body, name="fox_pre_bwd", grid=(nb,),
        in_specs=[rev(1024, 0), rev(1024, 1), rev(128, F_F0 // 128), vec(16), vec(128), vec(128),
                  rev(2048, 0), rev(2048, 0), rev(1024, 0), rev(1024, 0)],
        out_specs=[rev(F_INP, 0), pl.BlockSpec((8, 128), lambda i: (0, 0))],
        out_shape=[sd((S, F_INP), bf16), sd((8, 128), f32)],
        scratch_shapes=[pltpu.VMEM((8, 16), f32)],
        compiler_params=_params(("arbitrary",)),
    )(proj, proj, proj, fbias, qw2, kw2, dqa, dka, dv, dz)


def _me():
    return lax.axis_index("x"), lax.axis_index("y"), lax.axis_index("c")


def _other_chips(x, y):
    return [(1 - x, y), (x, 1 - y), (1 - x, 1 - y)]


def ag_small(xs):
    m_per, n = xs.shape

    def body(x_ref, out_ref, send_sems, recv_sems, local_sem):
        x, y, c = _me()
        me, sibling = (x, y, c), (x, y, 1 - c)
        chips = _other_chips(x, y)

        def rows(px, py, pc):
            return out_ref.at[pl.ds((4 * px + 2 * py + pc) * m_per, m_per), :]

        def copy(k, block, to, src=None):
            return pltpu.make_async_remote_copy(
                src_ref=rows(*block) if src is None else src, dst_ref=rows(*block),
                send_sem=send_sems.at[k], recv_sem=recv_sems.at[k], device_id=to, device_id_type=MESH)

        mine = pltpu.make_async_copy(x_ref, rows(*me), local_sem)
        mine.start()
        first = [copy(0, me, sibling, src=x_ref)]
        first += [copy(1 + j, me, (*chip, c), src=x_ref) for j, chip in enumerate(chips)]
        for cp in first:
            cp.start()
        passed = [copy(4 + j, (*chip, c), sibling) for j, chip in enumerate(chips)]
        for j, chip in enumerate(chips):
            copy(1 + j, (*chip, c), me).wait_recv()
            passed[j].start()
        copy(0, sibling, me).wait_recv()
        for j, chip in enumerate(chips):
            copy(4 + j, (*chip, 1 - c), me).wait_recv()
        for cp in first + passed:
            cp.wait_send()
        mine.wait()

    return pl.pallas_call(
        body, name="ag_small",
        out_shape=jax.ShapeDtypeStruct((8 * m_per, n), xs.dtype),
        in_specs=[pl.BlockSpec(memory_space=pltpu.VMEM)], out_specs=pl.BlockSpec(memory_space=pltpu.VMEM),
        scratch_shapes=[pltpu.SemaphoreType.DMA((7,)), pltpu.SemaphoreType.DMA((7,)), pltpu.SemaphoreType.DMA],
        compiler_params=pltpu.CompilerParams(vmem_limit_bytes=VMEM_LIMIT),
    )(xs)


_ANY = pl.BlockSpec(memory_space=pl.ANY)


def ag_chips(arrs):
    n = len(arrs)
    assert all(a.shape[0] == 2 for a in arrs)

    def body(*refs):
        ins, outs = refs[:n], refs[n:2 * n]
        send_sems, recv_sems, fwd_send, fwd_recv, local_sems = refs[2 * n:]
        x, y, c = _me()
        me = 2 * x + y
        chips = _other_chips(x, y)
        started = []
        for a in range(n):
            cp = pltpu.make_async_copy(ins[a], outs[a].at[me], local_sems.at[a])
            cp.start()
            started.append(cp)
        sends = []
        for a in range(n):
            for j, (px, py) in enumerate(chips):
                r = pltpu.make_async_remote_copy(
                    src_ref=ins[a].at[c], dst_ref=outs[a].at[me, c], send_sem=send_sems.at[3 * a + j],
                    recv_sem=recv_sems.at[3 * a + j], device_id=(px, py, c), device_id_type=MESH)
                r.start()
                sends.append(r)
        for a in range(n):
            for j, (px, py) in enumerate(chips):
                got = outs[a].at[2 * px + py, c]
                pltpu.make_async_remote_copy(
                    src_ref=ins[a].at[c], dst_ref=got, send_sem=send_sems.at[3 * a + j],
                    recv_sem=recv_sems.at[3 * a + j], device_id=(px, py, c), device_id_type=MESH).wait_recv()
                f = pltpu.make_async_remote_copy(
                    src_ref=got, dst_ref=got, send_sem=fwd_send.at[3 * a + j], recv_sem=fwd_recv.at[3 * a + j],
                    device_id=(x, y, 1 - c), device_id_type=MESH)
                f.start()
                sends.append(f)
        for a in range(n):
            for j, (px, py) in enumerate(chips):
                theirs = outs[a].at[2 * px + py, 1 - c]
                pltpu.make_async_remote_copy(
                    src_ref=theirs, dst_ref=theirs, send_sem=fwd_send.at[3 * a + j], recv_sem=fwd_recv.at[3 * a + j],
                    device_id=(x, y, 1 - c), device_id_type=MESH).wait_recv()
        for r in sends:
            r.wait_send()
        for cp in started:
            cp.wait()

    sems = pltpu.SemaphoreType.DMA((3 * n,))
    return pl.pallas_call(
        body, name="ag_chips",
        out_shape=[jax.ShapeDtypeStruct((4,) + a.shape, a.dtype) for a in arrs],
        in_specs=[_ANY] * n, out_specs=[_ANY] * n,
        scratch_shapes=[sems, sems, sems, sems, pltpu.SemaphoreType.DMA((n,))],
    )(*arrs)


def _ag_comm(arrs):
    n = len(arrs)

    def copies(ins, outs, sems):
        send_sems, recv_sems, local_sems = sems
        x, y, c = _me()
        me = 2 * x + y
        local = [pltpu.make_async_copy(ins[a], outs[a].at[me], local_sems.at[a]) for a in range(n)]
        out_cp, in_cp = [], []
        for a in range(n):
            for j, (px, py) in enumerate(_other_chips(x, y)):
                mk = functools.partial(pltpu.make_async_remote_copy, src_ref=ins[a], send_sem=send_sems.at[3 * a + j],
                                       recv_sem=recv_sems.at[3 * a + j], device_id=(px, py, c), device_id_type=MESH)
                out_cp.append(mk(dst_ref=outs[a].at[me]))
                in_cp.append(mk(dst_ref=outs[a].at[2 * px + py]))
        return local, out_cp, in_cp

    def start(ins, outs, sems):
        local, out_cp, _ = copies(ins, outs, sems)
        for cp in local + out_cp:
            cp.start()

    def wait(ins, outs, sems):
        local, out_cp, in_cp = copies(ins, outs, sems)
        for cp in in_cp:
            cp.wait_recv()
        for cp in out_cp:
            cp.wait_send()
        for cp in local:
            cp.wait()

    sems = [pltpu.SemaphoreType.DMA((3 * n,)), pltpu.SemaphoreType.DMA((3 * n,)), pltpu.SemaphoreType.DMA((n,))]
    return _Comm(arrs, [jax.ShapeDtypeStruct((4,) + a.shape, a.dtype) for a in arrs], sems, start, wait)


def _rs_comm(gs):
    n = len(gs)
    flips = [(fx, fy, fc) for fx in (0, 1) for fy in (0, 1) for fc in (0, 1)][1:]

    def copies(ins, outs, sems):
        send_sems, recv_sems, local_sems = sems
        x, y, c = _me()
        me = 4 * x + 2 * y + c
        local, out_cp, in_cp = [], [], []
        for a in range(n):
            rh = ins[a].shape[1] // 2
            mine = ins[a].at[2 * x + y, pl.ds(c * rh, rh), :]
            local.append(pltpu.make_async_copy(mine, outs[a].at[me], local_sems.at[a]))
            for j, (fx, fy, fc) in enumerate(flips):
                px, py, pc = (1 - x if fx else x), (1 - y if fy else y), (1 - c if fc else c)
                mk = functools.partial(pltpu.make_async_remote_copy, send_sem=send_sems.at[7 * a + j],
                                       recv_sem=recv_sems.at[7 * a + j], device_id=(px, py, pc), device_id_type=MESH)
                out_cp.append(mk(src_ref=ins[a].at[2 * px + py, pl.ds(pc * rh, rh), :], dst_ref=outs[a].at[me]))
                in_cp.append(mk(src_ref=mine, dst_ref=outs[a].at[4 * px + 2 * py + pc]))
        return local, out_cp, in_cp

    def start(ins, outs, sems):
        local, out_cp, _ = copies(ins, outs, sems)
        for cp in local + out_cp:
            cp.start()

    def wait(ins, outs, sems):
        local, out_cp, in_cp = copies(ins, outs, sems)
        for cp in in_cp:
            cp.wait_recv()
        for cp in out_cp:
            cp.wait_send()
        for cp in local:
            cp.wait()

    sems = [pltpu.SemaphoreType.DMA((7 * n,)), pltpu.SemaphoreType.DMA((7 * n,)), pltpu.SemaphoreType.DMA((n,))]
    return _Comm(gs, [jax.ShapeDtypeStruct((8, g.shape[1] // 2, g.shape[2]), g.dtype) for g in gs], sems, start, wait)


def sum_leading(q, name):
    K, R, C = q.shape
    tr = _pick(R, (256, 128, 64, 32, 16, 8))

    def body(q_ref, o_ref):
        acc = q_ref[0]
        for k in range(1, K):
            acc = acc + q_ref[k]
        o_ref[...] = acc

    return pl.pallas_call(
        body, name=name, grid=(R // tr,),
        in_specs=[pl.BlockSpec((K, tr, C), lambda i: (0, i, 0))], out_specs=pl.BlockSpec((tr, C), lambda i: (i, 0)),
        out_shape=jax.ShapeDtypeStruct((R, C), f32),
        compiler_params=_params(("parallel",)),
    )(q)


def rs_sum_devices(q, cidx):
    K, R, C = q.shape
    tr = _pick(R, (256, 128))

    def body(c_ref, q_ref, o_ref):
        acc = q_ref[0].astype(f32)
        for k in range(1, K):
            acc = acc + q_ref[k].astype(f32)
        o_ref[0] = acc

    return pl.pallas_call(
        body, name="rs_sum_devices",
        grid_spec=pltpu.PrefetchScalarGridSpec(
            num_scalar_prefetch=1, grid=(R // tr,),
            in_specs=[pl.BlockSpec((K, tr, C), lambda i, c_ref: (0, i, 0))],
            out_specs=pl.BlockSpec((1, tr, C), lambda i, c_ref: (c_ref[0], i, 0))),
        out_shape=jax.ShapeDtypeStruct((2, R, C), f32),
        compiler_params=_params(("parallel",)),
    )(cidx, q)


def rs_share_halves(rs):
    n = len(rs)

    def body(*refs):
        bufs = refs[n:2 * n]
        send_sems, recv_sems = refs[2 * n:]
        x, y, c = _me()
        cps = []
        for a in range(n):
            cp = pltpu.make_async_remote_copy(
                src_ref=bufs[a].at[c], dst_ref=bufs[a].at[c], send_sem=send_sems.at[a], recv_sem=recv_sems.at[a],
                device_id=(x, y, 1 - c), device_id_type=MESH)
            cp.start()
            cps.append(cp)
        for a, cp in enumerate(cps):
            pltpu.make_async_remote_copy(
                src_ref=bufs[a].at[c], dst_ref=bufs[a].at[1 - c], send_sem=send_sems.at[a], recv_sem=recv_sems.at[a],
                device_id=(x, y, 1 - c), device_id_type=MESH).wait_recv()
            cp.wait_send()

    return pl.pallas_call(
        body, name="rs_share_halves",
        out_shape=[jax.ShapeDtypeStruct(r.shape, r.dtype) for r in rs],
        in_specs=[_ANY] * n, out_specs=[_ANY] * n, input_output_aliases={a: a for a in range(n)},
        scratch_shapes=[pltpu.SemaphoreType.DMA((n,)), pltpu.SemaphoreType.DMA((n,))],
    )(*rs)


def ada_mod(c_all, ada_w):
    L, _, n = ada_w.shape

    def body(c_ref, w_ref, o_ref):
        o_ref[0] = _nn(_silu(c_ref[...]), w_ref[0], HI)

    return pl.pallas_call(
        body, name="ada_mod", grid=(L,),
        in_specs=[pl.BlockSpec((8, D), lambda l: (0, 0)), pl.BlockSpec((1, D, n), lambda l: (l, 0, 0))],
        out_specs=pl.BlockSpec((1, 8, n), lambda l: (l, 0, 0)),
        out_shape=jax.ShapeDtypeStruct((L, 8, n), f32),
        compiler_params=_params(("parallel",)),
    )(c_all, ada_w)


def ada_w_grad(c_all, dmod):
    L, _, n = dmod.shape

    def body(c_ref, d_ref, o_ref):
        o_ref[0] = _tn(_silu(c_ref[...]), d_ref[0], HI)

    return pl.pallas_call(
        body, name="ada_w_grad", grid=(L,),
        in_specs=[pl.BlockSpec((8, D), lambda l: (0, 0)), pl.BlockSpec((1, 8, n), lambda l: (l, 0, 0))],
        out_specs=pl.BlockSpec((1, D, n), lambda l: (l, 0, 0)),
        out_shape=jax.ShapeDtypeStruct((L, D, n), f32),
        compiler_params=_params(("parallel",)),
    )(c_all, dmod)


def adamw(w, g, m, v, name):
    shp = w.shape
    two = lambda a: a.reshape(-1, shp[-1])
    R, C = two(w).shape
    tr = _pick(R, (256, 128, 64, 32, 16, 8))
    bc1, bc2 = 1.0 - B1 ** STEP, 1.0 - B2 ** STEP

    def body(w_ref, g_ref, m_ref, v_ref, d_ref, mo_ref, vo_ref):
        gv = g_ref[...]
        mn = B1 * m_ref[...] + (1.0 - B1) * gv
        vn = B2 * v_ref[...] + (1.0 - B2) * (gv * gv)
        d_ref[...] = -LR * ((mn / bc1) / (jnp.sqrt(vn / bc2) + AEPS) + WD * w_ref[...])
        mo_ref[...] = mn
        vo_ref[...] = vn

    t = pl.BlockSpec((tr, C), lambda i: (i, 0))
    outs = pl.pallas_call(
        body, name=name, grid=(R // tr,),
        in_specs=[t] * 4, out_specs=[t] * 3, out_shape=[jax.ShapeDtypeStruct((R, C), f32)] * 3,
        compiler_params=_params(("parallel",)),
    )(two(w), two(g), two(m), two(v))
    return [o.reshape(shp) for o in outs]


def _pack(arrs):
    parts, offs, r0 = [], [], 0
    for a in arrs:
        n = a.size
        rows = -(-n // 1024) * 8
        parts.append(jnp.pad(a.reshape(-1), (0, rows * 128 - n)).reshape(rows, 128))
        offs.append((r0, rows))
        r0 += rows
    return jnp.concatenate(parts, axis=0), offs


def _unpack(buf, offs, shapes):
    out = []
    for (r0, rows), shp in zip(offs, shapes):
        n = 1
        for d in shp:
            n *= d
        out.append(buf[..., r0:r0 + rows, :].reshape(buf.shape[:-2] + (rows * 128,))[..., :n].reshape(buf.shape[:-2] + tuple(shp)))
    return out


def kernel(x, c, norm_w, ada_w, ada_b, a_w_in, a_conv_w, a_A_log, a_dt_bias, a_norm_w, a_w_out, b_w_in, b_f_bias, b_qn_w, b_kn_w, b_w_out, final_norm_w, loss_target, m_norm_w, m_ada_w, m_ada_b, m_a_w_in, m_a_conv_w, m_a_A_log, m_a_dt_bias, m_a_norm_w, m_a_w_out, m_b_w_in, m_b_f_bias, m_b_qn_w, m_b_kn_w, m_b_w_out, m_final_norm_w, v_norm_w, v_ada_w, v_ada_b, v_a_w_in, v_a_conv_w, v_a_A_log, v_a_dt_bias, v_a_norm_w, v_a_w_out, v_b_w_in, v_b_f_bias, v_b_qn_w, v_b_kn_w, v_b_w_out, v_final_norm_w):
    weights = dict(norm_w=norm_w, ada_w=ada_w, ada_b=ada_b, a_w_in=a_w_in, a_conv_w=a_conv_w, a_A_log=a_A_log,
                   a_dt_bias=a_dt_bias, a_norm_w=a_norm_w, a_w_out=a_w_out, b_w_in=b_w_in, b_f_bias=b_f_bias,
                   b_qn_w=b_qn_w, b_kn_w=b_kn_w, b_w_out=b_w_out, final_norm_w=final_norm_w)
    m_in = dict(norm_w=m_norm_w, ada_w=m_ada_w, ada_b=m_ada_b, a_w_in=m_a_w_in, a_conv_w=m_a_conv_w, a_A_log=m_a_A_log,
                a_dt_bias=m_a_dt_bias, a_norm_w=m_a_norm_w, a_w_out=m_a_w_out, b_w_in=m_b_w_in, b_f_bias=m_b_f_bias,
                b_qn_w=m_b_qn_w, b_kn_w=m_b_kn_w, b_w_out=m_b_w_out, final_norm_w=m_final_norm_w)
    v_in = dict(norm_w=v_norm_w, ada_w=v_ada_w, ada_b=v_ada_b, a_w_in=v_a_w_in, a_conv_w=v_a_conv_w, a_A_log=v_a_A_log,
                a_dt_bias=v_a_dt_bias, a_norm_w=v_a_norm_w, a_w_out=v_a_w_out, b_w_in=v_b_w_in, b_f_bias=v_b_f_bias,
                b_qn_w=v_b_qn_w, b_kn_w=v_b_kn_w, b_w_out=v_b_w_out, final_norm_w=v_final_norm_w)
    xi, yi, ci = _me()
    me_b, me_k = 4 * xi + 2 * yi + ci, 2 * xi + yi
    cidx = ci.astype(jnp.int32).reshape(1)
    S = x.shape[1]
    depth, n_a, n_b = norm_w.shape[0], a_w_in.shape[0], b_w_in.shape[0]
    x0, tgt = x.reshape(S, D), loss_target.reshape(S, D)

    c_all = ag_small(jnp.pad(c, ((0, 7), (0, 0)))).reshape(8, 8, D)[:, 0]
    nloc = ada_w.shape[2]
    parts = ag_small(ada_mod(c_all, ada_w).reshape(depth * 8, nloc)).reshape(4, 2, depth, 8, nloc)[:, 0]
    mine = lax.dynamic_index_in_dim(parts, me_b, axis=2, keepdims=False)
    mod = jnp.transpose(mine, (1, 0, 2)).reshape(depth, 4 * nloc) + ada_b
    shift, scale, gate = (mod[:, k * D:(k + 1) * D] for k in range(3))

    w_loc = [(a_w_in[i // 2] if i % 2 == 0 else b_w_in[i // 2]).astype(bf16) for i in range(depth)]
    wo_loc = [(a_w_out[i // 2] if i % 2 == 0 else b_w_out[i // 2]).astype(bf16) for i in range(depth)]
    pad_in = [(G_INP - G_IN) if i % 2 == 0 else (F_INP - F_IN) for i in range(depth)]
    halves = lambda w: w.reshape((2, w.shape[0] // 2) + w.shape[1:])

    def in_place(g_in, g_out, pad):
        w = jnp.transpose(g_in, (1, 0, 2)).reshape(g_in.shape[1], -1)
        return jnp.pad(w, ((0, 0), (0, pad))), g_out.reshape(-1, D)

    g_in0, g_out0, g_conv = ag_chips([halves(w_loc[0]), halves(wo_loc[0]), a_conv_w])
    w_full = [in_place(g_in0.reshape((4,) + w_loc[0].shape), g_out0.reshape((4,) + wo_loc[0].shape), pad_in[0])]
    conv = [jnp.transpose(g_conv[:, l], (1, 0, 2)).reshape(CONV_K, -1) for l in range(n_a)]
    qw2 = [_row(jnp.tile(b_qn_w[l], 2)) for l in range(n_b)]
    kw2 = [_row(jnp.tile(b_kn_w[l], 2)) for l in range(n_b)]

    saved, xc = [], x0
    for i in range(depth):
        l = i // 2
        w_in_i, w_out_i = w_full[i]
        nxt = _ag_comm([w_loc[i + 1], wo_loc[i + 1]]) if i + 1 < depth else None
        h = ln_mod(xc, _row(norm_w[i]), _row(scale[i]), _row(shift[i]))
        if i % 2 == 0:
            proj = matmul(h, w_in_i, "nn", "mm_a_in")
            pre = gdn_pre(proj, conv[l], _row(a_A_log[l]), _row(a_dt_bias[l]))
            res, got = gdn_fwd(*pre, comm=nxt)
            o2 = gdn_onorm(res[0], proj, _row(a_norm_w[l]))
            y, xn = out_proj(o2, w_out_i, xc, _row(gate[i]), "out_proj_a")
        else:
            proj = matmul(h, w_in_i, "nn", "mm_b_in")
            pre = fox_pre(proj, _row(b_f_bias[l]), qw2[l], kw2[l])
            res, got = fox_attn(*pre, comm=nxt)
            o2 = fox_gate(res[0], proj)
            y, xn = out_proj(o2, w_out_i, xc, _row(gate[i]), "out_proj_b")
        saved.append((xc, h, proj, o2, y, pre, res))
        if nxt is not None:
            w_full.append(in_place(got[0], got[1], pad_in[i + 1]))
        xc = xn
    dx, st_f = final_loss(xc, _row(final_norm_w), tgt)

    d_norm, d_mod = [None] * depth, [None] * depth
    d_conv, d_alog, d_dtb, d_anw = [None] * n_a, [None] * n_a, [None] * n_a, [None] * n_a
    d_fb, d_qn, d_kn = [None] * n_b, [None] * n_b, [None] * n_b
    exchanged, pending = [None] * depth, None
    for i in reversed(range(depth)):
        l = i // 2
        xin, h, proj, o2, y, pre, res = saved[i]
        w_in_i, w_out_i = w_full[i]
        ride = _rs_comm(pending) if pending is not None else None
        dy, st_g = gate_bwd(dx, y, _row(gate[i]))
        if i % 2 == 0:
            o, wv, at, tinv, vn, st = res
            do2 = matmul(dy, w_out_i, "nt", "mm_a_do2")
            d_out = matmul(o2, dy, "tn", "mm_a_dwo")
            do, dz, st_o = gdn_onorm_bwd(do2, o, proj, _row(a_norm_w[l]))
            grads, got = gdn_bwd(do, *pre, wv, at, tinv, vn, st, comm=ride)
            dcv, dba, st_s = gdn_pre_bwd(proj, conv[l], _row(a_A_log[l]), _row(a_dt_bias[l]), *grads)
            dproj, dcw = gdn_conv_bwd(proj, conv[l], dcv, dz, dba)
            dh = matmul(dproj, w_in_i, "nt", "mm_a_dh")
            d_in = matmul(h, dproj, "tn", "mm_a_dw")
            d_conv[l], d_alog[l], d_dtb[l], d_anw[l] = dcw[:CONV_K], st_s[0], st_s[1], st_o[0]
        else:
            o, lse = res
            do2 = matmul(dy, w_out_i, "nt", "mm_b_do2")
            d_out = matmul(o2, dy, "tn", "mm_b_dwo")
            do, dz, delta = fox_gate_bwd(do2, o, proj)
            (dqa, dka, dv), got = fox_attn_bwd(*pre, do, lse, delta, comm=ride)
            dproj, st_b = fox_pre_bwd(proj, _row(b_f_bias[l]), qw2[l], kw2[l], dqa, dka, dv, dz)
            dh = matmul(dproj, w_in_i, "nt", "mm_b_dh")
            d_in = matmul(h, dproj, "tn", "mm_b_dw")
            d_fb[l], d_qn[l], d_kn[l] = st_b[2, :F_H], st_b[0, :F_HD] + st_b[0, F_HD:], st_b[1, :F_HD] + st_b[1, F_HD:]
        if ride is not None:
            exchanged[i + 1] = got
        cl = w_loc[i].shape[1]
        pending = [jnp.transpose(d_in[:, :4 * cl].reshape(d_in.shape[0], 4, cl), (1, 0, 2)).astype(bf16),
                   d_out.reshape(4, d_out.shape[0] // 4, D).astype(bf16)]
        dx, st_n = ln_mod_bwd(xin, _row(norm_w[i]), _row(scale[i]), dh, dx)
        d_norm[i] = st_n[0]
        d_mod[i] = jnp.concatenate([st_n[2], st_n[1], st_g[0]])
    exchanged[0] = _comm_alone(_rs_comm(pending), "rs_exchange")

    small = [jnp.stack(d_norm), jnp.stack(d_mod), jnp.stack(d_conv), jnp.stack(d_alog), jnp.stack(d_dtb), jnp.stack(d_anw),
             jnp.stack(d_fb), jnp.stack(d_qn), jnp.stack(d_kn), st_f[0], jnp.sum(st_f[1]).reshape(1)]
    shapes = [a.shape for a in small]
    buf, offs = _pack(small)
    gathered = ag_small(buf).reshape(8, buf.shape[0], 128)
    tot = _unpack(sum_leading(gathered, "sum_devices"), offs, shapes)
    g_norm, g_adab, g_convf, g_alog, g_dtb, g_anw, g_fb, g_qn, g_kn, g_fin, loss = tot
    dmod_all = _unpack(gathered, offs[1:2], shapes[1:2])[0]
    dmod_loc = lax.dynamic_slice_in_dim(dmod_all, me_k * nloc, nloc, axis=2)
    g_adaw = ada_w_grad(c_all, jnp.transpose(dmod_loc, (1, 0, 2)))
    g_conv_loc = lax.dynamic_slice_in_dim(g_convf, me_k * a_conv_w.shape[2], a_conv_w.shape[2], axis=2)

    flat = [q for i in range(depth) for q in exchanged[i]]
    done = rs_share_halves([rs_sum_devices(q, cidx) for q in flat])
    red = [d.reshape(-1, d.shape[-1]) for d in done]
    r_in, r_out = red[0::2], red[1::2]
    grads = dict(norm_w=g_norm, ada_w=g_adaw, ada_b=g_adab, a_w_in=jnp.stack(r_in[0::2]), a_conv_w=g_conv_loc,
                 a_A_log=g_alog, a_dt_bias=g_dtb, a_norm_w=g_anw, a_w_out=jnp.stack(r_out[0::2]),
                 b_w_in=jnp.stack(r_in[1::2]), b_f_bias=g_fb, b_qn_w=g_qn, b_kn_w=g_kn,
                 b_w_out=jnp.stack(r_out[1::2]), final_norm_w=g_fin)
    names = list(weights)
    upd = {n: adamw(weights[n], grads[n], m_in[n], v_in[n], "adamw_" + n) for n in names}
    return (loss.reshape(()), dx.reshape(x.shape), *[grads[n] for n in names], *[upd[n][0] for n in names],
            *[upd[n][1] for n in names], *[upd[n][2] for n in names])
```

```python
import functools

import jax
import jax.numpy as jnp
from jax import lax
from jax.experimental import pallas as pl
from jax.experimental.pallas import tpu as pltpu

f32, bf16 = jnp.float32, jnp.bfloat16
HI = lax.Precision.HIGHEST
MESH = pl.DeviceIdType.MESH

EPS = 1e-6
D = 1024
CHUNK = 64
GQK_H, GV_H, GHD = 8, 16, 128
G_CONV = 4096
G_Z0 = 4096
G_BA0 = 6144
G_IN, G_INP = 6176, 6272
CONV_K = 4
F_H, F_HD = 16, 64
F_W = 1024
F_F0 = 4096
F_IN, F_INP = 4112, 4224
LR, B1, B2, AEPS, WD, STEP = 0.001, 0.9, 0.999, 1e-08, 0.01, 10
NEG = -1e30
VMEM_LIMIT = 56 * 1024 * 1024


def _nn(a, b, prec=None):
    return lax.dot_general(a, b, (((1,), (0,)), ((), ())), preferred_element_type=f32, precision=prec)


def _nt(a, b, prec=None):
    return lax.dot_general(a, b, (((1,), (1,)), ((), ())), preferred_element_type=f32, precision=prec)


def _tn(a, b, prec=None):
    return lax.dot_general(a, b, (((0,), (0,)), ((), ())), preferred_element_type=f32, precision=prec)


def _iota(shape, axis):
    return lax.broadcasted_iota(jnp.int32, shape, axis)


def _sigmoid(x):
    return 0.5 * jnp.tanh(0.5 * x) + 0.5


def _softplus(x):
    return jnp.maximum(x, 0.0) + jnp.log(1.0 + jnp.exp(-jnp.abs(x)))


def _silu(x):
    return x * _sigmoid(x)


def _dsilu(x):
    s = _sigmoid(x)
    return s * (1.0 + x * (1.0 - s))


def _params(sem=None, vmem=VMEM_LIMIT):
    return pltpu.CompilerParams(dimension_semantics=sem, vmem_limit_bytes=vmem)


def _row(v):
    return v.reshape(1, -1)


class _Comm:
    def __init__(self, ins, out_shapes, sems, start, wait):
        self.ins, self.out_shapes, self.sems, self.start, self.wait = list(ins), list(out_shapes), list(sems), start, wait


def _call(body, *, name, grid, in_specs, out_specs, out_shape, scratch_shapes, sem, args, comm=None, prefetch=()):
    n_pf, n_in, n_out, n_s = len(prefetch), len(in_specs), len(out_specs), len(scratch_shapes)
    n_ci, n_co = (len(comm.ins), len(comm.out_shapes)) if comm is not None else (0, 0)

    def wrapped(*refs):
        pf, refs = refs[:n_pf], refs[n_pf:]
        core_in, c_in = refs[:n_in], refs[n_in:n_in + n_ci]
        o0 = n_in + n_ci
        core_out, c_out = refs[o0:o0 + n_out], refs[o0 + n_out:o0 + n_out + n_co]
        s0 = o0 + n_out + n_co
        core_s, c_sem = refs[s0:s0 + n_s], refs[s0 + n_s:]
        if comm is not None:
            first = functools.reduce(jnp.logical_and, [pl.program_id(d) == 0 for d in range(len(grid))])
            pl.when(first)(functools.partial(comm.start, c_in, c_out, c_sem))
        body(*pf, *core_in, *core_out, *core_s)
        if comm is not None:
            last = functools.reduce(jnp.logical_and, [pl.program_id(d) == grid[d] - 1 for d in range(len(grid))])
            pl.when(last)(functools.partial(comm.wait, c_in, c_out, c_sem))

    extra = ([], [], [], []) if comm is None else ([_ANY] * n_ci, [_ANY] * n_co, comm.out_shapes, comm.sems)
    spec = pltpu.PrefetchScalarGridSpec(
        num_scalar_prefetch=n_pf, grid=grid, in_specs=list(in_specs) + extra[0], out_specs=list(out_specs) + extra[1],
        scratch_shapes=list(scratch_shapes) + extra[3])
    outs = pl.pallas_call(
        wrapped, name=name if comm is None else name + "_x", grid_spec=spec, out_shape=list(out_shape) + extra[2],
        compiler_params=_params(sem if comm is None else ("arbitrary",) * len(grid)),
    )(*prefetch, *args, *(comm.ins if comm is not None else []))
    return outs[:n_out], outs[n_out:]


def _pick(n, pref):
    for t in pref:
        if n % t == 0:
            return t
    return n


MM_VMEM_BUDGET = 44 * 1024 * 1024


def _mm_tiles(M, N, K):
    best = None
    for tk in [K] + [t for t in (2048, 1408, 1024, 896, 512, 384, 256, 128) if K % t == 0 and t < K]:
        for tm in (2048, 1024, 512, 256, 128):
            for tn in (1408, 1024, 896, 512, 384, 256, 128):
                if M % tm or N % tn:
                    continue
                nk = K // tk
                need = 2 * 2 * (tm * tk + tk * tn) + 2 * 4 * tm * tn + (4 * tm * tn if nk > 1 else 0)
                if need <= MM_VMEM_BUDGET:
                    cand = ((nk, -tm * tn), (tm, tn, tk))
                    best = cand if best is None or cand[0] < best[0] else best
    return best[1]


def matmul(a, b, mode, name, out_dtype=f32, comm=None):
    if mode == "nn":
        (M, K), (_, N) = a.shape, b.shape
    elif mode == "nt":
        (M, K), (N, _) = a.shape, b.shape
    else:
        (K, M), (_, N) = a.shape, b.shape
    tm, tn, tk = _mm_tiles(M, N, K)
    nk = K // tk
    dot = {"nn": _nn, "nt": _nt, "tn": _tn}[mode]

    def body(a_ref, b_ref, o_ref, *acc):
        k = pl.program_id(2)
        part = dot(a_ref[...], b_ref[...])
        if nk == 1:
            o_ref[...] = part.astype(out_dtype)
        else:
            acc_ref = acc[0]

            @pl.when(k == 0)
            def _():
                acc_ref[...] = part

            @pl.when(k > 0)
            def _():
                acc_ref[...] += part

            @pl.when(k == nk - 1)
            def _():
                o_ref[...] = acc_ref[...].astype(out_dtype)

    a_spec = pl.BlockSpec((tk, tm), lambda i, j, k: (k, i)) if mode == "tn" else pl.BlockSpec((tm, tk), lambda i, j, k: (i, k))
    b_spec = pl.BlockSpec((tn, tk), lambda i, j, k: (j, k)) if mode == "nt" else pl.BlockSpec((tk, tn), lambda i, j, k: (k, j))
    outs, got = _call(
        body, name=name, grid=(M // tm, N // tn, nk),
        in_specs=[a_spec, b_spec], out_specs=[pl.BlockSpec((tm, tn), lambda i, j, k: (i, j))],
        out_shape=[jax.ShapeDtypeStruct((M, N), out_dtype)],
        scratch_shapes=[] if nk == 1 else [pltpu.VMEM((tm, tn), f32)],
        sem=("parallel", "parallel", "arbitrary"), args=(a, b), comm=comm)
    return outs[0] if comm is None else (outs[0], got)


def out_proj(o2, w, x, gate, name):
    S, K = o2.shape
    N = w.shape[1]
    tm, tn = 512, 512

    def body(a_ref, b_ref, x_ref, g_ref, y_ref, xn_ref):
        y = _nn(a_ref[...], b_ref[...])
        y_ref[...] = y
        xn_ref[...] = x_ref[...] + g_ref[...] * y

    return pl.pallas_call(
        body, name=name, grid=(S // tm, N // tn),
        in_specs=[pl.BlockSpec((tm, K), lambda i, j: (i, 0)), pl.BlockSpec((K, tn), lambda i, j: (0, j)),
                  pl.BlockSpec((tm, tn), lambda i, j: (i, j)), pl.BlockSpec((1, tn), lambda i, j: (0, j))],
        out_specs=[pl.BlockSpec((tm, tn), lambda i, j: (i, j))] * 2,
        out_shape=[jax.ShapeDtypeStruct((S, N), f32)] * 2,
        compiler_params=_params(("parallel", "parallel")),
    )(o2, w, x, gate)


def ln_mod(x, nw, scale, shift):
    S = x.shape[0]
    tm = 512

    def body(x_ref, nw_ref, sc_ref, sh_ref, h_ref):
        xv = x_ref[...]
        r = lax.rsqrt(jnp.mean(xv * xv, axis=-1, keepdims=True) + EPS)
        h_ref[...] = ((xv * r) * nw_ref[...] * (1.0 + sc_ref[...]) + sh_ref[...]).astype(bf16)

    vec = pl.BlockSpec((1, D), lambda i: (0, 0))
    return pl.pallas_call(
        body, name="ln_mod", grid=(S // tm,),
        in_specs=[pl.BlockSpec((tm, D), lambda i: (i, 0)), vec, vec, vec],
        out_specs=pl.BlockSpec((tm, D), lambda i: (i, 0)),
        out_shape=jax.ShapeDtypeStruct((S, D), bf16),
        compiler_params=_params(("parallel",)),
    )(x, nw, scale, shift)


def ln_mod_bwd(x, nw, scale, dh, dxres):
    S = x.shape[0]
    tm = 512
    nb = S // tm

    def body(x_ref, nw_ref, sc_ref, dh_ref, dr_ref, dx_ref, st_ref):
        i = pl.program_id(0)
        xv = x_ref[...]
        r = lax.rsqrt(jnp.mean(xv * xv, axis=-1, keepdims=True) + EPS)
        xn = xv * r
        dh = dh_ref[...]
        dxn = dh * (nw_ref[...] * (1.0 + sc_ref[...]))
        dx_ref[...] = dr_ref[...] + r * (dxn - xn * jnp.mean(dxn * xn, axis=-1, keepdims=True))
        p1 = jnp.sum(dh * xn, axis=0, keepdims=True)
        p2 = jnp.sum(dh, axis=0, keepdims=True)
        upd = jnp.concatenate([p1, p1, p2, jnp.zeros((5, D), f32)], axis=0)

        @pl.when(i == 0)
        def _():
            st_ref[...] = upd

        @pl.when(i > 0)
        def _():
            st_ref[...] += upd

        @pl.when(i == nb - 1)
        def _():
            st_ref[0:1, :] = st_ref[0:1, :] * (1.0 + sc_ref[...])
            st_ref[1:2, :] = st_ref[1:2, :] * nw_ref[...]

    vec = pl.BlockSpec((1, D), lambda i: (0, 0))
    tile = pl.BlockSpec((tm, D), lambda i: (i, 0))
    return pl.pallas_call(
        body, name="ln_mod_bwd", grid=(S // tm,),
        in_specs=[tile, vec, vec, tile, tile],
        out_specs=[tile, pl.BlockSpec((8, D), lambda i: (0, 0))],
        out_shape=[jax.ShapeDtypeStruct((S, D), f32), jax.ShapeDtypeStruct((8, D), f32)],
        compiler_params=_params(("arbitrary",)),
    )(x, nw, scale, dh, dxres)


def final_loss(x, fw, tgt):
    S = x.shape[0]
    tm = 512

    def body(x_ref, w_ref, t_ref, dx_ref, st_ref):
        i = pl.program_id(0)
        xv = x_ref[...]
        r = lax.rsqrt(jnp.mean(xv * xv, axis=-1, keepdims=True) + EPS)
        xn = xv * r
        err = xn * w_ref[...] - t_ref[...]
        dy = err * (1.0 / D)
        dxn = dy * w_ref[...]
        dx_ref[...] = r * (dxn - xn * jnp.mean(dxn * xn, axis=-1, keepdims=True))
        p1 = jnp.sum(dy * xn, axis=0, keepdims=True)
        p2 = jnp.sum(err * err, axis=0, keepdims=True) * (0.5 / D)
        upd = jnp.concatenate([p1, p2, jnp.zeros((6, D), f32)], axis=0)

        @pl.when(i == 0)
        def _():
            st_ref[...] = upd

        @pl.when(i > 0)
        def _():
            st_ref[...] += upd

    tile = pl.BlockSpec((tm, D), lambda i: (i, 0))
    return pl.pallas_call(
        body, name="final_loss", grid=(S // tm,),
        in_specs=[tile, pl.BlockSpec((1, D), lambda i: (0, 0)), tile],
        out_specs=[tile, pl.BlockSpec((8, D), lambda i: (0, 0))],
        out_shape=[jax.ShapeDtypeStruct((S, D), f32), jax.ShapeDtypeStruct((8, D), f32)],
        compiler_params=_params(("arbitrary",)),
    )(x, fw, tgt)


def gate_bwd(dx, y, gate):
    S = dx.shape[0]
    tm = 512

    def body(dx_ref, y_ref, g_ref, dy_ref, st_ref):
        i = pl.program_id(0)
        dxv = dx_ref[...]
        dy_ref[...] = (g_ref[...] * dxv).astype(bf16)
        upd = jnp.concatenate([jnp.sum(dxv * y_ref[...], axis=0, keepdims=True), jnp.zeros((7, D), f32)], axis=0)

        @pl.when(i == 0)
        def _():
            st_ref[...] = upd

        @pl.when(i > 0)
        def _():
            st_ref[...] += upd

    tile = pl.BlockSpec((tm, D), lambda i: (i, 0))
    return pl.pallas_call(
        body, name="gate_bwd", grid=(S // tm,),
        in_specs=[tile, tile, pl.BlockSpec((1, D), lambda i: (0, 0))],
        out_specs=[tile, pl.BlockSpec((8, D), lambda i: (0, 0))],
        out_shape=[jax.ShapeDtypeStruct((S, D), bf16), jax.ShapeDtypeStruct((8, D), f32)],
        compiler_params=_params(("arbitrary",)),
    )(dx, y, gate)


def _chunk_mats(tm):
    r, c = _iota((tm, tm), 0), _iota((tm, tm), 1)
    same = jnp.right_shift(r, 6) == jnp.right_shift(c, 6)
    ltri = jnp.where(same & (c <= r), 1.0, 0.0).astype(f32)
    utri = jnp.where(same & (c >= r), 1.0, 0.0).astype(f32)
    bsame = jnp.where(same, 1.0, 0.0).astype(f32)
    return ltri, utri, bsame


def _gdn_scalars(ba, alog, dtb, ltri, bsame):
    beta = _sigmoid(ba[:, 0:16])
    u = ba[:, 16:32] + dtb
    neg_a = -jnp.exp(alog)
    g = neg_a * _softplus(u)
    gc = _nn(ltri, g, HI)
    glast = _nn(bsame, g, HI)
    return beta, u, neg_a, g, gc, glast


def _conv_taps(p_ref, halo_ref, first, gi):
    cs = slice(gi * 128, (gi + 1) * 128)
    cur = p_ref[:, cs]
    hal = jnp.where(first, 0.0, halo_ref[:, cs])
    ext = jnp.concatenate([hal, cur], axis=0)
    return [cur] + [pltpu.roll(ext, s, 0)[8:] for s in range(1, CONV_K)]


def _conv(taps, w):
    cv = taps[0] * w[3:4]
    for s in range(1, CONV_K):
        cv = cv + taps[s] * w[3 - s:4 - s]
    return cv


def _l2n(x):
    return x * lax.rsqrt(jnp.sum(x * x, axis=-1, keepdims=True) + EPS)


def _gdn_in_specs(tm, S):
    nb8 = tm // 8
    return [pl.BlockSpec((tm, G_CONV), lambda i: (i, 0)),
            pl.BlockSpec((8, G_CONV), lambda i: (jnp.maximum(i * nb8 - 1, 0), 0)),
            pl.BlockSpec((tm, 128), lambda i: (i, G_BA0 // 128))]


def gdn_pre(proj, conv_w, alog, dtb):
    S = proj.shape[0]
    tm = 256
    nch = tm // CHUNK

    def body(p_ref, halo_ref, ba_ref, w_ref, al_ref, dt_ref,
             q_ref, k_ref, kb_ref, kbg_ref, vb_ref, qd_ref, kd_ref, d_ref, gl_ref):
        first = pl.program_id(0) == 0
        ltri, _, bsame = _chunk_mats(tm)
        beta, _, _, _, gc, glast = _gdn_scalars(ba_ref[...], al_ref[...], dt_ref[...], ltri, bsame)
        eg, ek, egl = jnp.exp(gc), jnp.exp(glast - gc), jnp.exp(glast)
        eye = jnp.where(_iota((16, 16), 0) == _iota((16, 16), 1), 1.0, 0.0).astype(f32)
        gct = _nt(eye, gc, HI)
        low = _iota((CHUNK, CHUNK), 0) >= _iota((CHUNK, CHUNK), 1)

        def act(gi):
            return _silu(_conv(_conv_taps(p_ref, halo_ref, first, gi), w_ref[:, gi * 128:(gi + 1) * 128]))

        for j in range(GQK_H):
            js = slice(j * 128, (j + 1) * 128)
            qn = _l2n(act(j)) * (GHD ** -0.5)
            kn = _l2n(act(GQK_H + j))
            q_ref[:, js] = qn.astype(bf16)
            k_ref[:, js] = kn.astype(bf16)
            for e in range(2):
                h = 2 * j + e
                hs = slice(h * 128, (h + 1) * 128)
                v = act(2 * GQK_H + h)
                bh, egh, ekh = beta[:, h:h + 1], eg[:, h:h + 1], ek[:, h:h + 1]
                kbv = kn * bh
                kb_ref[:, hs] = kbv.astype(bf16)
                kbg_ref[:, hs] = (kbv * egh).astype(bf16)
                vb_ref[:, hs] = (v * bh).astype(bf16)
                qd_ref[:, hs] = (qn * egh).astype(bf16)
                kd_ref[:, hs] = (kn * ekh).astype(bf16)
                for c in range(nch):
                    rs = slice(c * CHUNK, (c + 1) * CHUNK)
                    diff = gc[rs, h:h + 1] - gct[h:h + 1, rs]
                    d_ref[rs, h * CHUNK:(h + 1) * CHUNK] = jnp.where(low, jnp.exp(jnp.where(low, diff, 0.0)), 0.0)
                    gl_ref[c * 8:(c + 1) * 8, hs] = jnp.broadcast_to(egl[c * CHUNK:c * CHUNK + 8, h:h + 1], (8, 128))

    full = lambda shape: pl.BlockSpec(shape, lambda i: (0, 0))
    t1 = pl.BlockSpec((tm, 1024), lambda i: (i, 0))
    t2 = pl.BlockSpec((tm, 2048), lambda i: (i, 0))
    sd = jax.ShapeDtypeStruct
    return pl.pallas_call(
        body, name="gdn_pre", grid=(S // tm,),
        in_specs=_gdn_in_specs(tm, S) + [full((CONV_K, G_CONV)), full((1, 16)), full((1, 16))],
        out_specs=[t1, t1, t2, t2, t2, t2, t2, t1, pl.BlockSpec((tm // 8, 2048), lambda i: (i, 0))],
        out_shape=[sd((S, 1024), bf16)] * 2 + [sd((S, 2048), bf16)] * 5 + [sd((S, 1024), f32), sd((S // 8, 2048), f32)],
        compiler_params=_params(("parallel",)),
    )(proj, proj, proj, conv_w, alog, dtb)


def _bnn(a, b):
    return lax.dot_general(a, b, (((2,), (1,)), ((0,), (0,))), preferred_element_type=f32)


def _bnt(a, b):
    return lax.dot_general(a, b, (((2,), (2,)), ((0,), (0,))), preferred_element_type=f32)


def _btn(a, b):
    return lax.dot_general(a, b, (((1,), (1,)), ((0,), (0,))), preferred_element_type=f32)


def _split(a):
    hi = a.astype(bf16)
    return hi, (a - hi.astype(f32)).astype(bf16)


def _cat3(h, l, axis, lhs):
    return jnp.concatenate([h, h, l] if lhs else [h, l, h], axis=axis)


def _tri_inv_b(L):
    eye = jnp.where(_iota((1, CHUNK, CHUNK), 1) == _iota((1, CHUNK, CHUNK), 2), 1.0, 0.0).astype(f32)
    P = -L
    T = eye + P
    ph, pl_ = _split(P)
    for _ in range(5):
        P = _bnn(_cat3(ph, pl_, 2, True), _cat3(ph, pl_, 1, False))
        ph, pl_ = _split(P)
        th, tl = _split(T)
        T = T + _bnn(_cat3(th, tl, 2, True), _cat3(ph, pl_, 1, False))
    return T


GTB = 512


def _gdn_slices(ncb):
    pairs = [(c, e) for c in range(ncb) for e in range(2)]
    rs = lambda c: slice(c * CHUNK, (c + 1) * CHUNK)
    cs = lambda e: slice(e * 128, (e + 1) * 128)
    ds_ = lambda e: slice(e * CHUNK, (e + 1) * CHUNK)
    return pairs, rs, cs, ds_


def gdn_fwd(q, k, kb, kbg, vb, qd, kd, dm, gl8, comm=None):
    S = q.shape[0]
    nb, ncb = S // GTB, GTB // CHUNK
    pairs, rs, cs, ds_ = _gdn_slices(ncb)

    def body(q_ref, k_ref, kb_ref, kbg_ref, vb_ref, qd_ref, kd_ref, d_ref, gl_ref,
             o_ref, w_ref, at_ref, t_ref, vn_ref, st_ref, state, u_scr):
        @pl.when(pl.program_id(1) == 0)
        def _():
            state[...] = jnp.zeros_like(state)

        stk = lambda ref, lanes: jnp.stack([ref[rs(c), lanes(e)] for c, e in pairs])
        kq = jnp.stack([k_ref[rs(c), :] for c, _ in pairs])
        dmat = stk(d_ref, ds_)
        strict = _iota((1, CHUNK, CHUNK), 1) > _iota((1, CHUNK, CHUNK), 2)
        T = _tri_inv_b(jnp.where(strict, _bnt(stk(kb_ref, cs), kq) * dmat, 0.0))
        tb = T.astype(bf16)
        u_scr[...] = _bnn(tb, stk(vb_ref, cs))
        wb = _bnn(tb, stk(kbg_ref, cs)).astype(bf16)
        qk = _bnt(jnp.stack([q_ref[rs(c), :] for c in range(ncb)]), jnp.stack([k_ref[rs(c), :] for c in range(ncb)]))
        for b, (c, e) in enumerate(pairs):
            w_ref[rs(c), cs(e)] = wb[b]
            at_ref[rs(c), ds_(e)] = (qk[c] * dmat[b]).astype(bf16)
            t_ref[rs(c), ds_(e)] = T[b]
        for b, (c, e) in enumerate(pairs):
            sb = state[e].astype(bf16)
            vnb = (u_scr[b] - _nn(w_ref[rs(c), cs(e)], sb)).astype(bf16)
            o_ref[rs(c), cs(e)] = _nn(qd_ref[rs(c), cs(e)], sb) + _nn(at_ref[rs(c), ds_(e)], vnb)
            st_ref[c * 128:(c + 1) * 128, cs(e)] = sb
            state[e] = state[e] * gl_ref[c * 8:c * 8 + 1, cs(e)] + _tn(kd_ref[rs(c), cs(e)], vnb)
            vn_ref[rs(c), cs(e)] = vnb

    b1 = pl.BlockSpec((GTB, 128), lambda j, i: (i, j))
    b2 = pl.BlockSpec((GTB, 256), lambda j, i: (i, j))
    sd = jax.ShapeDtypeStruct
    return _call(
        body, name="gdn_fwd", grid=(GQK_H, nb),
        in_specs=[b1, b1, b2, b2, b2, b2, b2, b1, pl.BlockSpec((GTB // 8, 256), lambda j, i: (i, j))],
        out_specs=[b2, b2, b1, b1, b2, pl.BlockSpec((ncb * 128, 256), lambda j, i: (i, j))],
        out_shape=[sd((S, 2048), f32), sd((S, 2048), bf16), sd((S, 1024), bf16), sd((S, 1024), f32),
                   sd((S, 2048), bf16), sd((S // CHUNK * 128, 2048), bf16)],
        scratch_shapes=[pltpu.VMEM((2, 128, 128), f32), pltpu.VMEM((2 * ncb, CHUNK, 128), f32)],
        sem=("parallel", "arbitrary"), args=(q, k, kb, kbg, vb, qd, kd, dm, gl8), comm=comm)


def gdn_bwd(do, q, k, kb, kbg, vb, qd, kd, dm, gl8, w, at, T, vn, st, comm=None):
    S = q.shape[0]
    nb, ncb = S // GTB, GTB // CHUNK
    pairs, rs, cs, ds_ = _gdn_slices(ncb)

    def body(do_ref, q_ref, k_ref, kb_ref, kbg_ref, vb_ref, qd_ref, kd_ref, d_ref, gl_ref, w_ref, at_ref, t_ref, vn_ref, st_ref,
             dq_ref, dk_ref, dkb_ref, dkbg_ref, dvb_ref, dqd_ref, dkd_ref, dgc_ref, dstate, dvn_scr, dw_scr, dat_scr, dgl_scr):
        @pl.when(pl.program_id(1) == 0)
        def _():
            dstate[...] = jnp.zeros_like(dstate)

        for b, (c, e) in reversed(list(enumerate(pairs))):
            dob = do_ref[rs(c), cs(e)].astype(bf16)
            sb = st_ref[c * 128:(c + 1) * 128, cs(e)]
            vnb = vn_ref[rs(c), cs(e)]
            gl = gl_ref[c * 8:c * 8 + 1, cs(e)]
            dS = dstate[e]
            dsb = dS.astype(bf16)
            dvnb = (_tn(at_ref[rs(c), ds_(e)], dob) + _nn(kd_ref[rs(c), cs(e)], dsb)).astype(bf16)
            dvn_scr[b] = dvnb
            dat_scr[b] = _nt(dob, vnb)
            dqd_ref[rs(c), cs(e)] = _nt(dob, sb)
            dkd_ref[rs(c), cs(e)] = _nt(vnb, dsb)
            dw_scr[b] = (-_nt(dvnb, sb)).astype(bf16)
            dgl = jnp.sum(jnp.sum(dS * sb.astype(f32), axis=1, keepdims=True), axis=0, keepdims=True)
            dgl_scr[b] = jnp.broadcast_to(dgl * gl, (8, 128))
            dstate[e] = gl * dS + _tn(qd_ref[rs(c), cs(e)], dob) - _tn(w_ref[rs(c), cs(e)], dvnb)

        stk = lambda ref, lanes: jnp.stack([ref[rs(c), lanes(e)] for c, e in pairs])
        kq = jnp.stack([k_ref[rs(c), :] for c, _ in pairs])
        qq = jnp.stack([q_ref[rs(c), :] for c, _ in pairs])
        kbb = stk(kb_ref, cs)
        Tm = stk(t_ref, ds_)
        tb = Tm.astype(bf16)
        dvn, dw = dvn_scr[...], dw_scr[...]
        dT = _bnt(dvn, stk(vb_ref, cs)) + _bnt(dw, stk(kbg_ref, cs))
        dvb, dkbg = _btn(tb, dvn), _btn(tb, dw)
        th, tl = _split(Tm)
        xh, xl = _split(_bnt(_cat3(*_split(dT), 2, True), _cat3(th, tl, 2, False)))
        dL = -_btn(_cat3(th, tl, 1, True), _cat3(xh, xl, 1, False))
        dmat = stk(d_ref, ds_)
        strict = _iota((1, CHUNK, CHUNK), 1) > _iota((1, CHUNK, CHUNK), 2)
        dA = jnp.where(strict, dL * dmat, 0.0)
        dB = dat_scr[...] * dmat
        dAb, dBb = dA.astype(bf16), dB.astype(bf16)
        dkb = _bnn(dAb, kq)
        dkc = _btn(dAb, kbb) + _btn(dBb, qq)
        dqc = _bnn(dBb, kq)
        M = dA * _bnt(kbb, kq) + dB * _bnt(qq, kq)
        mh, ml = _split(M)
        colsum = _btn(jnp.concatenate([mh, ml], axis=1), jnp.ones((2 * ncb, 2 * CHUNK, 128), bf16))
        lastrow = _iota((1, CHUNK, 128), 1) == CHUNK - 1
        for b, (c, e) in enumerate(pairs):
            dvb_ref[rs(c), cs(e)] = dvb[b]
            dkbg_ref[rs(c), cs(e)] = dkbg[b]
            dkb_ref[rs(c), cs(e)] = dkb[b]
            dgc_ref[rs(c), cs(e)] = (jnp.sum(M[b], axis=1, keepdims=True) - colsum[b]
                                     + jnp.where(lastrow[0], dgl_scr[b][0:1, :], 0.0))
        for c in range(ncb):
            dq_ref[rs(c), :] = dqc[2 * c] + dqc[2 * c + 1]
            dk_ref[rs(c), :] = dkc[2 * c] + dkc[2 * c + 1]

    b1 = pl.BlockSpec((GTB, 128), lambda j, i: (nb - 1 - i, j))
    b2 = pl.BlockSpec((GTB, 256), lambda j, i: (nb - 1 - i, j))
    sd = jax.ShapeDtypeStruct
    return _call(
        body, name="gdn_bwd", grid=(GQK_H, nb),
        in_specs=[b2, b1, b1, b2, b2, b2, b2, b2, b1, pl.BlockSpec((GTB // 8, 256), lambda j, i: (nb - 1 - i, j)),
                  b2, b1, b1, b2, pl.BlockSpec((ncb * 128, 256), lambda j, i: (nb - 1 - i, j))],
        out_specs=[b1, b1, b2, b2, b2, b2, b2, b2],
        out_shape=[sd((S, 1024), f32)] * 2 + [sd((S, 2048), f32)] * 6,
        scratch_shapes=[pltpu.VMEM((2, 128, 128), f32), pltpu.VMEM((2 * ncb, CHUNK, 128), bf16),
                        pltpu.VMEM((2 * ncb, CHUNK, 128), bf16), pltpu.VMEM((2 * ncb, CHUNK, CHUNK), f32),
                        pltpu.VMEM((2 * ncb, 8, 128), f32)],
        sem=("parallel", "arbitrary"), args=(do, q, k, kb, kbg, vb, qd, kd, dm, gl8, w, at, T, vn, st), comm=comm)


def gdn_onorm(o, proj, nw):
    S = o.shape[0]
    tm = 256

    def body(o_ref, z_ref, nw_ref, o2_ref):
        for h in range(GV_H):
            hs = slice(h * 128, (h + 1) * 128)
            oh = o_ref[:, hs]
            r = lax.rsqrt(jnp.mean(oh * oh, axis=-1, keepdims=True) + EPS)
            o2_ref[:, hs] = (((oh * r) * nw_ref[...]) * _silu(z_ref[:, hs])).astype(bf16)

    t2 = pl.BlockSpec((tm, 2048), lambda i: (i, 0))
    return pl.pallas_call(
        body, name="gdn_onorm", grid=(S // tm,),
        in_specs=[t2, pl.BlockSpec((tm, 2048), lambda i: (i, G_Z0 // 2048)), pl.BlockSpec((1, 128), lambda i: (0, 0))],
        out_specs=t2, out_shape=jax.ShapeDtypeStruct((S, 2048), bf16),
        compiler_params=_params(("parallel",)),
    )(o, proj, nw)


def gdn_onorm_bwd(do2, o, proj, nw):
    S = o.shape[0]
    tm = 256

    def body(d_ref, o_ref, z_ref, nw_ref, do_ref, dz_ref, st_ref):
        i = pl.program_id(0)
        acc = jnp.zeros((1, 128), f32)
        for h in range(GV_H):
            hs = slice(h * 128, (h + 1) * 128)
            oh, z, d2 = o_ref[:, hs], z_ref[:, hs], d_ref[:, hs]
            r = lax.rsqrt(jnp.mean(oh * oh, axis=-1, keepdims=True) + EPS)
            on = oh * r
            dt = d2 * _silu(z)
            dz_ref[:, hs] = (d2 * (on * nw_ref[...]) * _dsilu(z)).astype(bf16)
            don = dt * nw_ref[...]
            acc = acc + jnp.sum(dt * on, axis=0, keepdims=True)
            do_ref[:, hs] = r * (don - on * jnp.mean(don * on, axis=-1, keepdims=True))
        upd = jnp.concatenate([acc, jnp.zeros((7, 128), f32)], axis=0)

        @pl.when(i == 0)
        def _():
            st_ref[...] = upd

        @pl.when(i > 0)
        def _():
            st_ref[...] += upd

    t2 = pl.BlockSpec((tm, 2048), lambda i: (i, 0))
    sd = jax.ShapeDtypeStruct
    return pl.pallas_call(
        body, name="gdn_onorm_bwd", grid=(S // tm,),
        in_specs=[t2, t2, pl.BlockSpec((tm, 2048), lambda i: (i, G_Z0 // 2048)), pl.BlockSpec((1, 128), lambda i: (0, 0))],
        out_specs=[t2, t2, pl.BlockSpec((8, 128), lambda i: (0, 0))],
        out_shape=[sd((S, 2048), f32), sd((S, 2048), bf16), sd((8, 128), f32)],
        compiler_params=_params(("arbitrary",)),
    )(do2, o, proj, nw)


def gdn_pre_bwd(proj, conv_w, alog, dtb, dq, dk, dkb, dkbg, dvb, dqd, dkd, dgcd):
    S = proj.shape[0]
    tm = 128

    def body(p_ref, halo_ref, ba_ref, w_ref, al_ref, dt_ref, dq_ref, dk_ref, dkb_ref, dkbg_ref, dvb_ref, dqd_ref, dkd_ref, dgc_ref,
             dcv_ref, dba_ref, st_ref):
        i = pl.program_id(0)
        first = i == 0
        ltri, utri, bsame = _chunk_mats(tm)
        beta, u, neg_a, g, gc, glast = _gdn_scalars(ba_ref[...], al_ref[...], dt_ref[...], ltri, bsame)
        eg, ek = jnp.exp(gc), jnp.exp(glast - gc)
        lane16 = _iota((tm, 16), 1)
        dgc_all = jnp.zeros((tm, 16), f32)
        rkd_all = jnp.zeros((tm, 16), f32)
        dbeta_all = jnp.zeros((tm, 16), f32)

        def pre(gi):
            return _conv(_conv_taps(p_ref, halo_ref, first, gi), w_ref[:, gi * 128:(gi + 1) * 128])

        def l2n_bwd(xt, dy):
            r = lax.rsqrt(jnp.sum(xt * xt, axis=-1, keepdims=True) + EPS)
            y = xt * r
            return r * (dy - y * jnp.sum(dy * y, axis=-1, keepdims=True))

        for j in range(GQK_H):
            js = slice(j * 128, (j + 1) * 128)
            cvq, cvk = pre(j), pre(GQK_H + j)
            qt, kt = _silu(cvq), _silu(cvk)
            qn = _l2n(qt) * (GHD ** -0.5)
            kn = _l2n(kt)
            dq_tot, dk_tot = dq_ref[:, js], dk_ref[:, js]
            for e in range(2):
                h = 2 * j + e
                hs = slice(h * 128, (h + 1) * 128)
                gv = 2 * GQK_H + h
                cvv = pre(gv)
                v = _silu(cvv)
                bh, egh, ekh = beta[:, h:h + 1], eg[:, h:h + 1], ek[:, h:h + 1]
                dkbg, dkd, dqd, dvb = dkbg_ref[:, hs], dkd_ref[:, hs], dqd_ref[:, hs], dvb_ref[:, hs]
                dkb_t = dkb_ref[:, hs] + dkbg * egh
                dk_tot = dk_tot + dkb_t * bh + dkd * ekh
                dq_tot = dq_tot + dqd * egh
                dcv_ref[:, gv * 128:(gv + 1) * 128] = (dvb * bh) * _dsilu(cvv)
                dbeta = jnp.sum(dkb_t * kn, axis=-1, keepdims=True) + jnp.sum(dvb * v, axis=-1, keepdims=True)
                rkd = jnp.sum(dkd * (kn * ekh), axis=-1, keepdims=True)
                dgc = (dgc_ref[:, hs][:, 0:1] + jnp.sum(dkbg * (kn * bh * egh), axis=-1, keepdims=True)
                       + jnp.sum(dqd * (qn * egh), axis=-1, keepdims=True) - rkd)
                sel = lane16 == h
                dgc_all = dgc_all + jnp.where(sel, dgc, 0.0)
                rkd_all = rkd_all + jnp.where(sel, rkd, 0.0)
                dbeta_all = dbeta_all + jnp.where(sel, dbeta, 0.0)
            dcv_ref[:, js] = l2n_bwd(qt, dq_tot * (GHD ** -0.5)) * _dsilu(cvq)
            ks = slice((GQK_H + j) * 128, (GQK_H + j + 1) * 128)
            dcv_ref[:, ks] = l2n_bwd(kt, dk_tot) * _dsilu(cvk)

        islast = jnp.bitwise_and(_iota((tm, 16), 0), CHUNK - 1) == CHUNK - 1
        dgc_all = dgc_all + jnp.where(islast, _nn(bsame, rkd_all, HI), 0.0)
        dg = _nn(utri, dgc_all, HI)
        da = dg * neg_a * _sigmoid(u)
        db = dbeta_all * beta * (1.0 - beta)
        r16, c128 = _iota((16, 128), 0), _iota((16, 128), 1)
        pb = jnp.where(c128 == r16, 1.0, 0.0).astype(f32)
        pa = jnp.where(c128 == r16 + 16, 1.0, 0.0).astype(f32)
        dba_ref[...] = _nn(db, pb, HI) + _nn(da, pa, HI)
        upd = jnp.concatenate([jnp.sum(dg * g, axis=0, keepdims=True), jnp.sum(da, axis=0, keepdims=True),
                               jnp.zeros((6, 16), f32)], axis=0)

        @pl.when(i == 0)
        def _():
            st_ref[...] = upd

        @pl.when(i > 0)
        def _():
            st_ref[...] += upd

    full = lambda shape: pl.BlockSpec(shape, lambda i: (0, 0))
    t1 = pl.BlockSpec((tm, 1024), lambda i: (i, 0))
    t2 = pl.BlockSpec((tm, 2048), lambda i: (i, 0))
    sd = jax.ShapeDtypeStruct
    return pl.pallas_call(
        body, name="gdn_pre_bwd", grid=(S // tm,),
        in_specs=_gdn_in_specs(tm, S) + [full((CONV_K, G_CONV)), full((1, 16)), full((1, 16)), t1, t1] + [t2] * 6,
        out_specs=[pl.BlockSpec((tm, G_CONV), lambda i: (i, 0)), pl.BlockSpec((tm, 128), lambda i: (i, 0)), full((8, 16))],
        out_shape=[sd((S, G_CONV), f32), sd((S, 128), f32), sd((8, 16), f32)],
        compiler_params=_params(("arbitrary",)),
    )(proj, proj, proj, conv_w, alog, dtb, dq, dk, dkb, dkbg, dvb, dqd, dkd, dgcd)


def gdn_conv_bwd(proj, conv_w, dcv, dz, dba):
    S = proj.shape[0]
    tm = 256
    nb, nb8 = S // tm, tm // 8

    def body(p_ref, halo_ref, w_ref, dcv_ref, nxt_ref, dz_ref, dba_ref, dp_ref, dw_ref):
        i = pl.program_id(0)
        first, last = i == 0, i == nb - 1
        for gi in range(G_CONV // 128):
            cs = slice(gi * 128, (gi + 1) * 128)
            taps = _conv_taps(p_ref, halo_ref, first, gi)
            cur = dcv_ref[:, cs]
            ext = jnp.concatenate([cur, jnp.where(last, 0.0, nxt_ref[:, cs])], axis=0)
            w = w_ref[:, cs]
            dp = cur * w[3:4]
            rows = [jnp.sum(cur * taps[3 - kk], axis=0, keepdims=True) for kk in range(CONV_K)]
            for s in range(1, CONV_K):
                dp = dp + pltpu.roll(ext, tm + 8 - s, 0)[:tm] * w[3 - s:4 - s]
            dp_ref[:, cs] = dp.astype(bf16)
            upd = jnp.concatenate(rows + [jnp.zeros((4, 128), f32)], axis=0)

            @pl.when(first)
            def _():
                dw_ref[:, cs] = upd

            @pl.when(i > 0)
            def _():
                dw_ref[:, cs] += upd

        dp_ref[:, G_Z0:G_BA0] = dz_ref[...]
        dp_ref[:, G_BA0:G_INP] = dba_ref[...].astype(bf16)

    sd = jax.ShapeDtypeStruct
    return pl.pallas_call(
        body, name="gdn_conv_bwd", grid=(nb,),
        in_specs=[pl.BlockSpec((tm, G_CONV), lambda i: (i, 0)),
                  pl.BlockSpec((8, G_CONV), lambda i: (jnp.maximum(i * nb8 - 1, 0), 0)),
                  pl.BlockSpec((CONV_K, G_CONV), lambda i: (0, 0)),
                  pl.BlockSpec((tm, G_CONV), lambda i: (i, 0)),
                  pl.BlockSpec((8, G_CONV), lambda i: (jnp.minimum((i + 1) * nb8, S // 8 - 1), 0)),
                  pl.BlockSpec((tm, 2048), lambda i: (i, 0)), pl.BlockSpec((tm, 128), lambda i: (i, 0))],
        out_specs=[pl.BlockSpec((tm, G_INP), lambda i: (i, 0)), pl.BlockSpec((8, G_CONV), lambda i: (0, 0))],
        out_shape=[sd((S, G_INP), bf16), sd((8, G_CONV), f32)],
        compiler_params=_params(("arbitrary",)),
    )(proj, proj, conv_w, dcv, dcv, dz, dba)


def _half_mean(t, lo_half):
    m0 = jnp.sum(jnp.where(lo_half, t, 0.0), axis=-1, keepdims=True)
    m1 = jnp.sum(jnp.where(lo_half, 0.0, t), axis=-1, keepdims=True)
    return jnp.where(lo_half, m0, m1) * (1.0 / F_HD)


def _split3(c):
    hi = c.astype(bf16).astype(f32)
    mid = (c - hi).astype(bf16).astype(f32)
    lo = (c - hi - mid).astype(bf16).astype(f32)
    return hi, mid, lo


def fox_pre(proj, fbias, qw2, kw2):
    S = proj.shape[0]
    tm = 256

    def body(q_ref, k_ref, v_ref, f_ref, fb_ref, qw_ref, kw_ref, qa_ref, ka_ref, vb_ref, carry):
        @pl.when(pl.program_id(0) == 0)
        def _():
            carry[...] = jnp.zeros_like(carry)

        logf = -_softplus(-(f_ref[:, 0:16] + fb_ref[...]))
        ltri = jnp.where(_iota((tm, tm), 1) <= _iota((tm, tm), 0), 1.0, 0.0).astype(f32)
        cum = _nn(ltri, logf, HI) + carry[0:1, :]
        carry[0:1, :] = cum[tm - 1:tm, :]
        lane = _iota((tm, 128), 1)
        lo_half = lane < F_HD
        for p in range(F_H // 2):
            ps = slice(p * 128, (p + 1) * 128)
            for src, w_ref, dst, is_q in ((q_ref, qw_ref, qa_ref, True), (k_ref, kw_ref, ka_ref, False)):
                x = src[:, ps]
                xn = x * lax.rsqrt(_half_mean(x * x, lo_half) + EPS) * w_ref[...]
                if is_q:
                    xn = xn * (F_HD ** -0.5)
                for e in range(2):
                    h = 2 * p + e
                    base = xn if e == 0 else pltpu.roll(xn, F_HD, 1)
                    hi, mid, lo = _split3(cum[:, h:h + 1])
                    pieces = jnp.where(lane == 64, hi, 0.0) + jnp.where(lane == 65, mid, 0.0) + jnp.where(lane == 66, lo, 0.0)
                    if is_q:
                        ext = pieces + jnp.where((lane >= 67) & (lane <= 69), 1.0, 0.0)
                    else:
                        ext = jnp.where((lane >= 64) & (lane <= 66), 1.0, 0.0) - pltpu.roll(pieces, 3, 1)
                    dst[:, h * 128:(h + 1) * 128] = jnp.where(lo_half, base, ext).astype(bf16)
        one = jnp.where(lane == F_HD, 1.0, 0.0)
        for p in range(F_H // 2):
            vv = v_ref[:, p * 128:(p + 1) * 128]
            vb_ref[:, (2 * p) * 128:(2 * p + 1) * 128] = jnp.where(lo_half, vv, one).astype(bf16)
            vb_ref[:, (2 * p + 1) * 128:(2 * p + 2) * 128] = jnp.where(lo_half, pltpu.roll(vv, F_HD, 1), one).astype(bf16)

    t1 = lambda c: pl.BlockSpec((tm, 1024), lambda i: (i, c))
    vec = lambda n: pl.BlockSpec((1, n), lambda i: (0, 0))
    sd = jax.ShapeDtypeStruct
    return pl.pallas_call(
        body, name="fox_pre", grid=(S // tm,),
        in_specs=[t1(0), t1(1), t1(2), pl.BlockSpec((tm, 128), lambda i: (i, F_F0 // 128)), vec(16), vec(128), vec(128)],
        out_specs=[pl.BlockSpec((tm, 2048), lambda i: (i, 0))] * 3,
        out_shape=[sd((S, 2048), bf16)] * 3,
        scratch_shapes=[pltpu.VMEM((8, 16), f32)],
        compiler_params=_params(("arbitrary",)),
    )(proj, proj, proj, proj, fbias, qw2, kw2)


FTQ = 512


def fox_attn(qa, ka, v, comm=None):
    S = qa.shape[0]
    nq = S // FTQ

    live = [(i, j) for i in range(nq) for j in range(i + 1)]
    qi_tab = jnp.asarray([i for i, _ in live], jnp.int32)
    kj_tab = jnp.asarray([j for _, j in live], jnp.int32)

    def body(qi_ref, kj_ref, q_ref, k_ref, v_ref, o_ref, lse_ref, m_scr, acc_scr):
        t = pl.program_id(1)
        i, j = qi_ref[t], kj_ref[t]

        @pl.when(j == 0)
        def _():
            m_scr[...] = jnp.full_like(m_scr, NEG)
            acc_scr[...] = jnp.zeros_like(acc_scr)

        def step(diagonal):
            for e in range(2):
                es = slice(e * 128, (e + 1) * 128)
                s = _nt(q_ref[:, es], k_ref[:, es])
                if diagonal:
                    s = jnp.where(_iota((FTQ, FTQ), 0) >= _iota((FTQ, FTQ), 1), s, NEG)
                m_old = m_scr[e]
                m_new = jnp.maximum(m_old, jnp.max(s, axis=-1, keepdims=True))
                p = jnp.exp(s - m_new[:, 0:1])
                acc_scr[e] = acc_scr[e] * jnp.exp(m_old - m_new) + _nn(p.astype(bf16), v_ref[:, es])
                m_scr[e] = m_new

        pl.when(j < i)(functools.partial(step, False))

        @pl.when(j == i)
        def _():
            step(True)
            for e in range(2):
                vs = slice(e * F_HD, (e + 1) * F_HD)
                acc = acc_scr[e]
                l = acc[:, F_HD:F_HD + 1]
                o_ref[:, vs] = acc[:, 0:F_HD] / l
                lse_ref[:, vs] = m_scr[e][:, 0:F_HD] + jnp.log(l)

    sd = jax.ShapeDtypeStruct
    qo = pl.BlockSpec((FTQ, 128), lambda p, t, qi, kj: (qi[t], p))
    kv = pl.BlockSpec((FTQ, 256), lambda p, t, qi, kj: (kj[t], p))
    return _call(
        body, name="fox_attn", grid=(F_H // 2, len(live)),
        in_specs=[pl.BlockSpec((FTQ, 256), lambda p, t, qi, kj: (qi[t], p)), kv, kv],
        out_specs=[qo, qo],
        out_shape=[sd((S, 1024), f32), sd((S, 1024), f32)],
        scratch_shapes=[pltpu.VMEM((2, FTQ, 128), f32), pltpu.VMEM((2, FTQ, 128), f32)],
        sem=("parallel", "arbitrary"), args=(qa, ka, v), comm=comm, prefetch=(qi_tab, kj_tab))


def fox_attn_bwd(qa, ka, v, do, lse, delta, comm=None):
    S = qa.shape[0]
    nq = S // FTQ

    live = [(j, i) for j in range(nq) for i in range(j, nq)]
    kj_tab = jnp.asarray([j for j, _ in live], jnp.int32)
    qi_tab = jnp.asarray([i for _, i in live], jnp.int32)

    def body(kj_ref, qi_ref, q_ref, k_ref, v_ref, do_ref, lse_ref, dl_ref, dq_ref, dk_ref, dv_ref, dk_scr, dv_scr):
        t = pl.program_id(1)
        j, i = kj_ref[t], qi_ref[t]

        @pl.when(t == 0)
        def _():
            dq_ref[...] = jnp.zeros_like(dq_ref)

        @pl.when(i == j)
        def _():
            dk_scr[...] = jnp.zeros_like(dk_scr)
            dv_scr[...] = jnp.zeros_like(dv_scr)

        def step(diagonal):
            rows = pl.ds(pl.multiple_of(i * FTQ, FTQ), FTQ)
            for e in range(2):
                es, vs = slice(e * 128, (e + 1) * 128), slice(e * F_HD, (e + 1) * F_HD)
                qe, ke = q_ref[:, es], k_ref[:, es]
                dob = do_ref[:, vs].astype(bf16)
                s = _nt(qe, ke)
                if diagonal:
                    s = jnp.where(_iota((FTQ, FTQ), 0) >= _iota((FTQ, FTQ), 1), s, NEG)
                p = jnp.exp(s - lse_ref[:, e * F_HD:e * F_HD + 1])
                ds = p * (_nt(dob, v_ref[:, e * 128:e * 128 + F_HD]) - dl_ref[:, e * F_HD:e * F_HD + 1])
                dsb = ds.astype(bf16)
                dv_scr[e] += _tn(p.astype(bf16), dob)
                dk_scr[e] += _tn(dsb, qe)
                dq_ref[rows, es] += _nn(dsb, ke)

        pl.when(i > j)(functools.partial(step, False))
        pl.when(i == j)(functools.partial(step, True))

        @pl.when(i == nq - 1)
        def _():
            for e in range(2):
                dk_ref[:, e * 128:(e + 1) * 128] = dk_scr[e]
                dv_ref[:, e * F_HD:(e + 1) * F_HD] = dv_scr[e]

    sd = jax.ShapeDtypeStruct
    qi = lambda w: pl.BlockSpec((FTQ, w), lambda p, t, kj_, qi_: (qi_[t], p))
    kj = lambda w: pl.BlockSpec((FTQ, w), lambda p, t, kj_, qi_: (kj_[t], p))
    return _call(
        body, name="fox_attn_bwd", grid=(F_H // 2, len(live)),
        in_specs=[qi(256), kj(256), kj(256), qi(128), qi(128), qi(128)],
        out_specs=[pl.BlockSpec((S, 256), lambda p, t, kj_, qi_: (0, p)), kj(256), kj(128)],
        out_shape=[sd((S, 2048), f32), sd((S, 2048), f32), sd((S, 1024), f32)],
        scratch_shapes=[pltpu.VMEM((2, FTQ, 128), f32), pltpu.VMEM((2, FTQ, F_HD), f32)],
        sem=("parallel", "arbitrary"), args=(qa, ka, v, do, lse, delta), comm=comm, prefetch=(kj_tab, qi_tab))


def fox_gate(o, proj):
    S = o.shape[0]
    tm = 512

    def body(o_ref, z_ref, o2_ref):
        o2_ref[...] = (o_ref[...] * _silu(z_ref[...])).astype(bf16)

    t = pl.BlockSpec((tm, 1024), lambda i: (i, 0))
    return pl.pallas_call(
        body, name="fox_gate", grid=(S // tm,),
        in_specs=[t, pl.BlockSpec((tm, 1024), lambda i: (i, 3))], out_specs=t,
        out_shape=jax.ShapeDtypeStruct((S, 1024), bf16),
        compiler_params=_params(("parallel",)),
    )(o, proj)


def fox_gate_bwd(do2, o, proj):
    S = o.shape[0]
    tm = 256

    def body(d_ref, o_ref, z_ref, do_ref, dz_ref, dl_ref):
        lo_half = _iota((tm, 128), 1) < F_HD
        for p in range(F_H // 2):
            ps = slice(p * 128, (p + 1) * 128)
            d2, ov, z = d_ref[:, ps], o_ref[:, ps], z_ref[:, ps]
            dov = d2 * _silu(z)
            do_ref[:, ps] = dov
            dz_ref[:, ps] = (d2 * ov * _dsilu(z)).astype(bf16)
            dl_ref[:, ps] = _half_mean(dov * ov, lo_half) * float(F_HD)

    t = pl.BlockSpec((tm, 1024), lambda i: (i, 0))
    sd = jax.ShapeDtypeStruct
    return pl.pallas_call(
        body, name="fox_gate_bwd", grid=(S // tm,),
        in_specs=[t, t, pl.BlockSpec((tm, 1024), lambda i: (i, 3))], out_specs=[t, t, t],
        out_shape=[sd((S, 1024), f32), sd((S, 1024), bf16), sd((S, 1024), f32)],
        compiler_params=_params(("parallel",)),
    )(do2, o, proj)


def fox_pre_bwd(proj, fbias, qw2, kw2, dqa, dka, dv, dz):
    S = proj.shape[0]
    tm = 256
    nb = S // tm

    def body(q_ref, k_ref, f_ref, fb_ref, qw_ref, kw_ref, dqa_ref, dka_ref, dv_ref, dz_ref, dp_ref, st_ref, carry):
        i = pl.program_id(0)

        @pl.when(i == 0)
        def _():
            carry[...] = jnp.zeros_like(carry)

        lane = _iota((tm, 128), 1)
        lo_half = lane < F_HD
        lane16 = _iota((tm, 16), 1)
        dcum = jnp.zeros((tm, 16), f32)
        dws = []
        for src, w_ref, dsrc, is_q, col0 in ((q_ref, qw_ref, dqa_ref, True, 0), (k_ref, kw_ref, dka_ref, False, 1024)):
            dw = jnp.zeros((1, 128), f32)
            for p in range(F_H // 2):
                ps = slice(p * 128, (p + 1) * 128)
                x = src[:, ps]
                r = lax.rsqrt(_half_mean(x * x, lo_half) + EPS)
                xh = x * r
                d0 = dsrc[:, (2 * p) * 128:(2 * p + 1) * 128]
                d1 = dsrc[:, (2 * p + 1) * 128:(2 * p + 2) * 128]
                dy = jnp.where(lo_half, d0, pltpu.roll(d1, F_HD, 1))
                if is_q:
                    dy = dy * (F_HD ** -0.5)
                dxh = dy * w_ref[...]
                dw = dw + jnp.sum(dy * xh, axis=0, keepdims=True)
                dp_ref[:, col0 + p * 128:col0 + (p + 1) * 128] = (r * (dxh - xh * _half_mean(dxh * xh, lo_half))).astype(bf16)
                for e, de in ((0, d0), (1, d1)):
                    col = de[:, 64:65] if is_q else -de[:, 67:68]
                    dcum = dcum + jnp.where(lane16 == 2 * p + e, col, 0.0)
            dws.append(dw)
        dp_ref[:, 2048:3072] = dv_ref[...].astype(bf16)
        dp_ref[:, 3072:4096] = dz_ref[...]
        utri = jnp.where(_iota((tm, tm), 1) >= _iota((tm, tm), 0), 1.0, 0.0).astype(f32)
        dlogf = _nn(utri, dcum, HI) + carry[0:1, :]
        carry[0:1, :] = dlogf[0:1, :]
        fl = f_ref[:, 0:16] + fb_ref[...]
        df = dlogf * _sigmoid(-fl)
        place = jnp.where(_iota((16, 128), 1) == _iota((16, 128), 0), 1.0, 0.0).astype(f32)
        dfw = _nn(df, place, HI)
        dp_ref[:, F_F0:F_INP] = dfw.astype(bf16)
        upd = jnp.concatenate(dws + [jnp.sum(dfw, axis=0, keepdims=True), jnp.zeros((5, 128), f32)], axis=0)

        @pl.when(i == 0)
        def _():
            st_ref[...] = upd

        @pl.when(i > 0)
        def _():
            st_ref[...] += upd

    rev = lambda w, c: pl.BlockSpec((tm, w), lambda i: (nb - 1 - i, c))
    vec = lambda n: pl.BlockSpec((1, n), lambda i: (0, 0))
    sd = jax.ShapeDtypeStruct
    return pl.pallas_call(
        body, name="fox_pre_bwd", grid=(nb,),
        in_specs=[rev(1024, 0), rev(1024, 1), rev(128, F_F0 // 128), vec(16), vec(128), vec(128),
                  rev(2048, 0), rev(2048, 0), rev(1024, 0), rev(1024, 0)],
        out_specs=[rev(F_INP, 0), pl.BlockSpec((8, 128), lambda i: (0, 0))],
        out_shape=[sd((S, F_INP), bf16), sd((8, 128), f32)],
        scratch_shapes=[pltpu.VMEM((8, 16), f32)],
        compiler_params=_params(("arbitrary",)),
    )(proj, proj, proj, fbias, qw2, kw2, dqa, dka, dv, dz)


def _me():
    return lax.axis_index("x"), lax.axis_index("y"), lax.axis_index("c")


def _other_chips(x, y):
    return [(1 - x, y), (x, 1 - y), (1 - x, 1 - y)]


def ag_small(xs):
    m_per, n = xs.shape

    def body(x_ref, out_ref, send_sems, recv_sems, local_sem):
        x, y, c = _me()
        me, sibling = (x, y, c), (x, y, 1 - c)
        chips = _other_chips(x, y)

        def rows(px, py, pc):
            return out_ref.at[pl.ds((4 * px + 2 * py + pc) * m_per, m_per), :]

        def copy(k, block, to, src=None):
            return pltpu.make_async_remote_copy(
                src_ref=rows(*block) if src is None else src, dst_ref=rows(*block),
                send_sem=send_sems.at[k], recv_sem=recv_sems.at[k], device_id=to, device_id_type=MESH)

        mine = pltpu.make_async_copy(x_ref, rows(*me), local_sem)
        mine.start()
        first = [copy(0, me, sibling, src=x_ref)]
        first += [copy(1 + j, me, (*chip, c), src=x_ref) for j, chip in enumerate(chips)]
        for cp in first:
            cp.start()
        passed = [copy(4 + j, (*chip, c), sibling) for j, chip in enumerate(chips)]
        for j, chip in enumerate(chips):
            copy(1 + j, (*chip, c), me).wait_recv()
            passed[j].start()
        copy(0, sibling, me).wait_recv()
        for j, chip in enumerate(chips):
            copy(4 + j, (*chip, 1 - c), me).wait_recv()
        for cp in first + passed:
            cp.wait_send()
        mine.wait()

    return pl.pallas_call(
        body, name="ag_small",
        out_shape=jax.ShapeDtypeStruct((8 * m_per, n), xs.dtype),
        in_specs=[pl.BlockSpec(memory_space=pltpu.VMEM)], out_specs=pl.BlockSpec(memory_space=pltpu.VMEM),
        scratch_shapes=[pltpu.SemaphoreType.DMA((7,)), pltpu.SemaphoreType.DMA((7,)), pltpu.SemaphoreType.DMA],
        compiler_params=pltpu.CompilerParams(vmem_limit_bytes=VMEM_LIMIT),
    )(xs)


_ANY = pl.BlockSpec(memory_space=pl.ANY)


def ag_chips(arrs):
    n = len(arrs)
    assert all(a.shape[0] == 2 for a in arrs)

    def body(*refs):
        ins, outs = refs[:n], refs[n:2 * n]
        send_sems, recv_sems, fwd_send, fwd_recv, local_sems = refs[2 * n:]
        x, y, c = _me()
        me = 2 * x + y
        chips = _other_chips(x, y)
        started = []
        for a in range(n):
            cp = pltpu.make_async_copy(ins[a], outs[a].at[me], local_sems.at[a])
            cp.start()
            started.append(cp)
        sends = []
        for a in range(n):
            for j, (px, py) in enumerate(chips):
                r = pltpu.make_async_remote_copy(
                    src_ref=ins[a].at[c], dst_ref=outs[a].at[me, c], send_sem=send_sems.at[3 * a + j],
                    recv_sem=recv_sems.at[3 * a + j], device_id=(px, py, c), device_id_type=MESH)
                r.start()
                sends.append(r)
        for a in range(n):
            for j, (px, py) in enumerate(chips):
                got = outs[a].at[2 * px + py, c]
                pltpu.make_async_remote_copy(
                    src_ref=ins[a].at[c], dst_ref=got, send_sem=send_sems.at[3 * a + j],
                    recv_sem=recv_sems.at[3 * a + j], device_id=(px, py, c), device_id_type=MESH).wait_recv()
                f = pltpu.make_async_remote_copy(
                    src_ref=got, dst_ref=got, send_sem=fwd_send.at[3 * a + j], recv_sem=fwd_recv.at[3 * a + j],
                    device_id=(x, y, 1 - c), device_id_type=MESH)
                f.start()
                sends.append(f)
        for a in range(n):
            for j, (px, py) in enumerate(chips):
                theirs = outs[a].at[2 * px + py, 1 - c]
                pltpu.make_async_remote_copy(
                    src_ref=theirs, dst_ref=theirs, send_sem=fwd_send.at[3 * a + j], recv_sem=fwd_recv.at[3 * a + j],
                    device_id=(x, y, 1 - c), device_id_type=MESH).wait_recv()
        for r in sends:
            r.wait_send()
        for cp in started:
            cp.wait()

    sems = pltpu.SemaphoreType.DMA((3 * n,))
    return pl.pallas_call(
        body, name="ag_chips",
        out_shape=[jax.ShapeDtypeStruct((4,) + a.shape, a.dtype) for a in arrs],
        in_specs=[_ANY] * n, out_specs=[_ANY] * n,
        scratch_shapes=[sems, sems, sems, sems, pltpu.SemaphoreType.DMA((n,))],
    )(*arrs)


def _ag_comm(arrs):
    n = len(arrs)

    def copies(ins, outs, sems):
        send_sems, recv_sems, local_sems = sems
        x, y, c = _me()
        me = 2 * x + y
        local = [pltpu.make_async_copy(ins[a], outs[a].at[me], local_sems.at[a]) for a in range(n)]
        out_cp, in_cp = [], []
        for a in range(n):
            for j, (px, py) in enumerate(_other_chips(x, y)):
                mk = functools.partial(pltpu.make_async_remote_copy, src_ref=ins[a], send_sem=send_sems.at[3 * a + j],
                                       recv_sem=recv_sems.at[3 * a + j], device_id=(px, py, c), device_id_type=MESH)
                out_cp.append(mk(dst_ref=outs[a].at[me]))
                in_cp.append(mk(dst_ref=outs[a].at[2 * px + py]))
        return local, out_cp, in_cp

    def start(ins, outs, sems):
        local, out_cp, _ = copies(ins, outs, sems)
        for cp in local + out_cp:
            cp.start()

    def wait(ins, outs, sems):
        local, out_cp, in_cp = copies(ins, outs, sems)
        for cp in in_cp:
            cp.wait_recv()
        for cp in out_cp:
            cp.wait_send()
        for cp in local:
            cp.wait()

    sems = [pltpu.SemaphoreType.DMA((3 * n,)), pltpu.SemaphoreType.DMA((3 * n,)), pltpu.SemaphoreType.DMA((n,))]
    return _Comm(arrs, [jax.ShapeDtypeStruct((4,) + a.shape, a.dtype) for a in arrs], sems, start, wait)


def _rs_comm(gs):
    n = len(gs)
    flips = [(fx, fy, fc) for fx in (0, 1) for fy in (0, 1) for fc in (0, 1)][1:]

    def copies(ins, outs, sems):
        send_sems, recv_sems, local_sems = sems
        x, y, c = _me()
        me = 4 * x + 2 * y + c
        local, out_cp, in_cp = [], [], []
        for a in range(n):
            rh = ins[a].shape[1] // 2
            mine = ins[a].at[2 * x + y, pl.ds(c * rh, rh), :]
            local.append(pltpu.make_async_copy(mine, outs[a].at[me], local_sems.at[a]))
            for j, (fx, fy, fc) in enumerate(flips):
                px, py, pc = (1 - x if fx else x), (1 - y if fy else y), (1 - c if fc else c)
                mk = functools.partial(pltpu.make_async_remote_copy, send_sem=send_sems.at[7 * a + j],
                                       recv_sem=recv_sems.at[7 * a + j], device_id=(px, py, pc), device_id_type=MESH)
                out_cp.append(mk(src_ref=ins[a].at[2 * px + py, pl.ds(pc * rh, rh), :], dst_ref=outs[a].at[me]))
                in_cp.append(mk(src_ref=mine, dst_ref=outs[a].at[4 * px + 2 * py + pc]))
        return local, out_cp, in_cp

    def start(ins, outs, sems):
        local, out_cp, _ = copies(ins, outs, sems)
        for cp in local + out_cp:
            cp.start()

    def wait(ins, outs, sems):
        local, out_cp, in_cp = copies(ins, outs, sems)
        for cp in in_cp:
            cp.wait_recv()
        for cp in out_cp:
            cp.wait_send()
        for cp in local:
            cp.wait()

    sems = [pltpu.SemaphoreType.DMA((7 * n,)), pltpu.SemaphoreType.DMA((7 * n,)), pltpu.SemaphoreType.DMA((n,))]
    return _Comm(gs, [jax.ShapeDtypeStruct((8, g.shape[1] // 2, g.shape[2]), g.dtype) for g in gs], sems, start, wait)


def sum_leading(q, name):
    K, R, C = q.shape
    tr = _pick(R, (256, 128, 64, 32, 16, 8))

    def body(q_ref, o_ref):
        acc = q_ref[0]
        for k in range(1, K):
            acc = acc + q_ref[k]
        o_ref[...] = acc

    return pl.pallas_call(
        body, name=name, grid=(R // tr,),
        in_specs=[pl.BlockSpec((K, tr, C), lambda i: (0, i, 0))], out_specs=pl.BlockSpec((tr, C), lambda i: (i, 0)),
        out_shape=jax.ShapeDtypeStruct((R, C), f32),
        compiler_params=_params(("parallel",)),
    )(q)


def rs_sum_devices(q, cidx):
    K, R, C = q.shape
    tr = _pick(R, (256, 128))

    def body(c_ref, q_ref, o_ref):
        acc = q_ref[0].astype(f32)
        for k in range(1, K):
            acc = acc + q_ref[k].astype(f32)
        o_ref[0] = acc

    return pl.pallas_call(
        body, name="rs_sum_devices",
        grid_spec=pltpu.PrefetchScalarGridSpec(
            num_scalar_prefetch=1, grid=(R // tr,),
            in_specs=[pl.BlockSpec((K, tr, C), lambda i, c_ref: (0, i, 0))],
            out_specs=pl.BlockSpec((1, tr, C), lambda i, c_ref: (c_ref[0], i, 0))),
        out_shape=jax.ShapeDtypeStruct((2, R, C), f32),
        compiler_params=_params(("parallel",)),
    )(cidx, q)


def rs_share_halves(rs):
    n = len(rs)

    def body(*refs):
        bufs = refs[n:2 * n]
        send_sems, recv_sems = refs[2 * n:]
        x, y, c = _me()
        cps = []
        for a in range(n):
            cp = pltpu.make_async_remote_copy(
                src_ref=bufs[a].at[c], dst_ref=bufs[a].at[c], send_sem=send_sems.at[a], recv_sem=recv_sems.at[a],
                device_id=(x, y, 1 - c), device_id_type=MESH)
            cp.start()
            cps.append(cp)
        for a, cp in enumerate(cps):
            pltpu.make_async_remote_copy(
                src_ref=bufs[a].at[c], dst_ref=bufs[a].at[1 - c], send_sem=send_sems.at[a], recv_sem=recv_sems.at[a],
                device_id=(x, y, 1 - c), device_id_type=MESH).wait_recv()
            cp.wait_send()

    return pl.pallas_call(
        body, name="rs_share_halves",
        out_shape=[jax.ShapeDtypeStruct(r.shape, r.dtype) for r in rs],
        in_specs=[_ANY] * n, out_specs=[_ANY] * n, input_output_aliases={a: a for a in range(n)},
        scratch_shapes=[pltpu.SemaphoreType.DMA((n,)), pltpu.SemaphoreType.DMA((n,))],
    )(*rs)


def ada_mod(c_all, ada_w):
    L, _, n = ada_w.shape

    def body(c_ref, w_ref, o_ref):
        o_ref[0] = _nn(_silu(c_ref[...]), w_ref[0], HI)

    return pl.pallas_call(
        body, name="ada_mod", grid=(L,),
        in_specs=[pl.BlockSpec((8, D), lambda l: (0, 0)), pl.BlockSpec((1, D, n), lambda l: (l, 0, 0))],
        out_specs=pl.BlockSpec((1, 8, n), lambda l: (l, 0, 0)),
        out_shape=jax.ShapeDtypeStruct((L, 8, n), f32),
        compiler_params=_params(("parallel",)),
    )(c_all, ada_w)


def ada_w_grad(c_all, dmod):
    L, _, n = dmod.shape

    def body(c_ref, d_ref, o_ref):
        o_ref[0] = _tn(_silu(c_ref[...]), d_ref[0], HI)

    return pl.pallas_call(
        body, name="ada_w_grad", grid=(L,),
        in_specs=[pl.BlockSpec((8, D), lambda l: (0, 0)), pl.BlockSpec((1, 8, n), lambda l: (l, 0, 0))],
        out_specs=pl.BlockSpec((1, D, n), lambda l: (l, 0, 0)),
        out_shape=jax.ShapeDtypeStruct((L, D, n), f32),
        compiler_params=_params(("parallel",)),
    )(c_all, dmod)


def adamw(w, g, m, v, name):
    shp = w.shape
    two = lambda a: a.reshape(-1, shp[-1])
    R, C = two(w).shape
    tr = _pick(R, (256, 128, 64, 32, 16, 8))
    bc1, bc2 = 1.0 - B1 ** STEP, 1.0 - B2 ** STEP

    def body(w_ref, g_ref, m_ref, v_ref, d_ref, mo_ref, vo_ref):
        gv = g_ref[...]
        mn = B1 * m_ref[...] + (1.0 - B1) * gv
        vn = B2 * v_ref[...] + (1.0 - B2) * (gv * gv)
        d_ref[...] = -LR * ((mn / bc1) / (jnp.sqrt(vn / bc2) + AEPS) + WD * w_ref[...])
        mo_ref[...] = mn
        vo_ref[...] = vn

    t = pl.BlockSpec((tr, C), lambda i: (i, 0))
    outs = pl.pallas_call(
        body, name=name, grid=(R // tr,),
        in_specs=[t] * 4, out_specs=[t] * 3, out_shape=[jax.ShapeDtypeStruct((R, C), f32)] * 3,
        compiler_params=_params(("parallel",)),
    )(two(w), two(g), two(m), two(v))
    return [o.reshape(shp) for o in outs]


def _pack(arrs):
    parts, offs, r0 = [], [], 0
    for a in arrs:
        n = a.size
        rows = -(-n // 1024) * 8
        parts.append(jnp.pad(a.reshape(-1), (0, rows * 128 - n)).reshape(rows, 128))
        offs.append((r0, rows))
        r0 += rows
    return jnp.concatenate(parts, axis=0), offs


def _unpack(buf, offs, shapes):
    out = []
    for (r0, rows), shp in zip(offs, shapes):
        n = 1
        for d in shp:
            n *= d
        out.append(buf[..., r0:r0 + rows, :].reshape(buf.shape[:-2] + (rows * 128,))[..., :n].reshape(buf.shape[:-2] + tuple(shp)))
    return out


def kernel(x, c, norm_w, ada_w, ada_b, a_w_in, a_conv_w, a_A_log, a_dt_bias, a_norm_w, a_w_out, b_w_in, b_f_bias, b_qn_w, b_kn_w, b_w_out, final_norm_w, loss_target, m_norm_w, m_ada_w, m_ada_b, m_a_w_in, m_a_conv_w, m_a_A_log, m_a_dt_bias, m_a_norm_w, m_a_w_out, m_b_w_in, m_b_f_bias, m_b_qn_w, m_b_kn_w, m_b_w_out, m_final_norm_w, v_norm_w, v_ada_w, v_ada_b, v_a_w_in, v_a_conv_w, v_a_A_log, v_a_dt_bias, v_a_norm_w, v_a_w_out, v_b_w_in, v_b_f_bias, v_b_qn_w, v_b_kn_w, v_b_w_out, v_final_norm_w):
    weights = dict(norm_w=norm_w, ada_w=ada_w, ada_b=ada_b, a_w_in=a_w_in, a_conv_w=a_conv_w, a_A_log=a_A_log,
                   a_dt_bias=a_dt_bias, a_norm_w=a_norm_w, a_w_out=a_w_out, b_w_in=b_w_in, b_f_bias=b_f_bias,
                   b_qn_w=b_qn_w, b_kn_w=b_kn_w, b_w_out=b_w_out, final_norm_w=final_norm_w)
    m_in = dict(norm_w=m_norm_w, ada_w=m_ada_w, ada_b=m_ada_b, a_w_in=m_a_w_in, a_conv_w=m_a_conv_w, a_A_log=m_a_A_log,
                a_dt_bias=m_a_dt_bias, a_norm_w=m_a_norm_w, a_w_out=m_a_w_out, b_w_in=m_b_w_in, b_f_bias=m_b_f_bias,
                b_qn_w=m_b_qn_w, b_kn_w=m_b_kn_w, b_w_out=m_b_w_out, final_norm_w=m_final_norm_w)
    v_in = dict(norm_w=v_norm_w, ada_w=v_ada_w, ada_b=v_ada_b, a_w_in=v_a_w_in, a_conv_w=v_a_conv_w, a_A_log=v_a_A_log,
                a_dt_bias=v_a_dt_bias, a_norm_w=v_a_norm_w, a_w_out=v_a_w_out, b_w_in=v_b_w_in, b_f_bias=v_b_f_bias,
                b_qn_w=v_b_qn_w, b_kn_w=v_b_kn_w, b_w_out=v_b_w_out, final_norm_w=v_final_norm_w)
    xi, yi, ci = _me()
    me_b, me_k = 4 * xi + 2 * yi + ci, 2 * xi + yi
    cidx = ci.astype(jnp.int32).reshape(1)
    S = x.shape[1]
    depth, n_a, n_b = norm_w.shape[0], a_w_in.shape[0], b_w_in.shape[0]
    x0, tgt = x.reshape(S, D), loss_target.reshape(S, D)

    c_all = ag_small(jnp.pad(c, ((0, 7), (0, 0)))).reshape(8, 8, D)[:, 0]
    nloc = ada_w.shape[2]
    parts = ag_small(ada_mod(c_all, ada_w).reshape(depth * 8, nloc)).reshape(4, 2, depth, 8, nloc)[:, 0]
    mine = lax.dynamic_index_in_dim(parts, me_b, axis=2, keepdims=False)
    mod = jnp.transpose(mine, (1, 0, 2)).reshape(depth, 4 * nloc) + ada_b
    shift, scale, gate = (mod[:, k * D:(k + 1) * D] for k in range(3))

    w_loc = [(a_w_in[i // 2] if i % 2 == 0 else b_w_in[i // 2]).astype(bf16) for i in range(depth)]
    wo_loc = [(a_w_out[i // 2] if i % 2 == 0 else b_w_out[i // 2]).astype(bf16) for i in range(depth)]
    pad_in = [(G_INP - G_IN) if i % 2 == 0 else (F_INP - F_IN) for i in range(depth)]
    halves = lambda w: w.reshape((2, w.shape[0] // 2) + w.shape[1:])

    def cols_in_place(g_in, pad):
        w = jnp.transpose(g_in, (1, 0, 2)).reshape(g_in.shape[1], -1)
        return jnp.pad(w, ((0, 0), (0, pad)))

    g_in0, g_conv = ag_chips([halves(w_loc[0]), a_conv_w])
    w_in_full = [cols_in_place(g_in0.reshape((4,) + w_loc[0].shape), pad_in[0])]
    w_out_full = []
    conv = [jnp.transpose(g_conv[:, l], (1, 0, 2)).reshape(CONV_K, -1) for l in range(n_a)]
    qw2 = [_row(jnp.tile(b_qn_w[l], 2)) for l in range(n_b)]
    kw2 = [_row(jnp.tile(b_kn_w[l], 2)) for l in range(n_b)]

    saved, xc = [], x0
    for i in range(depth):
        l = i // 2
        nxt = _ag_comm([w_loc[i + 1], wo_loc[i + 1]]) if i + 1 < depth else None
        h = ln_mod(xc, _row(norm_w[i]), _row(scale[i]), _row(shift[i]))
        name = "mm_a_in" if i % 2 == 0 else "mm_b_in"
        if i == 0:
            proj, got = matmul(h, w_in_full[0], "nn", name, comm=_ag_comm([wo_loc[0]]))
            w_out_full.append(got[0].reshape(-1, D))
        else:
            proj = matmul(h, w_in_full[i], "nn", name)
        if i % 2 == 0:
            pre = gdn_pre(proj, conv[l], _row(a_A_log[l]), _row(a_dt_bias[l]))
            res, got = gdn_fwd(*pre, comm=nxt)
            o2 = gdn_onorm(res[0], proj, _row(a_norm_w[l]))
            y, xn = out_proj(o2, w_out_full[i], xc, _row(gate[i]), "out_proj_a")
        else:
            pre = fox_pre(proj, _row(b_f_bias[l]), qw2[l], kw2[l])
            res, got = fox_attn(*pre, comm=nxt)
            o2 = fox_gate(res[0], proj)
            y, xn = out_proj(o2, w_out_full[i], xc, _row(gate[i]), "out_proj_b")
        saved.append((xc, h, proj, o2, y, pre, res))
        if nxt is not None:
            w_in_full.append(cols_in_place(got[0], pad_in[i + 1]))
            w_out_full.append(got[1].reshape(-1, D))
        xc = xn
    dx, st_f = final_loss(xc, _row(final_norm_w), tgt)

    d_norm, d_mod = [None] * depth, [None] * depth
    d_conv, d_alog, d_dtb, d_anw = [None] * n_a, [None] * n_a, [None] * n_a, [None] * n_a
    d_fb, d_qn, d_kn = [None] * n_b, [None] * n_b, [None] * n_b
    ex_in, ex_out, pend_in = [None] * depth, [None] * depth, None
    for i in reversed(range(depth)):
        l = i // 2
        xin, h, proj, o2, y, pre, res = saved[i]
        ab = "a" if i % 2 == 0 else "b"
        dy, st_g = gate_bwd(dx, y, _row(gate[i]))
        do2 = matmul(dy, w_out_full[i], "nt", f"mm_{ab}_do2")
        d_out = matmul(o2, dy, "tn", f"mm_{ab}_dwo")
        ride = _rs_comm(([] if pend_in is None else [pend_in]) + [d_out.reshape(4, d_out.shape[0] // 4, D).astype(bf16)])
        if i % 2 == 0:
            o, wv, at, tinv, vn, st = res
            do, dz, st_o = gdn_onorm_bwd(do2, o, proj, _row(a_norm_w[l]))
            grads, got = gdn_bwd(do, *pre, wv, at, tinv, vn, st, comm=ride)
            dcv, dba, st_s = gdn_pre_bwd(proj, conv[l], _row(a_A_log[l]), _row(a_dt_bias[l]), *grads)
            dproj, dcw = gdn_conv_bwd(proj, conv[l], dcv, dz, dba)
            d_conv[l], d_alog[l], d_dtb[l], d_anw[l] = dcw[:CONV_K], st_s[0], st_s[1], st_o[0]
        else:
            o, lse = res
            do, dz, delta = fox_gate_bwd(do2, o, proj)
            (dqa, dka, dv), got = fox_attn_bwd(*pre, do, lse, delta, comm=ride)
            dproj, st_b = fox_pre_bwd(proj, _row(b_f_bias[l]), qw2[l], kw2[l], dqa, dka, dv, dz)
            d_fb[l], d_qn[l], d_kn[l] = st_b[2, :F_H], st_b[0, :F_HD] + st_b[0, F_HD:], st_b[1, :F_HD] + st_b[1, F_HD:]
        ex_out[i] = got[-1]
        if pend_in is not None:
            ex_in[i + 1] = got[0]
        d_in = matmul(h, dproj, "tn", f"mm_{ab}_dw")
        cl = w_loc[i].shape[1]
        pend_in = jnp.transpose(d_in[:, :4 * cl].reshape(d_in.shape[0], 4, cl), (1, 0, 2)).astype(bf16)
        if i == 0:
            dh, got = matmul(dproj, w_in_full[i], "nt", f"mm_{ab}_dh", comm=_rs_comm([pend_in]))
            ex_in[0] = got[0]
        else:
            dh = matmul(dproj, w_in_full[i], "nt", f"mm_{ab}_dh")
        dx, st_n = ln_mod_bwd(xin, _row(norm_w[i]), _row(scale[i]), dh, dx)
        d_norm[i] = st_n[0]
        d_mod[i] = jnp.concatenate([st_n[2], st_n[1], st_g[0]])

    small = [jnp.stack(d_norm), jnp.stack(d_mod), jnp.stack(d_conv), jnp.stack(d_alog), jnp.stack(d_dtb), jnp.stack(d_anw),
             jnp.stack(d_fb), jnp.stack(d_qn), jnp.stack(d_kn), st_f[0], jnp.sum(st_f[1]).reshape(1)]
    shapes = [a.shape for a in small]
    buf, offs = _pack(small)
    gathered = ag_small(buf).reshape(8, buf.shape[0], 128)
    tot = _unpack(sum_leading(gathered, "sum_devices"), offs, shapes)
    g_norm, g_adab, g_convf, g_alog, g_dtb, g_anw, g_fb, g_qn, g_kn, g_fin, loss = tot
    dmod_all = _unpack(gathered, offs[1:2], shapes[1:2])[0]
    dmod_loc = lax.dynamic_slice_in_dim(dmod_all, me_k * nloc, nloc, axis=2)
    g_adaw = ada_w_grad(c_all, jnp.transpose(dmod_loc, (1, 0, 2)))
    g_conv_loc = lax.dynamic_slice_in_dim(g_convf, me_k * a_conv_w.shape[2], a_conv_w.shape[2], axis=2)

    flat = [q for i in range(depth) for q in (ex_in[i], ex_out[i])]
    done = rs_share_halves([rs_sum_devices(q, cidx) for q in flat])
    red = [d.reshape(-1, d.shape[-1]) for d in done]
    r_in, r_out = red[0::2], red[1::2]
    grads = dict(norm_w=g_norm, ada_w=g_adaw, ada_b=g_adab, a_w_in=jnp.stack(r_in[0::2]), a_conv_w=g_conv_loc,
                 a_A_log=g_alog, a_dt_bias=g_dtb, a_norm_w=g_anw, a_w_out=jnp.stack(r_out[0::2]),
                 b_w_in=jnp.stack(r_in[1::2]), b_f_bias=g_fb, b_qn_w=g_qn, b_kn_w=g_kn,
                 b_w_out=jnp.stack(r_out[1::2]), final_norm_w=g_fin)
    names = list(weights)
    upd = {n: adamw(weights[n], grads[n], m_in[n], v_in[n], "adamw_" + n) for n in names}
    return (loss.reshape(()), dx.reshape(x.shape), *[grads[n] for n in names], *[upd[n][0] for n in names],
            *[upd[n][1] for n in names], *[upd[n][2] for n in names])
```

```python
import functools

import jax
import jax.numpy as jnp
from jax import lax
from jax.experimental import pallas as pl
from jax.experimental.pallas import tpu as pltpu

f32, bf16 = jnp.float32, jnp.bfloat16
HI = lax.Precision.HIGHEST
MESH = pl.DeviceIdType.MESH

EPS = 1e-6
D = 1024
CHUNK = 64
GQK_H, GV_H, GHD = 8, 16, 128
G_CONV = 4096
G_Z0 = 4096
G_BA0 = 6144
G_IN, G_INP = 6176, 6272
CONV_K = 4
F_H, F_HD = 16, 64
F_W = 1024
F_F0 = 4096
F_IN, F_INP = 4112, 4224
LR, B1, B2, AEPS, WD, STEP = 0.001, 0.9, 0.999, 1e-08, 0.01, 10
NEG = -1e30
VMEM_LIMIT = 56 * 1024 * 1024


def _nn(a, b, prec=None):
    return lax.dot_general(a, b, (((1,), (0,)), ((), ())), preferred_element_type=f32, precision=prec)


def _nt(a, b, prec=None):
    return lax.dot_general(a, b, (((1,), (1,)), ((), ())), preferred_element_type=f32, precision=prec)


def _tn(a, b, prec=None):
    return lax.dot_general(a, b, (((0,), (0,)), ((), ())), preferred_element_type=f32, precision=prec)


def _iota(shape, axis):
    return lax.broadcasted_iota(jnp.int32, shape, axis)


def _sigmoid(x):
    return 0.5 * jnp.tanh(0.5 * x) + 0.5


def _softplus(x):
    return jnp.maximum(x, 0.0) + jnp.log(1.0 + jnp.exp(-jnp.abs(x)))


def _silu(x):
    return x * _sigmoid(x)


def _dsilu(x):
    s = _sigmoid(x)
    return s * (1.0 + x * (1.0 - s))


def _params(sem=None, vmem=VMEM_LIMIT):
    return pltpu.CompilerParams(dimension_semantics=sem, vmem_limit_bytes=vmem)


def _row(v):
    return v.reshape(1, -1)


class _Comm:
    def __init__(self, ins, out_shapes, sems, start, wait):
        self.ins, self.out_shapes, self.sems, self.start, self.wait = list(ins), list(out_shapes), list(sems), start, wait


def _call(body, *, name, grid, in_specs, out_specs, out_shape, scratch_shapes, sem, args, comm=None, prefetch=()):
    n_pf, n_in, n_out, n_s = len(prefetch), len(in_specs), len(out_specs), len(scratch_shapes)
    n_ci, n_co = (len(comm.ins), len(comm.out_shapes)) if comm is not None else (0, 0)

    def wrapped(*refs):
        pf, refs = refs[:n_pf], refs[n_pf:]
        core_in, c_in = refs[:n_in], refs[n_in:n_in + n_ci]
        o0 = n_in + n_ci
        core_out, c_out = refs[o0:o0 + n_out], refs[o0 + n_out:o0 + n_out + n_co]
        s0 = o0 + n_out + n_co
        core_s, c_sem = refs[s0:s0 + n_s], refs[s0 + n_s:]
        if comm is not None:
            first = functools.reduce(jnp.logical_and, [pl.program_id(d) == 0 for d in range(len(grid))])
            pl.when(first)(functools.partial(comm.start, c_in, c_out, c_sem))
        body(*pf, *core_in, *core_out, *core_s)
        if comm is not None:
            last = functools.reduce(jnp.logical_and, [pl.program_id(d) == grid[d] - 1 for d in range(len(grid))])
            pl.when(last)(functools.partial(comm.wait, c_in, c_out, c_sem))

    extra = ([], [], [], []) if comm is None else ([_ANY] * n_ci, [_ANY] * n_co, comm.out_shapes, comm.sems)
    spec = pltpu.PrefetchScalarGridSpec(
        num_scalar_prefetch=n_pf, grid=grid, in_specs=list(in_specs) + extra[0], out_specs=list(out_specs) + extra[1],
        scratch_shapes=list(scratch_shapes) + extra[3])
    outs = pl.pallas_call(
        wrapped, name=name if comm is None else name + "_x", grid_spec=spec, out_shape=list(out_shape) + extra[2],
        compiler_params=_params(sem if comm is None else ("arbitrary",) * len(grid)),
    )(*prefetch, *args, *(comm.ins if comm is not None else []))
    return outs[:n_out], outs[n_out:]


def _pick(n, pref):
    for t in pref:
        if n % t == 0:
            return t
    return n


MM_VMEM_BUDGET = 44 * 1024 * 1024


def _mm_tiles(M, N, K):
    best = None
    for tk in [K] + [t for t in (2048, 1408, 1024, 896, 512, 384, 256, 128) if K % t == 0 and t < K]:
        for tm in (2048, 1024, 512, 256, 128):
            for tn in (1408, 1024, 896, 512, 384, 256, 128):
                if M % tm or N % tn:
                    continue
                nk = K // tk
                need = 2 * 2 * (tm * tk + tk * tn) + 2 * 4 * tm * tn + (4 * tm * tn if nk > 1 else 0)
                if need <= MM_VMEM_BUDGET:
                    cand = ((nk, -tm * tn), (tm, tn, tk))
                    best = cand if best is None or cand[0] < best[0] else best
    return best[1]


def matmul(a, b, mode, name, out_dtype=f32, comm=None):
    if mode == "nn":
        (M, K), (_, N) = a.shape, b.shape
    elif mode == "nt":
        (M, K), (N, _) = a.shape, b.shape
    else:
        (K, M), (_, N) = a.shape, b.shape
    tm, tn, tk = _mm_tiles(M, N, K)
    nk = K // tk
    dot = {"nn": _nn, "nt": _nt, "tn": _tn}[mode]

    def body(a_ref, b_ref, o_ref, *acc):
        k = pl.program_id(2)
        part = dot(a_ref[...], b_ref[...])
        if nk == 1:
            o_ref[...] = part.astype(out_dtype)
        else:
            acc_ref = acc[0]

            @pl.when(k == 0)
            def _():
                acc_ref[...] = part

            @pl.when(k > 0)
            def _():
                acc_ref[...] += part

            @pl.when(k == nk - 1)
            def _():
                o_ref[...] = acc_ref[...].astype(out_dtype)

    a_spec = pl.BlockSpec((tk, tm), lambda i, j, k: (k, i)) if mode == "tn" else pl.BlockSpec((tm, tk), lambda i, j, k: (i, k))
    b_spec = pl.BlockSpec((tn, tk), lambda i, j, k: (j, k)) if mode == "nt" else pl.BlockSpec((tk, tn), lambda i, j, k: (k, j))
    outs, got = _call(
        body, name=name, grid=(M // tm, N // tn, nk),
        in_specs=[a_spec, b_spec], out_specs=[pl.BlockSpec((tm, tn), lambda i, j, k: (i, j))],
        out_shape=[jax.ShapeDtypeStruct((M, N), out_dtype)],
        scratch_shapes=[] if nk == 1 else [pltpu.VMEM((tm, tn), f32)],
        sem=("parallel", "parallel", "arbitrary"), args=(a, b), comm=comm)
    return outs[0] if comm is None else (outs[0], got)


def out_proj(o2, w, x, gate, name):
    S, K = o2.shape
    N = w.shape[1]
    tm, tn = 512, 512

    def body(a_ref, b_ref, x_ref, g_ref, y_ref, xn_ref):
        y = _nn(a_ref[...], b_ref[...])
        y_ref[...] = y
        xn_ref[...] = x_ref[...] + g_ref[...] * y

    return pl.pallas_call(
        body, name=name, grid=(S // tm, N // tn),
        in_specs=[pl.BlockSpec((tm, K), lambda i, j: (i, 0)), pl.BlockSpec((K, tn), lambda i, j: (0, j)),
                  pl.BlockSpec((tm, tn), lambda i, j: (i, j)), pl.BlockSpec((1, tn), lambda i, j: (0, j))],
        out_specs=[pl.BlockSpec((tm, tn), lambda i, j: (i, j))] * 2,
        out_shape=[jax.ShapeDtypeStruct((S, N), f32)] * 2,
        compiler_params=_params(("parallel", "parallel")),
    )(o2, w, x, gate)


def ln_mod(x, nw, scale, shift):
    S = x.shape[0]
    tm = 512

    def body(x_ref, nw_ref, sc_ref, sh_ref, h_ref):
        xv = x_ref[...]
        r = lax.rsqrt(jnp.mean(xv * xv, axis=-1, keepdims=True) + EPS)
        h_ref[...] = ((xv * r) * nw_ref[...] * (1.0 + sc_ref[...]) + sh_ref[...]).astype(bf16)

    vec = pl.BlockSpec((1, D), lambda i: (0, 0))
    return pl.pallas_call(
        body, name="ln_mod", grid=(S // tm,),
        in_specs=[pl.BlockSpec((tm, D), lambda i: (i, 0)), vec, vec, vec],
        out_specs=pl.BlockSpec((tm, D), lambda i: (i, 0)),
        out_shape=jax.ShapeDtypeStruct((S, D), bf16),
        compiler_params=_params(("parallel",)),
    )(x, nw, scale, shift)


def ln_mod_bwd(x, nw, scale, dh, dxres):
    S = x.shape[0]
    tm = 512
    nb = S // tm

    def body(x_ref, nw_ref, sc_ref, dh_ref, dr_ref, dx_ref, st_ref):
        i = pl.program_id(0)
        xv = x_ref[...]
        r = lax.rsqrt(jnp.mean(xv * xv, axis=-1, keepdims=True) + EPS)
        xn = xv * r
        dh = dh_ref[...]
        dxn = dh * (nw_ref[...] * (1.0 + sc_ref[...]))
        dx_ref[...] = dr_ref[...] + r * (dxn - xn * jnp.mean(dxn * xn, axis=-1, keepdims=True))
        p1 = jnp.sum(dh * xn, axis=0, keepdims=True)
        p2 = jnp.sum(dh, axis=0, keepdims=True)
        upd = jnp.concatenate([p1, p1, p2, jnp.zeros((5, D), f32)], axis=0)

        @pl.when(i == 0)
        def _():
            st_ref[...] = upd

        @pl.when(i > 0)
        def _():
            st_ref[...] += upd

        @pl.when(i == nb - 1)
        def _():
            st_ref[0:1, :] = st_ref[0:1, :] * (1.0 + sc_ref[...])
            st_ref[1:2, :] = st_ref[1:2, :] * nw_ref[...]

    vec = pl.BlockSpec((1, D), lambda i: (0, 0))
    tile = pl.BlockSpec((tm, D), lambda i: (i, 0))
    return pl.pallas_call(
        body, name="ln_mod_bwd", grid=(S // tm,),
        in_specs=[tile, vec, vec, tile, tile],
        out_specs=[tile, pl.BlockSpec((8, D), lambda i: (0, 0))],
        out_shape=[jax.ShapeDtypeStruct((S, D), f32), jax.ShapeDtypeStruct((8, D), f32)],
        compiler_params=_params(("arbitrary",)),
    )(x, nw, scale, dh, dxres)


def final_loss(x, fw, tgt):
    S = x.shape[0]
    tm = 512

    def body(x_ref, w_ref, t_ref, dx_ref, st_ref):
        i = pl.program_id(0)
        xv = x_ref[...]
        r = lax.rsqrt(jnp.mean(xv * xv, axis=-1, keepdims=True) + EPS)
        xn = xv * r
        err = xn * w_ref[...] - t_ref[...]
        dy = err * (1.0 / D)
        dxn = dy * w_ref[...]
        dx_ref[...] = r * (dxn - xn * jnp.mean(dxn * xn, axis=-1, keepdims=True))
        p1 = jnp.sum(dy * xn, axis=0, keepdims=True)
        p2 = jnp.sum(err * err, axis=0, keepdims=True) * (0.5 / D)
        upd = jnp.concatenate([p1, p2, jnp.zeros((6, D), f32)], axis=0)

        @pl.when(i == 0)
        def _():
            st_ref[...] = upd

        @pl.when(i > 0)
        def _():
            st_ref[...] += upd

    tile = pl.BlockSpec((tm, D), lambda i: (i, 0))
    return pl.pallas_call(
        body, name="final_loss", grid=(S // tm,),
        in_specs=[tile, pl.BlockSpec((1, D), lambda i: (0, 0)), tile],
        out_specs=[tile, pl.BlockSpec((8, D), lambda i: (0, 0))],
        out_shape=[jax.ShapeDtypeStruct((S, D), f32), jax.ShapeDtypeStruct((8, D), f32)],
        compiler_params=_params(("arbitrary",)),
    )(x, fw, tgt)


def gate_bwd(dx, y, gate):
    S = dx.shape[0]
    tm = 512

    def body(dx_ref, y_ref, g_ref, dy_ref, st_ref):
        i = pl.program_id(0)
        dxv = dx_ref[...]
        dy_ref[...] = (g_ref[...] * dxv).astype(bf16)
        upd = jnp.concatenate([jnp.sum(dxv * y_ref[...], axis=0, keepdims=True), jnp.zeros((7, D), f32)], axis=0)

        @pl.when(i == 0)
        def _():
            st_ref[...] = upd

        @pl.when(i > 0)
        def _():
            st_ref[...] += upd

    tile = pl.BlockSpec((tm, D), lambda i: (i, 0))
    return pl.pallas_call(
        body, name="gate_bwd", grid=(S // tm,),
        in_specs=[tile, tile, pl.BlockSpec((1, D), lambda i: (0, 0))],
        out_specs=[tile, pl.BlockSpec((8, D), lambda i: (0, 0))],
        out_shape=[jax.ShapeDtypeStruct((S, D), bf16), jax.ShapeDtypeStruct((8, D), f32)],
        compiler_params=_params(("arbitrary",)),
    )(dx, y, gate)


def _chunk_mats(tm):
    r, c = _iota((tm, tm), 0), _iota((tm, tm), 1)
    same = jnp.right_shift(r, 6) == jnp.right_shift(c, 6)
    ltri = jnp.where(same & (c <= r), 1.0, 0.0).astype(f32)
    utri = jnp.where(same & (c >= r), 1.0, 0.0).astype(f32)
    bsame = jnp.where(same, 1.0, 0.0).astype(f32)
    return ltri, utri, bsame


def _gdn_scalars(ba, alog, dtb, ltri, bsame):
    beta = _sigmoid(ba[:, 0:16])
    u = ba[:, 16:32] + dtb
    neg_a = -jnp.exp(alog)
    g = neg_a * _softplus(u)
    gc = _nn(ltri, g, HI)
    glast = _nn(bsame, g, HI)
    return beta, u, neg_a, g, gc, glast


def _conv_taps(p_ref, halo_ref, first, gi):
    cs = slice(gi * 128, (gi + 1) * 128)
    cur = p_ref[:, cs]
    hal = jnp.where(first, 0.0, halo_ref[:, cs])
    ext = jnp.concatenate([hal, cur], axis=0)
    return [cur] + [pltpu.roll(ext, s, 0)[8:] for s in range(1, CONV_K)]


def _conv(taps, w):
    cv = taps[0] * w[3:4]
    for s in range(1, CONV_K):
        cv = cv + taps[s] * w[3 - s:4 - s]
    return cv


def _l2n(x):
    return x * lax.rsqrt(jnp.sum(x * x, axis=-1, keepdims=True) + EPS)


def _gdn_in_specs(tm, S):
    nb8 = tm // 8
    return [pl.BlockSpec((tm, G_CONV), lambda i: (i, 0)),
            pl.BlockSpec((8, G_CONV), lambda i: (jnp.maximum(i * nb8 - 1, 0), 0)),
            pl.BlockSpec((tm, 128), lambda i: (i, G_BA0 // 128))]


def gdn_pre(proj, conv_w, alog, dtb):
    S = proj.shape[0]
    tm = 256
    nch = tm // CHUNK

    def body(p_ref, halo_ref, ba_ref, w_ref, al_ref, dt_ref,
             q_ref, k_ref, kb_ref, kbg_ref, vb_ref, qd_ref, kd_ref, d_ref, gl_ref):
        first = pl.program_id(0) == 0
        ltri, _, bsame = _chunk_mats(tm)
        beta, _, _, _, gc, glast = _gdn_scalars(ba_ref[...], al_ref[...], dt_ref[...], ltri, bsame)
        eg, ek, egl = jnp.exp(gc), jnp.exp(glast - gc), jnp.exp(glast)
        eye = jnp.where(_iota((16, 16), 0) == _iota((16, 16), 1), 1.0, 0.0).astype(f32)
        gct = _nt(eye, gc, HI)
        low = _iota((CHUNK, CHUNK), 0) >= _iota((CHUNK, CHUNK), 1)

        def act(gi):
            return _silu(_conv(_conv_taps(p_ref, halo_ref, first, gi), w_ref[:, gi * 128:(gi + 1) * 128]))

        for j in range(GQK_H):
            js = slice(j * 128, (j + 1) * 128)
            qn = _l2n(act(j)) * (GHD ** -0.5)
            kn = _l2n(act(GQK_H + j))
            q_ref[:, js] = qn.astype(bf16)
            k_ref[:, js] = kn.astype(bf16)
            for e in range(2):
                h = 2 * j + e
                hs = slice(h * 128, (h + 1) * 128)
                v = act(2 * GQK_H + h)
                bh, egh, ekh = beta[:, h:h + 1], eg[:, h:h + 1], ek[:, h:h + 1]
                kbv = kn * bh
                kb_ref[:, hs] = kbv.astype(bf16)
                kbg_ref[:, hs] = (kbv * egh).astype(bf16)
                vb_ref[:, hs] = (v * bh).astype(bf16)
                qd_ref[:, hs] = (qn * egh).astype(bf16)
                kd_ref[:, hs] = (kn * ekh).astype(bf16)
                for c in range(nch):
                    rs = slice(c * CHUNK, (c + 1) * CHUNK)
                    diff = gc[rs, h:h + 1] - gct[h:h + 1, rs]
                    d_ref[rs, h * CHUNK:(h + 1) * CHUNK] = jnp.where(low, jnp.exp(jnp.where(low, diff, 0.0)), 0.0)
                    gl_ref[c * 8:(c + 1) * 8, hs] = jnp.broadcast_to(egl[c * CHUNK:c * CHUNK + 8, h:h + 1], (8, 128))

    full = lambda shape: pl.BlockSpec(shape, lambda i: (0, 0))
    t1 = pl.BlockSpec((tm, 1024), lambda i: (i, 0))
    t2 = pl.BlockSpec((tm, 2048), lambda i: (i, 0))
    sd = jax.ShapeDtypeStruct
    return pl.pallas_call(
        body, name="gdn_pre", grid=(S // tm,),
        in_specs=_gdn_in_specs(tm, S) + [full((CONV_K, G_CONV)), full((1, 16)), full((1, 16))],
        out_specs=[t1, t1, t2, t2, t2, t2, t2, t1, pl.BlockSpec((tm // 8, 2048), lambda i: (i, 0))],
        out_shape=[sd((S, 1024), bf16)] * 2 + [sd((S, 2048), bf16)] * 5 + [sd((S, 1024), f32), sd((S // 8, 2048), f32)],
        compiler_params=_params(("parallel",)),
    )(proj, proj, proj, conv_w, alog, dtb)


def _bnn(a, b):
    return lax.dot_general(a, b, (((2,), (1,)), ((0,), (0,))), preferred_element_type=f32)


def _bnt(a, b):
    return lax.dot_general(a, b, (((2,), (2,)), ((0,), (0,))), preferred_element_type=f32)


def _btn(a, b):
    return lax.dot_general(a, b, (((1,), (1,)), ((0,), (0,))), preferred_element_type=f32)


def _split(a):
    hi = a.astype(bf16)
    return hi, (a - hi.astype(f32)).astype(bf16)


def _cat3(h, l, axis, lhs):
    return jnp.concatenate([h, h, l] if lhs else [h, l, h], axis=axis)


def _tri_inv_b(L):
    eye = jnp.where(_iota((1, CHUNK, CHUNK), 1) == _iota((1, CHUNK, CHUNK), 2), 1.0, 0.0).astype(f32)
    P = -L
    T = eye + P
    ph, pl_ = _split(P)
    for _ in range(5):
        P = _bnn(_cat3(ph, pl_, 2, True), _cat3(ph, pl_, 1, False))
        ph, pl_ = _split(P)
        th, tl = _split(T)
        T = T + _bnn(_cat3(th, tl, 2, True), _cat3(ph, pl_, 1, False))
    return T


GTB = 512
GQH = 2
GNV = 2 * GQH


def _gdn_slices(ncb):
    pairs = [(c, e) for c in range(ncb) for e in range(GNV)]
    rs = lambda c: slice(c * CHUNK, (c + 1) * CHUNK)
    cs = lambda e: slice(e * 128, (e + 1) * 128)
    ds_ = lambda e: slice(e * CHUNK, (e + 1) * CHUNK)
    ks = lambda e: slice((e // 2) * 128, (e // 2 + 1) * 128)
    return pairs, rs, cs, ds_, ks


def gdn_fwd(q, k, kb, kbg, vb, qd, kd, dm, gl8, comm=None):
    S = q.shape[0]
    nb, ncb = S // GTB, GTB // CHUNK
    pairs, rs, cs, ds_, ks = _gdn_slices(ncb)

    def body(q_ref, k_ref, kb_ref, kbg_ref, vb_ref, qd_ref, kd_ref, d_ref, gl_ref,
             o_ref, w_ref, at_ref, t_ref, vn_ref, st_ref, state, u_scr):
        @pl.when(pl.program_id(1) == 0)
        def _():
            state[...] = jnp.zeros_like(state)

        stk = lambda ref, lanes: jnp.stack([ref[rs(c), lanes(e)] for c, e in pairs])
        kq = stk(k_ref, ks)
        dmat = stk(d_ref, ds_)
        strict = _iota((1, CHUNK, CHUNK), 1) > _iota((1, CHUNK, CHUNK), 2)
        T = _tri_inv_b(jnp.where(strict, _bnt(stk(kb_ref, cs), kq) * dmat, 0.0))
        tb = T.astype(bf16)
        u_scr[...] = _bnn(tb, stk(vb_ref, cs))
        wb = _bnn(tb, stk(kbg_ref, cs)).astype(bf16)
        per_qk = lambda ref: jnp.stack([ref[rs(c), ks(e)] for c, e in pairs if e % 2 == 0])
        qk = _bnt(per_qk(q_ref), per_qk(k_ref))
        for b, (c, e) in enumerate(pairs):
            w_ref[rs(c), cs(e)] = wb[b]
            at_ref[rs(c), ds_(e)] = (qk[b // 2] * dmat[b]).astype(bf16)
            t_ref[rs(c), ds_(e)] = T[b]
        for b, (c, e) in enumerate(pairs):
            sb = state[e].astype(bf16)
            vnb = (u_scr[b] - _nn(w_ref[rs(c), cs(e)], sb)).astype(bf16)
            o_ref[rs(c), cs(e)] = _nn(qd_ref[rs(c), cs(e)], sb) + _nn(at_ref[rs(c), ds_(e)], vnb)
            st_ref[c * 128:(c + 1) * 128, cs(e)] = sb
            state[e] = state[e] * gl_ref[c * 8:c * 8 + 1, cs(e)] + _tn(kd_ref[rs(c), cs(e)], vnb)
            vn_ref[rs(c), cs(e)] = vnb

    b1 = pl.BlockSpec((GTB, 128 * GQH), lambda j, i: (i, j))
    b2 = pl.BlockSpec((GTB, 256 * GQH), lambda j, i: (i, j))
    sd = jax.ShapeDtypeStruct
    return _call(
        body, name="gdn_fwd", grid=(GQK_H // GQH, nb),
        in_specs=[b1, b1, b2, b2, b2, b2, b2, b1, pl.BlockSpec((GTB // 8, 256 * GQH), lambda j, i: (i, j))],
        out_specs=[b2, b2, b1, b1, b2, pl.BlockSpec((ncb * 128, 256 * GQH), lambda j, i: (i, j))],
        out_shape=[sd((S, 2048), f32), sd((S, 2048), bf16), sd((S, 1024), bf16), sd((S, 1024), f32),
                   sd((S, 2048), bf16), sd((S // CHUNK * 128, 2048), bf16)],
        scratch_shapes=[pltpu.VMEM((GNV, 128, 128), f32), pltpu.VMEM((GNV * ncb, CHUNK, 128), f32)],
        sem=("parallel", "arbitrary"), args=(q, k, kb, kbg, vb, qd, kd, dm, gl8), comm=comm)


def gdn_bwd(do, q, k, kb, kbg, vb, qd, kd, dm, gl8, w, at, T, vn, st, comm=None):
    S = q.shape[0]
    nb, ncb = S // GTB, GTB // CHUNK
    pairs, rs, cs, ds_, ks = _gdn_slices(ncb)

    def body(do_ref, q_ref, k_ref, kb_ref, kbg_ref, vb_ref, qd_ref, kd_ref, d_ref, gl_ref, w_ref, at_ref, t_ref, vn_ref, st_ref,
             dq_ref, dk_ref, dkb_ref, dkbg_ref, dvb_ref, dqd_ref, dkd_ref, dgc_ref, dstate, dvn_scr, dw_scr, dat_scr, dgl_scr):
        @pl.when(pl.program_id(1) == 0)
        def _():
            dstate[...] = jnp.zeros_like(dstate)

        for b, (c, e) in reversed(list(enumerate(pairs))):
            dob = do_ref[rs(c), cs(e)].astype(bf16)
            sb = st_ref[c * 128:(c + 1) * 128, cs(e)]
            vnb = vn_ref[rs(c), cs(e)]
            gl = gl_ref[c * 8:c * 8 + 1, cs(e)]
            dS = dstate[e]
            dsb = dS.astype(bf16)
            dvnb = (_tn(at_ref[rs(c), ds_(e)], dob) + _nn(kd_ref[rs(c), cs(e)], dsb)).astype(bf16)
            dvn_scr[b] = dvnb
            dat_scr[b] = _nt(dob, vnb)
            dqd_ref[rs(c), cs(e)] = _nt(dob, sb)
            dkd_ref[rs(c), cs(e)] = _nt(vnb, dsb)
            dw_scr[b] = (-_nt(dvnb, sb)).astype(bf16)
            dgl = jnp.sum(jnp.sum(dS * sb.astype(f32), axis=1, keepdims=True), axis=0, keepdims=True)
            dgl_scr[b] = jnp.broadcast_to(dgl * gl, (8, 128))
            dstate[e] = gl * dS + _tn(qd_ref[rs(c), cs(e)], dob) - _tn(w_ref[rs(c), cs(e)], dvnb)

        stk = lambda ref, lanes: jnp.stack([ref[rs(c), lanes(e)] for c, e in pairs])
        kq, qq = stk(k_ref, ks), stk(q_ref, ks)
        kbb = stk(kb_ref, cs)
        Tm = stk(t_ref, ds_)
        tb = Tm.astype(bf16)
        dvn, dw = dvn_scr[...], dw_scr[...]
        dT = _bnt(dvn, stk(vb_ref, cs)) + _bnt(dw, stk(kbg_ref, cs))
        dvb, dkbg = _btn(tb, dvn), _btn(tb, dw)
        th, tl = _split(Tm)
        xh, xl = _split(_bnt(_cat3(*_split(dT), 2, True), _cat3(th, tl, 2, False)))
        dL = -_btn(_cat3(th, tl, 1, True), _cat3(xh, xl, 1, False))
        dmat = stk(d_ref, ds_)
        strict = _iota((1, CHUNK, CHUNK), 1) > _iota((1, CHUNK, CHUNK), 2)
        dA = jnp.where(strict, dL * dmat, 0.0)
        dB = dat_scr[...] * dmat
        dAb, dBb = dA.astype(bf16), dB.astype(bf16)
        dkb = _bnn(dAb, kq)
        dkc = _btn(dAb, kbb) + _btn(dBb, qq)
        dqc = _bnn(dBb, kq)
        M = dA * _bnt(kbb, kq) + dB * _bnt(qq, kq)
        mh, ml = _split(M)
        colsum = _btn(jnp.concatenate([mh, ml], axis=1), jnp.ones((GNV * ncb, 2 * CHUNK, 128), bf16))
        lastrow = _iota((1, CHUNK, 128), 1) == CHUNK - 1
        for b, (c, e) in enumerate(pairs):
            dvb_ref[rs(c), cs(e)] = dvb[b]
            dkbg_ref[rs(c), cs(e)] = dkbg[b]
            dkb_ref[rs(c), cs(e)] = dkb[b]
            dgc_ref[rs(c), cs(e)] = (jnp.sum(M[b], axis=1, keepdims=True) - colsum[b]
                                     + jnp.where(lastrow[0], dgl_scr[b][0:1, :], 0.0))
        for b, (c, e) in enumerate(pairs):
            if e % 2 == 0:
                dq_ref[rs(c), ks(e)] = dqc[b] + dqc[b + 1]
                dk_ref[rs(c), ks(e)] = dkc[b] + dkc[b + 1]

    b1 = pl.BlockSpec((GTB, 128 * GQH), lambda j, i: (nb - 1 - i, j))
    b2 = pl.BlockSpec((GTB, 256 * GQH), lambda j, i: (nb - 1 - i, j))
    sd = jax.ShapeDtypeStruct
    return _call(
        body, name="gdn_bwd", grid=(GQK_H // GQH, nb),
        in_specs=[b2, b1, b1, b2, b2, b2, b2, b2, b1, pl.BlockSpec((GTB // 8, 256 * GQH), lambda j, i: (nb - 1 - i, j)),
                  b2, b1, b1, b2, pl.BlockSpec((ncb * 128, 256 * GQH), lambda j, i: (nb - 1 - i, j))],
        out_specs=[b1, b1, b2, b2, b2, b2, b2, b2],
        out_shape=[sd((S, 1024), f32)] * 2 + [sd((S, 2048), f32)] * 6,
        scratch_shapes=[pltpu.VMEM((GNV, 128, 128), f32), pltpu.VMEM((GNV * ncb, CHUNK, 128), bf16),
                        pltpu.VMEM((GNV * ncb, CHUNK, 128), bf16), pltpu.VMEM((GNV * ncb, CHUNK, CHUNK), f32),
                        pltpu.VMEM((GNV * ncb, 8, 128), f32)],
        sem=("parallel", "arbitrary"), args=(do, q, k, kb, kbg, vb, qd, kd, dm, gl8, w, at, T, vn, st), comm=comm)


def gdn_onorm(o, proj, nw):
    S = o.shape[0]
    tm = 256

    def body(o_ref, z_ref, nw_ref, o2_ref):
        for h in range(GV_H):
            hs = slice(h * 128, (h + 1) * 128)
            oh = o_ref[:, hs]
            r = lax.rsqrt(jnp.mean(oh * oh, axis=-1, keepdims=True) + EPS)
            o2_ref[:, hs] = (((oh * r) * nw_ref[...]) * _silu(z_ref[:, hs])).astype(bf16)

    t2 = pl.BlockSpec((tm, 2048), lambda i: (i, 0))
    return pl.pallas_call(
        body, name="gdn_onorm", grid=(S // tm,),
        in_specs=[t2, pl.BlockSpec((tm, 2048), lambda i: (i, G_Z0 // 2048)), pl.BlockSpec((1, 128), lambda i: (0, 0))],
        out_specs=t2, out_shape=jax.ShapeDtypeStruct((S, 2048), bf16),
        compiler_params=_params(("parallel",)),
    )(o, proj, nw)


def gdn_onorm_bwd(do2, o, proj, nw):
    S = o.shape[0]
    tm = 256

    def body(d_ref, o_ref, z_ref, nw_ref, do_ref, dz_ref, st_ref):
        i = pl.program_id(0)
        acc = jnp.zeros((1, 128), f32)
        for h in range(GV_H):
            hs = slice(h * 128, (h + 1) * 128)
            oh, z, d2 = o_ref[:, hs], z_ref[:, hs], d_ref[:, hs]
            r = lax.rsqrt(jnp.mean(oh * oh, axis=-1, keepdims=True) + EPS)
            on = oh * r
            dt = d2 * _silu(z)
            dz_ref[:, hs] = (d2 * (on * nw_ref[...]) * _dsilu(z)).astype(bf16)
            don = dt * nw_ref[...]
            acc = acc + jnp.sum(dt * on, axis=0, keepdims=True)
            do_ref[:, hs] = r * (don - on * jnp.mean(don * on, axis=-1, keepdims=True))
        upd = jnp.concatenate([acc, jnp.zeros((7, 128), f32)], axis=0)

        @pl.when(i == 0)
        def _():
            st_ref[...] = upd

        @pl.when(i > 0)
        def _():
            st_ref[...] += upd

    t2 = pl.BlockSpec((tm, 2048), lambda i: (i, 0))
    sd = jax.ShapeDtypeStruct
    return pl.pallas_call(
        body, name="gdn_onorm_bwd", grid=(S // tm,),
        in_specs=[t2, t2, pl.BlockSpec((tm, 2048), lambda i: (i, G_Z0 // 2048)), pl.BlockSpec((1, 128), lambda i: (0, 0))],
        out_specs=[t2, t2, pl.BlockSpec((8, 128), lambda i: (0, 0))],
        out_shape=[sd((S, 2048), f32), sd((S, 2048), bf16), sd((8, 128), f32)],
        compiler_params=_params(("arbitrary",)),
    )(do2, o, proj, nw)


def gdn_pre_bwd(proj, conv_w, alog, dtb, dq, dk, dkb, dkbg, dvb, dqd, dkd, dgcd):
    S = proj.shape[0]
    tm = 128

    def body(p_ref, halo_ref, ba_ref, w_ref, al_ref, dt_ref, dq_ref, dk_ref, dkb_ref, dkbg_ref, dvb_ref, dqd_ref, dkd_ref, dgc_ref,
             dcv_ref, dba_ref, st_ref):
        i = pl.program_id(0)
        first = i == 0
        ltri, utri, bsame = _chunk_mats(tm)
        beta, u, neg_a, g, gc, glast = _gdn_scalars(ba_ref[...], al_ref[...], dt_ref[...], ltri, bsame)
        eg, ek = jnp.exp(gc), jnp.exp(glast - gc)
        lane16 = _iota((tm, 16), 1)
        dgc_all = jnp.zeros((tm, 16), f32)
        rkd_all = jnp.zeros((tm, 16), f32)
        dbeta_all = jnp.zeros((tm, 16), f32)

        def pre(gi):
            return _conv(_conv_taps(p_ref, halo_ref, first, gi), w_ref[:, gi * 128:(gi + 1) * 128])

        def l2n_bwd(xt, dy):
            r = lax.rsqrt(jnp.sum(xt * xt, axis=-1, keepdims=True) + EPS)
            y = xt * r
            return r * (dy - y * jnp.sum(dy * y, axis=-1, keepdims=True))

        for j in range(GQK_H):
            js = slice(j * 128, (j + 1) * 128)
            cvq, cvk = pre(j), pre(GQK_H + j)
            qt, kt = _silu(cvq), _silu(cvk)
            qn = _l2n(qt) * (GHD ** -0.5)
            kn = _l2n(kt)
            dq_tot, dk_tot = dq_ref[:, js], dk_ref[:, js]
            for e in range(2):
                h = 2 * j + e
                hs = slice(h * 128, (h + 1) * 128)
                gv = 2 * GQK_H + h
                cvv = pre(gv)
                v = _silu(cvv)
                bh, egh, ekh = beta[:, h:h + 1], eg[:, h:h + 1], ek[:, h:h + 1]
                dkbg, dkd, dqd, dvb = dkbg_ref[:, hs], dkd_ref[:, hs], dqd_ref[:, hs], dvb_ref[:, hs]
                dkb_t = dkb_ref[:, hs] + dkbg * egh
                dk_tot = dk_tot + dkb_t * bh + dkd * ekh
                dq_tot = dq_tot + dqd * egh
                dcv_ref[:, gv * 128:(gv + 1) * 128] = (dvb * bh) * _dsilu(cvv)
                dbeta = jnp.sum(dkb_t * kn, axis=-1, keepdims=True) + jnp.sum(dvb * v, axis=-1, keepdims=True)
                rkd = jnp.sum(dkd * (kn * ekh), axis=-1, keepdims=True)
                dgc = (dgc_ref[:, hs][:, 0:1] + jnp.sum(dkbg * (kn * bh * egh), axis=-1, keepdims=True)
                       + jnp.sum(dqd * (qn * egh), axis=-1, keepdims=True) - rkd)
                sel = lane16 == h
                dgc_all = dgc_all + jnp.where(sel, dgc, 0.0)
                rkd_all = rkd_all + jnp.where(sel, rkd, 0.0)
                dbeta_all = dbeta_all + jnp.where(sel, dbeta, 0.0)
            dcv_ref[:, js] = l2n_bwd(qt, dq_tot * (GHD ** -0.5)) * _dsilu(cvq)
            ks = slice((GQK_H + j) * 128, (GQK_H + j + 1) * 128)
            dcv_ref[:, ks] = l2n_bwd(kt, dk_tot) * _dsilu(cvk)

        islast = jnp.bitwise_and(_iota((tm, 16), 0), CHUNK - 1) == CHUNK - 1
        dgc_all = dgc_all + jnp.where(islast, _nn(bsame, rkd_all, HI), 0.0)
        dg = _nn(utri, dgc_all, HI)
        da = dg * neg_a * _sigmoid(u)
        db = dbeta_all * beta * (1.0 - beta)
        r16, c128 = _iota((16, 128), 0), _iota((16, 128), 1)
        pb = jnp.where(c128 == r16, 1.0, 0.0).astype(f32)
        pa = jnp.where(c128 == r16 + 16, 1.0, 0.0).astype(f32)
        dba_ref[...] = _nn(db, pb, HI) + _nn(da, pa, HI)
        upd = jnp.concatenate([jnp.sum(dg * g, axis=0, keepdims=True), jnp.sum(da, axis=0, keepdims=True),
                               jnp.zeros((6, 16), f32)], axis=0)

        @pl.when(i == 0)
        def _():
            st_ref[...] = upd

        @pl.when(i > 0)
        def _():
            st_ref[...] += upd

    full = lambda shape: pl.BlockSpec(shape, lambda i: (0, 0))
    t1 = pl.BlockSpec((tm, 1024), lambda i: (i, 0))
    t2 = pl.BlockSpec((tm, 2048), lambda i: (i, 0))
    sd = jax.ShapeDtypeStruct
    return pl.pallas_call(
        body, name="gdn_pre_bwd", grid=(S // tm,),
        in_specs=_gdn_in_specs(tm, S) + [full((CONV_K, G_CONV)), full((1, 16)), full((1, 16)), t1, t1] + [t2] * 6,
        out_specs=[pl.BlockSpec((tm, G_CONV), lambda i: (i, 0)), pl.BlockSpec((tm, 128), lambda i: (i, 0)), full((8, 16))],
        out_shape=[sd((S, G_CONV), f32), sd((S, 128), f32), sd((8, 16), f32)],
        compiler_params=_params(("arbitrary",)),
    )(proj, proj, proj, conv_w, alog, dtb, dq, dk, dkb, dkbg, dvb, dqd, dkd, dgcd)


def gdn_conv_bwd(proj, conv_w, dcv, dz, dba):
    S = proj.shape[0]
    tm = 256
    nb, nb8 = S // tm, tm // 8

    def body(p_ref, halo_ref, w_ref, dcv_ref, nxt_ref, dz_ref, dba_ref, dp_ref, dw_ref):
        i = pl.program_id(0)
        first, last = i == 0, i == nb - 1
        for gi in range(G_CONV // 128):
            cs = slice(gi * 128, (gi + 1) * 128)
            taps = _conv_taps(p_ref, halo_ref, first, gi)
            cur = dcv_ref[:, cs]
            ext = jnp.concatenate([cur, jnp.where(last, 0.0, nxt_ref[:, cs])], axis=0)
            w = w_ref[:, cs]
            dp = cur * w[3:4]
            rows = [jnp.sum(cur * taps[3 - kk], axis=0, keepdims=True) for kk in range(CONV_K)]
            for s in range(1, CONV_K):
                dp = dp + pltpu.roll(ext, tm + 8 - s, 0)[:tm] * w[3 - s:4 - s]
            dp_ref[:, cs] = dp.astype(bf16)
            upd = jnp.concatenate(rows + [jnp.zeros((4, 128), f32)], axis=0)

            @pl.when(first)
            def _():
                dw_ref[:, cs] = upd

            @pl.when(i > 0)
            def _():
                dw_ref[:, cs] += upd

        dp_ref[:, G_Z0:G_BA0] = dz_ref[...]
        dp_ref[:, G_BA0:G_INP] = dba_ref[...].astype(bf16)

    sd = jax.ShapeDtypeStruct
    return pl.pallas_call(
        body, name="gdn_conv_bwd", grid=(nb,),
        in_specs=[pl.BlockSpec((tm, G_CONV), lambda i: (i, 0)),
                  pl.BlockSpec((8, G_CONV), lambda i: (jnp.maximum(i * nb8 - 1, 0), 0)),
                  pl.BlockSpec((CONV_K, G_CONV), lambda i: (0, 0)),
                  pl.BlockSpec((tm, G_CONV), lambda i: (i, 0)),
                  pl.BlockSpec((8, G_CONV), lambda i: (jnp.minimum((i + 1) * nb8, S // 8 - 1), 0)),
                  pl.BlockSpec((tm, 2048), lambda i: (i, 0)), pl.BlockSpec((tm, 128), lambda i: (i, 0))],
        out_specs=[pl.BlockSpec((tm, G_INP), lambda i: (i, 0)), pl.BlockSpec((8, G_CONV), lambda i: (0, 0))],
        out_shape=[sd((S, G_INP), bf16), sd((8, G_CONV), f32)],
        compiler_params=_params(("arbitrary",)),
    )(proj, proj, conv_w, dcv, dcv, dz, dba)


def _half_mean(t, lo_half):
    m0 = jnp.sum(jnp.where(lo_half, t, 0.0), axis=-1, keepdims=True)
    m1 = jnp.sum(jnp.where(lo_half, 0.0, t), axis=-1, keepdims=True)
    return jnp.where(lo_half, m0, m1) * (1.0 / F_HD)


def _split3(c):
    hi = c.astype(bf16).astype(f32)
    mid = (c - hi).astype(bf16).astype(f32)
    lo = (c - hi - mid).astype(bf16).astype(f32)
    return hi, mid, lo


def fox_pre(proj, fbias, qw2, kw2):
    S = proj.shape[0]
    tm = 256

    def body(q_ref, k_ref, v_ref, f_ref, fb_ref, qw_ref, kw_ref, qa_ref, ka_ref, vb_ref, carry):
        @pl.when(pl.program_id(0) == 0)
        def _():
            carry[...] = jnp.zeros_like(carry)

        logf = -_softplus(-(f_ref[:, 0:16] + fb_ref[...]))
        ltri = jnp.where(_iota((tm, tm), 1) <= _iota((tm, tm), 0), 1.0, 0.0).astype(f32)
        cum = _nn(ltri, logf, HI) + carry[0:1, :]
        carry[0:1, :] = cum[tm - 1:tm, :]
        lane = _iota((tm, 128), 1)
        lo_half = lane < F_HD
        for p in range(F_H // 2):
            ps = slice(p * 128, (p + 1) * 128)
            for src, w_ref, dst, is_q in ((q_ref, qw_ref, qa_ref, True), (k_ref, kw_ref, ka_ref, False)):
                x = src[:, ps]
                xn = x * lax.rsqrt(_half_mean(x * x, lo_half) + EPS) * w_ref[...]
                if is_q:
                    xn = xn * (F_HD ** -0.5)
                for e in range(2):
                    h = 2 * p + e
                    base = xn if e == 0 else pltpu.roll(xn, F_HD, 1)
                    hi, mid, lo = _split3(cum[:, h:h + 1])
                    pieces = jnp.where(lane == 64, hi, 0.0) + jnp.where(lane == 65, mid, 0.0) + jnp.where(lane == 66, lo, 0.0)
                    if is_q:
                        ext = pieces + jnp.where((lane >= 67) & (lane <= 69), 1.0, 0.0)
                    else:
                        ext = jnp.where((lane >= 64) & (lane <= 66), 1.0, 0.0) - pltpu.roll(pieces, 3, 1)
                    dst[:, h * 128:(h + 1) * 128] = jnp.where(lo_half, base, ext).astype(bf16)
        one = jnp.where(lane == F_HD, 1.0, 0.0)
        for p in range(F_H // 2):
            vv = v_ref[:, p * 128:(p + 1) * 128]
            vb_ref[:, (2 * p) * 128:(2 * p + 1) * 128] = jnp.where(lo_half, vv, one).astype(bf16)
            vb_ref[:, (2 * p + 1) * 128:(2 * p + 2) * 128] = jnp.where(lo_half, pltpu.roll(vv, F_HD, 1), one).astype(bf16)

    t1 = lambda c: pl.BlockSpec((tm, 1024), lambda i: (i, c))
    vec = lambda n: pl.BlockSpec((1, n), lambda i: (0, 0))
    sd = jax.ShapeDtypeStruct
    return pl.pallas_call(
        body, name="fox_pre", grid=(S // tm,),
        in_specs=[t1(0), t1(1), t1(2), pl.BlockSpec((tm, 128), lambda i: (i, F_F0 // 128)), vec(16), vec(128), vec(128)],
        out_specs=[pl.BlockSpec((tm, 2048), lambda i: (i, 0))] * 3,
        out_shape=[sd((S, 2048), bf16)] * 3,
        scratch_shapes=[pltpu.VMEM((8, 16), f32)],
        compiler_params=_params(("arbitrary",)),
    )(proj, proj, proj, proj, fbias, qw2, kw2)


FTQ = 512
FHS = 4


def fox_attn(qa, ka, v, comm=None):
    S = qa.shape[0]
    nq = S // FTQ

    live = [(i, j) for i in range(nq) for j in range(i + 1)]
    qi_tab = jnp.asarray([i for i, _ in live], jnp.int32)
    kj_tab = jnp.asarray([j for _, j in live], jnp.int32)

    def body(qi_ref, kj_ref, q_ref, k_ref, v_ref, o_ref, lse_ref, m_scr, acc_scr):
        t = pl.program_id(1)
        i, j = qi_ref[t], kj_ref[t]

        @pl.when(j == 0)
        def _():
            m_scr[...] = jnp.full_like(m_scr, NEG)
            acc_scr[...] = jnp.zeros_like(acc_scr)

        def step(diagonal):
            for e in range(FHS):
                es = slice(e * 128, (e + 1) * 128)
                s = _nt(q_ref[:, es], k_ref[:, es])
                if diagonal:
                    s = jnp.where(_iota((FTQ, FTQ), 0) >= _iota((FTQ, FTQ), 1), s, NEG)
                m_old = m_scr[e]
                m_new = jnp.maximum(m_old, jnp.max(s, axis=-1, keepdims=True))
                p = jnp.exp(s - m_new[:, 0:1])
                acc_scr[e] = acc_scr[e] * jnp.exp(m_old - m_new) + _nn(p.astype(bf16), v_ref[:, es])
                m_scr[e] = m_new

        pl.when(j < i)(functools.partial(step, False))

        @pl.when(j == i)
        def _():
            step(True)
            for e in range(FHS):
                vs = slice(e * F_HD, (e + 1) * F_HD)
                acc = acc_scr[e]
                l = acc[:, F_HD:F_HD + 1]
                o_ref[:, vs] = acc[:, 0:F_HD] / l
                lse_ref[:, vs] = m_scr[e][:, 0:F_HD] + jnp.log(l)

    sd = jax.ShapeDtypeStruct
    qo = pl.BlockSpec((FTQ, F_HD * FHS), lambda p, t, qi, kj: (qi[t], p))
    kv = pl.BlockSpec((FTQ, 128 * FHS), lambda p, t, qi, kj: (kj[t], p))
    return _call(
        body, name="fox_attn", grid=(F_H // FHS, len(live)),
        in_specs=[pl.BlockSpec((FTQ, 128 * FHS), lambda p, t, qi, kj: (qi[t], p)), kv, kv],
        out_specs=[qo, qo],
        out_shape=[sd((S, 1024), f32), sd((S, 1024), f32)],
        scratch_shapes=[pltpu.VMEM((FHS, FTQ, 128), f32), pltpu.VMEM((FHS, FTQ, 128), f32)],
        sem=("parallel", "arbitrary"), args=(qa, ka, v), comm=comm, prefetch=(qi_tab, kj_tab))


def fox_attn_bwd(qa, ka, v, do, lse, delta, comm=None):
    S = qa.shape[0]
    nq = S // FTQ

    live = [(j, i) for j in range(nq) for i in range(j, nq)]
    kj_tab = jnp.asarray([j for j, _ in live], jnp.int32)
    qi_tab = jnp.asarray([i for _, i in live], jnp.int32)

    def body(kj_ref, qi_ref, q_ref, k_ref, v_ref, do_ref, lse_ref, dl_ref, dq_ref, dk_ref, dv_ref, dk_scr, dv_scr):
        t = pl.program_id(1)
        j, i = kj_ref[t], qi_ref[t]

        @pl.when(t == 0)
        def _():
            dq_ref[...] = jnp.zeros_like(dq_ref)

        @pl.when(i == j)
        def _():
            dk_scr[...] = jnp.zeros_like(dk_scr)
            dv_scr[...] = jnp.zeros_like(dv_scr)

        def step(diagonal):
            rows = pl.ds(pl.multiple_of(i * FTQ, FTQ), FTQ)
            for e in range(2):
                es, vs = slice(e * 128, (e + 1) * 128), slice(e * F_HD, (e + 1) * F_HD)
                qe, ke = q_ref[:, es], k_ref[:, es]
                dob = do_ref[:, vs].astype(bf16)
                s = _nt(qe, ke)
                if diagonal:
                    s = jnp.where(_iota((FTQ, FTQ), 0) >= _iota((FTQ, FTQ), 1), s, NEG)
                p = jnp.exp(s - lse_ref[:, e * F_HD:e * F_HD + 1])
                ds = p * (_nt(dob, v_ref[:, e * 128:e * 128 + F_HD]) - dl_ref[:, e * F_HD:e * F_HD + 1])
                dsb = ds.astype(bf16)
                dv_scr[e] += _tn(p.astype(bf16), dob)
                dk_scr[e] += _tn(dsb, qe)
                dq_ref[rows, es] += _nn(dsb, ke)

        pl.when(i > j)(functools.partial(step, False))
        pl.when(i == j)(functools.partial(step, True))

        @pl.when(i == nq - 1)
        def _():
            for e in range(2):
                dk_ref[:, e * 128:(e + 1) * 128] = dk_scr[e]
                dv_ref[:, e * F_HD:(e + 1) * F_HD] = dv_scr[e]

    sd = jax.ShapeDtypeStruct
    qi = lambda w: pl.BlockSpec((FTQ, w), lambda p, t, kj_, qi_: (qi_[t], p))
    kj = lambda w: pl.BlockSpec((FTQ, w), lambda p, t, kj_, qi_: (kj_[t], p))
    return _call(
        body, name="fox_attn_bwd", grid=(F_H // 2, len(live)),
        in_specs=[qi(256), kj(256), kj(256), qi(128), qi(128), qi(128)],
        out_specs=[pl.BlockSpec((S, 256), lambda p, t, kj_, qi_: (0, p)), kj(256), kj(128)],
        out_shape=[sd((S, 2048), f32), sd((S, 2048), f32), sd((S, 1024), f32)],
        scratch_shapes=[pltpu.VMEM((2, FTQ, 128), f32), pltpu.VMEM((2, FTQ, F_HD), f32)],
        sem=("parallel", "arbitrary"), args=(qa, ka, v, do, lse, delta), comm=comm, prefetch=(kj_tab, qi_tab))


def fox_gate(o, proj):
    S = o.shape[0]
    tm = 512

    def body(o_ref, z_ref, o2_ref):
        o2_ref[...] = (o_ref[...] * _silu(z_ref[...])).astype(bf16)

    t = pl.BlockSpec((tm, 1024), lambda i: (i, 0))
    return pl.pallas_call(
        body, name="fox_gate", grid=(S // tm,),
        in_specs=[t, pl.BlockSpec((tm, 1024), lambda i: (i, 3))], out_specs=t,
        out_shape=jax.ShapeDtypeStruct((S, 1024), bf16),
        compiler_params=_params(("parallel",)),
    )(o, proj)


def fox_gate_bwd(do2, o, proj):
    S = o.shape[0]
    tm = 256

    def body(d_ref, o_ref, z_ref, do_ref, dz_ref, dl_ref):
        lo_half = _iota((tm, 128), 1) < F_HD
        for p in range(F_H // 2):
            ps = slice(p * 128, (p + 1) * 128)
            d2, ov, z = d_ref[:, ps], o_ref[:, ps], z_ref[:, ps]
            dov = d2 * _silu(z)
            do_ref[:, ps] = dov
            dz_ref[:, ps] = (d2 * ov * _dsilu(z)).astype(bf16)
            dl_ref[:, ps] = _half_mean(dov * ov, lo_half) * float(F_HD)

    t = pl.BlockSpec((tm, 1024), lambda i: (i, 0))
    sd = jax.ShapeDtypeStruct
    return pl.pallas_call(
        body, name="fox_gate_bwd", grid=(S // tm,),
        in_specs=[t, t, pl.BlockSpec((tm, 1024), lambda i: (i, 3))], out_specs=[t, t, t],
        out_shape=[sd((S, 1024), f32), sd((S, 1024), bf16), sd((S, 1024), f32)],
        compiler_params=_params(("parallel",)),
    )(do2, o, proj)


def fox_pre_bwd(proj, fbias, qw2, kw2, dqa, dka, dv, dz):
    S = proj.shape[0]
    tm = 256
    nb = S // tm

    def body(q_ref, k_ref, f_ref, fb_ref, qw_ref, kw_ref, dqa_ref, dka_ref, dv_ref, dz_ref, dp_ref, st_ref, carry):
        i = pl.program_id(0)

        @pl.when(i == 0)
        def _():
            carry[...] = jnp.zeros_like(carry)

        lane = _iota((tm, 128), 1)
        lo_half = lane < F_HD
        lane16 = _iota((tm, 16), 1)
        dcum = jnp.zeros((tm, 16), f32)
        dws = []
        for src, w_ref, dsrc, is_q, col0 in ((q_ref, qw_ref, dqa_ref, True, 0), (k_ref, kw_ref, dka_ref, False, 1024)):
            dw = jnp.zeros((1, 128), f32)
            for p in range(F_H // 2):
                ps = slice(p * 128, (p + 1) * 128)
                x = src[:, ps]
                r = lax.rsqrt(_half_mean(x * x, lo_half) + EPS)
                xh = x * r
                d0 = dsrc[:, (2 * p) * 128:(2 * p + 1) * 128]
                d1 = dsrc[:, (2 * p + 1) * 128:(2 * p + 2) * 128]
                dy = jnp.where(lo_half, d0, pltpu.roll(d1, F_HD, 1))
                if is_q:
                    dy = dy * (F_HD ** -0.5)
                dxh = dy * w_ref[...]
                dw = dw + jnp.sum(dy * xh, axis=0, keepdims=True)
                dp_ref[:, col0 + p * 128:col0 + (p + 1) * 128] = (r * (dxh - xh * _half_mean(dxh * xh, lo_half))).astype(bf16)
                for e, de in ((0, d0), (1, d1)):
                    col = de[:, 64:65] if is_q else -de[:, 67:68]
                    dcum = dcum + jnp.where(lane16 == 2 * p + e, col, 0.0)
            dws.append(dw)
        dp_ref[:, 2048:3072] = dv_ref[...].astype(bf16)
        dp_ref[:, 3072:4096] = dz_ref[...]
        utri = jnp.where(_iota((tm, tm), 1) >= _iota((tm, tm), 0), 1.0, 0.0).astype(f32)
        dlogf = _nn(utri, dcum, HI) + carry[0:1, :]
        carry[0:1, :] = dlogf[0:1, :]
        fl = f_ref[:, 0:16] + fb_ref[...]
        df = dlogf * _sigmoid(-fl)
        place = jnp.where(_iota((16, 128), 1) == _iota((16, 128), 0), 1.0, 0.0).astype(f32)
        dfw = _nn(df, place, HI)
        dp_ref[:, F_F0:F_INP] = dfw.astype(bf16)
        upd = jnp.concatenate(dws + [jnp.sum(dfw, axis=0, keepdims=True), jnp.zeros((5, 128), f32)], axis=0)

        @pl.when(i == 0)
        def _():
            st_ref[...] = upd

        @pl.when(i > 0)
        def _():
            st_ref[...] += upd

    rev = lambda w, c: pl.BlockSpec((tm, w), lambda i: (nb - 1 - i, c))
    vec = lambda n: pl.BlockSpec((1, n), lambda i: (0, 0))
    sd = jax.ShapeDtypeStruct
    return pl.pallas_call(
        body, name="fox_pre_bwd", grid=(nb,),
        in_specs=[rev(1024, 0), rev(1024, 1), rev(128, F_F0 // 128), vec(16), vec(128), vec(128),
                  rev(2048, 0), rev(2048, 0), rev(1024, 0), rev(1024, 0)],
        out_specs=[rev(F_INP, 0), pl.BlockSpec((8, 128), lambda i: (0, 0))],
        out_shape=[sd((S, F_INP), bf16), sd((8, 128), f32)],
        scratch_shapes=[pltpu.VMEM((8, 16), f32)],
        compiler_params=_params(("arbitrary",)),
    )(proj, proj, proj, fbias, qw2, kw2, dqa, dka, dv, dz)


def _me():
    return lax.axis_index("x"), lax.axis_index("y"), lax.axis_index("c")


def _other_chips(x, y):
    return [(1 - x, y), (x, 1 - y), (1 - x, 1 - y)]


def ag_small(xs):
    m_per, n = xs.shape

    def body(x_ref, out_ref, send_sems, recv_sems, local_sem):
        x, y, c = _me()
        me, sibling = (x, y, c), (x, y, 1 - c)
        chips = _other_chips(x, y)

        def rows(px, py, pc):
            return out_ref.at[pl.ds((4 * px + 2 * py + pc) * m_per, m_per), :]

        def copy(k, block, to, src=None):
            return pltpu.make_async_remote_copy(
                src_ref=rows(*block) if src is None else src, dst_ref=rows(*block),
                send_sem=send_sems.at[k], recv_sem=recv_sems.at[k], device_id=to, device_id_type=MESH)

        mine = pltpu.make_async_copy(x_ref, rows(*me), local_sem)
        mine.start()
        first = [copy(0, me, sibling, src=x_ref)]
        first += [copy(1 + j, me, (*chip, c), src=x_ref) for j, chip in enumerate(chips)]
        for cp in first:
            cp.start()
        passed = [copy(4 + j, (*chip, c), sibling) for j, chip in enumerate(chips)]
        for j, chip in enumerate(chips):
            copy(1 + j, (*chip, c), me).wait_recv()
            passed[j].start()
        copy(0, sibling, me).wait_recv()
        for j, chip in enumerate(chips):
            copy(4 + j, (*chip, 1 - c), me).wait_recv()
        for cp in first + passed:
            cp.wait_send()
        mine.wait()

    return pl.pallas_call(
        body, name="ag_small",
        out_shape=jax.ShapeDtypeStruct((8 * m_per, n), xs.dtype),
        in_specs=[pl.BlockSpec(memory_space=pltpu.VMEM)], out_specs=pl.BlockSpec(memory_space=pltpu.VMEM),
        scratch_shapes=[pltpu.SemaphoreType.DMA((7,)), pltpu.SemaphoreType.DMA((7,)), pltpu.SemaphoreType.DMA],
        compiler_params=pltpu.CompilerParams(vmem_limit_bytes=VMEM_LIMIT),
    )(xs)


_ANY = pl.BlockSpec(memory_space=pl.ANY)


def ag_chips(arrs):
    n = len(arrs)
    assert all(a.shape[0] == 2 for a in arrs)

    def body(*refs):
        ins, outs = refs[:n], refs[n:2 * n]
        send_sems, recv_sems, fwd_send, fwd_recv, local_sems = refs[2 * n:]
        x, y, c = _me()
        me = 2 * x + y
        chips = _other_chips(x, y)
        started = []
        for a in range(n):
            cp = pltpu.make_async_copy(ins[a], outs[a].at[me], local_sems.at[a])
            cp.start()
            started.append(cp)
        sends = []
        for a in range(n):
            for j, (px, py) in enumerate(chips):
                r = pltpu.make_async_remote_copy(
                    src_ref=ins[a].at[c], dst_ref=outs[a].at[me, c], send_sem=send_sems.at[3 * a + j],
                    recv_sem=recv_sems.at[3 * a + j], device_id=(px, py, c), device_id_type=MESH)
                r.start()
                sends.append(r)
        for a in range(n):
            for j, (px, py) in enumerate(chips):
                got = outs[a].at[2 * px + py, c]
                pltpu.make_async_remote_copy(
                    src_ref=ins[a].at[c], dst_ref=got, send_sem=send_sems.at[3 * a + j],
                    recv_sem=recv_sems.at[3 * a + j], device_id=(px, py, c), device_id_type=MESH).wait_recv()
                f = pltpu.make_async_remote_copy(
                    src_ref=got, dst_ref=got, send_sem=fwd_send.at[3 * a + j], recv_sem=fwd_recv.at[3 * a + j],
                    device_id=(x, y, 1 - c), device_id_type=MESH)
                f.start()
                sends.append(f)
        for a in range(n):
            for j, (px, py) in enumerate(chips):
                theirs = outs[a].at[2 * px + py, 1 - c]
                pltpu.make_async_remote_copy(
                    src_ref=theirs, dst_ref=theirs, send_sem=fwd_send.at[3 * a + j], recv_sem=fwd_recv.at[3 * a + j],
                    device_id=(x, y, 1 - c), device_id_type=MESH).wait_recv()
        for r in sends:
            r.wait_send()
        for cp in started:
            cp.wait()

    sems = pltpu.SemaphoreType.DMA((3 * n,))
    return pl.pallas_call(
        body, name="ag_chips",
        out_shape=[jax.ShapeDtypeStruct((4,) + a.shape, a.dtype) for a in arrs],
        in_specs=[_ANY] * n, out_specs=[_ANY] * n,
        scratch_shapes=[sems, sems, sems, sems, pltpu.SemaphoreType.DMA((n,))],
    )(*arrs)


def _ag_comm(arrs):
    n = len(arrs)

    def copies(ins, outs, sems, inbound):
        send_sems, recv_sems, local_sems = sems
        x, y, c = _me()
        me = 2 * x + y
        local = [pltpu.make_async_copy(ins[a], outs[a].at[me], local_sems.at[a]) for a in range(n)]
        out_cp, in_cp = [], []
        for a in range(n):
            for j, (px, py) in enumerate(_other_chips(x, y)):
                mk = functools.partial(pltpu.make_async_remote_copy, src_ref=ins[a], send_sem=send_sems.at[3 * a + j],
                                       recv_sem=recv_sems.at[3 * a + j], device_id=(px, py, c), device_id_type=MESH)
                out_cp.append(mk(dst_ref=outs[a].at[me]))
                if inbound:
                    in_cp.append(mk(dst_ref=outs[a].at[2 * px + py]))
        return local, out_cp, in_cp

    def start(ins, outs, sems):
        local, out_cp, _ = copies(ins, outs, sems, False)
        for cp in local + out_cp:
            cp.start()

    def wait(ins, outs, sems):
        local, out_cp, in_cp = copies(ins, outs, sems, True)
        for cp in in_cp:
            cp.wait_recv()
        for cp in out_cp:
            cp.wait_send()
        for cp in local:
            cp.wait()

    sems = [pltpu.SemaphoreType.DMA((3 * n,)), pltpu.SemaphoreType.DMA((3 * n,)), pltpu.SemaphoreType.DMA((n,))]
    return _Comm(arrs, [jax.ShapeDtypeStruct((4,) + a.shape, a.dtype) for a in arrs], sems, start, wait)


def _rs_comm(gs):
    n = len(gs)
    flips = [(fx, fy, fc) for fx in (0, 1) for fy in (0, 1) for fc in (0, 1)][1:]

    def copies(ins, outs, sems, inbound):
        send_sems, recv_sems, local_sems = sems
        x, y, c = _me()
        me = 4 * x + 2 * y + c
        local, out_cp, in_cp = [], [], []
        for a in range(n):
            rh = ins[a].shape[1] // 2
            mine = ins[a].at[2 * x + y, pl.ds(c * rh, rh), :]
            local.append(pltpu.make_async_copy(mine, outs[a].at[me], local_sems.at[a]))
            for j, (fx, fy, fc) in enumerate(flips):
                px, py, pc = (1 - x if fx else x), (1 - y if fy else y), (1 - c if fc else c)
                mk = functools.partial(pltpu.make_async_remote_copy, send_sem=send_sems.at[7 * a + j],
                                       recv_sem=recv_sems.at[7 * a + j], device_id=(px, py, pc), device_id_type=MESH)
                out_cp.append(mk(src_ref=ins[a].at[2 * px + py, pl.ds(pc * rh, rh), :], dst_ref=outs[a].at[me]))
                if inbound:
                    in_cp.append(mk(src_ref=mine, dst_ref=outs[a].at[4 * px + 2 * py + pc]))
        return local, out_cp, in_cp

    def start(ins, outs, sems):
        local, out_cp, _ = copies(ins, outs, sems, False)
        for cp in local + out_cp:
            cp.start()

    def wait(ins, outs, sems):
        local, out_cp, in_cp = copies(ins, outs, sems, True)
        for cp in in_cp:
            cp.wait_recv()
        for cp in out_cp:
            cp.wait_send()
        for cp in local:
            cp.wait()

    sems = [pltpu.SemaphoreType.DMA((7 * n,)), pltpu.SemaphoreType.DMA((7 * n,)), pltpu.SemaphoreType.DMA((n,))]
    return _Comm(gs, [jax.ShapeDtypeStruct((8, g.shape[1] // 2, g.shape[2]), g.dtype) for g in gs], sems, start, wait)


def sum_leading(q, name):
    K, R, C = q.shape
    tr = _pick(R, (256, 128, 64, 32, 16, 8))

    def body(q_ref, o_ref):
        acc = q_ref[0]
        for k in range(1, K):
            acc = acc + q_ref[k]
        o_ref[...] = acc

    return pl.pallas_call(
        body, name=name, grid=(R // tr,),
        in_specs=[pl.BlockSpec((K, tr, C), lambda i: (0, i, 0))], out_specs=pl.BlockSpec((tr, C), lambda i: (i, 0)),
        out_shape=jax.ShapeDtypeStruct((R, C), f32),
        compiler_params=_params(("parallel",)),
    )(q)


def rs_sum_devices(q, cidx):
    K, R, C = q.shape
    tr = _pick(R, (256, 128))

    def body(c_ref, q_ref, o_ref):
        acc = q_ref[0].astype(f32)
        for k in range(1, K):
            acc = acc + q_ref[k].astype(f32)
        o_ref[0] = acc

    return pl.pallas_call(
        body, name="rs_sum_devices",
        grid_spec=pltpu.PrefetchScalarGridSpec(
            num_scalar_prefetch=1, grid=(R // tr,),
            in_specs=[pl.BlockSpec((K, tr, C), lambda i, c_ref: (0, i, 0))],
            out_specs=pl.BlockSpec((1, tr, C), lambda i, c_ref: (c_ref[0], i, 0))),
        out_shape=jax.ShapeDtypeStruct((2, R, C), f32),
        compiler_params=_params(("parallel",)),
    )(cidx, q)


def rs_share_halves(rs):
    n = len(rs)

    def body(*refs):
        bufs = refs[n:2 * n]
        send_sems, recv_sems = refs[2 * n:]
        x, y, c = _me()
        cps = []
        for a in range(n):
            cp = pltpu.make_async_remote_copy(
                src_ref=bufs[a].at[c], dst_ref=bufs[a].at[c], send_sem=send_sems.at[a], recv_sem=recv_sems.at[a],
                device_id=(x, y, 1 - c), device_id_type=MESH)
            cp.start()
            cps.append(cp)
        for a, cp in enumerate(cps):
            pltpu.make_async_remote_copy(
                src_ref=bufs[a].at[c], dst_ref=bufs[a].at[1 - c], send_sem=send_sems.at[a], recv_sem=recv_sems.at[a],
                device_id=(x, y, 1 - c), device_id_type=MESH).wait_recv()
            cp.wait_send()

    return pl.pallas_call(
        body, name="rs_share_halves",
        out_shape=[jax.ShapeDtypeStruct(r.shape, r.dtype) for r in rs],
        in_specs=[_ANY] * n, out_specs=[_ANY] * n, input_output_aliases={a: a for a in range(n)},
        scratch_shapes=[pltpu.SemaphoreType.DMA((n,)), pltpu.SemaphoreType.DMA((n,))],
    )(*rs)


def ada_mod(c_all, ada_w):
    L, _, n = ada_w.shape

    def body(c_ref, w_ref, o_ref):
        o_ref[0] = _nn(_silu(c_ref[...]), w_ref[0], HI)

    return pl.pallas_call(
        body, name="ada_mod", grid=(L,),
        in_specs=[pl.BlockSpec((8, D), lambda l: (0, 0)), pl.BlockSpec((1, D, n), lambda l: (l, 0, 0))],
        out_specs=pl.BlockSpec((1, 8, n), lambda l: (l, 0, 0)),
        out_shape=jax.ShapeDtypeStruct((L, 8, n), f32),
        compiler_params=_params(("parallel",)),
    )(c_all, ada_w)


def ada_w_grad(c_all, dmod):
    L, _, n = dmod.shape

    def body(c_ref, d_ref, o_ref):
        o_ref[0] = _tn(_silu(c_ref[...]), d_ref[0], HI)

    return pl.pallas_call(
        body, name="ada_w_grad", grid=(L,),
        in_specs=[pl.BlockSpec((8, D), lambda l: (0, 0)), pl.BlockSpec((1, 8, n), lambda l: (l, 0, 0))],
        out_specs=pl.BlockSpec((1, D, n), lambda l: (l, 0, 0)),
        out_shape=jax.ShapeDtypeStruct((L, D, n), f32),
        compiler_params=_params(("parallel",)),
    )(c_all, dmod)


def adamw(w, g, m, v, name):
    shp = w.shape
    two = lambda a: a.reshape(-1, shp[-1])
    R, C = two(w).shape
    tr = _pick(R, (256, 128, 64, 32, 16, 8))
    bc1, bc2 = 1.0 - B1 ** STEP, 1.0 - B2 ** STEP

    def body(w_ref, g_ref, m_ref, v_ref, d_ref, mo_ref, vo_ref):
        gv = g_ref[...]
        mn = B1 * m_ref[...] + (1.0 - B1) * gv
        vn = B2 * v_ref[...] + (1.0 - B2) * (gv * gv)
        d_ref[...] = -LR * ((mn / bc1) / (jnp.sqrt(vn / bc2) + AEPS) + WD * w_ref[...])
        mo_ref[...] = mn
        vo_ref[...] = vn

    t = pl.BlockSpec((tr, C), lambda i: (i, 0))
    outs = pl.pallas_call(
        body, name=name, grid=(R // tr,),
        in_specs=[t] * 4, out_specs=[t] * 3, out_shape=[jax.ShapeDtypeStruct((R, C), f32)] * 3,
        compiler_params=_params(("parallel",)),
    )(two(w), two(g), two(m), two(v))
    return [o.reshape(shp) for o in outs]


def _pack(arrs):
    parts, offs, r0 = [], [], 0
    for a in arrs:
        n = a.size
        rows = -(-n // 1024) * 8
        parts.append(jnp.pad(a.reshape(-1), (0, rows * 128 - n)).reshape(rows, 128))
        offs.append((r0, rows))
        r0 += rows
    return jnp.concatenate(parts, axis=0), offs


def _unpack(buf, offs, shapes):
    out = []
    for (r0, rows), shp in zip(offs, shapes):
        n = 1
        for d in shp:
            n *= d
        out.append(buf[..., r0:r0 + rows, :].reshape(buf.shape[:-2] + (rows * 128,))[..., :n].reshape(buf.shape[:-2] + tuple(shp)))
    return out


def kernel(x, c, norm_w, ada_w, ada_b, a_w_in, a_conv_w, a_A_log, a_dt_bias, a_norm_w, a_w_out, b_w_in, b_f_bias, b_qn_w, b_kn_w, b_w_out, final_norm_w, loss_target, m_norm_w, m_ada_w, m_ada_b, m_a_w_in, m_a_conv_w, m_a_A_log, m_a_dt_bias, m_a_norm_w, m_a_w_out, m_b_w_in, m_b_f_bias, m_b_qn_w, m_b_kn_w, m_b_w_out, m_final_norm_w, v_norm_w, v_ada_w, v_ada_b, v_a_w_in, v_a_conv_w, v_a_A_log, v_a_dt_bias, v_a_norm_w, v_a_w_out, v_b_w_in, v_b_f_bias, v_b_qn_w, v_b_kn_w, v_b_w_out, v_final_norm_w):
    weights = dict(norm_w=norm_w, ada_w=ada_w, ada_b=ada_b, a_w_in=a_w_in, a_conv_w=a_conv_w, a_A_log=a_A_log,
                   a_dt_bias=a_dt_bias, a_norm_w=a_norm_w, a_w_out=a_w_out, b_w_in=b_w_in, b_f_bias=b_f_bias,
                   b_qn_w=b_qn_w, b_kn_w=b_kn_w, b_w_out=b_w_out, final_norm_w=final_norm_w)
    m_in = dict(norm_w=m_norm_w, ada_w=m_ada_w, ada_b=m_ada_b, a_w_in=m_a_w_in, a_conv_w=m_a_conv_w, a_A_log=m_a_A_log,
                a_dt_bias=m_a_dt_bias, a_norm_w=m_a_norm_w, a_w_out=m_a_w_out, b_w_in=m_b_w_in, b_f_bias=m_b_f_bias,
                b_qn_w=m_b_qn_w, b_kn_w=m_b_kn_w, b_w_out=m_b_w_out, final_norm_w=m_final_norm_w)
    v_in = dict(norm_w=v_norm_w, ada_w=v_ada_w, ada_b=v_ada_b, a_w_in=v_a_w_in, a_conv_w=v_a_conv_w, a_A_log=v_a_A_log,
                a_dt_bias=v_a_dt_bias, a_norm_w=v_a_norm_w, a_w_out=v_a_w_out, b_w_in=v_b_w_in, b_f_bias=v_b_f_bias,
                b_qn_w=v_b_qn_w, b_kn_w=v_b_kn_w, b_w_out=v_b_w_out, final_norm_w=v_final_norm_w)
    xi, yi, ci = _me()
    me_b, me_k = 4 * xi + 2 * yi + ci, 2 * xi + yi
    cidx = ci.astype(jnp.int32).reshape(1)
    S = x.shape[1]
    depth, n_a, n_b = norm_w.shape[0], a_w_in.shape[0], b_w_in.shape[0]
    x0, tgt = x.reshape(S, D), loss_target.reshape(S, D)

    c_all = ag_small(jnp.pad(c, ((0, 7), (0, 0)))).reshape(8, 8, D)[:, 0]
    nloc = ada_w.shape[2]
    parts = ag_small(ada_mod(c_all, ada_w).reshape(depth * 8, nloc)).reshape(4, 2, depth, 8, nloc)[:, 0]
    mine = lax.dynamic_index_in_dim(parts, me_b, axis=2, keepdims=False)
    mod = jnp.transpose(mine, (1, 0, 2)).reshape(depth, 4 * nloc) + ada_b
    shift, scale, gate = (mod[:, k * D:(k + 1) * D] for k in range(3))

    w_loc = [(a_w_in[i // 2] if i % 2 == 0 else b_w_in[i // 2]).astype(bf16) for i in range(depth)]
    wo_loc = [(a_w_out[i // 2] if i % 2 == 0 else b_w_out[i // 2]).astype(bf16) for i in range(depth)]
    pad_in = [(G_INP - G_IN) if i % 2 == 0 else (F_INP - F_IN) for i in range(depth)]
    halves = lambda w: w.reshape((2, w.shape[0] // 2) + w.shape[1:])

    def cols_in_place(g_in, pad):
        w = jnp.transpose(g_in, (1, 0, 2)).reshape(g_in.shape[1], -1)
        return jnp.pad(w, ((0, 0), (0, pad)))

    g_in0, g_conv = ag_chips([halves(w_loc[0]), a_conv_w])
    w_in_full = [cols_in_place(g_in0.reshape((4,) + w_loc[0].shape), pad_in[0])]
    w_out_full = []
    conv = [jnp.transpose(g_conv[:, l], (1, 0, 2)).reshape(CONV_K, -1) for l in range(n_a)]
    qw2 = [_row(jnp.tile(b_qn_w[l], 2)) for l in range(n_b)]
    kw2 = [_row(jnp.tile(b_kn_w[l], 2)) for l in range(n_b)]

    saved, xc = [], x0
    for i in range(depth):
        l = i // 2
        nxt = _ag_comm([w_loc[i + 1], wo_loc[i + 1]]) if i + 1 < depth else None
        h = ln_mod(xc, _row(norm_w[i]), _row(scale[i]), _row(shift[i]))
        name = "mm_a_in" if i % 2 == 0 else "mm_b_in"
        if i == 0:
            proj, got = matmul(h, w_in_full[0], "nn", name, comm=_ag_comm([wo_loc[0]]))
            w_out_full.append(got[0].reshape(-1, D))
        else:
            proj = matmul(h, w_in_full[i], "nn", name)
        if i % 2 == 0:
            pre = gdn_pre(proj, conv[l], _row(a_A_log[l]), _row(a_dt_bias[l]))
            res, got = gdn_fwd(*pre, comm=nxt)
            o2 = gdn_onorm(res[0], proj, _row(a_norm_w[l]))
            y, xn = out_proj(o2, w_out_full[i], xc, _row(gate[i]), "out_proj_a")
        else:
            pre = fox_pre(proj, _row(b_f_bias[l]), qw2[l], kw2[l])
            res, got = fox_attn(*pre, comm=nxt)
            o2 = fox_gate(res[0], proj)
            y, xn = out_proj(o2, w_out_full[i], xc, _row(gate[i]), "out_proj_b")
        saved.append((xc, h, proj, o2, y, pre, res))
        if nxt is not None:
            w_in_full.append(cols_in_place(got[0], pad_in[i + 1]))
            w_out_full.append(got[1].reshape(-1, D))
        xc = xn
    dx, st_f = final_loss(xc, _row(final_norm_w), tgt)

    d_norm, d_mod = [None] * depth, [None] * depth
    d_conv, d_alog, d_dtb, d_anw = [None] * n_a, [None] * n_a, [None] * n_a, [None] * n_a
    d_fb, d_qn, d_kn = [None] * n_b, [None] * n_b, [None] * n_b
    ex_in, ex_out, pend_in = [None] * depth, [None] * depth, None
    for i in reversed(range(depth)):
        l = i // 2
        xin, h, proj, o2, y, pre, res = saved[i]
        ab = "a" if i % 2 == 0 else "b"
        dy, st_g = gate_bwd(dx, y, _row(gate[i]))
        do2 = matmul(dy, w_out_full[i], "nt", f"mm_{ab}_do2")
        d_out = matmul(o2, dy, "tn", f"mm_{ab}_dwo")
        ride = _rs_comm(([] if pend_in is None else [pend_in]) + [d_out.reshape(4, d_out.shape[0] // 4, D).astype(bf16)])
        if i % 2 == 0:
            o, wv, at, tinv, vn, st = res
            do, dz, st_o = gdn_onorm_bwd(do2, o, proj, _row(a_norm_w[l]))
            grads, got = gdn_bwd(do, *pre, wv, at, tinv, vn, st, comm=ride)
            dcv, dba, st_s = gdn_pre_bwd(proj, conv[l], _row(a_A_log[l]), _row(a_dt_bias[l]), *grads)
            dproj, dcw = gdn_conv_bwd(proj, conv[l], dcv, dz, dba)
            d_conv[l], d_alog[l], d_dtb[l], d_anw[l] = dcw[:CONV_K], st_s[0], st_s[1], st_o[0]
        else:
            o, lse = res
            do, dz, delta = fox_gate_bwd(do2, o, proj)
            (dqa, dka, dv), got = fox_attn_bwd(*pre, do, lse, delta, comm=ride)
            dproj, st_b = fox_pre_bwd(proj, _row(b_f_bias[l]), qw2[l], kw2[l], dqa, dka, dv, dz)
            d_fb[l], d_qn[l], d_kn[l] = st_b[2, :F_H], st_b[0, :F_HD] + st_b[0, F_HD:], st_b[1, :F_HD] + st_b[1, F_HD:]
        ex_out[i] = got[-1]
        if pend_in is not None:
            ex_in[i + 1] = got[0]
        d_in = matmul(h, dproj, "tn", f"mm_{ab}_dw")
        cl = w_loc[i].shape[1]
        pend_in = jnp.transpose(d_in[:, :4 * cl].reshape(d_in.shape[0], 4, cl), (1, 0, 2)).astype(bf16)
        if i == 0:
            dh, got = matmul(dproj, w_in_full[i], "nt", f"mm_{ab}_dh", comm=_rs_comm([pend_in]))
            ex_in[0] = got[0]
        else:
            dh = matmul(dproj, w_in_full[i], "nt", f"mm_{ab}_dh")
        dx, st_n = ln_mod_bwd(xin, _row(norm_w[i]), _row(scale[i]), dh, dx)
        d_norm[i] = st_n[0]
        d_mod[i] = jnp.concatenate([st_n[2], st_n[1], st_g[0]])

    small = [jnp.stack(d_norm), jnp.stack(d_mod), jnp.stack(d_conv), jnp.stack(d_alog), jnp.stack(d_dtb), jnp.stack(d_anw),
             jnp.stack(d_fb), jnp.stack(d_qn), jnp.stack(d_kn), st_f[0], jnp.sum(st_f[1]).reshape(1)]
    shapes = [a.shape for a in small]
    buf, offs = _pack(small)
    gathered = ag_small(buf).reshape(8, buf.shape[0], 128)
    tot = _unpack(sum_leading(gathered, "sum_devices"), offs, shapes)
    g_norm, g_adab, g_convf, g_alog, g_dtb, g_anw, g_fb, g_qn, g_kn, g_fin, loss = tot
    dmod_all = _unpack(gathered, offs[1:2], shapes[1:2])[0]
    dmod_loc = lax.dynamic_slice_in_dim(dmod_all, me_k * nloc, nloc, axis=2)
    g_adaw = ada_w_grad(c_all, jnp.transpose(dmod_loc, (1, 0, 2)))
    g_conv_loc = lax.dynamic_slice_in_dim(g_convf, me_k * a_conv_w.shape[2], a_conv_w.shape[2], axis=2)

    flat = [q for i in range(depth) for q in (ex_in[i], ex_out[i])]
    done = rs_share_halves([rs_sum_devices(q, cidx) for q in flat])
    red = [d.reshape(-1, d.shape[-1]) for d in done]
    r_in, r_out = red[0::2], red[1::2]
    grads = dict(norm_w=g_norm, ada_w=g_adaw, ada_b=g_adab, a_w_in=jnp.stack(r_in[0::2]), a_conv_w=g_conv_loc,
                 a_A_log=g_alog, a_dt_bias=g_dtb, a_norm_w=g_anw, a_w_out=jnp.stack(r_out[0::2]),
                 b_w_in=jnp.stack(r_in[1::2]), b_f_bias=g_fb, b_qn_w=g_qn, b_kn_w=g_kn,
                 b_w_out=jnp.stack(r_out[1::2]), final_norm_w=g_fin)
    names = list(weights)
    upd = {n: adamw(weights[n], grads[n], m_in[n], v_in[n], "adamw_" + n) for n in names}
    return (loss.reshape(()), dx.reshape(x.shape), *[grads[n] for n in names], *[upd[n][0] for n in names],
            *[upd[n][1] for n in names], *[upd[n][2] for n in names])
```

```python
import functools

import jax
import jax.numpy as jnp
from jax import lax
from jax.experimental import pallas as pl
from jax.experimental.pallas import tpu as pltpu

f32, bf16 = jnp.float32, jnp.bfloat16
HI = lax.Precision.HIGHEST
MESH = pl.DeviceIdType.MESH

EPS = 1e-6
D = 1024
CHUNK = 64
GQK_H, GV_H, GHD = 8, 16, 128
G_CONV = 4096
G_Z0 = 4096
G_BA0 = 6144
G_IN, G_INP = 6176, 6272
CONV_K = 4
F_H, F_HD = 16, 64
F_W = 1024
F_F0 = 4096
F_IN, F_INP = 4112, 4224
LR, B1, B2, AEPS, WD, STEP = 0.001, 0.9, 0.999, 1e-08, 0.01, 10
NEG = -1e30
VMEM_LIMIT = 56 * 1024 * 1024


def _nn(a, b, prec=None):
    return lax.dot_general(a, b, (((1,), (0,)), ((), ())), preferred_element_type=f32, precision=prec)


def _nt(a, b, prec=None):
    return lax.dot_general(a, b, (((1,), (1,)), ((), ())), preferred_element_type=f32, precision=prec)


def _tn(a, b, prec=None):
    return lax.dot_general(a, b, (((0,), (0,)), ((), ())), preferred_element_type=f32, precision=prec)


def _iota(shape, axis):
    return lax.broadcasted_iota(jnp.int32, shape, axis)


def _sigmoid(x):
    return 0.5 * jnp.tanh(0.5 * x) + 0.5


def _softplus(x):
    return jnp.maximum(x, 0.0) + jnp.log(1.0 + jnp.exp(-jnp.abs(x)))


def _silu(x):
    return x * _sigmoid(x)


def _dsilu(x):
    s = _sigmoid(x)
    return s * (1.0 + x * (1.0 - s))


def _params(sem=None, vmem=VMEM_LIMIT):
    return pltpu.CompilerParams(dimension_semantics=sem, vmem_limit_bytes=vmem)


def _row(v):
    return v.reshape(1, -1)


class _Comm:
    def __init__(self, ins, out_shapes, sems, start, wait):
        self.ins, self.out_shapes, self.sems, self.start, self.wait = list(ins), list(out_shapes), list(sems), start, wait


def _call(body, *, name, grid, in_specs, out_specs, out_shape, scratch_shapes, sem, args, comm=None, prefetch=()):
    n_pf, n_in, n_out, n_s = len(prefetch), len(in_specs), len(out_specs), len(scratch_shapes)
    n_ci, n_co = (len(comm.ins), len(comm.out_shapes)) if comm is not None else (0, 0)

    def wrapped(*refs):
        pf, refs = refs[:n_pf], refs[n_pf:]
        core_in, c_in = refs[:n_in], refs[n_in:n_in + n_ci]
        o0 = n_in + n_ci
        core_out, c_out = refs[o0:o0 + n_out], refs[o0 + n_out:o0 + n_out + n_co]
        s0 = o0 + n_out + n_co
        core_s, c_sem = refs[s0:s0 + n_s], refs[s0 + n_s:]
        if comm is not None:
            first = functools.reduce(jnp.logical_and, [pl.program_id(d) == 0 for d in range(len(grid))])
            pl.when(first)(functools.partial(comm.start, c_in, c_out, c_sem))
        body(*pf, *core_in, *core_out, *core_s)
        if comm is not None:
            last = functools.reduce(jnp.logical_and, [pl.program_id(d) == grid[d] - 1 for d in range(len(grid))])
            pl.when(last)(functools.partial(comm.wait, c_in, c_out, c_sem))

    extra = ([], [], [], []) if comm is None else ([_ANY] * n_ci, [_ANY] * n_co, comm.out_shapes, comm.sems)
    spec = pltpu.PrefetchScalarGridSpec(
        num_scalar_prefetch=n_pf, grid=grid, in_specs=list(in_specs) + extra[0], out_specs=list(out_specs) + extra[1],
        scratch_shapes=list(scratch_shapes) + extra[3])
    outs = pl.pallas_call(
        wrapped, name=name if comm is None else name + "_x", grid_spec=spec, out_shape=list(out_shape) + extra[2],
        compiler_params=_params(sem if comm is None else ("arbitrary",) * len(grid)),
    )(*prefetch, *args, *(comm.ins if comm is not None else []))
    return outs[:n_out], outs[n_out:]


def _pick(n, pref):
    for t in pref:
        if n % t == 0:
            return t
    return n


MM_VMEM_BUDGET = 44 * 1024 * 1024


def _mm_tiles(M, N, K):
    best = None
    for tk in [K] + [t for t in (2048, 1408, 1024, 896, 512, 384, 256, 128) if K % t == 0 and t < K]:
        for tm in (2048, 1024, 512, 256, 128):
            for tn in (1408, 1024, 896, 512, 384, 256, 128):
                if M % tm or N % tn:
                    continue
                nk = K // tk
                need = 2 * 2 * (tm * tk + tk * tn) + 2 * 4 * tm * tn + (4 * tm * tn if nk > 1 else 0)
                if need <= MM_VMEM_BUDGET:
                    cand = ((nk, -tm * tn), (tm, tn, tk))
                    best = cand if best is None or cand[0] < best[0] else best
    return best[1]


def matmul(a, b, mode, name, out_dtype=f32, comm=None):
    if mode == "nn":
        (M, K), (_, N) = a.shape, b.shape
    elif mode == "nt":
        (M, K), (N, _) = a.shape, b.shape
    else:
        (K, M), (_, N) = a.shape, b.shape
    tm, tn, tk = _mm_tiles(M, N, K)
    nk = K // tk
    dot = {"nn": _nn, "nt": _nt, "tn": _tn}[mode]

    def body(a_ref, b_ref, o_ref, *acc):
        k = pl.program_id(2)
        part = dot(a_ref[...], b_ref[...])
        if nk == 1:
            o_ref[...] = part.astype(out_dtype)
        else:
            acc_ref = acc[0]

            @pl.when(k == 0)
            def _():
                acc_ref[...] = part

            @pl.when(k > 0)
            def _():
                acc_ref[...] += part

            @pl.when(k == nk - 1)
            def _():
                o_ref[...] = acc_ref[...].astype(out_dtype)

    a_spec = pl.BlockSpec((tk, tm), lambda i, j, k: (k, i)) if mode == "tn" else pl.BlockSpec((tm, tk), lambda i, j, k: (i, k))
    b_spec = pl.BlockSpec((tn, tk), lambda i, j, k: (j, k)) if mode == "nt" else pl.BlockSpec((tk, tn), lambda i, j, k: (k, j))
    outs, got = _call(
        body, name=name, grid=(M // tm, N // tn, nk),
        in_specs=[a_spec, b_spec], out_specs=[pl.BlockSpec((tm, tn), lambda i, j, k: (i, j))],
        out_shape=[jax.ShapeDtypeStruct((M, N), out_dtype)],
        scratch_shapes=[] if nk == 1 else [pltpu.VMEM((tm, tn), f32)],
        sem=("parallel", "parallel", "arbitrary"), args=(a, b), comm=comm)
    return outs[0] if comm is None else (outs[0], got)


def out_proj(o2, w, x, gate, name):
    S, K = o2.shape
    N = w.shape[1]
    tm, tn = 512, 512

    def body(a_ref, b_ref, x_ref, g_ref, y_ref, xn_ref):
        y = _nn(a_ref[...], b_ref[...])
        y_ref[...] = y
        xn_ref[...] = x_ref[...] + g_ref[...] * y

    return pl.pallas_call(
        body, name=name, grid=(S // tm, N // tn),
        in_specs=[pl.BlockSpec((tm, K), lambda i, j: (i, 0)), pl.BlockSpec((K, tn), lambda i, j: (0, j)),
                  pl.BlockSpec((tm, tn), lambda i, j: (i, j)), pl.BlockSpec((1, tn), lambda i, j: (0, j))],
        out_specs=[pl.BlockSpec((tm, tn), lambda i, j: (i, j))] * 2,
        out_shape=[jax.ShapeDtypeStruct((S, N), f32)] * 2,
        compiler_params=_params(("parallel", "parallel")),
    )(o2, w, x, gate)


def ln_mod(x, nw, scale, shift):
    S = x.shape[0]
    tm = 512

    def body(x_ref, nw_ref, sc_ref, sh_ref, h_ref):
        xv = x_ref[...]
        r = lax.rsqrt(jnp.mean(xv * xv, axis=-1, keepdims=True) + EPS)
        h_ref[...] = ((xv * r) * nw_ref[...] * (1.0 + sc_ref[...]) + sh_ref[...]).astype(bf16)

    vec = pl.BlockSpec((1, D), lambda i: (0, 0))
    return pl.pallas_call(
        body, name="ln_mod", grid=(S // tm,),
        in_specs=[pl.BlockSpec((tm, D), lambda i: (i, 0)), vec, vec, vec],
        out_specs=pl.BlockSpec((tm, D), lambda i: (i, 0)),
        out_shape=jax.ShapeDtypeStruct((S, D), bf16),
        compiler_params=_params(("parallel",)),
    )(x, nw, scale, shift)


def ln_mod_bwd(x, nw, scale, dh, dxres):
    S = x.shape[0]
    tm = 512
    nb = S // tm

    def body(x_ref, nw_ref, sc_ref, dh_ref, dr_ref, dx_ref, st_ref):
        i = pl.program_id(0)
        xv = x_ref[...]
        r = lax.rsqrt(jnp.mean(xv * xv, axis=-1, keepdims=True) + EPS)
        xn = xv * r
        dh = dh_ref[...]
        dxn = dh * (nw_ref[...] * (1.0 + sc_ref[...]))
        dx_ref[...] = dr_ref[...] + r * (dxn - xn * jnp.mean(dxn * xn, axis=-1, keepdims=True))
        p1 = jnp.sum(dh * xn, axis=0, keepdims=True)
        p2 = jnp.sum(dh, axis=0, keepdims=True)
        upd = jnp.concatenate([p1, p1, p2, jnp.zeros((5, D), f32)], axis=0)

        @pl.when(i == 0)
        def _():
            st_ref[...] = upd

        @pl.when(i > 0)
        def _():
            st_ref[...] += upd

        @pl.when(i == nb - 1)
        def _():
            st_ref[0:1, :] = st_ref[0:1, :] * (1.0 + sc_ref[...])
            st_ref[1:2, :] = st_ref[1:2, :] * nw_ref[...]

    vec = pl.BlockSpec((1, D), lambda i: (0, 0))
    tile = pl.BlockSpec((tm, D), lambda i: (i, 0))
    return pl.pallas_call(
        body, name="ln_mod_bwd", grid=(S // tm,),
        in_specs=[tile, vec, vec, tile, tile],
        out_specs=[tile, pl.BlockSpec((8, D), lambda i: (0, 0))],
        out_shape=[jax.ShapeDtypeStruct((S, D), f32), jax.ShapeDtypeStruct((8, D), f32)],
        compiler_params=_params(("arbitrary",)),
    )(x, nw, scale, dh, dxres)


def final_loss(x, fw, tgt):
    S = x.shape[0]
    tm = 512

    def body(x_ref, w_ref, t_ref, dx_ref, st_ref):
        i = pl.program_id(0)
        xv = x_ref[...]
        r = lax.rsqrt(jnp.mean(xv * xv, axis=-1, keepdims=True) + EPS)
        xn = xv * r
        err = xn * w_ref[...] - t_ref[...]
        dy = err * (1.0 / D)
        dxn = dy * w_ref[...]
        dx_ref[...] = r * (dxn - xn * jnp.mean(dxn * xn, axis=-1, keepdims=True))
        p1 = jnp.sum(dy * xn, axis=0, keepdims=True)
        p2 = jnp.sum(err * err, axis=0, keepdims=True) * (0.5 / D)
        upd = jnp.concatenate([p1, p2, jnp.zeros((6, D), f32)], axis=0)

        @pl.when(i == 0)
        def _():
            st_ref[...] = upd

        @pl.when(i > 0)
        def _():
            st_ref[...] += upd

    tile = pl.BlockSpec((tm, D), lambda i: (i, 0))
    return pl.pallas_call(
        body, name="final_loss", grid=(S // tm,),
        in_specs=[tile, pl.BlockSpec((1, D), lambda i: (0, 0)), tile],
        out_specs=[tile, pl.BlockSpec((8, D), lambda i: (0, 0))],
        out_shape=[jax.ShapeDtypeStruct((S, D), f32), jax.ShapeDtypeStruct((8, D), f32)],
        compiler_params=_params(("arbitrary",)),
    )(x, fw, tgt)


def gate_bwd(dx, y, gate):
    S = dx.shape[0]
    tm = 512

    def body(dx_ref, y_ref, g_ref, dy_ref, st_ref):
        i = pl.program_id(0)
        dxv = dx_ref[...]
        dy_ref[...] = (g_ref[...] * dxv).astype(bf16)
        upd = jnp.concatenate([jnp.sum(dxv * y_ref[...], axis=0, keepdims=True), jnp.zeros((7, D), f32)], axis=0)

        @pl.when(i == 0)
        def _():
            st_ref[...] = upd

        @pl.when(i > 0)
        def _():
            st_ref[...] += upd

    tile = pl.BlockSpec((tm, D), lambda i: (i, 0))
    return pl.pallas_call(
        body, name="gate_bwd", grid=(S // tm,),
        in_specs=[tile, tile, pl.BlockSpec((1, D), lambda i: (0, 0))],
        out_specs=[tile, pl.BlockSpec((8, D), lambda i: (0, 0))],
        out_shape=[jax.ShapeDtypeStruct((S, D), bf16), jax.ShapeDtypeStruct((8, D), f32)],
        compiler_params=_params(("arbitrary",)),
    )(dx, y, gate)


def _chunk_mats(tm):
    r, c = _iota((tm, tm), 0), _iota((tm, tm), 1)
    same = jnp.right_shift(r, 6) == jnp.right_shift(c, 6)
    ltri = jnp.where(same & (c <= r), 1.0, 0.0).astype(f32)
    utri = jnp.where(same & (c >= r), 1.0, 0.0).astype(f32)
    bsame = jnp.where(same, 1.0, 0.0).astype(f32)
    return ltri, utri, bsame


def _gdn_scalars(ba, alog, dtb, ltri, bsame):
    beta = _sigmoid(ba[:, 0:16])
    u = ba[:, 16:32] + dtb
    neg_a = -jnp.exp(alog)
    g = neg_a * _softplus(u)
    gc = _nn(ltri, g, HI)
    glast = _nn(bsame, g, HI)
    return beta, u, neg_a, g, gc, glast


def _conv_taps(p_ref, halo_ref, first, gi):
    cs = slice(gi * 128, (gi + 1) * 128)
    cur = p_ref[:, cs]
    hal = jnp.where(first, 0.0, halo_ref[:, cs])
    ext = jnp.concatenate([hal, cur], axis=0)
    return [cur] + [pltpu.roll(ext, s, 0)[8:] for s in range(1, CONV_K)]


def _conv(taps, w):
    cv = taps[0] * w[3:4]
    for s in range(1, CONV_K):
        cv = cv + taps[s] * w[3 - s:4 - s]
    return cv


def _l2n(x):
    return x * lax.rsqrt(jnp.sum(x * x, axis=-1, keepdims=True) + EPS)


def _gdn_in_specs(tm, S):
    nb8 = tm // 8
    return [pl.BlockSpec((tm, G_CONV), lambda i: (i, 0)),
            pl.BlockSpec((8, G_CONV), lambda i: (jnp.maximum(i * nb8 - 1, 0), 0)),
            pl.BlockSpec((tm, 128), lambda i: (i, G_BA0 // 128))]


def gdn_pre(proj, conv_w, alog, dtb):
    S = proj.shape[0]
    tm = 256
    nch = tm // CHUNK

    def body(p_ref, halo_ref, ba_ref, w_ref, al_ref, dt_ref,
             q_ref, k_ref, kb_ref, kbg_ref, vb_ref, qd_ref, kd_ref, d_ref, gl_ref):
        first = pl.program_id(0) == 0
        ltri, _, bsame = _chunk_mats(tm)
        beta, _, _, _, gc, glast = _gdn_scalars(ba_ref[...], al_ref[...], dt_ref[...], ltri, bsame)
        eg, ek, egl = jnp.exp(gc), jnp.exp(glast - gc), jnp.exp(glast)
        eye = jnp.where(_iota((16, 16), 0) == _iota((16, 16), 1), 1.0, 0.0).astype(f32)
        gct = _nt(eye, gc, HI)
        low = _iota((CHUNK, CHUNK), 0) >= _iota((CHUNK, CHUNK), 1)

        def act(gi):
            return _silu(_conv(_conv_taps(p_ref, halo_ref, first, gi), w_ref[:, gi * 128:(gi + 1) * 128]))

        for j in range(GQK_H):
            js = slice(j * 128, (j + 1) * 128)
            qn = _l2n(act(j)) * (GHD ** -0.5)
            kn = _l2n(act(GQK_H + j))
            q_ref[:, js] = qn.astype(bf16)
            k_ref[:, js] = kn.astype(bf16)
            for e in range(2):
                h = 2 * j + e
                hs = slice(h * 128, (h + 1) * 128)
                v = act(2 * GQK_H + h)
                bh, egh, ekh = beta[:, h:h + 1], eg[:, h:h + 1], ek[:, h:h + 1]
                kbv = kn * bh
                kb_ref[:, hs] = kbv.astype(bf16)
                kbg_ref[:, hs] = (kbv * egh).astype(bf16)
                vb_ref[:, hs] = (v * bh).astype(bf16)
                qd_ref[:, hs] = (qn * egh).astype(bf16)
                kd_ref[:, hs] = (kn * ekh).astype(bf16)
                for c in range(nch):
                    rs = slice(c * CHUNK, (c + 1) * CHUNK)
                    diff = gc[rs, h:h + 1] - gct[h:h + 1, rs]
                    d_ref[rs, h * CHUNK:(h + 1) * CHUNK] = jnp.where(low, jnp.exp(jnp.where(low, diff, 0.0)), 0.0)
                    gl_ref[c * 8:(c + 1) * 8, hs] = jnp.broadcast_to(egl[c * CHUNK:c * CHUNK + 8, h:h + 1], (8, 128))

    full = lambda shape: pl.BlockSpec(shape, lambda i: (0, 0))
    t1 = pl.BlockSpec((tm, 1024), lambda i: (i, 0))
    t2 = pl.BlockSpec((tm, 2048), lambda i: (i, 0))
    sd = jax.ShapeDtypeStruct
    return pl.pallas_call(
        body, name="gdn_pre", grid=(S // tm,),
        in_specs=_gdn_in_specs(tm, S) + [full((CONV_K, G_CONV)), full((1, 16)), full((1, 16))],
        out_specs=[t1, t1, t2, t2, t2, t2, t2, t1, pl.BlockSpec((tm // 8, 2048), lambda i: (i, 0))],
        out_shape=[sd((S, 1024), bf16)] * 2 + [sd((S, 2048), bf16)] * 5 + [sd((S, 1024), f32), sd((S // 8, 2048), f32)],
        compiler_params=_params(("parallel",)),
    )(proj, proj, proj, conv_w, alog, dtb)


def _bnn(a, b):
    return lax.dot_general(a, b, (((2,), (1,)), ((0,), (0,))), preferred_element_type=f32)


def _bnt(a, b):
    return lax.dot_general(a, b, (((2,), (2,)), ((0,), (0,))), preferred_element_type=f32)


def _btn(a, b):
    return lax.dot_general(a, b, (((1,), (1,)), ((0,), (0,))), preferred_element_type=f32)


def _split(a):
    hi = a.astype(bf16)
    return hi, (a - hi.astype(f32)).astype(bf16)


def _cat3(h, l, axis, lhs):
    return jnp.concatenate([h, h, l] if lhs else [h, l, h], axis=axis)


def _tri_inv_b(L):
    eye = jnp.where(_iota((1, CHUNK, CHUNK), 1) == _iota((1, CHUNK, CHUNK), 2), 1.0, 0.0).astype(f32)
    P = -L
    T = eye + P
    ph, pl_ = _split(P)
    for _ in range(5):
        P = _bnn(_cat3(ph, pl_, 2, True), _cat3(ph, pl_, 1, False))
        ph, pl_ = _split(P)
        th, tl = _split(T)
        T = T + _bnn(_cat3(th, tl, 2, True), _cat3(ph, pl_, 1, False))
    return T


GTB = 512
GQH_FWD, GQH_BWD = 1, 2


def _gdn_slices(ncb, gnv):
    pairs = [(c, e) for c in range(ncb) for e in range(gnv)]
    rs = lambda c: slice(c * CHUNK, (c + 1) * CHUNK)
    cs = lambda e: slice(e * 128, (e + 1) * 128)
    ds_ = lambda e: slice(e * CHUNK, (e + 1) * CHUNK)
    ks = lambda e: slice((e // 2) * 128, (e // 2 + 1) * 128)
    return pairs, rs, cs, ds_, ks


def gdn_fwd(q, k, kb, kbg, vb, qd, kd, dm, gl8, comm=None):
    S = q.shape[0]
    nb, ncb = S // GTB, GTB // CHUNK
    GQH, GNV = GQH_FWD, 2 * GQH_FWD
    pairs, rs, cs, ds_, ks = _gdn_slices(ncb, GNV)

    def body(q_ref, k_ref, kb_ref, kbg_ref, vb_ref, qd_ref, kd_ref, d_ref, gl_ref,
             o_ref, w_ref, at_ref, t_ref, vn_ref, st_ref, state, u_scr):
        @pl.when(pl.program_id(1) == 0)
        def _():
            state[...] = jnp.zeros_like(state)

        stk = lambda ref, lanes: jnp.stack([ref[rs(c), lanes(e)] for c, e in pairs])
        kq = stk(k_ref, ks)
        dmat = stk(d_ref, ds_)
        strict = _iota((1, CHUNK, CHUNK), 1) > _iota((1, CHUNK, CHUNK), 2)
        T = _tri_inv_b(jnp.where(strict, _bnt(stk(kb_ref, cs), kq) * dmat, 0.0))
        tb = T.astype(bf16)
        u_scr[...] = _bnn(tb, stk(vb_ref, cs))
        wb = _bnn(tb, stk(kbg_ref, cs)).astype(bf16)
        per_qk = lambda ref: jnp.stack([ref[rs(c), ks(e)] for c, e in pairs if e % 2 == 0])
        qk = _bnt(per_qk(q_ref), per_qk(k_ref))
        for b, (c, e) in enumerate(pairs):
            w_ref[rs(c), cs(e)] = wb[b]
            at_ref[rs(c), ds_(e)] = (qk[b // 2] * dmat[b]).astype(bf16)
            t_ref[rs(c), ds_(e)] = T[b]
        for b, (c, e) in enumerate(pairs):
            sb = state[e].astype(bf16)
            vnb = (u_scr[b] - _nn(w_ref[rs(c), cs(e)], sb)).astype(bf16)
            o_ref[rs(c), cs(e)] = _nn(qd_ref[rs(c), cs(e)], sb) + _nn(at_ref[rs(c), ds_(e)], vnb)
            st_ref[c * 128:(c + 1) * 128, cs(e)] = sb
            state[e] = state[e] * gl_ref[c * 8:c * 8 + 1, cs(e)] + _tn(kd_ref[rs(c), cs(e)], vnb)
            vn_ref[rs(c), cs(e)] = vnb

    b1 = pl.BlockSpec((GTB, 128 * GQH), lambda j, i: (i, j))
    b2 = pl.BlockSpec((GTB, 256 * GQH), lambda j, i: (i, j))
    sd = jax.ShapeDtypeStruct
    return _call(
        body, name="gdn_fwd", grid=(GQK_H // GQH, nb),
        in_specs=[b1, b1, b2, b2, b2, b2, b2, b1, pl.BlockSpec((GTB // 8, 256 * GQH), lambda j, i: (i, j))],
        out_specs=[b2, b2, b1, b1, b2, pl.BlockSpec((ncb * 128, 256 * GQH), lambda j, i: (i, j))],
        out_shape=[sd((S, 2048), f32), sd((S, 2048), bf16), sd((S, 1024), bf16), sd((S, 1024), f32),
                   sd((S, 2048), bf16), sd((S // CHUNK * 128, 2048), bf16)],
        scratch_shapes=[pltpu.VMEM((GNV, 128, 128), f32), pltpu.VMEM((GNV * ncb, CHUNK, 128), f32)],
        sem=("parallel", "arbitrary"), args=(q, k, kb, kbg, vb, qd, kd, dm, gl8), comm=comm)


def gdn_bwd(do, q, k, kb, kbg, vb, qd, kd, dm, gl8, w, at, T, vn, st, comm=None):
    S = q.shape[0]
    nb, ncb = S // GTB, GTB // CHUNK
    GQH, GNV = GQH_BWD, 2 * GQH_BWD
    pairs, rs, cs, ds_, ks = _gdn_slices(ncb, GNV)

    def body(do_ref, q_ref, k_ref, kb_ref, kbg_ref, vb_ref, qd_ref, kd_ref, d_ref, gl_ref, w_ref, at_ref, t_ref, vn_ref, st_ref,
             dq_ref, dk_ref, dkb_ref, dkbg_ref, dvb_ref, dqd_ref, dkd_ref, dgc_ref, dstate, dvn_scr, dw_scr, dat_scr, dgl_scr):
        @pl.when(pl.program_id(1) == 0)
        def _():
            dstate[...] = jnp.zeros_like(dstate)

        for b, (c, e) in reversed(list(enumerate(pairs))):
            dob = do_ref[rs(c), cs(e)].astype(bf16)
            sb = st_ref[c * 128:(c + 1) * 128, cs(e)]
            vnb = vn_ref[rs(c), cs(e)]
            gl = gl_ref[c * 8:c * 8 + 1, cs(e)]
            dS = dstate[e]
            dsb = dS.astype(bf16)
            dvnb = (_tn(at_ref[rs(c), ds_(e)], dob) + _nn(kd_ref[rs(c), cs(e)], dsb)).astype(bf16)
            dvn_scr[b] = dvnb
            dat_scr[b] = _nt(dob, vnb)
            dqd_ref[rs(c), cs(e)] = _nt(dob, sb)
            dkd_ref[rs(c), cs(e)] = _nt(vnb, dsb)
            dw_scr[b] = (-_nt(dvnb, sb)).astype(bf16)
            dgl = jnp.sum(jnp.sum(dS * sb.astype(f32), axis=1, keepdims=True), axis=0, keepdims=True)
            dgl_scr[b] = jnp.broadcast_to(dgl * gl, (8, 128))
            dstate[e] = gl * dS + _tn(qd_ref[rs(c), cs(e)], dob) - _tn(w_ref[rs(c), cs(e)], dvnb)

        stk = lambda ref, lanes: jnp.stack([ref[rs(c), lanes(e)] for c, e in pairs])
        kq, qq = stk(k_ref, ks), stk(q_ref, ks)
        kbb = stk(kb_ref, cs)
        Tm = stk(t_ref, ds_)
        tb = Tm.astype(bf16)
        dvn, dw = dvn_scr[...], dw_scr[...]
        dT = _bnt(dvn, stk(vb_ref, cs)) + _bnt(dw, stk(kbg_ref, cs))
        dvb, dkbg = _btn(tb, dvn), _btn(tb, dw)
        th, tl = _split(Tm)
        xh, xl = _split(_bnt(_cat3(*_split(dT), 2, True), _cat3(th, tl, 2, False)))
        dL = -_btn(_cat3(th, tl, 1, True), _cat3(xh, xl, 1, False))
        dmat = stk(d_ref, ds_)
        strict = _iota((1, CHUNK, CHUNK), 1) > _iota((1, CHUNK, CHUNK), 2)
        dA = jnp.where(strict, dL * dmat, 0.0)
        dB = dat_scr[...] * dmat
        dAb, dBb = dA.astype(bf16), dB.astype(bf16)
        dkb = _bnn(dAb, kq)
        dkc = _btn(dAb, kbb) + _btn(dBb, qq)
        dqc = _bnn(dBb, kq)
        M = dA * _bnt(kbb, kq) + dB * _bnt(qq, kq)
        mh, ml = _split(M)
        colsum = _btn(jnp.concatenate([mh, ml], axis=1), jnp.ones((GNV * ncb, 2 * CHUNK, 128), bf16))
        lastrow = _iota((1, CHUNK, 128), 1) == CHUNK - 1
        for b, (c, e) in enumerate(pairs):
            dvb_ref[rs(c), cs(e)] = dvb[b]
            dkbg_ref[rs(c), cs(e)] = dkbg[b]
            dkb_ref[rs(c), cs(e)] = dkb[b]
            dgc_ref[rs(c), cs(e)] = (jnp.sum(M[b], axis=1, keepdims=True) - colsum[b]
                                     + jnp.where(lastrow[0], dgl_scr[b][0:1, :], 0.0))
        for b, (c, e) in enumerate(pairs):
            if e % 2 == 0:
                dq_ref[rs(c), ks(e)] = dqc[b] + dqc[b + 1]
                dk_ref[rs(c), ks(e)] = dkc[b] + dkc[b + 1]

    b1 = pl.BlockSpec((GTB, 128 * GQH), lambda j, i: (nb - 1 - i, j))
    b2 = pl.BlockSpec((GTB, 256 * GQH), lambda j, i: (nb - 1 - i, j))
    sd = jax.ShapeDtypeStruct
    return _call(
        body, name="gdn_bwd", grid=(GQK_H // GQH, nb),
        in_specs=[b2, b1, b1, b2, b2, b2, b2, b2, b1, pl.BlockSpec((GTB // 8, 256 * GQH), lambda j, i: (nb - 1 - i, j)),
                  b2, b1, b1, b2, pl.BlockSpec((ncb * 128, 256 * GQH), lambda j, i: (nb - 1 - i, j))],
        out_specs=[b1, b1, b2, b2, b2, b2, b2, b2],
        out_shape=[sd((S, 1024), f32)] * 2 + [sd((S, 2048), f32)] * 6,
        scratch_shapes=[pltpu.VMEM((GNV, 128, 128), f32), pltpu.VMEM((GNV * ncb, CHUNK, 128), bf16),
                        pltpu.VMEM((GNV * ncb, CHUNK, 128), bf16), pltpu.VMEM((GNV * ncb, CHUNK, CHUNK), f32),
                        pltpu.VMEM((GNV * ncb, 8, 128), f32)],
        sem=("parallel", "arbitrary"), args=(do, q, k, kb, kbg, vb, qd, kd, dm, gl8, w, at, T, vn, st), comm=comm)


def gdn_onorm(o, proj, nw):
    S = o.shape[0]
    tm = 256

    def body(o_ref, z_ref, nw_ref, o2_ref):
        for h in range(GV_H):
            hs = slice(h * 128, (h + 1) * 128)
            oh = o_ref[:, hs]
            r = lax.rsqrt(jnp.mean(oh * oh, axis=-1, keepdims=True) + EPS)
            o2_ref[:, hs] = (((oh * r) * nw_ref[...]) * _silu(z_ref[:, hs])).astype(bf16)

    t2 = pl.BlockSpec((tm, 2048), lambda i: (i, 0))
    return pl.pallas_call(
        body, name="gdn_onorm", grid=(S // tm,),
        in_specs=[t2, pl.BlockSpec((tm, 2048), lambda i: (i, G_Z0 // 2048)), pl.BlockSpec((1, 128), lambda i: (0, 0))],
        out_specs=t2, out_shape=jax.ShapeDtypeStruct((S, 2048), bf16),
        compiler_params=_params(("parallel",)),
    )(o, proj, nw)


def gdn_onorm_bwd(do2, o, proj, nw):
    S = o.shape[0]
    tm = 256

    def body(d_ref, o_ref, z_ref, nw_ref, do_ref, dz_ref, st_ref):
        i = pl.program_id(0)
        acc = jnp.zeros((1, 128), f32)
        for h in range(GV_H):
            hs = slice(h * 128, (h + 1) * 128)
            oh, z, d2 = o_ref[:, hs], z_ref[:, hs], d_ref[:, hs]
            r = lax.rsqrt(jnp.mean(oh * oh, axis=-1, keepdims=True) + EPS)
            on = oh * r
            dt = d2 * _silu(z)
            dz_ref[:, hs] = (d2 * (on * nw_ref[...]) * _dsilu(z)).astype(bf16)
            don = dt * nw_ref[...]
            acc = acc + jnp.sum(dt * on, axis=0, keepdims=True)
            do_ref[:, hs] = r * (don - on * jnp.mean(don * on, axis=-1, keepdims=True))
        upd = jnp.concatenate([acc, jnp.zeros((7, 128), f32)], axis=0)

        @pl.when(i == 0)
        def _():
            st_ref[...] = upd

        @pl.when(i > 0)
        def _():
            st_ref[...] += upd

    t2 = pl.BlockSpec((tm, 2048), lambda i: (i, 0))
    sd = jax.ShapeDtypeStruct
    return pl.pallas_call(
        body, name="gdn_onorm_bwd", grid=(S // tm,),
        in_specs=[t2, t2, pl.BlockSpec((tm, 2048), lambda i: (i, G_Z0 // 2048)), pl.BlockSpec((1, 128), lambda i: (0, 0))],
        out_specs=[t2, t2, pl.BlockSpec((8, 128), lambda i: (0, 0))],
        out_shape=[sd((S, 2048), f32), sd((S, 2048), bf16), sd((8, 128), f32)],
        compiler_params=_params(("arbitrary",)),
    )(do2, o, proj, nw)


def gdn_pre_bwd(proj, conv_w, alog, dtb, dq, dk, dkb, dkbg, dvb, dqd, dkd, dgcd):
    S = proj.shape[0]
    tm = 128

    def body(p_ref, halo_ref, ba_ref, w_ref, al_ref, dt_ref, dq_ref, dk_ref, dkb_ref, dkbg_ref, dvb_ref, dqd_ref, dkd_ref, dgc_ref,
             dcv_ref, dba_ref, st_ref):
        i = pl.program_id(0)
        first = i == 0
        ltri, utri, bsame = _chunk_mats(tm)
        beta, u, neg_a, g, gc, glast = _gdn_scalars(ba_ref[...], al_ref[...], dt_ref[...], ltri, bsame)
        eg, ek = jnp.exp(gc), jnp.exp(glast - gc)
        lane16 = _iota((tm, 16), 1)
        dgc_all = jnp.zeros((tm, 16), f32)
        rkd_all = jnp.zeros((tm, 16), f32)
        dbeta_all = jnp.zeros((tm, 16), f32)

        def pre(gi):
            return _conv(_conv_taps(p_ref, halo_ref, first, gi), w_ref[:, gi * 128:(gi + 1) * 128])

        def l2n_bwd(xt, dy):
            r = lax.rsqrt(jnp.sum(xt * xt, axis=-1, keepdims=True) + EPS)
            y = xt * r
            return r * (dy - y * jnp.sum(dy * y, axis=-1, keepdims=True))

        for j in range(GQK_H):
            js = slice(j * 128, (j + 1) * 128)
            cvq, cvk = pre(j), pre(GQK_H + j)
            qt, kt = _silu(cvq), _silu(cvk)
            qn = _l2n(qt) * (GHD ** -0.5)
            kn = _l2n(kt)
            dq_tot, dk_tot = dq_ref[:, js], dk_ref[:, js]
            for e in range(2):
                h = 2 * j + e
                hs = slice(h * 128, (h + 1) * 128)
                gv = 2 * GQK_H + h
                cvv = pre(gv)
                v = _silu(cvv)
                bh, egh, ekh = beta[:, h:h + 1], eg[:, h:h + 1], ek[:, h:h + 1]
                dkbg, dkd, dqd, dvb = dkbg_ref[:, hs], dkd_ref[:, hs], dqd_ref[:, hs], dvb_ref[:, hs]
                dkb_t = dkb_ref[:, hs] + dkbg * egh
                dk_tot = dk_tot + dkb_t * bh + dkd * ekh
                dq_tot = dq_tot + dqd * egh
                dcv_ref[:, gv * 128:(gv + 1) * 128] = (dvb * bh) * _dsilu(cvv)
                dbeta = jnp.sum(dkb_t * kn, axis=-1, keepdims=True) + jnp.sum(dvb * v, axis=-1, keepdims=True)
                rkd = jnp.sum(dkd * (kn * ekh), axis=-1, keepdims=True)
                dgc = (dgc_ref[:, hs][:, 0:1] + jnp.sum(dkbg * (kn * bh * egh), axis=-1, keepdims=True)
                       + jnp.sum(dqd * (qn * egh), axis=-1, keepdims=True) - rkd)
                sel = lane16 == h
                dgc_all = dgc_all + jnp.where(sel, dgc, 0.0)
                rkd_all = rkd_all + jnp.where(sel, rkd, 0.0)
                dbeta_all = dbeta_all + jnp.where(sel, dbeta, 0.0)
            dcv_ref[:, js] = l2n_bwd(qt, dq_tot * (GHD ** -0.5)) * _dsilu(cvq)
            ks = slice((GQK_H + j) * 128, (GQK_H + j + 1) * 128)
            dcv_ref[:, ks] = l2n_bwd(kt, dk_tot) * _dsilu(cvk)

        islast = jnp.bitwise_and(_iota((tm, 16), 0), CHUNK - 1) == CHUNK - 1
        dgc_all = dgc_all + jnp.where(islast, _nn(bsame, rkd_all, HI), 0.0)
        dg = _nn(utri, dgc_all, HI)
        da = dg * neg_a * _sigmoid(u)
        db = dbeta_all * beta * (1.0 - beta)
        r16, c128 = _iota((16, 128), 0), _iota((16, 128), 1)
        pb = jnp.where(c128 == r16, 1.0, 0.0).astype(f32)
        pa = jnp.where(c128 == r16 + 16, 1.0, 0.0).astype(f32)
        dba_ref[...] = _nn(db, pb, HI) + _nn(da, pa, HI)
        upd = jnp.concatenate([jnp.sum(dg * g, axis=0, keepdims=True), jnp.sum(da, axis=0, keepdims=True),
                               jnp.zeros((6, 16), f32)], axis=0)

        @pl.when(i == 0)
        def _():
            st_ref[...] = upd

        @pl.when(i > 0)
        def _():
            st_ref[...] += upd

    full = lambda shape: pl.BlockSpec(shape, lambda i: (0, 0))
    t1 = pl.BlockSpec((tm, 1024), lambda i: (i, 0))
    t2 = pl.BlockSpec((tm, 2048), lambda i: (i, 0))
    sd = jax.ShapeDtypeStruct
    return pl.pallas_call(
        body, name="gdn_pre_bwd", grid=(S // tm,),
        in_specs=_gdn_in_specs(tm, S) + [full((CONV_K, G_CONV)), full((1, 16)), full((1, 16)), t1, t1] + [t2] * 6,
        out_specs=[pl.BlockSpec((tm, G_CONV), lambda i: (i, 0)), pl.BlockSpec((tm, 128), lambda i: (i, 0)), full((8, 16))],
        out_shape=[sd((S, G_CONV), f32), sd((S, 128), f32), sd((8, 16), f32)],
        compiler_params=_params(("arbitrary",)),
    )(proj, proj, proj, conv_w, alog, dtb, dq, dk, dkb, dkbg, dvb, dqd, dkd, dgcd)


def gdn_conv_bwd(proj, conv_w, dcv, dz, dba):
    S = proj.shape[0]
    tm = 256
    nb, nb8 = S // tm, tm // 8

    def body(p_ref, halo_ref, w_ref, dcv_ref, nxt_ref, dz_ref, dba_ref, dp_ref, dw_ref):
        i = pl.program_id(0)
        first, last = i == 0, i == nb - 1
        for gi in range(G_CONV // 128):
            cs = slice(gi * 128, (gi + 1) * 128)
            taps = _conv_taps(p_ref, halo_ref, first, gi)
            cur = dcv_ref[:, cs]
            ext = jnp.concatenate([cur, jnp.where(last, 0.0, nxt_ref[:, cs])], axis=0)
            w = w_ref[:, cs]
            dp = cur * w[3:4]
            rows = [jnp.sum(cur * taps[3 - kk], axis=0, keepdims=True) for kk in range(CONV_K)]
            for s in range(1, CONV_K):
                dp = dp + pltpu.roll(ext, tm + 8 - s, 0)[:tm] * w[3 - s:4 - s]
            dp_ref[:, cs] = dp.astype(bf16)
            upd = jnp.concatenate(rows + [jnp.zeros((4, 128), f32)], axis=0)

            @pl.when(first)
            def _():
                dw_ref[:, cs] = upd

            @pl.when(i > 0)
            def _():
                dw_ref[:, cs] += upd

        dp_ref[:, G_Z0:G_BA0] = dz_ref[...]
        dp_ref[:, G_BA0:G_INP] = dba_ref[...].astype(bf16)

    sd = jax.ShapeDtypeStruct
    return pl.pallas_call(
        body, name="gdn_conv_bwd", grid=(nb,),
        in_specs=[pl.BlockSpec((tm, G_CONV), lambda i: (i, 0)),
                  pl.BlockSpec((8, G_CONV), lambda i: (jnp.maximum(i * nb8 - 1, 0), 0)),
                  pl.BlockSpec((CONV_K, G_CONV), lambda i: (0, 0)),
                  pl.BlockSpec((tm, G_CONV), lambda i: (i, 0)),
                  pl.BlockSpec((8, G_CONV), lambda i: (jnp.minimum((i + 1) * nb8, S // 8 - 1), 0)),
                  pl.BlockSpec((tm, 2048), lambda i: (i, 0)), pl.BlockSpec((tm, 128), lambda i: (i, 0))],
        out_specs=[pl.BlockSpec((tm, G_INP), lambda i: (i, 0)), pl.BlockSpec((8, G_CONV), lambda i: (0, 0))],
        out_shape=[sd((S, G_INP), bf16), sd((8, G_CONV), f32)],
        compiler_params=_params(("arbitrary",)),
    )(proj, proj, conv_w, dcv, dcv, dz, dba)


def _half_mean(t, lo_half):
    m0 = jnp.sum(jnp.where(lo_half, t, 0.0), axis=-1, keepdims=True)
    m1 = jnp.sum(jnp.where(lo_half, 0.0, t), axis=-1, keepdims=True)
    return jnp.where(lo_half, m0, m1) * (1.0 / F_HD)


def _split3(c):
    hi = c.astype(bf16).astype(f32)
    mid = (c - hi).astype(bf16).astype(f32)
    lo = (c - hi - mid).astype(bf16).astype(f32)
    return hi, mid, lo


def fox_pre(proj, fbias, qw2, kw2):
    S = proj.shape[0]
    tm = 256

    def body(q_ref, k_ref, v_ref, f_ref, fb_ref, qw_ref, kw_ref, qa_ref, ka_ref, vb_ref, carry):
        @pl.when(pl.program_id(0) == 0)
        def _():
            carry[...] = jnp.zeros_like(carry)

        logf = -_softplus(-(f_ref[:, 0:16] + fb_ref[...]))
        ltri = jnp.where(_iota((tm, tm), 1) <= _iota((tm, tm), 0), 1.0, 0.0).astype(f32)
        cum = _nn(ltri, logf, HI) + carry[0:1, :]
        carry[0:1, :] = cum[tm - 1:tm, :]
        lane = _iota((tm, 128), 1)
        lo_half = lane < F_HD
        for p in range(F_H // 2):
            ps = slice(p * 128, (p + 1) * 128)
            for src, w_ref, dst, is_q in ((q_ref, qw_ref, qa_ref, True), (k_ref, kw_ref, ka_ref, False)):
                x = src[:, ps]
                xn = x * lax.rsqrt(_half_mean(x * x, lo_half) + EPS) * w_ref[...]
                if is_q:
                    xn = xn * (F_HD ** -0.5)
                for e in range(2):
                    h = 2 * p + e
                    base = xn if e == 0 else pltpu.roll(xn, F_HD, 1)
                    hi, mid, lo = _split3(cum[:, h:h + 1])
                    pieces = jnp.where(lane == 64, hi, 0.0) + jnp.where(lane == 65, mid, 0.0) + jnp.where(lane == 66, lo, 0.0)
                    if is_q:
                        ext = pieces + jnp.where((lane >= 67) & (lane <= 69), 1.0, 0.0)
                    else:
                        ext = jnp.where((lane >= 64) & (lane <= 66), 1.0, 0.0) - pltpu.roll(pieces, 3, 1)
                    dst[:, h * 128:(h + 1) * 128] = jnp.where(lo_half, base, ext).astype(bf16)
        one = jnp.where(lane == F_HD, 1.0, 0.0)
        for p in range(F_H // 2):
            vv = v_ref[:, p * 128:(p + 1) * 128]
            vb_ref[:, (2 * p) * 128:(2 * p + 1) * 128] = jnp.where(lo_half, vv, one).astype(bf16)
            vb_ref[:, (2 * p + 1) * 128:(2 * p + 2) * 128] = jnp.where(lo_half, pltpu.roll(vv, F_HD, 1), one).astype(bf16)

    t1 = lambda c: pl.BlockSpec((tm, 1024), lambda i: (i, c))
    vec = lambda n: pl.BlockSpec((1, n), lambda i: (0, 0))
    sd = jax.ShapeDtypeStruct
    return pl.pallas_call(
        body, name="fox_pre", grid=(S // tm,),
        in_specs=[t1(0), t1(1), t1(2), pl.BlockSpec((tm, 128), lambda i: (i, F_F0 // 128)), vec(16), vec(128), vec(128)],
        out_specs=[pl.BlockSpec((tm, 2048), lambda i: (i, 0))] * 3,
        out_shape=[sd((S, 2048), bf16)] * 3,
        scratch_shapes=[pltpu.VMEM((8, 16), f32)],
        compiler_params=_params(("arbitrary",)),
    )(proj, proj, proj, proj, fbias, qw2, kw2)


FTQ = 512
FHS = 4


def fox_attn(qa, ka, v, comm=None):
    S = qa.shape[0]
    nq = S // FTQ

    live = [(i, j) for i in range(nq) for j in range(i + 1)]
    qi_tab = jnp.asarray([i for i, _ in live], jnp.int32)
    kj_tab = jnp.asarray([j for _, j in live], jnp.int32)

    def body(qi_ref, kj_ref, q_ref, k_ref, v_ref, o_ref, lse_ref, m_scr, acc_scr):
        t = pl.program_id(1)
        i, j = qi_ref[t], kj_ref[t]

        @pl.when(j == 0)
        def _():
            m_scr[...] = jnp.full_like(m_scr, NEG)
            acc_scr[...] = jnp.zeros_like(acc_scr)

        def step(diagonal):
            for e in range(FHS):
                es = slice(e * 128, (e + 1) * 128)
                s = _nt(q_ref[:, es], k_ref[:, es])
                if diagonal:
                    s = jnp.where(_iota((FTQ, FTQ), 0) >= _iota((FTQ, FTQ), 1), s, NEG)
                m_old = m_scr[e]
                m_new = jnp.maximum(m_old, jnp.max(s, axis=-1, keepdims=True))
                p = jnp.exp(s - m_new[:, 0:1])
                acc_scr[e] = acc_scr[e] * jnp.exp(m_old - m_new) + _nn(p.astype(bf16), v_ref[:, es])
                m_scr[e] = m_new

        pl.when(j < i)(functools.partial(step, False))

        @pl.when(j == i)
        def _():
            step(True)
            for e in range(FHS):
                vs = slice(e * F_HD, (e + 1) * F_HD)
                acc = acc_scr[e]
                l = acc[:, F_HD:F_HD + 1]
                o_ref[:, vs] = acc[:, 0:F_HD] / l
                lse_ref[:, vs] = m_scr[e][:, 0:F_HD] + jnp.log(l)

    sd = jax.ShapeDtypeStruct
    qo = pl.BlockSpec((FTQ, F_HD * FHS), lambda p, t, qi, kj: (qi[t], p))
    kv = pl.BlockSpec((FTQ, 128 * FHS), lambda p, t, qi, kj: (kj[t], p))
    return _call(
        body, name="fox_attn", grid=(F_H // FHS, len(live)),
        in_specs=[pl.BlockSpec((FTQ, 128 * FHS), lambda p, t, qi, kj: (qi[t], p)), kv, kv],
        out_specs=[qo, qo],
        out_shape=[sd((S, 1024), f32), sd((S, 1024), f32)],
        scratch_shapes=[pltpu.VMEM((FHS, FTQ, 128), f32), pltpu.VMEM((FHS, FTQ, 128), f32)],
        sem=("parallel", "arbitrary"), args=(qa, ka, v), comm=comm, prefetch=(qi_tab, kj_tab))


def fox_attn_bwd(qa, ka, v, do, lse, delta, comm=None):
    S = qa.shape[0]
    nq = S // FTQ

    live = [(j, i) for j in range(nq) for i in range(j, nq)]
    kj_tab = jnp.asarray([j for j, _ in live], jnp.int32)
    qi_tab = jnp.asarray([i for _, i in live], jnp.int32)

    def body(kj_ref, qi_ref, q_ref, k_ref, v_ref, do_ref, lse_ref, dl_ref, dq_ref, dk_ref, dv_ref, dk_scr, dv_scr):
        t = pl.program_id(1)
        j, i = kj_ref[t], qi_ref[t]

        @pl.when(t == 0)
        def _():
            dq_ref[...] = jnp.zeros_like(dq_ref)

        @pl.when(i == j)
        def _():
            dk_scr[...] = jnp.zeros_like(dk_scr)
            dv_scr[...] = jnp.zeros_like(dv_scr)

        def step(diagonal):
            rows = pl.ds(pl.multiple_of(i * FTQ, FTQ), FTQ)
            for e in range(2):
                es, vs = slice(e * 128, (e + 1) * 128), slice(e * F_HD, (e + 1) * F_HD)
                qe, ke = q_ref[:, es], k_ref[:, es]
                dob = do_ref[:, vs].astype(bf16)
                s = _nt(qe, ke)
                if diagonal:
                    s = jnp.where(_iota((FTQ, FTQ), 0) >= _iota((FTQ, FTQ), 1), s, NEG)
                p = jnp.exp(s - lse_ref[:, e * F_HD:e * F_HD + 1])
                ds = p * (_nt(dob, v_ref[:, e * 128:e * 128 + F_HD]) - dl_ref[:, e * F_HD:e * F_HD + 1])
                dsb = ds.astype(bf16)
                dv_scr[e] += _tn(dob, p.astype(bf16))
                dk_scr[e] += _tn(qe, dsb)
                dq_ref[rows, es] += _nn(dsb, ke)

        pl.when(i > j)(functools.partial(step, False))
        pl.when(i == j)(functools.partial(step, True))

        @pl.when(i == nq - 1)
        def _():
            for e in range(2):
                dk_ref[:, e * 128:(e + 1) * 128] = dk_scr[e].T
                dv_ref[:, e * F_HD:(e + 1) * F_HD] = dv_scr[e].T

    sd = jax.ShapeDtypeStruct
    qi = lambda w: pl.BlockSpec((FTQ, w), lambda p, t, kj_, qi_: (qi_[t], p))
    kj = lambda w: pl.BlockSpec((FTQ, w), lambda p, t, kj_, qi_: (kj_[t], p))
    return _call(
        body, name="fox_attn_bwd", grid=(F_H // 2, len(live)),
        in_specs=[qi(256), kj(256), kj(256), qi(128), qi(128), qi(128)],
        out_specs=[pl.BlockSpec((S, 256), lambda p, t, kj_, qi_: (0, p)), kj(256), kj(128)],
        out_shape=[sd((S, 2048), f32), sd((S, 2048), f32), sd((S, 1024), f32)],
        scratch_shapes=[pltpu.VMEM((2, 128, FTQ), f32), pltpu.VMEM((2, F_HD, FTQ), f32)],
        sem=("parallel", "arbitrary"), args=(qa, ka, v, do, lse, delta), comm=comm, prefetch=(kj_tab, qi_tab))


def fox_gate(o, proj):
    S = o.shape[0]
    tm = 512

    def body(o_ref, z_ref, o2_ref):
        o2_ref[...] = (o_ref[...] * _silu(z_ref[...])).astype(bf16)

    t = pl.BlockSpec((tm, 1024), lambda i: (i, 0))
    return pl.pallas_call(
        body, name="fox_gate", grid=(S // tm,),
        in_specs=[t, pl.BlockSpec((tm, 1024), lambda i: (i, 3))], out_specs=t,
        out_shape=jax.ShapeDtypeStruct((S, 1024), bf16),
        compiler_params=_params(("parallel",)),
    )(o, proj)


def fox_gate_bwd(do2, o, proj):
    S = o.shape[0]
    tm = 256

    def body(d_ref, o_ref, z_ref, do_ref, dz_ref, dl_ref):
        lo_half = _iota((tm, 128), 1) < F_HD
        for p in range(F_H // 2):
            ps = slice(p * 128, (p + 1) * 128)
            d2, ov, z = d_ref[:, ps], o_ref[:, ps], z_ref[:, ps]
            dov = d2 * _silu(z)
            do_ref[:, ps] = dov
            dz_ref[:, ps] = (d2 * ov * _dsilu(z)).astype(bf16)
            dl_ref[:, ps] = _half_mean(dov * ov, lo_half) * float(F_HD)

    t = pl.BlockSpec((tm, 1024), lambda i: (i, 0))
    sd = jax.ShapeDtypeStruct
    return pl.pallas_call(
        body, name="fox_gate_bwd", grid=(S // tm,),
        in_specs=[t, t, pl.BlockSpec((tm, 1024), lambda i: (i, 3))], out_specs=[t, t, t],
        out_shape=[sd((S, 1024), f32), sd((S, 1024), bf16), sd((S, 1024), f32)],
        compiler_params=_params(("parallel",)),
    )(do2, o, proj)


def fox_pre_bwd(proj, fbias, qw2, kw2, dqa, dka, dv, dz):
    S = proj.shape[0]
    tm = 256
    nb = S // tm

    def body(q_ref, k_ref, f_ref, fb_ref, qw_ref, kw_ref, dqa_ref, dka_ref, dv_ref, dz_ref, dp_ref, st_ref, carry):
        i = pl.program_id(0)

        @pl.when(i == 0)
        def _():
            carry[...] = jnp.zeros_like(carry)

        lane = _iota((tm, 128), 1)
        lo_half = lane < F_HD
        lane16 = _iota((tm, 16), 1)
        dcum = jnp.zeros((tm, 16), f32)
        dws = []
        for src, w_ref, dsrc, is_q, col0 in ((q_ref, qw_ref, dqa_ref, True, 0), (k_ref, kw_ref, dka_ref, False, 1024)):
            dw = jnp.zeros((1, 128), f32)
            for p in range(F_H // 2):
                ps = slice(p * 128, (p + 1) * 128)
                x = src[:, ps]
                r = lax.rsqrt(_half_mean(x * x, lo_half) + EPS)
                xh = x * r
                d0 = dsrc[:, (2 * p) * 128:(2 * p + 1) * 128]
                d1 = dsrc[:, (2 * p + 1) * 128:(2 * p + 2) * 128]
                dy = jnp.where(lo_half, d0, pltpu.roll(d1, F_HD, 1))
                if is_q:
                    dy = dy * (F_HD ** -0.5)
                dxh = dy * w_ref[...]
                dw = dw + jnp.sum(dy * xh, axis=0, keepdims=True)
                dp_ref[:, col0 + p * 128:col0 + (p + 1) * 128] = (r * (dxh - xh * _half_mean(dxh * xh, lo_half))).astype(bf16)
                for e, de in ((0, d0), (1, d1)):
                    col = de[:, 64:65] if is_q else -de[:, 67:68]
                    dcum = dcum + jnp.where(lane16 == 2 * p + e, col, 0.0)
            dws.append(dw)
        dp_ref[:, 2048:3072] = dv_ref[...].astype(bf16)
        dp_ref[:, 3072:4096] = dz_ref[...]
        utri = jnp.where(_iota((tm, tm), 1) >= _iota((tm, tm), 0), 1.0, 0.0).astype(f32)
        dlogf = _nn(utri, dcum, HI) + carry[0:1, :]
        carry[0:1, :] = dlogf[0:1, :]
        fl = f_ref[:, 0:16] + fb_ref[...]
        df = dlogf * _sigmoid(-fl)
        place = jnp.where(_iota((16, 128), 1) == _iota((16, 128), 0), 1.0, 0.0).astype(f32)
        dfw = _nn(df, place, HI)
        dp_ref[:, F_F0:F_INP] = dfw.astype(bf16)
        upd = jnp.concatenate(dws + [jnp.sum(dfw, axis=0, keepdims=True), jnp.zeros((5, 128), f32)], axis=0)

        @pl.when(i == 0)
        def _():
            st_ref[...] = upd

        @pl.when(i > 0)
        def _():
            st_ref[...] += upd

    rev = lambda w, c: pl.BlockSpec((tm, w), lambda i: (nb - 1 - i, c))
    vec = lambda n: pl.BlockSpec((1, n), lambda i: (0, 0))
    sd = jax.ShapeDtypeStruct
    return pl.pallas_call(
        body, name="fox_pre_bwd", grid=(nb,),
        in_specs=[rev(1024, 0), rev(1024, 1), rev(128, F_F0 // 128), vec(16), vec(128), vec(128),
                  rev(2048, 0), rev(2048, 0), rev(1024, 0), rev(1024, 0)],
        out_specs=[rev(F_INP, 0), pl.BlockSpec((8, 128), lambda i: (0, 0))],
        out_shape=[sd((S, F_INP), bf16), sd((8, 128), f32)],
        scratch_shapes=[pltpu.VMEM((8, 16), f32)],
        compiler_params=_params(("arbitrary",)),
    )(proj, proj, proj, fbias, qw2, kw2, dqa, dka, dv, dz)


def _me():
    return lax.axis_index("x"), lax.axis_index("y"), lax.axis_index("c")


def _other_chips(x, y):
    return [(1 - x, y), (x, 1 - y), (1 - x, 1 - y)]


def ag_small(xs):
    m_per, n = xs.shape

    def body(x_ref, out_ref, send_sems, recv_sems, local_sem):
        x, y, c = _me()
        me, sibling = (x, y, c), (x, y, 1 - c)
        chips = _other_chips(x, y)

        def rows(px, py, pc):
            return out_ref.at[pl.ds((4 * px + 2 * py + pc) * m_per, m_per), :]

        def copy(k, block, to, src=None):
            return pltpu.make_async_remote_copy(
                src_ref=rows(*block) if src is None else src, dst_ref=rows(*block),
                send_sem=send_sems.at[k], recv_sem=recv_sems.at[k], device_id=to, device_id_type=MESH)

        mine = pltpu.make_async_copy(x_ref, rows(*me), local_sem)
        mine.start()
        first = [copy(0, me, sibling, src=x_ref)]
        first += [copy(1 + j, me, (*chip, c), src=x_ref) for j, chip in enumerate(chips)]
        for cp in first:
            cp.start()
        passed = [copy(4 + j, (*chip, c), sibling) for j, chip in enumerate(chips)]
        for j, chip in enumerate(chips):
            copy(1 + j, (*chip, c), me).wait_recv()
            passed[j].start()
        copy(0, sibling, me).wait_recv()
        for j, chip in enumerate(chips):
            copy(4 + j, (*chip, 1 - c), me).wait_recv()
        for cp in first + passed:
            cp.wait_send()
        mine.wait()

    return pl.pallas_call(
        body, name="ag_small",
        out_shape=jax.ShapeDtypeStruct((8 * m_per, n), xs.dtype),
        in_specs=[pl.BlockSpec(memory_space=pltpu.VMEM)], out_specs=pl.BlockSpec(memory_space=pltpu.VMEM),
        scratch_shapes=[pltpu.SemaphoreType.DMA((7,)), pltpu.SemaphoreType.DMA((7,)), pltpu.SemaphoreType.DMA],
        compiler_params=pltpu.CompilerParams(vmem_limit_bytes=VMEM_LIMIT),
    )(xs)


_ANY = pl.BlockSpec(memory_space=pl.ANY)


def ag_chips(arrs):
    n = len(arrs)
    assert all(a.shape[0] == 2 for a in arrs)

    def body(*refs):
        ins, outs = refs[:n], refs[n:2 * n]
        send_sems, recv_sems, fwd_send, fwd_recv, local_sems = refs[2 * n:]
        x, y, c = _me()
        me = 2 * x + y
        chips = _other_chips(x, y)
        started = []
        for a in range(n):
            cp = pltpu.make_async_copy(ins[a], outs[a].at[me], local_sems.at[a])
            cp.start()
            started.append(cp)
        sends = []
        for a in range(n):
            for j, (px, py) in enumerate(chips):
                r = pltpu.make_async_remote_copy(
                    src_ref=ins[a].at[c], dst_ref=outs[a].at[me, c], send_sem=send_sems.at[3 * a + j],
                    recv_sem=recv_sems.at[3 * a + j], device_id=(px, py, c), device_id_type=MESH)
                r.start()
                sends.append(r)
        for a in range(n):
            for j, (px, py) in enumerate(chips):
                got = outs[a].at[2 * px + py, c]
                pltpu.make_async_remote_copy(
                    src_ref=ins[a].at[c], dst_ref=got, send_sem=send_sems.at[3 * a + j],
                    recv_sem=recv_sems.at[3 * a + j], device_id=(px, py, c), device_id_type=MESH).wait_recv()
                f = pltpu.make_async_remote_copy(
                    src_ref=got, dst_ref=got, send_sem=fwd_send.at[3 * a + j], recv_sem=fwd_recv.at[3 * a + j],
                    device_id=(x, y, 1 - c), device_id_type=MESH)
                f.start()
                sends.append(f)
        for a in range(n):
            for j, (px, py) in enumerate(chips):
                theirs = outs[a].at[2 * px + py, 1 - c]
                pltpu.make_async_remote_copy(
                    src_ref=theirs, dst_ref=theirs, send_sem=fwd_send.at[3 * a + j], recv_sem=fwd_recv.at[3 * a + j],
                    device_id=(x, y, 1 - c), device_id_type=MESH).wait_recv()
        for r in sends:
            r.wait_send()
        for cp in started:
            cp.wait()

    sems = pltpu.SemaphoreType.DMA((3 * n,))
    return pl.pallas_call(
        body, name="ag_chips",
        out_shape=[jax.ShapeDtypeStruct((4,) + a.shape, a.dtype) for a in arrs],
        in_specs=[_ANY] * n, out_specs=[_ANY] * n,
        scratch_shapes=[sems, sems, sems, sems, pltpu.SemaphoreType.DMA((n,))],
    )(*arrs)


def _ag_comm(arrs):
    n = len(arrs)

    def copies(ins, outs, sems, inbound):
        send_sems, recv_sems, local_sems = sems
        x, y, c = _me()
        me = 2 * x + y
        local = [pltpu.make_async_copy(ins[a], outs[a].at[me], local_sems.at[a]) for a in range(n)]
        out_cp, in_cp = [], []
        for a in range(n):
            for j, (px, py) in enumerate(_other_chips(x, y)):
                mk = functools.partial(pltpu.make_async_remote_copy, src_ref=ins[a], send_sem=send_sems.at[3 * a + j],
                                       recv_sem=recv_sems.at[3 * a + j], device_id=(px, py, c), device_id_type=MESH)
                out_cp.append(mk(dst_ref=outs[a].at[me]))
                if inbound:
                    in_cp.append(mk(dst_ref=outs[a].at[2 * px + py]))
        return local, out_cp, in_cp

    def start(ins, outs, sems):
        local, out_cp, _ = copies(ins, outs, sems, False)
        for cp in local + out_cp:
            cp.start()

    def wait(ins, outs, sems):
        local, out_cp, in_cp = copies(ins, outs, sems, True)
        for cp in in_cp:
            cp.wait_recv()
        for cp in out_cp:
            cp.wait_send()
        for cp in local:
            cp.wait()

    sems = [pltpu.SemaphoreType.DMA((3 * n,)), pltpu.SemaphoreType.DMA((3 * n,)), pltpu.SemaphoreType.DMA((n,))]
    return _Comm(arrs, [jax.ShapeDtypeStruct((4,) + a.shape, a.dtype) for a in arrs], sems, start, wait)


def _rs_comm(gs):
    n = len(gs)
    flips = [(fx, fy, fc) for fx in (0, 1) for fy in (0, 1) for fc in (0, 1)][1:]

    def copies(ins, outs, sems, inbound):
        send_sems, recv_sems, local_sems = sems
        x, y, c = _me()
        me = 4 * x + 2 * y + c
        local, out_cp, in_cp = [], [], []
        for a in range(n):
            rh = ins[a].shape[1] // 2
            mine = ins[a].at[2 * x + y, pl.ds(c * rh, rh), :]
            local.append(pltpu.make_async_copy(mine, outs[a].at[me], local_sems.at[a]))
            for j, (fx, fy, fc) in enumerate(flips):
                px, py, pc = (1 - x if fx else x), (1 - y if fy else y), (1 - c if fc else c)
                mk = functools.partial(pltpu.make_async_remote_copy, send_sem=send_sems.at[7 * a + j],
                                       recv_sem=recv_sems.at[7 * a + j], device_id=(px, py, pc), device_id_type=MESH)
                out_cp.append(mk(src_ref=ins[a].at[2 * px + py, pl.ds(pc * rh, rh), :], dst_ref=outs[a].at[me]))
                if inbound:
                    in_cp.append(mk(src_ref=mine, dst_ref=outs[a].at[4 * px + 2 * py + pc]))
        return local, out_cp, in_cp

    def start(ins, outs, sems):
        local, out_cp, _ = copies(ins, outs, sems, False)
        for cp in local + out_cp:
            cp.start()

    def wait(ins, outs, sems):
        local, out_cp, in_cp = copies(ins, outs, sems, True)
        for cp in in_cp:
            cp.wait_recv()
        for cp in out_cp:
            cp.wait_send()
        for cp in local:
            cp.wait()

    sems = [pltpu.SemaphoreType.DMA((7 * n,)), pltpu.SemaphoreType.DMA((7 * n,)), pltpu.SemaphoreType.DMA((n,))]
    return _Comm(gs, [jax.ShapeDtypeStruct((8, g.shape[1] // 2, g.shape[2]), g.dtype) for g in gs], sems, start, wait)


def sum_leading(q, name):
    K, R, C = q.shape
    tr = _pick(R, (256, 128, 64, 32, 16, 8))

    def body(q_ref, o_ref):
        acc = q_ref[0]
        for k in range(1, K):
            acc = acc + q_ref[k]
        o_ref[...] = acc

    return pl.pallas_call(
        body, name=name, grid=(R // tr,),
        in_specs=[pl.BlockSpec((K, tr, C), lambda i: (0, i, 0))], out_specs=pl.BlockSpec((tr, C), lambda i: (i, 0)),
        out_shape=jax.ShapeDtypeStruct((R, C), f32),
        compiler_params=_params(("parallel",)),
    )(q)


def rs_sum_devices(q, cidx):
    K, R, C = q.shape
    tr = _pick(R, (256, 128))

    def body(c_ref, q_ref, o_ref):
        acc = q_ref[0].astype(f32)
        for k in range(1, K):
            acc = acc + q_ref[k].astype(f32)
        o_ref[0] = acc

    return pl.pallas_call(
        body, name="rs_sum_devices",
        grid_spec=pltpu.PrefetchScalarGridSpec(
            num_scalar_prefetch=1, grid=(R // tr,),
            in_specs=[pl.BlockSpec((K, tr, C), lambda i, c_ref: (0, i, 0))],
            out_specs=pl.BlockSpec((1, tr, C), lambda i, c_ref: (c_ref[0], i, 0))),
        out_shape=jax.ShapeDtypeStruct((2, R, C), f32),
        compiler_params=_params(("parallel",)),
    )(cidx, q)


def rs_share_halves(rs):
    n = len(rs)

    def body(*refs):
        bufs = refs[n:2 * n]
        send_sems, recv_sems = refs[2 * n:]
        x, y, c = _me()
        cps = []
        for a in range(n):
            cp = pltpu.make_async_remote_copy(
                src_ref=bufs[a].at[c], dst_ref=bufs[a].at[c], send_sem=send_sems.at[a], recv_sem=recv_sems.at[a],
                device_id=(x, y, 1 - c), device_id_type=MESH)
            cp.start()
            cps.append(cp)
        for a, cp in enumerate(cps):
            pltpu.make_async_remote_copy(
                src_ref=bufs[a].at[c], dst_ref=bufs[a].at[1 - c], send_sem=send_sems.at[a], recv_sem=recv_sems.at[a],
                device_id=(x, y, 1 - c), device_id_type=MESH).wait_recv()
            cp.wait_send()

    return pl.pallas_call(
        body, name="rs_share_halves",
        out_shape=[jax.ShapeDtypeStruct(r.shape, r.dtype) for r in rs],
        in_specs=[_ANY] * n, out_specs=[_ANY] * n, input_output_aliases={a: a for a in range(n)},
        scratch_shapes=[pltpu.SemaphoreType.DMA((n,)), pltpu.SemaphoreType.DMA((n,))],
    )(*rs)


def ada_mod(c_all, ada_w):
    L, _, n = ada_w.shape

    def body(c_ref, w_ref, o_ref):
        o_ref[0] = _nn(_silu(c_ref[...]), w_ref[0], HI)

    return pl.pallas_call(
        body, name="ada_mod", grid=(L,),
        in_specs=[pl.BlockSpec((8, D), lambda l: (0, 0)), pl.BlockSpec((1, D, n), lambda l: (l, 0, 0))],
        out_specs=pl.BlockSpec((1, 8, n), lambda l: (l, 0, 0)),
        out_shape=jax.ShapeDtypeStruct((L, 8, n), f32),
        compiler_params=_params(("parallel",)),
    )(c_all, ada_w)


def ada_w_grad(c_all, dmod):
    L, _, n = dmod.shape

    def body(c_ref, d_ref, o_ref):
        o_ref[0] = _tn(_silu(c_ref[...]), d_ref[0], HI)

    return pl.pallas_call(
        body, name="ada_w_grad", grid=(L,),
        in_specs=[pl.BlockSpec((8, D), lambda l: (0, 0)), pl.BlockSpec((1, 8, n), lambda l: (l, 0, 0))],
        out_specs=pl.BlockSpec((1, D, n), lambda l: (l, 0, 0)),
        out_shape=jax.ShapeDtypeStruct((L, D, n), f32),
        compiler_params=_params(("parallel",)),
    )(c_all, dmod)


def adamw(w, g, m, v, name):
    shp = w.shape
    two = lambda a: a.reshape(-1, shp[-1])
    R, C = two(w).shape
    tr = _pick(R, (256, 128, 64, 32, 16, 8))
    bc1, bc2 = 1.0 - B1 ** STEP, 1.0 - B2 ** STEP

    def body(w_ref, g_ref, m_ref, v_ref, d_ref, mo_ref, vo_ref):
        gv = g_ref[...]
        mn = B1 * m_ref[...] + (1.0 - B1) * gv
        vn = B2 * v_ref[...] + (1.0 - B2) * (gv * gv)
        d_ref[...] = -LR * ((mn / bc1) / (jnp.sqrt(vn / bc2) + AEPS) + WD * w_ref[...])
        mo_ref[...] = mn
        vo_ref[...] = vn

    t = pl.BlockSpec((tr, C), lambda i: (i, 0))
    outs = pl.pallas_call(
        body, name=name, grid=(R // tr,),
        in_specs=[t] * 4, out_specs=[t] * 3, out_shape=[jax.ShapeDtypeStruct((R, C), f32)] * 3,
        compiler_params=_params(("parallel",)),
    )(two(w), two(g), two(m), two(v))
    return [o.reshape(shp) for o in outs]


def _pack(arrs):
    parts, offs, r0 = [], [], 0
    for a in arrs:
        n = a.size
        rows = -(-n // 1024) * 8
        parts.append(jnp.pad(a.reshape(-1), (0, rows * 128 - n)).reshape(rows, 128))
        offs.append((r0, rows))
        r0 += rows
    return jnp.concatenate(parts, axis=0), offs


def _unpack(buf, offs, shapes):
    out = []
    for (r0, rows), shp in zip(offs, shapes):
        n = 1
        for d in shp:
            n *= d
        out.append(buf[..., r0:r0 + rows, :].reshape(buf.shape[:-2] + (rows * 128,))[..., :n].reshape(buf.shape[:-2] + tuple(shp)))
    return out


def kernel(x, c, norm_w, ada_w, ada_b, a_w_in, a_conv_w, a_A_log, a_dt_bias, a_norm_w, a_w_out, b_w_in, b_f_bias, b_qn_w, b_kn_w, b_w_out, final_norm_w, loss_target, m_norm_w, m_ada_w, m_ada_b, m_a_w_in, m_a_conv_w, m_a_A_log, m_a_dt_bias, m_a_norm_w, m_a_w_out, m_b_w_in, m_b_f_bias, m_b_qn_w, m_b_kn_w, m_b_w_out, m_final_norm_w, v_norm_w, v_ada_w, v_ada_b, v_a_w_in, v_a_conv_w, v_a_A_log, v_a_dt_bias, v_a_norm_w, v_a_w_out, v_b_w_in, v_b_f_bias, v_b_qn_w, v_b_kn_w, v_b_w_out, v_final_norm_w):
    weights = dict(norm_w=norm_w, ada_w=ada_w, ada_b=ada_b, a_w_in=a_w_in, a_conv_w=a_conv_w, a_A_log=a_A_log,
                   a_dt_bias=a_dt_bias, a_norm_w=a_norm_w, a_w_out=a_w_out, b_w_in=b_w_in, b_f_bias=b_f_bias,
                   b_qn_w=b_qn_w, b_kn_w=b_kn_w, b_w_out=b_w_out, final_norm_w=final_norm_w)
    m_in = dict(norm_w=m_norm_w, ada_w=m_ada_w, ada_b=m_ada_b, a_w_in=m_a_w_in, a_conv_w=m_a_conv_w, a_A_log=m_a_A_log,
                a_dt_bias=m_a_dt_bias, a_norm_w=m_a_norm_w, a_w_out=m_a_w_out, b_w_in=m_b_w_in, b_f_bias=m_b_f_bias,
                b_qn_w=m_b_qn_w, b_kn_w=m_b_kn_w, b_w_out=m_b_w_out, final_norm_w=m_final_norm_w)
    v_in = dict(norm_w=v_norm_w, ada_w=v_ada_w, ada_b=v_ada_b, a_w_in=v_a_w_in, a_conv_w=v_a_conv_w, a_A_log=v_a_A_log,
                a_dt_bias=v_a_dt_bias, a_norm_w=v_a_norm_w, a_w_out=v_a_w_out, b_w_in=v_b_w_in, b_f_bias=v_b_f_bias,
                b_qn_w=v_b_qn_w, b_kn_w=v_b_kn_w, b_w_out=v_b_w_out, final_norm_w=v_final_norm_w)
    xi, yi, ci = _me()
    me_b, me_k = 4 * xi + 2 * yi + ci, 2 * xi + yi
    cidx = ci.astype(jnp.int32).reshape(1)
    S = x.shape[1]
    depth, n_a, n_b = norm_w.shape[0], a_w_in.shape[0], b_w_in.shape[0]
    x0, tgt = x.reshape(S, D), loss_target.reshape(S, D)

    c_all = ag_small(jnp.pad(c, ((0, 7), (0, 0)))).reshape(8, 8, D)[:, 0]
    nloc = ada_w.shape[2]
    parts = ag_small(ada_mod(c_all, ada_w).reshape(depth * 8, nloc)).reshape(4, 2, depth, 8, nloc)[:, 0]
    mine = lax.dynamic_index_in_dim(parts, me_b, axis=2, keepdims=False)
    mod = jnp.transpose(mine, (1, 0, 2)).reshape(depth, 4 * nloc) + ada_b
    shift, scale, gate = (mod[:, k * D:(k + 1) * D] for k in range(3))

    w_loc = [(a_w_in[i // 2] if i % 2 == 0 else b_w_in[i // 2]).astype(bf16) for i in range(depth)]
    wo_loc = [(a_w_out[i // 2] if i % 2 == 0 else b_w_out[i // 2]).astype(bf16) for i in range(depth)]
    pad_in = [(G_INP - G_IN) if i % 2 == 0 else (F_INP - F_IN) for i in range(depth)]
    halves = lambda w: w.reshape((2, w.shape[0] // 2) + w.shape[1:])

    def cols_in_place(g_in, pad):
        w = jnp.transpose(g_in, (1, 0, 2)).reshape(g_in.shape[1], -1)
        return jnp.pad(w, ((0, 0), (0, pad)))

    g_in0, g_conv = ag_chips([halves(w_loc[0]), a_conv_w])
    w_in_full = [cols_in_place(g_in0.reshape((4,) + w_loc[0].shape), pad_in[0])]
    w_out_full = []
    conv = [jnp.transpose(g_conv[:, l], (1, 0, 2)).reshape(CONV_K, -1) for l in range(n_a)]
    qw2 = [_row(jnp.tile(b_qn_w[l], 2)) for l in range(n_b)]
    kw2 = [_row(jnp.tile(b_kn_w[l], 2)) for l in range(n_b)]

    saved, xc = [], x0
    for i in range(depth):
        l = i // 2
        nxt = _ag_comm([w_loc[i + 1], wo_loc[i + 1]]) if i + 1 < depth else None
        h = ln_mod(xc, _row(norm_w[i]), _row(scale[i]), _row(shift[i]))
        name = "mm_a_in" if i % 2 == 0 else "mm_b_in"
        if i == 0:
            proj, got = matmul(h, w_in_full[0], "nn", name, comm=_ag_comm([wo_loc[0]]))
            w_out_full.append(got[0].reshape(-1, D))
        else:
            proj = matmul(h, w_in_full[i], "nn", name)
        if i % 2 == 0:
            pre = gdn_pre(proj, conv[l], _row(a_A_log[l]), _row(a_dt_bias[l]))
            res, got = gdn_fwd(*pre, comm=nxt)
            o2 = gdn_onorm(res[0], proj, _row(a_norm_w[l]))
            y, xn = out_proj(o2, w_out_full[i], xc, _row(gate[i]), "out_proj_a")
        else:
            pre = fox_pre(proj, _row(b_f_bias[l]), qw2[l], kw2[l])
            res, got = fox_attn(*pre, comm=nxt)
            o2 = fox_gate(res[0], proj)
            y, xn = out_proj(o2, w_out_full[i], xc, _row(gate[i]), "out_proj_b")
        saved.append((xc, h, proj, o2, y, pre, res))
        if nxt is not None:
            w_in_full.append(cols_in_place(got[0], pad_in[i + 1]))
            w_out_full.append(got[1].reshape(-1, D))
        xc = xn
    dx, st_f = final_loss(xc, _row(final_norm_w), tgt)

    d_norm, d_mod = [None] * depth, [None] * depth
    d_conv, d_alog, d_dtb, d_anw = [None] * n_a, [None] * n_a, [None] * n_a, [None] * n_a
    d_fb, d_qn, d_kn = [None] * n_b, [None] * n_b, [None] * n_b
    ex_in, ex_out, pend_in = [None] * depth, [None] * depth, None
    for i in reversed(range(depth)):
        l = i // 2
        xin, h, proj, o2, y, pre, res = saved[i]
        ab = "a" if i % 2 == 0 else "b"
        dy, st_g = gate_bwd(dx, y, _row(gate[i]))
        do2 = matmul(dy, w_out_full[i], "nt", f"mm_{ab}_do2")
        d_out = matmul(o2, dy, "tn", f"mm_{ab}_dwo")
        ride = _rs_comm(([] if pend_in is None else [pend_in]) + [d_out.reshape(4, d_out.shape[0] // 4, D).astype(bf16)])
        if i % 2 == 0:
            o, wv, at, tinv, vn, st = res
            do, dz, st_o = gdn_onorm_bwd(do2, o, proj, _row(a_norm_w[l]))
            grads, got = gdn_bwd(do, *pre, wv, at, tinv, vn, st, comm=ride)
            dcv, dba, st_s = gdn_pre_bwd(proj, conv[l], _row(a_A_log[l]), _row(a_dt_bias[l]), *grads)
            dproj, dcw = gdn_conv_bwd(proj, conv[l], dcv, dz, dba)
            d_conv[l], d_alog[l], d_dtb[l], d_anw[l] = dcw[:CONV_K], st_s[0], st_s[1], st_o[0]
        else:
            o, lse = res
            do, dz, delta = fox_gate_bwd(do2, o, proj)
            (dqa, dka, dv), got = fox_attn_bwd(*pre, do, lse, delta, comm=ride)
            dproj, st_b = fox_pre_bwd(proj, _row(b_f_bias[l]), qw2[l], kw2[l], dqa, dka, dv, dz)
            d_fb[l], d_qn[l], d_kn[l] = st_b[2, :F_H], st_b[0, :F_HD] + st_b[0, F_HD:], st_b[1, :F_HD] + st_b[1, F_HD:]
        ex_out[i] = got[-1]
        if pend_in is not None:
            ex_in[i + 1] = got[0]
        d_in = matmul(h, dproj, "tn", f"mm_{ab}_dw")
        cl = w_loc[i].shape[1]
        pend_in = jnp.transpose(d_in[:, :4 * cl].reshape(d_in.shape[0], 4, cl), (1, 0, 2)).astype(bf16)
        if i == 0:
            dh, got = matmul(dproj, w_in_full[i], "nt", f"mm_{ab}_dh", comm=_rs_comm([pend_in]))
            ex_in[0] = got[0]
        else:
            dh = matmul(dproj, w_in_full[i], "nt", f"mm_{ab}_dh")
        dx, st_n = ln_mod_bwd(xin, _row(norm_w[i]), _row(scale[i]), dh, dx)
        d_norm[i] = st_n[0]
        d_mod[i] = jnp.concatenate([st_n[2], st_n[1], st_g[0]])

    small = [jnp.stack(d_norm), jnp.stack(d_mod), jnp.stack(d_conv), jnp.stack(d_alog), jnp.stack(d_dtb), jnp.stack(d_anw),
             jnp.stack(d_fb), jnp.stack(d_qn), jnp.stack(d_kn), st_f[0], jnp.sum(st_f[1]).reshape(1)]
    shapes = [a.shape for a in small]
    buf, offs = _pack(small)
    gathered = ag_small(buf).reshape(8, buf.shape[0], 128)
    tot = _unpack(sum_leading(gathered, "sum_devices"), offs, shapes)
    g_norm, g_adab, g_convf, g_alog, g_dtb, g_anw, g_fb, g_qn, g_kn, g_fin, loss = tot
    dmod_all = _unpack(gathered, offs[1:2], shapes[1:2])[0]
    dmod_loc = lax.dynamic_slice_in_dim(dmod_all, me_k * nloc, nloc, axis=2)
    g_adaw = ada_w_grad(c_all, jnp.transpose(dmod_loc, (1, 0, 2)))
    g_conv_loc = lax.dynamic_slice_in_dim(g_convf, me_k * a_conv_w.shape[2], a_conv_w.shape[2], axis=2)

    flat = [q for i in range(depth) for q in (ex_in[i], ex_out[i])]
    done = rs_share_halves([rs_sum_devices(q, cidx) for q in flat])
    red = [d.reshape(-1, d.shape[-1]) for d in done]
    r_in, r_out = red[0::2], red[1::2]
    grads = dict(norm_w=g_norm, ada_w=g_adaw, ada_b=g_adab, a_w_in=jnp.stack(r_in[0::2]), a_conv_w=g_conv_loc,
                 a_A_log=g_alog, a_dt_bias=g_dtb, a_norm_w=g_anw, a_w_out=jnp.stack(r_out[0::2]),
                 b_w_in=jnp.stack(r_in[1::2]), b_f_bias=g_fb, b_qn_w=g_qn, b_kn_w=g_kn,
                 b_w_out=jnp.stack(r_out[1::2]), final_norm_w=g_fin)
    names = list(weights)
    upd = {n: adamw(weights[n], grads[n], m_in[n], v_in[n], "adamw_" + n) for n in names}
    return (loss.reshape(()), dx.reshape(x.shape), *[grads[n] for n in names], *[upd[n][0] for n in names],
            *[upd[n][1] for n in names], *[upd[n][2] for n in names])
```

```python
import functools

import jax
import jax.numpy as jnp
from jax import lax
from jax.experimental import pallas as pl
from jax.experimental.pallas import tpu as pltpu

f32, bf16 = jnp.float32, jnp.bfloat16
HI = lax.Precision.HIGHEST
MESH = pl.DeviceIdType.MESH

EPS = 1e-6
D = 1024
CHUNK = 64
GQK_H, GV_H, GHD = 8, 16, 128
G_CONV = 4096
G_Z0 = 4096
G_BA0 = 6144
G_IN, G_INP = 6176, 6272
CONV_K = 4
F_H, F_HD = 16, 64
F_W = 1024
F_F0 = 4096
F_IN, F_INP = 4112, 4224
LR, B1, B2, AEPS, WD, STEP = 0.001, 0.9, 0.999, 1e-08, 0.01, 10
NEG = -1e30
VMEM_LIMIT = 56 * 1024 * 1024


def _nn(a, b, prec=None):
    return lax.dot_general(a, b, (((1,), (0,)), ((), ())), preferred_element_type=f32, precision=prec)


def _nt(a, b, prec=None):
    return lax.dot_general(a, b, (((1,), (1,)), ((), ())), preferred_element_type=f32, precision=prec)


def _tn(a, b, prec=None):
    return lax.dot_general(a, b, (((0,), (0,)), ((), ())), preferred_element_type=f32, precision=prec)


def _iota(shape, axis):
    return lax.broadcasted_iota(jnp.int32, shape, axis)


def _sigmoid(x):
    return 0.5 * jnp.tanh(0.5 * x) + 0.5


def _softplus(x):
    return jnp.maximum(x, 0.0) + jnp.log(1.0 + jnp.exp(-jnp.abs(x)))


def _silu(x):
    return x * _sigmoid(x)


def _dsilu(x):
    s = _sigmoid(x)
    return s * (1.0 + x * (1.0 - s))


def _params(sem=None, vmem=VMEM_LIMIT):
    return pltpu.CompilerParams(dimension_semantics=sem, vmem_limit_bytes=vmem)


def _row(v):
    return v.reshape(1, -1)


class _Comm:
    def __init__(self, ins, out_shapes, sems, start, wait):
        self.ins, self.out_shapes, self.sems, self.start, self.wait = list(ins), list(out_shapes), list(sems), start, wait


def _call(body, *, name, grid, in_specs, out_specs, out_shape, scratch_shapes, sem, args, comm=None, prefetch=()):
    n_pf, n_in, n_out, n_s = len(prefetch), len(in_specs), len(out_specs), len(scratch_shapes)
    n_ci, n_co = (len(comm.ins), len(comm.out_shapes)) if comm is not None else (0, 0)

    def wrapped(*refs):
        pf, refs = refs[:n_pf], refs[n_pf:]
        core_in, c_in = refs[:n_in], refs[n_in:n_in + n_ci]
        o0 = n_in + n_ci
        core_out, c_out = refs[o0:o0 + n_out], refs[o0 + n_out:o0 + n_out + n_co]
        s0 = o0 + n_out + n_co
        core_s, c_sem = refs[s0:s0 + n_s], refs[s0 + n_s:]
        if comm is not None:
            first = functools.reduce(jnp.logical_and, [pl.program_id(d) == 0 for d in range(len(grid))])
            pl.when(first)(functools.partial(comm.start, c_in, c_out, c_sem))
        body(*pf, *core_in, *core_out, *core_s)
        if comm is not None:
            last = functools.reduce(jnp.logical_and, [pl.program_id(d) == grid[d] - 1 for d in range(len(grid))])
            pl.when(last)(functools.partial(comm.wait, c_in, c_out, c_sem))

    extra = ([], [], [], []) if comm is None else ([_ANY] * n_ci, [_ANY] * n_co, comm.out_shapes, comm.sems)
    spec = pltpu.PrefetchScalarGridSpec(
        num_scalar_prefetch=n_pf, grid=grid, in_specs=list(in_specs) + extra[0], out_specs=list(out_specs) + extra[1],
        scratch_shapes=list(scratch_shapes) + extra[3])
    outs = pl.pallas_call(
        wrapped, name=name if comm is None else name + "_x", grid_spec=spec, out_shape=list(out_shape) + extra[2],
        compiler_params=_params(sem if comm is None else ("arbitrary",) * len(grid)),
    )(*prefetch, *args, *(comm.ins if comm is not None else []))
    return outs[:n_out], outs[n_out:]


def _pick(n, pref):
    for t in pref:
        if n % t == 0:
            return t
    return n


MM_VMEM_BUDGET = 44 * 1024 * 1024


def _mm_tiles(M, N, K):
    best = None
    for tk in [K] + [t for t in (2048, 1408, 1024, 896, 512, 384, 256, 128) if K % t == 0 and t < K]:
        for tm in (2048, 1024, 512, 256, 128):
            for tn in (1408, 1024, 896, 512, 384, 256, 128):
                if M % tm or N % tn:
                    continue
                nk = K // tk
                need = 2 * 2 * (tm * tk + tk * tn) + 2 * 4 * tm * tn + (4 * tm * tn if nk > 1 else 0)
                if need <= MM_VMEM_BUDGET:
                    cand = ((nk, -tm * tn), (tm, tn, tk))
                    best = cand if best is None or cand[0] < best[0] else best
    return best[1]


def matmul(a, b, mode, name, out_dtype=f32, comm=None):
    if mode == "nn":
        (M, K), (_, N) = a.shape, b.shape
    elif mode == "nt":
        (M, K), (N, _) = a.shape, b.shape
    else:
        (K, M), (_, N) = a.shape, b.shape
    tm, tn, tk = _mm_tiles(M, N, K)
    nk = K // tk
    dot = {"nn": _nn, "nt": _nt, "tn": _tn}[mode]

    def body(a_ref, b_ref, o_ref, *acc):
        k = pl.program_id(2)
        part = dot(a_ref[...], b_ref[...])
        if nk == 1:
            o_ref[...] = part.astype(out_dtype)
        else:
            acc_ref = acc[0]

            @pl.when(k == 0)
            def _():
                acc_ref[...] = part

            @pl.when(k > 0)
            def _():
                acc_ref[...] += part

            @pl.when(k == nk - 1)
            def _():
                o_ref[...] = acc_ref[...].astype(out_dtype)

    a_spec = pl.BlockSpec((tk, tm), lambda i, j, k: (k, i)) if mode == "tn" else pl.BlockSpec((tm, tk), lambda i, j, k: (i, k))
    b_spec = pl.BlockSpec((tn, tk), lambda i, j, k: (j, k)) if mode == "nt" else pl.BlockSpec((tk, tn), lambda i, j, k: (k, j))
    outs, got = _call(
        body, name=name, grid=(M // tm, N // tn, nk),
        in_specs=[a_spec, b_spec], out_specs=[pl.BlockSpec((tm, tn), lambda i, j, k: (i, j))],
        out_shape=[jax.ShapeDtypeStruct((M, N), out_dtype)],
        scratch_shapes=[] if nk == 1 else [pltpu.VMEM((tm, tn), f32)],
        sem=("parallel", "parallel", "arbitrary"), args=(a, b), comm=comm)
    return outs[0] if comm is None else (outs[0], got)


def out_proj(o2, w, x, gate, name):
    S, K = o2.shape
    N = w.shape[1]
    tm, tn = 512, 512

    def body(a_ref, b_ref, x_ref, g_ref, y_ref, xn_ref):
        y = _nn(a_ref[...], b_ref[...])
        y_ref[...] = y
        xn_ref[...] = x_ref[...] + g_ref[...] * y

    return pl.pallas_call(
        body, name=name, grid=(S // tm, N // tn),
        in_specs=[pl.BlockSpec((tm, K), lambda i, j: (i, 0)), pl.BlockSpec((K, tn), lambda i, j: (0, j)),
                  pl.BlockSpec((tm, tn), lambda i, j: (i, j)), pl.BlockSpec((1, tn), lambda i, j: (0, j))],
        out_specs=[pl.BlockSpec((tm, tn), lambda i, j: (i, j))] * 2,
        out_shape=[jax.ShapeDtypeStruct((S, N), f32)] * 2,
        compiler_params=_params(("parallel", "parallel")),
    )(o2, w, x, gate)


def ln_mod(x, nw, scale, shift):
    S = x.shape[0]
    tm = 512

    def body(x_ref, nw_ref, sc_ref, sh_ref, h_ref):
        xv = x_ref[...]
        r = lax.rsqrt(jnp.mean(xv * xv, axis=-1, keepdims=True) + EPS)
        h_ref[...] = ((xv * r) * nw_ref[...] * (1.0 + sc_ref[...]) + sh_ref[...]).astype(bf16)

    vec = pl.BlockSpec((1, D), lambda i: (0, 0))
    return pl.pallas_call(
        body, name="ln_mod", grid=(S // tm,),
        in_specs=[pl.BlockSpec((tm, D), lambda i: (i, 0)), vec, vec, vec],
        out_specs=pl.BlockSpec((tm, D), lambda i: (i, 0)),
        out_shape=jax.ShapeDtypeStruct((S, D), bf16),
        compiler_params=_params(("parallel",)),
    )(x, nw, scale, shift)


def ln_mod_bwd(x, nw, scale, dh, dxres):
    S = x.shape[0]
    tm = 512
    nb = S // tm

    def body(x_ref, nw_ref, sc_ref, dh_ref, dr_ref, dx_ref, st_ref):
        i = pl.program_id(0)
        xv = x_ref[...]
        r = lax.rsqrt(jnp.mean(xv * xv, axis=-1, keepdims=True) + EPS)
        xn = xv * r
        dh = dh_ref[...]
        dxn = dh * (nw_ref[...] * (1.0 + sc_ref[...]))
        dx_ref[...] = dr_ref[...] + r * (dxn - xn * jnp.mean(dxn * xn, axis=-1, keepdims=True))
        p1 = jnp.sum(dh * xn, axis=0, keepdims=True)
        p2 = jnp.sum(dh, axis=0, keepdims=True)
        upd = jnp.concatenate([p1, p1, p2, jnp.zeros((5, D), f32)], axis=0)

        @pl.when(i == 0)
        def _():
            st_ref[...] = upd

        @pl.when(i > 0)
        def _():
            st_ref[...] += upd

        @pl.when(i == nb - 1)
        def _():
            st_ref[0:1, :] = st_ref[0:1, :] * (1.0 + sc_ref[...])
            st_ref[1:2, :] = st_ref[1:2, :] * nw_ref[...]

    vec = pl.BlockSpec((1, D), lambda i: (0, 0))
    tile = pl.BlockSpec((tm, D), lambda i: (i, 0))
    return pl.pallas_call(
        body, name="ln_mod_bwd", grid=(S // tm,),
        in_specs=[tile, vec, vec, tile, tile],
        out_specs=[tile, pl.BlockSpec((8, D), lambda i: (0, 0))],
        out_shape=[jax.ShapeDtypeStruct((S, D), f32), jax.ShapeDtypeStruct((8, D), f32)],
        compiler_params=_params(("arbitrary",)),
    )(x, nw, scale, dh, dxres)


def final_loss(x, fw, tgt):
    S = x.shape[0]
    tm = 512

    def body(x_ref, w_ref, t_ref, dx_ref, st_ref):
        i = pl.program_id(0)
        xv = x_ref[...]
        r = lax.rsqrt(jnp.mean(xv * xv, axis=-1, keepdims=True) + EPS)
        xn = xv * r
        err = xn * w_ref[...] - t_ref[...]
        dy = err * (1.0 / D)
        dxn = dy * w_ref[...]
        dx_ref[...] = r * (dxn - xn * jnp.mean(dxn * xn, axis=-1, keepdims=True))
        p1 = jnp.sum(dy * xn, axis=0, keepdims=True)
        p2 = jnp.sum(err * err, axis=0, keepdims=True) * (0.5 / D)
        upd = jnp.concatenate([p1, p2, jnp.zeros((6, D), f32)], axis=0)

        @pl.when(i == 0)
        def _():
            st_ref[...] = upd

        @pl.when(i > 0)
        def _():
            st_ref[...] += upd

    tile = pl.BlockSpec((tm, D), lambda i: (i, 0))
    return pl.pallas_call(
        body, name="final_loss", grid=(S // tm,),
        in_specs=[tile, pl.BlockSpec((1, D), lambda i: (0, 0)), tile],
        out_specs=[tile, pl.BlockSpec((8, D), lambda i: (0, 0))],
        out_shape=[jax.ShapeDtypeStruct((S, D), f32), jax.ShapeDtypeStruct((8, D), f32)],
        compiler_params=_params(("arbitrary",)),
    )(x, fw, tgt)


def gate_bwd(dx, y, gate):
    S = dx.shape[0]
    tm = 512

    def body(dx_ref, y_ref, g_ref, dy_ref, st_ref):
        i = pl.program_id(0)
        dxv = dx_ref[...]
        dy_ref[...] = (g_ref[...] * dxv).astype(bf16)
        upd = jnp.concatenate([jnp.sum(dxv * y_ref[...], axis=0, keepdims=True), jnp.zeros((7, D), f32)], axis=0)

        @pl.when(i == 0)
        def _():
            st_ref[...] = upd

        @pl.when(i > 0)
        def _():
            st_ref[...] += upd

    tile = pl.BlockSpec((tm, D), lambda i: (i, 0))
    return pl.pallas_call(
        body, name="gate_bwd", grid=(S // tm,),
        in_specs=[tile, tile, pl.BlockSpec((1, D), lambda i: (0, 0))],
        out_specs=[tile, pl.BlockSpec((8, D), lambda i: (0, 0))],
        out_shape=[jax.ShapeDtypeStruct((S, D), bf16), jax.ShapeDtypeStruct((8, D), f32)],
        compiler_params=_params(("arbitrary",)),
    )(dx, y, gate)


def _chunk_mats(tm):
    r, c = _iota((tm, tm), 0), _iota((tm, tm), 1)
    same = jnp.right_shift(r, 6) == jnp.right_shift(c, 6)
    ltri = jnp.where(same & (c <= r), 1.0, 0.0).astype(f32)
    utri = jnp.where(same & (c >= r), 1.0, 0.0).astype(f32)
    bsame = jnp.where(same, 1.0, 0.0).astype(f32)
    return ltri, utri, bsame


def _gdn_scalars(ba, alog, dtb, ltri, bsame):
    beta = _sigmoid(ba[:, 0:16])
    u = ba[:, 16:32] + dtb
    neg_a = -jnp.exp(alog)
    g = neg_a * _softplus(u)
    gc = _nn(ltri, g, HI)
    glast = _nn(bsame, g, HI)
    return beta, u, neg_a, g, gc, glast


def _conv_taps(p_ref, halo_ref, first, gi):
    cs = slice(gi * 128, (gi + 1) * 128)
    cur = p_ref[:, cs]
    hal = jnp.where(first, 0.0, halo_ref[:, cs])
    ext = jnp.concatenate([hal, cur], axis=0)
    return [cur] + [pltpu.roll(ext, s, 0)[8:] for s in range(1, CONV_K)]


def _conv(taps, w):
    cv = taps[0] * w[3:4]
    for s in range(1, CONV_K):
        cv = cv + taps[s] * w[3 - s:4 - s]
    return cv


def _l2n(x):
    return x * lax.rsqrt(jnp.sum(x * x, axis=-1, keepdims=True) + EPS)


def _gdn_in_specs(tm, S):
    nb8 = tm // 8
    return [pl.BlockSpec((tm, G_CONV), lambda i: (i, 0)),
            pl.BlockSpec((8, G_CONV), lambda i: (jnp.maximum(i * nb8 - 1, 0), 0)),
            pl.BlockSpec((tm, 128), lambda i: (i, G_BA0 // 128))]


def gdn_pre(proj, conv_w, alog, dtb):
    S = proj.shape[0]
    tm = 256
    nch = tm // CHUNK

    def body(p_ref, halo_ref, ba_ref, w_ref, al_ref, dt_ref,
             q_ref, k_ref, kb_ref, kbg_ref, vb_ref, qd_ref, kd_ref, d_ref, gl_ref):
        first = pl.program_id(0) == 0
        ltri, _, bsame = _chunk_mats(tm)
        beta, _, _, _, gc, glast = _gdn_scalars(ba_ref[...], al_ref[...], dt_ref[...], ltri, bsame)
        eg, ek, egl = jnp.exp(gc), jnp.exp(glast - gc), jnp.exp(glast)
        eye = jnp.where(_iota((16, 16), 0) == _iota((16, 16), 1), 1.0, 0.0).astype(f32)
        gct = _nt(eye, gc, HI)
        low = _iota((CHUNK, CHUNK), 0) >= _iota((CHUNK, CHUNK), 1)

        def act(gi):
            return _silu(_conv(_conv_taps(p_ref, halo_ref, first, gi), w_ref[:, gi * 128:(gi + 1) * 128]))

        for j in range(GQK_H):
            js = slice(j * 128, (j + 1) * 128)
            qn = _l2n(act(j)) * (GHD ** -0.5)
            kn = _l2n(act(GQK_H + j))
            q_ref[:, js] = qn.astype(bf16)
            k_ref[:, js] = kn.astype(bf16)
            for e in range(2):
                h = 2 * j + e
                hs = slice(h * 128, (h + 1) * 128)
                v = act(2 * GQK_H + h)
                bh, egh, ekh = beta[:, h:h + 1], eg[:, h:h + 1], ek[:, h:h + 1]
                kbv = kn * bh
                kb_ref[:, hs] = kbv.astype(bf16)
                kbg_ref[:, hs] = (kbv * egh).astype(bf16)
                vb_ref[:, hs] = (v * bh).astype(bf16)
                qd_ref[:, hs] = (qn * egh).astype(bf16)
                kd_ref[:, hs] = (kn * ekh).astype(bf16)
                for c in range(nch):
                    rs = slice(c * CHUNK, (c + 1) * CHUNK)
                    diff = gc[rs, h:h + 1] - gct[h:h + 1, rs]
                    d_ref[rs, h * CHUNK:(h + 1) * CHUNK] = jnp.where(low, jnp.exp(jnp.where(low, diff, 0.0)), 0.0)
                    gl_ref[c * 8:(c + 1) * 8, hs] = jnp.broadcast_to(egl[c * CHUNK:c * CHUNK + 8, h:h + 1], (8, 128))

    full = lambda shape: pl.BlockSpec(shape, lambda i: (0, 0))
    t1 = pl.BlockSpec((tm, 1024), lambda i: (i, 0))
    t2 = pl.BlockSpec((tm, 2048), lambda i: (i, 0))
    sd = jax.ShapeDtypeStruct
    return pl.pallas_call(
        body, name="gdn_pre", grid=(S // tm,),
        in_specs=_gdn_in_specs(tm, S) + [full((CONV_K, G_CONV)), full((1, 16)), full((1, 16))],
        out_specs=[t1, t1, t2, t2, t2, t2, t2, t1, pl.BlockSpec((tm // 8, 2048), lambda i: (i, 0))],
        out_shape=[sd((S, 1024), bf16)] * 2 + [sd((S, 2048), bf16)] * 5 + [sd((S, 1024), f32), sd((S // 8, 2048), f32)],
        compiler_params=_params(("parallel",)),
    )(proj, proj, proj, conv_w, alog, dtb)


def _bnn(a, b):
    return lax.dot_general(a, b, (((2,), (1,)), ((0,), (0,))), preferred_element_type=f32)


def _bnt(a, b):
    return lax.dot_general(a, b, (((2,), (2,)), ((0,), (0,))), preferred_element_type=f32)


def _btn(a, b):
    return lax.dot_general(a, b, (((1,), (1,)), ((0,), (0,))), preferred_element_type=f32)


def _split(a):
    hi = a.astype(bf16)
    return hi, (a - hi.astype(f32)).astype(bf16)


def _cat3(h, l, axis, lhs):
    return jnp.concatenate([h, h, l] if lhs else [h, l, h], axis=axis)


def _tri_inv_b(L):
    eye = jnp.where(_iota((1, CHUNK, CHUNK), 1) == _iota((1, CHUNK, CHUNK), 2), 1.0, 0.0).astype(f32)
    P = -L
    T = eye + P
    ph, pl_ = _split(P)
    for _ in range(5):
        P = _bnn(_cat3(ph, pl_, 2, True), _cat3(ph, pl_, 1, False))
        ph, pl_ = _split(P)
        th, tl = _split(T)
        T = T + _bnn(_cat3(th, tl, 2, True), _cat3(ph, pl_, 1, False))
    return T


GTB = 512
GQH_FWD, GQH_BWD = 1, 2


def _gdn_slices(ncb, gnv):
    pairs = [(c, e) for c in range(ncb) for e in range(gnv)]
    rs = lambda c: slice(c * CHUNK, (c + 1) * CHUNK)
    cs = lambda e: slice(e * 128, (e + 1) * 128)
    ds_ = lambda e: slice(e * CHUNK, (e + 1) * CHUNK)
    ks = lambda e: slice((e // 2) * 128, (e // 2 + 1) * 128)
    return pairs, rs, cs, ds_, ks


def gdn_fwd(q, k, kb, kbg, vb, qd, kd, dm, gl8, comm=None):
    S = q.shape[0]
    nb, ncb = S // GTB, GTB // CHUNK
    GQH, GNV = GQH_FWD, 2 * GQH_FWD
    pairs, rs, cs, ds_, ks = _gdn_slices(ncb, GNV)

    def body(q_ref, k_ref, kb_ref, kbg_ref, vb_ref, qd_ref, kd_ref, d_ref, gl_ref,
             o_ref, w_ref, at_ref, t_ref, vn_ref, st_ref, state, u_scr):
        @pl.when(pl.program_id(1) == 0)
        def _():
            state[...] = jnp.zeros_like(state)

        stk = lambda ref, lanes: jnp.stack([ref[rs(c), lanes(e)] for c, e in pairs])
        kq = stk(k_ref, ks)
        dmat = stk(d_ref, ds_)
        strict = _iota((1, CHUNK, CHUNK), 1) > _iota((1, CHUNK, CHUNK), 2)
        T = _tri_inv_b(jnp.where(strict, _bnt(stk(kb_ref, cs), kq) * dmat, 0.0))
        tb = T.astype(bf16)
        u_scr[...] = _bnn(tb, stk(vb_ref, cs))
        wb = _bnn(tb, stk(kbg_ref, cs)).astype(bf16)
        per_qk = lambda ref: jnp.stack([ref[rs(c), ks(e)] for c, e in pairs if e % 2 == 0])
        qk = _bnt(per_qk(q_ref), per_qk(k_ref))
        for b, (c, e) in enumerate(pairs):
            w_ref[rs(c), cs(e)] = wb[b]
            at_ref[rs(c), ds_(e)] = (qk[b // 2] * dmat[b]).astype(bf16)
            t_ref[rs(c), ds_(e)] = T[b]
        for b, (c, e) in enumerate(pairs):
            sb = state[e].astype(bf16)
            vnb = (u_scr[b] - _nn(w_ref[rs(c), cs(e)], sb)).astype(bf16)
            o_ref[rs(c), cs(e)] = _nn(qd_ref[rs(c), cs(e)], sb) + _nn(at_ref[rs(c), ds_(e)], vnb)
            st_ref[c * 128:(c + 1) * 128, cs(e)] = sb
            state[e] = state[e] * gl_ref[c * 8:c * 8 + 1, cs(e)] + _tn(kd_ref[rs(c), cs(e)], vnb)
            vn_ref[rs(c), cs(e)] = vnb

    b1 = pl.BlockSpec((GTB, 128 * GQH), lambda j, i: (i, j))
    b2 = pl.BlockSpec((GTB, 256 * GQH), lambda j, i: (i, j))
    sd = jax.ShapeDtypeStruct
    return _call(
        body, name="gdn_fwd", grid=(GQK_H // GQH, nb),
        in_specs=[b1, b1, b2, b2, b2, b2, b2, b1, pl.BlockSpec((GTB // 8, 256 * GQH), lambda j, i: (i, j))],
        out_specs=[b2, b2, b1, b1, b2, pl.BlockSpec((ncb * 128, 256 * GQH), lambda j, i: (i, j))],
        out_shape=[sd((S, 2048), f32), sd((S, 2048), bf16), sd((S, 1024), bf16), sd((S, 1024), f32),
                   sd((S, 2048), bf16), sd((S // CHUNK * 128, 2048), bf16)],
        scratch_shapes=[pltpu.VMEM((GNV, 128, 128), f32), pltpu.VMEM((GNV * ncb, CHUNK, 128), f32)],
        sem=("parallel", "arbitrary"), args=(q, k, kb, kbg, vb, qd, kd, dm, gl8), comm=comm)


def gdn_bwd(do, q, k, kb, kbg, vb, qd, kd, dm, gl8, w, at, T, vn, st, comm=None):
    S = q.shape[0]
    nb, ncb = S // GTB, GTB // CHUNK
    GQH, GNV = GQH_BWD, 2 * GQH_BWD
    pairs, rs, cs, ds_, ks = _gdn_slices(ncb, GNV)

    def body(do_ref, q_ref, k_ref, kb_ref, kbg_ref, vb_ref, qd_ref, kd_ref, d_ref, gl_ref, w_ref, at_ref, t_ref, vn_ref, st_ref,
             dq_ref, dk_ref, dkb_ref, dkbg_ref, dvb_ref, dqd_ref, dkd_ref, dgc_ref, dstate, dvn_scr, dw_scr, dat_scr, dgl_scr):
        @pl.when(pl.program_id(1) == 0)
        def _():
            dstate[...] = jnp.zeros_like(dstate)

        for b, (c, e) in reversed(list(enumerate(pairs))):
            dob = do_ref[rs(c), cs(e)].astype(bf16)
            sb = st_ref[c * 128:(c + 1) * 128, cs(e)]
            vnb = vn_ref[rs(c), cs(e)]
            gl = gl_ref[c * 8:c * 8 + 1, cs(e)]
            dS = dstate[e]
            dsb = dS.astype(bf16)
            dvnb = (_tn(at_ref[rs(c), ds_(e)], dob) + _nn(kd_ref[rs(c), cs(e)], dsb)).astype(bf16)
            dvn_scr[b] = dvnb
            dat_scr[b] = _nt(dob, vnb)
            dqd_ref[rs(c), cs(e)] = _nt(dob, sb)
            dkd_ref[rs(c), cs(e)] = _nt(vnb, dsb)
            dw_scr[b] = (-_nt(dvnb, sb)).astype(bf16)
            dgl = jnp.sum(jnp.sum(dS * sb.astype(f32), axis=1, keepdims=True), axis=0, keepdims=True)
            dgl_scr[b] = jnp.broadcast_to(dgl * gl, (8, 128))
            dstate[e] = gl * dS + _tn(qd_ref[rs(c), cs(e)], dob) - _tn(w_ref[rs(c), cs(e)], dvnb)

        stk = lambda ref, lanes: jnp.stack([ref[rs(c), lanes(e)] for c, e in pairs])
        kq, qq = stk(k_ref, ks), stk(q_ref, ks)
        kbb = stk(kb_ref, cs)
        Tm = stk(t_ref, ds_)
        tb = Tm.astype(bf16)
        dvn, dw = dvn_scr[...], dw_scr[...]
        dT = _bnt(dvn, stk(vb_ref, cs)) + _bnt(dw, stk(kbg_ref, cs))
        dvb, dkbg = _btn(tb, dvn), _btn(tb, dw)
        th, tl = _split(Tm)
        xh, xl = _split(_bnt(_cat3(*_split(dT), 2, True), _cat3(th, tl, 2, False)))
        dL = -_btn(_cat3(th, tl, 1, True), _cat3(xh, xl, 1, False))
        dmat = stk(d_ref, ds_)
        strict = _iota((1, CHUNK, CHUNK), 1) > _iota((1, CHUNK, CHUNK), 2)
        dA = jnp.where(strict, dL * dmat, 0.0)
        dB = dat_scr[...] * dmat
        dAb, dBb = dA.astype(bf16), dB.astype(bf16)
        dkb = _bnn(dAb, kq)
        dkc = _btn(dAb, kbb) + _btn(dBb, qq)
        dqc = _bnn(dBb, kq)
        M = dA * _bnt(kbb, kq) + dB * _bnt(qq, kq)
        mh, ml = _split(M)
        colsum = _btn(jnp.concatenate([mh, ml], axis=1), jnp.ones((GNV * ncb, 2 * CHUNK, 128), bf16))
        lastrow = _iota((1, CHUNK, 128), 1) == CHUNK - 1
        for b, (c, e) in enumerate(pairs):
            dvb_ref[rs(c), cs(e)] = dvb[b]
            dkbg_ref[rs(c), cs(e)] = dkbg[b]
            dkb_ref[rs(c), cs(e)] = dkb[b]
            dgc_ref[rs(c), cs(e)] = (jnp.sum(M[b], axis=1, keepdims=True) - colsum[b]
                                     + jnp.where(lastrow[0], dgl_scr[b][0:1, :], 0.0))
        for b, (c, e) in enumerate(pairs):
            if e % 2 == 0:
                dq_ref[rs(c), ks(e)] = dqc[b] + dqc[b + 1]
                dk_ref[rs(c), ks(e)] = dkc[b] + dkc[b + 1]

    b1 = pl.BlockSpec((GTB, 128 * GQH), lambda j, i: (nb - 1 - i, j))
    b2 = pl.BlockSpec((GTB, 256 * GQH), lambda j, i: (nb - 1 - i, j))
    sd = jax.ShapeDtypeStruct
    return _call(
        body, name="gdn_bwd", grid=(GQK_H // GQH, nb),
        in_specs=[b2, b1, b1, b2, b2, b2, b2, b2, b1, pl.BlockSpec((GTB // 8, 256 * GQH), lambda j, i: (nb - 1 - i, j)),
                  b2, b1, b1, b2, pl.BlockSpec((ncb * 128, 256 * GQH), lambda j, i: (nb - 1 - i, j))],
        out_specs=[b1, b1, b2, b2, b2, b2, b2, b2],
        out_shape=[sd((S, 1024), f32)] * 2 + [sd((S, 2048), f32)] * 6,
        scratch_shapes=[pltpu.VMEM((GNV, 128, 128), f32), pltpu.VMEM((GNV * ncb, CHUNK, 128), bf16),
                        pltpu.VMEM((GNV * ncb, CHUNK, 128), bf16), pltpu.VMEM((GNV * ncb, CHUNK, CHUNK), f32),
                        pltpu.VMEM((GNV * ncb, 8, 128), f32)],
        sem=("parallel", "arbitrary"), args=(do, q, k, kb, kbg, vb, qd, kd, dm, gl8, w, at, T, vn, st), comm=comm)


def gdn_onorm(o, proj, nw):
    S = o.shape[0]
    tm = 256

    def body(o_ref, z_ref, nw_ref, o2_ref):
        for h in range(GV_H):
            hs = slice(h * 128, (h + 1) * 128)
            oh = o_ref[:, hs]
            r = lax.rsqrt(jnp.mean(oh * oh, axis=-1, keepdims=True) + EPS)
            o2_ref[:, hs] = (((oh * r) * nw_ref[...]) * _silu(z_ref[:, hs])).astype(bf16)

    t2 = pl.BlockSpec((tm, 2048), lambda i: (i, 0))
    return pl.pallas_call(
        body, name="gdn_onorm", grid=(S // tm,),
        in_specs=[t2, pl.BlockSpec((tm, 2048), lambda i: (i, G_Z0 // 2048)), pl.BlockSpec((1, 128), lambda i: (0, 0))],
        out_specs=t2, out_shape=jax.ShapeDtypeStruct((S, 2048), bf16),
        compiler_params=_params(("parallel",)),
    )(o, proj, nw)


def gdn_onorm_bwd(do2, o, proj, nw):
    S = o.shape[0]
    tm = 256

    def body(d_ref, o_ref, z_ref, nw_ref, do_ref, dz_ref, st_ref):
        i = pl.program_id(0)
        acc = jnp.zeros((1, 128), f32)
        for h in range(GV_H):
            hs = slice(h * 128, (h + 1) * 128)
            oh, z, d2 = o_ref[:, hs], z_ref[:, hs], d_ref[:, hs]
            r = lax.rsqrt(jnp.mean(oh * oh, axis=-1, keepdims=True) + EPS)
            on = oh * r
            dt = d2 * _silu(z)
            dz_ref[:, hs] = (d2 * (on * nw_ref[...]) * _dsilu(z)).astype(bf16)
            don = dt * nw_ref[...]
            acc = acc + jnp.sum(dt * on, axis=0, keepdims=True)
            do_ref[:, hs] = r * (don - on * jnp.mean(don * on, axis=-1, keepdims=True))
        upd = jnp.concatenate([acc, jnp.zeros((7, 128), f32)], axis=0)

        @pl.when(i == 0)
        def _():
            st_ref[...] = upd

        @pl.when(i > 0)
        def _():
            st_ref[...] += upd

    t2 = pl.BlockSpec((tm, 2048), lambda i: (i, 0))
    sd = jax.ShapeDtypeStruct
    return pl.pallas_call(
        body, name="gdn_onorm_bwd", grid=(S // tm,),
        in_specs=[t2, t2, pl.BlockSpec((tm, 2048), lambda i: (i, G_Z0 // 2048)), pl.BlockSpec((1, 128), lambda i: (0, 0))],
        out_specs=[t2, t2, pl.BlockSpec((8, 128), lambda i: (0, 0))],
        out_shape=[sd((S, 2048), f32), sd((S, 2048), bf16), sd((8, 128), f32)],
        compiler_params=_params(("arbitrary",)),
    )(do2, o, proj, nw)


def gdn_pre_bwd(proj, conv_w, alog, dtb, dq, dk, dkb, dkbg, dvb, dqd, dkd, dgcd):
    S = proj.shape[0]
    tm = 128

    def body(p_ref, halo_ref, ba_ref, w_ref, al_ref, dt_ref, dq_ref, dk_ref, dkb_ref, dkbg_ref, dvb_ref, dqd_ref, dkd_ref, dgc_ref,
             dcv_ref, dba_ref, st_ref):
        i = pl.program_id(0)
        first = i == 0
        ltri, utri, bsame = _chunk_mats(tm)
        beta, u, neg_a, g, gc, glast = _gdn_scalars(ba_ref[...], al_ref[...], dt_ref[...], ltri, bsame)
        eg, ek = jnp.exp(gc), jnp.exp(glast - gc)
        lane16 = _iota((tm, 16), 1)
        dgc_all = jnp.zeros((tm, 16), f32)
        rkd_all = jnp.zeros((tm, 16), f32)
        dbeta_all = jnp.zeros((tm, 16), f32)

        def pre(gi):
            return _conv(_conv_taps(p_ref, halo_ref, first, gi), w_ref[:, gi * 128:(gi + 1) * 128])

        def l2n_bwd(xt, dy):
            r = lax.rsqrt(jnp.sum(xt * xt, axis=-1, keepdims=True) + EPS)
            y = xt * r
            return r * (dy - y * jnp.sum(dy * y, axis=-1, keepdims=True))

        for j in range(GQK_H):
            js = slice(j * 128, (j + 1) * 128)
            cvq, cvk = pre(j), pre(GQK_H + j)
            qt, kt = _silu(cvq), _silu(cvk)
            qn = _l2n(qt) * (GHD ** -0.5)
            kn = _l2n(kt)
            dq_tot, dk_tot = dq_ref[:, js], dk_ref[:, js]
            for e in range(2):
                h = 2 * j + e
                hs = slice(h * 128, (h + 1) * 128)
                gv = 2 * GQK_H + h
                cvv = pre(gv)
                v = _silu(cvv)
                bh, egh, ekh = beta[:, h:h + 1], eg[:, h:h + 1], ek[:, h:h + 1]
                dkbg, dkd, dqd, dvb = dkbg_ref[:, hs], dkd_ref[:, hs], dqd_ref[:, hs], dvb_ref[:, hs]
                dkb_t = dkb_ref[:, hs] + dkbg * egh
                dk_tot = dk_tot + dkb_t * bh + dkd * ekh
                dq_tot = dq_tot + dqd * egh
                dcv_ref[:, gv * 128:(gv + 1) * 128] = (dvb * bh) * _dsilu(cvv)
                dbeta = jnp.sum(dkb_t * kn, axis=-1, keepdims=True) + jnp.sum(dvb * v, axis=-1, keepdims=True)
                rkd = jnp.sum(dkd * (kn * ekh), axis=-1, keepdims=True)
                dgc = (dgc_ref[:, hs][:, 0:1] + jnp.sum(dkbg * (kn * bh * egh), axis=-1, keepdims=True)
                       + jnp.sum(dqd * (qn * egh), axis=-1, keepdims=True) - rkd)
                sel = lane16 == h
                dgc_all = dgc_all + jnp.where(sel, dgc, 0.0)
                rkd_all = rkd_all + jnp.where(sel, rkd, 0.0)
                dbeta_all = dbeta_all + jnp.where(sel, dbeta, 0.0)
            dcv_ref[:, js] = l2n_bwd(qt, dq_tot * (GHD ** -0.5)) * _dsilu(cvq)
            ks = slice((GQK_H + j) * 128, (GQK_H + j + 1) * 128)
            dcv_ref[:, ks] = l2n_bwd(kt, dk_tot) * _dsilu(cvk)

        islast = jnp.bitwise_and(_iota((tm, 16), 0), CHUNK - 1) == CHUNK - 1
        dgc_all = dgc_all + jnp.where(islast, _nn(bsame, rkd_all, HI), 0.0)
        dg = _nn(utri, dgc_all, HI)
        da = dg * neg_a * _sigmoid(u)
        db = dbeta_all * beta * (1.0 - beta)
        r16, c128 = _iota((16, 128), 0), _iota((16, 128), 1)
        pb = jnp.where(c128 == r16, 1.0, 0.0).astype(f32)
        pa = jnp.where(c128 == r16 + 16, 1.0, 0.0).astype(f32)
        dba_ref[...] = _nn(db, pb, HI) + _nn(da, pa, HI)
        upd = jnp.concatenate([jnp.sum(dg * g, axis=0, keepdims=True), jnp.sum(da, axis=0, keepdims=True),
                               jnp.zeros((6, 16), f32)], axis=0)

        @pl.when(i == 0)
        def _():
            st_ref[...] = upd

        @pl.when(i > 0)
        def _():
            st_ref[...] += upd

    full = lambda shape: pl.BlockSpec(shape, lambda i: (0, 0))
    t1 = pl.BlockSpec((tm, 1024), lambda i: (i, 0))
    t2 = pl.BlockSpec((tm, 2048), lambda i: (i, 0))
    sd = jax.ShapeDtypeStruct
    return pl.pallas_call(
        body, name="gdn_pre_bwd", grid=(S // tm,),
        in_specs=_gdn_in_specs(tm, S) + [full((CONV_K, G_CONV)), full((1, 16)), full((1, 16)), t1, t1] + [t2] * 6,
        out_specs=[pl.BlockSpec((tm, G_CONV), lambda i: (i, 0)), pl.BlockSpec((tm, 128), lambda i: (i, 0)), full((8, 16))],
        out_shape=[sd((S, G_CONV), f32), sd((S, 128), f32), sd((8, 16), f32)],
        compiler_params=_params(("arbitrary",)),
    )(proj, proj, proj, conv_w, alog, dtb, dq, dk, dkb, dkbg, dvb, dqd, dkd, dgcd)


def gdn_conv_bwd(proj, conv_w, dcv, dz, dba):
    S = proj.shape[0]
    tm = 256
    nb, nb8 = S // tm, tm // 8

    def body(p_ref, halo_ref, w_ref, dcv_ref, nxt_ref, dz_ref, dba_ref, dp_ref, dw_ref):
        i = pl.program_id(0)
        first, last = i == 0, i == nb - 1
        for gi in range(G_CONV // 128):
            cs = slice(gi * 128, (gi + 1) * 128)
            taps = _conv_taps(p_ref, halo_ref, first, gi)
            cur = dcv_ref[:, cs]
            ext = jnp.concatenate([cur, jnp.where(last, 0.0, nxt_ref[:, cs])], axis=0)
            w = w_ref[:, cs]
            dp = cur * w[3:4]
            rows = [jnp.sum(cur * taps[3 - kk], axis=0, keepdims=True) for kk in range(CONV_K)]
            for s in range(1, CONV_K):
                dp = dp + pltpu.roll(ext, tm + 8 - s, 0)[:tm] * w[3 - s:4 - s]
            dp_ref[:, cs] = dp.astype(bf16)
            upd = jnp.concatenate(rows + [jnp.zeros((4, 128), f32)], axis=0)

            @pl.when(first)
            def _():
                dw_ref[:, cs] = upd

            @pl.when(i > 0)
            def _():
                dw_ref[:, cs] += upd

        dp_ref[:, G_Z0:G_BA0] = dz_ref[...]
        dp_ref[:, G_BA0:G_INP] = dba_ref[...].astype(bf16)

    sd = jax.ShapeDtypeStruct
    return pl.pallas_call(
        body, name="gdn_conv_bwd", grid=(nb,),
        in_specs=[pl.BlockSpec((tm, G_CONV), lambda i: (i, 0)),
                  pl.BlockSpec((8, G_CONV), lambda i: (jnp.maximum(i * nb8 - 1, 0), 0)),
                  pl.BlockSpec((CONV_K, G_CONV), lambda i: (0, 0)),
                  pl.BlockSpec((tm, G_CONV), lambda i: (i, 0)),
                  pl.BlockSpec((8, G_CONV), lambda i: (jnp.minimum((i + 1) * nb8, S // 8 - 1), 0)),
                  pl.BlockSpec((tm, 2048), lambda i: (i, 0)), pl.BlockSpec((tm, 128), lambda i: (i, 0))],
        out_specs=[pl.BlockSpec((tm, G_INP), lambda i: (i, 0)), pl.BlockSpec((8, G_CONV), lambda i: (0, 0))],
        out_shape=[sd((S, G_INP), bf16), sd((8, G_CONV), f32)],
        compiler_params=_params(("arbitrary",)),
    )(proj, proj, conv_w, dcv, dcv, dz, dba)


def _half_mean(t, lo_half):
    m0 = jnp.sum(jnp.where(lo_half, t, 0.0), axis=-1, keepdims=True)
    m1 = jnp.sum(jnp.where(lo_half, 0.0, t), axis=-1, keepdims=True)
    return jnp.where(lo_half, m0, m1) * (1.0 / F_HD)


def _split3(c):
    hi = c.astype(bf16).astype(f32)
    mid = (c - hi).astype(bf16).astype(f32)
    lo = (c - hi - mid).astype(bf16).astype(f32)
    return hi, mid, lo


def fox_pre(proj, fbias, qw2, kw2):
    S = proj.shape[0]
    tm = 256

    def body(q_ref, k_ref, v_ref, f_ref, fb_ref, qw_ref, kw_ref, qa_ref, ka_ref, vb_ref, carry):
        @pl.when(pl.program_id(0) == 0)
        def _():
            carry[...] = jnp.zeros_like(carry)

        logf = -_softplus(-(f_ref[:, 0:16] + fb_ref[...]))
        ltri = jnp.where(_iota((tm, tm), 1) <= _iota((tm, tm), 0), 1.0, 0.0).astype(f32)
        cum = _nn(ltri, logf, HI) + carry[0:1, :]
        carry[0:1, :] = cum[tm - 1:tm, :]
        lane = _iota((tm, 128), 1)
        lo_half = lane < F_HD
        for p in range(F_H // 2):
            ps = slice(p * 128, (p + 1) * 128)
            for src, w_ref, dst, is_q in ((q_ref, qw_ref, qa_ref, True), (k_ref, kw_ref, ka_ref, False)):
                x = src[:, ps]
                xn = x * lax.rsqrt(_half_mean(x * x, lo_half) + EPS) * w_ref[...]
                if is_q:
                    xn = xn * (F_HD ** -0.5)
                for e in range(2):
                    h = 2 * p + e
                    base = xn if e == 0 else pltpu.roll(xn, F_HD, 1)
                    hi, mid, lo = _split3(cum[:, h:h + 1])
                    pieces = jnp.where(lane == 64, hi, 0.0) + jnp.where(lane == 65, mid, 0.0) + jnp.where(lane == 66, lo, 0.0)
                    if is_q:
                        ext = pieces + jnp.where((lane >= 67) & (lane <= 69), 1.0, 0.0)
                    else:
                        ext = jnp.where((lane >= 64) & (lane <= 66), 1.0, 0.0) - pltpu.roll(pieces, 3, 1)
                    dst[:, h * 128:(h + 1) * 128] = jnp.where(lo_half, base, ext).astype(bf16)
        one = jnp.where(lane == F_HD, 1.0, 0.0)
        for p in range(F_H // 2):
            vv = v_ref[:, p * 128:(p + 1) * 128]
            vb_ref[:, (2 * p) * 128:(2 * p + 1) * 128] = jnp.where(lo_half, vv, one).astype(bf16)
            vb_ref[:, (2 * p + 1) * 128:(2 * p + 2) * 128] = jnp.where(lo_half, pltpu.roll(vv, F_HD, 1), one).astype(bf16)

    t1 = lambda c: pl.BlockSpec((tm, 1024), lambda i: (i, c))
    vec = lambda n: pl.BlockSpec((1, n), lambda i: (0, 0))
    sd = jax.ShapeDtypeStruct
    return pl.pallas_call(
        body, name="fox_pre", grid=(S // tm,),
        in_specs=[t1(0), t1(1), t1(2), pl.BlockSpec((tm, 128), lambda i: (i, F_F0 // 128)), vec(16), vec(128), vec(128)],
        out_specs=[pl.BlockSpec((tm, 2048), lambda i: (i, 0))] * 3,
        out_shape=[sd((S, 2048), bf16)] * 3,
        scratch_shapes=[pltpu.VMEM((8, 16), f32)],
        compiler_params=_params(("arbitrary",)),
    )(proj, proj, proj, proj, fbias, qw2, kw2)


FTQ = 512
FHS = 4


def fox_attn(qa, ka, v, comm=None):
    S = qa.shape[0]
    nq = S // FTQ

    live = [(i, j) for i in range(nq) for j in range(i + 1)]
    qi_tab = jnp.asarray([i for i, _ in live], jnp.int32)
    kj_tab = jnp.asarray([j for _, j in live], jnp.int32)

    def body(qi_ref, kj_ref, q_ref, k_ref, v_ref, o_ref, lse_ref, m_scr, acc_scr):
        t = pl.program_id(1)
        i, j = qi_ref[t], kj_ref[t]

        @pl.when(j == 0)
        def _():
            m_scr[...] = jnp.full_like(m_scr, NEG)
            acc_scr[...] = jnp.zeros_like(acc_scr)

        def step(diagonal):
            for e in range(FHS):
                es = slice(e * 128, (e + 1) * 128)
                s = _nt(q_ref[:, es], k_ref[:, es])
                if diagonal:
                    s = jnp.where(_iota((FTQ, FTQ), 0) >= _iota((FTQ, FTQ), 1), s, NEG)
                m_old = m_scr[e]
                m_new = jnp.maximum(m_old, jnp.max(s, axis=-1, keepdims=True))
                p = jnp.exp(s - m_new[:, 0:1])
                acc_scr[e] = acc_scr[e] * jnp.exp(m_old - m_new) + _nn(p.astype(bf16), v_ref[:, es])
                m_scr[e] = m_new

        pl.when(j < i)(functools.partial(step, False))

        @pl.when(j == i)
        def _():
            step(True)
            for e in range(FHS):
                vs = slice(e * F_HD, (e + 1) * F_HD)
                acc = acc_scr[e]
                l = acc[:, F_HD:F_HD + 1]
                o_ref[:, vs] = acc[:, 0:F_HD] / l
                lse_ref[:, vs] = m_scr[e][:, 0:F_HD] + jnp.log(l)

    sd = jax.ShapeDtypeStruct
    qo = pl.BlockSpec((FTQ, F_HD * FHS), lambda p, t, qi, kj: (qi[t], p))
    kv = pl.BlockSpec((FTQ, 128 * FHS), lambda p, t, qi, kj: (kj[t], p))
    return _call(
        body, name="fox_attn", grid=(F_H // FHS, len(live)),
        in_specs=[pl.BlockSpec((FTQ, 128 * FHS), lambda p, t, qi, kj: (qi[t], p)), kv, kv],
        out_specs=[qo, qo],
        out_shape=[sd((S, 1024), f32), sd((S, 1024), f32)],
        scratch_shapes=[pltpu.VMEM((FHS, FTQ, 128), f32), pltpu.VMEM((FHS, FTQ, 128), f32)],
        sem=("parallel", "arbitrary"), args=(qa, ka, v), comm=comm, prefetch=(qi_tab, kj_tab))


def fox_attn_bwd(qa, ka, v, do, lse, delta, comm=None):
    S = qa.shape[0]
    nq = S // FTQ

    live = [(j, i) for j in range(nq) for i in range(j, nq)]
    kj_tab = jnp.asarray([j for j, _ in live], jnp.int32)
    qi_tab = jnp.asarray([i for _, i in live], jnp.int32)

    def body(kj_ref, qi_ref, q_ref, k_ref, v_ref, do_ref, lse_ref, dl_ref, dq_ref, dk_ref, dv_ref, dk_scr, dv_scr):
        t = pl.program_id(1)
        j, i = kj_ref[t], qi_ref[t]

        @pl.when(t == 0)
        def _():
            dq_ref[...] = jnp.zeros_like(dq_ref)

        @pl.when(i == j)
        def _():
            dk_scr[...] = jnp.zeros_like(dk_scr)
            dv_scr[...] = jnp.zeros_like(dv_scr)

        def step(diagonal):
            rows = pl.ds(pl.multiple_of(i * FTQ, FTQ), FTQ)
            for e in range(FHS):
                es, vs = slice(e * 128, (e + 1) * 128), slice(e * F_HD, (e + 1) * F_HD)
                qe, ke = q_ref[:, es], k_ref[:, es]
                dob = do_ref[:, vs]
                s = _nt(qe, ke)
                if diagonal:
                    s = jnp.where(_iota((FTQ, FTQ), 0) >= _iota((FTQ, FTQ), 1), s, NEG)
                p = jnp.exp(s - lse_ref[:, e * F_HD:e * F_HD + 1])
                ds = p * (_nt(dob, v_ref[:, e * 128:e * 128 + F_HD]) - dl_ref[:, e * F_HD:e * F_HD + 1])
                dsb = ds.astype(bf16)
                dv_scr[e] += _tn(dob, p.astype(bf16))
                dk_scr[e] += _tn(qe, dsb)
                dq_ref[rows, es] += _nn(dsb, ke)

        pl.when(i > j)(functools.partial(step, False))
        pl.when(i == j)(functools.partial(step, True))

        @pl.when(i == nq - 1)
        def _():
            for e in range(FHS):
                dk_ref[:, e * 128:(e + 1) * 128] = dk_scr[e].T
                dv_ref[:, e * F_HD:(e + 1) * F_HD] = dv_scr[e].T

    sd = jax.ShapeDtypeStruct
    qi = lambda w: pl.BlockSpec((FTQ, w * FHS), lambda p, t, kj_, qi_: (qi_[t], p))
    kj = lambda w: pl.BlockSpec((FTQ, w * FHS), lambda p, t, kj_, qi_: (kj_[t], p))
    return _call(
        body, name="fox_attn_bwd", grid=(F_H // FHS, len(live)),
        in_specs=[qi(128), kj(128), kj(128), qi(F_HD), qi(F_HD), qi(F_HD)],
        out_specs=[pl.BlockSpec((S, 128 * FHS), lambda p, t, kj_, qi_: (0, p)), kj(128), kj(F_HD)],
        out_shape=[sd((S, 2048), f32), sd((S, 2048), f32), sd((S, 1024), f32)],
        scratch_shapes=[pltpu.VMEM((FHS, 128, FTQ), f32), pltpu.VMEM((FHS, F_HD, FTQ), f32)],
        sem=("parallel", "arbitrary"), args=(qa, ka, v, do, lse, delta), comm=comm, prefetch=(kj_tab, qi_tab))


def fox_gate(o, proj):
    S = o.shape[0]
    tm = 512

    def body(o_ref, z_ref, o2_ref):
        o2_ref[...] = (o_ref[...] * _silu(z_ref[...])).astype(bf16)

    t = pl.BlockSpec((tm, 1024), lambda i: (i, 0))
    return pl.pallas_call(
        body, name="fox_gate", grid=(S // tm,),
        in_specs=[t, pl.BlockSpec((tm, 1024), lambda i: (i, 3))], out_specs=t,
        out_shape=jax.ShapeDtypeStruct((S, 1024), bf16),
        compiler_params=_params(("parallel",)),
    )(o, proj)


def fox_gate_bwd(do2, o, proj):
    S = o.shape[0]
    tm = 256

    def body(d_ref, o_ref, z_ref, do_ref, dz_ref, dl_ref):
        lo_half = _iota((tm, 128), 1) < F_HD
        for p in range(F_H // 2):
            ps = slice(p * 128, (p + 1) * 128)
            d2, ov, z = d_ref[:, ps], o_ref[:, ps], z_ref[:, ps]
            dov = d2 * _silu(z)
            do_ref[:, ps] = dov.astype(bf16)
            dz_ref[:, ps] = (d2 * ov * _dsilu(z)).astype(bf16)
            dl_ref[:, ps] = _half_mean(dov * ov, lo_half) * float(F_HD)

    t = pl.BlockSpec((tm, 1024), lambda i: (i, 0))
    sd = jax.ShapeDtypeStruct
    return pl.pallas_call(
        body, name="fox_gate_bwd", grid=(S // tm,),
        in_specs=[t, t, pl.BlockSpec((tm, 1024), lambda i: (i, 3))], out_specs=[t, t, t],
        out_shape=[sd((S, 1024), bf16), sd((S, 1024), bf16), sd((S, 1024), f32)],
        compiler_params=_params(("parallel",)),
    )(do2, o, proj)


def fox_pre_bwd(proj, fbias, qw2, kw2, dqa, dka, dv, dz):
    S = proj.shape[0]
    tm = 256
    nb = S // tm

    def body(q_ref, k_ref, f_ref, fb_ref, qw_ref, kw_ref, dqa_ref, dka_ref, dv_ref, dz_ref, dp_ref, st_ref, carry):
        i = pl.program_id(0)

        @pl.when(i == 0)
        def _():
            carry[...] = jnp.zeros_like(carry)

        lane = _iota((tm, 128), 1)
        lo_half = lane < F_HD
        lane16 = _iota((tm, 16), 1)
        dcum = jnp.zeros((tm, 16), f32)
        dws = []
        for src, w_ref, dsrc, is_q, col0 in ((q_ref, qw_ref, dqa_ref, True, 0), (k_ref, kw_ref, dka_ref, False, 1024)):
            dw = jnp.zeros((1, 128), f32)
            for p in range(F_H // 2):
                ps = slice(p * 128, (p + 1) * 128)
                x = src[:, ps]
                r = lax.rsqrt(_half_mean(x * x, lo_half) + EPS)
                xh = x * r
                d0 = dsrc[:, (2 * p) * 128:(2 * p + 1) * 128]
                d1 = dsrc[:, (2 * p + 1) * 128:(2 * p + 2) * 128]
                dy = jnp.where(lo_half, d0, pltpu.roll(d1, F_HD, 1))
                if is_q:
                    dy = dy * (F_HD ** -0.5)
                dxh = dy * w_ref[...]
                dw = dw + jnp.sum(dy * xh, axis=0, keepdims=True)
                dp_ref[:, col0 + p * 128:col0 + (p + 1) * 128] = (r * (dxh - xh * _half_mean(dxh * xh, lo_half))).astype(bf16)
                for e, de in ((0, d0), (1, d1)):
                    col = de[:, 64:65] if is_q else -de[:, 67:68]
                    dcum = dcum + jnp.where(lane16 == 2 * p + e, col, 0.0)
            dws.append(dw)
        dp_ref[:, 2048:3072] = dv_ref[...].astype(bf16)
        dp_ref[:, 3072:4096] = dz_ref[...]
        utri = jnp.where(_iota((tm, tm), 1) >= _iota((tm, tm), 0), 1.0, 0.0).astype(f32)
        dlogf = _nn(utri, dcum, HI) + carry[0:1, :]
        carry[0:1, :] = dlogf[0:1, :]
        fl = f_ref[:, 0:16] + fb_ref[...]
        df = dlogf * _sigmoid(-fl)
        place = jnp.where(_iota((16, 128), 1) == _iota((16, 128), 0), 1.0, 0.0).astype(f32)
        dfw = _nn(df, place, HI)
        dp_ref[:, F_F0:F_INP] = dfw.astype(bf16)
        upd = jnp.concatenate(dws + [jnp.sum(dfw, axis=0, keepdims=True), jnp.zeros((5, 128), f32)], axis=0)

        @pl.when(i == 0)
        def _():
            st_ref[...] = upd

        @pl.when(i > 0)
        def _():
            st_ref[...] += upd

    rev = lambda w, c: pl.BlockSpec((tm, w), lambda i: (nb - 1 - i, c))
    vec = lambda n: pl.BlockSpec((1, n), lambda i: (0, 0))
    sd = jax.ShapeDtypeStruct
    return pl.pallas_call(
        body, name="fox_pre_bwd", grid=(nb,),
        in_specs=[rev(1024, 0), rev(1024, 1), rev(128, F_F0 // 128), vec(16), vec(128), vec(128),
                  rev(2048, 0), rev(2048, 0), rev(1024, 0), rev(1024, 0)],
        out_specs=[rev(F_INP, 0), pl.BlockSpec((8, 128), lambda i: (0, 0))],
        out_shape=[sd((S, F_INP), bf16), sd((8, 128), f32)],
        scratch_shapes=[pltpu.VMEM((8, 16), f32)],
        compiler_params=_params(("arbitrary",)),
    )(proj, proj, proj, fbias, qw2, kw2, dqa, dka, dv, dz)


def _me():
    return lax.axis_index("x"), lax.axis_index("y"), lax.axis_index("c")


def _other_chips(x, y):
    return [(1 - x, y), (x, 1 - y), (1 - x, 1 - y)]


def ag_small(xs):
    m_per, n = xs.shape

    def body(x_ref, out_ref, send_sems, recv_sems, local_sem):
        x, y, c = _me()
        me, sibling = (x, y, c), (x, y, 1 - c)
        chips = _other_chips(x, y)

        def rows(px, py, pc):
            return out_ref.at[pl.ds((4 * px + 2 * py + pc) * m_per, m_per), :]

        def copy(k, block, to, src=None):
            return pltpu.make_async_remote_copy(
                src_ref=rows(*block) if src is None else src, dst_ref=rows(*block),
                send_sem=send_sems.at[k], recv_sem=recv_sems.at[k], device_id=to, device_id_type=MESH)

        mine = pltpu.make_async_copy(x_ref, rows(*me), local_sem)
        mine.start()
        first = [copy(0, me, sibling, src=x_ref)]
        first += [copy(1 + j, me, (*chip, c), src=x_ref) for j, chip in enumerate(chips)]
        for cp in first:
            cp.start()
        passed = [copy(4 + j, (*chip, c), sibling) for j, chip in enumerate(chips)]
        for j, chip in enumerate(chips):
            copy(1 + j, (*chip, c), me).wait_recv()
            passed[j].start()
        copy(0, sibling, me).wait_recv()
        for j, chip in enumerate(chips):
            copy(4 + j, (*chip, 1 - c), me).wait_recv()
        for cp in first + passed:
            cp.wait_send()
        mine.wait()

    return pl.pallas_call(
        body, name="ag_small",
        out_shape=jax.ShapeDtypeStruct((8 * m_per, n), xs.dtype),
        in_specs=[pl.BlockSpec(memory_space=pltpu.VMEM)], out_specs=pl.BlockSpec(memory_space=pltpu.VMEM),
        scratch_shapes=[pltpu.SemaphoreType.DMA((7,)), pltpu.SemaphoreType.DMA((7,)), pltpu.SemaphoreType.DMA],
        compiler_params=pltpu.CompilerParams(vmem_limit_bytes=VMEM_LIMIT),
    )(xs)


_ANY = pl.BlockSpec(memory_space=pl.ANY)


def ag_chips(arrs):
    n = len(arrs)
    assert all(a.shape[0] == 2 for a in arrs)

    def body(*refs):
        ins, outs = refs[:n], refs[n:2 * n]
        send_sems, recv_sems, fwd_send, fwd_recv, local_sems = refs[2 * n:]
        x, y, c = _me()
        me = 2 * x + y
        chips = _other_chips(x, y)
        started = []
        for a in range(n):
            cp = pltpu.make_async_copy(ins[a], outs[a].at[me], local_sems.at[a])
            cp.start()
            started.append(cp)
        sends = []
        for a in range(n):
            for j, (px, py) in enumerate(chips):
                r = pltpu.make_async_remote_copy(
                    src_ref=ins[a].at[c], dst_ref=outs[a].at[me, c], send_sem=send_sems.at[3 * a + j],
                    recv_sem=recv_sems.at[3 * a + j], device_id=(px, py, c), device_id_type=MESH)
                r.start()
                sends.append(r)
        for a in range(n):
            for j, (px, py) in enumerate(chips):
                got = outs[a].at[2 * px + py, c]
                pltpu.make_async_remote_copy(
                    src_ref=ins[a].at[c], dst_ref=got, send_sem=send_sems.at[3 * a + j],
                    recv_sem=recv_sems.at[3 * a + j], device_id=(px, py, c), device_id_type=MESH).wait_recv()
                f = pltpu.make_async_remote_copy(
                    src_ref=got, dst_ref=got, send_sem=fwd_send.at[3 * a + j], recv_sem=fwd_recv.at[3 * a + j],
                    device_id=(x, y, 1 - c), device_id_type=MESH)
                f.start()
                sends.append(f)
        for a in range(n):
            for j, (px, py) in enumerate(chips):
                theirs = outs[a].at[2 * px + py, 1 - c]
                pltpu.make_async_remote_copy(
                    src_ref=theirs, dst_ref=theirs, send_sem=fwd_send.at[3 * a + j], recv_sem=fwd_recv.at[3 * a + j],
                    device_id=(x, y, 1 - c), device_id_type=MESH).wait_recv()
        for r in sends:
            r.wait_send()
        for cp in started:
            cp.wait()

    sems = pltpu.SemaphoreType.DMA((3 * n,))
    return pl.pallas_call(
        body, name="ag_chips",
        out_shape=[jax.ShapeDtypeStruct((4,) + a.shape, a.dtype) for a in arrs],
        in_specs=[_ANY] * n, out_specs=[_ANY] * n,
        scratch_shapes=[sems, sems, sems, sems, pltpu.SemaphoreType.DMA((n,))],
    )(*arrs)


def _ag_comm(arrs):
    n = len(arrs)

    def copies(ins, outs, sems, inbound):
        send_sems, recv_sems, local_sems = sems
        x, y, c = _me()
        me = 2 * x + y
        local = [pltpu.make_async_copy(ins[a], outs[a].at[me], local_sems.at[a]) for a in range(n)]
        out_cp, in_cp = [], []
        for a in range(n):
            for j, (px, py) in enumerate(_other_chips(x, y)):
                mk = functools.partial(pltpu.make_async_remote_copy, src_ref=ins[a], send_sem=send_sems.at[3 * a + j],
                                       recv_sem=recv_sems.at[3 * a + j], device_id=(px, py, c), device_id_type=MESH)
                out_cp.append(mk(dst_ref=outs[a].at[me]))
                if inbound:
                    in_cp.append(mk(dst_ref=outs[a].at[2 * px + py]))
        return local, out_cp, in_cp

    def start(ins, outs, sems):
        local, out_cp, _ = copies(ins, outs, sems, False)
        for cp in local + out_cp:
            cp.start()

    def wait(ins, outs, sems):
        local, out_cp, in_cp = copies(ins, outs, sems, True)
        for cp in in_cp:
            cp.wait_recv()
        for cp in out_cp:
            cp.wait_send()
        for cp in local:
            cp.wait()

    sems = [pltpu.SemaphoreType.DMA((3 * n,)), pltpu.SemaphoreType.DMA((3 * n,)), pltpu.SemaphoreType.DMA((n,))]
    return _Comm(arrs, [jax.ShapeDtypeStruct((4,) + a.shape, a.dtype) for a in arrs], sems, start, wait)


def _rs_comm(gs):
    n = len(gs)
    flips = [(fx, fy, fc) for fx in (0, 1) for fy in (0, 1) for fc in (0, 1)][1:]

    def copies(ins, outs, sems, inbound):
        send_sems, recv_sems, local_sems = sems
        x, y, c = _me()
        me = 4 * x + 2 * y + c
        local, out_cp, in_cp = [], [], []
        for a in range(n):
            rh = ins[a].shape[1] // 2
            mine = ins[a].at[2 * x + y, pl.ds(c * rh, rh), :]
            local.append(pltpu.make_async_copy(mine, outs[a].at[me], local_sems.at[a]))
            for j, (fx, fy, fc) in enumerate(flips):
                px, py, pc = (1 - x if fx else x), (1 - y if fy else y), (1 - c if fc else c)
                mk = functools.partial(pltpu.make_async_remote_copy, send_sem=send_sems.at[7 * a + j],
                                       recv_sem=recv_sems.at[7 * a + j], device_id=(px, py, pc), device_id_type=MESH)
                out_cp.append(mk(src_ref=ins[a].at[2 * px + py, pl.ds(pc * rh, rh), :], dst_ref=outs[a].at[me]))
                if inbound:
                    in_cp.append(mk(src_ref=mine, dst_ref=outs[a].at[4 * px + 2 * py + pc]))
        return local, out_cp, in_cp

    def start(ins, outs, sems):
        local, out_cp, _ = copies(ins, outs, sems, False)
        for cp in local + out_cp:
            cp.start()

    def wait(ins, outs, sems):
        local, out_cp, in_cp = copies(ins, outs, sems, True)
        for cp in in_cp:
            cp.wait_recv()
        for cp in out_cp:
            cp.wait_send()
        for cp in local:
            cp.wait()

    sems = [pltpu.SemaphoreType.DMA((7 * n,)), pltpu.SemaphoreType.DMA((7 * n,)), pltpu.SemaphoreType.DMA((n,))]
    return _Comm(gs, [jax.ShapeDtypeStruct((8, g.shape[1] // 2, g.shape[2]), g.dtype) for g in gs], sems, start, wait)


def sum_leading(q, name):
    K, R, C = q.shape
    tr = _pick(R, (256, 128, 64, 32, 16, 8))

    def body(q_ref, o_ref):
        acc = q_ref[0]
        for k in range(1, K):
            acc = acc + q_ref[k]
        o_ref[...] = acc

    return pl.pallas_call(
        body, name=name, grid=(R // tr,),
        in_specs=[pl.BlockSpec((K, tr, C), lambda i: (0, i, 0))], out_specs=pl.BlockSpec((tr, C), lambda i: (i, 0)),
        out_shape=jax.ShapeDtypeStruct((R, C), f32),
        compiler_params=_params(("parallel",)),
    )(q)


def rs_sum_devices(q, cidx):
    K, R, C = q.shape
    tr = _pick(R, (256, 128))

    def body(c_ref, q_ref, o_ref):
        acc = q_ref[0].astype(f32)
        for k in range(1, K):
            acc = acc + q_ref[k].astype(f32)
        o_ref[0] = acc

    return pl.pallas_call(
        body, name="rs_sum_devices",
        grid_spec=pltpu.PrefetchScalarGridSpec(
            num_scalar_prefetch=1, grid=(R // tr,),
            in_specs=[pl.BlockSpec((K, tr, C), lambda i, c_ref: (0, i, 0))],
            out_specs=pl.BlockSpec((1, tr, C), lambda i, c_ref: (c_ref[0], i, 0))),
        out_shape=jax.ShapeDtypeStruct((2, R, C), f32),
        compiler_params=_params(("parallel",)),
    )(cidx, q)


def rs_share_halves(rs):
    n = len(rs)

    def body(*refs):
        bufs = refs[n:2 * n]
        send_sems, recv_sems = refs[2 * n:]
        x, y, c = _me()
        cps = []
        for a in range(n):
            cp = pltpu.make_async_remote_copy(
                src_ref=bufs[a].at[c], dst_ref=bufs[a].at[c], send_sem=send_sems.at[a], recv_sem=recv_sems.at[a],
                device_id=(x, y, 1 - c), device_id_type=MESH)
            cp.start()
            cps.append(cp)
        for a, cp in enumerate(cps):
            pltpu.make_async_remote_copy(
                src_ref=bufs[a].at[c], dst_ref=bufs[a].at[1 - c], send_sem=send_sems.at[a], recv_sem=recv_sems.at[a],
                device_id=(x, y, 1 - c), device_id_type=MESH).wait_recv()
            cp.wait_send()

    return pl.pallas_call(
        body, name="rs_share_halves",
        out_shape=[jax.ShapeDtypeStruct(r.shape, r.dtype) for r in rs],
        in_specs=[_ANY] * n, out_specs=[_ANY] * n, input_output_aliases={a: a for a in range(n)},
        scratch_shapes=[pltpu.SemaphoreType.DMA((n,)), pltpu.SemaphoreType.DMA((n,))],
    )(*rs)


def ada_mod(c_all, ada_w):
    L, _, n = ada_w.shape

    def body(c_ref, w_ref, o_ref):
        o_ref[0] = _nn(_silu(c_ref[...]), w_ref[0], HI)

    return pl.pallas_call(
        body, name="ada_mod", grid=(L,),
        in_specs=[pl.BlockSpec((8, D), lambda l: (0, 0)), pl.BlockSpec((1, D, n), lambda l: (l, 0, 0))],
        out_specs=pl.BlockSpec((1, 8, n), lambda l: (l, 0, 0)),
        out_shape=jax.ShapeDtypeStruct((L, 8, n), f32),
        compiler_params=_params(("parallel",)),
    )(c_all, ada_w)


def ada_w_grad(c_all, dmod):
    L, _, n = dmod.shape

    def body(c_ref, d_ref, o_ref):
        o_ref[0] = _tn(_silu(c_ref[...]), d_ref[0], HI)

    return pl.pallas_call(
        body, name="ada_w_grad", grid=(L,),
        in_specs=[pl.BlockSpec((8, D), lambda l: (0, 0)), pl.BlockSpec((1, 8, n), lambda l: (l, 0, 0))],
        out_specs=pl.BlockSpec((1, D, n), lambda l: (l, 0, 0)),
        out_shape=jax.ShapeDtypeStruct((L, D, n), f32),
        compiler_params=_params(("parallel",)),
    )(c_all, dmod)


def adamw(w, g, m, v, name):
    shp = w.shape
    two = lambda a: a.reshape(-1, shp[-1])
    R, C = two(w).shape
    tr = _pick(R, (256, 128, 64, 32, 16, 8))
    bc1, bc2 = 1.0 - B1 ** STEP, 1.0 - B2 ** STEP

    def body(w_ref, g_ref, m_ref, v_ref, d_ref, mo_ref, vo_ref):
        gv = g_ref[...]
        mn = B1 * m_ref[...] + (1.0 - B1) * gv
        vn = B2 * v_ref[...] + (1.0 - B2) * (gv * gv)
        d_ref[...] = -LR * ((mn / bc1) / (jnp.sqrt(vn / bc2) + AEPS) + WD * w_ref[...])
        mo_ref[...] = mn
        vo_ref[...] = vn

    t = pl.BlockSpec((tr, C), lambda i: (i, 0))
    outs = pl.pallas_call(
        body, name=name, grid=(R // tr,),
        in_specs=[t] * 4, out_specs=[t] * 3, out_shape=[jax.ShapeDtypeStruct((R, C), f32)] * 3,
        compiler_params=_params(("parallel",)),
    )(two(w), two(g), two(m), two(v))
    return [o.reshape(shp) for o in outs]


def _pack(arrs):
    parts, offs, r0 = [], [], 0
    for a in arrs:
        n = a.size
        rows = -(-n // 1024) * 8
        parts.append(jnp.pad(a.reshape(-1), (0, rows * 128 - n)).reshape(rows, 128))
        offs.append((r0, rows))
        r0 += rows
    return jnp.concatenate(parts, axis=0), offs


def _unpack(buf, offs, shapes):
    out = []
    for (r0, rows), shp in zip(offs, shapes):
        n = 1
        for d in shp:
            n *= d
        out.append(buf[..., r0:r0 + rows, :].reshape(buf.shape[:-2] + (rows * 128,))[..., :n].reshape(buf.shape[:-2] + tuple(shp)))
    return out


def kernel(x, c, norm_w, ada_w, ada_b, a_w_in, a_conv_w, a_A_log, a_dt_bias, a_norm_w, a_w_out, b_w_in, b_f_bias, b_qn_w, b_kn_w, b_w_out, final_norm_w, loss_target, m_norm_w, m_ada_w, m_ada_b, m_a_w_in, m_a_conv_w, m_a_A_log, m_a_dt_bias, m_a_norm_w, m_a_w_out, m_b_w_in, m_b_f_bias, m_b_qn_w, m_b_kn_w, m_b_w_out, m_final_norm_w, v_norm_w, v_ada_w, v_ada_b, v_a_w_in, v_a_conv_w, v_a_A_log, v_a_dt_bias, v_a_norm_w, v_a_w_out, v_b_w_in, v_b_f_bias, v_b_qn_w, v_b_kn_w, v_b_w_out, v_final_norm_w):
    weights = dict(norm_w=norm_w, ada_w=ada_w, ada_b=ada_b, a_w_in=a_w_in, a_conv_w=a_conv_w, a_A_log=a_A_log,
                   a_dt_bias=a_dt_bias, a_norm_w=a_norm_w, a_w_out=a_w_out, b_w_in=b_w_in, b_f_bias=b_f_bias,
                   b_qn_w=b_qn_w, b_kn_w=b_kn_w, b_w_out=b_w_out, final_norm_w=final_norm_w)
    m_in = dict(norm_w=m_norm_w, ada_w=m_ada_w, ada_b=m_ada_b, a_w_in=m_a_w_in, a_conv_w=m_a_conv_w, a_A_log=m_a_A_log,
                a_dt_bias=m_a_dt_bias, a_norm_w=m_a_norm_w, a_w_out=m_a_w_out, b_w_in=m_b_w_in, b_f_bias=m_b_f_bias,
                b_qn_w=m_b_qn_w, b_kn_w=m_b_kn_w, b_w_out=m_b_w_out, final_norm_w=m_final_norm_w)
    v_in = dict(norm_w=v_norm_w, ada_w=v_ada_w, ada_b=v_ada_b, a_w_in=v_a_w_in, a_conv_w=v_a_conv_w, a_A_log=v_a_A_log,
                a_dt_bias=v_a_dt_bias, a_norm_w=v_a_norm_w, a_w_out=v_a_w_out, b_w_in=v_b_w_in, b_f_bias=v_b_f_bias,
                b_qn_w=v_b_qn_w, b_kn_w=v_b_kn_w, b_w_out=v_b_w_out, final_norm_w=v_final_norm_w)
    xi, yi, ci = _me()
    me_b, me_k = 4 * xi + 2 * yi + ci, 2 * xi + yi
    cidx = ci.astype(jnp.int32).reshape(1)
    S = x.shape[1]
    depth, n_a, n_b = norm_w.shape[0], a_w_in.shape[0], b_w_in.shape[0]
    x0, tgt = x.reshape(S, D), loss_target.reshape(S, D)

    c_all = ag_small(jnp.pad(c, ((0, 7), (0, 0)))).reshape(8, 8, D)[:, 0]
    nloc = ada_w.shape[2]
    parts = ag_small(ada_mod(c_all, ada_w).reshape(depth * 8, nloc)).reshape(4, 2, depth, 8, nloc)[:, 0]
    mine = lax.dynamic_index_in_dim(parts, me_b, axis=2, keepdims=False)
    mod = jnp.transpose(mine, (1, 0, 2)).reshape(depth, 4 * nloc) + ada_b
    shift, scale, gate = (mod[:, k * D:(k + 1) * D] for k in range(3))

    w_loc = [(a_w_in[i // 2] if i % 2 == 0 else b_w_in[i // 2]).astype(bf16) for i in range(depth)]
    wo_loc = [(a_w_out[i // 2] if i % 2 == 0 else b_w_out[i // 2]).astype(bf16) for i in range(depth)]
    pad_in = [(G_INP - G_IN) if i % 2 == 0 else (F_INP - F_IN) for i in range(depth)]
    halves = lambda w: w.reshape((2, w.shape[0] // 2) + w.shape[1:])

    def cols_in_place(g_in, pad):
        w = jnp.transpose(g_in, (1, 0, 2)).reshape(g_in.shape[1], -1)
        return jnp.pad(w, ((0, 0), (0, pad)))

    g_in0, g_conv = ag_chips([halves(w_loc[0]), a_conv_w])
    w_in_full = [cols_in_place(g_in0.reshape((4,) + w_loc[0].shape), pad_in[0])]
    w_out_full = []
    conv = [jnp.transpose(g_conv[:, l], (1, 0, 2)).reshape(CONV_K, -1) for l in range(n_a)]
    qw2 = [_row(jnp.tile(b_qn_w[l], 2)) for l in range(n_b)]
    kw2 = [_row(jnp.tile(b_kn_w[l], 2)) for l in range(n_b)]

    saved, xc = [], x0
    for i in range(depth):
        l = i // 2
        nxt = _ag_comm([w_loc[i + 1], wo_loc[i + 1]]) if i + 1 < depth else None
        h = ln_mod(xc, _row(norm_w[i]), _row(scale[i]), _row(shift[i]))
        name = "mm_a_in" if i % 2 == 0 else "mm_b_in"
        if i == 0:
            proj, got = matmul(h, w_in_full[0], "nn", name, comm=_ag_comm([wo_loc[0]]))
            w_out_full.append(got[0].reshape(-1, D))
        else:
            proj = matmul(h, w_in_full[i], "nn", name)
        if i % 2 == 0:
            pre = gdn_pre(proj, conv[l], _row(a_A_log[l]), _row(a_dt_bias[l]))
            res, got = gdn_fwd(*pre, comm=nxt)
            o2 = gdn_onorm(res[0], proj, _row(a_norm_w[l]))
            y, xn = out_proj(o2, w_out_full[i], xc, _row(gate[i]), "out_proj_a")
        else:
            pre = fox_pre(proj, _row(b_f_bias[l]), qw2[l], kw2[l])
            res, got = fox_attn(*pre, comm=nxt)
            o2 = fox_gate(res[0], proj)
            y, xn = out_proj(o2, w_out_full[i], xc, _row(gate[i]), "out_proj_b")
        saved.append((xc, h, proj, o2, y, pre, res))
        if nxt is not None:
            w_in_full.append(cols_in_place(got[0], pad_in[i + 1]))
            w_out_full.append(got[1].reshape(-1, D))
        xc = xn
    dx, st_f = final_loss(xc, _row(final_norm_w), tgt)

    d_norm, d_mod = [None] * depth, [None] * depth
    d_conv, d_alog, d_dtb, d_anw = [None] * n_a, [None] * n_a, [None] * n_a, [None] * n_a
    d_fb, d_qn, d_kn = [None] * n_b, [None] * n_b, [None] * n_b
    ex_in, ex_out, pend_in = [None] * depth, [None] * depth, None
    for i in reversed(range(depth)):
        l = i // 2
        xin, h, proj, o2, y, pre, res = saved[i]
        ab = "a" if i % 2 == 0 else "b"
        dy, st_g = gate_bwd(dx, y, _row(gate[i]))
        do2 = matmul(dy, w_out_full[i], "nt", f"mm_{ab}_do2")
        d_out = matmul(o2, dy, "tn", f"mm_{ab}_dwo")
        ride = _rs_comm(([] if pend_in is None else [pend_in]) + [d_out.reshape(4, d_out.shape[0] // 4, D).astype(bf16)])
        if i % 2 == 0:
            o, wv, at, tinv, vn, st = res
            do, dz, st_o = gdn_onorm_bwd(do2, o, proj, _row(a_norm_w[l]))
            grads, got = gdn_bwd(do, *pre, wv, at, tinv, vn, st, comm=ride)
            dcv, dba, st_s = gdn_pre_bwd(proj, conv[l], _row(a_A_log[l]), _row(a_dt_bias[l]), *grads)
            dproj, dcw = gdn_conv_bwd(proj, conv[l], dcv, dz, dba)
            d_conv[l], d_alog[l], d_dtb[l], d_anw[l] = dcw[:CONV_K], st_s[0], st_s[1], st_o[0]
        else:
            o, lse = res
            do, dz, delta = fox_gate_bwd(do2, o, proj)
            (dqa, dka, dv), got = fox_attn_bwd(*pre, do, lse, delta, comm=ride)
            dproj, st_b = fox_pre_bwd(proj, _row(b_f_bias[l]), qw2[l], kw2[l], dqa, dka, dv, dz)
            d_fb[l], d_qn[l], d_kn[l] = st_b[2, :F_H], st_b[0, :F_HD] + st_b[0, F_HD:], st_b[1, :F_HD] + st_b[1, F_HD:]
        ex_out[i] = got[-1]
        if pend_in is not None:
            ex_in[i + 1] = got[0]
        d_in = matmul(h, dproj, "tn", f"mm_{ab}_dw")
        cl = w_loc[i].shape[1]
        pend_in = jnp.transpose(d_in[:, :4 * cl].reshape(d_in.shape[0], 4, cl), (1, 0, 2)).astype(bf16)
        if i == 0:
            dh, got = matmul(dproj, w_in_full[i], "nt", f"mm_{ab}_dh", comm=_rs_comm([pend_in]))
            ex_in[0] = got[0]
        else:
            dh = matmul(dproj, w_in_full[i], "nt", f"mm_{ab}_dh")
        dx, st_n = ln_mod_bwd(xin, _row(norm_w[i]), _row(scale[i]), dh, dx)
        d_norm[i] = st_n[0]
        d_mod[i] = jnp.concatenate([st_n[2], st_n[1], st_g[0]])

    small = [jnp.stack(d_norm), jnp.stack(d_mod), jnp.stack(d_conv), jnp.stack(d_alog), jnp.stack(d_dtb), jnp.stack(d_anw),
             jnp.stack(d_fb), jnp.stack(d_qn), jnp.stack(d_kn), st_f[0], jnp.sum(st_f[1]).reshape(1)]
    shapes = [a.shape for a in small]
    buf, offs = _pack(small)
    gathered = ag_small(buf).reshape(8, buf.shape[0], 128)
    tot = _unpack(sum_leading(gathered, "sum_devices"), offs, shapes)
    g_norm, g_adab, g_convf, g_alog, g_dtb, g_anw, g_fb, g_qn, g_kn, g_fin, loss = tot
    dmod_all = _unpack(gathered, offs[1:2], shapes[1:2])[0]
    dmod_loc = lax.dynamic_slice_in_dim(dmod_all, me_k * nloc, nloc, axis=2)
    g_adaw = ada_w_grad(c_all, jnp.transpose(dmod_loc, (1, 0, 2)))
    g_conv_loc = lax.dynamic_slice_in_dim(g_convf, me_k * a_conv_w.shape[2], a_conv_w.shape[2], axis=2)

    flat = [q for i in range(depth) for q in (ex_in[i], ex_out[i])]
    done = rs_share_halves([rs_sum_devices(q, cidx) for q in flat])
    red = [d.reshape(-1, d.shape[-1]) for d in done]
    r_in, r_out = red[0::2], red[1::2]
    grads = dict(norm_w=g_norm, ada_w=g_adaw, ada_b=g_adab, a_w_in=jnp.stack(r_in[0::2]), a_conv_w=g_conv_loc,
                 a_A_log=g_alog, a_dt_bias=g_dtb, a_norm_w=g_anw, a_w_out=jnp.stack(r_out[0::2]),
                 b_w_in=jnp.stack(r_in[1::2]), b_f_bias=g_fb, b_qn_w=g_qn, b_kn_w=g_kn,
                 b_w_out=jnp.stack(r_out[1::2]), final_norm_w=g_fin)
    names = list(weights)
    upd = {n: adamw(weights[n], grads[n], m_in[n], v_in[n], "adamw_" + n) for n in names}
    return (loss.reshape(()), dx.reshape(x.shape), *[grads[n] for n in names], *[upd[n][0] for n in names],
            *[upd[n][1] for n in names], *[upd[n][2] for n in names])
```

```python
import functools

import jax
import jax.numpy as jnp
from jax import lax
from jax.experimental import pallas as pl
from jax.experimental.pallas import tpu as pltpu

f32, bf16 = jnp.float32, jnp.bfloat16
HI = lax.Precision.HIGHEST
MESH = pl.DeviceIdType.MESH

EPS = 1e-6
D = 1024
CHUNK = 64
GQK_H, GV_H, GHD = 8, 16, 128
G_CONV = 4096
G_Z0 = 4096
G_BA0 = 6144
G_IN, G_INP = 6176, 6272
CONV_K = 4
F_H, F_HD = 16, 64
F_W = 1024
F_F0 = 4096
F_IN, F_INP = 4112, 4224
LR, B1, B2, AEPS, WD, STEP = 0.001, 0.9, 0.999, 1e-08, 0.01, 10
NEG = -1e30
VMEM_LIMIT = 56 * 1024 * 1024


def _nn(a, b, prec=None):
    return lax.dot_general(a, b, (((1,), (0,)), ((), ())), preferred_element_type=f32, precision=prec)


def _nt(a, b, prec=None):
    return lax.dot_general(a, b, (((1,), (1,)), ((), ())), preferred_element_type=f32, precision=prec)


def _tn(a, b, prec=None):
    return lax.dot_general(a, b, (((0,), (0,)), ((), ())), preferred_element_type=f32, precision=prec)


def _iota(shape, axis):
    return lax.broadcasted_iota(jnp.int32, shape, axis)


def _sigmoid(x):
    return 0.5 * jnp.tanh(0.5 * x) + 0.5


def _softplus(x):
    return jnp.maximum(x, 0.0) + jnp.log(1.0 + jnp.exp(-jnp.abs(x)))


def _silu(x):
    return x * _sigmoid(x)


def _dsilu(x):
    s = _sigmoid(x)
    return s * (1.0 + x * (1.0 - s))


def _params(sem=None, vmem=VMEM_LIMIT):
    return pltpu.CompilerParams(dimension_semantics=sem, vmem_limit_bytes=vmem)


def _row(v):
    return v.reshape(1, -1)


class _Comm:
    def __init__(self, ins, out_shapes, sems, start, wait):
        self.ins, self.out_shapes, self.sems, self.start, self.wait = list(ins), list(out_shapes), list(sems), start, wait


def _call(body, *, name, grid, in_specs, out_specs, out_shape, scratch_shapes, sem, args, comm=None, prefetch=()):
    n_pf, n_in, n_out, n_s = len(prefetch), len(in_specs), len(out_specs), len(scratch_shapes)
    n_ci, n_co = (len(comm.ins), len(comm.out_shapes)) if comm is not None else (0, 0)

    def wrapped(*refs):
        pf, refs = refs[:n_pf], refs[n_pf:]
        core_in, c_in = refs[:n_in], refs[n_in:n_in + n_ci]
        o0 = n_in + n_ci
        core_out, c_out = refs[o0:o0 + n_out], refs[o0 + n_out:o0 + n_out + n_co]
        s0 = o0 + n_out + n_co
        core_s, c_sem = refs[s0:s0 + n_s], refs[s0 + n_s:]
        if comm is not None:
            first = functools.reduce(jnp.logical_and, [pl.program_id(d) == 0 for d in range(len(grid))])
            pl.when(first)(functools.partial(comm.start, c_in, c_out, c_sem))
        body(*pf, *core_in, *core_out, *core_s)
        if comm is not None:
            last = functools.reduce(jnp.logical_and, [pl.program_id(d) == grid[d] - 1 for d in range(len(grid))])
            pl.when(last)(functools.partial(comm.wait, c_in, c_out, c_sem))

    extra = ([], [], [], []) if comm is None else ([_ANY] * n_ci, [_ANY] * n_co, comm.out_shapes, comm.sems)
    spec = pltpu.PrefetchScalarGridSpec(
        num_scalar_prefetch=n_pf, grid=grid, in_specs=list(in_specs) + extra[0], out_specs=list(out_specs) + extra[1],
        scratch_shapes=list(scratch_shapes) + extra[3])
    outs = pl.pallas_call(
        wrapped, name=name if comm is None else name + "_x", grid_spec=spec, out_shape=list(out_shape) + extra[2],
        compiler_params=_params(sem if comm is None else ("arbitrary",) * len(grid)),
    )(*prefetch, *args, *(comm.ins if comm is not None else []))
    return outs[:n_out], outs[n_out:]


def _pick(n, pref):
    for t in pref:
        if n % t == 0:
            return t
    return n


MM_VMEM_BUDGET = 44 * 1024 * 1024


def _mm_tiles(M, N, K):
    best = None
    for tk in [K] + [t for t in (2048, 1408, 1024, 896, 512, 384, 256, 128) if K % t == 0 and t < K]:
        for tm in (2048, 1024, 512, 256, 128):
            for tn in (1408, 1024, 896, 512, 384, 256, 128):
                if M % tm or N % tn:
                    continue
                nk = K // tk
                need = 2 * 2 * (tm * tk + tk * tn) + 2 * 4 * tm * tn + (4 * tm * tn if nk > 1 else 0)
                if need <= MM_VMEM_BUDGET:
                    cand = ((nk, -tm * tn), (tm, tn, tk))
                    best = cand if best is None or cand[0] < best[0] else best
    return best[1]


def matmul(a, b, mode, name, out_dtype=f32, comm=None):
    if mode == "nn":
        (M, K), (_, N) = a.shape, b.shape
    elif mode == "nt":
        (M, K), (N, _) = a.shape, b.shape
    else:
        (K, M), (_, N) = a.shape, b.shape
    tm, tn, tk = _mm_tiles(M, N, K)
    nk = K // tk
    dot = {"nn": _nn, "nt": _nt, "tn": _tn}[mode]

    def body(a_ref, b_ref, o_ref, *acc):
        k = pl.program_id(2)
        part = dot(a_ref[...], b_ref[...])
        if nk == 1:
            o_ref[...] = part.astype(out_dtype)
        else:
            acc_ref = acc[0]

            @pl.when(k == 0)
            def _():
                acc_ref[...] = part

            @pl.when(k > 0)
            def _():
                acc_ref[...] += part

            @pl.when(k == nk - 1)
            def _():
                o_ref[...] = acc_ref[...].astype(out_dtype)

    a_spec = pl.BlockSpec((tk, tm), lambda i, j, k: (k, i)) if mode == "tn" else pl.BlockSpec((tm, tk), lambda i, j, k: (i, k))
    b_spec = pl.BlockSpec((tn, tk), lambda i, j, k: (j, k)) if mode == "nt" else pl.BlockSpec((tk, tn), lambda i, j, k: (k, j))
    outs, got = _call(
        body, name=name, grid=(M // tm, N // tn, nk),
        in_specs=[a_spec, b_spec], out_specs=[pl.BlockSpec((tm, tn), lambda i, j, k: (i, j))],
        out_shape=[jax.ShapeDtypeStruct((M, N), out_dtype)],
        scratch_shapes=[] if nk == 1 else [pltpu.VMEM((tm, tn), f32)],
        sem=("parallel", "parallel", "arbitrary"), args=(a, b), comm=comm)
    return outs[0] if comm is None else (outs[0], got)


def out_proj(o2, w, x, gate, name):
    S, K = o2.shape
    N = w.shape[1]
    tm, tn = 1024, 1024

    def body(a_ref, b_ref, x_ref, g_ref, y_ref, xn_ref):
        y = _nn(a_ref[...], b_ref[...])
        y_ref[...] = y
        xn_ref[...] = x_ref[...] + g_ref[...] * y

    return pl.pallas_call(
        body, name=name, grid=(S // tm, N // tn),
        in_specs=[pl.BlockSpec((tm, K), lambda i, j: (i, 0)), pl.BlockSpec((K, tn), lambda i, j: (0, j)),
                  pl.BlockSpec((tm, tn), lambda i, j: (i, j)), pl.BlockSpec((1, tn), lambda i, j: (0, j))],
        out_specs=[pl.BlockSpec((tm, tn), lambda i, j: (i, j))] * 2,
        out_shape=[jax.ShapeDtypeStruct((S, N), f32)] * 2,
        compiler_params=_params(("parallel", "parallel")),
    )(o2, w, x, gate)


def ln_mod(x, nw, scale, shift):
    S = x.shape[0]
    tm = 512

    def body(x_ref, nw_ref, sc_ref, sh_ref, h_ref):
        xv = x_ref[...]
        r = lax.rsqrt(jnp.mean(xv * xv, axis=-1, keepdims=True) + EPS)
        h_ref[...] = ((xv * r) * nw_ref[...] * (1.0 + sc_ref[...]) + sh_ref[...]).astype(bf16)

    vec = pl.BlockSpec((1, D), lambda i: (0, 0))
    return pl.pallas_call(
        body, name="ln_mod", grid=(S // tm,),
        in_specs=[pl.BlockSpec((tm, D), lambda i: (i, 0)), vec, vec, vec],
        out_specs=pl.BlockSpec((tm, D), lambda i: (i, 0)),
        out_shape=jax.ShapeDtypeStruct((S, D), bf16),
        compiler_params=_params(("parallel",)),
    )(x, nw, scale, shift)


def ln_mod_bwd(x, nw, scale, dh, dxres):
    S = x.shape[0]
    tm = 512
    nb = S // tm

    def body(x_ref, nw_ref, sc_ref, dh_ref, dr_ref, dx_ref, st_ref):
        i = pl.program_id(0)
        xv = x_ref[...]
        r = lax.rsqrt(jnp.mean(xv * xv, axis=-1, keepdims=True) + EPS)
        xn = xv * r
        dh = dh_ref[...]
        dxn = dh * (nw_ref[...] * (1.0 + sc_ref[...]))
        dx_ref[...] = dr_ref[...] + r * (dxn - xn * jnp.mean(dxn * xn, axis=-1, keepdims=True))
        p1 = jnp.sum(dh * xn, axis=0, keepdims=True)
        p2 = jnp.sum(dh, axis=0, keepdims=True)
        upd = jnp.concatenate([p1, p1, p2, jnp.zeros((5, D), f32)], axis=0)

        @pl.when(i == 0)
        def _():
            st_ref[...] = upd

        @pl.when(i > 0)
        def _():
            st_ref[...] += upd

        @pl.when(i == nb - 1)
        def _():
            st_ref[0:1, :] = st_ref[0:1, :] * (1.0 + sc_ref[...])
            st_ref[1:2, :] = st_ref[1:2, :] * nw_ref[...]

    vec = pl.BlockSpec((1, D), lambda i: (0, 0))
    tile = pl.BlockSpec((tm, D), lambda i: (i, 0))
    return pl.pallas_call(
        body, name="ln_mod_bwd", grid=(S // tm,),
        in_specs=[tile, vec, vec, tile, tile],
        out_specs=[tile, pl.BlockSpec((8, D), lambda i: (0, 0))],
        out_shape=[jax.ShapeDtypeStruct((S, D), f32), jax.ShapeDtypeStruct((8, D), f32)],
        compiler_params=_params(("arbitrary",)),
    )(x, nw, scale, dh, dxres)


def final_loss(x, fw, tgt):
    S = x.shape[0]
    tm = 512

    def body(x_ref, w_ref, t_ref, dx_ref, st_ref):
        i = pl.program_id(0)
        xv = x_ref[...]
        r = lax.rsqrt(jnp.mean(xv * xv, axis=-1, keepdims=True) + EPS)
        xn = xv * r
        err = xn * w_ref[...] - t_ref[...]
        dy = err * (1.0 / D)
        dxn = dy * w_ref[...]
        dx_ref[...] = r * (dxn - xn * jnp.mean(dxn * xn, axis=-1, keepdims=True))
        p1 = jnp.sum(dy * xn, axis=0, keepdims=True)
        p2 = jnp.sum(err * err, axis=0, keepdims=True) * (0.5 / D)
        upd = jnp.concatenate([p1, p2, jnp.zeros((6, D), f32)], axis=0)

        @pl.when(i == 0)
        def _():
            st_ref[...] = upd

        @pl.when(i > 0)
        def _():
            st_ref[...] += upd

    tile = pl.BlockSpec((tm, D), lambda i: (i, 0))
    return pl.pallas_call(
        body, name="final_loss", grid=(S // tm,),
        in_specs=[tile, pl.BlockSpec((1, D), lambda i: (0, 0)), tile],
        out_specs=[tile, pl.BlockSpec((8, D), lambda i: (0, 0))],
        out_shape=[jax.ShapeDtypeStruct((S, D), f32), jax.ShapeDtypeStruct((8, D), f32)],
        compiler_params=_params(("arbitrary",)),
    )(x, fw, tgt)


def gate_bwd(dx, y, gate):
    S = dx.shape[0]
    tm = 512

    def body(dx_ref, y_ref, g_ref, dy_ref, st_ref):
        i = pl.program_id(0)
        dxv = dx_ref[...]
        dy_ref[...] = (g_ref[...] * dxv).astype(bf16)
        upd = jnp.concatenate([jnp.sum(dxv * y_ref[...], axis=0, keepdims=True), jnp.zeros((7, D), f32)], axis=0)

        @pl.when(i == 0)
        def _():
            st_ref[...] = upd

        @pl.when(i > 0)
        def _():
            st_ref[...] += upd

    tile = pl.BlockSpec((tm, D), lambda i: (i, 0))
    return pl.pallas_call(
        body, name="gate_bwd", grid=(S // tm,),
        in_specs=[tile, tile, pl.BlockSpec((1, D), lambda i: (0, 0))],
        out_specs=[tile, pl.BlockSpec((8, D), lambda i: (0, 0))],
        out_shape=[jax.ShapeDtypeStruct((S, D), bf16), jax.ShapeDtypeStruct((8, D), f32)],
        compiler_params=_params(("arbitrary",)),
    )(dx, y, gate)


def _chunk_mats(tm):
    r, c = _iota((tm, tm), 0), _iota((tm, tm), 1)
    same = jnp.right_shift(r, 6) == jnp.right_shift(c, 6)
    ltri = jnp.where(same & (c <= r), 1.0, 0.0).astype(f32)
    utri = jnp.where(same & (c >= r), 1.0, 0.0).astype(f32)
    bsame = jnp.where(same, 1.0, 0.0).astype(f32)
    return ltri, utri, bsame


def _gdn_scalars(ba, alog, dtb, ltri, bsame):
    beta = _sigmoid(ba[:, 0:16])
    u = ba[:, 16:32] + dtb
    neg_a = -jnp.exp(alog)
    g = neg_a * _softplus(u)
    gc = _nn(ltri, g, HI)
    glast = _nn(bsame, g, HI)
    return beta, u, neg_a, g, gc, glast


def _conv_taps(p_ref, halo_ref, first, gi):
    cs = slice(gi * 128, (gi + 1) * 128)
    cur = p_ref[:, cs]
    hal = jnp.where(first, 0.0, halo_ref[:, cs])
    ext = jnp.concatenate([hal, cur], axis=0)
    return [cur] + [pltpu.roll(ext, s, 0)[8:] for s in range(1, CONV_K)]


def _conv(taps, w):
    cv = taps[0] * w[3:4]
    for s in range(1, CONV_K):
        cv = cv + taps[s] * w[3 - s:4 - s]
    return cv


def _l2n(x):
    return x * lax.rsqrt(jnp.sum(x * x, axis=-1, keepdims=True) + EPS)


def _gdn_in_specs(tm, S):
    nb8 = tm // 8
    return [pl.BlockSpec((tm, G_CONV), lambda i: (i, 0)),
            pl.BlockSpec((8, G_CONV), lambda i: (jnp.maximum(i * nb8 - 1, 0), 0)),
            pl.BlockSpec((tm, 128), lambda i: (i, G_BA0 // 128))]


def gdn_pre(proj, conv_w, alog, dtb):
    S = proj.shape[0]
    tm = 256
    nch = tm // CHUNK

    def body(p_ref, halo_ref, ba_ref, w_ref, al_ref, dt_ref,
             q_ref, k_ref, kb_ref, kbg_ref, vb_ref, qd_ref, kd_ref, d_ref, gl_ref):
        first = pl.program_id(0) == 0
        ltri, _, bsame = _chunk_mats(tm)
        beta, _, _, _, gc, glast = _gdn_scalars(ba_ref[...], al_ref[...], dt_ref[...], ltri, bsame)
        eg, ek, egl = jnp.exp(gc), jnp.exp(glast - gc), jnp.exp(glast)
        eye = jnp.where(_iota((16, 16), 0) == _iota((16, 16), 1), 1.0, 0.0).astype(f32)
        gct = _nt(eye, gc, HI)
        low = _iota((CHUNK, CHUNK), 0) >= _iota((CHUNK, CHUNK), 1)

        def act(gi):
            return _silu(_conv(_conv_taps(p_ref, halo_ref, first, gi), w_ref[:, gi * 128:(gi + 1) * 128]))

        for j in range(GQK_H):
            js = slice(j * 128, (j + 1) * 128)
            qn = _l2n(act(j)) * (GHD ** -0.5)
            kn = _l2n(act(GQK_H + j))
            q_ref[:, js] = qn.astype(bf16)
            k_ref[:, js] = kn.astype(bf16)
            for e in range(2):
                h = 2 * j + e
                hs = slice(h * 128, (h + 1) * 128)
                v = act(2 * GQK_H + h)
                bh, egh, ekh = beta[:, h:h + 1], eg[:, h:h + 1], ek[:, h:h + 1]
                kbv = kn * bh
                kb_ref[:, hs] = kbv.astype(bf16)
                kbg_ref[:, hs] = (kbv * egh).astype(bf16)
                vb_ref[:, hs] = (v * bh).astype(bf16)
                qd_ref[:, hs] = (qn * egh).astype(bf16)
                kd_ref[:, hs] = (kn * ekh).astype(bf16)
                for c in range(nch):
                    rs = slice(c * CHUNK, (c + 1) * CHUNK)
                    diff = gc[rs, h:h + 1] - gct[h:h + 1, rs]
                    d_ref[rs, h * CHUNK:(h + 1) * CHUNK] = jnp.where(low, jnp.exp(jnp.where(low, diff, 0.0)), 0.0)
                    gl_ref[c * 8:(c + 1) * 8, hs] = jnp.broadcast_to(egl[c * CHUNK:c * CHUNK + 8, h:h + 1], (8, 128))

    full = lambda shape: pl.BlockSpec(shape, lambda i: (0, 0))
    t1 = pl.BlockSpec((tm, 1024), lambda i: (i, 0))
    t2 = pl.BlockSpec((tm, 2048), lambda i: (i, 0))
    sd = jax.ShapeDtypeStruct
    return pl.pallas_call(
        body, name="gdn_pre", grid=(S // tm,),
        in_specs=_gdn_in_specs(tm, S) + [full((CONV_K, G_CONV)), full((1, 16)), full((1, 16))],
        out_specs=[t1, t1, t2, t2, t2, t2, t2, t1, pl.BlockSpec((tm // 8, 2048), lambda i: (i, 0))],
        out_shape=[sd((S, 1024), bf16)] * 2 + [sd((S, 2048), bf16)] * 5 + [sd((S, 1024), f32), sd((S // 8, 2048), f32)],
        compiler_params=_params(("parallel",)),
    )(proj, proj, proj, conv_w, alog, dtb)


def _bnn(a, b):
    return lax.dot_general(a, b, (((2,), (1,)), ((0,), (0,))), preferred_element_type=f32)


def _bnt(a, b):
    return lax.dot_general(a, b, (((2,), (2,)), ((0,), (0,))), preferred_element_type=f32)


def _btn(a, b):
    return lax.dot_general(a, b, (((1,), (1,)), ((0,), (0,))), preferred_element_type=f32)


def _split(a):
    hi = a.astype(bf16)
    return hi, (a - hi.astype(f32)).astype(bf16)


def _cat3(h, l, axis, lhs):
    return jnp.concatenate([h, h, l] if lhs else [h, l, h], axis=axis)


def _tri_inv_b(L):
    eye = jnp.where(_iota((1, CHUNK, CHUNK), 1) == _iota((1, CHUNK, CHUNK), 2), 1.0, 0.0).astype(f32)
    P = -L
    T = eye + P
    ph, pl_ = _split(P)
    for _ in range(5):
        P = _bnn(_cat3(ph, pl_, 2, True), _cat3(ph, pl_, 1, False))
        ph, pl_ = _split(P)
        th, tl = _split(T)
        T = T + _bnn(_cat3(th, tl, 2, True), _cat3(ph, pl_, 1, False))
    return T


GTB = 512
GQH_FWD, GQH_BWD = 1, 2


def _gdn_slices(ncb, gnv):
    pairs = [(c, e) for c in range(ncb) for e in range(gnv)]
    rs = lambda c: slice(c * CHUNK, (c + 1) * CHUNK)
    cs = lambda e: slice(e * 128, (e + 1) * 128)
    ds_ = lambda e: slice(e * CHUNK, (e + 1) * CHUNK)
    ks = lambda e: slice((e // 2) * 128, (e // 2 + 1) * 128)
    return pairs, rs, cs, ds_, ks


def gdn_fwd(q, k, kb, kbg, vb, qd, kd, dm, gl8, comm=None):
    S = q.shape[0]
    nb, ncb = S // GTB, GTB // CHUNK
    GQH, GNV = GQH_FWD, 2 * GQH_FWD
    pairs, rs, cs, ds_, ks = _gdn_slices(ncb, GNV)

    def body(q_ref, k_ref, kb_ref, kbg_ref, vb_ref, qd_ref, kd_ref, d_ref, gl_ref,
             o_ref, w_ref, at_ref, t_ref, vn_ref, st_ref, state, u_scr):
        @pl.when(pl.program_id(1) == 0)
        def _():
            state[...] = jnp.zeros_like(state)

        stk = lambda ref, lanes: jnp.stack([ref[rs(c), lanes(e)] for c, e in pairs])
        kq = stk(k_ref, ks)
        dmat = stk(d_ref, ds_)
        strict = _iota((1, CHUNK, CHUNK), 1) > _iota((1, CHUNK, CHUNK), 2)
        T = _tri_inv_b(jnp.where(strict, _bnt(stk(kb_ref, cs), kq) * dmat, 0.0))
        tb = T.astype(bf16)
        u_scr[...] = _bnn(tb, stk(vb_ref, cs))
        wb = _bnn(tb, stk(kbg_ref, cs)).astype(bf16)
        per_qk = lambda ref: jnp.stack([ref[rs(c), ks(e)] for c, e in pairs if e % 2 == 0])
        qk = _bnt(per_qk(q_ref), per_qk(k_ref))
        for b, (c, e) in enumerate(pairs):
            w_ref[rs(c), cs(e)] = wb[b]
            at_ref[rs(c), ds_(e)] = (qk[b // 2] * dmat[b]).astype(bf16)
            t_ref[rs(c), ds_(e)] = T[b]
        for b, (c, e) in enumerate(pairs):
            sb = state[e].astype(bf16)
            vnb = (u_scr[b] - _nn(w_ref[rs(c), cs(e)], sb)).astype(bf16)
            o_ref[rs(c), cs(e)] = _nn(qd_ref[rs(c), cs(e)], sb) + _nn(at_ref[rs(c), ds_(e)], vnb)
            st_ref[c * 128:(c + 1) * 128, cs(e)] = sb
            state[e] = state[e] * gl_ref[c * 8:c * 8 + 1, cs(e)] + _tn(kd_ref[rs(c), cs(e)], vnb)
            vn_ref[rs(c), cs(e)] = vnb

    b1 = pl.BlockSpec((GTB, 128 * GQH), lambda j, i: (i, j))
    b2 = pl.BlockSpec((GTB, 256 * GQH), lambda j, i: (i, j))
    sd = jax.ShapeDtypeStruct
    return _call(
        body, name="gdn_fwd", grid=(GQK_H // GQH, nb),
        in_specs=[b1, b1, b2, b2, b2, b2, b2, b1, pl.BlockSpec((GTB // 8, 256 * GQH), lambda j, i: (i, j))],
        out_specs=[b2, b2, b1, b1, b2, pl.BlockSpec((ncb * 128, 256 * GQH), lambda j, i: (i, j))],
        out_shape=[sd((S, 2048), f32), sd((S, 2048), bf16), sd((S, 1024), bf16), sd((S, 1024), f32),
                   sd((S, 2048), bf16), sd((S // CHUNK * 128, 2048), bf16)],
        scratch_shapes=[pltpu.VMEM((GNV, 128, 128), f32), pltpu.VMEM((GNV * ncb, CHUNK, 128), f32)],
        sem=("parallel", "arbitrary"), args=(q, k, kb, kbg, vb, qd, kd, dm, gl8), comm=comm)


def gdn_bwd(do, q, k, kb, kbg, vb, qd, kd, dm, gl8, w, at, T, vn, st, comm=None):
    S = q.shape[0]
    nb, ncb = S // GTB, GTB // CHUNK
    GQH, GNV = GQH_BWD, 2 * GQH_BWD
    pairs, rs, cs, ds_, ks = _gdn_slices(ncb, GNV)

    def body(do_ref, q_ref, k_ref, kb_ref, kbg_ref, vb_ref, qd_ref, kd_ref, d_ref, gl_ref, w_ref, at_ref, t_ref, vn_ref, st_ref,
             dq_ref, dk_ref, dkb_ref, dkbg_ref, dvb_ref, dqd_ref, dkd_ref, dgc_ref, dstate, dvn_scr, dw_scr, dat_scr, dgl_scr):
        @pl.when(pl.program_id(1) == 0)
        def _():
            dstate[...] = jnp.zeros_like(dstate)

        for b, (c, e) in reversed(list(enumerate(pairs))):
            dob = do_ref[rs(c), cs(e)].astype(bf16)
            sb = st_ref[c * 128:(c + 1) * 128, cs(e)]
            vnb = vn_ref[rs(c), cs(e)]
            gl = gl_ref[c * 8:c * 8 + 1, cs(e)]
            dS = dstate[e]
            dsb = dS.astype(bf16)
            dvnb = (_tn(at_ref[rs(c), ds_(e)], dob) + _nn(kd_ref[rs(c), cs(e)], dsb)).astype(bf16)
            dvn_scr[b] = dvnb
            dat_scr[b] = _nt(dob, vnb)
            dqd_ref[rs(c), cs(e)] = _nt(dob, sb)
            dkd_ref[rs(c), cs(e)] = _nt(vnb, dsb)
            dw_scr[b] = (-_nt(dvnb, sb)).astype(bf16)
            dgl = jnp.sum(jnp.sum(dS * sb.astype(f32), axis=1, keepdims=True), axis=0, keepdims=True)
            dgl_scr[b] = jnp.broadcast_to(dgl * gl, (8, 128))
            dstate[e] = gl * dS + _tn(qd_ref[rs(c), cs(e)], dob) - _tn(w_ref[rs(c), cs(e)], dvnb)

        stk = lambda ref, lanes: jnp.stack([ref[rs(c), lanes(e)] for c, e in pairs])
        kq, qq = stk(k_ref, ks), stk(q_ref, ks)
        kbb = stk(kb_ref, cs)
        Tm = stk(t_ref, ds_)
        tb = Tm.astype(bf16)
        dvn, dw = dvn_scr[...], dw_scr[...]
        dT = _bnt(dvn, stk(vb_ref, cs)) + _bnt(dw, stk(kbg_ref, cs))
        dvb, dkbg = _btn(tb, dvn), _btn(tb, dw)
        th, tl = _split(Tm)
        xh, xl = _split(_bnt(_cat3(*_split(dT), 2, True), _cat3(th, tl, 2, False)))
        dL = -_btn(_cat3(th, tl, 1, True), _cat3(xh, xl, 1, False))
        dmat = stk(d_ref, ds_)
        strict = _iota((1, CHUNK, CHUNK), 1) > _iota((1, CHUNK, CHUNK), 2)
        dA = jnp.where(strict, dL * dmat, 0.0)
        dB = dat_scr[...] * dmat
        dAb, dBb = dA.astype(bf16), dB.astype(bf16)
        dkb = _bnn(dAb, kq)
        dkc = _btn(dAb, kbb) + _btn(dBb, qq)
        dqc = _bnn(dBb, kq)
        M = dA * _bnt(kbb, kq) + dB * _bnt(qq, kq)
        mh, ml = _split(M)
        colsum = _btn(jnp.concatenate([mh, ml], axis=1), jnp.ones((GNV * ncb, 2 * CHUNK, 128), bf16))
        lastrow = _iota((1, CHUNK, 128), 1) == CHUNK - 1
        for b, (c, e) in enumerate(pairs):
            dvb_ref[rs(c), cs(e)] = dvb[b]
            dkbg_ref[rs(c), cs(e)] = dkbg[b]
            dkb_ref[rs(c), cs(e)] = dkb[b]
            dgc_ref[rs(c), cs(e)] = (jnp.sum(M[b], axis=1, keepdims=True) - colsum[b]
                                     + jnp.where(lastrow[0], dgl_scr[b][0:1, :], 0.0))
        for b, (c, e) in enumerate(pairs):
            if e % 2 == 0:
                dq_ref[rs(c), ks(e)] = dqc[b] + dqc[b + 1]
                dk_ref[rs(c), ks(e)] = dkc[b] + dkc[b + 1]

    b1 = pl.BlockSpec((GTB, 128 * GQH), lambda j, i: (nb - 1 - i, j))
    b2 = pl.BlockSpec((GTB, 256 * GQH), lambda j, i: (nb - 1 - i, j))
    sd = jax.ShapeDtypeStruct
    return _call(
        body, name="gdn_bwd", grid=(GQK_H // GQH, nb),
        in_specs=[b2, b1, b1, b2, b2, b2, b2, b2, b1, pl.BlockSpec((GTB // 8, 256 * GQH), lambda j, i: (nb - 1 - i, j)),
                  b2, b1, b1, b2, pl.BlockSpec((ncb * 128, 256 * GQH), lambda j, i: (nb - 1 - i, j))],
        out_specs=[b1, b1, b2, b2, b2, b2, b2, b2],
        out_shape=[sd((S, 1024), f32)] * 2 + [sd((S, 2048), f32)] * 6,
        scratch_shapes=[pltpu.VMEM((GNV, 128, 128), f32), pltpu.VMEM((GNV * ncb, CHUNK, 128), bf16),
                        pltpu.VMEM((GNV * ncb, CHUNK, 128), bf16), pltpu.VMEM((GNV * ncb, CHUNK, CHUNK), f32),
                        pltpu.VMEM((GNV * ncb, 8, 128), f32)],
        sem=("parallel", "arbitrary"), args=(do, q, k, kb, kbg, vb, qd, kd, dm, gl8, w, at, T, vn, st), comm=comm)


def gdn_onorm(o, proj, nw):
    S = o.shape[0]
    tm = 256

    def body(o_ref, z_ref, nw_ref, o2_ref):
        for h in range(GV_H):
            hs = slice(h * 128, (h + 1) * 128)
            oh = o_ref[:, hs]
            r = lax.rsqrt(jnp.mean(oh * oh, axis=-1, keepdims=True) + EPS)
            o2_ref[:, hs] = (((oh * r) * nw_ref[...]) * _silu(z_ref[:, hs])).astype(bf16)

    t2 = pl.BlockSpec((tm, 2048), lambda i: (i, 0))
    return pl.pallas_call(
        body, name="gdn_onorm", grid=(S // tm,),
        in_specs=[t2, pl.BlockSpec((tm, 2048), lambda i: (i, G_Z0 // 2048)), pl.BlockSpec((1, 128), lambda i: (0, 0))],
        out_specs=t2, out_shape=jax.ShapeDtypeStruct((S, 2048), bf16),
        compiler_params=_params(("parallel",)),
    )(o, proj, nw)


def gdn_onorm_bwd(do2, o, proj, nw):
    S = o.shape[0]
    tm = 256

    def body(d_ref, o_ref, z_ref, nw_ref, do_ref, dz_ref, st_ref):
        i = pl.program_id(0)
        acc = jnp.zeros((1, 128), f32)
        for h in range(GV_H):
            hs = slice(h * 128, (h + 1) * 128)
            oh, z, d2 = o_ref[:, hs], z_ref[:, hs], d_ref[:, hs]
            r = lax.rsqrt(jnp.mean(oh * oh, axis=-1, keepdims=True) + EPS)
            on = oh * r
            dt = d2 * _silu(z)
            dz_ref[:, hs] = (d2 * (on * nw_ref[...]) * _dsilu(z)).astype(bf16)
            don = dt * nw_ref[...]
            acc = acc + jnp.sum(dt * on, axis=0, keepdims=True)
            do_ref[:, hs] = r * (don - on * jnp.mean(don * on, axis=-1, keepdims=True))
        upd = jnp.concatenate([acc, jnp.zeros((7, 128), f32)], axis=0)

        @pl.when(i == 0)
        def _():
            st_ref[...] = upd

        @pl.when(i > 0)
        def _():
            st_ref[...] += upd

    t2 = pl.BlockSpec((tm, 2048), lambda i: (i, 0))
    sd = jax.ShapeDtypeStruct
    return pl.pallas_call(
        body, name="gdn_onorm_bwd", grid=(S // tm,),
        in_specs=[t2, t2, pl.BlockSpec((tm, 2048), lambda i: (i, G_Z0 // 2048)), pl.BlockSpec((1, 128), lambda i: (0, 0))],
        out_specs=[t2, t2, pl.BlockSpec((8, 128), lambda i: (0, 0))],
        out_shape=[sd((S, 2048), f32), sd((S, 2048), bf16), sd((8, 128), f32)],
        compiler_params=_params(("arbitrary",)),
    )(do2, o, proj, nw)


def gdn_pre_bwd(proj, conv_w, alog, dtb, dq, dk, dkb, dkbg, dvb, dqd, dkd, dgcd):
    S = proj.shape[0]
    tm = 128

    def body(p_ref, halo_ref, ba_ref, w_ref, al_ref, dt_ref, dq_ref, dk_ref, dkb_ref, dkbg_ref, dvb_ref, dqd_ref, dkd_ref, dgc_ref,
             dcv_ref, dba_ref, st_ref):
        i = pl.program_id(0)
        first = i == 0
        ltri, utri, bsame = _chunk_mats(tm)
        beta, u, neg_a, g, gc, glast = _gdn_scalars(ba_ref[...], al_ref[...], dt_ref[...], ltri, bsame)
        eg, ek = jnp.exp(gc), jnp.exp(glast - gc)
        lane16 = _iota((tm, 16), 1)
        dgc_all = jnp.zeros((tm, 16), f32)
        rkd_all = jnp.zeros((tm, 16), f32)
        dbeta_all = jnp.zeros((tm, 16), f32)

        def pre(gi):
            return _conv(_conv_taps(p_ref, halo_ref, first, gi), w_ref[:, gi * 128:(gi + 1) * 128])

        def l2n_bwd(xt, dy):
            r = lax.rsqrt(jnp.sum(xt * xt, axis=-1, keepdims=True) + EPS)
            y = xt * r
            return r * (dy - y * jnp.sum(dy * y, axis=-1, keepdims=True))

        for j in range(GQK_H):
            js = slice(j * 128, (j + 1) * 128)
            cvq, cvk = pre(j), pre(GQK_H + j)
            qt, kt = _silu(cvq), _silu(cvk)
            qn = _l2n(qt) * (GHD ** -0.5)
            kn = _l2n(kt)
            dq_tot, dk_tot = dq_ref[:, js], dk_ref[:, js]
            for e in range(2):
                h = 2 * j + e
                hs = slice(h * 128, (h + 1) * 128)
                gv = 2 * GQK_H + h
                cvv = pre(gv)
                v = _silu(cvv)
                bh, egh, ekh = beta[:, h:h + 1], eg[:, h:h + 1], ek[:, h:h + 1]
                dkbg, dkd, dqd, dvb = dkbg_ref[:, hs], dkd_ref[:, hs], dqd_ref[:, hs], dvb_ref[:, hs]
                dkb_t = dkb_ref[:, hs] + dkbg * egh
                dk_tot = dk_tot + dkb_t * bh + dkd * ekh
                dq_tot = dq_tot + dqd * egh
                dcv_ref[:, gv * 128:(gv + 1) * 128] = (dvb * bh) * _dsilu(cvv)
                dbeta = jnp.sum(dkb_t * kn, axis=-1, keepdims=True) + jnp.sum(dvb * v, axis=-1, keepdims=True)
                rkd = jnp.sum(dkd * (kn * ekh), axis=-1, keepdims=True)
                dgc = (dgc_ref[:, hs][:, 0:1] + jnp.sum(dkbg * (kn * bh * egh), axis=-1, keepdims=True)
                       + jnp.sum(dqd * (qn * egh), axis=-1, keepdims=True) - rkd)
                sel = lane16 == h
                dgc_all = dgc_all + jnp.where(sel, dgc, 0.0)
                rkd_all = rkd_all + jnp.where(sel, rkd, 0.0)
                dbeta_all = dbeta_all + jnp.where(sel, dbeta, 0.0)
            dcv_ref[:, js] = l2n_bwd(qt, dq_tot * (GHD ** -0.5)) * _dsilu(cvq)
            ks = slice((GQK_H + j) * 128, (GQK_H + j + 1) * 128)
            dcv_ref[:, ks] = l2n_bwd(kt, dk_tot) * _dsilu(cvk)

        islast = jnp.bitwise_and(_iota((tm, 16), 0), CHUNK - 1) == CHUNK - 1
        dgc_all = dgc_all + jnp.where(islast, _nn(bsame, rkd_all, HI), 0.0)
        dg = _nn(utri, dgc_all, HI)
        da = dg * neg_a * _sigmoid(u)
        db = dbeta_all * beta * (1.0 - beta)
        r16, c128 = _iota((16, 128), 0), _iota((16, 128), 1)
        pb = jnp.where(c128 == r16, 1.0, 0.0).astype(f32)
        pa = jnp.where(c128 == r16 + 16, 1.0, 0.0).astype(f32)
        dba_ref[...] = _nn(db, pb, HI) + _nn(da, pa, HI)
        upd = jnp.concatenate([jnp.sum(dg * g, axis=0, keepdims=True), jnp.sum(da, axis=0, keepdims=True),
                               jnp.zeros((6, 16), f32)], axis=0)

        @pl.when(i == 0)
        def _():
            st_ref[...] = upd

        @pl.when(i > 0)
        def _():
            st_ref[...] += upd

    full = lambda shape: pl.BlockSpec(shape, lambda i: (0, 0))
    t1 = pl.BlockSpec((tm, 1024), lambda i: (i, 0))
    t2 = pl.BlockSpec((tm, 2048), lambda i: (i, 0))
    sd = jax.ShapeDtypeStruct
    return pl.pallas_call(
        body, name="gdn_pre_bwd", grid=(S // tm,),
        in_specs=_gdn_in_specs(tm, S) + [full((CONV_K, G_CONV)), full((1, 16)), full((1, 16)), t1, t1] + [t2] * 6,
        out_specs=[pl.BlockSpec((tm, G_CONV), lambda i: (i, 0)), pl.BlockSpec((tm, 128), lambda i: (i, 0)), full((8, 16))],
        out_shape=[sd((S, G_CONV), f32), sd((S, 128), f32), sd((8, 16), f32)],
        compiler_params=_params(("arbitrary",)),
    )(proj, proj, proj, conv_w, alog, dtb, dq, dk, dkb, dkbg, dvb, dqd, dkd, dgcd)


def gdn_conv_bwd(proj, conv_w, dcv, dz, dba):
    S = proj.shape[0]
    tm = 256
    nb, nb8 = S // tm, tm // 8

    def body(p_ref, halo_ref, w_ref, dcv_ref, nxt_ref, dz_ref, dba_ref, dp_ref, dw_ref):
        i = pl.program_id(0)
        first, last = i == 0, i == nb - 1
        for gi in range(G_CONV // 128):
            cs = slice(gi * 128, (gi + 1) * 128)
            taps = _conv_taps(p_ref, halo_ref, first, gi)
            cur = dcv_ref[:, cs]
            ext = jnp.concatenate([cur, jnp.where(last, 0.0, nxt_ref[:, cs])], axis=0)
            w = w_ref[:, cs]
            dp = cur * w[3:4]
            rows = [jnp.sum(cur * taps[3 - kk], axis=0, keepdims=True) for kk in range(CONV_K)]
            for s in range(1, CONV_K):
                dp = dp + pltpu.roll(ext, tm + 8 - s, 0)[:tm] * w[3 - s:4 - s]
            dp_ref[:, cs] = dp.astype(bf16)
            upd = jnp.concatenate(rows + [jnp.zeros((4, 128), f32)], axis=0)

            @pl.when(first)
            def _():
                dw_ref[:, cs] = upd

            @pl.when(i > 0)
            def _():
                dw_ref[:, cs] += upd

        dp_ref[:, G_Z0:G_BA0] = dz_ref[...]
        dp_ref[:, G_BA0:G_INP] = dba_ref[...].astype(bf16)

    sd = jax.ShapeDtypeStruct
    return pl.pallas_call(
        body, name="gdn_conv_bwd", grid=(nb,),
        in_specs=[pl.BlockSpec((tm, G_CONV), lambda i: (i, 0)),
                  pl.BlockSpec((8, G_CONV), lambda i: (jnp.maximum(i * nb8 - 1, 0), 0)),
                  pl.BlockSpec((CONV_K, G_CONV), lambda i: (0, 0)),
                  pl.BlockSpec((tm, G_CONV), lambda i: (i, 0)),
                  pl.BlockSpec((8, G_CONV), lambda i: (jnp.minimum((i + 1) * nb8, S // 8 - 1), 0)),
                  pl.BlockSpec((tm, 2048), lambda i: (i, 0)), pl.BlockSpec((tm, 128), lambda i: (i, 0))],
        out_specs=[pl.BlockSpec((tm, G_INP), lambda i: (i, 0)), pl.BlockSpec((8, G_CONV), lambda i: (0, 0))],
        out_shape=[sd((S, G_INP), bf16), sd((8, G_CONV), f32)],
        compiler_params=_params(("arbitrary",)),
    )(proj, proj, conv_w, dcv, dcv, dz, dba)


def _half_mean(t, lo_half):
    m0 = jnp.sum(jnp.where(lo_half, t, 0.0), axis=-1, keepdims=True)
    m1 = jnp.sum(jnp.where(lo_half, 0.0, t), axis=-1, keepdims=True)
    return jnp.where(lo_half, m0, m1) * (1.0 / F_HD)


def _split3(c):
    hi = c.astype(bf16).astype(f32)
    mid = (c - hi).astype(bf16).astype(f32)
    lo = (c - hi - mid).astype(bf16).astype(f32)
    return hi, mid, lo


def fox_pre(proj, fbias, qw2, kw2):
    S = proj.shape[0]
    tm = 256

    def body(q_ref, k_ref, v_ref, f_ref, fb_ref, qw_ref, kw_ref, qa_ref, ka_ref, vb_ref, carry):
        @pl.when(pl.program_id(0) == 0)
        def _():
            carry[...] = jnp.zeros_like(carry)

        logf = -_softplus(-(f_ref[:, 0:16] + fb_ref[...]))
        ltri = jnp.where(_iota((tm, tm), 1) <= _iota((tm, tm), 0), 1.0, 0.0).astype(f32)
        cum = _nn(ltri, logf, HI) + carry[0:1, :]
        carry[0:1, :] = cum[tm - 1:tm, :]
        lane = _iota((tm, 128), 1)
        lo_half = lane < F_HD
        for p in range(F_H // 2):
            ps = slice(p * 128, (p + 1) * 128)
            for src, w_ref, dst, is_q in ((q_ref, qw_ref, qa_ref, True), (k_ref, kw_ref, ka_ref, False)):
                x = src[:, ps]
                xn = x * lax.rsqrt(_half_mean(x * x, lo_half) + EPS) * w_ref[...]
                if is_q:
                    xn = xn * (F_HD ** -0.5)
                for e in range(2):
                    h = 2 * p + e
                    base = xn if e == 0 else pltpu.roll(xn, F_HD, 1)
                    hi, mid, lo = _split3(cum[:, h:h + 1])
                    pieces = jnp.where(lane == 64, hi, 0.0) + jnp.where(lane == 65, mid, 0.0) + jnp.where(lane == 66, lo, 0.0)
                    if is_q:
                        ext = pieces + jnp.where((lane >= 67) & (lane <= 69), 1.0, 0.0)
                    else:
                        ext = jnp.where((lane >= 64) & (lane <= 66), 1.0, 0.0) - pltpu.roll(pieces, 3, 1)
                    dst[:, h * 128:(h + 1) * 128] = jnp.where(lo_half, base, ext).astype(bf16)
        one = jnp.where(lane == F_HD, 1.0, 0.0)
        for p in range(F_H // 2):
            vv = v_ref[:, p * 128:(p + 1) * 128]
            vb_ref[:, (2 * p) * 128:(2 * p + 1) * 128] = jnp.where(lo_half, vv, one).astype(bf16)
            vb_ref[:, (2 * p + 1) * 128:(2 * p + 2) * 128] = jnp.where(lo_half, pltpu.roll(vv, F_HD, 1), one).astype(bf16)

    t1 = lambda c: pl.BlockSpec((tm, 1024), lambda i: (i, c))
    vec = lambda n: pl.BlockSpec((1, n), lambda i: (0, 0))
    sd = jax.ShapeDtypeStruct
    return pl.pallas_call(
        body, name="fox_pre", grid=(S // tm,),
        in_specs=[t1(0), t1(1), t1(2), pl.BlockSpec((tm, 128), lambda i: (i, F_F0 // 128)), vec(16), vec(128), vec(128)],
        out_specs=[pl.BlockSpec((tm, 2048), lambda i: (i, 0))] * 3,
        out_shape=[sd((S, 2048), bf16)] * 3,
        scratch_shapes=[pltpu.VMEM((8, 16), f32)],
        compiler_params=_params(("arbitrary",)),
    )(proj, proj, proj, proj, fbias, qw2, kw2)


FTQ = 512
FHS = 4


def fox_attn(qa, ka, v, comm=None):
    S = qa.shape[0]
    nq = S // FTQ

    live = [(i, j) for i in range(nq) for j in range(i + 1)]
    qi_tab = jnp.asarray([i for i, _ in live], jnp.int32)
    kj_tab = jnp.asarray([j for _, j in live], jnp.int32)

    def body(qi_ref, kj_ref, q_ref, k_ref, v_ref, o_ref, lse_ref, m_scr, acc_scr):
        t = pl.program_id(1)
        i, j = qi_ref[t], kj_ref[t]

        @pl.when(j == 0)
        def _():
            m_scr[...] = jnp.full_like(m_scr, NEG)
            acc_scr[...] = jnp.zeros_like(acc_scr)

        def step(diagonal):
            for e in range(FHS):
                es = slice(e * 128, (e + 1) * 128)
                s = _nt(q_ref[:, es], k_ref[:, es])
                if diagonal:
                    s = jnp.where(_iota((FTQ, FTQ), 0) >= _iota((FTQ, FTQ), 1), s, NEG)
                m_old = m_scr[e]
                m_new = jnp.maximum(m_old, jnp.max(s, axis=-1, keepdims=True))
                p = jnp.exp(s - m_new[:, 0:1])
                acc_scr[e] = acc_scr[e] * jnp.exp(m_old - m_new) + _nn(p.astype(bf16), v_ref[:, es])
                m_scr[e] = m_new

        pl.when(j < i)(functools.partial(step, False))

        @pl.when(j == i)
        def _():
            step(True)
            for e in range(FHS):
                vs = slice(e * F_HD, (e + 1) * F_HD)
                acc = acc_scr[e]
                l = acc[:, F_HD:F_HD + 1]
                o_ref[:, vs] = acc[:, 0:F_HD] / l
                lse_ref[:, vs] = m_scr[e][:, 0:F_HD] + jnp.log(l)

    sd = jax.ShapeDtypeStruct
    qo = pl.BlockSpec((FTQ, F_HD * FHS), lambda p, t, qi, kj: (qi[t], p))
    kv = pl.BlockSpec((FTQ, 128 * FHS), lambda p, t, qi, kj: (kj[t], p))
    return _call(
        body, name="fox_attn", grid=(F_H // FHS, len(live)),
        in_specs=[pl.BlockSpec((FTQ, 128 * FHS), lambda p, t, qi, kj: (qi[t], p)), kv, kv],
        out_specs=[qo, qo],
        out_shape=[sd((S, 1024), f32), sd((S, 1024), f32)],
        scratch_shapes=[pltpu.VMEM((FHS, FTQ, 128), f32), pltpu.VMEM((FHS, FTQ, 128), f32)],
        sem=("parallel", "arbitrary"), args=(qa, ka, v), comm=comm, prefetch=(qi_tab, kj_tab))


def fox_attn_bwd(qa, ka, v, do, lse, delta, comm=None):
    S = qa.shape[0]
    nq = S // FTQ

    live = [(j, i) for j in range(nq) for i in range(j, nq)]
    kj_tab = jnp.asarray([j for j, _ in live], jnp.int32)
    qi_tab = jnp.asarray([i for _, i in live], jnp.int32)

    def body(kj_ref, qi_ref, q_ref, k_ref, v_ref, do_ref, lse_ref, dl_ref, dq_ref, dk_ref, dv_ref, dk_scr, dv_scr):
        t = pl.program_id(1)
        j, i = kj_ref[t], qi_ref[t]

        @pl.when(t == 0)
        def _():
            dq_ref[...] = jnp.zeros_like(dq_ref)

        @pl.when(i == j)
        def _():
            dk_scr[...] = jnp.zeros_like(dk_scr)
            dv_scr[...] = jnp.zeros_like(dv_scr)

        def step(diagonal):
            rows = pl.ds(pl.multiple_of(i * FTQ, FTQ), FTQ)
            for e in range(FHS):
                es, vs = slice(e * 128, (e + 1) * 128), slice(e * F_HD, (e + 1) * F_HD)
                qe, ke = q_ref[:, es], k_ref[:, es]
                dob = do_ref[:, vs]
                s = _nt(qe, ke)
                if diagonal:
                    s = jnp.where(_iota((FTQ, FTQ), 0) >= _iota((FTQ, FTQ), 1), s, NEG)
                p = jnp.exp(s - lse_ref[:, e * F_HD:e * F_HD + 1])
                ds = p * (_nt(dob, v_ref[:, e * 128:e * 128 + F_HD]) - dl_ref[:, e * F_HD:e * F_HD + 1])
                dsb = ds.astype(bf16)
                dv_scr[e] += _tn(dob, p.astype(bf16))
                dk_scr[e] += _tn(qe, dsb)
                dq_ref[rows, es] += _nn(dsb, ke)

        pl.when(i > j)(functools.partial(step, False))
        pl.when(i == j)(functools.partial(step, True))

        @pl.when(i == nq - 1)
        def _():
            for e in range(FHS):
                dk_ref[:, e * 128:(e + 1) * 128] = dk_scr[e].T
                dv_ref[:, e * F_HD:(e + 1) * F_HD] = dv_scr[e].T

    sd = jax.ShapeDtypeStruct
    qi = lambda w: pl.BlockSpec((FTQ, w * FHS), lambda p, t, kj_, qi_: (qi_[t], p))
    kj = lambda w: pl.BlockSpec((FTQ, w * FHS), lambda p, t, kj_, qi_: (kj_[t], p))
    return _call(
        body, name="fox_attn_bwd", grid=(F_H // FHS, len(live)),
        in_specs=[qi(128), kj(128), kj(128), qi(F_HD), qi(F_HD), qi(F_HD)],
        out_specs=[pl.BlockSpec((S, 128 * FHS), lambda p, t, kj_, qi_: (0, p)), kj(128), kj(F_HD)],
        out_shape=[sd((S, 2048), f32), sd((S, 2048), f32), sd((S, 1024), f32)],
        scratch_shapes=[pltpu.VMEM((FHS, 128, FTQ), f32), pltpu.VMEM((FHS, F_HD, FTQ), f32)],
        sem=("parallel", "arbitrary"), args=(qa, ka, v, do, lse, delta), comm=comm, prefetch=(kj_tab, qi_tab))


def fox_gate(o, proj):
    S = o.shape[0]
    tm = 512

    def body(o_ref, z_ref, o2_ref):
        o2_ref[...] = (o_ref[...] * _silu(z_ref[...])).astype(bf16)

    t = pl.BlockSpec((tm, 1024), lambda i: (i, 0))
    return pl.pallas_call(
        body, name="fox_gate", grid=(S // tm,),
        in_specs=[t, pl.BlockSpec((tm, 1024), lambda i: (i, 3))], out_specs=t,
        out_shape=jax.ShapeDtypeStruct((S, 1024), bf16),
        compiler_params=_params(("parallel",)),
    )(o, proj)


def fox_gate_bwd(do2, o, proj):
    S = o.shape[0]
    tm = 256

    def body(d_ref, o_ref, z_ref, do_ref, dz_ref, dl_ref):
        lo_half = _iota((tm, 128), 1) < F_HD
        for p in range(F_H // 2):
            ps = slice(p * 128, (p + 1) * 128)
            d2, ov, z = d_ref[:, ps], o_ref[:, ps], z_ref[:, ps]
            dov = d2 * _silu(z)
            do_ref[:, ps] = dov.astype(bf16)
            dz_ref[:, ps] = (d2 * ov * _dsilu(z)).astype(bf16)
            dl_ref[:, ps] = _half_mean(dov * ov, lo_half) * float(F_HD)

    t = pl.BlockSpec((tm, 1024), lambda i: (i, 0))
    sd = jax.ShapeDtypeStruct
    return pl.pallas_call(
        body, name="fox_gate_bwd", grid=(S // tm,),
        in_specs=[t, t, pl.BlockSpec((tm, 1024), lambda i: (i, 3))], out_specs=[t, t, t],
        out_shape=[sd((S, 1024), bf16), sd((S, 1024), bf16), sd((S, 1024), f32)],
        compiler_params=_params(("parallel",)),
    )(do2, o, proj)


def fox_pre_bwd(proj, fbias, qw2, kw2, dqa, dka, dv, dz):
    S = proj.shape[0]
    tm = 256
    nb = S // tm

    def body(q_ref, k_ref, f_ref, fb_ref, qw_ref, kw_ref, dqa_ref, dka_ref, dv_ref, dz_ref, dp_ref, st_ref, carry):
        i = pl.program_id(0)

        @pl.when(i == 0)
        def _():
            carry[...] = jnp.zeros_like(carry)

        lane = _iota((tm, 128), 1)
        lo_half = lane < F_HD
        lane16 = _iota((tm, 16), 1)
        dcum = jnp.zeros((tm, 16), f32)
        dws = []
        for src, w_ref, dsrc, is_q, col0 in ((q_ref, qw_ref, dqa_ref, True, 0), (k_ref, kw_ref, dka_ref, False, 1024)):
            dw = jnp.zeros((1, 128), f32)
            for p in range(F_H // 2):
                ps = slice(p * 128, (p + 1) * 128)
                x = src[:, ps]
                r = lax.rsqrt(_half_mean(x * x, lo_half) + EPS)
                xh = x * r
                d0 = dsrc[:, (2 * p) * 128:(2 * p + 1) * 128]
                d1 = dsrc[:, (2 * p + 1) * 128:(2 * p + 2) * 128]
                dy = jnp.where(lo_half, d0, pltpu.roll(d1, F_HD, 1))
                if is_q:
                    dy = dy * (F_HD ** -0.5)
                dxh = dy * w_ref[...]
                dw = dw + jnp.sum(dy * xh, axis=0, keepdims=True)
                dp_ref[:, col0 + p * 128:col0 + (p + 1) * 128] = (r * (dxh - xh * _half_mean(dxh * xh, lo_half))).astype(bf16)
                for e, de in ((0, d0), (1, d1)):
                    col = de[:, 64:65] if is_q else -de[:, 67:68]
                    dcum = dcum + jnp.where(lane16 == 2 * p + e, col, 0.0)
            dws.append(dw)
        dp_ref[:, 2048:3072] = dv_ref[...].astype(bf16)
        dp_ref[:, 3072:4096] = dz_ref[...]
        utri = jnp.where(_iota((tm, tm), 1) >= _iota((tm, tm), 0), 1.0, 0.0).astype(f32)
        dlogf = _nn(utri, dcum, HI) + carry[0:1, :]
        carry[0:1, :] = dlogf[0:1, :]
        fl = f_ref[:, 0:16] + fb_ref[...]
        df = dlogf * _sigmoid(-fl)
        place = jnp.where(_iota((16, 128), 1) == _iota((16, 128), 0), 1.0, 0.0).astype(f32)
        dfw = _nn(df, place, HI)
        dp_ref[:, F_F0:F_INP] = dfw.astype(bf16)
        upd = jnp.concatenate(dws + [jnp.sum(dfw, axis=0, keepdims=True), jnp.zeros((5, 128), f32)], axis=0)

        @pl.when(i == 0)
        def _():
            st_ref[...] = upd

        @pl.when(i > 0)
        def _():
            st_ref[...] += upd

    rev = lambda w, c: pl.BlockSpec((tm, w), lambda i: (nb - 1 - i, c))
    vec = lambda n: pl.BlockSpec((1, n), lambda i: (0, 0))
    sd = jax.ShapeDtypeStruct
    return pl.pallas_call(
        body, name="fox_pre_bwd", grid=(nb,),
        in_specs=[rev(1024, 0), rev(1024, 1), rev(128, F_F0 // 128), vec(16), vec(128), vec(128),
                  rev(2048, 0), rev(2048, 0), rev(1024, 0), rev(1024, 0)],
        out_specs=[rev(F_INP, 0), pl.BlockSpec((8, 128), lambda i: (0, 0))],
        out_shape=[sd((S, F_INP), bf16), sd((8, 128), f32)],
        scratch_shapes=[pltpu.VMEM((8, 16), f32)],
        compiler_params=_params(("arbitrary",)),
    )(proj, proj, proj, fbias, qw2, kw2, dqa, dka, dv, dz)


def _me():
    return lax.axis_index("x"), lax.axis_index("y"), lax.axis_index("c")


def _other_chips(x, y):
    return [(1 - x, y), (x, 1 - y), (1 - x, 1 - y)]


def ag_small(xs):
    m_per, n = xs.shape

    def body(x_ref, out_ref, send_sems, recv_sems, local_sem):
        x, y, c = _me()
        me, sibling = (x, y, c), (x, y, 1 - c)
        chips = _other_chips(x, y)

        def rows(px, py, pc):
            return out_ref.at[pl.ds((4 * px + 2 * py + pc) * m_per, m_per), :]

        def copy(k, block, to, src=None):
            return pltpu.make_async_remote_copy(
                src_ref=rows(*block) if src is None else src, dst_ref=rows(*block),
                send_sem=send_sems.at[k], recv_sem=recv_sems.at[k], device_id=to, device_id_type=MESH)

        mine = pltpu.make_async_copy(x_ref, rows(*me), local_sem)
        mine.start()
        first = [copy(0, me, sibling, src=x_ref)]
        first += [copy(1 + j, me, (*chip, c), src=x_ref) for j, chip in enumerate(chips)]
        for cp in first:
            cp.start()
        passed = [copy(4 + j, (*chip, c), sibling) for j, chip in enumerate(chips)]
        for j, chip in enumerate(chips):
            copy(1 + j, (*chip, c), me).wait_recv()
            passed[j].start()
        copy(0, sibling, me).wait_recv()
        for j, chip in enumerate(chips):
            copy(4 + j, (*chip, 1 - c), me).wait_recv()
        for cp in first + passed:
            cp.wait_send()
        mine.wait()

    return pl.pallas_call(
        body, name="ag_small",
        out_shape=jax.ShapeDtypeStruct((8 * m_per, n), xs.dtype),
        in_specs=[pl.BlockSpec(memory_space=pltpu.VMEM)], out_specs=pl.BlockSpec(memory_space=pltpu.VMEM),
        scratch_shapes=[pltpu.SemaphoreType.DMA((7,)), pltpu.SemaphoreType.DMA((7,)), pltpu.SemaphoreType.DMA],
        compiler_params=pltpu.CompilerParams(vmem_limit_bytes=VMEM_LIMIT),
    )(xs)


_ANY = pl.BlockSpec(memory_space=pl.ANY)


def ag_chips(arrs):
    n = len(arrs)
    assert all(a.shape[0] == 2 for a in arrs)

    def body(*refs):
        ins, outs = refs[:n], refs[n:2 * n]
        send_sems, recv_sems, fwd_send, fwd_recv, local_sems = refs[2 * n:]
        x, y, c = _me()
        me = 2 * x + y
        chips = _other_chips(x, y)
        started = []
        for a in range(n):
            cp = pltpu.make_async_copy(ins[a], outs[a].at[me], local_sems.at[a])
            cp.start()
            started.append(cp)
        sends = []
        for a in range(n):
            for j, (px, py) in enumerate(chips):
                r = pltpu.make_async_remote_copy(
                    src_ref=ins[a].at[c], dst_ref=outs[a].at[me, c], send_sem=send_sems.at[3 * a + j],
                    recv_sem=recv_sems.at[3 * a + j], device_id=(px, py, c), device_id_type=MESH)
                r.start()
                sends.append(r)
        for a in range(n):
            for j, (px, py) in enumerate(chips):
                got = outs[a].at[2 * px + py, c]
                pltpu.make_async_remote_copy(
                    src_ref=ins[a].at[c], dst_ref=got, send_sem=send_sems.at[3 * a + j],
                    recv_sem=recv_sems.at[3 * a + j], device_id=(px, py, c), device_id_type=MESH).wait_recv()
                f = pltpu.make_async_remote_copy(
                    src_ref=got, dst_ref=got, send_sem=fwd_send.at[3 * a + j], recv_sem=fwd_recv.at[3 * a + j],
                    device_id=(x, y, 1 - c), device_id_type=MESH)
                f.start()
                sends.append(f)
        for a in range(n):
            for j, (px, py) in enumerate(chips):
                theirs = outs[a].at[2 * px + py, 1 - c]
                pltpu.make_async_remote_copy(
                    src_ref=theirs, dst_ref=theirs, send_sem=fwd_send.at[3 * a + j], recv_sem=fwd_recv.at[3 * a + j],
                    device_id=(x, y, 1 - c), device_id_type=MESH).wait_recv()
        for r in sends:
            r.wait_send()
        for cp in started:
            cp.wait()

    sems = pltpu.SemaphoreType.DMA((3 * n,))
    return pl.pallas_call(
        body, name="ag_chips",
        out_shape=[jax.ShapeDtypeStruct((4,) + a.shape, a.dtype) for a in arrs],
        in_specs=[_ANY] * n, out_specs=[_ANY] * n,
        scratch_shapes=[sems, sems, sems, sems, pltpu.SemaphoreType.DMA((n,))],
    )(*arrs)


def _ag_comm(arrs):
    n = len(arrs)

    def copies(ins, outs, sems, inbound):
        send_sems, recv_sems, local_sems = sems
        x, y, c = _me()
        me = 2 * x + y
        local = [pltpu.make_async_copy(ins[a], outs[a].at[me], local_sems.at[a]) for a in range(n)]
        out_cp, in_cp = [], []
        for a in range(n):
            for j, (px, py) in enumerate(_other_chips(x, y)):
                mk = functools.partial(pltpu.make_async_remote_copy, src_ref=ins[a], send_sem=send_sems.at[3 * a + j],
                                       recv_sem=recv_sems.at[3 * a + j], device_id=(px, py, c), device_id_type=MESH)
                out_cp.append(mk(dst_ref=outs[a].at[me]))
                if inbound:
                    in_cp.append(mk(dst_ref=outs[a].at[2 * px + py]))
        return local, out_cp, in_cp

    def start(ins, outs, sems):
        local, out_cp, _ = copies(ins, outs, sems, False)
        for cp in local + out_cp:
            cp.start()

    def wait(ins, outs, sems):
        local, out_cp, in_cp = copies(ins, outs, sems, True)
        for cp in in_cp:
            cp.wait_recv()
        for cp in out_cp:
            cp.wait_send()
        for cp in local:
            cp.wait()

    sems = [pltpu.SemaphoreType.DMA((3 * n,)), pltpu.SemaphoreType.DMA((3 * n,)), pltpu.SemaphoreType.DMA((n,))]
    return _Comm(arrs, [jax.ShapeDtypeStruct((4,) + a.shape, a.dtype) for a in arrs], sems, start, wait)


def _rs_comm(gs):
    n = len(gs)
    flips = [(fx, fy, fc) for fx in (0, 1) for fy in (0, 1) for fc in (0, 1)][1:]

    def copies(ins, outs, sems, inbound):
        send_sems, recv_sems, local_sems = sems
        x, y, c = _me()
        me = 4 * x + 2 * y + c
        local, out_cp, in_cp = [], [], []
        for a in range(n):
            rh = ins[a].shape[1] // 2
            mine = ins[a].at[2 * x + y, pl.ds(c * rh, rh), :]
            local.append(pltpu.make_async_copy(mine, outs[a].at[me], local_sems.at[a]))
            for j, (fx, fy, fc) in enumerate(flips):
                px, py, pc = (1 - x if fx else x), (1 - y if fy else y), (1 - c if fc else c)
                mk = functools.partial(pltpu.make_async_remote_copy, send_sem=send_sems.at[7 * a + j],
                                       recv_sem=recv_sems.at[7 * a + j], device_id=(px, py, pc), device_id_type=MESH)
                out_cp.append(mk(src_ref=ins[a].at[2 * px + py, pl.ds(pc * rh, rh), :], dst_ref=outs[a].at[me]))
                if inbound:
                    in_cp.append(mk(src_ref=mine, dst_ref=outs[a].at[4 * px + 2 * py + pc]))
        return local, out_cp, in_cp

    def start(ins, outs, sems):
        local, out_cp, _ = copies(ins, outs, sems, False)
        for cp in local + out_cp:
            cp.start()

    def wait(ins, outs, sems):
        local, out_cp, in_cp = copies(ins, outs, sems, True)
        for cp in in_cp:
            cp.wait_recv()
        for cp in out_cp:
            cp.wait_send()
        for cp in local:
            cp.wait()

    sems = [pltpu.SemaphoreType.DMA((7 * n,)), pltpu.SemaphoreType.DMA((7 * n,)), pltpu.SemaphoreType.DMA((n,))]
    return _Comm(gs, [jax.ShapeDtypeStruct((8, g.shape[1] // 2, g.shape[2]), g.dtype) for g in gs], sems, start, wait)


def sum_leading(q, name):
    K, R, C = q.shape
    tr = _pick(R, (256, 128, 64, 32, 16, 8))

    def body(q_ref, o_ref):
        acc = q_ref[0]
        for k in range(1, K):
            acc = acc + q_ref[k]
        o_ref[...] = acc

    return pl.pallas_call(
        body, name=name, grid=(R // tr,),
        in_specs=[pl.BlockSpec((K, tr, C), lambda i: (0, i, 0))], out_specs=pl.BlockSpec((tr, C), lambda i: (i, 0)),
        out_shape=jax.ShapeDtypeStruct((R, C), f32),
        compiler_params=_params(("parallel",)),
    )(q)


def rs_sum_devices(q, cidx, layer, n_layers, into=None):
    K, R, C = q.shape
    tr = _pick(R, (256, 128))

    def body(c_ref, q_ref, *rest):
        acc = q_ref[0].astype(f32)
        for k in range(1, K):
            acc = acc + q_ref[k].astype(f32)
        rest[-1][0, 0] = acc

    return pl.pallas_call(
        body, name="rs_sum_devices",
        grid_spec=pltpu.PrefetchScalarGridSpec(
            num_scalar_prefetch=1, grid=(R // tr,),
            in_specs=[pl.BlockSpec((K, tr, C), lambda i, c_ref: (0, i, 0))] + ([] if into is None else [_ANY]),
            out_specs=pl.BlockSpec((1, 1, tr, C), lambda i, c_ref: (layer, c_ref[0], i, 0))),
        out_shape=jax.ShapeDtypeStruct((n_layers, 2, R, C), f32),
        input_output_aliases={} if into is None else {2: 0},
        compiler_params=_params(("parallel",)),
    )(cidx, q, *([] if into is None else [into]))


def rs_share_halves(rs):
    n = len(rs)

    def body(*refs):
        bufs = refs[n:2 * n]
        send_sems, recv_sems = refs[2 * n:]
        x, y, c = _me()
        cps = []
        for a in range(n):
            mine = bufs[a].at[pl.ds(0, bufs[a].shape[0]), c]
            cp = pltpu.make_async_remote_copy(
                src_ref=mine, dst_ref=mine, send_sem=send_sems.at[a], recv_sem=recv_sems.at[a],
                device_id=(x, y, 1 - c), device_id_type=MESH)
            cp.start()
            cps.append(cp)
        for a, cp in enumerate(cps):
            theirs = bufs[a].at[pl.ds(0, bufs[a].shape[0]), 1 - c]
            pltpu.make_async_remote_copy(
                src_ref=theirs, dst_ref=theirs, send_sem=send_sems.at[a], recv_sem=recv_sems.at[a],
                device_id=(x, y, 1 - c), device_id_type=MESH).wait_recv()
            cp.wait_send()

    return pl.pallas_call(
        body, name="rs_share_halves",
        out_shape=[jax.ShapeDtypeStruct(r.shape, r.dtype) for r in rs],
        in_specs=[_ANY] * n, out_specs=[_ANY] * n, input_output_aliases={a: a for a in range(n)},
        scratch_shapes=[pltpu.SemaphoreType.DMA((n,)), pltpu.SemaphoreType.DMA((n,))],
    )(*rs)


def ada_mod(c_all, ada_w):
    L, _, n = ada_w.shape

    def body(c_ref, w_ref, o_ref):
        o_ref[0] = _nn(_silu(c_ref[...]), w_ref[0], HI)

    return pl.pallas_call(
        body, name="ada_mod", grid=(L,),
        in_specs=[pl.BlockSpec((8, D), lambda l: (0, 0)), pl.BlockSpec((1, D, n), lambda l: (l, 0, 0))],
        out_specs=pl.BlockSpec((1, 8, n), lambda l: (l, 0, 0)),
        out_shape=jax.ShapeDtypeStruct((L, 8, n), f32),
        compiler_params=_params(("parallel",)),
    )(c_all, ada_w)


def ada_w_grad(c_all, dmod):
    L, _, n = dmod.shape

    def body(c_ref, d_ref, o_ref):
        o_ref[0] = _tn(_silu(c_ref[...]), d_ref[0], HI)

    return pl.pallas_call(
        body, name="ada_w_grad", grid=(L,),
        in_specs=[pl.BlockSpec((8, D), lambda l: (0, 0)), pl.BlockSpec((1, 8, n), lambda l: (l, 0, 0))],
        out_specs=pl.BlockSpec((1, D, n), lambda l: (l, 0, 0)),
        out_shape=jax.ShapeDtypeStruct((L, D, n), f32),
        compiler_params=_params(("parallel",)),
    )(c_all, dmod)


def adamw(w, g, m, v, name):
    shp = w.shape
    two = lambda a: a.reshape(-1, shp[-1])
    R, C = two(w).shape
    tr = _pick(R, (256, 128, 64, 32, 16, 8))
    bc1, bc2 = 1.0 - B1 ** STEP, 1.0 - B2 ** STEP

    def body(w_ref, g_ref, m_ref, v_ref, d_ref, mo_ref, vo_ref):
        gv = g_ref[...]
        mn = B1 * m_ref[...] + (1.0 - B1) * gv
        vn = B2 * v_ref[...] + (1.0 - B2) * (gv * gv)
        d_ref[...] = -LR * ((mn / bc1) / (jnp.sqrt(vn / bc2) + AEPS) + WD * w_ref[...])
        mo_ref[...] = mn
        vo_ref[...] = vn

    t = pl.BlockSpec((tr, C), lambda i: (i, 0))
    outs = pl.pallas_call(
        body, name=name, grid=(R // tr,),
        in_specs=[t] * 4, out_specs=[t] * 3, out_shape=[jax.ShapeDtypeStruct((R, C), f32)] * 3,
        compiler_params=_params(("parallel",)),
    )(two(w), two(g), two(m), two(v))
    return [o.reshape(shp) for o in outs]


def _pack(arrs):
    parts, offs, r0 = [], [], 0
    for a in arrs:
        n = a.size
        rows = -(-n // 1024) * 8
        parts.append(jnp.pad(a.reshape(-1), (0, rows * 128 - n)).reshape(rows, 128))
        offs.append((r0, rows))
        r0 += rows
    return jnp.concatenate(parts, axis=0), offs


def _unpack(buf, offs, shapes):
    out = []
    for (r0, rows), shp in zip(offs, shapes):
        n = 1
        for d in shp:
            n *= d
        out.append(buf[..., r0:r0 + rows, :].reshape(buf.shape[:-2] + (rows * 128,))[..., :n].reshape(buf.shape[:-2] + tuple(shp)))
    return out


def kernel(x, c, norm_w, ada_w, ada_b, a_w_in, a_conv_w, a_A_log, a_dt_bias, a_norm_w, a_w_out, b_w_in, b_f_bias, b_qn_w, b_kn_w, b_w_out, final_norm_w, loss_target, m_norm_w, m_ada_w, m_ada_b, m_a_w_in, m_a_conv_w, m_a_A_log, m_a_dt_bias, m_a_norm_w, m_a_w_out, m_b_w_in, m_b_f_bias, m_b_qn_w, m_b_kn_w, m_b_w_out, m_final_norm_w, v_norm_w, v_ada_w, v_ada_b, v_a_w_in, v_a_conv_w, v_a_A_log, v_a_dt_bias, v_a_norm_w, v_a_w_out, v_b_w_in, v_b_f_bias, v_b_qn_w, v_b_kn_w, v_b_w_out, v_final_norm_w):
    weights = dict(norm_w=norm_w, ada_w=ada_w, ada_b=ada_b, a_w_in=a_w_in, a_conv_w=a_conv_w, a_A_log=a_A_log,
                   a_dt_bias=a_dt_bias, a_norm_w=a_norm_w, a_w_out=a_w_out, b_w_in=b_w_in, b_f_bias=b_f_bias,
                   b_qn_w=b_qn_w, b_kn_w=b_kn_w, b_w_out=b_w_out, final_norm_w=final_norm_w)
    m_in = dict(norm_w=m_norm_w, ada_w=m_ada_w, ada_b=m_ada_b, a_w_in=m_a_w_in, a_conv_w=m_a_conv_w, a_A_log=m_a_A_log,
                a_dt_bias=m_a_dt_bias, a_norm_w=m_a_norm_w, a_w_out=m_a_w_out, b_w_in=m_b_w_in, b_f_bias=m_b_f_bias,
                b_qn_w=m_b_qn_w, b_kn_w=m_b_kn_w, b_w_out=m_b_w_out, final_norm_w=m_final_norm_w)
    v_in = dict(norm_w=v_norm_w, ada_w=v_ada_w, ada_b=v_ada_b, a_w_in=v_a_w_in, a_conv_w=v_a_conv_w, a_A_log=v_a_A_log,
                a_dt_bias=v_a_dt_bias, a_norm_w=v_a_norm_w, a_w_out=v_a_w_out, b_w_in=v_b_w_in, b_f_bias=v_b_f_bias,
                b_qn_w=v_b_qn_w, b_kn_w=v_b_kn_w, b_w_out=v_b_w_out, final_norm_w=v_final_norm_w)
    xi, yi, ci = _me()
    me_b, me_k = 4 * xi + 2 * yi + ci, 2 * xi + yi
    cidx = ci.astype(jnp.int32).reshape(1)
    S = x.shape[1]
    depth, n_a, n_b = norm_w.shape[0], a_w_in.shape[0], b_w_in.shape[0]
    x0, tgt = x.reshape(S, D), loss_target.reshape(S, D)

    c_all = ag_small(jnp.pad(c, ((0, 7), (0, 0)))).reshape(8, 8, D)[:, 0]
    nloc = ada_w.shape[2]
    parts = ag_small(ada_mod(c_all, ada_w).reshape(depth * 8, nloc)).reshape(4, 2, depth, 8, nloc)[:, 0]
    mine = lax.dynamic_index_in_dim(parts, me_b, axis=2, keepdims=False)
    mod = jnp.transpose(mine, (1, 0, 2)).reshape(depth, 4 * nloc) + ada_b
    shift, scale, gate = (mod[:, k * D:(k + 1) * D] for k in range(3))

    w_loc = [(a_w_in[i // 2] if i % 2 == 0 else b_w_in[i // 2]).astype(bf16) for i in range(depth)]
    wo_loc = [(a_w_out[i // 2] if i % 2 == 0 else b_w_out[i // 2]).astype(bf16) for i in range(depth)]
    pad_in = [(G_INP - G_IN) if i % 2 == 0 else (F_INP - F_IN) for i in range(depth)]
    halves = lambda w: w.reshape((2, w.shape[0] // 2) + w.shape[1:])

    def cols_in_place(g_in, pad):
        w = jnp.transpose(g_in, (1, 0, 2)).reshape(g_in.shape[1], -1)
        return jnp.pad(w, ((0, 0), (0, pad)))

    g_in0, g_conv = ag_chips([halves(w_loc[0]), a_conv_w])
    w_in_full = [cols_in_place(g_in0.reshape((4,) + w_loc[0].shape), pad_in[0])]
    w_out_full = []
    conv = [jnp.transpose(g_conv[:, l], (1, 0, 2)).reshape(CONV_K, -1) for l in range(n_a)]
    qw2 = [_row(jnp.tile(b_qn_w[l], 2)) for l in range(n_b)]
    kw2 = [_row(jnp.tile(b_kn_w[l], 2)) for l in range(n_b)]

    saved, xc = [], x0
    for i in range(depth):
        l = i // 2
        nxt = _ag_comm([w_loc[i + 1], wo_loc[i + 1]]) if i + 1 < depth else None
        h = ln_mod(xc, _row(norm_w[i]), _row(scale[i]), _row(shift[i]))
        name = "mm_a_in" if i % 2 == 0 else "mm_b_in"
        if i == 0:
            proj, got = matmul(h, w_in_full[0], "nn", name, comm=_ag_comm([wo_loc[0]]))
            w_out_full.append(got[0].reshape(-1, D))
        else:
            proj = matmul(h, w_in_full[i], "nn", name)
        if i % 2 == 0:
            pre = gdn_pre(proj, conv[l], _row(a_A_log[l]), _row(a_dt_bias[l]))
            res, got = gdn_fwd(*pre, comm=nxt)
            o2 = gdn_onorm(res[0], proj, _row(a_norm_w[l]))
            y, xn = out_proj(o2, w_out_full[i], xc, _row(gate[i]), "out_proj_a")
        else:
            pre = fox_pre(proj, _row(b_f_bias[l]), qw2[l], kw2[l])
            res, got = fox_attn(*pre, comm=nxt)
            o2 = fox_gate(res[0], proj)
            y, xn = out_proj(o2, w_out_full[i], xc, _row(gate[i]), "out_proj_b")
        saved.append((xc, h, proj, o2, y, pre, res))
        if nxt is not None:
            w_in_full.append(cols_in_place(got[0], pad_in[i + 1]))
            w_out_full.append(got[1].reshape(-1, D))
        xc = xn
    dx, st_f = final_loss(xc, _row(final_norm_w), tgt)

    d_norm, d_mod = [None] * depth, [None] * depth
    d_conv, d_alog, d_dtb, d_anw = [None] * n_a, [None] * n_a, [None] * n_a, [None] * n_a
    d_fb, d_qn, d_kn = [None] * n_b, [None] * n_b, [None] * n_b
    ex_in, ex_out, pend_in = [None] * depth, [None] * depth, None
    for i in reversed(range(depth)):
        l = i // 2
        xin, h, proj, o2, y, pre, res = saved[i]
        ab = "a" if i % 2 == 0 else "b"
        dy, st_g = gate_bwd(dx, y, _row(gate[i]))
        do2 = matmul(dy, w_out_full[i], "nt", f"mm_{ab}_do2")
        d_out = matmul(o2, dy, "tn", f"mm_{ab}_dwo")
        ride = _rs_comm(([] if pend_in is None else [pend_in]) + [d_out.reshape(4, d_out.shape[0] // 4, D).astype(bf16)])
        if i % 2 == 0:
            o, wv, at, tinv, vn, st = res
            do, dz, st_o = gdn_onorm_bwd(do2, o, proj, _row(a_norm_w[l]))
            grads, got = gdn_bwd(do, *pre, wv, at, tinv, vn, st, comm=ride)
            dcv, dba, st_s = gdn_pre_bwd(proj, conv[l], _row(a_A_log[l]), _row(a_dt_bias[l]), *grads)
            dproj, dcw = gdn_conv_bwd(proj, conv[l], dcv, dz, dba)
            d_conv[l], d_alog[l], d_dtb[l], d_anw[l] = dcw[:CONV_K], st_s[0], st_s[1], st_o[0]
        else:
            o, lse = res
            do, dz, delta = fox_gate_bwd(do2, o, proj)
            (dqa, dka, dv), got = fox_attn_bwd(*pre, do, lse, delta, comm=ride)
            dproj, st_b = fox_pre_bwd(proj, _row(b_f_bias[l]), qw2[l], kw2[l], dqa, dka, dv, dz)
            d_fb[l], d_qn[l], d_kn[l] = st_b[2, :F_H], st_b[0, :F_HD] + st_b[0, F_HD:], st_b[1, :F_HD] + st_b[1, F_HD:]
        ex_out[i] = got[-1]
        if pend_in is not None:
            ex_in[i + 1] = got[0]
        d_in = matmul(h, dproj, "tn", f"mm_{ab}_dw")
        cl = w_loc[i].shape[1]
        pend_in = jnp.transpose(d_in[:, :4 * cl].reshape(d_in.shape[0], 4, cl), (1, 0, 2)).astype(bf16)
        if i == 0:
            dh, got = matmul(dproj, w_in_full[i], "nt", f"mm_{ab}_dh", comm=_rs_comm([pend_in]))
            ex_in[0] = got[0]
        else:
            dh = matmul(dproj, w_in_full[i], "nt", f"mm_{ab}_dh")
        dx, st_n = ln_mod_bwd(xin, _row(norm_w[i]), _row(scale[i]), dh, dx)
        d_norm[i] = st_n[0]
        d_mod[i] = jnp.concatenate([st_n[2], st_n[1], st_g[0]])

    small = [jnp.stack(d_norm), jnp.stack(d_mod), jnp.stack(d_conv), jnp.stack(d_alog), jnp.stack(d_dtb), jnp.stack(d_anw),
             jnp.stack(d_fb), jnp.stack(d_qn), jnp.stack(d_kn), st_f[0], jnp.sum(st_f[1]).reshape(1)]
    shapes = [a.shape for a in small]
    buf, offs = _pack(small)
    gathered = ag_small(buf).reshape(8, buf.shape[0], 128)
    tot = _unpack(sum_leading(gathered, "sum_devices"), offs, shapes)
    g_norm, g_adab, g_convf, g_alog, g_dtb, g_anw, g_fb, g_qn, g_kn, g_fin, loss = tot
    dmod_all = _unpack(gathered, offs[1:2], shapes[1:2])[0]
    dmod_loc = lax.dynamic_slice_in_dim(dmod_all, me_k * nloc, nloc, axis=2)
    g_adaw = ada_w_grad(c_all, jnp.transpose(dmod_loc, (1, 0, 2)))
    g_conv_loc = lax.dynamic_slice_in_dim(g_convf, me_k * a_conv_w.shape[2], a_conv_w.shape[2], axis=2)

    bufs = {}
    for i in range(depth):
        for which, q in (("in", ex_in[i]), ("out", ex_out[i])):
            key = ("a" if i % 2 == 0 else "b", which)
            bufs[key] = rs_sum_devices(q, cidx, i // 2, depth // 2, into=bufs.get(key))
    keys = list(bufs)
    done = dict(zip(keys, rs_share_halves([bufs[k] for k in keys])))
    grads = dict(norm_w=g_norm, ada_w=g_adaw, ada_b=g_adab, a_w_in=done["a", "in"].reshape(a_w_in.shape),
                 a_conv_w=g_conv_loc, a_A_log=g_alog, a_dt_bias=g_dtb, a_norm_w=g_anw,
                 a_w_out=done["a", "out"].reshape(a_w_out.shape), b_w_in=done["b", "in"].reshape(b_w_in.shape),
                 b_f_bias=g_fb, b_qn_w=g_qn, b_kn_w=g_kn, b_w_out=done["b", "out"].reshape(b_w_out.shape),
                 final_norm_w=g_fin)
    names = list(weights)
    upd = {n: adamw(weights[n], grads[n], m_in[n], v_in[n], "adamw_" + n) for n in names}
    return (loss.reshape(()), dx.reshape(x.shape), *[grads[n] for n in names], *[upd[n][0] for n in names],
            *[upd[n][1] for n in names], *[upd[n][2] for n in names])
```

```python
import functools

import jax
import jax.numpy as jnp
from jax import lax
from jax.experimental import pallas as pl
from jax.experimental.pallas import tpu as pltpu

f32, bf16 = jnp.float32, jnp.bfloat16
HI = lax.Precision.HIGHEST
MESH = pl.DeviceIdType.MESH

EPS = 1e-6
D = 1024
CHUNK = 64
GQK_H, GV_H, GHD = 8, 16, 128
G_CONV = 4096
G_Z0 = 4096
G_BA0 = 6144
G_IN, G_INP = 6176, 6272
CONV_K = 4
F_H, F_HD = 16, 64
F_W = 1024
F_F0 = 4096
F_IN, F_INP = 4112, 4224
LR, B1, B2, AEPS, WD, STEP = 0.001, 0.9, 0.999, 1e-08, 0.01, 10
NEG = -1e30
VMEM_LIMIT = 56 * 1024 * 1024


def _nn(a, b, prec=None):
    return lax.dot_general(a, b, (((1,), (0,)), ((), ())), preferred_element_type=f32, precision=prec)


def _nt(a, b, prec=None):
    return lax.dot_general(a, b, (((1,), (1,)), ((), ())), preferred_element_type=f32, precision=prec)


def _tn(a, b, prec=None):
    return lax.dot_general(a, b, (((0,), (0,)), ((), ())), preferred_element_type=f32, precision=prec)


def _iota(shape, axis):
    return lax.broadcasted_iota(jnp.int32, shape, axis)


def _sigmoid(x):
    return 0.5 * jnp.tanh(0.5 * x) + 0.5


def _softplus(x):
    return jnp.maximum(x, 0.0) + jnp.log(1.0 + jnp.exp(-jnp.abs(x)))


def _silu(x):
    return x * _sigmoid(x)


def _dsilu(x):
    s = _sigmoid(x)
    return s * (1.0 + x * (1.0 - s))


def _params(sem=None, vmem=VMEM_LIMIT):
    return pltpu.CompilerParams(dimension_semantics=sem, vmem_limit_bytes=vmem)


def _row(v):
    return v.reshape(1, -1)


class _Comm:
    def __init__(self, ins, out_shapes, sems, start, wait):
        self.ins, self.out_shapes, self.sems, self.start, self.wait = list(ins), list(out_shapes), list(sems), start, wait


def _call(body, *, name, grid, in_specs, out_specs, out_shape, scratch_shapes, sem, args, comm=None, prefetch=()):
    n_pf, n_in, n_out, n_s = len(prefetch), len(in_specs), len(out_specs), len(scratch_shapes)
    n_ci, n_co = (len(comm.ins), len(comm.out_shapes)) if comm is not None else (0, 0)

    def wrapped(*refs):
        pf, refs = refs[:n_pf], refs[n_pf:]
        core_in, c_in = refs[:n_in], refs[n_in:n_in + n_ci]
        o0 = n_in + n_ci
        core_out, c_out = refs[o0:o0 + n_out], refs[o0 + n_out:o0 + n_out + n_co]
        s0 = o0 + n_out + n_co
        core_s, c_sem = refs[s0:s0 + n_s], refs[s0 + n_s:]
        if comm is not None:
            first = functools.reduce(jnp.logical_and, [pl.program_id(d) == 0 for d in range(len(grid))])
            pl.when(first)(functools.partial(comm.start, c_in, c_out, c_sem))
        body(*pf, *core_in, *core_out, *core_s)
        if comm is not None:
            last = functools.reduce(jnp.logical_and, [pl.program_id(d) == grid[d] - 1 for d in range(len(grid))])
            pl.when(last)(functools.partial(comm.wait, c_in, c_out, c_sem))

    extra = ([], [], [], []) if comm is None else ([_ANY] * n_ci, [_ANY] * n_co, comm.out_shapes, comm.sems)
    spec = pltpu.PrefetchScalarGridSpec(
        num_scalar_prefetch=n_pf, grid=grid, in_specs=list(in_specs) + extra[0], out_specs=list(out_specs) + extra[1],
        scratch_shapes=list(scratch_shapes) + extra[3])
    outs = pl.pallas_call(
        wrapped, name=name if comm is None else name + "_x", grid_spec=spec, out_shape=list(out_shape) + extra[2],
        compiler_params=_params(sem if comm is None else ("arbitrary",) * len(grid)),
    )(*prefetch, *args, *(comm.ins if comm is not None else []))
    return outs[:n_out], outs[n_out:]


def _pick(n, pref):
    for t in pref:
        if n % t == 0:
            return t
    return n


MM_VMEM_BUDGET = 44 * 1024 * 1024


def _mm_tiles(M, N, K):
    best = None
    for tk in [K] + [t for t in (2048, 1408, 1024, 896, 512, 384, 256, 128) if K % t == 0 and t < K]:
        for tm in (2048, 1024, 512, 256, 128):
            for tn in (1408, 1024, 896, 512, 384, 256, 128):
                if M % tm or N % tn:
                    continue
                nk = K // tk
                need = 2 * 2 * (tm * tk + tk * tn) + 2 * 4 * tm * tn + (4 * tm * tn if nk > 1 else 0)
                if need <= MM_VMEM_BUDGET:
                    cand = ((nk, -tm * tn), (tm, tn, tk))
                    best = cand if best is None or cand[0] < best[0] else best
    return best[1]


def matmul(a, b, mode, name, out_dtype=f32, comm=None):
    if mode == "nn":
        (M, K), (_, N) = a.shape, b.shape
    elif mode == "nt":
        (M, K), (N, _) = a.shape, b.shape
    else:
        (K, M), (_, N) = a.shape, b.shape
    tm, tn, tk = _mm_tiles(M, N, K)
    nk = K // tk
    dot = {"nn": _nn, "nt": _nt, "tn": _tn}[mode]

    def body(a_ref, b_ref, o_ref, *acc):
        k = pl.program_id(2)
        part = dot(a_ref[...], b_ref[...])
        if nk == 1:
            o_ref[...] = part.astype(out_dtype)
        else:
            acc_ref = acc[0]

            @pl.when(k == 0)
            def _():
                acc_ref[...] = part

            @pl.when(k > 0)
            def _():
                acc_ref[...] += part

            @pl.when(k == nk - 1)
            def _():
                o_ref[...] = acc_ref[...].astype(out_dtype)

    a_spec = pl.BlockSpec((tk, tm), lambda i, j, k: (k, i)) if mode == "tn" else pl.BlockSpec((tm, tk), lambda i, j, k: (i, k))
    b_spec = pl.BlockSpec((tn, tk), lambda i, j, k: (j, k)) if mode == "nt" else pl.BlockSpec((tk, tn), lambda i, j, k: (k, j))
    outs, got = _call(
        body, name=name, grid=(M // tm, N // tn, nk),
        in_specs=[a_spec, b_spec], out_specs=[pl.BlockSpec((tm, tn), lambda i, j, k: (i, j))],
        out_shape=[jax.ShapeDtypeStruct((M, N), out_dtype)],
        scratch_shapes=[] if nk == 1 else [pltpu.VMEM((tm, tn), f32)],
        sem=("parallel", "parallel", "arbitrary"), args=(a, b), comm=comm)
    return outs[0] if comm is None else (outs[0], got)


def out_proj(o2, w, x, gate, name):
    S, K = o2.shape
    N = w.shape[1]
    tm, tn = 1024, 1024

    def body(a_ref, b_ref, x_ref, g_ref, y_ref, xn_ref):
        y = _nn(a_ref[...], b_ref[...])
        y_ref[...] = y
        xn_ref[...] = x_ref[...] + g_ref[...] * y

    return pl.pallas_call(
        body, name=name, grid=(S // tm, N // tn),
        in_specs=[pl.BlockSpec((tm, K), lambda i, j: (i, 0)), pl.BlockSpec((K, tn), lambda i, j: (0, j)),
                  pl.BlockSpec((tm, tn), lambda i, j: (i, j)), pl.BlockSpec((1, tn), lambda i, j: (0, j))],
        out_specs=[pl.BlockSpec((tm, tn), lambda i, j: (i, j))] * 2,
        out_shape=[jax.ShapeDtypeStruct((S, N), f32)] * 2,
        compiler_params=_params(("parallel", "parallel")),
    )(o2, w, x, gate)


def ln_mod(x, nw, scale, shift):
    S = x.shape[0]
    tm = 512

    def body(x_ref, nw_ref, sc_ref, sh_ref, h_ref):
        xv = x_ref[...]
        r = lax.rsqrt(jnp.mean(xv * xv, axis=-1, keepdims=True) + EPS)
        h_ref[...] = ((xv * r) * nw_ref[...] * (1.0 + sc_ref[...]) + sh_ref[...]).astype(bf16)

    vec = pl.BlockSpec((1, D), lambda i: (0, 0))
    return pl.pallas_call(
        body, name="ln_mod", grid=(S // tm,),
        in_specs=[pl.BlockSpec((tm, D), lambda i: (i, 0)), vec, vec, vec],
        out_specs=pl.BlockSpec((tm, D), lambda i: (i, 0)),
        out_shape=jax.ShapeDtypeStruct((S, D), bf16),
        compiler_params=_params(("parallel",)),
    )(x, nw, scale, shift)


def ln_mod_bwd(x, nw, scale, dh, dxres):
    S = x.shape[0]
    tm = 512
    nb = S // tm

    def body(x_ref, nw_ref, sc_ref, dh_ref, dr_ref, dx_ref, st_ref):
        i = pl.program_id(0)
        xv = x_ref[...]
        r = lax.rsqrt(jnp.mean(xv * xv, axis=-1, keepdims=True) + EPS)
        xn = xv * r
        dh = dh_ref[...]
        dxn = dh * (nw_ref[...] * (1.0 + sc_ref[...]))
        dx_ref[...] = dr_ref[...] + r * (dxn - xn * jnp.mean(dxn * xn, axis=-1, keepdims=True))
        p1 = jnp.sum(dh * xn, axis=0, keepdims=True)
        p2 = jnp.sum(dh, axis=0, keepdims=True)
        upd = jnp.concatenate([p1, p1, p2, jnp.zeros((5, D), f32)], axis=0)

        @pl.when(i == 0)
        def _():
            st_ref[...] = upd

        @pl.when(i > 0)
        def _():
            st_ref[...] += upd

        @pl.when(i == nb - 1)
        def _():
            st_ref[0:1, :] = st_ref[0:1, :] * (1.0 + sc_ref[...])
            st_ref[1:2, :] = st_ref[1:2, :] * nw_ref[...]

    vec = pl.BlockSpec((1, D), lambda i: (0, 0))
    tile = pl.BlockSpec((tm, D), lambda i: (i, 0))
    return pl.pallas_call(
        body, name="ln_mod_bwd", grid=(S // tm,),
        in_specs=[tile, vec, vec, tile, tile],
        out_specs=[tile, pl.BlockSpec((8, D), lambda i: (0, 0))],
        out_shape=[jax.ShapeDtypeStruct((S, D), f32), jax.ShapeDtypeStruct((8, D), f32)],
        compiler_params=_params(("arbitrary",)),
    )(x, nw, scale, dh, dxres)


def final_loss(x, fw, tgt):
    S = x.shape[0]
    tm = 512

    def body(x_ref, w_ref, t_ref, dx_ref, st_ref):
        i = pl.program_id(0)
        xv = x_ref[...]
        r = lax.rsqrt(jnp.mean(xv * xv, axis=-1, keepdims=True) + EPS)
        xn = xv * r
        err = xn * w_ref[...] - t_ref[...]
        dy = err * (1.0 / D)
        dxn = dy * w_ref[...]
        dx_ref[...] = r * (dxn - xn * jnp.mean(dxn * xn, axis=-1, keepdims=True))
        p1 = jnp.sum(dy * xn, axis=0, keepdims=True)
        p2 = jnp.sum(err * err, axis=0, keepdims=True) * (0.5 / D)
        upd = jnp.concatenate([p1, p2, jnp.zeros((6, D), f32)], axis=0)

        @pl.when(i == 0)
        def _():
            st_ref[...] = upd

        @pl.when(i > 0)
        def _():
            st_ref[...] += upd

    tile = pl.BlockSpec((tm, D), lambda i: (i, 0))
    return pl.pallas_call(
        body, name="final_loss", grid=(S // tm,),
        in_specs=[tile, pl.BlockSpec((1, D), lambda i: (0, 0)), tile],
        out_specs=[tile, pl.BlockSpec((8, D), lambda i: (0, 0))],
        out_shape=[jax.ShapeDtypeStruct((S, D), f32), jax.ShapeDtypeStruct((8, D), f32)],
        compiler_params=_params(("arbitrary",)),
    )(x, fw, tgt)


def gate_bwd(dx, y, gate):
    S = dx.shape[0]
    tm = 512

    def body(dx_ref, y_ref, g_ref, dy_ref, st_ref):
        i = pl.program_id(0)
        dxv = dx_ref[...]
        dy_ref[...] = (g_ref[...] * dxv).astype(bf16)
        upd = jnp.concatenate([jnp.sum(dxv * y_ref[...], axis=0, keepdims=True), jnp.zeros((7, D), f32)], axis=0)

        @pl.when(i == 0)
        def _():
            st_ref[...] = upd

        @pl.when(i > 0)
        def _():
            st_ref[...] += upd

    tile = pl.BlockSpec((tm, D), lambda i: (i, 0))
    return pl.pallas_call(
        body, name="gate_bwd", grid=(S // tm,),
        in_specs=[tile, tile, pl.BlockSpec((1, D), lambda i: (0, 0))],
        out_specs=[tile, pl.BlockSpec((8, D), lambda i: (0, 0))],
        out_shape=[jax.ShapeDtypeStruct((S, D), bf16), jax.ShapeDtypeStruct((8, D), f32)],
        compiler_params=_params(("arbitrary",)),
    )(dx, y, gate)


def _chunk_mats(tm):
    r, c = _iota((tm, tm), 0), _iota((tm, tm), 1)
    same = jnp.right_shift(r, 6) == jnp.right_shift(c, 6)
    ltri = jnp.where(same & (c <= r), 1.0, 0.0).astype(f32)
    utri = jnp.where(same & (c >= r), 1.0, 0.0).astype(f32)
    bsame = jnp.where(same, 1.0, 0.0).astype(f32)
    return ltri, utri, bsame


def _gdn_scalars(ba, alog, dtb, ltri, bsame):
    beta = _sigmoid(ba[:, 0:16])
    u = ba[:, 16:32] + dtb
    neg_a = -jnp.exp(alog)
    g = neg_a * _softplus(u)
    gc = _nn(ltri, g, HI)
    glast = _nn(bsame, g, HI)
    return beta, u, neg_a, g, gc, glast


def _conv_taps(p_ref, halo_ref, first, gi):
    cs = slice(gi * 128, (gi + 1) * 128)
    cur = p_ref[:, cs]
    hal = jnp.where(first, 0.0, halo_ref[:, cs])
    ext = jnp.concatenate([hal, cur], axis=0)
    return [cur] + [pltpu.roll(ext, s, 0)[8:] for s in range(1, CONV_K)]


def _conv(taps, w):
    cv = taps[0] * w[3:4]
    for s in range(1, CONV_K):
        cv = cv + taps[s] * w[3 - s:4 - s]
    return cv


def _l2n(x):
    return x * lax.rsqrt(jnp.sum(x * x, axis=-1, keepdims=True) + EPS)


def _gdn_in_specs(tm, S):
    nb8 = tm // 8
    return [pl.BlockSpec((tm, G_CONV), lambda i: (i, 0)),
            pl.BlockSpec((8, G_CONV), lambda i: (jnp.maximum(i * nb8 - 1, 0), 0)),
            pl.BlockSpec((tm, 128), lambda i: (i, G_BA0 // 128))]


def gdn_pre(proj, conv_w, alog, dtb):
    S = proj.shape[0]
    tm = 256
    nch = tm // CHUNK

    def body(p_ref, halo_ref, ba_ref, w_ref, al_ref, dt_ref,
             q_ref, k_ref, kb_ref, kbg_ref, vb_ref, qd_ref, kd_ref, d_ref, gl_ref):
        first = pl.program_id(0) == 0
        ltri, _, bsame = _chunk_mats(tm)
        beta, _, _, _, gc, glast = _gdn_scalars(ba_ref[...], al_ref[...], dt_ref[...], ltri, bsame)
        eg, ek, egl = jnp.exp(gc), jnp.exp(glast - gc), jnp.exp(glast)
        eye = jnp.where(_iota((16, 16), 0) == _iota((16, 16), 1), 1.0, 0.0).astype(f32)
        gct = _nt(eye, gc, HI)
        low = _iota((CHUNK, CHUNK), 0) >= _iota((CHUNK, CHUNK), 1)

        def act(gi):
            return _silu(_conv(_conv_taps(p_ref, halo_ref, first, gi), w_ref[:, gi * 128:(gi + 1) * 128]))

        for j in range(GQK_H):
            js = slice(j * 128, (j + 1) * 128)
            qn = _l2n(act(j)) * (GHD ** -0.5)
            kn = _l2n(act(GQK_H + j))
            q_ref[:, js] = qn.astype(bf16)
            k_ref[:, js] = kn.astype(bf16)
            for e in range(2):
                h = 2 * j + e
                hs = slice(h * 128, (h + 1) * 128)
                v = act(2 * GQK_H + h)
                bh, egh, ekh = beta[:, h:h + 1], eg[:, h:h + 1], ek[:, h:h + 1]
                kbv = kn * bh
                kb_ref[:, hs] = kbv.astype(bf16)
                kbg_ref[:, hs] = (kbv * egh).astype(bf16)
                vb_ref[:, hs] = (v * bh).astype(bf16)
                qd_ref[:, hs] = (qn * egh).astype(bf16)
                kd_ref[:, hs] = (kn * ekh).astype(bf16)
                for c in range(nch):
                    rs = slice(c * CHUNK, (c + 1) * CHUNK)
                    diff = gc[rs, h:h + 1] - gct[h:h + 1, rs]
                    d_ref[rs, h * CHUNK:(h + 1) * CHUNK] = jnp.where(low, jnp.exp(jnp.where(low, diff, 0.0)), 0.0)
                    gl_ref[c * 8:(c + 1) * 8, hs] = jnp.broadcast_to(egl[c * CHUNK:c * CHUNK + 8, h:h + 1], (8, 128))

    full = lambda shape: pl.BlockSpec(shape, lambda i: (0, 0))
    t1 = pl.BlockSpec((tm, 1024), lambda i: (i, 0))
    t2 = pl.BlockSpec((tm, 2048), lambda i: (i, 0))
    sd = jax.ShapeDtypeStruct
    return pl.pallas_call(
        body, name="gdn_pre", grid=(S // tm,),
        in_specs=_gdn_in_specs(tm, S) + [full((CONV_K, G_CONV)), full((1, 16)), full((1, 16))],
        out_specs=[t1, t1, t2, t2, t2, t2, t2, t1, pl.BlockSpec((tm // 8, 2048), lambda i: (i, 0))],
        out_shape=[sd((S, 1024), bf16)] * 2 + [sd((S, 2048), bf16)] * 5 + [sd((S, 1024), f32), sd((S // 8, 2048), f32)],
        compiler_params=_params(("parallel",)),
    )(proj, proj, proj, conv_w, alog, dtb)


def _bnn(a, b):
    return lax.dot_general(a, b, (((2,), (1,)), ((0,), (0,))), preferred_element_type=f32)


def _bnt(a, b):
    return lax.dot_general(a, b, (((2,), (2,)), ((0,), (0,))), preferred_element_type=f32)


def _btn(a, b):
    return lax.dot_general(a, b, (((1,), (1,)), ((0,), (0,))), preferred_element_type=f32)


def _split(a):
    hi = a.astype(bf16)
    return hi, (a - hi.astype(f32)).astype(bf16)


def _cat3(h, l, axis, lhs):
    return jnp.concatenate([h, h, l] if lhs else [h, l, h], axis=axis)


def _tri_inv_b(L):
    eye = jnp.where(_iota((1, CHUNK, CHUNK), 1) == _iota((1, CHUNK, CHUNK), 2), 1.0, 0.0).astype(f32)
    P = -L
    T = eye + P
    ph, pl_ = _split(P)
    for _ in range(5):
        P = _bnn(_cat3(ph, pl_, 2, True), _cat3(ph, pl_, 1, False))
        ph, pl_ = _split(P)
        th, tl = _split(T)
        T = T + _bnn(_cat3(th, tl, 2, True), _cat3(ph, pl_, 1, False))
    return T


GTB = 512
GQH_FWD, GQH_BWD = 1, 2


def _gdn_slices(ncb, gnv):
    pairs = [(c, e) for c in range(ncb) for e in range(gnv)]
    rs = lambda c: slice(c * CHUNK, (c + 1) * CHUNK)
    cs = lambda e: slice(e * 128, (e + 1) * 128)
    ds_ = lambda e: slice(e * CHUNK, (e + 1) * CHUNK)
    ks = lambda e: slice((e // 2) * 128, (e // 2 + 1) * 128)
    return pairs, rs, cs, ds_, ks


def gdn_fwd(q, k, kb, kbg, vb, qd, kd, dm, gl8, comm=None):
    S = q.shape[0]
    nb, ncb = S // GTB, GTB // CHUNK
    GQH, GNV = GQH_FWD, 2 * GQH_FWD
    pairs, rs, cs, ds_, ks = _gdn_slices(ncb, GNV)

    def body(q_ref, k_ref, kb_ref, kbg_ref, vb_ref, qd_ref, kd_ref, d_ref, gl_ref,
             o_ref, w_ref, at_ref, t_ref, vn_ref, st_ref, state, u_scr):
        @pl.when(pl.program_id(1) == 0)
        def _():
            state[...] = jnp.zeros_like(state)

        stk = lambda ref, lanes: jnp.stack([ref[rs(c), lanes(e)] for c, e in pairs])
        kq = stk(k_ref, ks)
        dmat = stk(d_ref, ds_)
        strict = _iota((1, CHUNK, CHUNK), 1) > _iota((1, CHUNK, CHUNK), 2)
        T = _tri_inv_b(jnp.where(strict, _bnt(stk(kb_ref, cs), kq) * dmat, 0.0))
        tb = T.astype(bf16)
        u_scr[...] = _bnn(tb, stk(vb_ref, cs))
        wb = _bnn(tb, stk(kbg_ref, cs)).astype(bf16)
        per_qk = lambda ref: jnp.stack([ref[rs(c), ks(e)] for c, e in pairs if e % 2 == 0])
        qk = _bnt(per_qk(q_ref), per_qk(k_ref))
        for b, (c, e) in enumerate(pairs):
            w_ref[rs(c), cs(e)] = wb[b]
            at_ref[rs(c), ds_(e)] = (qk[b // 2] * dmat[b]).astype(bf16)
            t_ref[rs(c), ds_(e)] = T[b]
        for b, (c, e) in enumerate(pairs):
            sb = state[e].astype(bf16)
            vnb = (u_scr[b] - _nn(w_ref[rs(c), cs(e)], sb)).astype(bf16)
            o_ref[rs(c), cs(e)] = _nn(qd_ref[rs(c), cs(e)], sb) + _nn(at_ref[rs(c), ds_(e)], vnb)
            st_ref[c * 128:(c + 1) * 128, cs(e)] = sb
            state[e] = state[e] * gl_ref[c * 8:c * 8 + 1, cs(e)] + _tn(kd_ref[rs(c), cs(e)], vnb)
            vn_ref[rs(c), cs(e)] = vnb

    b1 = pl.BlockSpec((GTB, 128 * GQH), lambda j, i: (i, j))
    b2 = pl.BlockSpec((GTB, 256 * GQH), lambda j, i: (i, j))
    sd = jax.ShapeDtypeStruct
    return _call(
        body, name="gdn_fwd", grid=(GQK_H // GQH, nb),
        in_specs=[b1, b1, b2, b2, b2, b2, b2, b1, pl.BlockSpec((GTB // 8, 256 * GQH), lambda j, i: (i, j))],
        out_specs=[b2, b2, b1, b1, b2, pl.BlockSpec((ncb * 128, 256 * GQH), lambda j, i: (i, j))],
        out_shape=[sd((S, 2048), f32), sd((S, 2048), bf16), sd((S, 1024), bf16), sd((S, 1024), f32),
                   sd((S, 2048), bf16), sd((S // CHUNK * 128, 2048), bf16)],
        scratch_shapes=[pltpu.VMEM((GNV, 128, 128), f32), pltpu.VMEM((GNV * ncb, CHUNK, 128), f32)],
        sem=("parallel", "arbitrary"), args=(q, k, kb, kbg, vb, qd, kd, dm, gl8), comm=comm)


def gdn_bwd(do, q, k, kb, kbg, vb, qd, kd, dm, gl8, w, at, T, vn, st, comm=None):
    S = q.shape[0]
    nb, ncb = S // GTB, GTB // CHUNK
    GQH, GNV = GQH_BWD, 2 * GQH_BWD
    pairs, rs, cs, ds_, ks = _gdn_slices(ncb, GNV)

    def body(do_ref, q_ref, k_ref, kb_ref, kbg_ref, vb_ref, qd_ref, kd_ref, d_ref, gl_ref, w_ref, at_ref, t_ref, vn_ref, st_ref,
             dq_ref, dk_ref, dkb_ref, dkbg_ref, dvb_ref, dqd_ref, dkd_ref, dgc_ref, dstate, dvn_scr, dw_scr, dat_scr, dgl_scr):
        @pl.when(pl.program_id(1) == 0)
        def _():
            dstate[...] = jnp.zeros_like(dstate)

        for b, (c, e) in reversed(list(enumerate(pairs))):
            dob = do_ref[rs(c), cs(e)].astype(bf16)
            sb = st_ref[c * 128:(c + 1) * 128, cs(e)]
            vnb = vn_ref[rs(c), cs(e)]
            gl = gl_ref[c * 8:c * 8 + 1, cs(e)]
            dS = dstate[e]
            dsb = dS.astype(bf16)
            dvnb = (_tn(at_ref[rs(c), ds_(e)], dob) + _nn(kd_ref[rs(c), cs(e)], dsb)).astype(bf16)
            dvn_scr[b] = dvnb
            dat_scr[b] = _nt(dob, vnb)
            dqd_ref[rs(c), cs(e)] = _nt(dob, sb)
            dkd_ref[rs(c), cs(e)] = _nt(vnb, dsb)
            dw_scr[b] = (-_nt(dvnb, sb)).astype(bf16)
            dgl = jnp.sum(jnp.sum(dS * sb.astype(f32), axis=1, keepdims=True), axis=0, keepdims=True)
            dgl_scr[b] = jnp.broadcast_to(dgl * gl, (8, 128))
            dstate[e] = gl * dS + _tn(qd_ref[rs(c), cs(e)], dob) - _tn(w_ref[rs(c), cs(e)], dvnb)

        stk = lambda ref, lanes: jnp.stack([ref[rs(c), lanes(e)] for c, e in pairs])
        kq, qq = stk(k_ref, ks), stk(q_ref, ks)
        kbb = stk(kb_ref, cs)
        Tm = stk(t_ref, ds_)
        tb = Tm.astype(bf16)
        dvn, dw = dvn_scr[...], dw_scr[...]
        dT = _bnt(dvn, stk(vb_ref, cs)) + _bnt(dw, stk(kbg_ref, cs))
        dvb, dkbg = _btn(tb, dvn), _btn(tb, dw)
        th, tl = _split(Tm)
        xh, xl = _split(_bnt(_cat3(*_split(dT), 2, True), _cat3(th, tl, 2, False)))
        dL = -_btn(_cat3(th, tl, 1, True), _cat3(xh, xl, 1, False))
        dmat = stk(d_ref, ds_)
        strict = _iota((1, CHUNK, CHUNK), 1) > _iota((1, CHUNK, CHUNK), 2)
        dA = jnp.where(strict, dL * dmat, 0.0)
        dB = dat_scr[...] * dmat
        dAb, dBb = dA.astype(bf16), dB.astype(bf16)
        dkb = _bnn(dAb, kq)
        dkc = _btn(dAb, kbb) + _btn(dBb, qq)
        dqc = _bnn(dBb, kq)
        M = dA * _bnt(kbb, kq) + dB * _bnt(qq, kq)
        mh, ml = _split(M)
        colsum = _btn(jnp.concatenate([mh, ml], axis=1), jnp.ones((GNV * ncb, 2 * CHUNK, 128), bf16))
        lastrow = _iota((1, CHUNK, 128), 1) == CHUNK - 1
        for b, (c, e) in enumerate(pairs):
            dvb_ref[rs(c), cs(e)] = dvb[b]
            dkbg_ref[rs(c), cs(e)] = dkbg[b]
            dkb_ref[rs(c), cs(e)] = dkb[b]
            dgc_ref[rs(c), cs(e)] = (jnp.sum(M[b], axis=1, keepdims=True) - colsum[b]
                                     + jnp.where(lastrow[0], dgl_scr[b][0:1, :], 0.0))
        for b, (c, e) in enumerate(pairs):
            if e % 2 == 0:
                dq_ref[rs(c), ks(e)] = dqc[b] + dqc[b + 1]
                dk_ref[rs(c), ks(e)] = dkc[b] + dkc[b + 1]

    b1 = pl.BlockSpec((GTB, 128 * GQH), lambda j, i: (nb - 1 - i, j))
    b2 = pl.BlockSpec((GTB, 256 * GQH), lambda j, i: (nb - 1 - i, j))
    sd = jax.ShapeDtypeStruct
    return _call(
        body, name="gdn_bwd", grid=(GQK_H // GQH, nb),
        in_specs=[b2, b1, b1, b2, b2, b2, b2, b2, b1, pl.BlockSpec((GTB // 8, 256 * GQH), lambda j, i: (nb - 1 - i, j)),
                  b2, b1, b1, b2, pl.BlockSpec((ncb * 128, 256 * GQH), lambda j, i: (nb - 1 - i, j))],
        out_specs=[b1, b1, b2, b2, b2, b2, b2, b2],
        out_shape=[sd((S, 1024), f32)] * 2 + [sd((S, 2048), f32)] * 6,
        scratch_shapes=[pltpu.VMEM((GNV, 128, 128), f32), pltpu.VMEM((GNV * ncb, CHUNK, 128), bf16),
                        pltpu.VMEM((GNV * ncb, CHUNK, 128), bf16), pltpu.VMEM((GNV * ncb, CHUNK, CHUNK), f32),
                        pltpu.VMEM((GNV * ncb, 8, 128), f32)],
        sem=("parallel", "arbitrary"), args=(do, q, k, kb, kbg, vb, qd, kd, dm, gl8, w, at, T, vn, st), comm=comm)


def gdn_onorm(o, proj, nw):
    S = o.shape[0]
    tm = 256

    def body(o_ref, z_ref, nw_ref, o2_ref):
        for h in range(GV_H):
            hs = slice(h * 128, (h + 1) * 128)
            oh = o_ref[:, hs]
            r = lax.rsqrt(jnp.mean(oh * oh, axis=-1, keepdims=True) + EPS)
            o2_ref[:, hs] = (((oh * r) * nw_ref[...]) * _silu(z_ref[:, hs])).astype(bf16)

    t2 = pl.BlockSpec((tm, 2048), lambda i: (i, 0))
    return pl.pallas_call(
        body, name="gdn_onorm", grid=(S // tm,),
        in_specs=[t2, pl.BlockSpec((tm, 2048), lambda i: (i, G_Z0 // 2048)), pl.BlockSpec((1, 128), lambda i: (0, 0))],
        out_specs=t2, out_shape=jax.ShapeDtypeStruct((S, 2048), bf16),
        compiler_params=_params(("parallel",)),
    )(o, proj, nw)


def gdn_onorm_bwd(do2, o, proj, nw):
    S = o.shape[0]
    tm = 256

    def body(d_ref, o_ref, z_ref, nw_ref, do_ref, dz_ref, st_ref):
        i = pl.program_id(0)
        acc = jnp.zeros((1, 128), f32)
        for h in range(GV_H):
            hs = slice(h * 128, (h + 1) * 128)
            oh, z, d2 = o_ref[:, hs], z_ref[:, hs], d_ref[:, hs]
            r = lax.rsqrt(jnp.mean(oh * oh, axis=-1, keepdims=True) + EPS)
            on = oh * r
            dt = d2 * _silu(z)
            dz_ref[:, hs] = (d2 * (on * nw_ref[...]) * _dsilu(z)).astype(bf16)
            don = dt * nw_ref[...]
            acc = acc + jnp.sum(dt * on, axis=0, keepdims=True)
            do_ref[:, hs] = r * (don - on * jnp.mean(don * on, axis=-1, keepdims=True))
        upd = jnp.concatenate([acc, jnp.zeros((7, 128), f32)], axis=0)

        @pl.when(i == 0)
        def _():
            st_ref[...] = upd

        @pl.when(i > 0)
        def _():
            st_ref[...] += upd

    t2 = pl.BlockSpec((tm, 2048), lambda i: (i, 0))
    sd = jax.ShapeDtypeStruct
    return pl.pallas_call(
        body, name="gdn_onorm_bwd", grid=(S // tm,),
        in_specs=[t2, t2, pl.BlockSpec((tm, 2048), lambda i: (i, G_Z0 // 2048)), pl.BlockSpec((1, 128), lambda i: (0, 0))],
        out_specs=[t2, t2, pl.BlockSpec((8, 128), lambda i: (0, 0))],
        out_shape=[sd((S, 2048), f32), sd((S, 2048), bf16), sd((8, 128), f32)],
        compiler_params=_params(("arbitrary",)),
    )(do2, o, proj, nw)


def gdn_pre_bwd(proj, conv_w, alog, dtb, dq, dk, dkb, dkbg, dvb, dqd, dkd, dgcd):
    S = proj.shape[0]
    tm = 128

    def body(p_ref, halo_ref, ba_ref, w_ref, al_ref, dt_ref, dq_ref, dk_ref, dkb_ref, dkbg_ref, dvb_ref, dqd_ref, dkd_ref, dgc_ref,
             dcv_ref, dba_ref, st_ref):
        i = pl.program_id(0)
        first = i == 0
        ltri, utri, bsame = _chunk_mats(tm)
        beta, u, neg_a, g, gc, glast = _gdn_scalars(ba_ref[...], al_ref[...], dt_ref[...], ltri, bsame)
        eg, ek = jnp.exp(gc), jnp.exp(glast - gc)
        lane16 = _iota((tm, 16), 1)
        dgc_all = jnp.zeros((tm, 16), f32)
        rkd_all = jnp.zeros((tm, 16), f32)
        dbeta_all = jnp.zeros((tm, 16), f32)

        def pre(gi):
            return _conv(_conv_taps(p_ref, halo_ref, first, gi), w_ref[:, gi * 128:(gi + 1) * 128])

        def l2n_bwd(xt, dy):
            r = lax.rsqrt(jnp.sum(xt * xt, axis=-1, keepdims=True) + EPS)
            y = xt * r
            return r * (dy - y * jnp.sum(dy * y, axis=-1, keepdims=True))

        for j in range(GQK_H):
            js = slice(j * 128, (j + 1) * 128)
            cvq, cvk = pre(j), pre(GQK_H + j)
            qt, kt = _silu(cvq), _silu(cvk)
            qn = _l2n(qt) * (GHD ** -0.5)
            kn = _l2n(kt)
            dq_tot, dk_tot = dq_ref[:, js], dk_ref[:, js]
            for e in range(2):
                h = 2 * j + e
                hs = slice(h * 128, (h + 1) * 128)
                gv = 2 * GQK_H + h
                cvv = pre(gv)
                v = _silu(cvv)
                bh, egh, ekh = beta[:, h:h + 1], eg[:, h:h + 1], ek[:, h:h + 1]
                dkbg, dkd, dqd, dvb = dkbg_ref[:, hs], dkd_ref[:, hs], dqd_ref[:, hs], dvb_ref[:, hs]
                dkb_t = dkb_ref[:, hs] + dkbg * egh
                dk_tot = dk_tot + dkb_t * bh + dkd * ekh
                dq_tot = dq_tot + dqd * egh
                dcv_ref[:, gv * 128:(gv + 1) * 128] = (dvb * bh) * _dsilu(cvv)
                dbeta = jnp.sum(dkb_t * kn, axis=-1, keepdims=True) + jnp.sum(dvb * v, axis=-1, keepdims=True)
                rkd = jnp.sum(dkd * (kn * ekh), axis=-1, keepdims=True)
                dgc = (dgc_ref[:, hs][:, 0:1] + jnp.sum(dkbg * (kn * bh * egh), axis=-1, keepdims=True)
                       + jnp.sum(dqd * (qn * egh), axis=-1, keepdims=True) - rkd)
                sel = lane16 == h
                dgc_all = dgc_all + jnp.where(sel, dgc, 0.0)
                rkd_all = rkd_all + jnp.where(sel, rkd, 0.0)
                dbeta_all = dbeta_all + jnp.where(sel, dbeta, 0.0)
            dcv_ref[:, js] = l2n_bwd(qt, dq_tot * (GHD ** -0.5)) * _dsilu(cvq)
            ks = slice((GQK_H + j) * 128, (GQK_H + j + 1) * 128)
            dcv_ref[:, ks] = l2n_bwd(kt, dk_tot) * _dsilu(cvk)

        islast = jnp.bitwise_and(_iota((tm, 16), 0), CHUNK - 1) == CHUNK - 1
        dgc_all = dgc_all + jnp.where(islast, _nn(bsame, rkd_all, HI), 0.0)
        dg = _nn(utri, dgc_all, HI)
        da = dg * neg_a * _sigmoid(u)
        db = dbeta_all * beta * (1.0 - beta)
        r16, c128 = _iota((16, 128), 0), _iota((16, 128), 1)
        pb = jnp.where(c128 == r16, 1.0, 0.0).astype(f32)
        pa = jnp.where(c128 == r16 + 16, 1.0, 0.0).astype(f32)
        dba_ref[...] = _nn(db, pb, HI) + _nn(da, pa, HI)
        upd = jnp.concatenate([jnp.sum(dg * g, axis=0, keepdims=True), jnp.sum(da, axis=0, keepdims=True),
                               jnp.zeros((6, 16), f32)], axis=0)

        @pl.when(i == 0)
        def _():
            st_ref[...] = upd

        @pl.when(i > 0)
        def _():
            st_ref[...] += upd

    full = lambda shape: pl.BlockSpec(shape, lambda i: (0, 0))
    t1 = pl.BlockSpec((tm, 1024), lambda i: (i, 0))
    t2 = pl.BlockSpec((tm, 2048), lambda i: (i, 0))
    sd = jax.ShapeDtypeStruct
    return pl.pallas_call(
        body, name="gdn_pre_bwd", grid=(S // tm,),
        in_specs=_gdn_in_specs(tm, S) + [full((CONV_K, G_CONV)), full((1, 16)), full((1, 16)), t1, t1] + [t2] * 6,
        out_specs=[pl.BlockSpec((tm, G_CONV), lambda i: (i, 0)), pl.BlockSpec((tm, 128), lambda i: (i, 0)), full((8, 16))],
        out_shape=[sd((S, G_CONV), f32), sd((S, 128), f32), sd((8, 16), f32)],
        compiler_params=_params(("arbitrary",)),
    )(proj, proj, proj, conv_w, alog, dtb, dq, dk, dkb, dkbg, dvb, dqd, dkd, dgcd)


def gdn_conv_bwd(proj, conv_w, dcv, dz, dba):
    S = proj.shape[0]
    tm = 256
    nb, nb8 = S // tm, tm // 8

    def body(p_ref, halo_ref, w_ref, dcv_ref, nxt_ref, dz_ref, dba_ref, dp_ref, dw_ref):
        i = pl.program_id(0)
        first, last = i == 0, i == nb - 1
        for gi in range(G_CONV // 128):
            cs = slice(gi * 128, (gi + 1) * 128)
            taps = _conv_taps(p_ref, halo_ref, first, gi)
            cur = dcv_ref[:, cs]
            ext = jnp.concatenate([cur, jnp.where(last, 0.0, nxt_ref[:, cs])], axis=0)
            w = w_ref[:, cs]
            dp = cur * w[3:4]
            rows = [jnp.sum(cur * taps[3 - kk], axis=0, keepdims=True) for kk in range(CONV_K)]
            for s in range(1, CONV_K):
                dp = dp + pltpu.roll(ext, tm + 8 - s, 0)[:tm] * w[3 - s:4 - s]
            dp_ref[:, cs] = dp.astype(bf16)
            upd = jnp.concatenate(rows + [jnp.zeros((4, 128), f32)], axis=0)

            @pl.when(first)
            def _():
                dw_ref[:, cs] = upd

            @pl.when(i > 0)
            def _():
                dw_ref[:, cs] += upd

        dp_ref[:, G_Z0:G_BA0] = dz_ref[...]
        dp_ref[:, G_BA0:G_INP] = dba_ref[...].astype(bf16)

    sd = jax.ShapeDtypeStruct
    return pl.pallas_call(
        body, name="gdn_conv_bwd", grid=(nb,),
        in_specs=[pl.BlockSpec((tm, G_CONV), lambda i: (i, 0)),
                  pl.BlockSpec((8, G_CONV), lambda i: (jnp.maximum(i * nb8 - 1, 0), 0)),
                  pl.BlockSpec((CONV_K, G_CONV), lambda i: (0, 0)),
                  pl.BlockSpec((tm, G_CONV), lambda i: (i, 0)),
                  pl.BlockSpec((8, G_CONV), lambda i: (jnp.minimum((i + 1) * nb8, S // 8 - 1), 0)),
                  pl.BlockSpec((tm, 2048), lambda i: (i, 0)), pl.BlockSpec((tm, 128), lambda i: (i, 0))],
        out_specs=[pl.BlockSpec((tm, G_INP), lambda i: (i, 0)), pl.BlockSpec((8, G_CONV), lambda i: (0, 0))],
        out_shape=[sd((S, G_INP), bf16), sd((8, G_CONV), f32)],
        compiler_params=_params(("arbitrary",)),
    )(proj, proj, conv_w, dcv, dcv, dz, dba)


def _half_mean(t, lo_half):
    m0 = jnp.sum(jnp.where(lo_half, t, 0.0), axis=-1, keepdims=True)
    m1 = jnp.sum(jnp.where(lo_half, 0.0, t), axis=-1, keepdims=True)
    return jnp.where(lo_half, m0, m1) * (1.0 / F_HD)


def _split3(c):
    hi = c.astype(bf16).astype(f32)
    mid = (c - hi).astype(bf16).astype(f32)
    lo = (c - hi - mid).astype(bf16).astype(f32)
    return hi, mid, lo


def fox_pre(proj, fbias, qw2, kw2):
    S = proj.shape[0]
    tm = 256

    def body(q_ref, k_ref, v_ref, f_ref, fb_ref, qw_ref, kw_ref, qa_ref, ka_ref, vb_ref, carry):
        @pl.when(pl.program_id(0) == 0)
        def _():
            carry[...] = jnp.zeros_like(carry)

        logf = -_softplus(-(f_ref[:, 0:16] + fb_ref[...]))
        ltri = jnp.where(_iota((tm, tm), 1) <= _iota((tm, tm), 0), 1.0, 0.0).astype(f32)
        cum = _nn(ltri, logf, HI) + carry[0:1, :]
        carry[0:1, :] = cum[tm - 1:tm, :]
        lane = _iota((tm, 128), 1)
        lo_half = lane < F_HD
        for p in range(F_H // 2):
            ps = slice(p * 128, (p + 1) * 128)
            for src, w_ref, dst, is_q in ((q_ref, qw_ref, qa_ref, True), (k_ref, kw_ref, ka_ref, False)):
                x = src[:, ps]
                xn = x * lax.rsqrt(_half_mean(x * x, lo_half) + EPS) * w_ref[...]
                if is_q:
                    xn = xn * (F_HD ** -0.5)
                for e in range(2):
                    h = 2 * p + e
                    base = xn if e == 0 else pltpu.roll(xn, F_HD, 1)
                    hi, mid, lo = _split3(cum[:, h:h + 1])
                    pieces = jnp.where(lane == 64, hi, 0.0) + jnp.where(lane == 65, mid, 0.0) + jnp.where(lane == 66, lo, 0.0)
                    if is_q:
                        ext = pieces + jnp.where((lane >= 67) & (lane <= 69), 1.0, 0.0)
                    else:
                        ext = jnp.where((lane >= 64) & (lane <= 66), 1.0, 0.0) - pltpu.roll(pieces, 3, 1)
                    dst[:, h * 128:(h + 1) * 128] = jnp.where(lo_half, base, ext).astype(bf16)
        one = jnp.where(lane == F_HD, 1.0, 0.0)
        for p in range(F_H // 2):
            vv = v_ref[:, p * 128:(p + 1) * 128]
            vb_ref[:, (2 * p) * 128:(2 * p + 1) * 128] = jnp.where(lo_half, vv, one).astype(bf16)
            vb_ref[:, (2 * p + 1) * 128:(2 * p + 2) * 128] = jnp.where(lo_half, pltpu.roll(vv, F_HD, 1), one).astype(bf16)

    t1 = lambda c: pl.BlockSpec((tm, 1024), lambda i: (i, c))
    vec = lambda n: pl.BlockSpec((1, n), lambda i: (0, 0))
    sd = jax.ShapeDtypeStruct
    return pl.pallas_call(
        body, name="fox_pre", grid=(S // tm,),
        in_specs=[t1(0), t1(1), t1(2), pl.BlockSpec((tm, 128), lambda i: (i, F_F0 // 128)), vec(16), vec(128), vec(128)],
        out_specs=[pl.BlockSpec((tm, 2048), lambda i: (i, 0))] * 3,
        out_shape=[sd((S, 2048), bf16)] * 3,
        scratch_shapes=[pltpu.VMEM((8, 16), f32)],
        compiler_params=_params(("arbitrary",)),
    )(proj, proj, proj, proj, fbias, qw2, kw2)


FTQ = 512
FHS_FWD, FHS_BWD = 8, 4


def fox_attn(qa, ka, v, comm=None):
    S = qa.shape[0]
    nq = S // FTQ
    FHS = FHS_FWD

    live = [(i, j) for i in range(nq) for j in range(i + 1)]
    qi_tab = jnp.asarray([i for i, _ in live], jnp.int32)
    kj_tab = jnp.asarray([j for _, j in live], jnp.int32)

    def body(qi_ref, kj_ref, q_ref, k_ref, v_ref, o_ref, lse_ref, m_scr, acc_scr):
        t = pl.program_id(1)
        i, j = qi_ref[t], kj_ref[t]

        @pl.when(j == 0)
        def _():
            m_scr[...] = jnp.full_like(m_scr, NEG)
            acc_scr[...] = jnp.zeros_like(acc_scr)

        def step(diagonal):
            for e in range(FHS):
                es = slice(e * 128, (e + 1) * 128)
                s = _nt(q_ref[:, es], k_ref[:, es])
                if diagonal:
                    s = jnp.where(_iota((FTQ, FTQ), 0) >= _iota((FTQ, FTQ), 1), s, NEG)
                m_old = m_scr[e]
                m_new = jnp.maximum(m_old, jnp.max(s, axis=-1, keepdims=True))
                p = jnp.exp(s - m_new[:, 0:1])
                acc_scr[e] = acc_scr[e] * jnp.exp(m_old - m_new) + _nn(p.astype(bf16), v_ref[:, es])
                m_scr[e] = m_new

        pl.when(j < i)(functools.partial(step, False))

        @pl.when(j == i)
        def _():
            step(True)
            for e in range(FHS):
                vs = slice(e * F_HD, (e + 1) * F_HD)
                acc = acc_scr[e]
                l = acc[:, F_HD:F_HD + 1]
                o_ref[:, vs] = acc[:, 0:F_HD] / l
                lse_ref[:, vs] = m_scr[e][:, 0:F_HD] + jnp.log(l)

    sd = jax.ShapeDtypeStruct
    qo = pl.BlockSpec((FTQ, F_HD * FHS), lambda p, t, qi, kj: (qi[t], p))
    kv = pl.BlockSpec((FTQ, 128 * FHS), lambda p, t, qi, kj: (kj[t], p))
    return _call(
        body, name="fox_attn", grid=(F_H // FHS, len(live)),
        in_specs=[pl.BlockSpec((FTQ, 128 * FHS), lambda p, t, qi, kj: (qi[t], p)), kv, kv],
        out_specs=[qo, qo],
        out_shape=[sd((S, 1024), f32), sd((S, 1024), f32)],
        scratch_shapes=[pltpu.VMEM((FHS, FTQ, 128), f32), pltpu.VMEM((FHS, FTQ, 128), f32)],
        sem=("parallel", "arbitrary"), args=(qa, ka, v), comm=comm, prefetch=(qi_tab, kj_tab))


def fox_attn_bwd(qa, ka, v, do, lse, delta, comm=None):
    S = qa.shape[0]
    nq = S // FTQ
    FHS = FHS_BWD

    live = [(j, i) for j in range(nq) for i in range(j, nq)]
    kj_tab = jnp.asarray([j for j, _ in live], jnp.int32)
    qi_tab = jnp.asarray([i for _, i in live], jnp.int32)

    def body(kj_ref, qi_ref, q_ref, k_ref, v_ref, do_ref, lse_ref, dl_ref, dq_ref, dk_ref, dv_ref, dk_scr, dv_scr):
        t = pl.program_id(1)
        j, i = kj_ref[t], qi_ref[t]

        @pl.when(t == 0)
        def _():
            dq_ref[...] = jnp.zeros_like(dq_ref)

        @pl.when(i == j)
        def _():
            dk_scr[...] = jnp.zeros_like(dk_scr)
            dv_scr[...] = jnp.zeros_like(dv_scr)

        def step(diagonal):
            rows = pl.ds(pl.multiple_of(i * FTQ, FTQ), FTQ)
            for e in range(FHS):
                es, vs = slice(e * 128, (e + 1) * 128), slice(e * F_HD, (e + 1) * F_HD)
                qe, ke = q_ref[:, es], k_ref[:, es]
                dob = do_ref[:, vs]
                s = _nt(qe, ke)
                if diagonal:
                    s = jnp.where(_iota((FTQ, FTQ), 0) >= _iota((FTQ, FTQ), 1), s, NEG)
                p = jnp.exp(s - lse_ref[:, e * F_HD:e * F_HD + 1])
                ds = p * (_nt(dob, v_ref[:, e * 128:e * 128 + F_HD]) - dl_ref[:, e * F_HD:e * F_HD + 1])
                dsb = ds.astype(bf16)
                dv_scr[e] += _tn(dob, p.astype(bf16))
                dk_scr[e] += _tn(qe, dsb)
                dq_ref[rows, es] += _nn(dsb, ke)

        pl.when(i > j)(functools.partial(step, False))
        pl.when(i == j)(functools.partial(step, True))

        @pl.when(i == nq - 1)
        def _():
            for e in range(FHS):
                dk_ref[:, e * 128:(e + 1) * 128] = dk_scr[e].T
                dv_ref[:, e * F_HD:(e + 1) * F_HD] = dv_scr[e].T

    sd = jax.ShapeDtypeStruct
    qi = lambda w: pl.BlockSpec((FTQ, w * FHS), lambda p, t, kj_, qi_: (qi_[t], p))
    kj = lambda w: pl.BlockSpec((FTQ, w * FHS), lambda p, t, kj_, qi_: (kj_[t], p))
    return _call(
        body, name="fox_attn_bwd", grid=(F_H // FHS, len(live)),
        in_specs=[qi(128), kj(128), kj(128), qi(F_HD), qi(F_HD), qi(F_HD)],
        out_specs=[pl.BlockSpec((S, 128 * FHS), lambda p, t, kj_, qi_: (0, p)), kj(128), kj(F_HD)],
        out_shape=[sd((S, 2048), f32), sd((S, 2048), f32), sd((S, 1024), f32)],
        scratch_shapes=[pltpu.VMEM((FHS, 128, FTQ), f32), pltpu.VMEM((FHS, F_HD, FTQ), f32)],
        sem=("parallel", "arbitrary"), args=(qa, ka, v, do, lse, delta), comm=comm, prefetch=(kj_tab, qi_tab))


def fox_gate(o, proj):
    S = o.shape[0]
    tm = 512

    def body(o_ref, z_ref, o2_ref):
        o2_ref[...] = (o_ref[...] * _silu(z_ref[...])).astype(bf16)

    t = pl.BlockSpec((tm, 1024), lambda i: (i, 0))
    return pl.pallas_call(
        body, name="fox_gate", grid=(S // tm,),
        in_specs=[t, pl.BlockSpec((tm, 1024), lambda i: (i, 3))], out_specs=t,
        out_shape=jax.ShapeDtypeStruct((S, 1024), bf16),
        compiler_params=_params(("parallel",)),
    )(o, proj)


def fox_gate_bwd(do2, o, proj):
    S = o.shape[0]
    tm = 256

    def body(d_ref, o_ref, z_ref, do_ref, dz_ref, dl_ref):
        lo_half = _iota((tm, 128), 1) < F_HD
        for p in range(F_H // 2):
            ps = slice(p * 128, (p + 1) * 128)
            d2, ov, z = d_ref[:, ps], o_ref[:, ps], z_ref[:, ps]
            dov = d2 * _silu(z)
            do_ref[:, ps] = dov.astype(bf16)
            dz_ref[:, ps] = (d2 * ov * _dsilu(z)).astype(bf16)
            dl_ref[:, ps] = _half_mean(dov * ov, lo_half) * float(F_HD)

    t = pl.BlockSpec((tm, 1024), lambda i: (i, 0))
    sd = jax.ShapeDtypeStruct
    return pl.pallas_call(
        body, name="fox_gate_bwd", grid=(S // tm,),
        in_specs=[t, t, pl.BlockSpec((tm, 1024), lambda i: (i, 3))], out_specs=[t, t, t],
        out_shape=[sd((S, 1024), bf16), sd((S, 1024), bf16), sd((S, 1024), f32)],
        compiler_params=_params(("parallel",)),
    )(do2, o, proj)


def fox_pre_bwd(proj, fbias, qw2, kw2, dqa, dka, dv, dz):
    S = proj.shape[0]
    tm = 256
    nb = S // tm

    def body(q_ref, k_ref, f_ref, fb_ref, qw_ref, kw_ref, dqa_ref, dka_ref, dv_ref, dz_ref, dp_ref, st_ref, carry):
        i = pl.program_id(0)

        @pl.when(i == 0)
        def _():
            carry[...] = jnp.zeros_like(carry)

        lane = _iota((tm, 128), 1)
        lo_half = lane < F_HD
        lane16 = _iota((tm, 16), 1)
        dcum = jnp.zeros((tm, 16), f32)
        dws = []
        for src, w_ref, dsrc, is_q, col0 in ((q_ref, qw_ref, dqa_ref, True, 0), (k_ref, kw_ref, dka_ref, False, 1024)):
            dw = jnp.zeros((1, 128), f32)
            for p in range(F_H // 2):
                ps = slice(p * 128, (p + 1) * 128)
                x = src[:, ps]
                r = lax.rsqrt(_half_mean(x * x, lo_half) + EPS)
                xh = x * r
                d0 = dsrc[:, (2 * p) * 128:(2 * p + 1) * 128]
                d1 = dsrc[:, (2 * p + 1) * 128:(2 * p + 2) * 128]
                dy = jnp.where(lo_half, d0, pltpu.roll(d1, F_HD, 1))
                if is_q:
                    dy = dy * (F_HD ** -0.5)
                dxh = dy * w_ref[...]
                dw = dw + jnp.sum(dy * xh, axis=0, keepdims=True)
                dp_ref[:, col0 + p * 128:col0 + (p + 1) * 128] = (r * (dxh - xh * _half_mean(dxh * xh, lo_half))).astype(bf16)
                for e, de in ((0, d0), (1, d1)):
                    col = de[:, 64:65] if is_q else -de[:, 67:68]
                    dcum = dcum + jnp.where(lane16 == 2 * p + e, col, 0.0)
            dws.append(dw)
        dp_ref[:, 2048:3072] = dv_ref[...].astype(bf16)
        dp_ref[:, 3072:4096] = dz_ref[...]
        utri = jnp.where(_iota((tm, tm), 1) >= _iota((tm, tm), 0), 1.0, 0.0).astype(f32)
        dlogf = _nn(utri, dcum, HI) + carry[0:1, :]
        carry[0:1, :] = dlogf[0:1, :]
        fl = f_ref[:, 0:16] + fb_ref[...]
        df = dlogf * _sigmoid(-fl)
        place = jnp.where(_iota((16, 128), 1) == _iota((16, 128), 0), 1.0, 0.0).astype(f32)
        dfw = _nn(df, place, HI)
        dp_ref[:, F_F0:F_INP] = dfw.astype(bf16)
        upd = jnp.concatenate(dws + [jnp.sum(dfw, axis=0, keepdims=True), jnp.zeros((5, 128), f32)], axis=0)

        @pl.when(i == 0)
        def _():
            st_ref[...] = upd

        @pl.when(i > 0)
        def _():
            st_ref[...] += upd

    rev = lambda w, c: pl.BlockSpec((tm, w), lambda i: (nb - 1 - i, c))
    vec = lambda n: pl.BlockSpec((1, n), lambda i: (0, 0))
    sd = jax.ShapeDtypeStruct
    return pl.pallas_call(
        body, name="fox_pre_bwd", grid=(nb,),
        in_specs=[rev(1024, 0), rev(1024, 1), rev(128, F_F0 // 128), vec(16), vec(128), vec(128),
                  rev(2048, 0), rev(2048, 0), rev(1024, 0), rev(1024, 0)],
        out_specs=[rev(F_INP, 0), pl.BlockSpec((8, 128), lambda i: (0, 0))],
        out_shape=[sd((S, F_INP), bf16), sd((8, 128), f32)],
        scratch_shapes=[pltpu.VMEM((8, 16), f32)],
        compiler_params=_params(("arbitrary",)),
    )(proj, proj, proj, fbias, qw2, kw2, dqa, dka, dv, dz)


def _me():
    return lax.axis_index("x"), lax.axis_index("y"), lax.axis_index("c")


def _other_chips(x, y):
    return [(1 - x, y), (x, 1 - y), (1 - x, 1 - y)]


def ag_small(xs):
    m_per, n = xs.shape

    def body(x_ref, out_ref, send_sems, recv_sems, local_sem):
        x, y, c = _me()
        me, sibling = (x, y, c), (x, y, 1 - c)
        chips = _other_chips(x, y)

        def rows(px, py, pc):
            return out_ref.at[pl.ds((4 * px + 2 * py + pc) * m_per, m_per), :]

        def copy(k, block, to, src=None):
            return pltpu.make_async_remote_copy(
                src_ref=rows(*block) if src is None else src, dst_ref=rows(*block),
                send_sem=send_sems.at[k], recv_sem=recv_sems.at[k], device_id=to, device_id_type=MESH)

        mine = pltpu.make_async_copy(x_ref, rows(*me), local_sem)
        mine.start()
        first = [copy(0, me, sibling, src=x_ref)]
        first += [copy(1 + j, me, (*chip, c), src=x_ref) for j, chip in enumerate(chips)]
        for cp in first:
            cp.start()
        passed = [copy(4 + j, (*chip, c), sibling) for j, chip in enumerate(chips)]
        for j, chip in enumerate(chips):
            copy(1 + j, (*chip, c), me).wait_recv()
            passed[j].start()
        copy(0, sibling, me).wait_recv()
        for j, chip in enumerate(chips):
            copy(4 + j, (*chip, 1 - c), me).wait_recv()
        for cp in first + passed:
            cp.wait_send()
        mine.wait()

    return pl.pallas_call(
        body, name="ag_small",
        out_shape=jax.ShapeDtypeStruct((8 * m_per, n), xs.dtype),
        in_specs=[pl.BlockSpec(memory_space=pltpu.VMEM)], out_specs=pl.BlockSpec(memory_space=pltpu.VMEM),
        scratch_shapes=[pltpu.SemaphoreType.DMA((7,)), pltpu.SemaphoreType.DMA((7,)), pltpu.SemaphoreType.DMA],
        compiler_params=pltpu.CompilerParams(vmem_limit_bytes=VMEM_LIMIT),
    )(xs)


_ANY = pl.BlockSpec(memory_space=pl.ANY)


def ag_chips(arrs):
    n = len(arrs)
    assert all(a.shape[0] == 2 for a in arrs)

    def body(*refs):
        ins, outs = refs[:n], refs[n:2 * n]
        send_sems, recv_sems, fwd_send, fwd_recv, local_sems = refs[2 * n:]
        x, y, c = _me()
        me = 2 * x + y
        chips = _other_chips(x, y)
        started = []
        for a in range(n):
            cp = pltpu.make_async_copy(ins[a], outs[a].at[me], local_sems.at[a])
            cp.start()
            started.append(cp)
        sends = []
        for a in range(n):
            for j, (px, py) in enumerate(chips):
                r = pltpu.make_async_remote_copy(
                    src_ref=ins[a].at[c], dst_ref=outs[a].at[me, c], send_sem=send_sems.at[3 * a + j],
                    recv_sem=recv_sems.at[3 * a + j], device_id=(px, py, c), device_id_type=MESH)
                r.start()
                sends.append(r)
        for a in range(n):
            for j, (px, py) in enumerate(chips):
                got = outs[a].at[2 * px + py, c]
                pltpu.make_async_remote_copy(
                    src_ref=ins[a].at[c], dst_ref=got, send_sem=send_sems.at[3 * a + j],
                    recv_sem=recv_sems.at[3 * a + j], device_id=(px, py, c), device_id_type=MESH).wait_recv()
                f = pltpu.make_async_remote_copy(
                    src_ref=got, dst_ref=got, send_sem=fwd_send.at[3 * a + j], recv_sem=fwd_recv.at[3 * a + j],
                    device_id=(x, y, 1 - c), device_id_type=MESH)
                f.start()
                sends.append(f)
        for a in range(n):
            for j, (px, py) in enumerate(chips):
                theirs = outs[a].at[2 * px + py, 1 - c]
                pltpu.make_async_remote_copy(
                    src_ref=theirs, dst_ref=theirs, send_sem=fwd_send.at[3 * a + j], recv_sem=fwd_recv.at[3 * a + j],
                    device_id=(x, y, 1 - c), device_id_type=MESH).wait_recv()
        for r in sends:
            r.wait_send()
        for cp in started:
            cp.wait()

    sems = pltpu.SemaphoreType.DMA((3 * n,))
    return pl.pallas_call(
        body, name="ag_chips",
        out_shape=[jax.ShapeDtypeStruct((4,) + a.shape, a.dtype) for a in arrs],
        in_specs=[_ANY] * n, out_specs=[_ANY] * n,
        scratch_shapes=[sems, sems, sems, sems, pltpu.SemaphoreType.DMA((n,))],
    )(*arrs)


def _ag_comm(arrs):
    n = len(arrs)

    def copies(ins, outs, sems, inbound):
        send_sems, recv_sems, local_sems = sems
        x, y, c = _me()
        me = 2 * x + y
        local = [pltpu.make_async_copy(ins[a], outs[a].at[me], local_sems.at[a]) for a in range(n)]
        out_cp, in_cp = [], []
        for a in range(n):
            for j, (px, py) in enumerate(_other_chips(x, y)):
                mk = functools.partial(pltpu.make_async_remote_copy, src_ref=ins[a], send_sem=send_sems.at[3 * a + j],
                                       recv_sem=recv_sems.at[3 * a + j], device_id=(px, py, c), device_id_type=MESH)
                out_cp.append(mk(dst_ref=outs[a].at[me]))
                if inbound:
                    in_cp.append(mk(dst_ref=outs[a].at[2 * px + py]))
        return local, out_cp, in_cp

    def start(ins, outs, sems):
        local, out_cp, _ = copies(ins, outs, sems, False)
        for cp in local + out_cp:
            cp.start()

    def wait(ins, outs, sems):
        local, out_cp, in_cp = copies(ins, outs, sems, True)
        for cp in in_cp:
            cp.wait_recv()
        for cp in out_cp:
            cp.wait_send()
        for cp in local:
            cp.wait()

    sems = [pltpu.SemaphoreType.DMA((3 * n,)), pltpu.SemaphoreType.DMA((3 * n,)), pltpu.SemaphoreType.DMA((n,))]
    return _Comm(arrs, [jax.ShapeDtypeStruct((4,) + a.shape, a.dtype) for a in arrs], sems, start, wait)


def _rs_comm(gs):
    n = len(gs)
    flips = [(fx, fy, fc) for fx in (0, 1) for fy in (0, 1) for fc in (0, 1)][1:]

    def copies(ins, outs, sems, inbound):
        send_sems, recv_sems, local_sems = sems
        x, y, c = _me()
        me = 4 * x + 2 * y + c
        local, out_cp, in_cp = [], [], []
        for a in range(n):
            rh = ins[a].shape[1] // 2
            mine = ins[a].at[2 * x + y, pl.ds(c * rh, rh), :]
            local.append(pltpu.make_async_copy(mine, outs[a].at[me], local_sems.at[a]))
            for j, (fx, fy, fc) in enumerate(flips):
                px, py, pc = (1 - x if fx else x), (1 - y if fy else y), (1 - c if fc else c)
                mk = functools.partial(pltpu.make_async_remote_copy, send_sem=send_sems.at[7 * a + j],
                                       recv_sem=recv_sems.at[7 * a + j], device_id=(px, py, pc), device_id_type=MESH)
                out_cp.append(mk(src_ref=ins[a].at[2 * px + py, pl.ds(pc * rh, rh), :], dst_ref=outs[a].at[me]))
                if inbound:
                    in_cp.append(mk(src_ref=mine, dst_ref=outs[a].at[4 * px + 2 * py + pc]))
        return local, out_cp, in_cp

    def start(ins, outs, sems):
        local, out_cp, _ = copies(ins, outs, sems, False)
        for cp in local + out_cp:
            cp.start()

    def wait(ins, outs, sems):
        local, out_cp, in_cp = copies(ins, outs, sems, True)
        for cp in in_cp:
            cp.wait_recv()
        for cp in out_cp:
            cp.wait_send()
        for cp in local:
            cp.wait()

    sems = [pltpu.SemaphoreType.DMA((7 * n,)), pltpu.SemaphoreType.DMA((7 * n,)), pltpu.SemaphoreType.DMA((n,))]
    return _Comm(gs, [jax.ShapeDtypeStruct((8, g.shape[1] // 2, g.shape[2]), g.dtype) for g in gs], sems, start, wait)


def sum_leading(q, name):
    K, R, C = q.shape
    tr = _pick(R, (256, 128, 64, 32, 16, 8))

    def body(q_ref, o_ref):
        acc = q_ref[0]
        for k in range(1, K):
            acc = acc + q_ref[k]
        o_ref[...] = acc

    return pl.pallas_call(
        body, name=name, grid=(R // tr,),
        in_specs=[pl.BlockSpec((K, tr, C), lambda i: (0, i, 0))], out_specs=pl.BlockSpec((tr, C), lambda i: (i, 0)),
        out_shape=jax.ShapeDtypeStruct((R, C), f32),
        compiler_params=_params(("parallel",)),
    )(q)


def rs_sum_devices(q, cidx, layer, n_layers, into=None):
    K, R, C = q.shape
    tr = _pick(R, (256, 128))

    def body(c_ref, q_ref, *rest):
        acc = q_ref[0].astype(f32)
        for k in range(1, K):
            acc = acc + q_ref[k].astype(f32)
        rest[-1][0, 0] = acc

    return pl.pallas_call(
        body, name="rs_sum_devices",
        grid_spec=pltpu.PrefetchScalarGridSpec(
            num_scalar_prefetch=1, grid=(R // tr,),
            in_specs=[pl.BlockSpec((K, tr, C), lambda i, c_ref: (0, i, 0))] + ([] if into is None else [_ANY]),
            out_specs=pl.BlockSpec((1, 1, tr, C), lambda i, c_ref: (layer, c_ref[0], i, 0))),
        out_shape=jax.ShapeDtypeStruct((n_layers, 2, R, C), f32),
        input_output_aliases={} if into is None else {2: 0},
        compiler_params=_params(("parallel",)),
    )(cidx, q, *([] if into is None else [into]))


def rs_share_halves(rs):
    n = len(rs)

    def body(*refs):
        bufs = refs[n:2 * n]
        send_sems, recv_sems = refs[2 * n:]
        x, y, c = _me()
        cps = []
        for a in range(n):
            mine = bufs[a].at[pl.ds(0, bufs[a].shape[0]), c]
            cp = pltpu.make_async_remote_copy(
                src_ref=mine, dst_ref=mine, send_sem=send_sems.at[a], recv_sem=recv_sems.at[a],
                device_id=(x, y, 1 - c), device_id_type=MESH)
            cp.start()
            cps.append(cp)
        for a, cp in enumerate(cps):
            theirs = bufs[a].at[pl.ds(0, bufs[a].shape[0]), 1 - c]
            pltpu.make_async_remote_copy(
                src_ref=theirs, dst_ref=theirs, send_sem=send_sems.at[a], recv_sem=recv_sems.at[a],
                device_id=(x, y, 1 - c), device_id_type=MESH).wait_recv()
            cp.wait_send()

    return pl.pallas_call(
        body, name="rs_share_halves",
        out_shape=[jax.ShapeDtypeStruct(r.shape, r.dtype) for r in rs],
        in_specs=[_ANY] * n, out_specs=[_ANY] * n, input_output_aliases={a: a for a in range(n)},
        scratch_shapes=[pltpu.SemaphoreType.DMA((n,)), pltpu.SemaphoreType.DMA((n,))],
    )(*rs)


def ada_mod(c_all, ada_w):
    L, _, n = ada_w.shape

    def body(c_ref, w_ref, o_ref):
        o_ref[0] = _nn(_silu(c_ref[...]), w_ref[0], HI)

    return pl.pallas_call(
        body, name="ada_mod", grid=(L,),
        in_specs=[pl.BlockSpec((8, D), lambda l: (0, 0)), pl.BlockSpec((1, D, n), lambda l: (l, 0, 0))],
        out_specs=pl.BlockSpec((1, 8, n), lambda l: (l, 0, 0)),
        out_shape=jax.ShapeDtypeStruct((L, 8, n), f32),
        compiler_params=_params(("parallel",)),
    )(c_all, ada_w)


def ada_w_grad(c_all, dmod):
    L, _, n = dmod.shape

    def body(c_ref, d_ref, o_ref):
        o_ref[0] = _tn(_silu(c_ref[...]), d_ref[0], HI)

    return pl.pallas_call(
        body, name="ada_w_grad", grid=(L,),
        in_specs=[pl.BlockSpec((8, D), lambda l: (0, 0)), pl.BlockSpec((1, 8, n), lambda l: (l, 0, 0))],
        out_specs=pl.BlockSpec((1, D, n), lambda l: (l, 0, 0)),
        out_shape=jax.ShapeDtypeStruct((L, D, n), f32),
        compiler_params=_params(("parallel",)),
    )(c_all, dmod)


def adamw(w, g, m, v, name):
    shp = w.shape
    two = lambda a: a.reshape(-1, shp[-1])
    R, C = two(w).shape
    tr = _pick(R, (256, 128, 64, 32, 16, 8))
    bc1, bc2 = 1.0 - B1 ** STEP, 1.0 - B2 ** STEP

    def body(w_ref, g_ref, m_ref, v_ref, d_ref, mo_ref, vo_ref):
        gv = g_ref[...]
        mn = B1 * m_ref[...] + (1.0 - B1) * gv
        vn = B2 * v_ref[...] + (1.0 - B2) * (gv * gv)
        d_ref[...] = -LR * ((mn / bc1) / (jnp.sqrt(vn / bc2) + AEPS) + WD * w_ref[...])
        mo_ref[...] = mn
        vo_ref[...] = vn

    t = pl.BlockSpec((tr, C), lambda i: (i, 0))
    outs = pl.pallas_call(
        body, name=name, grid=(R // tr,),
        in_specs=[t] * 4, out_specs=[t] * 3, out_shape=[jax.ShapeDtypeStruct((R, C), f32)] * 3,
        compiler_params=_params(("parallel",)),
    )(two(w), two(g), two(m), two(v))
    return [o.reshape(shp) for o in outs]


def _pack(arrs):
    parts, offs, r0 = [], [], 0
    for a in arrs:
        n = a.size
        rows = -(-n // 1024) * 8
        parts.append(jnp.pad(a.reshape(-1), (0, rows * 128 - n)).reshape(rows, 128))
        offs.append((r0, rows))
        r0 += rows
    return jnp.concatenate(parts, axis=0), offs


def _unpack(buf, offs, shapes):
    out = []
    for (r0, rows), shp in zip(offs, shapes):
        n = 1
        for d in shp:
            n *= d
        out.append(buf[..., r0:r0 + rows, :].reshape(buf.shape[:-2] + (rows * 128,))[..., :n].reshape(buf.shape[:-2] + tuple(shp)))
    return out


def kernel(x, c, norm_w, ada_w, ada_b, a_w_in, a_conv_w, a_A_log, a_dt_bias, a_norm_w, a_w_out, b_w_in, b_f_bias, b_qn_w, b_kn_w, b_w_out, final_norm_w, loss_target, m_norm_w, m_ada_w, m_ada_b, m_a_w_in, m_a_conv_w, m_a_A_log, m_a_dt_bias, m_a_norm_w, m_a_w_out, m_b_w_in, m_b_f_bias, m_b_qn_w, m_b_kn_w, m_b_w_out, m_final_norm_w, v_norm_w, v_ada_w, v_ada_b, v_a_w_in, v_a_conv_w, v_a_A_log, v_a_dt_bias, v_a_norm_w, v_a_w_out, v_b_w_in, v_b_f_bias, v_b_qn_w, v_b_kn_w, v_b_w_out, v_final_norm_w):
    weights = dict(norm_w=norm_w, ada_w=ada_w, ada_b=ada_b, a_w_in=a_w_in, a_conv_w=a_conv_w, a_A_log=a_A_log,
                   a_dt_bias=a_dt_bias, a_norm_w=a_norm_w, a_w_out=a_w_out, b_w_in=b_w_in, b_f_bias=b_f_bias,
                   b_qn_w=b_qn_w, b_kn_w=b_kn_w, b_w_out=b_w_out, final_norm_w=final_norm_w)
    m_in = dict(norm_w=m_norm_w, ada_w=m_ada_w, ada_b=m_ada_b, a_w_in=m_a_w_in, a_conv_w=m_a_conv_w, a_A_log=m_a_A_log,
                a_dt_bias=m_a_dt_bias, a_norm_w=m_a_norm_w, a_w_out=m_a_w_out, b_w_in=m_b_w_in, b_f_bias=m_b_f_bias,
                b_qn_w=m_b_qn_w, b_kn_w=m_b_kn_w, b_w_out=m_b_w_out, final_norm_w=m_final_norm_w)
    v_in = dict(norm_w=v_norm_w, ada_w=v_ada_w, ada_b=v_ada_b, a_w_in=v_a_w_in, a_conv_w=v_a_conv_w, a_A_log=v_a_A_log,
                a_dt_bias=v_a_dt_bias, a_norm_w=v_a_norm_w, a_w_out=v_a_w_out, b_w_in=v_b_w_in, b_f_bias=v_b_f_bias,
                b_qn_w=v_b_qn_w, b_kn_w=v_b_kn_w, b_w_out=v_b_w_out, final_norm_w=v_final_norm_w)
    xi, yi, ci = _me()
    me_b, me_k = 4 * xi + 2 * yi + ci, 2 * xi + yi
    cidx = ci.astype(jnp.int32).reshape(1)
    S = x.shape[1]
    depth, n_a, n_b = norm_w.shape[0], a_w_in.shape[0], b_w_in.shape[0]
    x0, tgt = x.reshape(S, D), loss_target.reshape(S, D)

    c_all = ag_small(jnp.pad(c, ((0, 7), (0, 0)))).reshape(8, 8, D)[:, 0]
    nloc = ada_w.shape[2]
    parts = ag_small(ada_mod(c_all, ada_w).reshape(depth * 8, nloc)).reshape(4, 2, depth, 8, nloc)[:, 0]
    mine = lax.dynamic_index_in_dim(parts, me_b, axis=2, keepdims=False)
    mod = jnp.transpose(mine, (1, 0, 2)).reshape(depth, 4 * nloc) + ada_b
    shift, scale, gate = (mod[:, k * D:(k + 1) * D] for k in range(3))

    w_loc = [(a_w_in[i // 2] if i % 2 == 0 else b_w_in[i // 2]).astype(bf16) for i in range(depth)]
    wo_loc = [(a_w_out[i // 2] if i % 2 == 0 else b_w_out[i // 2]).astype(bf16) for i in range(depth)]
    pad_in = [(G_INP - G_IN) if i % 2 == 0 else (F_INP - F_IN) for i in range(depth)]
    halves = lambda w: w.reshape((2, w.shape[0] // 2) + w.shape[1:])

    def cols_in_place(g_in, pad):
        w = jnp.transpose(g_in, (1, 0, 2)).reshape(g_in.shape[1], -1)
        return jnp.pad(w, ((0, 0), (0, pad)))

    g_in0, g_conv = ag_chips([halves(w_loc[0]), a_conv_w])
    w_in_full = [cols_in_place(g_in0.reshape((4,) + w_loc[0].shape), pad_in[0])]
    w_out_full = []
    conv = [jnp.transpose(g_conv[:, l], (1, 0, 2)).reshape(CONV_K, -1) for l in range(n_a)]
    qw2 = [_row(jnp.tile(b_qn_w[l], 2)) for l in range(n_b)]
    kw2 = [_row(jnp.tile(b_kn_w[l], 2)) for l in range(n_b)]

    saved, xc = [], x0
    for i in range(depth):
        l = i // 2
        nxt = _ag_comm([w_loc[i + 1], wo_loc[i + 1]]) if i + 1 < depth else None
        h = ln_mod(xc, _row(norm_w[i]), _row(scale[i]), _row(shift[i]))
        name = "mm_a_in" if i % 2 == 0 else "mm_b_in"
        if i == 0:
            proj, got = matmul(h, w_in_full[0], "nn", name, comm=_ag_comm([wo_loc[0]]))
            w_out_full.append(got[0].reshape(-1, D))
        else:
            proj = matmul(h, w_in_full[i], "nn", name)
        if i % 2 == 0:
            pre = gdn_pre(proj, conv[l], _row(a_A_log[l]), _row(a_dt_bias[l]))
            res, got = gdn_fwd(*pre, comm=nxt)
            o2 = gdn_onorm(res[0], proj, _row(a_norm_w[l]))
            y, xn = out_proj(o2, w_out_full[i], xc, _row(gate[i]), "out_proj_a")
        else:
            pre = fox_pre(proj, _row(b_f_bias[l]), qw2[l], kw2[l])
            res, got = fox_attn(*pre, comm=nxt)
            o2 = fox_gate(res[0], proj)
            y, xn = out_proj(o2, w_out_full[i], xc, _row(gate[i]), "out_proj_b")
        saved.append((xc, h, proj, o2, y, pre, res))
        if nxt is not None:
            w_in_full.append(cols_in_place(got[0], pad_in[i + 1]))
            w_out_full.append(got[1].reshape(-1, D))
        xc = xn
    dx, st_f = final_loss(xc, _row(final_norm_w), tgt)

    d_norm, d_mod = [None] * depth, [None] * depth
    d_conv, d_alog, d_dtb, d_anw = [None] * n_a, [None] * n_a, [None] * n_a, [None] * n_a
    d_fb, d_qn, d_kn = [None] * n_b, [None] * n_b, [None] * n_b
    ex_in, ex_out, pend_in = [None] * depth, [None] * depth, None
    for i in reversed(range(depth)):
        l = i // 2
        xin, h, proj, o2, y, pre, res = saved[i]
        ab = "a" if i % 2 == 0 else "b"
        dy, st_g = gate_bwd(dx, y, _row(gate[i]))
        do2 = matmul(dy, w_out_full[i], "nt", f"mm_{ab}_do2")
        d_out = matmul(o2, dy, "tn", f"mm_{ab}_dwo", out_dtype=bf16)
        ride = _rs_comm(([] if pend_in is None else [pend_in]) + [d_out.reshape(4, d_out.shape[0] // 4, D)])
        if i % 2 == 0:
            o, wv, at, tinv, vn, st = res
            do, dz, st_o = gdn_onorm_bwd(do2, o, proj, _row(a_norm_w[l]))
            grads, got = gdn_bwd(do, *pre, wv, at, tinv, vn, st, comm=ride)
            dcv, dba, st_s = gdn_pre_bwd(proj, conv[l], _row(a_A_log[l]), _row(a_dt_bias[l]), *grads)
            dproj, dcw = gdn_conv_bwd(proj, conv[l], dcv, dz, dba)
            d_conv[l], d_alog[l], d_dtb[l], d_anw[l] = dcw[:CONV_K], st_s[0], st_s[1], st_o[0]
        else:
            o, lse = res
            do, dz, delta = fox_gate_bwd(do2, o, proj)
            (dqa, dka, dv), got = fox_attn_bwd(*pre, do, lse, delta, comm=ride)
            dproj, st_b = fox_pre_bwd(proj, _row(b_f_bias[l]), qw2[l], kw2[l], dqa, dka, dv, dz)
            d_fb[l], d_qn[l], d_kn[l] = st_b[2, :F_H], st_b[0, :F_HD] + st_b[0, F_HD:], st_b[1, :F_HD] + st_b[1, F_HD:]
        ex_out[i] = got[-1]
        if pend_in is not None:
            ex_in[i + 1] = got[0]
        d_in = matmul(h, dproj, "tn", f"mm_{ab}_dw", out_dtype=bf16)
        cl = w_loc[i].shape[1]
        pend_in = jnp.transpose(d_in[:, :4 * cl].reshape(d_in.shape[0], 4, cl), (1, 0, 2))
        if i == 0:
            dh, got = matmul(dproj, w_in_full[i], "nt", f"mm_{ab}_dh", comm=_rs_comm([pend_in]))
            ex_in[0] = got[0]
        else:
            dh = matmul(dproj, w_in_full[i], "nt", f"mm_{ab}_dh")
        dx, st_n = ln_mod_bwd(xin, _row(norm_w[i]), _row(scale[i]), dh, dx)
        d_norm[i] = st_n[0]
        d_mod[i] = jnp.concatenate([st_n[2], st_n[1], st_g[0]])

    small = [jnp.stack(d_norm), jnp.stack(d_mod), jnp.stack(d_conv), jnp.stack(d_alog), jnp.stack(d_dtb), jnp.stack(d_anw),
             jnp.stack(d_fb), jnp.stack(d_qn), jnp.stack(d_kn), st_f[0], jnp.sum(st_f[1]).reshape(1)]
    shapes = [a.shape for a in small]
    buf, offs = _pack(small)
    gathered = ag_small(buf).reshape(8, buf.shape[0], 128)
    tot = _unpack(sum_leading(gathered, "sum_devices"), offs, shapes)
    g_norm, g_adab, g_convf, g_alog, g_dtb, g_anw, g_fb, g_qn, g_kn, g_fin, loss = tot
    dmod_all = _unpack(gathered, offs[1:2], shapes[1:2])[0]
    dmod_loc = lax.dynamic_slice_in_dim(dmod_all, me_k * nloc, nloc, axis=2)
    g_adaw = ada_w_grad(c_all, jnp.transpose(dmod_loc, (1, 0, 2)))
    g_conv_loc = lax.dynamic_slice_in_dim(g_convf, me_k * a_conv_w.shape[2], a_conv_w.shape[2], axis=2)

    bufs = {}
    for i in range(depth):
        for which, q in (("in", ex_in[i]), ("out", ex_out[i])):
            key = ("a" if i % 2 == 0 else "b", which)
            bufs[key] = rs_sum_devices(q, cidx, i // 2, depth // 2, into=bufs.get(key))
    keys = list(bufs)
    done = dict(zip(keys, rs_share_halves([bufs[k] for k in keys])))
    grads = dict(norm_w=g_norm, ada_w=g_adaw, ada_b=g_adab, a_w_in=done["a", "in"].reshape(a_w_in.shape),
                 a_conv_w=g_conv_loc, a_A_log=g_alog, a_dt_bias=g_dtb, a_norm_w=g_anw,
                 a_w_out=done["a", "out"].reshape(a_w_out.shape), b_w_in=done["b", "in"].reshape(b_w_in.shape),
                 b_f_bias=g_fb, b_qn_w=g_qn, b_kn_w=g_kn, b_w_out=done["b", "out"].reshape(b_w_out.shape),
                 final_norm_w=g_fin)
    names = list(weights)
    upd = {n: adamw(weights[n], grads[n], m_in[n], v_in[n], "adamw_" + n) for n in names}
    return (loss.reshape(()), dx.reshape(x.shape), *[grads[n] for n in names], *[upd[n][0] for n in names],
            *[upd[n][1] for n in names], *[upd[n][2] for n in names])
```

```python
import functools

import jax
import jax.numpy as jnp
from jax import lax
from jax.experimental import pallas as pl
from jax.experimental.pallas import tpu as pltpu

f32, bf16 = jnp.float32, jnp.bfloat16
HI = lax.Precision.HIGHEST
MESH = pl.DeviceIdType.MESH

EPS = 1e-6
D = 1024
CHUNK = 64
GQK_H, GV_H, GHD = 8, 16, 128
G_CONV = 4096
G_Z0 = 4096
G_BA0 = 6144
G_IN, G_INP = 6176, 6272
CONV_K = 4
F_H, F_HD = 16, 64
F_W = 1024
F_F0 = 4096
F_IN, F_INP = 4112, 4224
LR, B1, B2, AEPS, WD, STEP = 0.001, 0.9, 0.999, 1e-08, 0.01, 10
NEG = -1e30
VMEM_LIMIT = 56 * 1024 * 1024


def _nn(a, b, prec=None):
    return lax.dot_general(a, b, (((1,), (0,)), ((), ())), preferred_element_type=f32, precision=prec)


def _nt(a, b, prec=None):
    return lax.dot_general(a, b, (((1,), (1,)), ((), ())), preferred_element_type=f32, precision=prec)


def _tn(a, b, prec=None):
    return lax.dot_general(a, b, (((0,), (0,)), ((), ())), preferred_element_type=f32, precision=prec)


def _iota(shape, axis):
    return lax.broadcasted_iota(jnp.int32, shape, axis)


def _sigmoid(x):
    return 0.5 * jnp.tanh(0.5 * x) + 0.5


def _softplus(x):
    return jnp.maximum(x, 0.0) + jnp.log(1.0 + jnp.exp(-jnp.abs(x)))


def _silu(x):
    return x * _sigmoid(x)


def _dsilu(x):
    s = _sigmoid(x)
    return s * (1.0 + x * (1.0 - s))


def _params(sem=None, vmem=VMEM_LIMIT):
    return pltpu.CompilerParams(dimension_semantics=sem, vmem_limit_bytes=vmem)


def _row(v):
    return v.reshape(1, -1)


class _Comm:
    def __init__(self, ins, out_shapes, sems, start, wait):
        self.ins, self.out_shapes, self.sems, self.start, self.wait = list(ins), list(out_shapes), list(sems), start, wait


def _call(body, *, name, grid, in_specs, out_specs, out_shape, scratch_shapes, sem, args, comm=None, prefetch=()):
    n_pf, n_in, n_out, n_s = len(prefetch), len(in_specs), len(out_specs), len(scratch_shapes)
    n_ci, n_co = (len(comm.ins), len(comm.out_shapes)) if comm is not None else (0, 0)

    def wrapped(*refs):
        pf, refs = refs[:n_pf], refs[n_pf:]
        core_in, c_in = refs[:n_in], refs[n_in:n_in + n_ci]
        o0 = n_in + n_ci
        core_out, c_out = refs[o0:o0 + n_out], refs[o0 + n_out:o0 + n_out + n_co]
        s0 = o0 + n_out + n_co
        core_s, c_sem = refs[s0:s0 + n_s], refs[s0 + n_s:]
        if comm is not None:
            first = functools.reduce(jnp.logical_and, [pl.program_id(d) == 0 for d in range(len(grid))])
            pl.when(first)(functools.partial(comm.start, c_in, c_out, c_sem))
        body(*pf, *core_in, *core_out, *core_s)
        if comm is not None:
            last = functools.reduce(jnp.logical_and, [pl.program_id(d) == grid[d] - 1 for d in range(len(grid))])
            pl.when(last)(functools.partial(comm.wait, c_in, c_out, c_sem))

    extra = ([], [], [], []) if comm is None else ([_ANY] * n_ci, [_ANY] * n_co, comm.out_shapes, comm.sems)
    spec = pltpu.PrefetchScalarGridSpec(
        num_scalar_prefetch=n_pf, grid=grid, in_specs=list(in_specs) + extra[0], out_specs=list(out_specs) + extra[1],
        scratch_shapes=list(scratch_shapes) + extra[3])
    outs = pl.pallas_call(
        wrapped, name=name if comm is None else name + "_x", grid_spec=spec, out_shape=list(out_shape) + extra[2],
        compiler_params=_params(sem if comm is None else ("arbitrary",) * len(grid)),
    )(*prefetch, *args, *(comm.ins if comm is not None else []))
    return outs[:n_out], outs[n_out:]


def _pick(n, pref):
    for t in pref:
        if n % t == 0:
            return t
    return n


MM_VMEM_BUDGET = 44 * 1024 * 1024


def _mm_tiles(M, N, K):
    best = None
    for tk in [K] + [t for t in (2048, 1408, 1024, 896, 512, 384, 256, 128) if K % t == 0 and t < K]:
        for tm in (2048, 1024, 512, 256, 128):
            for tn in (1408, 1024, 896, 512, 384, 256, 128):
                if M % tm or N % tn:
                    continue
                nk = K // tk
                need = 2 * 2 * (tm * tk + tk * tn) + 2 * 4 * tm * tn + (4 * tm * tn if nk > 1 else 0)
                if need <= MM_VMEM_BUDGET:
                    cand = ((nk, -tm * tn), (tm, tn, tk))
                    best = cand if best is None or cand[0] < best[0] else best
    return best[1]


def matmul(a, b, mode, name, out_dtype=f32, comm=None):
    if mode == "nn":
        (M, K), (_, N) = a.shape, b.shape
    elif mode == "nt":
        (M, K), (N, _) = a.shape, b.shape
    else:
        (K, M), (_, N) = a.shape, b.shape
    tm, tn, tk = _mm_tiles(M, N, K)
    nk = K // tk
    dot = {"nn": _nn, "nt": _nt, "tn": _tn}[mode]

    def body(a_ref, b_ref, o_ref, *acc):
        k = pl.program_id(2)
        part = dot(a_ref[...], b_ref[...])
        if nk == 1:
            o_ref[...] = part.astype(out_dtype)
        else:
            acc_ref = acc[0]

            @pl.when(k == 0)
            def _():
                acc_ref[...] = part

            @pl.when(k > 0)
            def _():
                acc_ref[...] += part

            @pl.when(k == nk - 1)
            def _():
                o_ref[...] = acc_ref[...].astype(out_dtype)

    a_spec = pl.BlockSpec((tk, tm), lambda i, j, k: (k, i)) if mode == "tn" else pl.BlockSpec((tm, tk), lambda i, j, k: (i, k))
    b_spec = pl.BlockSpec((tn, tk), lambda i, j, k: (j, k)) if mode == "nt" else pl.BlockSpec((tk, tn), lambda i, j, k: (k, j))
    outs, got = _call(
        body, name=name, grid=(M // tm, N // tn, nk),
        in_specs=[a_spec, b_spec], out_specs=[pl.BlockSpec((tm, tn), lambda i, j, k: (i, j))],
        out_shape=[jax.ShapeDtypeStruct((M, N), out_dtype)],
        scratch_shapes=[] if nk == 1 else [pltpu.VMEM((tm, tn), f32)],
        sem=("parallel", "parallel", "arbitrary"), args=(a, b), comm=comm)
    return outs[0] if comm is None else (outs[0], got)


def out_proj(o2, w, x, gate, name):
    S, K = o2.shape
    N = w.shape[1]
    tm, tn = 1024, 1024

    def body(a_ref, b_ref, x_ref, g_ref, y_ref, xn_ref):
        y = _nn(a_ref[...], b_ref[...])
        y_ref[...] = y
        xn_ref[...] = x_ref[...] + g_ref[...] * y

    return pl.pallas_call(
        body, name=name, grid=(S // tm, N // tn),
        in_specs=[pl.BlockSpec((tm, K), lambda i, j: (i, 0)), pl.BlockSpec((K, tn), lambda i, j: (0, j)),
                  pl.BlockSpec((tm, tn), lambda i, j: (i, j)), pl.BlockSpec((1, tn), lambda i, j: (0, j))],
        out_specs=[pl.BlockSpec((tm, tn), lambda i, j: (i, j))] * 2,
        out_shape=[jax.ShapeDtypeStruct((S, N), f32)] * 2,
        compiler_params=_params(("parallel", "parallel")),
    )(o2, w, x, gate)


def ln_mod(x, nw, scale, shift):
    S = x.shape[0]
    tm = 512

    def body(x_ref, nw_ref, sc_ref, sh_ref, h_ref):
        xv = x_ref[...]
        r = lax.rsqrt(jnp.mean(xv * xv, axis=-1, keepdims=True) + EPS)
        h_ref[...] = ((xv * r) * nw_ref[...] * (1.0 + sc_ref[...]) + sh_ref[...]).astype(bf16)

    vec = pl.BlockSpec((1, D), lambda i: (0, 0))
    return pl.pallas_call(
        body, name="ln_mod", grid=(S // tm,),
        in_specs=[pl.BlockSpec((tm, D), lambda i: (i, 0)), vec, vec, vec],
        out_specs=pl.BlockSpec((tm, D), lambda i: (i, 0)),
        out_shape=jax.ShapeDtypeStruct((S, D), bf16),
        compiler_params=_params(("parallel",)),
    )(x, nw, scale, shift)


def ln_mod_bwd(x, nw, scale, dh, dxres):
    S = x.shape[0]
    tm = 512
    nb = S // tm

    def body(x_ref, nw_ref, sc_ref, dh_ref, dr_ref, dx_ref, st_ref):
        i = pl.program_id(0)
        xv = x_ref[...]
        r = lax.rsqrt(jnp.mean(xv * xv, axis=-1, keepdims=True) + EPS)
        xn = xv * r
        dh = dh_ref[...]
        dxn = dh * (nw_ref[...] * (1.0 + sc_ref[...]))
        dx_ref[...] = dr_ref[...] + r * (dxn - xn * jnp.mean(dxn * xn, axis=-1, keepdims=True))
        p1 = jnp.sum(dh * xn, axis=0, keepdims=True)
        p2 = jnp.sum(dh, axis=0, keepdims=True)
        upd = jnp.concatenate([p1, p1, p2, jnp.zeros((5, D), f32)], axis=0)

        @pl.when(i == 0)
        def _():
            st_ref[...] = upd

        @pl.when(i > 0)
        def _():
            st_ref[...] += upd

        @pl.when(i == nb - 1)
        def _():
            st_ref[0:1, :] = st_ref[0:1, :] * (1.0 + sc_ref[...])
            st_ref[1:2, :] = st_ref[1:2, :] * nw_ref[...]

    vec = pl.BlockSpec((1, D), lambda i: (0, 0))
    tile = pl.BlockSpec((tm, D), lambda i: (i, 0))
    return pl.pallas_call(
        body, name="ln_mod_bwd", grid=(S // tm,),
        in_specs=[tile, vec, vec, tile, tile],
        out_specs=[tile, pl.BlockSpec((8, D), lambda i: (0, 0))],
        out_shape=[jax.ShapeDtypeStruct((S, D), f32), jax.ShapeDtypeStruct((8, D), f32)],
        compiler_params=_params(("arbitrary",)),
    )(x, nw, scale, dh, dxres)


def final_loss(x, fw, tgt):
    S = x.shape[0]
    tm = 512

    def body(x_ref, w_ref, t_ref, dx_ref, st_ref):
        i = pl.program_id(0)
        xv = x_ref[...]
        r = lax.rsqrt(jnp.mean(xv * xv, axis=-1, keepdims=True) + EPS)
        xn = xv * r
        err = xn * w_ref[...] - t_ref[...]
        dy = err * (1.0 / D)
        dxn = dy * w_ref[...]
        dx_ref[...] = r * (dxn - xn * jnp.mean(dxn * xn, axis=-1, keepdims=True))
        p1 = jnp.sum(dy * xn, axis=0, keepdims=True)
        p2 = jnp.sum(err * err, axis=0, keepdims=True) * (0.5 / D)
        upd = jnp.concatenate([p1, p2, jnp.zeros((6, D), f32)], axis=0)

        @pl.when(i == 0)
        def _():
            st_ref[...] = upd

        @pl.when(i > 0)
        def _():
            st_ref[...] += upd

    tile = pl.BlockSpec((tm, D), lambda i: (i, 0))
    return pl.pallas_call(
        body, name="final_loss", grid=(S // tm,),
        in_specs=[tile, pl.BlockSpec((1, D), lambda i: (0, 0)), tile],
        out_specs=[tile, pl.BlockSpec((8, D), lambda i: (0, 0))],
        out_shape=[jax.ShapeDtypeStruct((S, D), f32), jax.ShapeDtypeStruct((8, D), f32)],
        compiler_params=_params(("arbitrary",)),
    )(x, fw, tgt)


def gate_bwd(dx, y, gate):
    S = dx.shape[0]
    tm = 512

    def body(dx_ref, y_ref, g_ref, dy_ref, st_ref):
        i = pl.program_id(0)
        dxv = dx_ref[...]
        dy_ref[...] = (g_ref[...] * dxv).astype(bf16)
        upd = jnp.concatenate([jnp.sum(dxv * y_ref[...], axis=0, keepdims=True), jnp.zeros((7, D), f32)], axis=0)

        @pl.when(i == 0)
        def _():
            st_ref[...] = upd

        @pl.when(i > 0)
        def _():
            st_ref[...] += upd

    tile = pl.BlockSpec((tm, D), lambda i: (i, 0))
    return pl.pallas_call(
        body, name="gate_bwd", grid=(S // tm,),
        in_specs=[tile, tile, pl.BlockSpec((1, D), lambda i: (0, 0))],
        out_specs=[tile, pl.BlockSpec((8, D), lambda i: (0, 0))],
        out_shape=[jax.ShapeDtypeStruct((S, D), bf16), jax.ShapeDtypeStruct((8, D), f32)],
        compiler_params=_params(("arbitrary",)),
    )(dx, y, gate)


def _chunk_mats(tm):
    r, c = _iota((tm, tm), 0), _iota((tm, tm), 1)
    same = jnp.right_shift(r, 6) == jnp.right_shift(c, 6)
    ltri = jnp.where(same & (c <= r), 1.0, 0.0).astype(f32)
    utri = jnp.where(same & (c >= r), 1.0, 0.0).astype(f32)
    bsame = jnp.where(same, 1.0, 0.0).astype(f32)
    return ltri, utri, bsame


def _gdn_scalars(ba, alog, dtb, ltri, bsame):
    beta = _sigmoid(ba[:, 0:16])
    u = ba[:, 16:32] + dtb
    neg_a = -jnp.exp(alog)
    g = neg_a * _softplus(u)
    gc = _nn(ltri, g, HI)
    glast = _nn(bsame, g, HI)
    return beta, u, neg_a, g, gc, glast


def _conv_taps(p_ref, halo_ref, first, gi, ext_scr):
    cs = slice(gi * 128, (gi + 1) * 128)
    tm = p_ref.shape[0]
    cur = p_ref[:, cs]
    ext_scr[gi, 0:8, :] = jnp.where(first, 0.0, halo_ref[:, cs])
    ext_scr[gi, 8:, :] = cur
    return [cur] + [ext_scr[gi, 8 - s:8 - s + tm, :] for s in range(1, CONV_K)]


def _conv_scratch(tm):
    return pltpu.VMEM((G_CONV // 128, tm + 8, 128), f32)


def _conv(taps, w):
    cv = taps[0] * w[3:4]
    for s in range(1, CONV_K):
        cv = cv + taps[s] * w[3 - s:4 - s]
    return cv


def _l2n(x):
    return x * lax.rsqrt(jnp.sum(x * x, axis=-1, keepdims=True) + EPS)


def _gdn_in_specs(tm, S):
    nb8 = tm // 8
    return [pl.BlockSpec((tm, G_CONV), lambda i: (i, 0)),
            pl.BlockSpec((8, G_CONV), lambda i: (jnp.maximum(i * nb8 - 1, 0), 0)),
            pl.BlockSpec((tm, 128), lambda i: (i, G_BA0 // 128))]


def gdn_pre(proj, conv_w, alog, dtb):
    S = proj.shape[0]
    tm = 256
    nch = tm // CHUNK

    def body(p_ref, halo_ref, ba_ref, w_ref, al_ref, dt_ref,
             q_ref, k_ref, kb_ref, kbg_ref, vb_ref, qd_ref, kd_ref, d_ref, gl_ref, ext_scr):
        first = pl.program_id(0) == 0
        ltri, _, bsame = _chunk_mats(tm)
        beta, _, _, _, gc, glast = _gdn_scalars(ba_ref[...], al_ref[...], dt_ref[...], ltri, bsame)
        eg, ek, egl = jnp.exp(gc), jnp.exp(glast - gc), jnp.exp(glast)
        eye = jnp.where(_iota((16, 16), 0) == _iota((16, 16), 1), 1.0, 0.0).astype(f32)
        gct = _nt(eye, gc, HI)
        low = _iota((CHUNK, CHUNK), 0) >= _iota((CHUNK, CHUNK), 1)

        def act(gi):
            return _silu(_conv(_conv_taps(p_ref, halo_ref, first, gi, ext_scr), w_ref[:, gi * 128:(gi + 1) * 128]))

        for j in range(GQK_H):
            js = slice(j * 128, (j + 1) * 128)
            qn = _l2n(act(j)) * (GHD ** -0.5)
            kn = _l2n(act(GQK_H + j))
            q_ref[:, js] = qn.astype(bf16)
            k_ref[:, js] = kn.astype(bf16)
            for e in range(2):
                h = 2 * j + e
                hs = slice(h * 128, (h + 1) * 128)
                v = act(2 * GQK_H + h)
                bh, egh, ekh = beta[:, h:h + 1], eg[:, h:h + 1], ek[:, h:h + 1]
                kbv = kn * bh
                kb_ref[:, hs] = kbv.astype(bf16)
                kbg_ref[:, hs] = (kbv * egh).astype(bf16)
                vb_ref[:, hs] = (v * bh).astype(bf16)
                qd_ref[:, hs] = (qn * egh).astype(bf16)
                kd_ref[:, hs] = (kn * ekh).astype(bf16)
                for c in range(nch):
                    rs = slice(c * CHUNK, (c + 1) * CHUNK)
                    diff = gc[rs, h:h + 1] - gct[h:h + 1, rs]
                    d_ref[rs, h * CHUNK:(h + 1) * CHUNK] = jnp.where(low, jnp.exp(jnp.where(low, diff, 0.0)), 0.0)
                    gl_ref[c * 8:(c + 1) * 8, hs] = jnp.broadcast_to(egl[c * CHUNK:c * CHUNK + 8, h:h + 1], (8, 128))

    full = lambda shape: pl.BlockSpec(shape, lambda i: (0, 0))
    t1 = pl.BlockSpec((tm, 1024), lambda i: (i, 0))
    t2 = pl.BlockSpec((tm, 2048), lambda i: (i, 0))
    sd = jax.ShapeDtypeStruct
    return pl.pallas_call(
        body, name="gdn_pre", grid=(S // tm,),
        in_specs=_gdn_in_specs(tm, S) + [full((CONV_K, G_CONV)), full((1, 16)), full((1, 16))],
        out_specs=[t1, t1, t2, t2, t2, t2, t2, t1, pl.BlockSpec((tm // 8, 2048), lambda i: (i, 0))],
        out_shape=[sd((S, 1024), bf16)] * 2 + [sd((S, 2048), bf16)] * 5 + [sd((S, 1024), f32), sd((S // 8, 2048), f32)],
        scratch_shapes=[_conv_scratch(tm)],
        compiler_params=_params(("parallel",)),
    )(proj, proj, proj, conv_w, alog, dtb)


def _bnn(a, b):
    return lax.dot_general(a, b, (((2,), (1,)), ((0,), (0,))), preferred_element_type=f32)


def _bnt(a, b):
    return lax.dot_general(a, b, (((2,), (2,)), ((0,), (0,))), preferred_element_type=f32)


def _btn(a, b):
    return lax.dot_general(a, b, (((1,), (1,)), ((0,), (0,))), preferred_element_type=f32)


def _split(a):
    hi = a.astype(bf16)
    return hi, (a - hi.astype(f32)).astype(bf16)


def _cat3(h, l, axis, lhs):
    return jnp.concatenate([h, h, l] if lhs else [h, l, h], axis=axis)


def _tri_inv_b(L):
    eye = jnp.where(_iota((1, CHUNK, CHUNK), 1) == _iota((1, CHUNK, CHUNK), 2), 1.0, 0.0).astype(f32)
    P = -L
    T = eye + P
    ph, pl_ = _split(P)
    for _ in range(5):
        P = _bnn(_cat3(ph, pl_, 2, True), _cat3(ph, pl_, 1, False))
        ph, pl_ = _split(P)
        th, tl = _split(T)
        T = T + _bnn(_cat3(th, tl, 2, True), _cat3(ph, pl_, 1, False))
    return T


GTB = 512
GQH_FWD, GQH_BWD = 1, 2


def _gdn_slices(ncb, gnv):
    pairs = [(c, e) for c in range(ncb) for e in range(gnv)]
    rs = lambda c: slice(c * CHUNK, (c + 1) * CHUNK)
    cs = lambda e: slice(e * 128, (e + 1) * 128)
    ds_ = lambda e: slice(e * CHUNK, (e + 1) * CHUNK)
    ks = lambda e: slice((e // 2) * 128, (e // 2 + 1) * 128)
    return pairs, rs, cs, ds_, ks


def gdn_fwd(q, k, kb, kbg, vb, qd, kd, dm, gl8, comm=None):
    S = q.shape[0]
    nb, ncb = S // GTB, GTB // CHUNK
    GQH, GNV = GQH_FWD, 2 * GQH_FWD
    pairs, rs, cs, ds_, ks = _gdn_slices(ncb, GNV)

    def body(q_ref, k_ref, kb_ref, kbg_ref, vb_ref, qd_ref, kd_ref, d_ref, gl_ref,
             o_ref, w_ref, at_ref, t_ref, vn_ref, st_ref, state, u_scr):
        @pl.when(pl.program_id(1) == 0)
        def _():
            state[...] = jnp.zeros_like(state)

        stk = lambda ref, lanes: jnp.stack([ref[rs(c), lanes(e)] for c, e in pairs])
        kq = stk(k_ref, ks)
        dmat = stk(d_ref, ds_)
        strict = _iota((1, CHUNK, CHUNK), 1) > _iota((1, CHUNK, CHUNK), 2)
        T = _tri_inv_b(jnp.where(strict, _bnt(stk(kb_ref, cs), kq) * dmat, 0.0))
        tb = T.astype(bf16)
        u_scr[...] = _bnn(tb, stk(vb_ref, cs))
        wb = _bnn(tb, stk(kbg_ref, cs)).astype(bf16)
        per_qk = lambda ref: jnp.stack([ref[rs(c), ks(e)] for c, e in pairs if e % 2 == 0])
        qk = _bnt(per_qk(q_ref), per_qk(k_ref))
        for b, (c, e) in enumerate(pairs):
            w_ref[rs(c), cs(e)] = wb[b]
            at_ref[rs(c), ds_(e)] = (qk[b // 2] * dmat[b]).astype(bf16)
            t_ref[rs(c), ds_(e)] = T[b]
        for b, (c, e) in enumerate(pairs):
            sb = state[e].astype(bf16)
            vnb = (u_scr[b] - _nn(w_ref[rs(c), cs(e)], sb)).astype(bf16)
            o_ref[rs(c), cs(e)] = _nn(qd_ref[rs(c), cs(e)], sb) + _nn(at_ref[rs(c), ds_(e)], vnb)
            st_ref[c * 128:(c + 1) * 128, cs(e)] = sb
            state[e] = state[e] * gl_ref[c * 8:c * 8 + 1, cs(e)] + _tn(kd_ref[rs(c), cs(e)], vnb)
            vn_ref[rs(c), cs(e)] = vnb

    b1 = pl.BlockSpec((GTB, 128 * GQH), lambda j, i: (i, j))
    b2 = pl.BlockSpec((GTB, 256 * GQH), lambda j, i: (i, j))
    sd = jax.ShapeDtypeStruct
    return _call(
        body, name="gdn_fwd", grid=(GQK_H // GQH, nb),
        in_specs=[b1, b1, b2, b2, b2, b2, b2, b1, pl.BlockSpec((GTB // 8, 256 * GQH), lambda j, i: (i, j))],
        out_specs=[b2, b2, b1, b1, b2, pl.BlockSpec((ncb * 128, 256 * GQH), lambda j, i: (i, j))],
        out_shape=[sd((S, 2048), f32), sd((S, 2048), bf16), sd((S, 1024), bf16), sd((S, 1024), f32),
                   sd((S, 2048), bf16), sd((S // CHUNK * 128, 2048), bf16)],
        scratch_shapes=[pltpu.VMEM((GNV, 128, 128), f32), pltpu.VMEM((GNV * ncb, CHUNK, 128), f32)],
        sem=("parallel", "arbitrary"), args=(q, k, kb, kbg, vb, qd, kd, dm, gl8), comm=comm)


def gdn_bwd(do, q, k, kb, kbg, vb, qd, kd, dm, gl8, w, at, T, vn, st, comm=None):
    S = q.shape[0]
    nb, ncb = S // GTB, GTB // CHUNK
    GQH, GNV = GQH_BWD, 2 * GQH_BWD
    pairs, rs, cs, ds_, ks = _gdn_slices(ncb, GNV)

    def body(do_ref, q_ref, k_ref, kb_ref, kbg_ref, vb_ref, qd_ref, kd_ref, d_ref, gl_ref, w_ref, at_ref, t_ref, vn_ref, st_ref,
             dq_ref, dk_ref, dkb_ref, dkbg_ref, dvb_ref, dqd_ref, dkd_ref, dgc_ref, dstate, dvn_scr, dw_scr, dat_scr, dgl_scr):
        @pl.when(pl.program_id(1) == 0)
        def _():
            dstate[...] = jnp.zeros_like(dstate)

        for b, (c, e) in reversed(list(enumerate(pairs))):
            dob = do_ref[rs(c), cs(e)].astype(bf16)
            sb = st_ref[c * 128:(c + 1) * 128, cs(e)]
            vnb = vn_ref[rs(c), cs(e)]
            gl = gl_ref[c * 8:c * 8 + 1, cs(e)]
            dS = dstate[e]
            dsb = dS.astype(bf16)
            dvnb = (_tn(at_ref[rs(c), ds_(e)], dob) + _nn(kd_ref[rs(c), cs(e)], dsb)).astype(bf16)
            dvn_scr[b] = dvnb
            dat_scr[b] = _nt(dob, vnb)
            dqd_ref[rs(c), cs(e)] = _nt(dob, sb)
            dkd_ref[rs(c), cs(e)] = _nt(vnb, dsb)
            dw_scr[b] = (-_nt(dvnb, sb)).astype(bf16)
            dgl = jnp.sum(jnp.sum(dS * sb.astype(f32), axis=1, keepdims=True), axis=0, keepdims=True)
            dgl_scr[b] = jnp.broadcast_to(dgl * gl, (8, 128))
            dstate[e] = gl * dS + _tn(qd_ref[rs(c), cs(e)], dob) - _tn(w_ref[rs(c), cs(e)], dvnb)

        stk = lambda ref, lanes: jnp.stack([ref[rs(c), lanes(e)] for c, e in pairs])
        kq, qq = stk(k_ref, ks), stk(q_ref, ks)
        kbb = stk(kb_ref, cs)
        Tm = stk(t_ref, ds_)
        tb = Tm.astype(bf16)
        dvn, dw = dvn_scr[...], dw_scr[...]
        dT = _bnt(dvn, stk(vb_ref, cs)) + _bnt(dw, stk(kbg_ref, cs))
        dvb, dkbg = _btn(tb, dvn), _btn(tb, dw)
        th, tl = _split(Tm)
        xh, xl = _split(_bnt(_cat3(*_split(dT), 2, True), _cat3(th, tl, 2, False)))
        dL = -_btn(_cat3(th, tl, 1, True), _cat3(xh, xl, 1, False))
        dmat = stk(d_ref, ds_)
        strict = _iota((1, CHUNK, CHUNK), 1) > _iota((1, CHUNK, CHUNK), 2)
        dA = jnp.where(strict, dL * dmat, 0.0)
        dB = dat_scr[...] * dmat
        dAb, dBb = dA.astype(bf16), dB.astype(bf16)
        dkb = _bnn(dAb, kq)
        dkc = _btn(dAb, kbb) + _btn(dBb, qq)
        dqc = _bnn(dBb, kq)
        M = dA * _bnt(kbb, kq) + dB * _bnt(qq, kq)
        mh, ml = _split(M)
        colsum = _btn(jnp.concatenate([mh, ml], axis=1), jnp.ones((GNV * ncb, 2 * CHUNK, 128), bf16))
        lastrow = _iota((1, CHUNK, 128), 1) == CHUNK - 1
        for b, (c, e) in enumerate(pairs):
            dvb_ref[rs(c), cs(e)] = dvb[b]
            dkbg_ref[rs(c), cs(e)] = dkbg[b]
            dkb_ref[rs(c), cs(e)] = dkb[b]
            dgc_ref[rs(c), cs(e)] = (jnp.sum(M[b], axis=1, keepdims=True) - colsum[b]
                                     + jnp.where(lastrow[0], dgl_scr[b][0:1, :], 0.0))
        for b, (c, e) in enumerate(pairs):
            if e % 2 == 0:
                dq_ref[rs(c), ks(e)] = dqc[b] + dqc[b + 1]
                dk_ref[rs(c), ks(e)] = dkc[b] + dkc[b + 1]

    b1 = pl.BlockSpec((GTB, 128 * GQH), lambda j, i: (nb - 1 - i, j))
    b2 = pl.BlockSpec((GTB, 256 * GQH), lambda j, i: (nb - 1 - i, j))
    sd = jax.ShapeDtypeStruct
    return _call(
        body, name="gdn_bwd", grid=(GQK_H // GQH, nb),
        in_specs=[b2, b1, b1, b2, b2, b2, b2, b2, b1, pl.BlockSpec((GTB // 8, 256 * GQH), lambda j, i: (nb - 1 - i, j)),
                  b2, b1, b1, b2, pl.BlockSpec((ncb * 128, 256 * GQH), lambda j, i: (nb - 1 - i, j))],
        out_specs=[b1, b1, b2, b2, b2, b2, b2, b2],
        out_shape=[sd((S, 1024), f32)] * 2 + [sd((S, 2048), f32)] * 6,
        scratch_shapes=[pltpu.VMEM((GNV, 128, 128), f32), pltpu.VMEM((GNV * ncb, CHUNK, 128), bf16),
                        pltpu.VMEM((GNV * ncb, CHUNK, 128), bf16), pltpu.VMEM((GNV * ncb, CHUNK, CHUNK), f32),
                        pltpu.VMEM((GNV * ncb, 8, 128), f32)],
        sem=("parallel", "arbitrary"), args=(do, q, k, kb, kbg, vb, qd, kd, dm, gl8, w, at, T, vn, st), comm=comm)


def gdn_onorm(o, proj, nw):
    S = o.shape[0]
    tm = 256

    def body(o_ref, z_ref, nw_ref, o2_ref):
        for h in range(GV_H):
            hs = slice(h * 128, (h + 1) * 128)
            oh = o_ref[:, hs]
            r = lax.rsqrt(jnp.mean(oh * oh, axis=-1, keepdims=True) + EPS)
            o2_ref[:, hs] = (((oh * r) * nw_ref[...]) * _silu(z_ref[:, hs])).astype(bf16)

    t2 = pl.BlockSpec((tm, 2048), lambda i: (i, 0))
    return pl.pallas_call(
        body, name="gdn_onorm", grid=(S // tm,),
        in_specs=[t2, pl.BlockSpec((tm, 2048), lambda i: (i, G_Z0 // 2048)), pl.BlockSpec((1, 128), lambda i: (0, 0))],
        out_specs=t2, out_shape=jax.ShapeDtypeStruct((S, 2048), bf16),
        compiler_params=_params(("parallel",)),
    )(o, proj, nw)


def gdn_onorm_bwd(do2, o, proj, nw):
    S = o.shape[0]
    tm = 256

    def body(d_ref, o_ref, z_ref, nw_ref, do_ref, dz_ref, st_ref):
        i = pl.program_id(0)
        acc = jnp.zeros((1, 128), f32)
        for h in range(GV_H):
            hs = slice(h * 128, (h + 1) * 128)
            oh, z, d2 = o_ref[:, hs], z_ref[:, hs], d_ref[:, hs]
            r = lax.rsqrt(jnp.mean(oh * oh, axis=-1, keepdims=True) + EPS)
            on = oh * r
            dt = d2 * _silu(z)
            dz_ref[:, hs] = (d2 * (on * nw_ref[...]) * _dsilu(z)).astype(bf16)
            don = dt * nw_ref[...]
            acc = acc + jnp.sum(dt * on, axis=0, keepdims=True)
            do_ref[:, hs] = r * (don - on * jnp.mean(don * on, axis=-1, keepdims=True))
        upd = jnp.concatenate([acc, jnp.zeros((7, 128), f32)], axis=0)

        @pl.when(i == 0)
        def _():
            st_ref[...] = upd

        @pl.when(i > 0)
        def _():
            st_ref[...] += upd

    t2 = pl.BlockSpec((tm, 2048), lambda i: (i, 0))
    sd = jax.ShapeDtypeStruct
    return pl.pallas_call(
        body, name="gdn_onorm_bwd", grid=(S // tm,),
        in_specs=[t2, t2, pl.BlockSpec((tm, 2048), lambda i: (i, G_Z0 // 2048)), pl.BlockSpec((1, 128), lambda i: (0, 0))],
        out_specs=[t2, t2, pl.BlockSpec((8, 128), lambda i: (0, 0))],
        out_shape=[sd((S, 2048), f32), sd((S, 2048), bf16), sd((8, 128), f32)],
        compiler_params=_params(("arbitrary",)),
    )(do2, o, proj, nw)


def gdn_pre_bwd(proj, conv_w, alog, dtb, dq, dk, dkb, dkbg, dvb, dqd, dkd, dgcd):
    S = proj.shape[0]
    tm = 128

    def body(p_ref, halo_ref, ba_ref, w_ref, al_ref, dt_ref, dq_ref, dk_ref, dkb_ref, dkbg_ref, dvb_ref, dqd_ref, dkd_ref, dgc_ref,
             dcv_ref, dba_ref, st_ref, ext_scr):
        i = pl.program_id(0)
        first = i == 0
        ltri, utri, bsame = _chunk_mats(tm)
        beta, u, neg_a, g, gc, glast = _gdn_scalars(ba_ref[...], al_ref[...], dt_ref[...], ltri, bsame)
        eg, ek = jnp.exp(gc), jnp.exp(glast - gc)
        lane16 = _iota((tm, 16), 1)
        dgc_all = jnp.zeros((tm, 16), f32)
        rkd_all = jnp.zeros((tm, 16), f32)
        dbeta_all = jnp.zeros((tm, 16), f32)

        def pre(gi):
            return _conv(_conv_taps(p_ref, halo_ref, first, gi, ext_scr), w_ref[:, gi * 128:(gi + 1) * 128])

        def l2n_bwd(xt, dy):
            r = lax.rsqrt(jnp.sum(xt * xt, axis=-1, keepdims=True) + EPS)
            y = xt * r
            return r * (dy - y * jnp.sum(dy * y, axis=-1, keepdims=True))

        for j in range(GQK_H):
            js = slice(j * 128, (j + 1) * 128)
            cvq, cvk = pre(j), pre(GQK_H + j)
            qt, kt = _silu(cvq), _silu(cvk)
            qn = _l2n(qt) * (GHD ** -0.5)
            kn = _l2n(kt)
            dq_tot, dk_tot = dq_ref[:, js], dk_ref[:, js]
            for e in range(2):
                h = 2 * j + e
                hs = slice(h * 128, (h + 1) * 128)
                gv = 2 * GQK_H + h
                cvv = pre(gv)
                v = _silu(cvv)
                bh, egh, ekh = beta[:, h:h + 1], eg[:, h:h + 1], ek[:, h:h + 1]
                dkbg, dkd, dqd, dvb = dkbg_ref[:, hs], dkd_ref[:, hs], dqd_ref[:, hs], dvb_ref[:, hs]
                dkb_t = dkb_ref[:, hs] + dkbg * egh
                dk_tot = dk_tot + dkb_t * bh + dkd * ekh
                dq_tot = dq_tot + dqd * egh
                dcv_ref[:, gv * 128:(gv + 1) * 128] = (dvb * bh) * _dsilu(cvv)
                dbeta = jnp.sum(dkb_t * kn, axis=-1, keepdims=True) + jnp.sum(dvb * v, axis=-1, keepdims=True)
                rkd = jnp.sum(dkd * (kn * ekh), axis=-1, keepdims=True)
                dgc = (dgc_ref[:, hs][:, 0:1] + jnp.sum(dkbg * (kn * bh * egh), axis=-1, keepdims=True)
                       + jnp.sum(dqd * (qn * egh), axis=-1, keepdims=True) - rkd)
                sel = lane16 == h
                dgc_all = dgc_all + jnp.where(sel, dgc, 0.0)
                rkd_all = rkd_all + jnp.where(sel, rkd, 0.0)
                dbeta_all = dbeta_all + jnp.where(sel, dbeta, 0.0)
            dcv_ref[:, js] = l2n_bwd(qt, dq_tot * (GHD ** -0.5)) * _dsilu(cvq)
            ks = slice((GQK_H + j) * 128, (GQK_H + j + 1) * 128)
            dcv_ref[:, ks] = l2n_bwd(kt, dk_tot) * _dsilu(cvk)

        islast = jnp.bitwise_and(_iota((tm, 16), 0), CHUNK - 1) == CHUNK - 1
        dgc_all = dgc_all + jnp.where(islast, _nn(bsame, rkd_all, HI), 0.0)
        dg = _nn(utri, dgc_all, HI)
        da = dg * neg_a * _sigmoid(u)
        db = dbeta_all * beta * (1.0 - beta)
        r16, c128 = _iota((16, 128), 0), _iota((16, 128), 1)
        pb = jnp.where(c128 == r16, 1.0, 0.0).astype(f32)
        pa = jnp.where(c128 == r16 + 16, 1.0, 0.0).astype(f32)
        dba_ref[...] = _nn(db, pb, HI) + _nn(da, pa, HI)
        upd = jnp.concatenate([jnp.sum(dg * g, axis=0, keepdims=True), jnp.sum(da, axis=0, keepdims=True),
                               jnp.zeros((6, 16), f32)], axis=0)

        @pl.when(i == 0)
        def _():
            st_ref[...] = upd

        @pl.when(i > 0)
        def _():
            st_ref[...] += upd

    full = lambda shape: pl.BlockSpec(shape, lambda i: (0, 0))
    t1 = pl.BlockSpec((tm, 1024), lambda i: (i, 0))
    t2 = pl.BlockSpec((tm, 2048), lambda i: (i, 0))
    sd = jax.ShapeDtypeStruct
    return pl.pallas_call(
        body, name="gdn_pre_bwd", grid=(S // tm,),
        in_specs=_gdn_in_specs(tm, S) + [full((CONV_K, G_CONV)), full((1, 16)), full((1, 16)), t1, t1] + [t2] * 6,
        out_specs=[pl.BlockSpec((tm, G_CONV), lambda i: (i, 0)), pl.BlockSpec((tm, 128), lambda i: (i, 0)), full((8, 16))],
        out_shape=[sd((S, G_CONV), f32), sd((S, 128), f32), sd((8, 16), f32)],
        scratch_shapes=[_conv_scratch(tm)],
        compiler_params=_params(("arbitrary",)),
    )(proj, proj, proj, conv_w, alog, dtb, dq, dk, dkb, dkbg, dvb, dqd, dkd, dgcd)


def gdn_conv_bwd(proj, conv_w, dcv, dz, dba):
    S = proj.shape[0]
    tm = 256
    nb, nb8 = S // tm, tm // 8

    def body(p_ref, halo_ref, w_ref, dcv_ref, nxt_ref, dz_ref, dba_ref, dp_ref, dw_ref, ext_scr, nxt_scr):
        i = pl.program_id(0)
        first, last = i == 0, i == nb - 1
        for gi in range(G_CONV // 128):
            cs = slice(gi * 128, (gi + 1) * 128)
            taps = _conv_taps(p_ref, halo_ref, first, gi, ext_scr)
            cur = dcv_ref[:, cs]
            nxt_scr[gi, 0:tm, :] = cur
            nxt_scr[gi, tm:, :] = jnp.where(last, 0.0, nxt_ref[:, cs])
            w = w_ref[:, cs]
            dp = cur * w[3:4]
            rows = [jnp.sum(cur * taps[3 - kk], axis=0, keepdims=True) for kk in range(CONV_K)]
            for s in range(1, CONV_K):
                dp = dp + nxt_scr[gi, s:s + tm, :] * w[3 - s:4 - s]
            dp_ref[:, cs] = dp.astype(bf16)
            upd = jnp.concatenate(rows + [jnp.zeros((4, 128), f32)], axis=0)

            @pl.when(first)
            def _():
                dw_ref[:, cs] = upd

            @pl.when(i > 0)
            def _():
                dw_ref[:, cs] += upd

        dp_ref[:, G_Z0:G_BA0] = dz_ref[...]
        dp_ref[:, G_BA0:G_INP] = dba_ref[...].astype(bf16)

    sd = jax.ShapeDtypeStruct
    return pl.pallas_call(
        body, name="gdn_conv_bwd", grid=(nb,),
        in_specs=[pl.BlockSpec((tm, G_CONV), lambda i: (i, 0)),
                  pl.BlockSpec((8, G_CONV), lambda i: (jnp.maximum(i * nb8 - 1, 0), 0)),
                  pl.BlockSpec((CONV_K, G_CONV), lambda i: (0, 0)),
                  pl.BlockSpec((tm, G_CONV), lambda i: (i, 0)),
                  pl.BlockSpec((8, G_CONV), lambda i: (jnp.minimum((i + 1) * nb8, S // 8 - 1), 0)),
                  pl.BlockSpec((tm, 2048), lambda i: (i, 0)), pl.BlockSpec((tm, 128), lambda i: (i, 0))],
        out_specs=[pl.BlockSpec((tm, G_INP), lambda i: (i, 0)), pl.BlockSpec((8, G_CONV), lambda i: (0, 0))],
        out_shape=[sd((S, G_INP), bf16), sd((8, G_CONV), f32)],
        scratch_shapes=[_conv_scratch(tm), _conv_scratch(tm)],
        compiler_params=_params(("arbitrary",)),
    )(proj, proj, conv_w, dcv, dcv, dz, dba)


def _half_mean(t, lo_half):
    m0 = jnp.sum(jnp.where(lo_half, t, 0.0), axis=-1, keepdims=True)
    m1 = jnp.sum(jnp.where(lo_half, 0.0, t), axis=-1, keepdims=True)
    return jnp.where(lo_half, m0, m1) * (1.0 / F_HD)


def _split3(c):
    hi = c.astype(bf16).astype(f32)
    mid = (c - hi).astype(bf16).astype(f32)
    lo = (c - hi - mid).astype(bf16).astype(f32)
    return hi, mid, lo


def fox_pre(proj, fbias, qw2, kw2):
    S = proj.shape[0]
    tm = 256

    def body(q_ref, k_ref, v_ref, f_ref, fb_ref, qw_ref, kw_ref, qa_ref, ka_ref, vb_ref, carry):
        @pl.when(pl.program_id(0) == 0)
        def _():
            carry[...] = jnp.zeros_like(carry)

        logf = -_softplus(-(f_ref[:, 0:16] + fb_ref[...]))
        ltri = jnp.where(_iota((tm, tm), 1) <= _iota((tm, tm), 0), 1.0, 0.0).astype(f32)
        cum = _nn(ltri, logf, HI) + carry[0:1, :]
        carry[0:1, :] = cum[tm - 1:tm, :]
        lane = _iota((tm, 128), 1)
        lo_half = lane < F_HD
        for p in range(F_H // 2):
            ps = slice(p * 128, (p + 1) * 128)
            for src, w_ref, dst, is_q in ((q_ref, qw_ref, qa_ref, True), (k_ref, kw_ref, ka_ref, False)):
                x = src[:, ps]
                xn = x * lax.rsqrt(_half_mean(x * x, lo_half) + EPS) * w_ref[...]
                if is_q:
                    xn = xn * (F_HD ** -0.5)
                for e in range(2):
                    h = 2 * p + e
                    base = xn if e == 0 else pltpu.roll(xn, F_HD, 1)
                    hi, mid, lo = _split3(cum[:, h:h + 1])
                    pieces = jnp.where(lane == 64, hi, 0.0) + jnp.where(lane == 65, mid, 0.0) + jnp.where(lane == 66, lo, 0.0)
                    if is_q:
                        ext = pieces + jnp.where((lane >= 67) & (lane <= 69), 1.0, 0.0)
                    else:
                        ext = jnp.where((lane >= 64) & (lane <= 66), 1.0, 0.0) - pltpu.roll(pieces, 3, 1)
                    dst[:, h * 128:(h + 1) * 128] = jnp.where(lo_half, base, ext).astype(bf16)
        one = jnp.where(lane == F_HD, 1.0, 0.0)
        for p in range(F_H // 2):
            vv = v_ref[:, p * 128:(p + 1) * 128]
            vb_ref[:, (2 * p) * 128:(2 * p + 1) * 128] = jnp.where(lo_half, vv, one).astype(bf16)
            vb_ref[:, (2 * p + 1) * 128:(2 * p + 2) * 128] = jnp.where(lo_half, pltpu.roll(vv, F_HD, 1), one).astype(bf16)

    t1 = lambda c: pl.BlockSpec((tm, 1024), lambda i: (i, c))
    vec = lambda n: pl.BlockSpec((1, n), lambda i: (0, 0))
    sd = jax.ShapeDtypeStruct
    return pl.pallas_call(
        body, name="fox_pre", grid=(S // tm,),
        in_specs=[t1(0), t1(1), t1(2), pl.BlockSpec((tm, 128), lambda i: (i, F_F0 // 128)), vec(16), vec(128), vec(128)],
        out_specs=[pl.BlockSpec((tm, 2048), lambda i: (i, 0))] * 3,
        out_shape=[sd((S, 2048), bf16)] * 3,
        scratch_shapes=[pltpu.VMEM((8, 16), f32)],
        compiler_params=_params(("arbitrary",)),
    )(proj, proj, proj, proj, fbias, qw2, kw2)


FTQ = 512
FHS_FWD, FHS_BWD = 8, 4


def fox_attn(qa, ka, v, comm=None):
    S = qa.shape[0]
    nq = S // FTQ
    FHS = FHS_FWD

    live = [(i, j) for i in range(nq) for j in range(i + 1)]
    qi_tab = jnp.asarray([i for i, _ in live], jnp.int32)
    kj_tab = jnp.asarray([j for _, j in live], jnp.int32)

    def body(qi_ref, kj_ref, q_ref, k_ref, v_ref, o_ref, lse_ref, m_scr, acc_scr):
        t = pl.program_id(1)
        i, j = qi_ref[t], kj_ref[t]

        @pl.when(j == 0)
        def _():
            m_scr[...] = jnp.full_like(m_scr, NEG)
            acc_scr[...] = jnp.zeros_like(acc_scr)

        def step(diagonal):
            for e in range(FHS):
                es = slice(e * 128, (e + 1) * 128)
                s = _nt(q_ref[:, es], k_ref[:, es])
                if diagonal:
                    s = jnp.where(_iota((FTQ, FTQ), 0) >= _iota((FTQ, FTQ), 1), s, NEG)
                m_old = m_scr[e]
                m_new = jnp.maximum(m_old, jnp.max(s, axis=-1, keepdims=True))
                p = jnp.exp(s - m_new[:, 0:1])
                acc_scr[e] = acc_scr[e] * jnp.exp(m_old - m_new) + _nn(p.astype(bf16), v_ref[:, es])
                m_scr[e] = m_new

        pl.when(j < i)(functools.partial(step, False))

        @pl.when(j == i)
        def _():
            step(True)
            for e in range(FHS):
                vs = slice(e * F_HD, (e + 1) * F_HD)
                acc = acc_scr[e]
                l = acc[:, F_HD:F_HD + 1]
                o_ref[:, vs] = acc[:, 0:F_HD] / l
                lse_ref[:, vs] = m_scr[e][:, 0:F_HD] + jnp.log(l)

    sd = jax.ShapeDtypeStruct
    qo = pl.BlockSpec((FTQ, F_HD * FHS), lambda p, t, qi, kj: (qi[t], p))
    kv = pl.BlockSpec((FTQ, 128 * FHS), lambda p, t, qi, kj: (kj[t], p))
    return _call(
        body, name="fox_attn", grid=(F_H // FHS, len(live)),
        in_specs=[pl.BlockSpec((FTQ, 128 * FHS), lambda p, t, qi, kj: (qi[t], p)), kv, kv],
        out_specs=[qo, qo],
        out_shape=[sd((S, 1024), f32), sd((S, 1024), f32)],
        scratch_shapes=[pltpu.VMEM((FHS, FTQ, 128), f32), pltpu.VMEM((FHS, FTQ, 128), f32)],
        sem=("parallel", "arbitrary"), args=(qa, ka, v), comm=comm, prefetch=(qi_tab, kj_tab))


def fox_attn_bwd(qa, ka, v, do, lse, delta, comm=None):
    S = qa.shape[0]
    nq = S // FTQ
    FHS = FHS_BWD

    live = [(j, i) for j in range(nq) for i in range(j, nq)]
    kj_tab = jnp.asarray([j for j, _ in live], jnp.int32)
    qi_tab = jnp.asarray([i for _, i in live], jnp.int32)

    def body(kj_ref, qi_ref, q_ref, k_ref, v_ref, do_ref, lse_ref, dl_ref, dq_ref, dk_ref, dv_ref, dk_scr, dv_scr):
        t = pl.program_id(1)
        j, i = kj_ref[t], qi_ref[t]

        @pl.when(t == 0)
        def _():
            dq_ref[...] = jnp.zeros_like(dq_ref)

        @pl.when(i == j)
        def _():
            dk_scr[...] = jnp.zeros_like(dk_scr)
            dv_scr[...] = jnp.zeros_like(dv_scr)

        def step(diagonal):
            rows = pl.ds(pl.multiple_of(i * FTQ, FTQ), FTQ)
            for e in range(FHS):
                es, vs = slice(e * 128, (e + 1) * 128), slice(e * F_HD, (e + 1) * F_HD)
                qe, ke = q_ref[:, es], k_ref[:, es]
                dob = do_ref[:, vs]
                s = _nt(qe, ke)
                if diagonal:
                    s = jnp.where(_iota((FTQ, FTQ), 0) >= _iota((FTQ, FTQ), 1), s, NEG)
                p = jnp.exp(s - lse_ref[:, e * F_HD:e * F_HD + 1])
                ds = p * (_nt(dob, v_ref[:, e * 128:e * 128 + F_HD]) - dl_ref[:, e * F_HD:e * F_HD + 1])
                dsb = ds.astype(bf16)
                dv_scr[e] += _tn(dob, p.astype(bf16))
                dk_scr[e] += _tn(qe, dsb)
                dq_ref[rows, es] += _nn(dsb, ke)

        pl.when(i > j)(functools.partial(step, False))
        pl.when(i == j)(functools.partial(step, True))

        @pl.when(i == nq - 1)
        def _():
            for e in range(FHS):
                dk_ref[:, e * 128:(e + 1) * 128] = dk_scr[e].T
                dv_ref[:, e * F_HD:(e + 1) * F_HD] = dv_scr[e].T

    sd = jax.ShapeDtypeStruct
    qi = lambda w: pl.BlockSpec((FTQ, w * FHS), lambda p, t, kj_, qi_: (qi_[t], p))
    kj = lambda w: pl.BlockSpec((FTQ, w * FHS), lambda p, t, kj_, qi_: (kj_[t], p))
    return _call(
        body, name="fox_attn_bwd", grid=(F_H // FHS, len(live)),
        in_specs=[qi(128), kj(128), kj(128), qi(F_HD), qi(F_HD), qi(F_HD)],
        out_specs=[pl.BlockSpec((S, 128 * FHS), lambda p, t, kj_, qi_: (0, p)), kj(128), kj(F_HD)],
        out_shape=[sd((S, 2048), f32), sd((S, 2048), f32), sd((S, 1024), f32)],
        scratch_shapes=[pltpu.VMEM((FHS, 128, FTQ), f32), pltpu.VMEM((FHS, F_HD, FTQ), f32)],
        sem=("parallel", "arbitrary"), args=(qa, ka, v, do, lse, delta), comm=comm, prefetch=(kj_tab, qi_tab))


def fox_gate(o, proj):
    S = o.shape[0]
    tm = 512

    def body(o_ref, z_ref, o2_ref):
        o2_ref[...] = (o_ref[...] * _silu(z_ref[...])).astype(bf16)

    t = pl.BlockSpec((tm, 1024), lambda i: (i, 0))
    return pl.pallas_call(
        body, name="fox_gate", grid=(S // tm,),
        in_specs=[t, pl.BlockSpec((tm, 1024), lambda i: (i, 3))], out_specs=t,
        out_shape=jax.ShapeDtypeStruct((S, 1024), bf16),
        compiler_params=_params(("parallel",)),
    )(o, proj)


def fox_gate_bwd(do2, o, proj):
    S = o.shape[0]
    tm = 256

    def body(d_ref, o_ref, z_ref, do_ref, dz_ref, dl_ref):
        lo_half = _iota((tm, 128), 1) < F_HD
        for p in range(F_H // 2):
            ps = slice(p * 128, (p + 1) * 128)
            d2, ov, z = d_ref[:, ps], o_ref[:, ps], z_ref[:, ps]
            dov = d2 * _silu(z)
            do_ref[:, ps] = dov.astype(bf16)
            dz_ref[:, ps] = (d2 * ov * _dsilu(z)).astype(bf16)
            dl_ref[:, ps] = _half_mean(dov * ov, lo_half) * float(F_HD)

    t = pl.BlockSpec((tm, 1024), lambda i: (i, 0))
    sd = jax.ShapeDtypeStruct
    return pl.pallas_call(
        body, name="fox_gate_bwd", grid=(S // tm,),
        in_specs=[t, t, pl.BlockSpec((tm, 1024), lambda i: (i, 3))], out_specs=[t, t, t],
        out_shape=[sd((S, 1024), bf16), sd((S, 1024), bf16), sd((S, 1024), f32)],
        compiler_params=_params(("parallel",)),
    )(do2, o, proj)


def fox_pre_bwd(proj, fbias, qw2, kw2, dqa, dka, dv, dz):
    S = proj.shape[0]
    tm = 256
    nb = S // tm

    def body(q_ref, k_ref, f_ref, fb_ref, qw_ref, kw_ref, dqa_ref, dka_ref, dv_ref, dz_ref, dp_ref, st_ref, carry):
        i = pl.program_id(0)

        @pl.when(i == 0)
        def _():
            carry[...] = jnp.zeros_like(carry)

        lane = _iota((tm, 128), 1)
        lo_half = lane < F_HD
        lane16 = _iota((tm, 16), 1)
        dcum = jnp.zeros((tm, 16), f32)
        dws = []
        for src, w_ref, dsrc, is_q, col0 in ((q_ref, qw_ref, dqa_ref, True, 0), (k_ref, kw_ref, dka_ref, False, 1024)):
            dw = jnp.zeros((1, 128), f32)
            for p in range(F_H // 2):
                ps = slice(p * 128, (p + 1) * 128)
                x = src[:, ps]
                r = lax.rsqrt(_half_mean(x * x, lo_half) + EPS)
                xh = x * r
                d0 = dsrc[:, (2 * p) * 128:(2 * p + 1) * 128]
                d1 = dsrc[:, (2 * p + 1) * 128:(2 * p + 2) * 128]
                dy = jnp.where(lo_half, d0, pltpu.roll(d1, F_HD, 1))
                if is_q:
                    dy = dy * (F_HD ** -0.5)
                dxh = dy * w_ref[...]
                dw = dw + jnp.sum(dy * xh, axis=0, keepdims=True)
                dp_ref[:, col0 + p * 128:col0 + (p + 1) * 128] = (r * (dxh - xh * _half_mean(dxh * xh, lo_half))).astype(bf16)
                for e, de in ((0, d0), (1, d1)):
                    col = de[:, 64:65] if is_q else -de[:, 67:68]
                    dcum = dcum + jnp.where(lane16 == 2 * p + e, col, 0.0)
            dws.append(dw)
        dp_ref[:, 2048:3072] = dv_ref[...].astype(bf16)
        dp_ref[:, 3072:4096] = dz_ref[...]
        utri = jnp.where(_iota((tm, tm), 1) >= _iota((tm, tm), 0), 1.0, 0.0).astype(f32)
        dlogf = _nn(utri, dcum, HI) + carry[0:1, :]
        carry[0:1, :] = dlogf[0:1, :]
        fl = f_ref[:, 0:16] + fb_ref[...]
        df = dlogf * _sigmoid(-fl)
        place = jnp.where(_iota((16, 128), 1) == _iota((16, 128), 0), 1.0, 0.0).astype(f32)
        dfw = _nn(df, place, HI)
        dp_ref[:, F_F0:F_INP] = dfw.astype(bf16)
        upd = jnp.concatenate(dws + [jnp.sum(dfw, axis=0, keepdims=True), jnp.zeros((5, 128), f32)], axis=0)

        @pl.when(i == 0)
        def _():
            st_ref[...] = upd

        @pl.when(i > 0)
        def _():
            st_ref[...] += upd

    rev = lambda w, c: pl.BlockSpec((tm, w), lambda i: (nb - 1 - i, c))
    vec = lambda n: pl.BlockSpec((1, n), lambda i: (0, 0))
    sd = jax.ShapeDtypeStruct
    return pl.pallas_call(
        body, name="fox_pre_bwd", grid=(nb,),
        in_specs=[rev(1024, 0), rev(1024, 1), rev(128, F_F0 // 128), vec(16), vec(128), vec(128),
                  rev(2048, 0), rev(2048, 0), rev(1024, 0), rev(1024, 0)],
        out_specs=[rev(F_INP, 0), pl.BlockSpec((8, 128), lambda i: (0, 0))],
        out_shape=[sd((S, F_INP), bf16), sd((8, 128), f32)],
        scratch_shapes=[pltpu.VMEM((8, 16), f32)],
        compiler_params=_params(("arbitrary",)),
    )(proj, proj, proj, fbias, qw2, kw2, dqa, dka, dv, dz)


def _me():
    return lax.axis_index("x"), lax.axis_index("y"), lax.axis_index("c")


def _other_chips(x, y):
    return [(1 - x, y), (x, 1 - y), (1 - x, 1 - y)]


def ag_small(xs):
    m_per, n = xs.shape

    def body(x_ref, out_ref, send_sems, recv_sems, local_sem):
        x, y, c = _me()
        me, sibling = (x, y, c), (x, y, 1 - c)
        chips = _other_chips(x, y)

        def rows(px, py, pc):
            return out_ref.at[pl.ds((4 * px + 2 * py + pc) * m_per, m_per), :]

        def copy(k, block, to, src=None):
            return pltpu.make_async_remote_copy(
                src_ref=rows(*block) if src is None else src, dst_ref=rows(*block),
                send_sem=send_sems.at[k], recv_sem=recv_sems.at[k], device_id=to, device_id_type=MESH)

        mine = pltpu.make_async_copy(x_ref, rows(*me), local_sem)
        mine.start()
        first = [copy(0, me, sibling, src=x_ref)]
        first += [copy(1 + j, me, (*chip, c), src=x_ref) for j, chip in enumerate(chips)]
        for cp in first:
            cp.start()
        passed = [copy(4 + j, (*chip, c), sibling) for j, chip in enumerate(chips)]
        for j, chip in enumerate(chips):
            copy(1 + j, (*chip, c), me).wait_recv()
            passed[j].start()
        copy(0, sibling, me).wait_recv()
        for j, chip in enumerate(chips):
            copy(4 + j, (*chip, 1 - c), me).wait_recv()
        for cp in first + passed:
            cp.wait_send()
        mine.wait()

    return pl.pallas_call(
        body, name="ag_small",
        out_shape=jax.ShapeDtypeStruct((8 * m_per, n), xs.dtype),
        in_specs=[pl.BlockSpec(memory_space=pltpu.VMEM)], out_specs=pl.BlockSpec(memory_space=pltpu.VMEM),
        scratch_shapes=[pltpu.SemaphoreType.DMA((7,)), pltpu.SemaphoreType.DMA((7,)), pltpu.SemaphoreType.DMA],
        compiler_params=pltpu.CompilerParams(vmem_limit_bytes=VMEM_LIMIT),
    )(xs)


_ANY = pl.BlockSpec(memory_space=pl.ANY)


def ag_chips(arrs):
    n = len(arrs)
    assert all(a.shape[0] == 2 for a in arrs)

    def body(*refs):
        ins, outs = refs[:n], refs[n:2 * n]
        send_sems, recv_sems, fwd_send, fwd_recv, local_sems = refs[2 * n:]
        x, y, c = _me()
        me = 2 * x + y
        chips = _other_chips(x, y)
        started = []
        for a in range(n):
            cp = pltpu.make_async_copy(ins[a], outs[a].at[me], local_sems.at[a])
            cp.start()
            started.append(cp)
        sends = []
        for a in range(n):
            for j, (px, py) in enumerate(chips):
                r = pltpu.make_async_remote_copy(
                    src_ref=ins[a].at[c], dst_ref=outs[a].at[me, c], send_sem=send_sems.at[3 * a + j],
                    recv_sem=recv_sems.at[3 * a + j], device_id=(px, py, c), device_id_type=MESH)
                r.start()
                sends.append(r)
        for a in range(n):
            for j, (px, py) in enumerate(chips):
                got = outs[a].at[2 * px + py, c]
                pltpu.make_async_remote_copy(
                    src_ref=ins[a].at[c], dst_ref=got, send_sem=send_sems.at[3 * a + j],
                    recv_sem=recv_sems.at[3 * a + j], device_id=(px, py, c), device_id_type=MESH).wait_recv()
                f = pltpu.make_async_remote_copy(
                    src_ref=got, dst_ref=got, send_sem=fwd_send.at[3 * a + j], recv_sem=fwd_recv.at[3 * a + j],
                    device_id=(x, y, 1 - c), device_id_type=MESH)
                f.start()
                sends.append(f)
        for a in range(n):
            for j, (px, py) in enumerate(chips):
                theirs = outs[a].at[2 * px + py, 1 - c]
                pltpu.make_async_remote_copy(
                    src_ref=theirs, dst_ref=theirs, send_sem=fwd_send.at[3 * a + j], recv_sem=fwd_recv.at[3 * a + j],
                    device_id=(x, y, 1 - c), device_id_type=MESH).wait_recv()
        for r in sends:
            r.wait_send()
        for cp in started:
            cp.wait()

    sems = pltpu.SemaphoreType.DMA((3 * n,))
    return pl.pallas_call(
        body, name="ag_chips",
        out_shape=[jax.ShapeDtypeStruct((4,) + a.shape, a.dtype) for a in arrs],
        in_specs=[_ANY] * n, out_specs=[_ANY] * n,
        scratch_shapes=[sems, sems, sems, sems, pltpu.SemaphoreType.DMA((n,))],
    )(*arrs)


def _ag_comm(arrs):
    n = len(arrs)

    def copies(ins, outs, sems, inbound):
        send_sems, recv_sems, local_sems = sems
        x, y, c = _me()
        me = 2 * x + y
        local = [pltpu.make_async_copy(ins[a], outs[a].at[me], local_sems.at[a]) for a in range(n)]
        out_cp, in_cp = [], []
        for a in range(n):
            for j, (px, py) in enumerate(_other_chips(x, y)):
                mk = functools.partial(pltpu.make_async_remote_copy, src_ref=ins[a], send_sem=send_sems.at[3 * a + j],
                                       recv_sem=recv_sems.at[3 * a + j], device_id=(px, py, c), device_id_type=MESH)
                out_cp.append(mk(dst_ref=outs[a].at[me]))
                if inbound:
                    in_cp.append(mk(dst_ref=outs[a].at[2 * px + py]))
        return local, out_cp, in_cp

    def start(ins, outs, sems):
        local, out_cp, _ = copies(ins, outs, sems, False)
        for cp in local + out_cp:
            cp.start()

    def wait(ins, outs, sems):
        local, out_cp, in_cp = copies(ins, outs, sems, True)
        for cp in in_cp:
            cp.wait_recv()
        for cp in out_cp:
            cp.wait_send()
        for cp in local:
            cp.wait()

    sems = [pltpu.SemaphoreType.DMA((3 * n,)), pltpu.SemaphoreType.DMA((3 * n,)), pltpu.SemaphoreType.DMA((n,))]
    return _Comm(arrs, [jax.ShapeDtypeStruct((4,) + a.shape, a.dtype) for a in arrs], sems, start, wait)


def _rs_comm(gs):
    n = len(gs)
    flips = [(fx, fy, fc) for fx in (0, 1) for fy in (0, 1) for fc in (0, 1)][1:]

    def copies(ins, outs, sems, inbound):
        send_sems, recv_sems, local_sems = sems
        x, y, c = _me()
        me = 4 * x + 2 * y + c
        local, out_cp, in_cp = [], [], []
        for a in range(n):
            rh = ins[a].shape[1] // 2
            mine = ins[a].at[2 * x + y, pl.ds(c * rh, rh), :]
            local.append(pltpu.make_async_copy(mine, outs[a].at[me], local_sems.at[a]))
            for j, (fx, fy, fc) in enumerate(flips):
                px, py, pc = (1 - x if fx else x), (1 - y if fy else y), (1 - c if fc else c)
                mk = functools.partial(pltpu.make_async_remote_copy, send_sem=send_sems.at[7 * a + j],
                                       recv_sem=recv_sems.at[7 * a + j], device_id=(px, py, pc), device_id_type=MESH)
                out_cp.append(mk(src_ref=ins[a].at[2 * px + py, pl.ds(pc * rh, rh), :], dst_ref=outs[a].at[me]))
                if inbound:
                    in_cp.append(mk(src_ref=mine, dst_ref=outs[a].at[4 * px + 2 * py + pc]))
        return local, out_cp, in_cp

    def start(ins, outs, sems):
        local, out_cp, _ = copies(ins, outs, sems, False)
        for cp in local + out_cp:
            cp.start()

    def wait(ins, outs, sems):
        local, out_cp, in_cp = copies(ins, outs, sems, True)
        for cp in in_cp:
            cp.wait_recv()
        for cp in out_cp:
            cp.wait_send()
        for cp in local:
            cp.wait()

    sems = [pltpu.SemaphoreType.DMA((7 * n,)), pltpu.SemaphoreType.DMA((7 * n,)), pltpu.SemaphoreType.DMA((n,))]
    return _Comm(gs, [jax.ShapeDtypeStruct((8, g.shape[1] // 2, g.shape[2]), g.dtype) for g in gs], sems, start, wait)


def sum_leading(q, name):
    K, R, C = q.shape
    tr = _pick(R, (256, 128, 64, 32, 16, 8))

    def body(q_ref, o_ref):
        acc = q_ref[0]
        for k in range(1, K):
            acc = acc + q_ref[k]
        o_ref[...] = acc

    return pl.pallas_call(
        body, name=name, grid=(R // tr,),
        in_specs=[pl.BlockSpec((K, tr, C), lambda i: (0, i, 0))], out_specs=pl.BlockSpec((tr, C), lambda i: (i, 0)),
        out_shape=jax.ShapeDtypeStruct((R, C), f32),
        compiler_params=_params(("parallel",)),
    )(q)


def rs_sum_devices(q, cidx, layer, n_layers, into=None):
    K, R, C = q.shape
    tr = _pick(R, (256, 128))

    def body(c_ref, q_ref, *rest):
        acc = q_ref[0].astype(f32)
        for k in range(1, K):
            acc = acc + q_ref[k].astype(f32)
        rest[-1][0, 0] = acc

    return pl.pallas_call(
        body, name="rs_sum_devices",
        grid_spec=pltpu.PrefetchScalarGridSpec(
            num_scalar_prefetch=1, grid=(R // tr,),
            in_specs=[pl.BlockSpec((K, tr, C), lambda i, c_ref: (0, i, 0))] + ([] if into is None else [_ANY]),
            out_specs=pl.BlockSpec((1, 1, tr, C), lambda i, c_ref: (layer, c_ref[0], i, 0))),
        out_shape=jax.ShapeDtypeStruct((n_layers, 2, R, C), f32),
        input_output_aliases={} if into is None else {2: 0},
        compiler_params=_params(("parallel",)),
    )(cidx, q, *([] if into is None else [into]))


def rs_share_halves(rs):
    n = len(rs)

    def body(*refs):
        bufs = refs[n:2 * n]
        send_sems, recv_sems = refs[2 * n:]
        x, y, c = _me()
        cps = []
        for a in range(n):
            mine = bufs[a].at[pl.ds(0, bufs[a].shape[0]), c]
            cp = pltpu.make_async_remote_copy(
                src_ref=mine, dst_ref=mine, send_sem=send_sems.at[a], recv_sem=recv_sems.at[a],
                device_id=(x, y, 1 - c), device_id_type=MESH)
            cp.start()
            cps.append(cp)
        for a, cp in enumerate(cps):
            theirs = bufs[a].at[pl.ds(0, bufs[a].shape[0]), 1 - c]
            pltpu.make_async_remote_copy(
                src_ref=theirs, dst_ref=theirs, send_sem=send_sems.at[a], recv_sem=recv_sems.at[a],
                device_id=(x, y, 1 - c), device_id_type=MESH).wait_recv()
            cp.wait_send()

    return pl.pallas_call(
        body, name="rs_share_halves",
        out_shape=[jax.ShapeDtypeStruct(r.shape, r.dtype) for r in rs],
        in_specs=[_ANY] * n, out_specs=[_ANY] * n, input_output_aliases={a: a for a in range(n)},
        scratch_shapes=[pltpu.SemaphoreType.DMA((n,)), pltpu.SemaphoreType.DMA((n,))],
    )(*rs)


def ada_mod(c_all, ada_w):
    L, _, n = ada_w.shape

    def body(c_ref, w_ref, o_ref):
        o_ref[0] = _nn(_silu(c_ref[...]), w_ref[0], HI)

    return pl.pallas_call(
        body, name="ada_mod", grid=(L,),
        in_specs=[pl.BlockSpec((8, D), lambda l: (0, 0)), pl.BlockSpec((1, D, n), lambda l: (l, 0, 0))],
        out_specs=pl.BlockSpec((1, 8, n), lambda l: (l, 0, 0)),
        out_shape=jax.ShapeDtypeStruct((L, 8, n), f32),
        compiler_params=_params(("parallel",)),
    )(c_all, ada_w)


def ada_w_grad(c_all, dmod):
    L, _, n = dmod.shape

    def body(c_ref, d_ref, o_ref):
        o_ref[0] = _tn(_silu(c_ref[...]), d_ref[0], HI)

    return pl.pallas_call(
        body, name="ada_w_grad", grid=(L,),
        in_specs=[pl.BlockSpec((8, D), lambda l: (0, 0)), pl.BlockSpec((1, 8, n), lambda l: (l, 0, 0))],
        out_specs=pl.BlockSpec((1, D, n), lambda l: (l, 0, 0)),
        out_shape=jax.ShapeDtypeStruct((L, D, n), f32),
        compiler_params=_params(("parallel",)),
    )(c_all, dmod)


def adamw(w, g, m, v, name):
    shp = w.shape
    two = lambda a: a.reshape(-1, shp[-1])
    R, C = two(w).shape
    tr = _pick(R, (256, 128, 64, 32, 16, 8))
    bc1, bc2 = 1.0 - B1 ** STEP, 1.0 - B2 ** STEP

    def body(w_ref, g_ref, m_ref, v_ref, d_ref, mo_ref, vo_ref):
        gv = g_ref[...]
        mn = B1 * m_ref[...] + (1.0 - B1) * gv
        vn = B2 * v_ref[...] + (1.0 - B2) * (gv * gv)
        d_ref[...] = -LR * ((mn / bc1) / (jnp.sqrt(vn / bc2) + AEPS) + WD * w_ref[...])
        mo_ref[...] = mn
        vo_ref[...] = vn

    t = pl.BlockSpec((tr, C), lambda i: (i, 0))
    outs = pl.pallas_call(
        body, name=name, grid=(R // tr,),
        in_specs=[t] * 4, out_specs=[t] * 3, out_shape=[jax.ShapeDtypeStruct((R, C), f32)] * 3,
        compiler_params=_params(("parallel",)),
    )(two(w), two(g), two(m), two(v))
    return [o.reshape(shp) for o in outs]


def _pack(arrs):
    parts, offs, r0 = [], [], 0
    for a in arrs:
        n = a.size
        rows = -(-n // 1024) * 8
        parts.append(jnp.pad(a.reshape(-1), (0, rows * 128 - n)).reshape(rows, 128))
        offs.append((r0, rows))
        r0 += rows
    return jnp.concatenate(parts, axis=0), offs


def _unpack(buf, offs, shapes):
    out = []
    for (r0, rows), shp in zip(offs, shapes):
        n = 1
        for d in shp:
            n *= d
        out.append(buf[..., r0:r0 + rows, :].reshape(buf.shape[:-2] + (rows * 128,))[..., :n].reshape(buf.shape[:-2] + tuple(shp)))
    return out


def kernel(x, c, norm_w, ada_w, ada_b, a_w_in, a_conv_w, a_A_log, a_dt_bias, a_norm_w, a_w_out, b_w_in, b_f_bias, b_qn_w, b_kn_w, b_w_out, final_norm_w, loss_target, m_norm_w, m_ada_w, m_ada_b, m_a_w_in, m_a_conv_w, m_a_A_log, m_a_dt_bias, m_a_norm_w, m_a_w_out, m_b_w_in, m_b_f_bias, m_b_qn_w, m_b_kn_w, m_b_w_out, m_final_norm_w, v_norm_w, v_ada_w, v_ada_b, v_a_w_in, v_a_conv_w, v_a_A_log, v_a_dt_bias, v_a_norm_w, v_a_w_out, v_b_w_in, v_b_f_bias, v_b_qn_w, v_b_kn_w, v_b_w_out, v_final_norm_w):
    weights = dict(norm_w=norm_w, ada_w=ada_w, ada_b=ada_b, a_w_in=a_w_in, a_conv_w=a_conv_w, a_A_log=a_A_log,
                   a_dt_bias=a_dt_bias, a_norm_w=a_norm_w, a_w_out=a_w_out, b_w_in=b_w_in, b_f_bias=b_f_bias,
                   b_qn_w=b_qn_w, b_kn_w=b_kn_w, b_w_out=b_w_out, final_norm_w=final_norm_w)
    m_in = dict(norm_w=m_norm_w, ada_w=m_ada_w, ada_b=m_ada_b, a_w_in=m_a_w_in, a_conv_w=m_a_conv_w, a_A_log=m_a_A_log,
                a_dt_bias=m_a_dt_bias, a_norm_w=m_a_norm_w, a_w_out=m_a_w_out, b_w_in=m_b_w_in, b_f_bias=m_b_f_bias,
                b_qn_w=m_b_qn_w, b_kn_w=m_b_kn_w, b_w_out=m_b_w_out, final_norm_w=m_final_norm_w)
    v_in = dict(norm_w=v_norm_w, ada_w=v_ada_w, ada_b=v_ada_b, a_w_in=v_a_w_in, a_conv_w=v_a_conv_w, a_A_log=v_a_A_log,
                a_dt_bias=v_a_dt_bias, a_norm_w=v_a_norm_w, a_w_out=v_a_w_out, b_w_in=v_b_w_in, b_f_bias=v_b_f_bias,
                b_qn_w=v_b_qn_w, b_kn_w=v_b_kn_w, b_w_out=v_b_w_out, final_norm_w=v_final_norm_w)
    xi, yi, ci = _me()
    me_b, me_k = 4 * xi + 2 * yi + ci, 2 * xi + yi
    cidx = ci.astype(jnp.int32).reshape(1)
    S = x.shape[1]
    depth, n_a, n_b = norm_w.shape[0], a_w_in.shape[0], b_w_in.shape[0]
    x0, tgt = x.reshape(S, D), loss_target.reshape(S, D)

    c_all = ag_small(jnp.pad(c, ((0, 7), (0, 0)))).reshape(8, 8, D)[:, 0]
    nloc = ada_w.shape[2]
    parts = ag_small(ada_mod(c_all, ada_w).reshape(depth * 8, nloc)).reshape(4, 2, depth, 8, nloc)[:, 0]
    mine = lax.dynamic_index_in_dim(parts, me_b, axis=2, keepdims=False)
    mod = jnp.transpose(mine, (1, 0, 2)).reshape(depth, 4 * nloc) + ada_b
    shift, scale, gate = (mod[:, k * D:(k + 1) * D] for k in range(3))

    w_loc = [(a_w_in[i // 2] if i % 2 == 0 else b_w_in[i // 2]).astype(bf16) for i in range(depth)]
    wo_loc = [(a_w_out[i // 2] if i % 2 == 0 else b_w_out[i // 2]).astype(bf16) for i in range(depth)]
    pad_in = [(G_INP - G_IN) if i % 2 == 0 else (F_INP - F_IN) for i in range(depth)]
    halves = lambda w: w.reshape((2, w.shape[0] // 2) + w.shape[1:])

    def cols_in_place(g_in, pad):
        w = jnp.transpose(g_in, (1, 0, 2)).reshape(g_in.shape[1], -1)
        return jnp.pad(w, ((0, 0), (0, pad)))

    g_in0, g_conv = ag_chips([halves(w_loc[0]), a_conv_w])
    w_in_full = [cols_in_place(g_in0.reshape((4,) + w_loc[0].shape), pad_in[0])]
    w_out_full = []
    conv = [jnp.transpose(g_conv[:, l], (1, 0, 2)).reshape(CONV_K, -1) for l in range(n_a)]
    qw2 = [_row(jnp.tile(b_qn_w[l], 2)) for l in range(n_b)]
    kw2 = [_row(jnp.tile(b_kn_w[l], 2)) for l in range(n_b)]

    saved, xc = [], x0
    for i in range(depth):
        l = i // 2
        nxt = _ag_comm([w_loc[i + 1], wo_loc[i + 1]]) if i + 1 < depth else None
        h = ln_mod(xc, _row(norm_w[i]), _row(scale[i]), _row(shift[i]))
        name = "mm_a_in" if i % 2 == 0 else "mm_b_in"
        if i == 0:
            proj, got = matmul(h, w_in_full[0], "nn", name, comm=_ag_comm([wo_loc[0]]))
            w_out_full.append(got[0].reshape(-1, D))
        else:
            proj = matmul(h, w_in_full[i], "nn", name)
        if i % 2 == 0:
            pre = gdn_pre(proj, conv[l], _row(a_A_log[l]), _row(a_dt_bias[l]))
            res, got = gdn_fwd(*pre, comm=nxt)
            o2 = gdn_onorm(res[0], proj, _row(a_norm_w[l]))
            y, xn = out_proj(o2, w_out_full[i], xc, _row(gate[i]), "out_proj_a")
        else:
            pre = fox_pre(proj, _row(b_f_bias[l]), qw2[l], kw2[l])
            res, got = fox_attn(*pre, comm=nxt)
            o2 = fox_gate(res[0], proj)
            y, xn = out_proj(o2, w_out_full[i], xc, _row(gate[i]), "out_proj_b")
        saved.append((xc, h, proj, o2, y, pre, res))
        if nxt is not None:
            w_in_full.append(cols_in_place(got[0], pad_in[i + 1]))
            w_out_full.append(got[1].reshape(-1, D))
        xc = xn
    dx, st_f = final_loss(xc, _row(final_norm_w), tgt)

    d_norm, d_mod = [None] * depth, [None] * depth
    d_conv, d_alog, d_dtb, d_anw = [None] * n_a, [None] * n_a, [None] * n_a, [None] * n_a
    d_fb, d_qn, d_kn = [None] * n_b, [None] * n_b, [None] * n_b
    ex_in, ex_out, pend_in = [None] * depth, [None] * depth, None
    for i in reversed(range(depth)):
        l = i // 2
        xin, h, proj, o2, y, pre, res = saved[i]
        ab = "a" if i % 2 == 0 else "b"
        dy, st_g = gate_bwd(dx, y, _row(gate[i]))
        do2 = matmul(dy, w_out_full[i], "nt", f"mm_{ab}_do2")
        d_out = matmul(o2, dy, "tn", f"mm_{ab}_dwo", out_dtype=bf16)
        ride = _rs_comm(([] if pend_in is None else [pend_in]) + [d_out.reshape(4, d_out.shape[0] // 4, D)])
        if i % 2 == 0:
            o, wv, at, tinv, vn, st = res
            do, dz, st_o = gdn_onorm_bwd(do2, o, proj, _row(a_norm_w[l]))
            grads, got = gdn_bwd(do, *pre, wv, at, tinv, vn, st, comm=ride)
            dcv, dba, st_s = gdn_pre_bwd(proj, conv[l], _row(a_A_log[l]), _row(a_dt_bias[l]), *grads)
            dproj, dcw = gdn_conv_bwd(proj, conv[l], dcv, dz, dba)
            d_conv[l], d_alog[l], d_dtb[l], d_anw[l] = dcw[:CONV_K], st_s[0], st_s[1], st_o[0]
        else:
            o, lse = res
            do, dz, delta = fox_gate_bwd(do2, o, proj)
            (dqa, dka, dv), got = fox_attn_bwd(*pre, do, lse, delta, comm=ride)
            dproj, st_b = fox_pre_bwd(proj, _row(b_f_bias[l]), qw2[l], kw2[l], dqa, dka, dv, dz)
            d_fb[l], d_qn[l], d_kn[l] = st_b[2, :F_H], st_b[0, :F_HD] + st_b[0, F_HD:], st_b[1, :F_HD] + st_b[1, F_HD:]
        ex_out[i] = got[-1]
        if pend_in is not None:
            ex_in[i + 1] = got[0]
        d_in = matmul(h, dproj, "tn", f"mm_{ab}_dw", out_dtype=bf16)
        cl = w_loc[i].shape[1]
        pend_in = jnp.transpose(d_in[:, :4 * cl].reshape(d_in.shape[0], 4, cl), (1, 0, 2))
        if i == 0:
            dh, got = matmul(dproj, w_in_full[i], "nt", f"mm_{ab}_dh", comm=_rs_comm([pend_in]))
            ex_in[0] = got[0]
        else:
            dh = matmul(dproj, w_in_full[i], "nt", f"mm_{ab}_dh")
        dx, st_n = ln_mod_bwd(xin, _row(norm_w[i]), _row(scale[i]), dh, dx)
        d_norm[i] = st_n[0]
        d_mod[i] = jnp.concatenate([st_n[2], st_n[1], st_g[0]])

    small = [jnp.stack(d_norm), jnp.stack(d_mod), jnp.stack(d_conv), jnp.stack(d_alog), jnp.stack(d_dtb), jnp.stack(d_anw),
             jnp.stack(d_fb), jnp.stack(d_qn), jnp.stack(d_kn), st_f[0], jnp.sum(st_f[1]).reshape(1)]
    shapes = [a.shape for a in small]
    buf, offs = _pack(small)
    gathered = ag_small(buf).reshape(8, buf.shape[0], 128)
    tot = _unpack(sum_leading(gathered, "sum_devices"), offs, shapes)
    g_norm, g_adab, g_convf, g_alog, g_dtb, g_anw, g_fb, g_qn, g_kn, g_fin, loss = tot
    dmod_all = _unpack(gathered, offs[1:2], shapes[1:2])[0]
    dmod_loc = lax.dynamic_slice_in_dim(dmod_all, me_k * nloc, nloc, axis=2)
    g_adaw = ada_w_grad(c_all, jnp.transpose(dmod_loc, (1, 0, 2)))
    g_conv_loc = lax.dynamic_slice_in_dim(g_convf, me_k * a_conv_w.shape[2], a_conv_w.shape[2], axis=2)

    bufs = {}
    for i in range(depth):
        for which, q in (("in", ex_in[i]), ("out", ex_out[i])):
            key = ("a" if i % 2 == 0 else "b", which)
            bufs[key] = rs_sum_devices(q, cidx, i // 2, depth // 2, into=bufs.get(key))
    keys = list(bufs)
    done = dict(zip(keys, rs_share_halves([bufs[k] for k in keys])))
    grads = dict(norm_w=g_norm, ada_w=g_adaw, ada_b=g_adab, a_w_in=done["a", "in"].reshape(a_w_in.shape),
                 a_conv_w=g_conv_loc, a_A_log=g_alog, a_dt_bias=g_dtb, a_norm_w=g_anw,
                 a_w_out=done["a", "out"].reshape(a_w_out.shape), b_w_in=done["b", "in"].reshape(b_w_in.shape),
                 b_f_bias=g_fb, b_qn_w=g_qn, b_kn_w=g_kn, b_w_out=done["b", "out"].reshape(b_w_out.shape),
                 final_norm_w=g_fin)
    names = list(weights)
    upd = {n: adamw(weights[n], grads[n], m_in[n], v_in[n], "adamw_" + n) for n in names}
    return (loss.reshape(()), dx.reshape(x.shape), *[grads[n] for n in names], *[upd[n][0] for n in names],
            *[upd[n][1] for n in names], *[upd[n][2] for n in names])
```

```python
import functools

import jax
import jax.numpy as jnp
from jax import lax
from jax.experimental import pallas as pl
from jax.experimental.pallas import tpu as pltpu

f32, bf16 = jnp.float32, jnp.bfloat16
HI = lax.Precision.HIGHEST
MESH = pl.DeviceIdType.MESH

EPS = 1e-6
D = 1024
CHUNK = 64
GQK_H, GV_H, GHD = 8, 16, 128
G_CONV = 4096
G_Z0 = 4096
G_BA0 = 6144
G_IN, G_INP = 6176, 6272
CONV_K = 4
F_H, F_HD = 16, 64
F_F0 = 4096
F_IN, F_INP = 4112, 4224
LR, B1, B2, AEPS, WD, STEP = 0.001, 0.9, 0.999, 1e-08, 0.01, 10
NEG = -1e30
VMEM_LIMIT = 56 * 1024 * 1024


def _nn(a, b, prec=None):
    return lax.dot_general(a, b, (((1,), (0,)), ((), ())), preferred_element_type=f32, precision=prec)


def _nt(a, b, prec=None):
    return lax.dot_general(a, b, (((1,), (1,)), ((), ())), preferred_element_type=f32, precision=prec)


def _tn(a, b, prec=None):
    return lax.dot_general(a, b, (((0,), (0,)), ((), ())), preferred_element_type=f32, precision=prec)


def _iota(shape, axis):
    return lax.broadcasted_iota(jnp.int32, shape, axis)


def _sigmoid(x):
    return 0.5 * jnp.tanh(0.5 * x) + 0.5


def _softplus(x):
    return jnp.maximum(x, 0.0) + jnp.log(1.0 + jnp.exp(-jnp.abs(x)))


def _silu(x):
    return x * _sigmoid(x)


def _dsilu(x):
    s = _sigmoid(x)
    return s * (1.0 + x * (1.0 - s))


def _params(sem=None, vmem=VMEM_LIMIT):
    return pltpu.CompilerParams(dimension_semantics=sem, vmem_limit_bytes=vmem)


def _row(v):
    return v.reshape(1, -1)


class _Comm:
    def __init__(self, ins, out_shapes, sems, start, wait):
        self.ins, self.out_shapes, self.sems, self.start, self.wait = list(ins), list(out_shapes), list(sems), start, wait


def _call(body, *, name, grid, in_specs, out_specs, out_shape, scratch_shapes, sem, args, comm=None, prefetch=()):
    n_pf, n_in, n_out, n_s = len(prefetch), len(in_specs), len(out_specs), len(scratch_shapes)
    n_ci, n_co = (len(comm.ins), len(comm.out_shapes)) if comm is not None else (0, 0)

    def wrapped(*refs):
        pf, refs = refs[:n_pf], refs[n_pf:]
        core_in, c_in = refs[:n_in], refs[n_in:n_in + n_ci]
        o0 = n_in + n_ci
        core_out, c_out = refs[o0:o0 + n_out], refs[o0 + n_out:o0 + n_out + n_co]
        s0 = o0 + n_out + n_co
        core_s, c_sem = refs[s0:s0 + n_s], refs[s0 + n_s:]
        if comm is not None:
            first = functools.reduce(jnp.logical_and, [pl.program_id(d) == 0 for d in range(len(grid))])
            pl.when(first)(functools.partial(comm.start, c_in, c_out, c_sem))
        body(*pf, *core_in, *core_out, *core_s)
        if comm is not None:
            last = functools.reduce(jnp.logical_and, [pl.program_id(d) == grid[d] - 1 for d in range(len(grid))])
            pl.when(last)(functools.partial(comm.wait, c_in, c_out, c_sem))

    extra = ([], [], [], []) if comm is None else ([_ANY] * n_ci, [_ANY] * n_co, comm.out_shapes, comm.sems)
    spec = pltpu.PrefetchScalarGridSpec(
        num_scalar_prefetch=n_pf, grid=grid, in_specs=list(in_specs) + extra[0], out_specs=list(out_specs) + extra[1],
        scratch_shapes=list(scratch_shapes) + extra[3])
    outs = pl.pallas_call(
        wrapped, name=name if comm is None else name + "_x", grid_spec=spec, out_shape=list(out_shape) + extra[2],
        compiler_params=_params(sem if comm is None else ("arbitrary",) * len(grid)),
    )(*prefetch, *args, *(comm.ins if comm is not None else []))
    return outs[:n_out], outs[n_out:]


def _pick(n, pref):
    for t in pref:
        if n % t == 0:
            return t
    return n


MM_VMEM_BUDGET = 44 * 1024 * 1024


def _mm_tiles(M, N, K):
    best = None
    for tk in [K] + [t for t in (2048, 1408, 1024, 896, 512, 384, 256, 128) if K % t == 0 and t < K]:
        for tm in (2048, 1024, 512, 256, 128):
            for tn in (1408, 1024, 896, 512, 384, 256, 128):
                if M % tm or N % tn:
                    continue
                nk = K // tk
                need = 2 * 2 * (tm * tk + tk * tn) + 2 * 4 * tm * tn + (4 * tm * tn if nk > 1 else 0)
                if need <= MM_VMEM_BUDGET:
                    cand = ((nk, -tm * tn), (tm, tn, tk))
                    best = cand if best is None or cand[0] < best[0] else best
    return best[1]


def matmul(a, b, mode, name, out_dtype=f32, comm=None):
    if mode == "nn":
        (M, K), (_, N) = a.shape, b.shape
    elif mode == "nt":
        (M, K), (N, _) = a.shape, b.shape
    else:
        (K, M), (_, N) = a.shape, b.shape
    tm, tn, tk = _mm_tiles(M, N, K)
    nk = K // tk
    dot = {"nn": _nn, "nt": _nt, "tn": _tn}[mode]

    def body(a_ref, b_ref, o_ref, *acc):
        k = pl.program_id(2)
        part = dot(a_ref[...], b_ref[...])
        if nk == 1:
            o_ref[...] = part.astype(out_dtype)
        else:
            acc_ref = acc[0]

            @pl.when(k == 0)
            def _():
                acc_ref[...] = part

            @pl.when(k > 0)
            def _():
                acc_ref[...] += part

            @pl.when(k == nk - 1)
            def _():
                o_ref[...] = acc_ref[...].astype(out_dtype)

    a_spec = pl.BlockSpec((tk, tm), lambda i, j, k: (k, i)) if mode == "tn" else pl.BlockSpec((tm, tk), lambda i, j, k: (i, k))
    b_spec = pl.BlockSpec((tn, tk), lambda i, j, k: (j, k)) if mode == "nt" else pl.BlockSpec((tk, tn), lambda i, j, k: (k, j))
    outs, got = _call(
        body, name=name, grid=(M // tm, N // tn, nk),
        in_specs=[a_spec, b_spec], out_specs=[pl.BlockSpec((tm, tn), lambda i, j, k: (i, j))],
        out_shape=[jax.ShapeDtypeStruct((M, N), out_dtype)],
        scratch_shapes=[] if nk == 1 else [pltpu.VMEM((tm, tn), f32)],
        sem=("parallel", "parallel", "arbitrary"), args=(a, b), comm=comm)
    return outs[0] if comm is None else (outs[0], got)


def out_proj(o2, w, x, gate, name):
    S, K = o2.shape
    N = w.shape[1]
    tm, tn = 1024, 1024

    def body(a_ref, b_ref, x_ref, g_ref, y_ref, xn_ref):
        y = _nn(a_ref[...], b_ref[...])
        y_ref[...] = y
        xn_ref[...] = x_ref[...] + g_ref[...] * y

    return pl.pallas_call(
        body, name=name, grid=(S // tm, N // tn),
        in_specs=[pl.BlockSpec((tm, K), lambda i, j: (i, 0)), pl.BlockSpec((K, tn), lambda i, j: (0, j)),
                  pl.BlockSpec((tm, tn), lambda i, j: (i, j)), pl.BlockSpec((1, tn), lambda i, j: (0, j))],
        out_specs=[pl.BlockSpec((tm, tn), lambda i, j: (i, j))] * 2,
        out_shape=[jax.ShapeDtypeStruct((S, N), f32)] * 2,
        compiler_params=_params(("parallel", "parallel")),
    )(o2, w, x, gate)


def ln_mod(x, nw, scale, shift):
    S = x.shape[0]
    tm = 512

    def body(x_ref, nw_ref, sc_ref, sh_ref, h_ref):
        xv = x_ref[...]
        r = lax.rsqrt(jnp.mean(xv * xv, axis=-1, keepdims=True) + EPS)
        h_ref[...] = ((xv * r) * nw_ref[...] * (1.0 + sc_ref[...]) + sh_ref[...]).astype(bf16)

    vec = pl.BlockSpec((1, D), lambda i: (0, 0))
    return pl.pallas_call(
        body, name="ln_mod", grid=(S // tm,),
        in_specs=[pl.BlockSpec((tm, D), lambda i: (i, 0)), vec, vec, vec],
        out_specs=pl.BlockSpec((tm, D), lambda i: (i, 0)),
        out_shape=jax.ShapeDtypeStruct((S, D), bf16),
        compiler_params=_params(("parallel",)),
    )(x, nw, scale, shift)


def ln_mod_bwd(x, nw, scale, dh, dxres):
    S = x.shape[0]
    tm = 512
    nb = S // tm

    def body(x_ref, nw_ref, sc_ref, dh_ref, dr_ref, dx_ref, st_ref):
        i = pl.program_id(0)
        xv = x_ref[...]
        r = lax.rsqrt(jnp.mean(xv * xv, axis=-1, keepdims=True) + EPS)
        xn = xv * r
        dh = dh_ref[...]
        dxn = dh * (nw_ref[...] * (1.0 + sc_ref[...]))
        dx_ref[...] = dr_ref[...] + r * (dxn - xn * jnp.mean(dxn * xn, axis=-1, keepdims=True))
        p1 = jnp.sum(dh * xn, axis=0, keepdims=True)
        p2 = jnp.sum(dh, axis=0, keepdims=True)
        upd = jnp.concatenate([p1, p1, p2, jnp.zeros((5, D), f32)], axis=0)

        @pl.when(i == 0)
        def _():
            st_ref[...] = upd

        @pl.when(i > 0)
        def _():
            st_ref[...] += upd

        @pl.when(i == nb - 1)
        def _():
            st_ref[0:1, :] = st_ref[0:1, :] * (1.0 + sc_ref[...])
            st_ref[1:2, :] = st_ref[1:2, :] * nw_ref[...]

    vec = pl.BlockSpec((1, D), lambda i: (0, 0))
    tile = pl.BlockSpec((tm, D), lambda i: (i, 0))
    return pl.pallas_call(
        body, name="ln_mod_bwd", grid=(S // tm,),
        in_specs=[tile, vec, vec, tile, tile],
        out_specs=[tile, pl.BlockSpec((8, D), lambda i: (0, 0))],
        out_shape=[jax.ShapeDtypeStruct((S, D), f32), jax.ShapeDtypeStruct((8, D), f32)],
        compiler_params=_params(("arbitrary",)),
    )(x, nw, scale, dh, dxres)


def final_loss(x, fw, tgt):
    S = x.shape[0]
    tm = 512

    def body(x_ref, w_ref, t_ref, dx_ref, st_ref):
        i = pl.program_id(0)
        xv = x_ref[...]
        r = lax.rsqrt(jnp.mean(xv * xv, axis=-1, keepdims=True) + EPS)
        xn = xv * r
        err = xn * w_ref[...] - t_ref[...]
        dy = err * (1.0 / D)
        dxn = dy * w_ref[...]
        dx_ref[...] = r * (dxn - xn * jnp.mean(dxn * xn, axis=-1, keepdims=True))
        p1 = jnp.sum(dy * xn, axis=0, keepdims=True)
        p2 = jnp.sum(err * err, axis=0, keepdims=True) * (0.5 / D)
        upd = jnp.concatenate([p1, p2, jnp.zeros((6, D), f32)], axis=0)

        @pl.when(i == 0)
        def _():
            st_ref[...] = upd

        @pl.when(i > 0)
        def _():
            st_ref[...] += upd

    tile = pl.BlockSpec((tm, D), lambda i: (i, 0))
    return pl.pallas_call(
        body, name="final_loss", grid=(S // tm,),
        in_specs=[tile, pl.BlockSpec((1, D), lambda i: (0, 0)), tile],
        out_specs=[tile, pl.BlockSpec((8, D), lambda i: (0, 0))],
        out_shape=[jax.ShapeDtypeStruct((S, D), f32), jax.ShapeDtypeStruct((8, D), f32)],
        compiler_params=_params(("arbitrary",)),
    )(x, fw, tgt)


def gate_bwd(dx, y, gate):
    S = dx.shape[0]
    tm = 512

    def body(dx_ref, y_ref, g_ref, dy_ref, st_ref):
        i = pl.program_id(0)
        dxv = dx_ref[...]
        dy_ref[...] = (g_ref[...] * dxv).astype(bf16)
        upd = jnp.concatenate([jnp.sum(dxv * y_ref[...], axis=0, keepdims=True), jnp.zeros((7, D), f32)], axis=0)

        @pl.when(i == 0)
        def _():
            st_ref[...] = upd

        @pl.when(i > 0)
        def _():
            st_ref[...] += upd

    tile = pl.BlockSpec((tm, D), lambda i: (i, 0))
    return pl.pallas_call(
        body, name="gate_bwd", grid=(S // tm,),
        in_specs=[tile, tile, pl.BlockSpec((1, D), lambda i: (0, 0))],
        out_specs=[tile, pl.BlockSpec((8, D), lambda i: (0, 0))],
        out_shape=[jax.ShapeDtypeStruct((S, D), bf16), jax.ShapeDtypeStruct((8, D), f32)],
        compiler_params=_params(("arbitrary",)),
    )(dx, y, gate)


def _chunk_mats(tm):
    r, c = _iota((tm, tm), 0), _iota((tm, tm), 1)
    same = jnp.right_shift(r, 6) == jnp.right_shift(c, 6)
    ltri = jnp.where(same & (c <= r), 1.0, 0.0).astype(f32)
    utri = jnp.where(same & (c >= r), 1.0, 0.0).astype(f32)
    bsame = jnp.where(same, 1.0, 0.0).astype(f32)
    return ltri, utri, bsame


def _gdn_scalars(ba, alog, dtb, ltri, bsame):
    beta = _sigmoid(ba[:, 0:16])
    u = ba[:, 16:32] + dtb
    neg_a = -jnp.exp(alog)
    g = neg_a * _softplus(u)
    gc = _nn(ltri, g, HI)
    glast = _nn(bsame, g, HI)
    return beta, u, neg_a, g, gc, glast


def _conv_taps(p_ref, halo_ref, first, gi, ext_scr):
    cs = slice(gi * 128, (gi + 1) * 128)
    tm = p_ref.shape[0]
    cur = p_ref[:, cs]
    ext_scr[gi, 0:8, :] = jnp.where(first, 0.0, halo_ref[:, cs])
    ext_scr[gi, 8:, :] = cur
    return [cur] + [ext_scr[gi, 8 - s:8 - s + tm, :] for s in range(1, CONV_K)]


def _conv_scratch(tm):
    return pltpu.VMEM((G_CONV // 128, tm + 8, 128), f32)


def _conv(taps, w):
    cv = taps[0] * w[3:4]
    for s in range(1, CONV_K):
        cv = cv + taps[s] * w[3 - s:4 - s]
    return cv


def _l2n(x):
    return x * lax.rsqrt(jnp.sum(x * x, axis=-1, keepdims=True) + EPS)


def _gdn_in_specs(tm, S):
    nb8 = tm // 8
    return [pl.BlockSpec((tm, G_CONV), lambda i: (i, 0)),
            pl.BlockSpec((8, G_CONV), lambda i: (jnp.maximum(i * nb8 - 1, 0), 0)),
            pl.BlockSpec((tm, 128), lambda i: (i, G_BA0 // 128))]


def gdn_pre(proj, conv_w, alog, dtb):
    S = proj.shape[0]
    tm = 256
    nch = tm // CHUNK

    def body(p_ref, halo_ref, ba_ref, w_ref, al_ref, dt_ref,
             q_ref, k_ref, kb_ref, kbg_ref, vb_ref, qd_ref, kd_ref, d_ref, gl_ref, ext_scr):
        first = pl.program_id(0) == 0
        ltri, _, bsame = _chunk_mats(tm)
        beta, _, _, _, gc, glast = _gdn_scalars(ba_ref[...], al_ref[...], dt_ref[...], ltri, bsame)
        eg, ek, egl = jnp.exp(gc), jnp.exp(glast - gc), jnp.exp(glast)
        eye = jnp.where(_iota((16, 16), 0) == _iota((16, 16), 1), 1.0, 0.0).astype(f32)
        gct = _nt(eye, gc, HI)
        low = _iota((CHUNK, CHUNK), 0) >= _iota((CHUNK, CHUNK), 1)

        def act(gi):
            return _silu(_conv(_conv_taps(p_ref, halo_ref, first, gi, ext_scr), w_ref[:, gi * 128:(gi + 1) * 128]))

        for j in range(GQK_H):
            js = slice(j * 128, (j + 1) * 128)
            qn = _l2n(act(j)) * (GHD ** -0.5)
            kn = _l2n(act(GQK_H + j))
            q_ref[:, js] = qn.astype(bf16)
            k_ref[:, js] = kn.astype(bf16)
            for e in range(2):
                h = 2 * j + e
                hs = slice(h * 128, (h + 1) * 128)
                v = act(2 * GQK_H + h)
                bh, egh, ekh = beta[:, h:h + 1], eg[:, h:h + 1], ek[:, h:h + 1]
                kbv = kn * bh
                kb_ref[:, hs] = kbv.astype(bf16)
                kbg_ref[:, hs] = (kbv * egh).astype(bf16)
                vb_ref[:, hs] = (v * bh).astype(bf16)
                qd_ref[:, hs] = (qn * egh).astype(bf16)
                kd_ref[:, hs] = (kn * ekh).astype(bf16)
                for c in range(nch):
                    rs = slice(c * CHUNK, (c + 1) * CHUNK)
                    diff = gc[rs, h:h + 1] - gct[h:h + 1, rs]
                    d_ref[rs, h * CHUNK:(h + 1) * CHUNK] = jnp.where(low, jnp.exp(jnp.where(low, diff, 0.0)), 0.0)
                    gl_ref[c * 8:(c + 1) * 8, hs] = jnp.broadcast_to(egl[c * CHUNK:c * CHUNK + 8, h:h + 1], (8, 128))

    full = lambda shape: pl.BlockSpec(shape, lambda i: (0, 0))
    t1 = pl.BlockSpec((tm, 1024), lambda i: (i, 0))
    t2 = pl.BlockSpec((tm, 2048), lambda i: (i, 0))
    sd = jax.ShapeDtypeStruct
    return pl.pallas_call(
        body, name="gdn_pre", grid=(S // tm,),
        in_specs=_gdn_in_specs(tm, S) + [full((CONV_K, G_CONV)), full((1, 16)), full((1, 16))],
        out_specs=[t1, t1, t2, t2, t2, t2, t2, t1, pl.BlockSpec((tm // 8, 2048), lambda i: (i, 0))],
        out_shape=[sd((S, 1024), bf16)] * 2 + [sd((S, 2048), bf16)] * 5 + [sd((S, 1024), f32), sd((S // 8, 2048), f32)],
        scratch_shapes=[_conv_scratch(tm)],
        compiler_params=_params(("parallel",)),
    )(proj, proj, proj, conv_w, alog, dtb)


def _bnn(a, b):
    return lax.dot_general(a, b, (((2,), (1,)), ((0,), (0,))), preferred_element_type=f32)


def _bnt(a, b):
    return lax.dot_general(a, b, (((2,), (2,)), ((0,), (0,))), preferred_element_type=f32)


def _btn(a, b):
    return lax.dot_general(a, b, (((1,), (1,)), ((0,), (0,))), preferred_element_type=f32)


def _split(a):
    hi = a.astype(bf16)
    return hi, (a - hi.astype(f32)).astype(bf16)


def _cat3(h, l, axis, lhs):
    return jnp.concatenate([h, h, l] if lhs else [h, l, h], axis=axis)


def _tri_inv_b(L):
    eye = jnp.where(_iota((1, CHUNK, CHUNK), 1) == _iota((1, CHUNK, CHUNK), 2), 1.0, 0.0).astype(f32)
    P = -L
    T = eye + P
    ph, pl_ = _split(P)
    for _ in range(5):
        P = _bnn(_cat3(ph, pl_, 2, True), _cat3(ph, pl_, 1, False))
        ph, pl_ = _split(P)
        th, tl = _split(T)
        T = T + _bnn(_cat3(th, tl, 2, True), _cat3(ph, pl_, 1, False))
    return T


GTB = 512
GQH_FWD, GQH_BWD = 1, 2


def _gdn_slices(ncb, gnv):
    pairs = [(c, e) for c in range(ncb) for e in range(gnv)]
    rs = lambda c: slice(c * CHUNK, (c + 1) * CHUNK)
    cs = lambda e: slice(e * 128, (e + 1) * 128)
    ds_ = lambda e: slice(e * CHUNK, (e + 1) * CHUNK)
    ks = lambda e: slice((e // 2) * 128, (e // 2 + 1) * 128)
    return pairs, rs, cs, ds_, ks


def gdn_fwd(q, k, kb, kbg, vb, qd, kd, dm, gl8, comm=None):
    S = q.shape[0]
    nb, ncb = S // GTB, GTB // CHUNK
    GQH, GNV = GQH_FWD, 2 * GQH_FWD
    pairs, rs, cs, ds_, ks = _gdn_slices(ncb, GNV)

    def body(q_ref, k_ref, kb_ref, kbg_ref, vb_ref, qd_ref, kd_ref, d_ref, gl_ref,
             o_ref, w_ref, at_ref, t_ref, vn_ref, st_ref, state, u_scr):
        @pl.when(pl.program_id(1) == 0)
        def _():
            state[...] = jnp.zeros_like(state)

        stk = lambda ref, lanes: jnp.stack([ref[rs(c), lanes(e)] for c, e in pairs])
        kq = stk(k_ref, ks)
        dmat = stk(d_ref, ds_)
        strict = _iota((1, CHUNK, CHUNK), 1) > _iota((1, CHUNK, CHUNK), 2)
        T = _tri_inv_b(jnp.where(strict, _bnt(stk(kb_ref, cs), kq) * dmat, 0.0))
        tb = T.astype(bf16)
        u_scr[...] = _bnn(tb, stk(vb_ref, cs))
        wb = _bnn(tb, stk(kbg_ref, cs)).astype(bf16)
        per_qk = lambda ref: jnp.stack([ref[rs(c), ks(e)] for c, e in pairs if e % 2 == 0])
        qk = _bnt(per_qk(q_ref), per_qk(k_ref))
        for b, (c, e) in enumerate(pairs):
            w_ref[rs(c), cs(e)] = wb[b]
            at_ref[rs(c), ds_(e)] = (qk[b // 2] * dmat[b]).astype(bf16)
            t_ref[rs(c), ds_(e)] = T[b]
        for b, (c, e) in enumerate(pairs):
            sb = state[e].astype(bf16)
            vnb = (u_scr[b] - _nn(w_ref[rs(c), cs(e)], sb)).astype(bf16)
            o_ref[rs(c), cs(e)] = _nn(qd_ref[rs(c), cs(e)], sb) + _nn(at_ref[rs(c), ds_(e)], vnb)
            st_ref[c * 128:(c + 1) * 128, cs(e)] = sb
            state[e] = state[e] * gl_ref[c * 8:c * 8 + 1, cs(e)] + _tn(kd_ref[rs(c), cs(e)], vnb)
            vn_ref[rs(c), cs(e)] = vnb

    b1 = pl.BlockSpec((GTB, 128 * GQH), lambda j, i: (i, j))
    b2 = pl.BlockSpec((GTB, 256 * GQH), lambda j, i: (i, j))
    sd = jax.ShapeDtypeStruct
    return _call(
        body, name="gdn_fwd", grid=(GQK_H // GQH, nb),
        in_specs=[b1, b1, b2, b2, b2, b2, b2, b1, pl.BlockSpec((GTB // 8, 256 * GQH), lambda j, i: (i, j))],
        out_specs=[b2, b2, b1, b1, b2, pl.BlockSpec((ncb * 128, 256 * GQH), lambda j, i: (i, j))],
        out_shape=[sd((S, 2048), f32), sd((S, 2048), bf16), sd((S, 1024), bf16), sd((S, 1024), f32),
                   sd((S, 2048), bf16), sd((S // CHUNK * 128, 2048), bf16)],
        scratch_shapes=[pltpu.VMEM((GNV, 128, 128), f32), pltpu.VMEM((GNV * ncb, CHUNK, 128), f32)],
        sem=("parallel", "arbitrary"), args=(q, k, kb, kbg, vb, qd, kd, dm, gl8), comm=comm)


def gdn_bwd(do, q, k, kb, kbg, vb, qd, kd, dm, gl8, w, at, T, vn, st, comm=None):
    S = q.shape[0]
    nb, ncb = S // GTB, GTB // CHUNK
    GQH, GNV = GQH_BWD, 2 * GQH_BWD
    pairs, rs, cs, ds_, ks = _gdn_slices(ncb, GNV)

    def body(do_ref, q_ref, k_ref, kb_ref, kbg_ref, vb_ref, qd_ref, kd_ref, d_ref, gl_ref, w_ref, at_ref, t_ref, vn_ref, st_ref,
             dq_ref, dk_ref, dkb_ref, dkbg_ref, dvb_ref, dqd_ref, dkd_ref, dgc_ref, dstate, dvn_scr, dw_scr, dat_scr, dgl_scr):
        @pl.when(pl.program_id(1) == 0)
        def _():
            dstate[...] = jnp.zeros_like(dstate)

        for b, (c, e) in reversed(list(enumerate(pairs))):
            dob = do_ref[rs(c), cs(e)].astype(bf16)
            sb = st_ref[c * 128:(c + 1) * 128, cs(e)]
            vnb = vn_ref[rs(c), cs(e)]
            gl = gl_ref[c * 8:c * 8 + 1, cs(e)]
            dS = dstate[e]
            dsb = dS.astype(bf16)
            dvnb = (_tn(at_ref[rs(c), ds_(e)], dob) + _nn(kd_ref[rs(c), cs(e)], dsb)).astype(bf16)
            dvn_scr[b] = dvnb
            dat_scr[b] = _nt(dob, vnb)
            dqd_ref[rs(c), cs(e)] = _nt(dob, sb)
            dkd_ref[rs(c), cs(e)] = _nt(vnb, dsb)
            dw_scr[b] = (-_nt(dvnb, sb)).astype(bf16)
            dgl = jnp.sum(jnp.sum(dS * sb.astype(f32), axis=1, keepdims=True), axis=0, keepdims=True)
            dgl_scr[b] = jnp.broadcast_to(dgl * gl, (8, 128))
            dstate[e] = gl * dS + _tn(qd_ref[rs(c), cs(e)], dob) - _tn(w_ref[rs(c), cs(e)], dvnb)

        stk = lambda ref, lanes: jnp.stack([ref[rs(c), lanes(e)] for c, e in pairs])
        kq, qq = stk(k_ref, ks), stk(q_ref, ks)
        kbb = stk(kb_ref, cs)
        Tm = stk(t_ref, ds_)
        tb = Tm.astype(bf16)
        dvn, dw = dvn_scr[...], dw_scr[...]
        dT = _bnt(dvn, stk(vb_ref, cs)) + _bnt(dw, stk(kbg_ref, cs))
        dvb, dkbg = _btn(tb, dvn), _btn(tb, dw)
        th, tl = _split(Tm)
        xh, xl = _split(_bnt(_cat3(*_split(dT), 2, True), _cat3(th, tl, 2, False)))
        dL = -_btn(_cat3(th, tl, 1, True), _cat3(xh, xl, 1, False))
        dmat = stk(d_ref, ds_)
        strict = _iota((1, CHUNK, CHUNK), 1) > _iota((1, CHUNK, CHUNK), 2)
        dA = jnp.where(strict, dL * dmat, 0.0)
        dB = dat_scr[...] * dmat
        dAb, dBb = dA.astype(bf16), dB.astype(bf16)
        dkb = _bnn(dAb, kq)
        dkc = _btn(dAb, kbb) + _btn(dBb, qq)
        dqc = _bnn(dBb, kq)
        M = dA * _bnt(kbb, kq) + dB * _bnt(qq, kq)
        mh, ml = _split(M)
        colsum = _btn(jnp.concatenate([mh, ml], axis=1), jnp.ones((GNV * ncb, 2 * CHUNK, 128), bf16))
        lastrow = _iota((1, CHUNK, 128), 1) == CHUNK - 1
        for b, (c, e) in enumerate(pairs):
            dvb_ref[rs(c), cs(e)] = dvb[b]
            dkbg_ref[rs(c), cs(e)] = dkbg[b]
            dkb_ref[rs(c), cs(e)] = dkb[b]
            dgc_ref[rs(c), cs(e)] = (jnp.sum(M[b], axis=1, keepdims=True) - colsum[b]
                                     + jnp.where(lastrow[0], dgl_scr[b][0:1, :], 0.0))
        for b, (c, e) in enumerate(pairs):
            if e % 2 == 0:
                dq_ref[rs(c), ks(e)] = dqc[b] + dqc[b + 1]
                dk_ref[rs(c), ks(e)] = dkc[b] + dkc[b + 1]

    b1 = pl.BlockSpec((GTB, 128 * GQH), lambda j, i: (nb - 1 - i, j))
    b2 = pl.BlockSpec((GTB, 256 * GQH), lambda j, i: (nb - 1 - i, j))
    sd = jax.ShapeDtypeStruct
    return _call(
        body, name="gdn_bwd", grid=(GQK_H // GQH, nb),
        in_specs=[b2, b1, b1, b2, b2, b2, b2, b2, b1, pl.BlockSpec((GTB // 8, 256 * GQH), lambda j, i: (nb - 1 - i, j)),
                  b2, b1, b1, b2, pl.BlockSpec((ncb * 128, 256 * GQH), lambda j, i: (nb - 1 - i, j))],
        out_specs=[b1, b1, b2, b2, b2, b2, b2, b2],
        out_shape=[sd((S, 1024), f32)] * 2 + [sd((S, 2048), f32)] * 6,
        scratch_shapes=[pltpu.VMEM((GNV, 128, 128), f32), pltpu.VMEM((GNV * ncb, CHUNK, 128), bf16),
                        pltpu.VMEM((GNV * ncb, CHUNK, 128), bf16), pltpu.VMEM((GNV * ncb, CHUNK, CHUNK), f32),
                        pltpu.VMEM((GNV * ncb, 8, 128), f32)],
        sem=("parallel", "arbitrary"), args=(do, q, k, kb, kbg, vb, qd, kd, dm, gl8, w, at, T, vn, st), comm=comm)


def gdn_onorm(o, proj, nw):
    S = o.shape[0]
    tm = 256

    def body(o_ref, z_ref, nw_ref, o2_ref):
        for h in range(GV_H):
            hs = slice(h * 128, (h + 1) * 128)
            oh = o_ref[:, hs]
            r = lax.rsqrt(jnp.mean(oh * oh, axis=-1, keepdims=True) + EPS)
            o2_ref[:, hs] = (((oh * r) * nw_ref[...]) * _silu(z_ref[:, hs])).astype(bf16)

    t2 = pl.BlockSpec((tm, 2048), lambda i: (i, 0))
    return pl.pallas_call(
        body, name="gdn_onorm", grid=(S // tm,),
        in_specs=[t2, pl.BlockSpec((tm, 2048), lambda i: (i, G_Z0 // 2048)), pl.BlockSpec((1, 128), lambda i: (0, 0))],
        out_specs=t2, out_shape=jax.ShapeDtypeStruct((S, 2048), bf16),
        compiler_params=_params(("parallel",)),
    )(o, proj, nw)


def gdn_onorm_bwd(do2, o, proj, nw):
    S = o.shape[0]
    tm = 256

    def body(d_ref, o_ref, z_ref, nw_ref, do_ref, dz_ref, st_ref):
        i = pl.program_id(0)
        acc = jnp.zeros((1, 128), f32)
        for h in range(GV_H):
            hs = slice(h * 128, (h + 1) * 128)
            oh, z, d2 = o_ref[:, hs], z_ref[:, hs], d_ref[:, hs]
            r = lax.rsqrt(jnp.mean(oh * oh, axis=-1, keepdims=True) + EPS)
            on = oh * r
            dt = d2 * _silu(z)
            dz_ref[:, hs] = (d2 * (on * nw_ref[...]) * _dsilu(z)).astype(bf16)
            don = dt * nw_ref[...]
            acc = acc + jnp.sum(dt * on, axis=0, keepdims=True)
            do_ref[:, hs] = r * (don - on * jnp.mean(don * on, axis=-1, keepdims=True))
        upd = jnp.concatenate([acc, jnp.zeros((7, 128), f32)], axis=0)

        @pl.when(i == 0)
        def _():
            st_ref[...] = upd

        @pl.when(i > 0)
        def _():
            st_ref[...] += upd

    t2 = pl.BlockSpec((tm, 2048), lambda i: (i, 0))
    sd = jax.ShapeDtypeStruct
    return pl.pallas_call(
        body, name="gdn_onorm_bwd", grid=(S // tm,),
        in_specs=[t2, t2, pl.BlockSpec((tm, 2048), lambda i: (i, G_Z0 // 2048)), pl.BlockSpec((1, 128), lambda i: (0, 0))],
        out_specs=[t2, t2, pl.BlockSpec((8, 128), lambda i: (0, 0))],
        out_shape=[sd((S, 2048), f32), sd((S, 2048), bf16), sd((8, 128), f32)],
        compiler_params=_params(("arbitrary",)),
    )(do2, o, proj, nw)


def gdn_pre_bwd(proj, conv_w, alog, dtb, dq, dk, dkb, dkbg, dvb, dqd, dkd, dgcd):
    S = proj.shape[0]
    tm = 128

    def body(p_ref, halo_ref, ba_ref, w_ref, al_ref, dt_ref, dq_ref, dk_ref, dkb_ref, dkbg_ref, dvb_ref, dqd_ref, dkd_ref, dgc_ref,
             dcv_ref, dba_ref, st_ref, ext_scr):
        i = pl.program_id(0)
        first = i == 0
        ltri, utri, bsame = _chunk_mats(tm)
        beta, u, neg_a, g, gc, glast = _gdn_scalars(ba_ref[...], al_ref[...], dt_ref[...], ltri, bsame)
        eg, ek = jnp.exp(gc), jnp.exp(glast - gc)
        lane16 = _iota((tm, 16), 1)
        dgc_all = jnp.zeros((tm, 16), f32)
        rkd_all = jnp.zeros((tm, 16), f32)
        dbeta_all = jnp.zeros((tm, 16), f32)

        def pre(gi):
            return _conv(_conv_taps(p_ref, halo_ref, first, gi, ext_scr), w_ref[:, gi * 128:(gi + 1) * 128])

        def l2n_bwd(xt, dy):
            r = lax.rsqrt(jnp.sum(xt * xt, axis=-1, keepdims=True) + EPS)
            y = xt * r
            return r * (dy - y * jnp.sum(dy * y, axis=-1, keepdims=True))

        for j in range(GQK_H):
            js = slice(j * 128, (j + 1) * 128)
            cvq, cvk = pre(j), pre(GQK_H + j)
            qt, kt = _silu(cvq), _silu(cvk)
            qn = _l2n(qt) * (GHD ** -0.5)
            kn = _l2n(kt)
            dq_tot, dk_tot = dq_ref[:, js], dk_ref[:, js]
            for e in range(2):
                h = 2 * j + e
                hs = slice(h * 128, (h + 1) * 128)
                gv = 2 * GQK_H + h
                cvv = pre(gv)
                v = _silu(cvv)
                bh, egh, ekh = beta[:, h:h + 1], eg[:, h:h + 1], ek[:, h:h + 1]
                dkbg, dkd, dqd, dvb = dkbg_ref[:, hs], dkd_ref[:, hs], dqd_ref[:, hs], dvb_ref[:, hs]
                dkb_t = dkb_ref[:, hs] + dkbg * egh
                dk_tot = dk_tot + dkb_t * bh + dkd * ekh
                dq_tot = dq_tot + dqd * egh
                dcv_ref[:, gv * 128:(gv + 1) * 128] = (dvb * bh) * _dsilu(cvv)
                dbeta = jnp.sum(dkb_t * kn, axis=-1, keepdims=True) + jnp.sum(dvb * v, axis=-1, keepdims=True)
                rkd = jnp.sum(dkd * (kn * ekh), axis=-1, keepdims=True)
                dgc = (dgc_ref[:, hs][:, 0:1] + jnp.sum(dkbg * (kn * bh * egh), axis=-1, keepdims=True)
                       + jnp.sum(dqd * (qn * egh), axis=-1, keepdims=True) - rkd)
                sel = lane16 == h
                dgc_all = dgc_all + jnp.where(sel, dgc, 0.0)
                rkd_all = rkd_all + jnp.where(sel, rkd, 0.0)
                dbeta_all = dbeta_all + jnp.where(sel, dbeta, 0.0)
            dcv_ref[:, js] = l2n_bwd(qt, dq_tot * (GHD ** -0.5)) * _dsilu(cvq)
            ks = slice((GQK_H + j) * 128, (GQK_H + j + 1) * 128)
            dcv_ref[:, ks] = l2n_bwd(kt, dk_tot) * _dsilu(cvk)

        islast = jnp.bitwise_and(_iota((tm, 16), 0), CHUNK - 1) == CHUNK - 1
        dgc_all = dgc_all + jnp.where(islast, _nn(bsame, rkd_all, HI), 0.0)
        dg = _nn(utri, dgc_all, HI)
        da = dg * neg_a * _sigmoid(u)
        db = dbeta_all * beta * (1.0 - beta)
        r16, c128 = _iota((16, 128), 0), _iota((16, 128), 1)
        pb = jnp.where(c128 == r16, 1.0, 0.0).astype(f32)
        pa = jnp.where(c128 == r16 + 16, 1.0, 0.0).astype(f32)
        dba_ref[...] = _nn(db, pb, HI) + _nn(da, pa, HI)
        upd = jnp.concatenate([jnp.sum(dg * g, axis=0, keepdims=True), jnp.sum(da, axis=0, keepdims=True),
                               jnp.zeros((6, 16), f32)], axis=0)

        @pl.when(i == 0)
        def _():
            st_ref[...] = upd

        @pl.when(i > 0)
        def _():
            st_ref[...] += upd

    full = lambda shape: pl.BlockSpec(shape, lambda i: (0, 0))
    t1 = pl.BlockSpec((tm, 1024), lambda i: (i, 0))
    t2 = pl.BlockSpec((tm, 2048), lambda i: (i, 0))
    sd = jax.ShapeDtypeStruct
    return pl.pallas_call(
        body, name="gdn_pre_bwd", grid=(S // tm,),
        in_specs=_gdn_in_specs(tm, S) + [full((CONV_K, G_CONV)), full((1, 16)), full((1, 16)), t1, t1] + [t2] * 6,
        out_specs=[pl.BlockSpec((tm, G_CONV), lambda i: (i, 0)), pl.BlockSpec((tm, 128), lambda i: (i, 0)), full((8, 16))],
        out_shape=[sd((S, G_CONV), f32), sd((S, 128), f32), sd((8, 16), f32)],
        scratch_shapes=[_conv_scratch(tm)],
        compiler_params=_params(("arbitrary",)),
    )(proj, proj, proj, conv_w, alog, dtb, dq, dk, dkb, dkbg, dvb, dqd, dkd, dgcd)


def gdn_conv_bwd(proj, conv_w, dcv, dz, dba):
    S = proj.shape[0]
    tm = 256
    nb, nb8 = S // tm, tm // 8

    def body(p_ref, halo_ref, w_ref, dcv_ref, nxt_ref, dz_ref, dba_ref, dp_ref, dw_ref, ext_scr, nxt_scr):
        i = pl.program_id(0)
        first, last = i == 0, i == nb - 1
        for gi in range(G_CONV // 128):
            cs = slice(gi * 128, (gi + 1) * 128)
            taps = _conv_taps(p_ref, halo_ref, first, gi, ext_scr)
            cur = dcv_ref[:, cs]
            nxt_scr[gi, 0:tm, :] = cur
            nxt_scr[gi, tm:, :] = jnp.where(last, 0.0, nxt_ref[:, cs])
            w = w_ref[:, cs]
            dp = cur * w[3:4]
            rows = [jnp.sum(cur * taps[3 - kk], axis=0, keepdims=True) for kk in range(CONV_K)]
            for s in range(1, CONV_K):
                dp = dp + nxt_scr[gi, s:s + tm, :] * w[3 - s:4 - s]
            dp_ref[:, cs] = dp.astype(bf16)
            upd = jnp.concatenate(rows + [jnp.zeros((4, 128), f32)], axis=0)

            @pl.when(first)
            def _():
                dw_ref[:, cs] = upd

            @pl.when(i > 0)
            def _():
                dw_ref[:, cs] += upd

        dp_ref[:, G_Z0:G_BA0] = dz_ref[...]
        dp_ref[:, G_BA0:G_INP] = dba_ref[...].astype(bf16)

    sd = jax.ShapeDtypeStruct
    return pl.pallas_call(
        body, name="gdn_conv_bwd", grid=(nb,),
        in_specs=[pl.BlockSpec((tm, G_CONV), lambda i: (i, 0)),
                  pl.BlockSpec((8, G_CONV), lambda i: (jnp.maximum(i * nb8 - 1, 0), 0)),
                  pl.BlockSpec((CONV_K, G_CONV), lambda i: (0, 0)),
                  pl.BlockSpec((tm, G_CONV), lambda i: (i, 0)),
                  pl.BlockSpec((8, G_CONV), lambda i: (jnp.minimum((i + 1) * nb8, S // 8 - 1), 0)),
                  pl.BlockSpec((tm, 2048), lambda i: (i, 0)), pl.BlockSpec((tm, 128), lambda i: (i, 0))],
        out_specs=[pl.BlockSpec((tm, G_INP), lambda i: (i, 0)), pl.BlockSpec((8, G_CONV), lambda i: (0, 0))],
        out_shape=[sd((S, G_INP), bf16), sd((8, G_CONV), f32)],
        scratch_shapes=[_conv_scratch(tm), _conv_scratch(tm)],
        compiler_params=_params(("arbitrary",)),
    )(proj, proj, conv_w, dcv, dcv, dz, dba)


def _half_mean(t, lo_half):
    m0 = jnp.sum(jnp.where(lo_half, t, 0.0), axis=-1, keepdims=True)
    m1 = jnp.sum(jnp.where(lo_half, 0.0, t), axis=-1, keepdims=True)
    return jnp.where(lo_half, m0, m1) * (1.0 / F_HD)


def _split3(c):
    hi = c.astype(bf16).astype(f32)
    mid = (c - hi).astype(bf16).astype(f32)
    lo = (c - hi - mid).astype(bf16).astype(f32)
    return hi, mid, lo


def fox_pre(proj, fbias, qw2, kw2):
    S = proj.shape[0]
    tm = 256

    def body(q_ref, k_ref, v_ref, f_ref, fb_ref, qw_ref, kw_ref, qa_ref, ka_ref, vb_ref, carry):
        @pl.when(pl.program_id(0) == 0)
        def _():
            carry[...] = jnp.zeros_like(carry)

        logf = -_softplus(-(f_ref[:, 0:16] + fb_ref[...]))
        ltri = jnp.where(_iota((tm, tm), 1) <= _iota((tm, tm), 0), 1.0, 0.0).astype(f32)
        cum = _nn(ltri, logf, HI) + carry[0:1, :]
        carry[0:1, :] = cum[tm - 1:tm, :]
        lane = _iota((tm, 128), 1)
        lo_half = lane < F_HD
        for p in range(F_H // 2):
            ps = slice(p * 128, (p + 1) * 128)
            for src, w_ref, dst, is_q in ((q_ref, qw_ref, qa_ref, True), (k_ref, kw_ref, ka_ref, False)):
                x = src[:, ps]
                xn = x * lax.rsqrt(_half_mean(x * x, lo_half) + EPS) * w_ref[...]
                if is_q:
                    xn = xn * (F_HD ** -0.5)
                for e in range(2):
                    h = 2 * p + e
                    base = xn if e == 0 else pltpu.roll(xn, F_HD, 1)
                    hi, mid, lo = _split3(cum[:, h:h + 1])
                    pieces = jnp.where(lane == 64, hi, 0.0) + jnp.where(lane == 65, mid, 0.0) + jnp.where(lane == 66, lo, 0.0)
                    if is_q:
                        ext = pieces + jnp.where((lane >= 67) & (lane <= 69), 1.0, 0.0)
                    else:
                        ext = jnp.where((lane >= 64) & (lane <= 66), 1.0, 0.0) - pltpu.roll(pieces, 3, 1)
                    dst[:, h * 128:(h + 1) * 128] = jnp.where(lo_half, base, ext).astype(bf16)
        one = jnp.where(lane == F_HD, 1.0, 0.0)
        for p in range(F_H // 2):
            vv = v_ref[:, p * 128:(p + 1) * 128]
            vb_ref[:, (2 * p) * 128:(2 * p + 1) * 128] = jnp.where(lo_half, vv, one).astype(bf16)
            vb_ref[:, (2 * p + 1) * 128:(2 * p + 2) * 128] = jnp.where(lo_half, pltpu.roll(vv, F_HD, 1), one).astype(bf16)

    t1 = lambda c: pl.BlockSpec((tm, 1024), lambda i: (i, c))
    vec = lambda n: pl.BlockSpec((1, n), lambda i: (0, 0))
    sd = jax.ShapeDtypeStruct
    return pl.pallas_call(
        body, name="fox_pre", grid=(S // tm,),
        in_specs=[t1(0), t1(1), t1(2), pl.BlockSpec((tm, 128), lambda i: (i, F_F0 // 128)), vec(16), vec(128), vec(128)],
        out_specs=[pl.BlockSpec((tm, 2048), lambda i: (i, 0))] * 3,
        out_shape=[sd((S, 2048), bf16)] * 3,
        scratch_shapes=[pltpu.VMEM((8, 16), f32)],
        compiler_params=_params(("arbitrary",)),
    )(proj, proj, proj, proj, fbias, qw2, kw2)


FTQ = 512
FHS_FWD, FHS_BWD = 8, 8


def fox_attn(qa, ka, v, comm=None):
    S = qa.shape[0]
    nq = S // FTQ
    FHS = FHS_FWD

    live = [(i, j) for i in range(nq) for j in range(i + 1)]
    qi_tab = jnp.asarray([i for i, _ in live], jnp.int32)
    kj_tab = jnp.asarray([j for _, j in live], jnp.int32)

    def body(qi_ref, kj_ref, q_ref, k_ref, v_ref, o_ref, lse_ref, m_scr, acc_scr):
        t = pl.program_id(1)
        i, j = qi_ref[t], kj_ref[t]

        @pl.when(j == 0)
        def _():
            m_scr[...] = jnp.full_like(m_scr, NEG)
            acc_scr[...] = jnp.zeros_like(acc_scr)

        def step(diagonal):
            for e in range(FHS):
                es = slice(e * 128, (e + 1) * 128)
                s = _nt(q_ref[:, es], k_ref[:, es])
                if diagonal:
                    s = jnp.where(_iota((FTQ, FTQ), 0) >= _iota((FTQ, FTQ), 1), s, NEG)
                m_old = m_scr[e]
                m_new = jnp.maximum(m_old, jnp.max(s, axis=-1, keepdims=True))
                p = jnp.exp(s - m_new[:, 0:1])
                acc_scr[e] = acc_scr[e] * jnp.exp(m_old - m_new) + _nn(p.astype(bf16), v_ref[:, es])
                m_scr[e] = m_new

        pl.when(j < i)(functools.partial(step, False))

        @pl.when(j == i)
        def _():
            step(True)
            for e in range(FHS):
                vs = slice(e * F_HD, (e + 1) * F_HD)
                acc = acc_scr[e]
                l = acc[:, F_HD:F_HD + 1]
                o_ref[:, vs] = acc[:, 0:F_HD] / l
                lse_ref[:, vs] = m_scr[e][:, 0:F_HD] + jnp.log(l)

    sd = jax.ShapeDtypeStruct
    qo = pl.BlockSpec((FTQ, F_HD * FHS), lambda p, t, qi, kj: (qi[t], p))
    kv = pl.BlockSpec((FTQ, 128 * FHS), lambda p, t, qi, kj: (kj[t], p))
    return _call(
        body, name="fox_attn", grid=(F_H // FHS, len(live)),
        in_specs=[pl.BlockSpec((FTQ, 128 * FHS), lambda p, t, qi, kj: (qi[t], p)), kv, kv],
        out_specs=[qo, qo],
        out_shape=[sd((S, 1024), f32), sd((S, 1024), f32)],
        scratch_shapes=[pltpu.VMEM((FHS, FTQ, 128), f32), pltpu.VMEM((FHS, FTQ, 128), f32)],
        sem=("parallel", "arbitrary"), args=(qa, ka, v), comm=comm, prefetch=(qi_tab, kj_tab))


def fox_attn_bwd(qa, ka, v, do, lse, delta, comm=None):
    S = qa.shape[0]
    nq = S // FTQ
    FHS = FHS_BWD

    live = [(j, i) for j in range(nq) for i in range(j, nq)]
    kj_tab = jnp.asarray([j for j, _ in live], jnp.int32)
    qi_tab = jnp.asarray([i for _, i in live], jnp.int32)

    def body(kj_ref, qi_ref, q_ref, k_ref, v_ref, do_ref, lse_ref, dl_ref, dq_ref, dk_ref, dv_ref, dk_scr, dv_scr):
        t = pl.program_id(1)
        j, i = kj_ref[t], qi_ref[t]

        @pl.when(t == 0)
        def _():
            dq_ref[...] = jnp.zeros_like(dq_ref)

        @pl.when(i == j)
        def _():
            dk_scr[...] = jnp.zeros_like(dk_scr)
            dv_scr[...] = jnp.zeros_like(dv_scr)

        def step(diagonal):
            rows = pl.ds(pl.multiple_of(i * FTQ, FTQ), FTQ)
            for e in range(FHS):
                es, vs = slice(e * 128, (e + 1) * 128), slice(e * F_HD, (e + 1) * F_HD)
                qe, ke = q_ref[:, es], k_ref[:, es]
                dob = do_ref[:, vs]
                s = _nt(qe, ke)
                if diagonal:
                    s = jnp.where(_iota((FTQ, FTQ), 0) >= _iota((FTQ, FTQ), 1), s, NEG)
                p = jnp.exp(s - lse_ref[:, e * F_HD:e * F_HD + 1])
                ds = p * (_nt(dob, v_ref[:, e * 128:e * 128 + F_HD]) - dl_ref[:, e * F_HD:e * F_HD + 1])
                dsb = ds.astype(bf16)
                dv_scr[e] += _tn(dob, p.astype(bf16))
                dk_scr[e] += _tn(qe, dsb)
                dq_ref[rows, es] += _nn(dsb, ke)

        pl.when(i > j)(functools.partial(step, False))
        pl.when(i == j)(functools.partial(step, True))

        @pl.when(i == nq - 1)
        def _():
            for e in range(FHS):
                dk_ref[:, e * 128:(e + 1) * 128] = dk_scr[e].T
                dv_ref[:, e * F_HD:(e + 1) * F_HD] = dv_scr[e].T

    sd = jax.ShapeDtypeStruct
    qi = lambda w: pl.BlockSpec((FTQ, w * FHS), lambda p, t, kj_, qi_: (qi_[t], p))
    kj = lambda w: pl.BlockSpec((FTQ, w * FHS), lambda p, t, kj_, qi_: (kj_[t], p))
    return _call(
        body, name="fox_attn_bwd", grid=(F_H // FHS, len(live)),
        in_specs=[qi(128), kj(128), kj(128), qi(F_HD), qi(F_HD), qi(F_HD)],
        out_specs=[pl.BlockSpec((S, 128 * FHS), lambda p, t, kj_, qi_: (0, p)), kj(128), kj(F_HD)],
        out_shape=[sd((S, 2048), f32), sd((S, 2048), f32), sd((S, 1024), f32)],
        scratch_shapes=[pltpu.VMEM((FHS, 128, FTQ), f32), pltpu.VMEM((FHS, F_HD, FTQ), f32)],
        sem=("parallel", "arbitrary"), args=(qa, ka, v, do, lse, delta), comm=comm, prefetch=(kj_tab, qi_tab))


def fox_gate(o, proj):
    S = o.shape[0]
    tm = 512

    def body(o_ref, z_ref, o2_ref):
        o2_ref[...] = (o_ref[...] * _silu(z_ref[...])).astype(bf16)

    t = pl.BlockSpec((tm, 1024), lambda i: (i, 0))
    return pl.pallas_call(
        body, name="fox_gate", grid=(S // tm,),
        in_specs=[t, pl.BlockSpec((tm, 1024), lambda i: (i, 3))], out_specs=t,
        out_shape=jax.ShapeDtypeStruct((S, 1024), bf16),
        compiler_params=_params(("parallel",)),
    )(o, proj)


def fox_gate_bwd(do2, o, proj):
    S = o.shape[0]
    tm = 256

    def body(d_ref, o_ref, z_ref, do_ref, dz_ref, dl_ref):
        lo_half = _iota((tm, 128), 1) < F_HD
        for p in range(F_H // 2):
            ps = slice(p * 128, (p + 1) * 128)
            d2, ov, z = d_ref[:, ps], o_ref[:, ps], z_ref[:, ps]
            dov = d2 * _silu(z)
            do_ref[:, ps] = dov.astype(bf16)
            dz_ref[:, ps] = (d2 * ov * _dsilu(z)).astype(bf16)
            dl_ref[:, ps] = _half_mean(dov * ov, lo_half) * float(F_HD)

    t = pl.BlockSpec((tm, 1024), lambda i: (i, 0))
    sd = jax.ShapeDtypeStruct
    return pl.pallas_call(
        body, name="fox_gate_bwd", grid=(S // tm,),
        in_specs=[t, t, pl.BlockSpec((tm, 1024), lambda i: (i, 3))], out_specs=[t, t, t],
        out_shape=[sd((S, 1024), bf16), sd((S, 1024), bf16), sd((S, 1024), f32)],
        compiler_params=_params(("parallel",)),
    )(do2, o, proj)


def fox_pre_bwd(proj, fbias, qw2, kw2, dqa, dka, dv, dz):
    S = proj.shape[0]
    tm = 256
    nb = S // tm

    def body(q_ref, k_ref, f_ref, fb_ref, qw_ref, kw_ref, dqa_ref, dka_ref, dv_ref, dz_ref, dp_ref, st_ref, carry):
        i = pl.program_id(0)

        @pl.when(i == 0)
        def _():
            carry[...] = jnp.zeros_like(carry)

        lane = _iota((tm, 128), 1)
        lo_half = lane < F_HD
        lane16 = _iota((tm, 16), 1)
        dcum = jnp.zeros((tm, 16), f32)
        dws = []
        for src, w_ref, dsrc, is_q, col0 in ((q_ref, qw_ref, dqa_ref, True, 0), (k_ref, kw_ref, dka_ref, False, 1024)):
            dw = jnp.zeros((1, 128), f32)
            for p in range(F_H // 2):
                ps = slice(p * 128, (p + 1) * 128)
                x = src[:, ps]
                r = lax.rsqrt(_half_mean(x * x, lo_half) + EPS)
                xh = x * r
                d0 = dsrc[:, (2 * p) * 128:(2 * p + 1) * 128]
                d1 = dsrc[:, (2 * p + 1) * 128:(2 * p + 2) * 128]
                dy = jnp.where(lo_half, d0, pltpu.roll(d1, F_HD, 1))
                if is_q:
                    dy = dy * (F_HD ** -0.5)
                dxh = dy * w_ref[...]
                dw = dw + jnp.sum(dy * xh, axis=0, keepdims=True)
                dp_ref[:, col0 + p * 128:col0 + (p + 1) * 128] = (r * (dxh - xh * _half_mean(dxh * xh, lo_half))).astype(bf16)
                for e, de in ((0, d0), (1, d1)):
                    col = de[:, 64:65] if is_q else -de[:, 67:68]
                    dcum = dcum + jnp.where(lane16 == 2 * p + e, col, 0.0)
            dws.append(dw)
        dp_ref[:, 2048:3072] = dv_ref[...].astype(bf16)
        dp_ref[:, 3072:4096] = dz_ref[...]
        utri = jnp.where(_iota((tm, tm), 1) >= _iota((tm, tm), 0), 1.0, 0.0).astype(f32)
        dlogf = _nn(utri, dcum, HI) + carry[0:1, :]
        carry[0:1, :] = dlogf[0:1, :]
        fl = f_ref[:, 0:16] + fb_ref[...]
        df = dlogf * _sigmoid(-fl)
        place = jnp.where(_iota((16, 128), 1) == _iota((16, 128), 0), 1.0, 0.0).astype(f32)
        dfw = _nn(df, place, HI)
        dp_ref[:, F_F0:F_INP] = dfw.astype(bf16)
        upd = jnp.concatenate(dws + [jnp.sum(dfw, axis=0, keepdims=True), jnp.zeros((5, 128), f32)], axis=0)

        @pl.when(i == 0)
        def _():
            st_ref[...] = upd

        @pl.when(i > 0)
        def _():
            st_ref[...] += upd

    rev = lambda w, c: pl.BlockSpec((tm, w), lambda i: (nb - 1 - i, c))
    vec = lambda n: pl.BlockSpec((1, n), lambda i: (0, 0))
    sd = jax.ShapeDtypeStruct
    return pl.pallas_call(
        body, name="fox_pre_bwd", grid=(nb,),
        in_specs=[rev(1024, 0), rev(1024, 1), rev(128, F_F0 // 128), vec(16), vec(128), vec(128),
                  rev(2048, 0), rev(2048, 0), rev(1024, 0), rev(1024, 0)],
        out_specs=[rev(F_INP, 0), pl.BlockSpec((8, 128), lambda i: (0, 0))],
        out_shape=[sd((S, F_INP), bf16), sd((8, 128), f32)],
        scratch_shapes=[pltpu.VMEM((8, 16), f32)],
        compiler_params=_params(("arbitrary",)),
    )(proj, proj, proj, fbias, qw2, kw2, dqa, dka, dv, dz)


def _me():
    return lax.axis_index("x"), lax.axis_index("y"), lax.axis_index("c")


def _other_chips(x, y):
    return [(1 - x, y), (x, 1 - y), (1 - x, 1 - y)]


def ag_small(xs):
    m_per, n = xs.shape

    def body(x_ref, out_ref, send_sems, recv_sems, local_sem):
        x, y, c = _me()
        me, sibling = (x, y, c), (x, y, 1 - c)
        chips = _other_chips(x, y)

        def rows(px, py, pc):
            return out_ref.at[pl.ds((4 * px + 2 * py + pc) * m_per, m_per), :]

        def copy(k, block, to, src=None):
            return pltpu.make_async_remote_copy(
                src_ref=rows(*block) if src is None else src, dst_ref=rows(*block),
                send_sem=send_sems.at[k], recv_sem=recv_sems.at[k], device_id=to, device_id_type=MESH)

        mine = pltpu.make_async_copy(x_ref, rows(*me), local_sem)
        mine.start()
        first = [copy(0, me, sibling, src=x_ref)]
        first += [copy(1 + j, me, (*chip, c), src=x_ref) for j, chip in enumerate(chips)]
        for cp in first:
            cp.start()
        passed = [copy(4 + j, (*chip, c), sibling) for j, chip in enumerate(chips)]
        for j, chip in enumerate(chips):
            copy(1 + j, (*chip, c), me).wait_recv()
            passed[j].start()
        copy(0, sibling, me).wait_recv()
        for j, chip in enumerate(chips):
            copy(4 + j, (*chip, 1 - c), me).wait_recv()
        for cp in first + passed:
            cp.wait_send()
        mine.wait()

    return pl.pallas_call(
        body, name="ag_small",
        out_shape=jax.ShapeDtypeStruct((8 * m_per, n), xs.dtype),
        in_specs=[pl.BlockSpec(memory_space=pltpu.VMEM)], out_specs=pl.BlockSpec(memory_space=pltpu.VMEM),
        scratch_shapes=[pltpu.SemaphoreType.DMA((7,)), pltpu.SemaphoreType.DMA((7,)), pltpu.SemaphoreType.DMA],
        compiler_params=pltpu.CompilerParams(vmem_limit_bytes=VMEM_LIMIT),
    )(xs)


_ANY = pl.BlockSpec(memory_space=pl.ANY)


def ag_chips(arrs):
    n = len(arrs)
    assert all(a.shape[0] == 2 for a in arrs)

    def body(*refs):
        ins, outs = refs[:n], refs[n:2 * n]
        send_sems, recv_sems, fwd_send, fwd_recv, local_sems = refs[2 * n:]
        x, y, c = _me()
        me = 2 * x + y
        chips = _other_chips(x, y)
        started = []
        for a in range(n):
            cp = pltpu.make_async_copy(ins[a], outs[a].at[me], local_sems.at[a])
            cp.start()
            started.append(cp)
        sends = []
        for a in range(n):
            for j, (px, py) in enumerate(chips):
                r = pltpu.make_async_remote_copy(
                    src_ref=ins[a].at[c], dst_ref=outs[a].at[me, c], send_sem=send_sems.at[3 * a + j],
                    recv_sem=recv_sems.at[3 * a + j], device_id=(px, py, c), device_id_type=MESH)
                r.start()
                sends.append(r)
        for a in range(n):
            for j, (px, py) in enumerate(chips):
                got = outs[a].at[2 * px + py, c]
                pltpu.make_async_remote_copy(
                    src_ref=ins[a].at[c], dst_ref=got, send_sem=send_sems.at[3 * a + j],
                    recv_sem=recv_sems.at[3 * a + j], device_id=(px, py, c), device_id_type=MESH).wait_recv()
                f = pltpu.make_async_remote_copy(
                    src_ref=got, dst_ref=got, send_sem=fwd_send.at[3 * a + j], recv_sem=fwd_recv.at[3 * a + j],
                    device_id=(x, y, 1 - c), device_id_type=MESH)
                f.start()
                sends.append(f)
        for a in range(n):
            for j, (px, py) in enumerate(chips):
                theirs = outs[a].at[2 * px + py, 1 - c]
                pltpu.make_async_remote_copy(
                    src_ref=theirs, dst_ref=theirs, send_sem=fwd_send.at[3 * a + j], recv_sem=fwd_recv.at[3 * a + j],
                    device_id=(x, y, 1 - c), device_id_type=MESH).wait_recv()
        for r in sends:
            r.wait_send()
        for cp in started:
            cp.wait()

    sems = pltpu.SemaphoreType.DMA((3 * n,))
    return pl.pallas_call(
        body, name="ag_chips",
        out_shape=[jax.ShapeDtypeStruct((4,) + a.shape, a.dtype) for a in arrs],
        in_specs=[_ANY] * n, out_specs=[_ANY] * n,
        scratch_shapes=[sems, sems, sems, sems, pltpu.SemaphoreType.DMA((n,))],
    )(*arrs)


def _ag_comm(arrs):
    n = len(arrs)

    def copies(ins, outs, sems, inbound):
        send_sems, recv_sems, local_sems = sems
        x, y, c = _me()
        me = 2 * x + y
        local = [pltpu.make_async_copy(ins[a], outs[a].at[me], local_sems.at[a]) for a in range(n)]
        out_cp, in_cp = [], []
        for a in range(n):
            for j, (px, py) in enumerate(_other_chips(x, y)):
                mk = functools.partial(pltpu.make_async_remote_copy, src_ref=ins[a], send_sem=send_sems.at[3 * a + j],
                                       recv_sem=recv_sems.at[3 * a + j], device_id=(px, py, c), device_id_type=MESH)
                out_cp.append(mk(dst_ref=outs[a].at[me]))
                if inbound:
                    in_cp.append(mk(dst_ref=outs[a].at[2 * px + py]))
        return local, out_cp, in_cp

    def start(ins, outs, sems):
        local, out_cp, _ = copies(ins, outs, sems, False)
        for cp in local + out_cp:
            cp.start()

    def wait(ins, outs, sems):
        local, out_cp, in_cp = copies(ins, outs, sems, True)
        for cp in in_cp:
            cp.wait_recv()
        for cp in out_cp:
            cp.wait_send()
        for cp in local:
            cp.wait()

    sems = [pltpu.SemaphoreType.DMA((3 * n,)), pltpu.SemaphoreType.DMA((3 * n,)), pltpu.SemaphoreType.DMA((n,))]
    return _Comm(arrs, [jax.ShapeDtypeStruct((4,) + a.shape, a.dtype) for a in arrs], sems, start, wait)


def _rs_comm(gs):
    n = len(gs)
    flips = [(fx, fy, fc) for fx in (0, 1) for fy in (0, 1) for fc in (0, 1)][1:]

    def copies(ins, outs, sems, inbound):
        send_sems, recv_sems, local_sems = sems
        x, y, c = _me()
        me = 4 * x + 2 * y + c
        local, out_cp, in_cp = [], [], []
        for a in range(n):
            rh = ins[a].shape[1] // 2
            mine = ins[a].at[2 * x + y, pl.ds(c * rh, rh), :]
            local.append(pltpu.make_async_copy(mine, outs[a].at[me], local_sems.at[a]))
            for j, (fx, fy, fc) in enumerate(flips):
                px, py, pc = (1 - x if fx else x), (1 - y if fy else y), (1 - c if fc else c)
                mk = functools.partial(pltpu.make_async_remote_copy, send_sem=send_sems.at[7 * a + j],
                                       recv_sem=recv_sems.at[7 * a + j], device_id=(px, py, pc), device_id_type=MESH)
                out_cp.append(mk(src_ref=ins[a].at[2 * px + py, pl.ds(pc * rh, rh), :], dst_ref=outs[a].at[me]))
                if inbound:
                    in_cp.append(mk(src_ref=mine, dst_ref=outs[a].at[4 * px + 2 * py + pc]))
        return local, out_cp, in_cp

    def start(ins, outs, sems):
        local, out_cp, _ = copies(ins, outs, sems, False)
        for cp in local + out_cp:
            cp.start()

    def wait(ins, outs, sems):
        local, out_cp, in_cp = copies(ins, outs, sems, True)
        for cp in in_cp:
            cp.wait_recv()
        for cp in out_cp:
            cp.wait_send()
        for cp in local:
            cp.wait()

    sems = [pltpu.SemaphoreType.DMA((7 * n,)), pltpu.SemaphoreType.DMA((7 * n,)), pltpu.SemaphoreType.DMA((n,))]
    return _Comm(gs, [jax.ShapeDtypeStruct((8, g.shape[1] // 2, g.shape[2]), g.dtype) for g in gs], sems, start, wait)


def sum_leading(q, name):
    K, R, C = q.shape
    tr = _pick(R, (256, 128, 64, 32, 16, 8))

    def body(q_ref, o_ref):
        acc = q_ref[0]
        for k in range(1, K):
            acc = acc + q_ref[k]
        o_ref[...] = acc

    return pl.pallas_call(
        body, name=name, grid=(R // tr,),
        in_specs=[pl.BlockSpec((K, tr, C), lambda i: (0, i, 0))], out_specs=pl.BlockSpec((tr, C), lambda i: (i, 0)),
        out_shape=jax.ShapeDtypeStruct((R, C), f32),
        compiler_params=_params(("parallel",)),
    )(q)


def rs_sum_devices(q, cidx, layer, n_layers, into=None):
    K, R, C = q.shape
    tr = _pick(R, (256, 128))

    def body(c_ref, q_ref, *rest):
        acc = q_ref[0].astype(f32)
        for k in range(1, K):
            acc = acc + q_ref[k].astype(f32)
        rest[-1][0, 0] = acc

    return pl.pallas_call(
        body, name="rs_sum_devices",
        grid_spec=pltpu.PrefetchScalarGridSpec(
            num_scalar_prefetch=1, grid=(R // tr,),
            in_specs=[pl.BlockSpec((K, tr, C), lambda i, c_ref: (0, i, 0))] + ([] if into is None else [_ANY]),
            out_specs=pl.BlockSpec((1, 1, tr, C), lambda i, c_ref: (layer, c_ref[0], i, 0))),
        out_shape=jax.ShapeDtypeStruct((n_layers, 2, R, C), f32),
        input_output_aliases={} if into is None else {2: 0},
        compiler_params=_params(("parallel",)),
    )(cidx, q, *([] if into is None else [into]))


def rs_share_halves(rs):
    n = len(rs)

    def body(*refs):
        bufs = refs[n:2 * n]
        send_sems, recv_sems = refs[2 * n:]
        x, y, c = _me()
        cps = []
        for a in range(n):
            mine = bufs[a].at[pl.ds(0, bufs[a].shape[0]), c]
            cp = pltpu.make_async_remote_copy(
                src_ref=mine, dst_ref=mine, send_sem=send_sems.at[a], recv_sem=recv_sems.at[a],
                device_id=(x, y, 1 - c), device_id_type=MESH)
            cp.start()
            cps.append(cp)
        for a, cp in enumerate(cps):
            theirs = bufs[a].at[pl.ds(0, bufs[a].shape[0]), 1 - c]
            pltpu.make_async_remote_copy(
                src_ref=theirs, dst_ref=theirs, send_sem=send_sems.at[a], recv_sem=recv_sems.at[a],
                device_id=(x, y, 1 - c), device_id_type=MESH).wait_recv()
            cp.wait_send()

    return pl.pallas_call(
        body, name="rs_share_halves",
        out_shape=[jax.ShapeDtypeStruct(r.shape, r.dtype) for r in rs],
        in_specs=[_ANY] * n, out_specs=[_ANY] * n, input_output_aliases={a: a for a in range(n)},
        scratch_shapes=[pltpu.SemaphoreType.DMA((n,)), pltpu.SemaphoreType.DMA((n,))],
    )(*rs)


def ada_mod(c_all, ada_w):
    L, _, n = ada_w.shape

    def body(c_ref, w_ref, o_ref):
        o_ref[0] = _nn(_silu(c_ref[...]), w_ref[0], HI)

    return pl.pallas_call(
        body, name="ada_mod", grid=(L,),
        in_specs=[pl.BlockSpec((8, D), lambda l: (0, 0)), pl.BlockSpec((1, D, n), lambda l: (l, 0, 0))],
        out_specs=pl.BlockSpec((1, 8, n), lambda l: (l, 0, 0)),
        out_shape=jax.ShapeDtypeStruct((L, 8, n), f32),
        compiler_params=_params(("parallel",)),
    )(c_all, ada_w)


def ada_w_grad(c_all, dmod):
    L, _, n = dmod.shape

    def body(c_ref, d_ref, o_ref):
        o_ref[0] = _tn(_silu(c_ref[...]), d_ref[0], HI)

    return pl.pallas_call(
        body, name="ada_w_grad", grid=(L,),
        in_specs=[pl.BlockSpec((8, D), lambda l: (0, 0)), pl.BlockSpec((1, 8, n), lambda l: (l, 0, 0))],
        out_specs=pl.BlockSpec((1, D, n), lambda l: (l, 0, 0)),
        out_shape=jax.ShapeDtypeStruct((L, D, n), f32),
        compiler_params=_params(("parallel",)),
    )(c_all, dmod)


def adamw(w, g, m, v, name):
    shp = w.shape
    two = lambda a: a.reshape(-1, shp[-1])
    R, C = two(w).shape
    tr = _pick(R, (256, 128, 64, 32, 16, 8))
    bc1, bc2 = 1.0 - B1 ** STEP, 1.0 - B2 ** STEP

    def body(w_ref, g_ref, m_ref, v_ref, d_ref, mo_ref, vo_ref):
        gv = g_ref[...]
        mn = B1 * m_ref[...] + (1.0 - B1) * gv
        vn = B2 * v_ref[...] + (1.0 - B2) * (gv * gv)
        d_ref[...] = -LR * ((mn / bc1) / (jnp.sqrt(vn / bc2) + AEPS) + WD * w_ref[...])
        mo_ref[...] = mn
        vo_ref[...] = vn

    t = pl.BlockSpec((tr, C), lambda i: (i, 0))
    outs = pl.pallas_call(
        body, name=name, grid=(R // tr,),
        in_specs=[t] * 4, out_specs=[t] * 3, out_shape=[jax.ShapeDtypeStruct((R, C), f32)] * 3,
        compiler_params=_params(("parallel",)),
    )(two(w), two(g), two(m), two(v))
    return [o.reshape(shp) for o in outs]


def _pack(arrs):
    parts, offs, r0 = [], [], 0
    for a in arrs:
        n = a.size
        rows = -(-n // 1024) * 8
        parts.append(jnp.pad(a.reshape(-1), (0, rows * 128 - n)).reshape(rows, 128))
        offs.append((r0, rows))
        r0 += rows
    return jnp.concatenate(parts, axis=0), offs


def _unpack(buf, offs, shapes):
    out = []
    for (r0, rows), shp in zip(offs, shapes):
        n = 1
        for d in shp:
            n *= d
        out.append(buf[..., r0:r0 + rows, :].reshape(buf.shape[:-2] + (rows * 128,))[..., :n].reshape(buf.shape[:-2] + tuple(shp)))
    return out


def kernel(x, c, norm_w, ada_w, ada_b, a_w_in, a_conv_w, a_A_log, a_dt_bias, a_norm_w, a_w_out, b_w_in, b_f_bias, b_qn_w, b_kn_w, b_w_out, final_norm_w, loss_target, m_norm_w, m_ada_w, m_ada_b, m_a_w_in, m_a_conv_w, m_a_A_log, m_a_dt_bias, m_a_norm_w, m_a_w_out, m_b_w_in, m_b_f_bias, m_b_qn_w, m_b_kn_w, m_b_w_out, m_final_norm_w, v_norm_w, v_ada_w, v_ada_b, v_a_w_in, v_a_conv_w, v_a_A_log, v_a_dt_bias, v_a_norm_w, v_a_w_out, v_b_w_in, v_b_f_bias, v_b_qn_w, v_b_kn_w, v_b_w_out, v_final_norm_w):
    weights = dict(norm_w=norm_w, ada_w=ada_w, ada_b=ada_b, a_w_in=a_w_in, a_conv_w=a_conv_w, a_A_log=a_A_log,
                   a_dt_bias=a_dt_bias, a_norm_w=a_norm_w, a_w_out=a_w_out, b_w_in=b_w_in, b_f_bias=b_f_bias,
                   b_qn_w=b_qn_w, b_kn_w=b_kn_w, b_w_out=b_w_out, final_norm_w=final_norm_w)
    m_in = dict(norm_w=m_norm_w, ada_w=m_ada_w, ada_b=m_ada_b, a_w_in=m_a_w_in, a_conv_w=m_a_conv_w, a_A_log=m_a_A_log,
                a_dt_bias=m_a_dt_bias, a_norm_w=m_a_norm_w, a_w_out=m_a_w_out, b_w_in=m_b_w_in, b_f_bias=m_b_f_bias,
                b_qn_w=m_b_qn_w, b_kn_w=m_b_kn_w, b_w_out=m_b_w_out, final_norm_w=m_final_norm_w)
    v_in = dict(norm_w=v_norm_w, ada_w=v_ada_w, ada_b=v_ada_b, a_w_in=v_a_w_in, a_conv_w=v_a_conv_w, a_A_log=v_a_A_log,
                a_dt_bias=v_a_dt_bias, a_norm_w=v_a_norm_w, a_w_out=v_a_w_out, b_w_in=v_b_w_in, b_f_bias=v_b_f_bias,
                b_qn_w=v_b_qn_w, b_kn_w=v_b_kn_w, b_w_out=v_b_w_out, final_norm_w=v_final_norm_w)
    xi, yi, ci = _me()
    me_b, me_k = 4 * xi + 2 * yi + ci, 2 * xi + yi
    cidx = ci.astype(jnp.int32).reshape(1)
    S = x.shape[1]
    depth, n_a, n_b = norm_w.shape[0], a_w_in.shape[0], b_w_in.shape[0]
    x0, tgt = x.reshape(S, D), loss_target.reshape(S, D)

    c_all = ag_small(jnp.pad(c, ((0, 7), (0, 0)))).reshape(8, 8, D)[:, 0]
    nloc = ada_w.shape[2]
    parts = ag_small(ada_mod(c_all, ada_w).reshape(depth * 8, nloc)).reshape(4, 2, depth, 8, nloc)[:, 0]
    mine = lax.dynamic_index_in_dim(parts, me_b, axis=2, keepdims=False)
    mod = jnp.transpose(mine, (1, 0, 2)).reshape(depth, 4 * nloc) + ada_b
    shift, scale, gate = (mod[:, k * D:(k + 1) * D] for k in range(3))

    w_loc = [(a_w_in[i // 2] if i % 2 == 0 else b_w_in[i // 2]).astype(bf16) for i in range(depth)]
    wo_loc = [(a_w_out[i // 2] if i % 2 == 0 else b_w_out[i // 2]).astype(bf16) for i in range(depth)]
    pad_in = [(G_INP - G_IN) if i % 2 == 0 else (F_INP - F_IN) for i in range(depth)]
    halves = lambda w: w.reshape((2, w.shape[0] // 2) + w.shape[1:])

    def cols_in_place(g_in, pad):
        w = jnp.transpose(g_in, (1, 0, 2)).reshape(g_in.shape[1], -1)
        return jnp.pad(w, ((0, 0), (0, pad)))

    g_in0, g_conv = ag_chips([halves(w_loc[0]), a_conv_w])
    w_in_full = [cols_in_place(g_in0.reshape((4,) + w_loc[0].shape), pad_in[0])]
    w_out_full = []
    conv = [jnp.transpose(g_conv[:, l], (1, 0, 2)).reshape(CONV_K, -1) for l in range(n_a)]
    qw2 = [_row(jnp.tile(b_qn_w[l], 2)) for l in range(n_b)]
    kw2 = [_row(jnp.tile(b_kn_w[l], 2)) for l in range(n_b)]

    saved, xc = [], x0
    for i in range(depth):
        l = i // 2
        nxt = _ag_comm([w_loc[i + 1], wo_loc[i + 1]]) if i + 1 < depth else None
        h = ln_mod(xc, _row(norm_w[i]), _row(scale[i]), _row(shift[i]))
        name = "mm_a_in" if i % 2 == 0 else "mm_b_in"
        if i == 0:
            proj, got = matmul(h, w_in_full[0], "nn", name, comm=_ag_comm([wo_loc[0]]))
            w_out_full.append(got[0].reshape(-1, D))
        else:
            proj = matmul(h, w_in_full[i], "nn", name)
        if i % 2 == 0:
            pre = gdn_pre(proj, conv[l], _row(a_A_log[l]), _row(a_dt_bias[l]))
            res, got = gdn_fwd(*pre, comm=nxt)
            o2 = gdn_onorm(res[0], proj, _row(a_norm_w[l]))
            y, xn = out_proj(o2, w_out_full[i], xc, _row(gate[i]), "out_proj_a")
        else:
            pre = fox_pre(proj, _row(b_f_bias[l]), qw2[l], kw2[l])
            res, got = fox_attn(*pre, comm=nxt)
            o2 = fox_gate(res[0], proj)
            y, xn = out_proj(o2, w_out_full[i], xc, _row(gate[i]), "out_proj_b")
        saved.append((xc, h, proj, o2, y, pre, res))
        if nxt is not None:
            w_in_full.append(cols_in_place(got[0], pad_in[i + 1]))
            w_out_full.append(got[1].reshape(-1, D))
        xc = xn
    dx, st_f = final_loss(xc, _row(final_norm_w), tgt)

    d_norm, d_mod = [None] * depth, [None] * depth
    d_conv, d_alog, d_dtb, d_anw = [None] * n_a, [None] * n_a, [None] * n_a, [None] * n_a
    d_fb, d_qn, d_kn = [None] * n_b, [None] * n_b, [None] * n_b
    ex_in, ex_out, pend_in = [None] * depth, [None] * depth, None
    for i in reversed(range(depth)):
        l = i // 2
        xin, h, proj, o2, y, pre, res = saved[i]
        ab = "a" if i % 2 == 0 else "b"
        dy, st_g = gate_bwd(dx, y, _row(gate[i]))
        do2 = matmul(dy, w_out_full[i], "nt", f"mm_{ab}_do2")
        d_out = matmul(o2, dy, "tn", f"mm_{ab}_dwo", out_dtype=bf16)
        ride = _rs_comm(([] if pend_in is None else [pend_in]) + [d_out.reshape(4, d_out.shape[0] // 4, D)])
        if i % 2 == 0:
            o, wv, at, tinv, vn, st = res
            do, dz, st_o = gdn_onorm_bwd(do2, o, proj, _row(a_norm_w[l]))
            grads, got = gdn_bwd(do, *pre, wv, at, tinv, vn, st, comm=ride)
            dcv, dba, st_s = gdn_pre_bwd(proj, conv[l], _row(a_A_log[l]), _row(a_dt_bias[l]), *grads)
            dproj, dcw = gdn_conv_bwd(proj, conv[l], dcv, dz, dba)
            d_conv[l], d_alog[l], d_dtb[l], d_anw[l] = dcw[:CONV_K], st_s[0], st_s[1], st_o[0]
        else:
            o, lse = res
            do, dz, delta = fox_gate_bwd(do2, o, proj)
            (dqa, dka, dv), got = fox_attn_bwd(*pre, do, lse, delta, comm=ride)
            dproj, st_b = fox_pre_bwd(proj, _row(b_f_bias[l]), qw2[l], kw2[l], dqa, dka, dv, dz)
            d_fb[l], d_qn[l], d_kn[l] = st_b[2, :F_H], st_b[0, :F_HD] + st_b[0, F_HD:], st_b[1, :F_HD] + st_b[1, F_HD:]
        ex_out[i] = got[-1]
        if pend_in is not None:
            ex_in[i + 1] = got[0]
        d_in = matmul(h, dproj, "tn", f"mm_{ab}_dw", out_dtype=bf16)
        cl = w_loc[i].shape[1]
        pend_in = jnp.transpose(d_in[:, :4 * cl].reshape(d_in.shape[0], 4, cl), (1, 0, 2))
        if i == 0:
            dh, got = matmul(dproj, w_in_full[i], "nt", f"mm_{ab}_dh", comm=_rs_comm([pend_in]))
            ex_in[0] = got[0]
        else:
            dh = matmul(dproj, w_in_full[i], "nt", f"mm_{ab}_dh")
        dx, st_n = ln_mod_bwd(xin, _row(norm_w[i]), _row(scale[i]), dh, dx)
        d_norm[i] = st_n[0]
        d_mod[i] = jnp.concatenate([st_n[2], st_n[1], st_g[0]])

    small = [jnp.stack(d_norm), jnp.stack(d_mod), jnp.stack(d_conv), jnp.stack(d_alog), jnp.stack(d_dtb), jnp.stack(d_anw),
             jnp.stack(d_fb), jnp.stack(d_qn), jnp.stack(d_kn), st_f[0], jnp.sum(st_f[1]).reshape(1)]
    shapes = [a.shape for a in small]
    buf, offs = _pack(small)
    gathered = ag_small(buf).reshape(8, buf.shape[0], 128)
    tot = _unpack(sum_leading(gathered, "sum_devices"), offs, shapes)
    g_norm, g_adab, g_convf, g_alog, g_dtb, g_anw, g_fb, g_qn, g_kn, g_fin, loss = tot
    dmod_all = _unpack(gathered, offs[1:2], shapes[1:2])[0]
    dmod_loc = lax.dynamic_slice_in_dim(dmod_all, me_k * nloc, nloc, axis=2)
    g_adaw = ada_w_grad(c_all, jnp.transpose(dmod_loc, (1, 0, 2)))
    g_conv_loc = lax.dynamic_slice_in_dim(g_convf, me_k * a_conv_w.shape[2], a_conv_w.shape[2], axis=2)

    bufs = {}
    for i in range(depth):
        for which, q in (("in", ex_in[i]), ("out", ex_out[i])):
            key = ("a" if i % 2 == 0 else "b", which)
            bufs[key] = rs_sum_devices(q, cidx, i // 2, depth // 2, into=bufs.get(key))
    keys = list(bufs)
    done = dict(zip(keys, rs_share_halves([bufs[k] for k in keys])))
    grads = dict(norm_w=g_norm, ada_w=g_adaw, ada_b=g_adab, a_w_in=done["a", "in"].reshape(a_w_in.shape),
                 a_conv_w=g_conv_loc, a_A_log=g_alog, a_dt_bias=g_dtb, a_norm_w=g_anw,
                 a_w_out=done["a", "out"].reshape(a_w_out.shape), b_w_in=done["b", "in"].reshape(b_w_in.shape),
                 b_f_bias=g_fb, b_qn_w=g_qn, b_kn_w=g_kn, b_w_out=done["b", "out"].reshape(b_w_out.shape),
                 final_norm_w=g_fin)
    names = list(weights)
    upd = {n: adamw(weights[n], grads[n], m_in[n], v_in[n], "adamw_" + n) for n in names}
    return (loss.reshape(()), dx.reshape(x.shape), *[grads[n] for n in names], *[upd[n][0] for n in names],
            *[upd[n][1] for n in names], *[upd[n][2] for n in names])
```

```python
import functools

import jax
import jax.numpy as jnp
from jax import lax
from jax.experimental import pallas as pl
from jax.experimental.pallas import tpu as pltpu

f32, bf16 = jnp.float32, jnp.bfloat16
HI = lax.Precision.HIGHEST
MESH = pl.DeviceIdType.MESH

EPS = 1e-6
D = 1024
CHUNK = 64
GQK_H, GV_H, GHD = 8, 16, 128
G_CONV = 4096
G_Z0 = 4096
G_BA0 = 6144
G_IN, G_INP = 6176, 6272
CONV_K = 4
F_H, F_HD = 16, 64
F_F0 = 4096
F_IN, F_INP = 4112, 4224
LR, B1, B2, AEPS, WD, STEP = 0.001, 0.9, 0.999, 1e-08, 0.01, 10
NEG = -1e30
VMEM_LIMIT = 56 * 1024 * 1024


def _nn(a, b, prec=None):
    return lax.dot_general(a, b, (((1,), (0,)), ((), ())), preferred_element_type=f32, precision=prec)


def _nt(a, b, prec=None):
    return lax.dot_general(a, b, (((1,), (1,)), ((), ())), preferred_element_type=f32, precision=prec)


def _tn(a, b, prec=None):
    return lax.dot_general(a, b, (((0,), (0,)), ((), ())), preferred_element_type=f32, precision=prec)


def _iota(shape, axis):
    return lax.broadcasted_iota(jnp.int32, shape, axis)


def _sigmoid(x):
    return 0.5 * jnp.tanh(0.5 * x) + 0.5


def _softplus(x):
    return jnp.maximum(x, 0.0) + jnp.log(1.0 + jnp.exp(-jnp.abs(x)))


def _silu(x):
    return x * _sigmoid(x)


def _dsilu(x):
    s = _sigmoid(x)
    return s * (1.0 + x * (1.0 - s))


def _params(sem=None, vmem=VMEM_LIMIT):
    return pltpu.CompilerParams(dimension_semantics=sem, vmem_limit_bytes=vmem)


def _row(v):
    return v.reshape(1, -1)


class _Comm:
    def __init__(self, ins, out_shapes, sems, start, wait):
        self.ins, self.out_shapes, self.sems, self.start, self.wait = list(ins), list(out_shapes), list(sems), start, wait


def _call(body, *, name, grid, in_specs, out_specs, out_shape, scratch_shapes, sem, args, comm=None, prefetch=()):
    n_pf, n_in, n_out, n_s = len(prefetch), len(in_specs), len(out_specs), len(scratch_shapes)
    n_ci, n_co = (len(comm.ins), len(comm.out_shapes)) if comm is not None else (0, 0)

    def wrapped(*refs):
        pf, refs = refs[:n_pf], refs[n_pf:]
        core_in, c_in = refs[:n_in], refs[n_in:n_in + n_ci]
        o0 = n_in + n_ci
        core_out, c_out = refs[o0:o0 + n_out], refs[o0 + n_out:o0 + n_out + n_co]
        s0 = o0 + n_out + n_co
        core_s, c_sem = refs[s0:s0 + n_s], refs[s0 + n_s:]
        if comm is not None:
            first = functools.reduce(jnp.logical_and, [pl.program_id(d) == 0 for d in range(len(grid))])
            pl.when(first)(functools.partial(comm.start, c_in, c_out, c_sem))
        body(*pf, *core_in, *core_out, *core_s)
        if comm is not None:
            last = functools.reduce(jnp.logical_and, [pl.program_id(d) == grid[d] - 1 for d in range(len(grid))])
            pl.when(last)(functools.partial(comm.wait, c_in, c_out, c_sem))

    extra = ([], [], [], []) if comm is None else ([_ANY] * n_ci, [_ANY] * n_co, comm.out_shapes, comm.sems)
    spec = pltpu.PrefetchScalarGridSpec(
        num_scalar_prefetch=n_pf, grid=grid, in_specs=list(in_specs) + extra[0], out_specs=list(out_specs) + extra[1],
        scratch_shapes=list(scratch_shapes) + extra[3])
    outs = pl.pallas_call(
        wrapped, name=name if comm is None else name + "_x", grid_spec=spec, out_shape=list(out_shape) + extra[2],
        compiler_params=_params(sem if comm is None else ("arbitrary",) * len(grid)),
    )(*prefetch, *args, *(comm.ins if comm is not None else []))
    return outs[:n_out], outs[n_out:]


def _pick(n, pref):
    for t in pref:
        if n % t == 0:
            return t
    return n


MM_VMEM_BUDGET = 44 * 1024 * 1024


def _mm_tiles(M, N, K):
    best = None
    for tk in [K] + [t for t in (2048, 1408, 1024, 896, 512, 384, 256, 128) if K % t == 0 and t < K]:
        for tm in (2048, 1024, 512, 256, 128):
            for tn in (1408, 1024, 896, 512, 384, 256, 128):
                if M % tm or N % tn:
                    continue
                nk = K // tk
                need = 2 * 2 * (tm * tk + tk * tn) + 2 * 4 * tm * tn + (4 * tm * tn if nk > 1 else 0)
                if need <= MM_VMEM_BUDGET:
                    cand = ((nk, -tm * tn), (tm, tn, tk))
                    best = cand if best is None or cand[0] < best[0] else best
    return best[1]


def matmul(a, b, mode, name, out_dtype=f32, comm=None):
    if mode == "nn":
        (M, K), (_, N) = a.shape, b.shape
    elif mode == "nt":
        (M, K), (N, _) = a.shape, b.shape
    else:
        (K, M), (_, N) = a.shape, b.shape
    tm, tn, tk = _mm_tiles(M, N, K)
    nk = K // tk
    dot = {"nn": _nn, "nt": _nt, "tn": _tn}[mode]

    def body(a_ref, b_ref, o_ref, *acc):
        k = pl.program_id(2)
        part = dot(a_ref[...], b_ref[...])
        if nk == 1:
            o_ref[...] = part.astype(out_dtype)
        else:
            acc_ref = acc[0]

            @pl.when(k == 0)
            def _():
                acc_ref[...] = part

            @pl.when(k > 0)
            def _():
                acc_ref[...] += part

            @pl.when(k == nk - 1)
            def _():
                o_ref[...] = acc_ref[...].astype(out_dtype)

    a_spec = pl.BlockSpec((tk, tm), lambda i, j, k: (k, i)) if mode == "tn" else pl.BlockSpec((tm, tk), lambda i, j, k: (i, k))
    b_spec = pl.BlockSpec((tn, tk), lambda i, j, k: (j, k)) if mode == "nt" else pl.BlockSpec((tk, tn), lambda i, j, k: (k, j))
    outs, got = _call(
        body, name=name, grid=(M // tm, N // tn, nk),
        in_specs=[a_spec, b_spec], out_specs=[pl.BlockSpec((tm, tn), lambda i, j, k: (i, j))],
        out_shape=[jax.ShapeDtypeStruct((M, N), out_dtype)],
        scratch_shapes=[] if nk == 1 else [pltpu.VMEM((tm, tn), f32)],
        sem=("parallel", "parallel", "arbitrary"), args=(a, b), comm=comm)
    return outs[0] if comm is None else (outs[0], got)


def out_proj(o2, w, x, gate, name):
    S, K = o2.shape
    N = w.shape[1]
    tm, tn = 1024, 1024

    def body(a_ref, b_ref, x_ref, g_ref, y_ref, xn_ref):
        y = _nn(a_ref[...], b_ref[...])
        y_ref[...] = y
        xn_ref[...] = x_ref[...] + g_ref[...] * y

    return pl.pallas_call(
        body, name=name, grid=(S // tm, N // tn),
        in_specs=[pl.BlockSpec((tm, K), lambda i, j: (i, 0)), pl.BlockSpec((K, tn), lambda i, j: (0, j)),
                  pl.BlockSpec((tm, tn), lambda i, j: (i, j)), pl.BlockSpec((1, tn), lambda i, j: (0, j))],
        out_specs=[pl.BlockSpec((tm, tn), lambda i, j: (i, j))] * 2,
        out_shape=[jax.ShapeDtypeStruct((S, N), f32)] * 2,
        compiler_params=_params(("parallel", "parallel")),
    )(o2, w, x, gate)


def ln_mod(x, nw, scale, shift):
    S = x.shape[0]
    tm = 512

    def body(x_ref, nw_ref, sc_ref, sh_ref, h_ref):
        xv = x_ref[...]
        r = lax.rsqrt(jnp.mean(xv * xv, axis=-1, keepdims=True) + EPS)
        h_ref[...] = ((xv * r) * nw_ref[...] * (1.0 + sc_ref[...]) + sh_ref[...]).astype(bf16)

    vec = pl.BlockSpec((1, D), lambda i: (0, 0))
    return pl.pallas_call(
        body, name="ln_mod", grid=(S // tm,),
        in_specs=[pl.BlockSpec((tm, D), lambda i: (i, 0)), vec, vec, vec],
        out_specs=pl.BlockSpec((tm, D), lambda i: (i, 0)),
        out_shape=jax.ShapeDtypeStruct((S, D), bf16),
        compiler_params=_params(("parallel",)),
    )(x, nw, scale, shift)


def ln_mod_bwd(x, nw, scale, dh, dxres):
    S = x.shape[0]
    tm = 512
    nb = S // tm

    def body(x_ref, nw_ref, sc_ref, dh_ref, dr_ref, dx_ref, st_ref):
        i = pl.program_id(0)
        xv = x_ref[...]
        r = lax.rsqrt(jnp.mean(xv * xv, axis=-1, keepdims=True) + EPS)
        xn = xv * r
        dh = dh_ref[...]
        dxn = dh * (nw_ref[...] * (1.0 + sc_ref[...]))
        dx_ref[...] = dr_ref[...] + r * (dxn - xn * jnp.mean(dxn * xn, axis=-1, keepdims=True))
        p1 = jnp.sum(dh * xn, axis=0, keepdims=True)
        p2 = jnp.sum(dh, axis=0, keepdims=True)
        upd = jnp.concatenate([p1, p1, p2, jnp.zeros((5, D), f32)], axis=0)

        @pl.when(i == 0)
        def _():
            st_ref[...] = upd

        @pl.when(i > 0)
        def _():
            st_ref[...] += upd

        @pl.when(i == nb - 1)
        def _():
            st_ref[0:1, :] = st_ref[0:1, :] * (1.0 + sc_ref[...])
            st_ref[1:2, :] = st_ref[1:2, :] * nw_ref[...]

    vec = pl.BlockSpec((1, D), lambda i: (0, 0))
    tile = pl.BlockSpec((tm, D), lambda i: (i, 0))
    return pl.pallas_call(
        body, name="ln_mod_bwd", grid=(S // tm,),
        in_specs=[tile, vec, vec, tile, tile],
        out_specs=[tile, pl.BlockSpec((8, D), lambda i: (0, 0))],
        out_shape=[jax.ShapeDtypeStruct((S, D), f32), jax.ShapeDtypeStruct((8, D), f32)],
        compiler_params=_params(("arbitrary",)),
    )(x, nw, scale, dh, dxres)


def final_loss(x, fw, tgt):
    S = x.shape[0]
    tm = 512

    def body(x_ref, w_ref, t_ref, dx_ref, st_ref):
        i = pl.program_id(0)
        xv = x_ref[...]
        r = lax.rsqrt(jnp.mean(xv * xv, axis=-1, keepdims=True) + EPS)
        xn = xv * r
        err = xn * w_ref[...] - t_ref[...]
        dy = err * (1.0 / D)
        dxn = dy * w_ref[...]
        dx_ref[...] = r * (dxn - xn * jnp.mean(dxn * xn, axis=-1, keepdims=True))
        p1 = jnp.sum(dy * xn, axis=0, keepdims=True)
        p2 = jnp.sum(err * err, axis=0, keepdims=True) * (0.5 / D)
        upd = jnp.concatenate([p1, p2, jnp.zeros((6, D), f32)], axis=0)

        @pl.when(i == 0)
        def _():
            st_ref[...] = upd

        @pl.when(i > 0)
        def _():
            st_ref[...] += upd

    tile = pl.BlockSpec((tm, D), lambda i: (i, 0))
    return pl.pallas_call(
        body, name="final_loss", grid=(S // tm,),
        in_specs=[tile, pl.BlockSpec((1, D), lambda i: (0, 0)), tile],
        out_specs=[tile, pl.BlockSpec((8, D), lambda i: (0, 0))],
        out_shape=[jax.ShapeDtypeStruct((S, D), f32), jax.ShapeDtypeStruct((8, D), f32)],
        compiler_params=_params(("arbitrary",)),
    )(x, fw, tgt)


def gate_bwd(dx, y, gate):
    S = dx.shape[0]
    tm = 512

    def body(dx_ref, y_ref, g_ref, dy_ref, st_ref):
        i = pl.program_id(0)
        dxv = dx_ref[...]
        dy_ref[...] = (g_ref[...] * dxv).astype(bf16)
        upd = jnp.concatenate([jnp.sum(dxv * y_ref[...], axis=0, keepdims=True), jnp.zeros((7, D), f32)], axis=0)

        @pl.when(i == 0)
        def _():
            st_ref[...] = upd

        @pl.when(i > 0)
        def _():
            st_ref[...] += upd

    tile = pl.BlockSpec((tm, D), lambda i: (i, 0))
    return pl.pallas_call(
        body, name="gate_bwd", grid=(S // tm,),
        in_specs=[tile, tile, pl.BlockSpec((1, D), lambda i: (0, 0))],
        out_specs=[tile, pl.BlockSpec((8, D), lambda i: (0, 0))],
        out_shape=[jax.ShapeDtypeStruct((S, D), bf16), jax.ShapeDtypeStruct((8, D), f32)],
        compiler_params=_params(("arbitrary",)),
    )(dx, y, gate)


def _chunk_mats(tm):
    r, c = _iota((tm, tm), 0), _iota((tm, tm), 1)
    same = jnp.right_shift(r, 6) == jnp.right_shift(c, 6)
    ltri = jnp.where(same & (c <= r), 1.0, 0.0).astype(f32)
    utri = jnp.where(same & (c >= r), 1.0, 0.0).astype(f32)
    bsame = jnp.where(same, 1.0, 0.0).astype(f32)
    return ltri, utri, bsame


def _gdn_scalars(ba, alog, dtb, ltri, bsame):
    beta = _sigmoid(ba[:, 0:16])
    u = ba[:, 16:32] + dtb
    neg_a = -jnp.exp(alog)
    g = neg_a * _softplus(u)
    gc = _nn(ltri, g, HI)
    glast = _nn(bsame, g, HI)
    return beta, u, neg_a, g, gc, glast


def _conv_taps(p_ref, halo_ref, first, gi, ext_scr):
    cs = slice(gi * 128, (gi + 1) * 128)
    tm = p_ref.shape[0]
    cur = p_ref[:, cs]
    ext_scr[gi, 0:8, :] = jnp.where(first, 0.0, halo_ref[:, cs])
    ext_scr[gi, 8:, :] = cur
    return [cur] + [ext_scr[gi, 8 - s:8 - s + tm, :] for s in range(1, CONV_K)]


def _conv_scratch(tm):
    return pltpu.VMEM((G_CONV // 128, tm + 8, 128), f32)


def _conv(taps, w):
    cv = taps[0] * w[3:4]
    for s in range(1, CONV_K):
        cv = cv + taps[s] * w[3 - s:4 - s]
    return cv


def _l2n(x):
    return x * lax.rsqrt(jnp.sum(x * x, axis=-1, keepdims=True) + EPS)


def _gdn_in_specs(tm, S):
    nb8 = tm // 8
    return [pl.BlockSpec((tm, G_CONV), lambda i: (i, 0)),
            pl.BlockSpec((8, G_CONV), lambda i: (jnp.maximum(i * nb8 - 1, 0), 0)),
            pl.BlockSpec((tm, 128), lambda i: (i, G_BA0 // 128))]


def gdn_pre(proj, conv_w, alog, dtb):
    S = proj.shape[0]
    tm = 256
    nch = tm // CHUNK

    def body(p_ref, halo_ref, ba_ref, w_ref, al_ref, dt_ref,
             q_ref, k_ref, kb_ref, kbg_ref, vb_ref, qd_ref, kd_ref, d_ref, gl_ref, ext_scr):
        first = pl.program_id(0) == 0
        ltri, _, bsame = _chunk_mats(tm)
        beta, _, _, _, gc, glast = _gdn_scalars(ba_ref[...], al_ref[...], dt_ref[...], ltri, bsame)
        eg, ek, egl = jnp.exp(gc), jnp.exp(glast - gc), jnp.exp(glast)
        eye = jnp.where(_iota((16, 16), 0) == _iota((16, 16), 1), 1.0, 0.0).astype(f32)
        gct = _nt(eye, gc, HI)
        low = _iota((CHUNK, CHUNK), 0) >= _iota((CHUNK, CHUNK), 1)

        def act(gi):
            return _silu(_conv(_conv_taps(p_ref, halo_ref, first, gi, ext_scr), w_ref[:, gi * 128:(gi + 1) * 128]))

        for j in range(GQK_H):
            js = slice(j * 128, (j + 1) * 128)
            qn = _l2n(act(j)) * (GHD ** -0.5)
            kn = _l2n(act(GQK_H + j))
            q_ref[:, js] = qn.astype(bf16)
            k_ref[:, js] = kn.astype(bf16)
            for e in range(2):
                h = 2 * j + e
                hs = slice(h * 128, (h + 1) * 128)
                v = act(2 * GQK_H + h)
                bh, egh, ekh = beta[:, h:h + 1], eg[:, h:h + 1], ek[:, h:h + 1]
                kbv = kn * bh
                kb_ref[:, hs] = kbv.astype(bf16)
                kbg_ref[:, hs] = (kbv * egh).astype(bf16)
                vb_ref[:, hs] = (v * bh).astype(bf16)
                qd_ref[:, hs] = (qn * egh).astype(bf16)
                kd_ref[:, hs] = (kn * ekh).astype(bf16)
                for c in range(nch):
                    rs = slice(c * CHUNK, (c + 1) * CHUNK)
                    diff = gc[rs, h:h + 1] - gct[h:h + 1, rs]
                    d_ref[rs, h * CHUNK:(h + 1) * CHUNK] = jnp.where(low, jnp.exp(jnp.where(low, diff, 0.0)), 0.0)
                    gl_ref[c * 8:(c + 1) * 8, hs] = jnp.broadcast_to(egl[c * CHUNK:c * CHUNK + 8, h:h + 1], (8, 128))

    full = lambda shape: pl.BlockSpec(shape, lambda i: (0, 0))
    t1 = pl.BlockSpec((tm, 1024), lambda i: (i, 0))
    t2 = pl.BlockSpec((tm, 2048), lambda i: (i, 0))
    sd = jax.ShapeDtypeStruct
    return pl.pallas_call(
        body, name="gdn_pre", grid=(S // tm,),
        in_specs=_gdn_in_specs(tm, S) + [full((CONV_K, G_CONV)), full((1, 16)), full((1, 16))],
        out_specs=[t1, t1, t2, t2, t2, t2, t2, t1, pl.BlockSpec((tm // 8, 2048), lambda i: (i, 0))],
        out_shape=[sd((S, 1024), bf16)] * 2 + [sd((S, 2048), bf16)] * 5 + [sd((S, 1024), f32), sd((S // 8, 2048), f32)],
        scratch_shapes=[_conv_scratch(tm)],
        compiler_params=_params(("parallel",)),
    )(proj, proj, proj, conv_w, alog, dtb)


def _bnn(a, b):
    return lax.dot_general(a, b, (((2,), (1,)), ((0,), (0,))), preferred_element_type=f32)


def _bnt(a, b):
    return lax.dot_general(a, b, (((2,), (2,)), ((0,), (0,))), preferred_element_type=f32)


def _btn(a, b):
    return lax.dot_general(a, b, (((1,), (1,)), ((0,), (0,))), preferred_element_type=f32)


def _split(a):
    hi = a.astype(bf16)
    return hi, (a - hi.astype(f32)).astype(bf16)


def _cat3(h, l, axis, lhs):
    return jnp.concatenate([h, h, l] if lhs else [h, l, h], axis=axis)


def _tri_inv_b(L):
    eye = jnp.where(_iota((1, CHUNK, CHUNK), 1) == _iota((1, CHUNK, CHUNK), 2), 1.0, 0.0).astype(f32)
    P = -L
    T = eye + P
    ph, pl_ = _split(P)
    for _ in range(5):
        P = _bnn(_cat3(ph, pl_, 2, True), _cat3(ph, pl_, 1, False))
        ph, pl_ = _split(P)
        th, tl = _split(T)
        T = T + _bnn(_cat3(th, tl, 2, True), _cat3(ph, pl_, 1, False))
    return T


GTB = 512
GQH_FWD, GQH_BWD = 1, 2


def _gdn_slices(ncb, gnv):
    pairs = [(c, e) for c in range(ncb) for e in range(gnv)]
    rs = lambda c: slice(c * CHUNK, (c + 1) * CHUNK)
    cs = lambda e: slice(e * 128, (e + 1) * 128)
    ds_ = lambda e: slice(e * CHUNK, (e + 1) * CHUNK)
    ks = lambda e: slice((e // 2) * 128, (e // 2 + 1) * 128)
    return pairs, rs, cs, ds_, ks


def gdn_fwd(q, k, kb, kbg, vb, qd, kd, dm, gl8, comm=None):
    S = q.shape[0]
    nb, ncb = S // GTB, GTB // CHUNK
    GQH, GNV = GQH_FWD, 2 * GQH_FWD
    pairs, rs, cs, ds_, ks = _gdn_slices(ncb, GNV)

    def body(q_ref, k_ref, kb_ref, kbg_ref, vb_ref, qd_ref, kd_ref, d_ref, gl_ref,
             o_ref, w_ref, at_ref, t_ref, vn_ref, st_ref, state, u_scr):
        @pl.when(pl.program_id(1) == 0)
        def _():
            state[...] = jnp.zeros_like(state)

        stk = lambda ref, lanes: jnp.stack([ref[rs(c), lanes(e)] for c, e in pairs])
        kq = stk(k_ref, ks)
        dmat = stk(d_ref, ds_)
        strict = _iota((1, CHUNK, CHUNK), 1) > _iota((1, CHUNK, CHUNK), 2)
        T = _tri_inv_b(jnp.where(strict, _bnt(stk(kb_ref, cs), kq) * dmat, 0.0))
        tb = T.astype(bf16)
        u_scr[...] = _bnn(tb, stk(vb_ref, cs))
        wb = _bnn(tb, stk(kbg_ref, cs)).astype(bf16)
        per_qk = lambda ref: jnp.stack([ref[rs(c), ks(e)] for c, e in pairs if e % 2 == 0])
        qk = _bnt(per_qk(q_ref), per_qk(k_ref))
        for b, (c, e) in enumerate(pairs):
            w_ref[rs(c), cs(e)] = wb[b]
            at_ref[rs(c), ds_(e)] = (qk[b // 2] * dmat[b]).astype(bf16)
            t_ref[rs(c), ds_(e)] = T[b]
        for b, (c, e) in enumerate(pairs):
            sb = state[e].astype(bf16)
            vnb = (u_scr[b] - _nn(w_ref[rs(c), cs(e)], sb)).astype(bf16)
            o_ref[rs(c), cs(e)] = _nn(qd_ref[rs(c), cs(e)], sb) + _nn(at_ref[rs(c), ds_(e)], vnb)
            st_ref[c * 128:(c + 1) * 128, cs(e)] = sb
            state[e] = state[e] * gl_ref[c * 8:c * 8 + 1, cs(e)] + _tn(kd_ref[rs(c), cs(e)], vnb)
            vn_ref[rs(c), cs(e)] = vnb

    b1 = pl.BlockSpec((GTB, 128 * GQH), lambda j, i: (i, j))
    b2 = pl.BlockSpec((GTB, 256 * GQH), lambda j, i: (i, j))
    sd = jax.ShapeDtypeStruct
    return _call(
        body, name="gdn_fwd", grid=(GQK_H // GQH, nb),
        in_specs=[b1, b1, b2, b2, b2, b2, b2, b1, pl.BlockSpec((GTB // 8, 256 * GQH), lambda j, i: (i, j))],
        out_specs=[b2, b2, b1, b1, b2, pl.BlockSpec((ncb * 128, 256 * GQH), lambda j, i: (i, j))],
        out_shape=[sd((S, 2048), f32), sd((S, 2048), bf16), sd((S, 1024), bf16), sd((S, 1024), f32),
                   sd((S, 2048), bf16), sd((S // CHUNK * 128, 2048), bf16)],
        scratch_shapes=[pltpu.VMEM((GNV, 128, 128), f32), pltpu.VMEM((GNV * ncb, CHUNK, 128), f32)],
        sem=("parallel", "arbitrary"), args=(q, k, kb, kbg, vb, qd, kd, dm, gl8), comm=comm)


def gdn_bwd(do, q, k, kb, kbg, vb, qd, kd, dm, gl8, w, at, T, vn, st, comm=None):
    S = q.shape[0]
    nb, ncb = S // GTB, GTB // CHUNK
    GQH, GNV = GQH_BWD, 2 * GQH_BWD
    pairs, rs, cs, ds_, ks = _gdn_slices(ncb, GNV)

    def body(do_ref, q_ref, k_ref, kb_ref, kbg_ref, vb_ref, qd_ref, kd_ref, d_ref, gl_ref, w_ref, at_ref, t_ref, vn_ref, st_ref,
             dq_ref, dk_ref, dkb_ref, dkbg_ref, dvb_ref, dqd_ref, dkd_ref, dgc_ref, dstate, dvn_scr, dw_scr, dat_scr, dgl_scr):
        @pl.when(pl.program_id(1) == 0)
        def _():
            dstate[...] = jnp.zeros_like(dstate)

        for b, (c, e) in reversed(list(enumerate(pairs))):
            dob = do_ref[rs(c), cs(e)].astype(bf16)
            sb = st_ref[c * 128:(c + 1) * 128, cs(e)]
            vnb = vn_ref[rs(c), cs(e)]
            gl = gl_ref[c * 8:c * 8 + 1, cs(e)]
            dS = dstate[e]
            dsb = dS.astype(bf16)
            dvnb = (_tn(at_ref[rs(c), ds_(e)], dob) + _nn(kd_ref[rs(c), cs(e)], dsb)).astype(bf16)
            dvn_scr[b] = dvnb
            dat_scr[b] = _nt(dob, vnb)
            dqd_ref[rs(c), cs(e)] = _nt(dob, sb)
            dkd_ref[rs(c), cs(e)] = _nt(vnb, dsb)
            dw_scr[b] = (-_nt(dvnb, sb)).astype(bf16)
            dgl = jnp.sum(jnp.sum(dS * sb.astype(f32), axis=1, keepdims=True), axis=0, keepdims=True)
            dgl_scr[b] = jnp.broadcast_to(dgl * gl, (8, 128))
            dstate[e] = gl * dS + _tn(qd_ref[rs(c), cs(e)], dob) - _tn(w_ref[rs(c), cs(e)], dvnb)

        stk = lambda ref, lanes: jnp.stack([ref[rs(c), lanes(e)] for c, e in pairs])
        kq, qq = stk(k_ref, ks), stk(q_ref, ks)
        kbb = stk(kb_ref, cs)
        Tm = stk(t_ref, ds_)
        tb = Tm.astype(bf16)
        dvn, dw = dvn_scr[...], dw_scr[...]
        dT = _bnt(dvn, stk(vb_ref, cs)) + _bnt(dw, stk(kbg_ref, cs))
        dvb, dkbg = _btn(tb, dvn), _btn(tb, dw)
        th, tl = _split(Tm)
        xh, xl = _split(_bnt(_cat3(*_split(dT), 2, True), _cat3(th, tl, 2, False)))
        dL = -_btn(_cat3(th, tl, 1, True), _cat3(xh, xl, 1, False))
        dmat = stk(d_ref, ds_)
        strict = _iota((1, CHUNK, CHUNK), 1) > _iota((1, CHUNK, CHUNK), 2)
        dA = jnp.where(strict, dL * dmat, 0.0)
        dB = dat_scr[...] * dmat
        dAb, dBb = dA.astype(bf16), dB.astype(bf16)
        dkb = _bnn(dAb, kq)
        dkc = _btn(dAb, kbb) + _btn(dBb, qq)
        dqc = _bnn(dBb, kq)
        M = dA * _bnt(kbb, kq) + dB * _bnt(qq, kq)
        mh, ml = _split(M)
        colsum = _btn(jnp.concatenate([mh, ml], axis=1), jnp.ones((GNV * ncb, 2 * CHUNK, 128), bf16))
        lastrow = _iota((1, CHUNK, 128), 1) == CHUNK - 1
        for b, (c, e) in enumerate(pairs):
            dvb_ref[rs(c), cs(e)] = dvb[b]
            dkbg_ref[rs(c), cs(e)] = dkbg[b]
            dkb_ref[rs(c), cs(e)] = dkb[b]
            dgc_ref[rs(c), cs(e)] = (jnp.sum(M[b], axis=1, keepdims=True) - colsum[b]
                                     + jnp.where(lastrow[0], dgl_scr[b][0:1, :], 0.0))
        for b, (c, e) in enumerate(pairs):
            if e % 2 == 0:
                dq_ref[rs(c), ks(e)] = dqc[b] + dqc[b + 1]
                dk_ref[rs(c), ks(e)] = dkc[b] + dkc[b + 1]

    b1 = pl.BlockSpec((GTB, 128 * GQH), lambda j, i: (nb - 1 - i, j))
    b2 = pl.BlockSpec((GTB, 256 * GQH), lambda j, i: (nb - 1 - i, j))
    sd = jax.ShapeDtypeStruct
    return _call(
        body, name="gdn_bwd", grid=(GQK_H // GQH, nb),
        in_specs=[b2, b1, b1, b2, b2, b2, b2, b2, b1, pl.BlockSpec((GTB // 8, 256 * GQH), lambda j, i: (nb - 1 - i, j)),
                  b2, b1, b1, b2, pl.BlockSpec((ncb * 128, 256 * GQH), lambda j, i: (nb - 1 - i, j))],
        out_specs=[b1, b1, b2, b2, b2, b2, b2, b2],
        out_shape=[sd((S, 1024), f32)] * 2 + [sd((S, 2048), f32)] * 6,
        scratch_shapes=[pltpu.VMEM((GNV, 128, 128), f32), pltpu.VMEM((GNV * ncb, CHUNK, 128), bf16),
                        pltpu.VMEM((GNV * ncb, CHUNK, 128), bf16), pltpu.VMEM((GNV * ncb, CHUNK, CHUNK), f32),
                        pltpu.VMEM((GNV * ncb, 8, 128), f32)],
        sem=("parallel", "arbitrary"), args=(do, q, k, kb, kbg, vb, qd, kd, dm, gl8, w, at, T, vn, st), comm=comm)


def gdn_onorm(o, proj, nw):
    S = o.shape[0]
    tm = 256

    def body(o_ref, z_ref, nw_ref, o2_ref):
        for h in range(GV_H):
            hs = slice(h * 128, (h + 1) * 128)
            oh = o_ref[:, hs]
            r = lax.rsqrt(jnp.mean(oh * oh, axis=-1, keepdims=True) + EPS)
            o2_ref[:, hs] = (((oh * r) * nw_ref[...]) * _silu(z_ref[:, hs])).astype(bf16)

    t2 = pl.BlockSpec((tm, 2048), lambda i: (i, 0))
    return pl.pallas_call(
        body, name="gdn_onorm", grid=(S // tm,),
        in_specs=[t2, pl.BlockSpec((tm, 2048), lambda i: (i, G_Z0 // 2048)), pl.BlockSpec((1, 128), lambda i: (0, 0))],
        out_specs=t2, out_shape=jax.ShapeDtypeStruct((S, 2048), bf16),
        compiler_params=_params(("parallel",)),
    )(o, proj, nw)


def gdn_onorm_bwd(dy, w_out, o, proj, nw):
    S = o.shape[0]
    tm = 256

    def body(dy_ref, w_ref, o_ref, z_ref, nw_ref, do_ref, dz_ref, st_ref):
        i = pl.program_id(0)
        d_all = _nt(dy_ref[...], w_ref[...])
        acc = jnp.zeros((1, 128), f32)
        for h in range(GV_H):
            hs = slice(h * 128, (h + 1) * 128)
            oh, z, d2 = o_ref[:, hs], z_ref[:, hs], d_all[:, hs]
            r = lax.rsqrt(jnp.mean(oh * oh, axis=-1, keepdims=True) + EPS)
            on = oh * r
            dt = d2 * _silu(z)
            dz_ref[:, hs] = (d2 * (on * nw_ref[...]) * _dsilu(z)).astype(bf16)
            don = dt * nw_ref[...]
            acc = acc + jnp.sum(dt * on, axis=0, keepdims=True)
            do_ref[:, hs] = r * (don - on * jnp.mean(don * on, axis=-1, keepdims=True))
        upd = jnp.concatenate([acc, jnp.zeros((7, 128), f32)], axis=0)

        @pl.when(i == 0)
        def _():
            st_ref[...] = upd

        @pl.when(i > 0)
        def _():
            st_ref[...] += upd

    t2 = pl.BlockSpec((tm, 2048), lambda i: (i, 0))
    sd = jax.ShapeDtypeStruct
    return pl.pallas_call(
        body, name="gdn_onorm_bwd", grid=(S // tm,),
        in_specs=[pl.BlockSpec((tm, D), lambda i: (i, 0)), pl.BlockSpec(w_out.shape, lambda i: (0, 0)), t2,
                  pl.BlockSpec((tm, 2048), lambda i: (i, G_Z0 // 2048)), pl.BlockSpec((1, 128), lambda i: (0, 0))],
        out_specs=[t2, t2, pl.BlockSpec((8, 128), lambda i: (0, 0))],
        out_shape=[sd((S, 2048), f32), sd((S, 2048), bf16), sd((8, 128), f32)],
        compiler_params=_params(("arbitrary",)),
    )(dy, w_out, o, proj, nw)


def gdn_pre_bwd(proj, conv_w, alog, dtb, dq, dk, dkb, dkbg, dvb, dqd, dkd, dgcd):
    S = proj.shape[0]
    tm = 128

    def body(p_ref, halo_ref, ba_ref, w_ref, al_ref, dt_ref, dq_ref, dk_ref, dkb_ref, dkbg_ref, dvb_ref, dqd_ref, dkd_ref, dgc_ref,
             dcv_ref, dba_ref, st_ref, ext_scr):
        i = pl.program_id(0)
        first = i == 0
        ltri, utri, bsame = _chunk_mats(tm)
        beta, u, neg_a, g, gc, glast = _gdn_scalars(ba_ref[...], al_ref[...], dt_ref[...], ltri, bsame)
        eg, ek = jnp.exp(gc), jnp.exp(glast - gc)
        lane16 = _iota((tm, 16), 1)
        dgc_all = jnp.zeros((tm, 16), f32)
        rkd_all = jnp.zeros((tm, 16), f32)
        dbeta_all = jnp.zeros((tm, 16), f32)

        def pre(gi):
            return _conv(_conv_taps(p_ref, halo_ref, first, gi, ext_scr), w_ref[:, gi * 128:(gi + 1) * 128])

        def l2n_bwd(xt, dy):
            r = lax.rsqrt(jnp.sum(xt * xt, axis=-1, keepdims=True) + EPS)
            y = xt * r
            return r * (dy - y * jnp.sum(dy * y, axis=-1, keepdims=True))

        for j in range(GQK_H):
            js = slice(j * 128, (j + 1) * 128)
            cvq, cvk = pre(j), pre(GQK_H + j)
            qt, kt = _silu(cvq), _silu(cvk)
            qn = _l2n(qt) * (GHD ** -0.5)
            kn = _l2n(kt)
            dq_tot, dk_tot = dq_ref[:, js], dk_ref[:, js]
            for e in range(2):
                h = 2 * j + e
                hs = slice(h * 128, (h + 1) * 128)
                gv = 2 * GQK_H + h
                cvv = pre(gv)
                v = _silu(cvv)
                bh, egh, ekh = beta[:, h:h + 1], eg[:, h:h + 1], ek[:, h:h + 1]
                dkbg, dkd, dqd, dvb = dkbg_ref[:, hs], dkd_ref[:, hs], dqd_ref[:, hs], dvb_ref[:, hs]
                dkb_t = dkb_ref[:, hs] + dkbg * egh
                dk_tot = dk_tot + dkb_t * bh + dkd * ekh
                dq_tot = dq_tot + dqd * egh
                dcv_ref[:, gv * 128:(gv + 1) * 128] = (dvb * bh) * _dsilu(cvv)
                dbeta = jnp.sum(dkb_t * kn, axis=-1, keepdims=True) + jnp.sum(dvb * v, axis=-1, keepdims=True)
                rkd = jnp.sum(dkd * (kn * ekh), axis=-1, keepdims=True)
                dgc = (dgc_ref[:, hs][:, 0:1] + jnp.sum(dkbg * (kn * bh * egh), axis=-1, keepdims=True)
                       + jnp.sum(dqd * (qn * egh), axis=-1, keepdims=True) - rkd)
                sel = lane16 == h
                dgc_all = dgc_all + jnp.where(sel, dgc, 0.0)
                rkd_all = rkd_all + jnp.where(sel, rkd, 0.0)
                dbeta_all = dbeta_all + jnp.where(sel, dbeta, 0.0)
            dcv_ref[:, js] = l2n_bwd(qt, dq_tot * (GHD ** -0.5)) * _dsilu(cvq)
            ks = slice((GQK_H + j) * 128, (GQK_H + j + 1) * 128)
            dcv_ref[:, ks] = l2n_bwd(kt, dk_tot) * _dsilu(cvk)

        islast = jnp.bitwise_and(_iota((tm, 16), 0), CHUNK - 1) == CHUNK - 1
        dgc_all = dgc_all + jnp.where(islast, _nn(bsame, rkd_all, HI), 0.0)
        dg = _nn(utri, dgc_all, HI)
        da = dg * neg_a * _sigmoid(u)
        db = dbeta_all * beta * (1.0 - beta)
        r16, c128 = _iota((16, 128), 0), _iota((16, 128), 1)
        pb = jnp.where(c128 == r16, 1.0, 0.0).astype(f32)
        pa = jnp.where(c128 == r16 + 16, 1.0, 0.0).astype(f32)
        dba_ref[...] = _nn(db, pb, HI) + _nn(da, pa, HI)
        upd = jnp.concatenate([jnp.sum(dg * g, axis=0, keepdims=True), jnp.sum(da, axis=0, keepdims=True),
                               jnp.zeros((6, 16), f32)], axis=0)

        @pl.when(i == 0)
        def _():
            st_ref[...] = upd

        @pl.when(i > 0)
        def _():
            st_ref[...] += upd

    full = lambda shape: pl.BlockSpec(shape, lambda i: (0, 0))
    t1 = pl.BlockSpec((tm, 1024), lambda i: (i, 0))
    t2 = pl.BlockSpec((tm, 2048), lambda i: (i, 0))
    sd = jax.ShapeDtypeStruct
    return pl.pallas_call(
        body, name="gdn_pre_bwd", grid=(S // tm,),
        in_specs=_gdn_in_specs(tm, S) + [full((CONV_K, G_CONV)), full((1, 16)), full((1, 16)), t1, t1] + [t2] * 6,
        out_specs=[pl.BlockSpec((tm, G_CONV), lambda i: (i, 0)), pl.BlockSpec((tm, 128), lambda i: (i, 0)), full((8, 16))],
        out_shape=[sd((S, G_CONV), f32), sd((S, 128), f32), sd((8, 16), f32)],
        scratch_shapes=[_conv_scratch(tm)],
        compiler_params=_params(("arbitrary",)),
    )(proj, proj, proj, conv_w, alog, dtb, dq, dk, dkb, dkbg, dvb, dqd, dkd, dgcd)


def gdn_conv_bwd(proj, conv_w, dcv, dz, dba):
    S = proj.shape[0]
    tm = 256
    nb, nb8 = S // tm, tm // 8

    def body(p_ref, halo_ref, w_ref, dcv_ref, nxt_ref, dz_ref, dba_ref, dp_ref, dw_ref, ext_scr, nxt_scr):
        i = pl.program_id(0)
        first, last = i == 0, i == nb - 1
        for gi in range(G_CONV // 128):
            cs = slice(gi * 128, (gi + 1) * 128)
            taps = _conv_taps(p_ref, halo_ref, first, gi, ext_scr)
            cur = dcv_ref[:, cs]
            nxt_scr[gi, 0:tm, :] = cur
            nxt_scr[gi, tm:, :] = jnp.where(last, 0.0, nxt_ref[:, cs])
            w = w_ref[:, cs]
            dp = cur * w[3:4]
            rows = [jnp.sum(cur * taps[3 - kk], axis=0, keepdims=True) for kk in range(CONV_K)]
            for s in range(1, CONV_K):
                dp = dp + nxt_scr[gi, s:s + tm, :] * w[3 - s:4 - s]
            dp_ref[:, cs] = dp.astype(bf16)
            upd = jnp.concatenate(rows + [jnp.zeros((4, 128), f32)], axis=0)

            @pl.when(first)
            def _():
                dw_ref[:, cs] = upd

            @pl.when(i > 0)
            def _():
                dw_ref[:, cs] += upd

        dp_ref[:, G_Z0:G_BA0] = dz_ref[...]
        dp_ref[:, G_BA0:G_INP] = dba_ref[...].astype(bf16)

    sd = jax.ShapeDtypeStruct
    return pl.pallas_call(
        body, name="gdn_conv_bwd", grid=(nb,),
        in_specs=[pl.BlockSpec((tm, G_CONV), lambda i: (i, 0)),
                  pl.BlockSpec((8, G_CONV), lambda i: (jnp.maximum(i * nb8 - 1, 0), 0)),
                  pl.BlockSpec((CONV_K, G_CONV), lambda i: (0, 0)),
                  pl.BlockSpec((tm, G_CONV), lambda i: (i, 0)),
                  pl.BlockSpec((8, G_CONV), lambda i: (jnp.minimum((i + 1) * nb8, S // 8 - 1), 0)),
                  pl.BlockSpec((tm, 2048), lambda i: (i, 0)), pl.BlockSpec((tm, 128), lambda i: (i, 0))],
        out_specs=[pl.BlockSpec((tm, G_INP), lambda i: (i, 0)), pl.BlockSpec((8, G_CONV), lambda i: (0, 0))],
        out_shape=[sd((S, G_INP), bf16), sd((8, G_CONV), f32)],
        scratch_shapes=[_conv_scratch(tm), _conv_scratch(tm)],
        compiler_params=_params(("arbitrary",)),
    )(proj, proj, conv_w, dcv, dcv, dz, dba)


def _half_mean(t, lo_half):
    m0 = jnp.sum(jnp.where(lo_half, t, 0.0), axis=-1, keepdims=True)
    m1 = jnp.sum(jnp.where(lo_half, 0.0, t), axis=-1, keepdims=True)
    return jnp.where(lo_half, m0, m1) * (1.0 / F_HD)


def _split3(c):
    hi = c.astype(bf16).astype(f32)
    mid = (c - hi).astype(bf16).astype(f32)
    lo = (c - hi - mid).astype(bf16).astype(f32)
    return hi, mid, lo


def fox_pre(proj, fbias, qw2, kw2):
    S = proj.shape[0]
    tm = 256

    def body(q_ref, k_ref, v_ref, f_ref, fb_ref, qw_ref, kw_ref, qa_ref, ka_ref, vb_ref, carry):
        @pl.when(pl.program_id(0) == 0)
        def _():
            carry[...] = jnp.zeros_like(carry)

        logf = -_softplus(-(f_ref[:, 0:16] + fb_ref[...]))
        ltri = jnp.where(_iota((tm, tm), 1) <= _iota((tm, tm), 0), 1.0, 0.0).astype(f32)
        cum = _nn(ltri, logf, HI) + carry[0:1, :]
        carry[0:1, :] = cum[tm - 1:tm, :]
        lane = _iota((tm, 128), 1)
        lo_half = lane < F_HD
        for p in range(F_H // 2):
            ps = slice(p * 128, (p + 1) * 128)
            for src, w_ref, dst, is_q in ((q_ref, qw_ref, qa_ref, True), (k_ref, kw_ref, ka_ref, False)):
                x = src[:, ps]
                xn = x * lax.rsqrt(_half_mean(x * x, lo_half) + EPS) * w_ref[...]
                if is_q:
                    xn = xn * (F_HD ** -0.5)
                for e in range(2):
                    h = 2 * p + e
                    base = xn if e == 0 else pltpu.roll(xn, F_HD, 1)
                    hi, mid, lo = _split3(cum[:, h:h + 1])
                    pieces = jnp.where(lane == 64, hi, 0.0) + jnp.where(lane == 65, mid, 0.0) + jnp.where(lane == 66, lo, 0.0)
                    if is_q:
                        ext = pieces + jnp.where((lane >= 67) & (lane <= 69), 1.0, 0.0)
                    else:
                        ext = jnp.where((lane >= 64) & (lane <= 66), 1.0, 0.0) - pltpu.roll(pieces, 3, 1)
                    dst[:, h * 128:(h + 1) * 128] = jnp.where(lo_half, base, ext).astype(bf16)
        one = jnp.where(lane == F_HD, 1.0, 0.0)
        for p in range(F_H // 2):
            vv = v_ref[:, p * 128:(p + 1) * 128]
            vb_ref[:, (2 * p) * 128:(2 * p + 1) * 128] = jnp.where(lo_half, vv, one).astype(bf16)
            vb_ref[:, (2 * p + 1) * 128:(2 * p + 2) * 128] = jnp.where(lo_half, pltpu.roll(vv, F_HD, 1), one).astype(bf16)

    t1 = lambda c: pl.BlockSpec((tm, 1024), lambda i: (i, c))
    vec = lambda n: pl.BlockSpec((1, n), lambda i: (0, 0))
    sd = jax.ShapeDtypeStruct
    return pl.pallas_call(
        body, name="fox_pre", grid=(S // tm,),
        in_specs=[t1(0), t1(1), t1(2), pl.BlockSpec((tm, 128), lambda i: (i, F_F0 // 128)), vec(16), vec(128), vec(128)],
        out_specs=[pl.BlockSpec((tm, 2048), lambda i: (i, 0))] * 3,
        out_shape=[sd((S, 2048), bf16)] * 3,
        scratch_shapes=[pltpu.VMEM((8, 16), f32)],
        compiler_params=_params(("arbitrary",)),
    )(proj, proj, proj, proj, fbias, qw2, kw2)


FTQ = 512
FHS_FWD, FHS_BWD = 8, 8


def fox_attn(qa, ka, v, comm=None):
    S = qa.shape[0]
    nq = S // FTQ
    FHS = FHS_FWD

    live = [(i, j) for i in range(nq) for j in range(i + 1)]
    qi_tab = jnp.asarray([i for i, _ in live], jnp.int32)
    kj_tab = jnp.asarray([j for _, j in live], jnp.int32)

    def body(qi_ref, kj_ref, q_ref, k_ref, v_ref, o_ref, lse_ref, m_scr, acc_scr):
        t = pl.program_id(1)
        i, j = qi_ref[t], kj_ref[t]

        @pl.when(j == 0)
        def _():
            m_scr[...] = jnp.full_like(m_scr, NEG)
            acc_scr[...] = jnp.zeros_like(acc_scr)

        def step(diagonal):
            for e in range(FHS):
                es = slice(e * 128, (e + 1) * 128)
                s = _nt(q_ref[:, es], k_ref[:, es])
                if diagonal:
                    s = jnp.where(_iota((FTQ, FTQ), 0) >= _iota((FTQ, FTQ), 1), s, NEG)
                m_old = m_scr[e]
                m_new = jnp.maximum(m_old, jnp.max(s, axis=-1, keepdims=True))
                p = jnp.exp(s - m_new[:, 0:1])
                acc_scr[e] = acc_scr[e] * jnp.exp(m_old - m_new) + _nn(p.astype(bf16), v_ref[:, es])
                m_scr[e] = m_new

        pl.when(j < i)(functools.partial(step, False))

        @pl.when(j == i)
        def _():
            step(True)
            for e in range(FHS):
                vs = slice(e * F_HD, (e + 1) * F_HD)
                acc = acc_scr[e]
                l = acc[:, F_HD:F_HD + 1]
                o_ref[:, vs] = acc[:, 0:F_HD] / l
                lse_ref[:, vs] = m_scr[e][:, 0:F_HD] + jnp.log(l)

    sd = jax.ShapeDtypeStruct
    qo = pl.BlockSpec((FTQ, F_HD * FHS), lambda p, t, qi, kj: (qi[t], p))
    kv = pl.BlockSpec((FTQ, 128 * FHS), lambda p, t, qi, kj: (kj[t], p))
    return _call(
        body, name="fox_attn", grid=(F_H // FHS, len(live)),
        in_specs=[pl.BlockSpec((FTQ, 128 * FHS), lambda p, t, qi, kj: (qi[t], p)), kv, kv],
        out_specs=[qo, qo],
        out_shape=[sd((S, 1024), f32), sd((S, 1024), f32)],
        scratch_shapes=[pltpu.VMEM((FHS, FTQ, 128), f32), pltpu.VMEM((FHS, FTQ, 128), f32)],
        sem=("parallel", "arbitrary"), args=(qa, ka, v), comm=comm, prefetch=(qi_tab, kj_tab))


def fox_attn_bwd(qa, ka, v, do, lse, delta, comm=None):
    S = qa.shape[0]
    nq = S // FTQ
    FHS = FHS_BWD

    live = [(j, i) for j in range(nq) for i in range(j, nq)]
    kj_tab = jnp.asarray([j for j, _ in live], jnp.int32)
    qi_tab = jnp.asarray([i for _, i in live], jnp.int32)

    def body(kj_ref, qi_ref, q_ref, k_ref, v_ref, do_ref, lse_ref, dl_ref, dq_ref, dk_ref, dv_ref, dk_scr, dv_scr):
        t = pl.program_id(1)
        j, i = kj_ref[t], qi_ref[t]

        @pl.when(t == 0)
        def _():
            dq_ref[...] = jnp.zeros_like(dq_ref)

        @pl.when(i == j)
        def _():
            dk_scr[...] = jnp.zeros_like(dk_scr)
            dv_scr[...] = jnp.zeros_like(dv_scr)

        def step(diagonal):
            rows = pl.ds(pl.multiple_of(i * FTQ, FTQ), FTQ)
            for e in range(FHS):
                es, vs = slice(e * 128, (e + 1) * 128), slice(e * F_HD, (e + 1) * F_HD)
                qe, ke = q_ref[:, es], k_ref[:, es]
                dob = do_ref[:, vs]
                s = _nt(qe, ke)
                if diagonal:
                    s = jnp.where(_iota((FTQ, FTQ), 0) >= _iota((FTQ, FTQ), 1), s, NEG)
                p = jnp.exp(s - lse_ref[:, e * F_HD:e * F_HD + 1])
                ds = p * (_nt(dob, v_ref[:, e * 128:e * 128 + F_HD]) - dl_ref[:, e * F_HD:e * F_HD + 1])
                dsb = ds.astype(bf16)
                dv_scr[e] += _tn(dob, p.astype(bf16))
                dk_scr[e] += _tn(qe, dsb)
                dq_ref[rows, es] += _nn(dsb, ke)

        pl.when(i > j)(functools.partial(step, False))
        pl.when(i == j)(functools.partial(step, True))

        @pl.when(i == nq - 1)
        def _():
            for e in range(FHS):
                dk_ref[:, e * 128:(e + 1) * 128] = dk_scr[e].T
                dv_ref[:, e * F_HD:(e + 1) * F_HD] = dv_scr[e].T

    sd = jax.ShapeDtypeStruct
    qi = lambda w: pl.BlockSpec((FTQ, w * FHS), lambda p, t, kj_, qi_: (qi_[t], p))
    kj = lambda w: pl.BlockSpec((FTQ, w * FHS), lambda p, t, kj_, qi_: (kj_[t], p))
    return _call(
        body, name="fox_attn_bwd", grid=(F_H // FHS, len(live)),
        in_specs=[qi(128), kj(128), kj(128), qi(F_HD), qi(F_HD), qi(F_HD)],
        out_specs=[pl.BlockSpec((S, 128 * FHS), lambda p, t, kj_, qi_: (0, p)), kj(128), kj(F_HD)],
        out_shape=[sd((S, 2048), f32), sd((S, 2048), f32), sd((S, 1024), f32)],
        scratch_shapes=[pltpu.VMEM((FHS, 128, FTQ), f32), pltpu.VMEM((FHS, F_HD, FTQ), f32)],
        sem=("parallel", "arbitrary"), args=(qa, ka, v, do, lse, delta), comm=comm, prefetch=(kj_tab, qi_tab))


def fox_gate(o, proj):
    S = o.shape[0]
    tm = 512

    def body(o_ref, z_ref, o2_ref):
        o2_ref[...] = (o_ref[...] * _silu(z_ref[...])).astype(bf16)

    t = pl.BlockSpec((tm, 1024), lambda i: (i, 0))
    return pl.pallas_call(
        body, name="fox_gate", grid=(S // tm,),
        in_specs=[t, pl.BlockSpec((tm, 1024), lambda i: (i, 3))], out_specs=t,
        out_shape=jax.ShapeDtypeStruct((S, 1024), bf16),
        compiler_params=_params(("parallel",)),
    )(o, proj)


def fox_gate_bwd(dy, w_out, o, proj):
    S = o.shape[0]
    tm = 256

    def body(dy_ref, w_ref, o_ref, z_ref, do_ref, dz_ref, dl_ref):
        d_all = _nt(dy_ref[...], w_ref[...])
        lo_half = _iota((tm, 128), 1) < F_HD
        for p in range(F_H // 2):
            ps = slice(p * 128, (p + 1) * 128)
            d2, ov, z = d_all[:, ps], o_ref[:, ps], z_ref[:, ps]
            dov = d2 * _silu(z)
            do_ref[:, ps] = dov.astype(bf16)
            dz_ref[:, ps] = (d2 * ov * _dsilu(z)).astype(bf16)
            dl_ref[:, ps] = _half_mean(dov * ov, lo_half) * float(F_HD)

    t = pl.BlockSpec((tm, 1024), lambda i: (i, 0))
    sd = jax.ShapeDtypeStruct
    return pl.pallas_call(
        body, name="fox_gate_bwd", grid=(S // tm,),
        in_specs=[t, pl.BlockSpec(w_out.shape, lambda i: (0, 0)), t, pl.BlockSpec((tm, 1024), lambda i: (i, 3))],
        out_specs=[t, t, t],
        out_shape=[sd((S, 1024), bf16), sd((S, 1024), bf16), sd((S, 1024), f32)],
        compiler_params=_params(("parallel",)),
    )(dy, w_out, o, proj)


def fox_pre_bwd(proj, fbias, qw2, kw2, dqa, dka, dv, dz):
    S = proj.shape[0]
    tm = 256
    nb = S // tm

    def body(q_ref, k_ref, f_ref, fb_ref, qw_ref, kw_ref, dqa_ref, dka_ref, dv_ref, dz_ref, dp_ref, st_ref, carry):
        i = pl.program_id(0)

        @pl.when(i == 0)
        def _():
            carry[...] = jnp.zeros_like(carry)

        lane = _iota((tm, 128), 1)
        lo_half = lane < F_HD
        lane16 = _iota((tm, 16), 1)
        dcum = jnp.zeros((tm, 16), f32)
        dws = []
        for src, w_ref, dsrc, is_q, col0 in ((q_ref, qw_ref, dqa_ref, True, 0), (k_ref, kw_ref, dka_ref, False, 1024)):
            dw = jnp.zeros((1, 128), f32)
            for p in range(F_H // 2):
                ps = slice(p * 128, (p + 1) * 128)
                x = src[:, ps]
                r = lax.rsqrt(_half_mean(x * x, lo_half) + EPS)
                xh = x * r
                d0 = dsrc[:, (2 * p) * 128:(2 * p + 1) * 128]
                d1 = dsrc[:, (2 * p + 1) * 128:(2 * p + 2) * 128]
                dy = jnp.where(lo_half, d0, pltpu.roll(d1, F_HD, 1))
                if is_q:
                    dy = dy * (F_HD ** -0.5)
                dxh = dy * w_ref[...]
                dw = dw + jnp.sum(dy * xh, axis=0, keepdims=True)
                dp_ref[:, col0 + p * 128:col0 + (p + 1) * 128] = (r * (dxh - xh * _half_mean(dxh * xh, lo_half))).astype(bf16)
                for e, de in ((0, d0), (1, d1)):
                    col = de[:, 64:65] if is_q else -de[:, 67:68]
                    dcum = dcum + jnp.where(lane16 == 2 * p + e, col, 0.0)
            dws.append(dw)
        dp_ref[:, 2048:3072] = dv_ref[...].astype(bf16)
        dp_ref[:, 3072:4096] = dz_ref[...]
        utri = jnp.where(_iota((tm, tm), 1) >= _iota((tm, tm), 0), 1.0, 0.0).astype(f32)
        dlogf = _nn(utri, dcum, HI) + carry[0:1, :]
        carry[0:1, :] = dlogf[0:1, :]
        fl = f_ref[:, 0:16] + fb_ref[...]
        df = dlogf * _sigmoid(-fl)
        place = jnp.where(_iota((16, 128), 1) == _iota((16, 128), 0), 1.0, 0.0).astype(f32)
        dfw = _nn(df, place, HI)
        dp_ref[:, F_F0:F_INP] = dfw.astype(bf16)
        upd = jnp.concatenate(dws + [jnp.sum(dfw, axis=0, keepdims=True), jnp.zeros((5, 128), f32)], axis=0)

        @pl.when(i == 0)
        def _():
            st_ref[...] = upd

        @pl.when(i > 0)
        def _():
            st_ref[...] += upd

    rev = lambda w, c: pl.BlockSpec((tm, w), lambda i: (nb - 1 - i, c))
    vec = lambda n: pl.BlockSpec((1, n), lambda i: (0, 0))
    sd = jax.ShapeDtypeStruct
    return pl.pallas_call(
        body, name="fox_pre_bwd", grid=(nb,),
        in_specs=[rev(1024, 0), rev(1024, 1), rev(128, F_F0 // 128), vec(16), vec(128), vec(128),
                  rev(2048, 0), rev(2048, 0), rev(1024, 0), rev(1024, 0)],
        out_specs=[rev(F_INP, 0), pl.BlockSpec((8, 128), lambda i: (0, 0))],
        out_shape=[sd((S, F_INP), bf16), sd((8, 128), f32)],
        scratch_shapes=[pltpu.VMEM((8, 16), f32)],
        compiler_params=_params(("arbitrary",)),
    )(proj, proj, proj, fbias, qw2, kw2, dqa, dka, dv, dz)


def _me():
    return lax.axis_index("x"), lax.axis_index("y"), lax.axis_index("c")


def _other_chips(x, y):
    return [(1 - x, y), (x, 1 - y), (1 - x, 1 - y)]


def ag_small(xs):
    m_per, n = xs.shape

    def body(x_ref, out_ref, send_sems, recv_sems, local_sem):
        x, y, c = _me()
        me, sibling = (x, y, c), (x, y, 1 - c)
        chips = _other_chips(x, y)

        def rows(px, py, pc):
            return out_ref.at[pl.ds((4 * px + 2 * py + pc) * m_per, m_per), :]

        def copy(k, block, to, src=None):
            return pltpu.make_async_remote_copy(
                src_ref=rows(*block) if src is None else src, dst_ref=rows(*block),
                send_sem=send_sems.at[k], recv_sem=recv_sems.at[k], device_id=to, device_id_type=MESH)

        mine = pltpu.make_async_copy(x_ref, rows(*me), local_sem)
        mine.start()
        first = [copy(0, me, sibling, src=x_ref)]
        first += [copy(1 + j, me, (*chip, c), src=x_ref) for j, chip in enumerate(chips)]
        for cp in first:
            cp.start()
        passed = [copy(4 + j, (*chip, c), sibling) for j, chip in enumerate(chips)]
        for j, chip in enumerate(chips):
            copy(1 + j, (*chip, c), me).wait_recv()
            passed[j].start()
        copy(0, sibling, me).wait_recv()
        for j, chip in enumerate(chips):
            copy(4 + j, (*chip, 1 - c), me).wait_recv()
        for cp in first + passed:
            cp.wait_send()
        mine.wait()

    return pl.pallas_call(
        body, name="ag_small",
        out_shape=jax.ShapeDtypeStruct((8 * m_per, n), xs.dtype),
        in_specs=[pl.BlockSpec(memory_space=pltpu.VMEM)], out_specs=pl.BlockSpec(memory_space=pltpu.VMEM),
        scratch_shapes=[pltpu.SemaphoreType.DMA((7,)), pltpu.SemaphoreType.DMA((7,)), pltpu.SemaphoreType.DMA],
        compiler_params=pltpu.CompilerParams(vmem_limit_bytes=VMEM_LIMIT),
    )(xs)


_ANY = pl.BlockSpec(memory_space=pl.ANY)


def ag_chips(arrs):
    n = len(arrs)
    assert all(a.shape[0] == 2 for a in arrs)

    def body(*refs):
        ins, outs = refs[:n], refs[n:2 * n]
        send_sems, recv_sems, fwd_send, fwd_recv, local_sems = refs[2 * n:]
        x, y, c = _me()
        me = 2 * x + y
        chips = _other_chips(x, y)
        started = []
        for a in range(n):
            cp = pltpu.make_async_copy(ins[a], outs[a].at[me], local_sems.at[a])
            cp.start()
            started.append(cp)
        sends = []
        for a in range(n):
            for j, (px, py) in enumerate(chips):
                r = pltpu.make_async_remote_copy(
                    src_ref=ins[a].at[c], dst_ref=outs[a].at[me, c], send_sem=send_sems.at[3 * a + j],
                    recv_sem=recv_sems.at[3 * a + j], device_id=(px, py, c), device_id_type=MESH)
                r.start()
                sends.append(r)
        for a in range(n):
            for j, (px, py) in enumerate(chips):
                got = outs[a].at[2 * px + py, c]
                pltpu.make_async_remote_copy(
                    src_ref=ins[a].at[c], dst_ref=got, send_sem=send_sems.at[3 * a + j],
                    recv_sem=recv_sems.at[3 * a + j], device_id=(px, py, c), device_id_type=MESH).wait_recv()
                f = pltpu.make_async_remote_copy(
                    src_ref=got, dst_ref=got, send_sem=fwd_send.at[3 * a + j], recv_sem=fwd_recv.at[3 * a + j],
                    device_id=(x, y, 1 - c), device_id_type=MESH)
                f.start()
                sends.append(f)
        for a in range(n):
            for j, (px, py) in enumerate(chips):
                theirs = outs[a].at[2 * px + py, 1 - c]
                pltpu.make_async_remote_copy(
                    src_ref=theirs, dst_ref=theirs, send_sem=fwd_send.at[3 * a + j], recv_sem=fwd_recv.at[3 * a + j],
                    device_id=(x, y, 1 - c), device_id_type=MESH).wait_recv()
        for r in sends:
            r.wait_send()
        for cp in started:
            cp.wait()

    sems = pltpu.SemaphoreType.DMA((3 * n,))
    return pl.pallas_call(
        body, name="ag_chips",
        out_shape=[jax.ShapeDtypeStruct((4,) + a.shape, a.dtype) for a in arrs],
        in_specs=[_ANY] * n, out_specs=[_ANY] * n,
        scratch_shapes=[sems, sems, sems, sems, pltpu.SemaphoreType.DMA((n,))],
    )(*arrs)


def _ag_comm(arrs):
    n = len(arrs)

    def copies(ins, outs, sems, inbound):
        send_sems, recv_sems, local_sems = sems
        x, y, c = _me()
        me = 2 * x + y
        local = [pltpu.make_async_copy(ins[a], outs[a].at[me], local_sems.at[a]) for a in range(n)]
        out_cp, in_cp = [], []
        for a in range(n):
            for j, (px, py) in enumerate(_other_chips(x, y)):
                mk = functools.partial(pltpu.make_async_remote_copy, src_ref=ins[a], send_sem=send_sems.at[3 * a + j],
                                       recv_sem=recv_sems.at[3 * a + j], device_id=(px, py, c), device_id_type=MESH)
                out_cp.append(mk(dst_ref=outs[a].at[me]))
                if inbound:
                    in_cp.append(mk(dst_ref=outs[a].at[2 * px + py]))
        return local, out_cp, in_cp

    def start(ins, outs, sems):
        local, out_cp, _ = copies(ins, outs, sems, False)
        for cp in local + out_cp:
            cp.start()

    def wait(ins, outs, sems):
        local, out_cp, in_cp = copies(ins, outs, sems, True)
        for cp in in_cp:
            cp.wait_recv()
        for cp in out_cp:
            cp.wait_send()
        for cp in local:
            cp.wait()

    sems = [pltpu.SemaphoreType.DMA((3 * n,)), pltpu.SemaphoreType.DMA((3 * n,)), pltpu.SemaphoreType.DMA((n,))]
    return _Comm(arrs, [jax.ShapeDtypeStruct((4,) + a.shape, a.dtype) for a in arrs], sems, start, wait)


def _rs_comm(gs):
    n = len(gs)
    flips = [(fx, fy, fc) for fx in (0, 1) for fy in (0, 1) for fc in (0, 1)][1:]

    def copies(ins, outs, sems, inbound):
        send_sems, recv_sems, local_sems = sems
        x, y, c = _me()
        me = 4 * x + 2 * y + c
        local, out_cp, in_cp = [], [], []
        for a in range(n):
            rh = ins[a].shape[1] // 2
            mine = ins[a].at[2 * x + y, pl.ds(c * rh, rh), :]
            local.append(pltpu.make_async_copy(mine, outs[a].at[me], local_sems.at[a]))
            for j, (fx, fy, fc) in enumerate(flips):
                px, py, pc = (1 - x if fx else x), (1 - y if fy else y), (1 - c if fc else c)
                mk = functools.partial(pltpu.make_async_remote_copy, send_sem=send_sems.at[7 * a + j],
                                       recv_sem=recv_sems.at[7 * a + j], device_id=(px, py, pc), device_id_type=MESH)
                out_cp.append(mk(src_ref=ins[a].at[2 * px + py, pl.ds(pc * rh, rh), :], dst_ref=outs[a].at[me]))
                if inbound:
                    in_cp.append(mk(src_ref=mine, dst_ref=outs[a].at[4 * px + 2 * py + pc]))
        return local, out_cp, in_cp

    def start(ins, outs, sems):
        local, out_cp, _ = copies(ins, outs, sems, False)
        for cp in local + out_cp:
            cp.start()

    def wait(ins, outs, sems):
        local, out_cp, in_cp = copies(ins, outs, sems, True)
        for cp in in_cp:
            cp.wait_recv()
        for cp in out_cp:
            cp.wait_send()
        for cp in local:
            cp.wait()

    sems = [pltpu.SemaphoreType.DMA((7 * n,)), pltpu.SemaphoreType.DMA((7 * n,)), pltpu.SemaphoreType.DMA((n,))]
    return _Comm(gs, [jax.ShapeDtypeStruct((8, g.shape[1] // 2, g.shape[2]), g.dtype) for g in gs], sems, start, wait)


def sum_leading(q, name):
    K, R, C = q.shape
    tr = _pick(R, (256, 128, 64, 32, 16, 8))

    def body(q_ref, o_ref):
        acc = q_ref[0]
        for k in range(1, K):
            acc = acc + q_ref[k]
        o_ref[...] = acc

    return pl.pallas_call(
        body, name=name, grid=(R // tr,),
        in_specs=[pl.BlockSpec((K, tr, C), lambda i: (0, i, 0))], out_specs=pl.BlockSpec((tr, C), lambda i: (i, 0)),
        out_shape=jax.ShapeDtypeStruct((R, C), f32),
        compiler_params=_params(("parallel",)),
    )(q)


def rs_sum_devices(q, cidx, layer, n_layers, into=None):
    K, R, C = q.shape
    tr = _pick(R, (256, 128))

    def body(c_ref, q_ref, *rest):
        acc = q_ref[0].astype(f32)
        for k in range(1, K):
            acc = acc + q_ref[k].astype(f32)
        rest[-1][0, 0] = acc

    return pl.pallas_call(
        body, name="rs_sum_devices",
        grid_spec=pltpu.PrefetchScalarGridSpec(
            num_scalar_prefetch=1, grid=(R // tr,),
            in_specs=[pl.BlockSpec((K, tr, C), lambda i, c_ref: (0, i, 0))] + ([] if into is None else [_ANY]),
            out_specs=pl.BlockSpec((1, 1, tr, C), lambda i, c_ref: (layer, c_ref[0], i, 0))),
        out_shape=jax.ShapeDtypeStruct((n_layers, 2, R, C), f32),
        input_output_aliases={} if into is None else {2: 0},
        compiler_params=_params(("parallel",)),
    )(cidx, q, *([] if into is None else [into]))


def rs_share_halves(rs):
    n = len(rs)

    def body(*refs):
        bufs = refs[n:2 * n]
        send_sems, recv_sems = refs[2 * n:]
        x, y, c = _me()
        cps = []
        for a in range(n):
            mine = bufs[a].at[pl.ds(0, bufs[a].shape[0]), c]
            cp = pltpu.make_async_remote_copy(
                src_ref=mine, dst_ref=mine, send_sem=send_sems.at[a], recv_sem=recv_sems.at[a],
                device_id=(x, y, 1 - c), device_id_type=MESH)
            cp.start()
            cps.append(cp)
        for a, cp in enumerate(cps):
            theirs = bufs[a].at[pl.ds(0, bufs[a].shape[0]), 1 - c]
            pltpu.make_async_remote_copy(
                src_ref=theirs, dst_ref=theirs, send_sem=send_sems.at[a], recv_sem=recv_sems.at[a],
                device_id=(x, y, 1 - c), device_id_type=MESH).wait_recv()
            cp.wait_send()

    return pl.pallas_call(
        body, name="rs_share_halves",
        out_shape=[jax.ShapeDtypeStruct(r.shape, r.dtype) for r in rs],
        in_specs=[_ANY] * n, out_specs=[_ANY] * n, input_output_aliases={a: a for a in range(n)},
        scratch_shapes=[pltpu.SemaphoreType.DMA((n,)), pltpu.SemaphoreType.DMA((n,))],
    )(*rs)


def ada_mod(c_all, ada_w):
    L, _, n = ada_w.shape

    def body(c_ref, w_ref, o_ref):
        o_ref[0] = _nn(_silu(c_ref[...]), w_ref[0], HI)

    return pl.pallas_call(
        body, name="ada_mod", grid=(L,),
        in_specs=[pl.BlockSpec((8, D), lambda l: (0, 0)), pl.BlockSpec((1, D, n), lambda l: (l, 0, 0))],
        out_specs=pl.BlockSpec((1, 8, n), lambda l: (l, 0, 0)),
        out_shape=jax.ShapeDtypeStruct((L, 8, n), f32),
        compiler_params=_params(("parallel",)),
    )(c_all, ada_w)


def ada_w_grad(c_all, dmod):
    L, _, n = dmod.shape

    def body(c_ref, d_ref, o_ref):
        o_ref[0] = _tn(_silu(c_ref[...]), d_ref[0], HI)

    return pl.pallas_call(
        body, name="ada_w_grad", grid=(L,),
        in_specs=[pl.BlockSpec((8, D), lambda l: (0, 0)), pl.BlockSpec((1, 8, n), lambda l: (l, 0, 0))],
        out_specs=pl.BlockSpec((1, D, n), lambda l: (l, 0, 0)),
        out_shape=jax.ShapeDtypeStruct((L, D, n), f32),
        compiler_params=_params(("parallel",)),
    )(c_all, dmod)


def adamw(w, g, m, v, name):
    shp = w.shape
    two = lambda a: a.reshape(-1, shp[-1])
    R, C = two(w).shape
    tr = _pick(R, (256, 128, 64, 32, 16, 8))
    bc1, bc2 = 1.0 - B1 ** STEP, 1.0 - B2 ** STEP

    def body(w_ref, g_ref, m_ref, v_ref, d_ref, mo_ref, vo_ref):
        gv = g_ref[...]
        mn = B1 * m_ref[...] + (1.0 - B1) * gv
        vn = B2 * v_ref[...] + (1.0 - B2) * (gv * gv)
        d_ref[...] = -LR * ((mn / bc1) / (jnp.sqrt(vn / bc2) + AEPS) + WD * w_ref[...])
        mo_ref[...] = mn
        vo_ref[...] = vn

    t = pl.BlockSpec((tr, C), lambda i: (i, 0))
    outs = pl.pallas_call(
        body, name=name, grid=(R // tr,),
        in_specs=[t] * 4, out_specs=[t] * 3, out_shape=[jax.ShapeDtypeStruct((R, C), f32)] * 3,
        compiler_params=_params(("parallel",)),
    )(two(w), two(g), two(m), two(v))
    return [o.reshape(shp) for o in outs]


def _pack(arrs):
    parts, offs, r0 = [], [], 0
    for a in arrs:
        n = a.size
        rows = -(-n // 1024) * 8
        parts.append(jnp.pad(a.reshape(-1), (0, rows * 128 - n)).reshape(rows, 128))
        offs.append((r0, rows))
        r0 += rows
    return jnp.concatenate(parts, axis=0), offs


def _unpack(buf, offs, shapes):
    out = []
    for (r0, rows), shp in zip(offs, shapes):
        n = 1
        for d in shp:
            n *= d
        out.append(buf[..., r0:r0 + rows, :].reshape(buf.shape[:-2] + (rows * 128,))[..., :n].reshape(buf.shape[:-2] + tuple(shp)))
    return out


def kernel(x, c, norm_w, ada_w, ada_b, a_w_in, a_conv_w, a_A_log, a_dt_bias, a_norm_w, a_w_out, b_w_in, b_f_bias, b_qn_w, b_kn_w, b_w_out, final_norm_w, loss_target, m_norm_w, m_ada_w, m_ada_b, m_a_w_in, m_a_conv_w, m_a_A_log, m_a_dt_bias, m_a_norm_w, m_a_w_out, m_b_w_in, m_b_f_bias, m_b_qn_w, m_b_kn_w, m_b_w_out, m_final_norm_w, v_norm_w, v_ada_w, v_ada_b, v_a_w_in, v_a_conv_w, v_a_A_log, v_a_dt_bias, v_a_norm_w, v_a_w_out, v_b_w_in, v_b_f_bias, v_b_qn_w, v_b_kn_w, v_b_w_out, v_final_norm_w):
    weights = dict(norm_w=norm_w, ada_w=ada_w, ada_b=ada_b, a_w_in=a_w_in, a_conv_w=a_conv_w, a_A_log=a_A_log,
                   a_dt_bias=a_dt_bias, a_norm_w=a_norm_w, a_w_out=a_w_out, b_w_in=b_w_in, b_f_bias=b_f_bias,
                   b_qn_w=b_qn_w, b_kn_w=b_kn_w, b_w_out=b_w_out, final_norm_w=final_norm_w)
    m_in = dict(norm_w=m_norm_w, ada_w=m_ada_w, ada_b=m_ada_b, a_w_in=m_a_w_in, a_conv_w=m_a_conv_w, a_A_log=m_a_A_log,
                a_dt_bias=m_a_dt_bias, a_norm_w=m_a_norm_w, a_w_out=m_a_w_out, b_w_in=m_b_w_in, b_f_bias=m_b_f_bias,
                b_qn_w=m_b_qn_w, b_kn_w=m_b_kn_w, b_w_out=m_b_w_out, final_norm_w=m_final_norm_w)
    v_in = dict(norm_w=v_norm_w, ada_w=v_ada_w, ada_b=v_ada_b, a_w_in=v_a_w_in, a_conv_w=v_a_conv_w, a_A_log=v_a_A_log,
                a_dt_bias=v_a_dt_bias, a_norm_w=v_a_norm_w, a_w_out=v_a_w_out, b_w_in=v_b_w_in, b_f_bias=v_b_f_bias,
                b_qn_w=v_b_qn_w, b_kn_w=v_b_kn_w, b_w_out=v_b_w_out, final_norm_w=v_final_norm_w)
    xi, yi, ci = _me()
    me_b, me_k = 4 * xi + 2 * yi + ci, 2 * xi + yi
    cidx = ci.astype(jnp.int32).reshape(1)
    S = x.shape[1]
    depth, n_a, n_b = norm_w.shape[0], a_w_in.shape[0], b_w_in.shape[0]
    x0, tgt = x.reshape(S, D), loss_target.reshape(S, D)

    c_all = ag_small(jnp.pad(c, ((0, 7), (0, 0)))).reshape(8, 8, D)[:, 0]
    nloc = ada_w.shape[2]
    parts = ag_small(ada_mod(c_all, ada_w).reshape(depth * 8, nloc)).reshape(4, 2, depth, 8, nloc)[:, 0]
    mine = lax.dynamic_index_in_dim(parts, me_b, axis=2, keepdims=False)
    mod = jnp.transpose(mine, (1, 0, 2)).reshape(depth, 4 * nloc) + ada_b
    shift, scale, gate = (mod[:, k * D:(k + 1) * D] for k in range(3))

    w_loc = [(a_w_in[i // 2] if i % 2 == 0 else b_w_in[i // 2]).astype(bf16) for i in range(depth)]
    wo_loc = [(a_w_out[i // 2] if i % 2 == 0 else b_w_out[i // 2]).astype(bf16) for i in range(depth)]
    pad_in = [(G_INP - G_IN) if i % 2 == 0 else (F_INP - F_IN) for i in range(depth)]
    halves = lambda w: w.reshape((2, w.shape[0] // 2) + w.shape[1:])

    def cols_in_place(g_in, pad):
        w = jnp.transpose(g_in, (1, 0, 2)).reshape(g_in.shape[1], -1)
        return jnp.pad(w, ((0, 0), (0, pad)))

    g_in0, g_conv = ag_chips([halves(w_loc[0]), a_conv_w])
    w_in_full = [cols_in_place(g_in0.reshape((4,) + w_loc[0].shape), pad_in[0])]
    w_out_full = []
    conv = [jnp.transpose(g_conv[:, l], (1, 0, 2)).reshape(CONV_K, -1) for l in range(n_a)]
    qw2 = [_row(jnp.tile(b_qn_w[l], 2)) for l in range(n_b)]
    kw2 = [_row(jnp.tile(b_kn_w[l], 2)) for l in range(n_b)]

    saved, xc = [], x0
    for i in range(depth):
        l = i // 2
        nxt = _ag_comm([w_loc[i + 1], wo_loc[i + 1]]) if i + 1 < depth else None
        h = ln_mod(xc, _row(norm_w[i]), _row(scale[i]), _row(shift[i]))
        name = "mm_a_in" if i % 2 == 0 else "mm_b_in"
        if i == 0:
            proj, got = matmul(h, w_in_full[0], "nn", name, comm=_ag_comm([wo_loc[0]]))
            w_out_full.append(got[0].reshape(-1, D))
        else:
            proj = matmul(h, w_in_full[i], "nn", name)
        if i % 2 == 0:
            pre = gdn_pre(proj, conv[l], _row(a_A_log[l]), _row(a_dt_bias[l]))
            res, got = gdn_fwd(*pre, comm=nxt)
            o2 = gdn_onorm(res[0], proj, _row(a_norm_w[l]))
            y, xn = out_proj(o2, w_out_full[i], xc, _row(gate[i]), "out_proj_a")
        else:
            pre = fox_pre(proj, _row(b_f_bias[l]), qw2[l], kw2[l])
            res, got = fox_attn(*pre, comm=nxt)
            o2 = fox_gate(res[0], proj)
            y, xn = out_proj(o2, w_out_full[i], xc, _row(gate[i]), "out_proj_b")
        saved.append((xc, h, proj, o2, y, pre, res))
        if nxt is not None:
            w_in_full.append(cols_in_place(got[0], pad_in[i + 1]))
            w_out_full.append(got[1].reshape(-1, D))
        xc = xn
    dx, st_f = final_loss(xc, _row(final_norm_w), tgt)

    d_norm, d_mod = [None] * depth, [None] * depth
    d_conv, d_alog, d_dtb, d_anw = [None] * n_a, [None] * n_a, [None] * n_a, [None] * n_a
    d_fb, d_qn, d_kn = [None] * n_b, [None] * n_b, [None] * n_b
    ex_in, ex_out, pend_in = [None] * depth, [None] * depth, None
    for i in reversed(range(depth)):
        l = i // 2
        xin, h, proj, o2, y, pre, res = saved[i]
        ab = "a" if i % 2 == 0 else "b"
        dy, st_g = gate_bwd(dx, y, _row(gate[i]))
        d_out = matmul(o2, dy, "tn", f"mm_{ab}_dwo", out_dtype=bf16)
        ride = _rs_comm(([] if pend_in is None else [pend_in]) + [d_out.reshape(4, d_out.shape[0] // 4, D)])
        if i % 2 == 0:
            o, wv, at, tinv, vn, st = res
            do, dz, st_o = gdn_onorm_bwd(dy, w_out_full[i], o, proj, _row(a_norm_w[l]))
            grads, got = gdn_bwd(do, *pre, wv, at, tinv, vn, st, comm=ride)
            dcv, dba, st_s = gdn_pre_bwd(proj, conv[l], _row(a_A_log[l]), _row(a_dt_bias[l]), *grads)
            dproj, dcw = gdn_conv_bwd(proj, conv[l], dcv, dz, dba)
            d_conv[l], d_alog[l], d_dtb[l], d_anw[l] = dcw[:CONV_K], st_s[0], st_s[1], st_o[0]
        else:
            o, lse = res
            do, dz, delta = fox_gate_bwd(dy, w_out_full[i], o, proj)
            (dqa, dka, dv), got = fox_attn_bwd(*pre, do, lse, delta, comm=ride)
            dproj, st_b = fox_pre_bwd(proj, _row(b_f_bias[l]), qw2[l], kw2[l], dqa, dka, dv, dz)
            d_fb[l], d_qn[l], d_kn[l] = st_b[2, :F_H], st_b[0, :F_HD] + st_b[0, F_HD:], st_b[1, :F_HD] + st_b[1, F_HD:]
        ex_out[i] = got[-1]
        if pend_in is not None:
            ex_in[i + 1] = got[0]
        d_in = matmul(h, dproj, "tn", f"mm_{ab}_dw", out_dtype=bf16)
        cl = w_loc[i].shape[1]
        pend_in = jnp.transpose(d_in[:, :4 * cl].reshape(d_in.shape[0], 4, cl), (1, 0, 2))
        if i == 0:
            dh, got = matmul(dproj, w_in_full[i], "nt", f"mm_{ab}_dh", comm=_rs_comm([pend_in]))
            ex_in[0] = got[0]
        else:
            dh = matmul(dproj, w_in_full[i], "nt", f"mm_{ab}_dh")
        dx, st_n = ln_mod_bwd(xin, _row(norm_w[i]), _row(scale[i]), dh, dx)
        d_norm[i] = st_n[0]
        d_mod[i] = jnp.concatenate([st_n[2], st_n[1], st_g[0]])

    small = [jnp.stack(d_norm), jnp.stack(d_mod), jnp.stack(d_conv), jnp.stack(d_alog), jnp.stack(d_dtb), jnp.stack(d_anw),
             jnp.stack(d_fb), jnp.stack(d_qn), jnp.stack(d_kn), st_f[0], jnp.sum(st_f[1]).reshape(1)]
    shapes = [a.shape for a in small]
    buf, offs = _pack(small)
    gathered = ag_small(buf).reshape(8, buf.shape[0], 128)
    tot = _unpack(sum_leading(gathered, "sum_devices"), offs, shapes)
    g_norm, g_adab, g_convf, g_alog, g_dtb, g_anw, g_fb, g_qn, g_kn, g_fin, loss = tot
    dmod_all = _unpack(gathered, offs[1:2], shapes[1:2])[0]
    dmod_loc = lax.dynamic_slice_in_dim(dmod_all, me_k * nloc, nloc, axis=2)
    g_adaw = ada_w_grad(c_all, jnp.transpose(dmod_loc, (1, 0, 2)))
    g_conv_loc = lax.dynamic_slice_in_dim(g_convf, me_k * a_conv_w.shape[2], a_conv_w.shape[2], axis=2)

    bufs = {}
    for i in range(depth):
        for which, q in (("in", ex_in[i]), ("out", ex_out[i])):
            key = ("a" if i % 2 == 0 else "b", which)
            bufs[key] = rs_sum_devices(q, cidx, i // 2, depth // 2, into=bufs.get(key))
    keys = list(bufs)
    done = dict(zip(keys, rs_share_halves([bufs[k] for k in keys])))
    grads = dict(norm_w=g_norm, ada_w=g_adaw, ada_b=g_adab, a_w_in=done["a", "in"].reshape(a_w_in.shape),
                 a_conv_w=g_conv_loc, a_A_log=g_alog, a_dt_bias=g_dtb, a_norm_w=g_anw,
                 a_w_out=done["a", "out"].reshape(a_w_out.shape), b_w_in=done["b", "in"].reshape(b_w_in.shape),
                 b_f_bias=g_fb, b_qn_w=g_qn, b_kn_w=g_kn, b_w_out=done["b", "out"].reshape(b_w_out.shape),
                 final_norm_w=g_fin)
    names = list(weights)
    upd = {n: adamw(weights[n], grads[n], m_in[n], v_in[n], "adamw_" + n) for n in names}
    return (loss.reshape(()), dx.reshape(x.shape), *[grads[n] for n in names], *[upd[n][0] for n in names],
            *[upd[n][1] for n in names], *[upd[n][2] for n in names])
```

```python
import functools

import jax
import jax.numpy as jnp
from jax import lax
from jax.experimental import pallas as pl
from jax.experimental.pallas import tpu as pltpu

f32, bf16 = jnp.float32, jnp.bfloat16
HI = lax.Precision.HIGHEST
MESH = pl.DeviceIdType.MESH

EPS = 1e-6
D = 1024
CHUNK = 64
GQK_H, GV_H, GHD = 8, 16, 128
G_CONV = 4096
G_Z0 = 4096
G_BA0 = 6144
G_IN, G_INP = 6176, 6272
CONV_K = 4
F_H, F_HD = 16, 64
F_F0 = 4096
F_IN, F_INP = 4112, 4224
LR, B1, B2, AEPS, WD, STEP = 0.001, 0.9, 0.999, 1e-08, 0.01, 10
NEG = -1e30
VMEM_LIMIT = 56 * 1024 * 1024


def _nn(a, b, prec=None):
    return lax.dot_general(a, b, (((1,), (0,)), ((), ())), preferred_element_type=f32, precision=prec)


def _nt(a, b, prec=None):
    return lax.dot_general(a, b, (((1,), (1,)), ((), ())), preferred_element_type=f32, precision=prec)


def _tn(a, b, prec=None):
    return lax.dot_general(a, b, (((0,), (0,)), ((), ())), preferred_element_type=f32, precision=prec)


def _iota(shape, axis):
    return lax.broadcasted_iota(jnp.int32, shape, axis)


def _sigmoid(x):
    return 0.5 * jnp.tanh(0.5 * x) + 0.5


def _softplus(x):
    return jnp.maximum(x, 0.0) + jnp.log(1.0 + jnp.exp(-jnp.abs(x)))


def _silu(x):
    return x * _sigmoid(x)


def _dsilu(x):
    s = _sigmoid(x)
    return s * (1.0 + x * (1.0 - s))


def _params(sem=None, vmem=VMEM_LIMIT):
    return pltpu.CompilerParams(dimension_semantics=sem, vmem_limit_bytes=vmem)


def _row(v):
    return v.reshape(1, -1)


class _Comm:
    def __init__(self, ins, out_shapes, sems, start, wait):
        self.ins, self.out_shapes, self.sems, self.start, self.wait = list(ins), list(out_shapes), list(sems), start, wait


def _call(body, *, name, grid, in_specs, out_specs, out_shape, scratch_shapes, sem, args, comm=None, prefetch=()):
    n_pf, n_in, n_out, n_s = len(prefetch), len(in_specs), len(out_specs), len(scratch_shapes)
    n_ci, n_co = (len(comm.ins), len(comm.out_shapes)) if comm is not None else (0, 0)

    def wrapped(*refs):
        pf, refs = refs[:n_pf], refs[n_pf:]
        core_in, c_in = refs[:n_in], refs[n_in:n_in + n_ci]
        o0 = n_in + n_ci
        core_out, c_out = refs[o0:o0 + n_out], refs[o0 + n_out:o0 + n_out + n_co]
        s0 = o0 + n_out + n_co
        core_s, c_sem = refs[s0:s0 + n_s], refs[s0 + n_s:]
        if comm is not None:
            first = functools.reduce(jnp.logical_and, [pl.program_id(d) == 0 for d in range(len(grid))])
            pl.when(first)(functools.partial(comm.start, c_in, c_out, c_sem))
        body(*pf, *core_in, *core_out, *core_s)
        if comm is not None:
            last = functools.reduce(jnp.logical_and, [pl.program_id(d) == grid[d] - 1 for d in range(len(grid))])
            pl.when(last)(functools.partial(comm.wait, c_in, c_out, c_sem))

    extra = ([], [], [], []) if comm is None else ([_ANY] * n_ci, [_ANY] * n_co, comm.out_shapes, comm.sems)
    spec = pltpu.PrefetchScalarGridSpec(
        num_scalar_prefetch=n_pf, grid=grid, in_specs=list(in_specs) + extra[0], out_specs=list(out_specs) + extra[1],
        scratch_shapes=list(scratch_shapes) + extra[3])
    outs = pl.pallas_call(
        wrapped, name=name if comm is None else name + "_x", grid_spec=spec, out_shape=list(out_shape) + extra[2],
        compiler_params=_params(sem if comm is None else ("arbitrary",) * len(grid)),
    )(*prefetch, *args, *(comm.ins if comm is not None else []))
    return outs[:n_out], outs[n_out:]


def _pick(n, pref):
    for t in pref:
        if n % t == 0:
            return t
    return n


MM_VMEM_BUDGET = 44 * 1024 * 1024


def _mm_tiles(M, N, K):
    best = None
    for tk in [K] + [t for t in (2048, 1408, 1024, 896, 512, 384, 256, 128) if K % t == 0 and t < K]:
        for tm in (2048, 1024, 512, 256, 128):
            for tn in (1408, 1024, 896, 512, 384, 256, 128):
                if M % tm or N % tn:
                    continue
                nk = K // tk
                need = 2 * 2 * (tm * tk + tk * tn) + 2 * 4 * tm * tn + (4 * tm * tn if nk > 1 else 0)
                if need <= MM_VMEM_BUDGET:
                    cand = ((nk, -tm * tn), (tm, tn, tk))
                    best = cand if best is None or cand[0] < best[0] else best
    return best[1]


def matmul(a, b, mode, name, out_dtype=f32, comm=None):
    if mode == "nn":
        (M, K), (_, N) = a.shape, b.shape
    elif mode == "nt":
        (M, K), (N, _) = a.shape, b.shape
    else:
        (K, M), (_, N) = a.shape, b.shape
    tm, tn, tk = _mm_tiles(M, N, K)
    nk = K // tk
    dot = {"nn": _nn, "nt": _nt, "tn": _tn}[mode]

    def body(a_ref, b_ref, o_ref, *acc):
        k = pl.program_id(2)
        part = dot(a_ref[...], b_ref[...])
        if nk == 1:
            o_ref[...] = part.astype(out_dtype)
        else:
            acc_ref = acc[0]

            @pl.when(k == 0)
            def _():
                acc_ref[...] = part

            @pl.when(k > 0)
            def _():
                acc_ref[...] += part

            @pl.when(k == nk - 1)
            def _():
                o_ref[...] = acc_ref[...].astype(out_dtype)

    a_spec = pl.BlockSpec((tk, tm), lambda i, j, k: (k, i)) if mode == "tn" else pl.BlockSpec((tm, tk), lambda i, j, k: (i, k))
    b_spec = pl.BlockSpec((tn, tk), lambda i, j, k: (j, k)) if mode == "nt" else pl.BlockSpec((tk, tn), lambda i, j, k: (k, j))
    outs, got = _call(
        body, name=name, grid=(M // tm, N // tn, nk),
        in_specs=[a_spec, b_spec], out_specs=[pl.BlockSpec((tm, tn), lambda i, j, k: (i, j))],
        out_shape=[jax.ShapeDtypeStruct((M, N), out_dtype)],
        scratch_shapes=[] if nk == 1 else [pltpu.VMEM((tm, tn), f32)],
        sem=("parallel", "parallel", "arbitrary"), args=(a, b), comm=comm)
    return outs[0] if comm is None else (outs[0], got)


def out_proj(o2, w, x, gate, name):
    S, K = o2.shape
    N = w.shape[1]
    tm, tn = 1024, 1024

    def body(a_ref, b_ref, x_ref, g_ref, y_ref, xn_ref):
        y = _nn(a_ref[...], b_ref[...])
        y_ref[...] = y
        xn_ref[...] = x_ref[...] + g_ref[...] * y

    return pl.pallas_call(
        body, name=name, grid=(S // tm, N // tn),
        in_specs=[pl.BlockSpec((tm, K), lambda i, j: (i, 0)), pl.BlockSpec((K, tn), lambda i, j: (0, j)),
                  pl.BlockSpec((tm, tn), lambda i, j: (i, j)), pl.BlockSpec((1, tn), lambda i, j: (0, j))],
        out_specs=[pl.BlockSpec((tm, tn), lambda i, j: (i, j))] * 2,
        out_shape=[jax.ShapeDtypeStruct((S, N), f32)] * 2,
        compiler_params=_params(("parallel", "parallel")),
    )(o2, w, x, gate)


def ln_mod(x, nw, scale, shift):
    S = x.shape[0]
    tm = 512

    def body(x_ref, nw_ref, sc_ref, sh_ref, h_ref):
        xv = x_ref[...]
        r = lax.rsqrt(jnp.mean(xv * xv, axis=-1, keepdims=True) + EPS)
        h_ref[...] = ((xv * r) * nw_ref[...] * (1.0 + sc_ref[...]) + sh_ref[...]).astype(bf16)

    vec = pl.BlockSpec((1, D), lambda i: (0, 0))
    return pl.pallas_call(
        body, name="ln_mod", grid=(S // tm,),
        in_specs=[pl.BlockSpec((tm, D), lambda i: (i, 0)), vec, vec, vec],
        out_specs=pl.BlockSpec((tm, D), lambda i: (i, 0)),
        out_shape=jax.ShapeDtypeStruct((S, D), bf16),
        compiler_params=_params(("parallel",)),
    )(x, nw, scale, shift)


def ln_mod_bwd(x, nw, scale, dh, dxres):
    S = x.shape[0]
    tm = 512
    nb = S // tm

    def body(x_ref, nw_ref, sc_ref, dh_ref, dr_ref, dx_ref, st_ref):
        i = pl.program_id(0)
        xv = x_ref[...]
        r = lax.rsqrt(jnp.mean(xv * xv, axis=-1, keepdims=True) + EPS)
        xn = xv * r
        dh = dh_ref[...]
        dxn = dh * (nw_ref[...] * (1.0 + sc_ref[...]))
        dx_ref[...] = dr_ref[...] + r * (dxn - xn * jnp.mean(dxn * xn, axis=-1, keepdims=True))
        p1 = jnp.sum(dh * xn, axis=0, keepdims=True)
        p2 = jnp.sum(dh, axis=0, keepdims=True)
        upd = jnp.concatenate([p1, p1, p2, jnp.zeros((5, D), f32)], axis=0)

        @pl.when(i == 0)
        def _():
            st_ref[...] = upd

        @pl.when(i > 0)
        def _():
            st_ref[...] += upd

        @pl.when(i == nb - 1)
        def _():
            st_ref[0:1, :] = st_ref[0:1, :] * (1.0 + sc_ref[...])
            st_ref[1:2, :] = st_ref[1:2, :] * nw_ref[...]

    vec = pl.BlockSpec((1, D), lambda i: (0, 0))
    tile = pl.BlockSpec((tm, D), lambda i: (i, 0))
    return pl.pallas_call(
        body, name="ln_mod_bwd", grid=(S // tm,),
        in_specs=[tile, vec, vec, tile, tile],
        out_specs=[tile, pl.BlockSpec((8, D), lambda i: (0, 0))],
        out_shape=[jax.ShapeDtypeStruct((S, D), f32), jax.ShapeDtypeStruct((8, D), f32)],
        compiler_params=_params(("arbitrary",)),
    )(x, nw, scale, dh, dxres)


def final_loss(x, fw, tgt):
    S = x.shape[0]
    tm = 512

    def body(x_ref, w_ref, t_ref, dx_ref, st_ref):
        i = pl.program_id(0)
        xv = x_ref[...]
        r = lax.rsqrt(jnp.mean(xv * xv, axis=-1, keepdims=True) + EPS)
        xn = xv * r
        err = xn * w_ref[...] - t_ref[...]
        dy = err * (1.0 / D)
        dxn = dy * w_ref[...]
        dx_ref[...] = r * (dxn - xn * jnp.mean(dxn * xn, axis=-1, keepdims=True))
        p1 = jnp.sum(dy * xn, axis=0, keepdims=True)
        p2 = jnp.sum(err * err, axis=0, keepdims=True) * (0.5 / D)
        upd = jnp.concatenate([p1, p2, jnp.zeros((6, D), f32)], axis=0)

        @pl.when(i == 0)
        def _():
            st_ref[...] = upd

        @pl.when(i > 0)
        def _():
            st_ref[...] += upd

    tile = pl.BlockSpec((tm, D), lambda i: (i, 0))
    return pl.pallas_call(
        body, name="final_loss", grid=(S // tm,),
        in_specs=[tile, pl.BlockSpec((1, D), lambda i: (0, 0)), tile],
        out_specs=[tile, pl.BlockSpec((8, D), lambda i: (0, 0))],
        out_shape=[jax.ShapeDtypeStruct((S, D), f32), jax.ShapeDtypeStruct((8, D), f32)],
        compiler_params=_params(("arbitrary",)),
    )(x, fw, tgt)


def gate_bwd(dx, y, gate):
    S = dx.shape[0]
    tm = 512

    def body(dx_ref, y_ref, g_ref, dy_ref, st_ref):
        i = pl.program_id(0)
        dxv = dx_ref[...]
        dy_ref[...] = (g_ref[...] * dxv).astype(bf16)
        upd = jnp.concatenate([jnp.sum(dxv * y_ref[...], axis=0, keepdims=True), jnp.zeros((7, D), f32)], axis=0)

        @pl.when(i == 0)
        def _():
            st_ref[...] = upd

        @pl.when(i > 0)
        def _():
            st_ref[...] += upd

    tile = pl.BlockSpec((tm, D), lambda i: (i, 0))
    return pl.pallas_call(
        body, name="gate_bwd", grid=(S // tm,),
        in_specs=[tile, tile, pl.BlockSpec((1, D), lambda i: (0, 0))],
        out_specs=[tile, pl.BlockSpec((8, D), lambda i: (0, 0))],
        out_shape=[jax.ShapeDtypeStruct((S, D), bf16), jax.ShapeDtypeStruct((8, D), f32)],
        compiler_params=_params(("arbitrary",)),
    )(dx, y, gate)


def _chunk_mats(tm):
    r, c = _iota((tm, tm), 0), _iota((tm, tm), 1)
    same = jnp.right_shift(r, 6) == jnp.right_shift(c, 6)
    ltri = jnp.where(same & (c <= r), 1.0, 0.0).astype(f32)
    utri = jnp.where(same & (c >= r), 1.0, 0.0).astype(f32)
    bsame = jnp.where(same, 1.0, 0.0).astype(f32)
    return ltri, utri, bsame


def _gdn_scalars(ba, alog, dtb, ltri, bsame):
    beta = _sigmoid(ba[:, 0:16])
    u = ba[:, 16:32] + dtb
    neg_a = -jnp.exp(alog)
    g = neg_a * _softplus(u)
    gc = _nn(ltri, g, HI)
    glast = _nn(bsame, g, HI)
    return beta, u, neg_a, g, gc, glast


def _conv_taps(p_ref, halo_ref, first, gi, ext_scr):
    cs = slice(gi * 128, (gi + 1) * 128)
    tm = p_ref.shape[0]
    cur = p_ref[:, cs]
    ext_scr[gi, 0:8, :] = jnp.where(first, 0.0, halo_ref[:, cs])
    ext_scr[gi, 8:, :] = cur
    return [cur] + [ext_scr[gi, 8 - s:8 - s + tm, :] for s in range(1, CONV_K)]


def _conv_scratch(tm):
    return pltpu.VMEM((G_CONV // 128, tm + 8, 128), f32)


def _conv(taps, w):
    cv = taps[0] * w[3:4]
    for s in range(1, CONV_K):
        cv = cv + taps[s] * w[3 - s:4 - s]
    return cv


def _l2n(x):
    return x * lax.rsqrt(jnp.sum(x * x, axis=-1, keepdims=True) + EPS)


def _gdn_in_specs(tm, S):
    nb8 = tm // 8
    return [pl.BlockSpec((tm, G_CONV), lambda i: (i, 0)),
            pl.BlockSpec((8, G_CONV), lambda i: (jnp.maximum(i * nb8 - 1, 0), 0)),
            pl.BlockSpec((tm, 128), lambda i: (i, G_BA0 // 128))]


def gdn_pre(proj, conv_w, alog, dtb):
    S = proj.shape[0]
    tm = 256
    nch = tm // CHUNK

    def body(p_ref, halo_ref, ba_ref, w_ref, al_ref, dt_ref,
             q_ref, k_ref, kb_ref, kbg_ref, vb_ref, qd_ref, kd_ref, d_ref, gl_ref, ext_scr):
        first = pl.program_id(0) == 0
        ltri, _, bsame = _chunk_mats(tm)
        beta, _, _, _, gc, glast = _gdn_scalars(ba_ref[...], al_ref[...], dt_ref[...], ltri, bsame)
        eg, ek, egl = jnp.exp(gc), jnp.exp(glast - gc), jnp.exp(glast)
        eye = jnp.where(_iota((16, 16), 0) == _iota((16, 16), 1), 1.0, 0.0).astype(f32)
        gct = _nt(eye, gc, HI)
        low = _iota((CHUNK, CHUNK), 0) >= _iota((CHUNK, CHUNK), 1)

        def act(gi):
            return _silu(_conv(_conv_taps(p_ref, halo_ref, first, gi, ext_scr), w_ref[:, gi * 128:(gi + 1) * 128]))

        for j in range(GQK_H):
            js = slice(j * 128, (j + 1) * 128)
            qn = _l2n(act(j)) * (GHD ** -0.5)
            kn = _l2n(act(GQK_H + j))
            q_ref[:, js] = qn.astype(bf16)
            k_ref[:, js] = kn.astype(bf16)
            for e in range(2):
                h = 2 * j + e
                hs = slice(h * 128, (h + 1) * 128)
                v = act(2 * GQK_H + h)
                bh, egh, ekh = beta[:, h:h + 1], eg[:, h:h + 1], ek[:, h:h + 1]
                kbv = kn * bh
                kb_ref[:, hs] = kbv.astype(bf16)
                kbg_ref[:, hs] = (kbv * egh).astype(bf16)
                vb_ref[:, hs] = (v * bh).astype(bf16)
                qd_ref[:, hs] = (qn * egh).astype(bf16)
                kd_ref[:, hs] = (kn * ekh).astype(bf16)
                for c in range(nch):
                    rs = slice(c * CHUNK, (c + 1) * CHUNK)
                    diff = gc[rs, h:h + 1] - gct[h:h + 1, rs]
                    d_ref[rs, h * CHUNK:(h + 1) * CHUNK] = jnp.where(low, jnp.exp(jnp.where(low, diff, 0.0)), 0.0)
                    gl_ref[c * 8:(c + 1) * 8, hs] = jnp.broadcast_to(egl[c * CHUNK:c * CHUNK + 8, h:h + 1], (8, 128))

    full = lambda shape: pl.BlockSpec(shape, lambda i: (0, 0))
    t1 = pl.BlockSpec((tm, 1024), lambda i: (i, 0))
    t2 = pl.BlockSpec((tm, 2048), lambda i: (i, 0))
    sd = jax.ShapeDtypeStruct
    return pl.pallas_call(
        body, name="gdn_pre", grid=(S // tm,),
        in_specs=_gdn_in_specs(tm, S) + [full((CONV_K, G_CONV)), full((1, 16)), full((1, 16))],
        out_specs=[t1, t1, t2, t2, t2, t2, t2, t1, pl.BlockSpec((tm // 8, 2048), lambda i: (i, 0))],
        out_shape=[sd((S, 1024), bf16)] * 2 + [sd((S, 2048), bf16)] * 5 + [sd((S, 1024), f32), sd((S // 8, 2048), f32)],
        scratch_shapes=[_conv_scratch(tm)],
        compiler_params=_params(("parallel",)),
    )(proj, proj, proj, conv_w, alog, dtb)


def _bnn(a, b):
    return lax.dot_general(a, b, (((2,), (1,)), ((0,), (0,))), preferred_element_type=f32)


def _bnt(a, b):
    return lax.dot_general(a, b, (((2,), (2,)), ((0,), (0,))), preferred_element_type=f32)


def _btn(a, b):
    return lax.dot_general(a, b, (((1,), (1,)), ((0,), (0,))), preferred_element_type=f32)


def _split(a):
    hi = a.astype(bf16)
    return hi, (a - hi.astype(f32)).astype(bf16)


def _cat3(h, l, axis, lhs):
    return jnp.concatenate([h, h, l] if lhs else [h, l, h], axis=axis)


def _tri_inv_b(L):
    eye = jnp.where(_iota((1, CHUNK, CHUNK), 1) == _iota((1, CHUNK, CHUNK), 2), 1.0, 0.0).astype(f32)
    P = -L
    T = eye + P
    ph, pl_ = _split(P)
    for _ in range(5):
        P = _bnn(_cat3(ph, pl_, 2, True), _cat3(ph, pl_, 1, False))
        ph, pl_ = _split(P)
        th, tl = _split(T)
        T = T + _bnn(_cat3(th, tl, 2, True), _cat3(ph, pl_, 1, False))
    return T


GTB = 512
GQH_FWD, GQH_BWD = 1, 2


def _gdn_slices(ncb, gnv):
    pairs = [(c, e) for c in range(ncb) for e in range(gnv)]
    rs = lambda c: slice(c * CHUNK, (c + 1) * CHUNK)
    cs = lambda e: slice(e * 128, (e + 1) * 128)
    ds_ = lambda e: slice(e * CHUNK, (e + 1) * CHUNK)
    ks = lambda e: slice((e // 2) * 128, (e // 2 + 1) * 128)
    return pairs, rs, cs, ds_, ks


def gdn_fwd(q, k, kb, kbg, vb, qd, kd, dm, gl8, comm=None):
    S = q.shape[0]
    nb, ncb = S // GTB, GTB // CHUNK
    GQH, GNV = GQH_FWD, 2 * GQH_FWD
    pairs, rs, cs, ds_, ks = _gdn_slices(ncb, GNV)

    def body(q_ref, k_ref, kb_ref, kbg_ref, vb_ref, qd_ref, kd_ref, d_ref, gl_ref,
             o_ref, w_ref, at_ref, t_ref, vn_ref, st_ref, state, u_scr):
        @pl.when(pl.program_id(1) == 0)
        def _():
            state[...] = jnp.zeros_like(state)

        stk = lambda ref, lanes: jnp.stack([ref[rs(c), lanes(e)] for c, e in pairs])
        kq = stk(k_ref, ks)
        dmat = stk(d_ref, ds_)
        strict = _iota((1, CHUNK, CHUNK), 1) > _iota((1, CHUNK, CHUNK), 2)
        T = _tri_inv_b(jnp.where(strict, _bnt(stk(kb_ref, cs), kq) * dmat, 0.0))
        tb = T.astype(bf16)
        u_scr[...] = _bnn(tb, stk(vb_ref, cs))
        wb = _bnn(tb, stk(kbg_ref, cs)).astype(bf16)
        per_qk = lambda ref: jnp.stack([ref[rs(c), ks(e)] for c, e in pairs if e % 2 == 0])
        qk = _bnt(per_qk(q_ref), per_qk(k_ref))
        for b, (c, e) in enumerate(pairs):
            w_ref[rs(c), cs(e)] = wb[b]
            at_ref[rs(c), ds_(e)] = (qk[b // 2] * dmat[b]).astype(bf16)
            t_ref[rs(c), ds_(e)] = T[b]
        for b, (c, e) in enumerate(pairs):
            sb = state[e].astype(bf16)
            vnb = (u_scr[b] - _nn(w_ref[rs(c), cs(e)], sb)).astype(bf16)
            o_ref[rs(c), cs(e)] = _nn(qd_ref[rs(c), cs(e)], sb) + _nn(at_ref[rs(c), ds_(e)], vnb)
            st_ref[c * 128:(c + 1) * 128, cs(e)] = sb
            state[e] = state[e] * gl_ref[c * 8:c * 8 + 1, cs(e)] + _tn(kd_ref[rs(c), cs(e)], vnb)
            vn_ref[rs(c), cs(e)] = vnb

    b1 = pl.BlockSpec((GTB, 128 * GQH), lambda j, i: (i, j))
    b2 = pl.BlockSpec((GTB, 256 * GQH), lambda j, i: (i, j))
    sd = jax.ShapeDtypeStruct
    return _call(
        body, name="gdn_fwd", grid=(GQK_H // GQH, nb),
        in_specs=[b1, b1, b2, b2, b2, b2, b2, b1, pl.BlockSpec((GTB // 8, 256 * GQH), lambda j, i: (i, j))],
        out_specs=[b2, b2, b1, b1, b2, pl.BlockSpec((ncb * 128, 256 * GQH), lambda j, i: (i, j))],
        out_shape=[sd((S, 2048), f32), sd((S, 2048), bf16), sd((S, 1024), bf16), sd((S, 1024), f32),
                   sd((S, 2048), bf16), sd((S // CHUNK * 128, 2048), bf16)],
        scratch_shapes=[pltpu.VMEM((GNV, 128, 128), f32), pltpu.VMEM((GNV * ncb, CHUNK, 128), f32)],
        sem=("parallel", "arbitrary"), args=(q, k, kb, kbg, vb, qd, kd, dm, gl8), comm=comm)


def gdn_bwd(do, q, k, kb, kbg, vb, qd, kd, dm, gl8, w, at, T, vn, st, comm=None):
    S = q.shape[0]
    nb, ncb = S // GTB, GTB // CHUNK
    GQH, GNV = GQH_BWD, 2 * GQH_BWD
    pairs, rs, cs, ds_, ks = _gdn_slices(ncb, GNV)

    def body(do_ref, q_ref, k_ref, kb_ref, kbg_ref, vb_ref, qd_ref, kd_ref, d_ref, gl_ref, w_ref, at_ref, t_ref, vn_ref, st_ref,
             dq_ref, dk_ref, dkb_ref, dkbg_ref, dvb_ref, dqd_ref, dkd_ref, dgc_ref, dstate, dvn_scr, dw_scr, dat_scr, dgl_scr):
        @pl.when(pl.program_id(1) == 0)
        def _():
            dstate[...] = jnp.zeros_like(dstate)

        for b, (c, e) in reversed(list(enumerate(pairs))):
            dob = do_ref[rs(c), cs(e)].astype(bf16)
            sb = st_ref[c * 128:(c + 1) * 128, cs(e)]
            vnb = vn_ref[rs(c), cs(e)]
            gl = gl_ref[c * 8:c * 8 + 1, cs(e)]
            dS = dstate[e]
            dsb = dS.astype(bf16)
            dvnb = (_tn(at_ref[rs(c), ds_(e)], dob) + _nn(kd_ref[rs(c), cs(e)], dsb)).astype(bf16)
            dvn_scr[b] = dvnb
            dat_scr[b] = _nt(dob, vnb)
            dqd_ref[rs(c), cs(e)] = _nt(dob, sb)
            dkd_ref[rs(c), cs(e)] = _nt(vnb, dsb)
            dw_scr[b] = (-_nt(dvnb, sb)).astype(bf16)
            dgl = jnp.sum(jnp.sum(dS * sb.astype(f32), axis=1, keepdims=True), axis=0, keepdims=True)
            dgl_scr[b] = jnp.broadcast_to(dgl * gl, (8, 128))
            dstate[e] = gl * dS + _tn(qd_ref[rs(c), cs(e)], dob) - _tn(w_ref[rs(c), cs(e)], dvnb)

        stk = lambda ref, lanes: jnp.stack([ref[rs(c), lanes(e)] for c, e in pairs])
        kq, qq = stk(k_ref, ks), stk(q_ref, ks)
        kbb = stk(kb_ref, cs)
        Tm = stk(t_ref, ds_)
        tb = Tm.astype(bf16)
        dvn, dw = dvn_scr[...], dw_scr[...]
        dT = _bnt(dvn, stk(vb_ref, cs)) + _bnt(dw, stk(kbg_ref, cs))
        dvb, dkbg = _btn(tb, dvn), _btn(tb, dw)
        th, tl = _split(Tm)
        xh, xl = _split(_bnt(_cat3(*_split(dT), 2, True), _cat3(th, tl, 2, False)))
        dL = -_btn(_cat3(th, tl, 1, True), _cat3(xh, xl, 1, False))
        dmat = stk(d_ref, ds_)
        strict = _iota((1, CHUNK, CHUNK), 1) > _iota((1, CHUNK, CHUNK), 2)
        dA = jnp.where(strict, dL * dmat, 0.0)
        dB = dat_scr[...] * dmat
        dAb, dBb = dA.astype(bf16), dB.astype(bf16)
        dkb = _bnn(dAb, kq)
        dkc = _btn(dAb, kbb) + _btn(dBb, qq)
        dqc = _bnn(dBb, kq)
        M = dA * _bnt(kbb, kq) + dB * _bnt(qq, kq)
        mh, ml = _split(M)
        colsum = _btn(jnp.concatenate([mh, ml], axis=1), jnp.ones((GNV * ncb, 2 * CHUNK, 128), bf16))
        lastrow = _iota((1, CHUNK, 128), 1) == CHUNK - 1
        for b, (c, e) in enumerate(pairs):
            dvb_ref[rs(c), cs(e)] = dvb[b]
            dkbg_ref[rs(c), cs(e)] = dkbg[b]
            dkb_ref[rs(c), cs(e)] = dkb[b]
            dgc_ref[rs(c), cs(e)] = (jnp.sum(M[b], axis=1, keepdims=True) - colsum[b]
                                     + jnp.where(lastrow[0], dgl_scr[b][0:1, :], 0.0))
        for b, (c, e) in enumerate(pairs):
            if e % 2 == 0:
                dq_ref[rs(c), ks(e)] = dqc[b] + dqc[b + 1]
                dk_ref[rs(c), ks(e)] = dkc[b] + dkc[b + 1]

    b1 = pl.BlockSpec((GTB, 128 * GQH), lambda j, i: (nb - 1 - i, j))
    b2 = pl.BlockSpec((GTB, 256 * GQH), lambda j, i: (nb - 1 - i, j))
    sd = jax.ShapeDtypeStruct
    return _call(
        body, name="gdn_bwd", grid=(GQK_H // GQH, nb),
        in_specs=[b2, b1, b1, b2, b2, b2, b2, b2, b1, pl.BlockSpec((GTB // 8, 256 * GQH), lambda j, i: (nb - 1 - i, j)),
                  b2, b1, b1, b2, pl.BlockSpec((ncb * 128, 256 * GQH), lambda j, i: (nb - 1 - i, j))],
        out_specs=[b1, b1, b2, b2, b2, b2, b2, b2],
        out_shape=[sd((S, 1024), f32)] * 2 + [sd((S, 2048), f32)] * 6,
        scratch_shapes=[pltpu.VMEM((GNV, 128, 128), f32), pltpu.VMEM((GNV * ncb, CHUNK, 128), bf16),
                        pltpu.VMEM((GNV * ncb, CHUNK, 128), bf16), pltpu.VMEM((GNV * ncb, CHUNK, CHUNK), f32),
                        pltpu.VMEM((GNV * ncb, 8, 128), f32)],
        sem=("parallel", "arbitrary"), args=(do, q, k, kb, kbg, vb, qd, kd, dm, gl8, w, at, T, vn, st), comm=comm)


def gdn_onorm(o, proj, nw):
    S = o.shape[0]
    tm = 256

    def body(o_ref, z_ref, nw_ref, o2_ref):
        for h in range(GV_H):
            hs = slice(h * 128, (h + 1) * 128)
            oh = o_ref[:, hs]
            r = lax.rsqrt(jnp.mean(oh * oh, axis=-1, keepdims=True) + EPS)
            o2_ref[:, hs] = (((oh * r) * nw_ref[...]) * _silu(z_ref[:, hs])).astype(bf16)

    t2 = pl.BlockSpec((tm, 2048), lambda i: (i, 0))
    return pl.pallas_call(
        body, name="gdn_onorm", grid=(S // tm,),
        in_specs=[t2, pl.BlockSpec((tm, 2048), lambda i: (i, G_Z0 // 2048)), pl.BlockSpec((1, 128), lambda i: (0, 0))],
        out_specs=t2, out_shape=jax.ShapeDtypeStruct((S, 2048), bf16),
        compiler_params=_params(("parallel",)),
    )(o, proj, nw)


def gdn_onorm_bwd(dy, w_out, o, proj, nw):
    S = o.shape[0]
    tm = 512

    def body(dy_ref, w_ref, o_ref, z_ref, nw_ref, do_ref, dz_ref, st_ref):
        i = pl.program_id(0)
        d_all = _nt(dy_ref[...], w_ref[...])
        acc = jnp.zeros((1, 128), f32)
        for h in range(GV_H):
            hs = slice(h * 128, (h + 1) * 128)
            oh, z, d2 = o_ref[:, hs], z_ref[:, hs], d_all[:, hs]
            r = lax.rsqrt(jnp.mean(oh * oh, axis=-1, keepdims=True) + EPS)
            on = oh * r
            dt = d2 * _silu(z)
            dz_ref[:, hs] = (d2 * (on * nw_ref[...]) * _dsilu(z)).astype(bf16)
            don = dt * nw_ref[...]
            acc = acc + jnp.sum(dt * on, axis=0, keepdims=True)
            do_ref[:, hs] = r * (don - on * jnp.mean(don * on, axis=-1, keepdims=True))
        upd = jnp.concatenate([acc, jnp.zeros((7, 128), f32)], axis=0)

        @pl.when(i == 0)
        def _():
            st_ref[...] = upd

        @pl.when(i > 0)
        def _():
            st_ref[...] += upd

    t2 = pl.BlockSpec((tm, 2048), lambda i: (i, 0))
    sd = jax.ShapeDtypeStruct
    return pl.pallas_call(
        body, name="gdn_onorm_bwd", grid=(S // tm,),
        in_specs=[pl.BlockSpec((tm, D), lambda i: (i, 0)), pl.BlockSpec(w_out.shape, lambda i: (0, 0)), t2,
                  pl.BlockSpec((tm, 2048), lambda i: (i, G_Z0 // 2048)), pl.BlockSpec((1, 128), lambda i: (0, 0))],
        out_specs=[t2, t2, pl.BlockSpec((8, 128), lambda i: (0, 0))],
        out_shape=[sd((S, 2048), f32), sd((S, 2048), bf16), sd((8, 128), f32)],
        compiler_params=_params(("arbitrary",)),
    )(dy, w_out, o, proj, nw)


def gdn_pre_bwd(proj, conv_w, alog, dtb, dq, dk, dkb, dkbg, dvb, dqd, dkd, dgcd):
    S = proj.shape[0]
    tm = 128

    def body(p_ref, halo_ref, ba_ref, w_ref, al_ref, dt_ref, dq_ref, dk_ref, dkb_ref, dkbg_ref, dvb_ref, dqd_ref, dkd_ref, dgc_ref,
             dcv_ref, dba_ref, st_ref, ext_scr):
        i = pl.program_id(0)
        first = i == 0
        ltri, utri, bsame = _chunk_mats(tm)
        beta, u, neg_a, g, gc, glast = _gdn_scalars(ba_ref[...], al_ref[...], dt_ref[...], ltri, bsame)
        eg, ek = jnp.exp(gc), jnp.exp(glast - gc)
        lane16 = _iota((tm, 16), 1)
        dgc_all = jnp.zeros((tm, 16), f32)
        rkd_all = jnp.zeros((tm, 16), f32)
        dbeta_all = jnp.zeros((tm, 16), f32)

        def pre(gi):
            return _conv(_conv_taps(p_ref, halo_ref, first, gi, ext_scr), w_ref[:, gi * 128:(gi + 1) * 128])

        def l2n_bwd(xt, dy):
            r = lax.rsqrt(jnp.sum(xt * xt, axis=-1, keepdims=True) + EPS)
            y = xt * r
            return r * (dy - y * jnp.sum(dy * y, axis=-1, keepdims=True))

        for j in range(GQK_H):
            js = slice(j * 128, (j + 1) * 128)
            cvq, cvk = pre(j), pre(GQK_H + j)
            qt, kt = _silu(cvq), _silu(cvk)
            qn = _l2n(qt) * (GHD ** -0.5)
            kn = _l2n(kt)
            dq_tot, dk_tot = dq_ref[:, js], dk_ref[:, js]
            for e in range(2):
                h = 2 * j + e
                hs = slice(h * 128, (h + 1) * 128)
                gv = 2 * GQK_H + h
                cvv = pre(gv)
                v = _silu(cvv)
                bh, egh, ekh = beta[:, h:h + 1], eg[:, h:h + 1], ek[:, h:h + 1]
                dkbg, dkd, dqd, dvb = dkbg_ref[:, hs], dkd_ref[:, hs], dqd_ref[:, hs], dvb_ref[:, hs]
                dkb_t = dkb_ref[:, hs] + dkbg * egh
                dk_tot = dk_tot + dkb_t * bh + dkd * ekh
                dq_tot = dq_tot + dqd * egh
                dcv_ref[:, gv * 128:(gv + 1) * 128] = (dvb * bh) * _dsilu(cvv)
                dbeta = jnp.sum(dkb_t * kn, axis=-1, keepdims=True) + jnp.sum(dvb * v, axis=-1, keepdims=True)
                rkd = jnp.sum(dkd * (kn * ekh), axis=-1, keepdims=True)
                dgc = (dgc_ref[:, hs][:, 0:1] + jnp.sum(dkbg * (kn * bh * egh), axis=-1, keepdims=True)
                       + jnp.sum(dqd * (qn * egh), axis=-1, keepdims=True) - rkd)
                sel = lane16 == h
                dgc_all = dgc_all + jnp.where(sel, dgc, 0.0)
                rkd_all = rkd_all + jnp.where(sel, rkd, 0.0)
                dbeta_all = dbeta_all + jnp.where(sel, dbeta, 0.0)
            dcv_ref[:, js] = l2n_bwd(qt, dq_tot * (GHD ** -0.5)) * _dsilu(cvq)
            ks = slice((GQK_H + j) * 128, (GQK_H + j + 1) * 128)
            dcv_ref[:, ks] = l2n_bwd(kt, dk_tot) * _dsilu(cvk)

        islast = jnp.bitwise_and(_iota((tm, 16), 0), CHUNK - 1) == CHUNK - 1
        dgc_all = dgc_all + jnp.where(islast, _nn(bsame, rkd_all, HI), 0.0)
        dg = _nn(utri, dgc_all, HI)
        da = dg * neg_a * _sigmoid(u)
        db = dbeta_all * beta * (1.0 - beta)
        r16, c128 = _iota((16, 128), 0), _iota((16, 128), 1)
        pb = jnp.where(c128 == r16, 1.0, 0.0).astype(f32)
        pa = jnp.where(c128 == r16 + 16, 1.0, 0.0).astype(f32)
        dba_ref[...] = _nn(db, pb, HI) + _nn(da, pa, HI)
        upd = jnp.concatenate([jnp.sum(dg * g, axis=0, keepdims=True), jnp.sum(da, axis=0, keepdims=True),
                               jnp.zeros((6, 16), f32)], axis=0)

        @pl.when(i == 0)
        def _():
            st_ref[...] = upd

        @pl.when(i > 0)
        def _():
            st_ref[...] += upd

    full = lambda shape: pl.BlockSpec(shape, lambda i: (0, 0))
    t1 = pl.BlockSpec((tm, 1024), lambda i: (i, 0))
    t2 = pl.BlockSpec((tm, 2048), lambda i: (i, 0))
    sd = jax.ShapeDtypeStruct
    return pl.pallas_call(
        body, name="gdn_pre_bwd", grid=(S // tm,),
        in_specs=_gdn_in_specs(tm, S) + [full((CONV_K, G_CONV)), full((1, 16)), full((1, 16)), t1, t1] + [t2] * 6,
        out_specs=[pl.BlockSpec((tm, G_CONV), lambda i: (i, 0)), pl.BlockSpec((tm, 128), lambda i: (i, 0)), full((8, 16))],
        out_shape=[sd((S, G_CONV), f32), sd((S, 128), f32), sd((8, 16), f32)],
        scratch_shapes=[_conv_scratch(tm)],
        compiler_params=_params(("arbitrary",)),
    )(proj, proj, proj, conv_w, alog, dtb, dq, dk, dkb, dkbg, dvb, dqd, dkd, dgcd)


def gdn_conv_bwd(proj, conv_w, dcv, dz, dba):
    S = proj.shape[0]
    tm = 256
    nb, nb8 = S // tm, tm // 8

    def body(p_ref, halo_ref, w_ref, dcv_ref, nxt_ref, dz_ref, dba_ref, dp_ref, dw_ref, ext_scr, nxt_scr):
        i = pl.program_id(0)
        first, last = i == 0, i == nb - 1
        for gi in range(G_CONV // 128):
            cs = slice(gi * 128, (gi + 1) * 128)
            taps = _conv_taps(p_ref, halo_ref, first, gi, ext_scr)
            cur = dcv_ref[:, cs]
            nxt_scr[gi, 0:tm, :] = cur
            nxt_scr[gi, tm:, :] = jnp.where(last, 0.0, nxt_ref[:, cs])
            w = w_ref[:, cs]
            dp = cur * w[3:4]
            rows = [jnp.sum(cur * taps[3 - kk], axis=0, keepdims=True) for kk in range(CONV_K)]
            for s in range(1, CONV_K):
                dp = dp + nxt_scr[gi, s:s + tm, :] * w[3 - s:4 - s]
            dp_ref[:, cs] = dp.astype(bf16)
            upd = jnp.concatenate(rows + [jnp.zeros((4, 128), f32)], axis=0)

            @pl.when(first)
            def _():
                dw_ref[:, cs] = upd

            @pl.when(i > 0)
            def _():
                dw_ref[:, cs] += upd

        dp_ref[:, G_Z0:G_BA0] = dz_ref[...]
        dp_ref[:, G_BA0:G_INP] = dba_ref[...].astype(bf16)

    sd = jax.ShapeDtypeStruct
    return pl.pallas_call(
        body, name="gdn_conv_bwd", grid=(nb,),
        in_specs=[pl.BlockSpec((tm, G_CONV), lambda i: (i, 0)),
                  pl.BlockSpec((8, G_CONV), lambda i: (jnp.maximum(i * nb8 - 1, 0), 0)),
                  pl.BlockSpec((CONV_K, G_CONV), lambda i: (0, 0)),
                  pl.BlockSpec((tm, G_CONV), lambda i: (i, 0)),
                  pl.BlockSpec((8, G_CONV), lambda i: (jnp.minimum((i + 1) * nb8, S // 8 - 1), 0)),
                  pl.BlockSpec((tm, 2048), lambda i: (i, 0)), pl.BlockSpec((tm, 128), lambda i: (i, 0))],
        out_specs=[pl.BlockSpec((tm, G_INP), lambda i: (i, 0)), pl.BlockSpec((8, G_CONV), lambda i: (0, 0))],
        out_shape=[sd((S, G_INP), bf16), sd((8, G_CONV), f32)],
        scratch_shapes=[_conv_scratch(tm), _conv_scratch(tm)],
        compiler_params=_params(("arbitrary",)),
    )(proj, proj, conv_w, dcv, dcv, dz, dba)


def _half_mean(t, lo_half):
    m0 = jnp.sum(jnp.where(lo_half, t, 0.0), axis=-1, keepdims=True)
    m1 = jnp.sum(jnp.where(lo_half, 0.0, t), axis=-1, keepdims=True)
    return jnp.where(lo_half, m0, m1) * (1.0 / F_HD)


def _split3(c):
    hi = c.astype(bf16).astype(f32)
    mid = (c - hi).astype(bf16).astype(f32)
    lo = (c - hi - mid).astype(bf16).astype(f32)
    return hi, mid, lo


def fox_pre(proj, fbias, qw2, kw2):
    S = proj.shape[0]
    tm = 256

    def body(q_ref, k_ref, v_ref, f_ref, fb_ref, qw_ref, kw_ref, qa_ref, ka_ref, vb_ref, carry):
        @pl.when(pl.program_id(0) == 0)
        def _():
            carry[...] = jnp.zeros_like(carry)

        logf = -_softplus(-(f_ref[:, 0:16] + fb_ref[...]))
        ltri = jnp.where(_iota((tm, tm), 1) <= _iota((tm, tm), 0), 1.0, 0.0).astype(f32)
        cum = _nn(ltri, logf, HI) + carry[0:1, :]
        carry[0:1, :] = cum[tm - 1:tm, :]
        lane = _iota((tm, 128), 1)
        lo_half = lane < F_HD
        for p in range(F_H // 2):
            ps = slice(p * 128, (p + 1) * 128)
            for src, w_ref, dst, is_q in ((q_ref, qw_ref, qa_ref, True), (k_ref, kw_ref, ka_ref, False)):
                x = src[:, ps]
                xn = x * lax.rsqrt(_half_mean(x * x, lo_half) + EPS) * w_ref[...]
                if is_q:
                    xn = xn * (F_HD ** -0.5)
                for e in range(2):
                    h = 2 * p + e
                    base = xn if e == 0 else pltpu.roll(xn, F_HD, 1)
                    hi, mid, lo = _split3(cum[:, h:h + 1])
                    pieces = jnp.where(lane == 64, hi, 0.0) + jnp.where(lane == 65, mid, 0.0) + jnp.where(lane == 66, lo, 0.0)
                    if is_q:
                        ext = pieces + jnp.where((lane >= 67) & (lane <= 69), 1.0, 0.0)
                    else:
                        ext = jnp.where((lane >= 64) & (lane <= 66), 1.0, 0.0) - pltpu.roll(pieces, 3, 1)
                    dst[:, h * 128:(h + 1) * 128] = jnp.where(lo_half, base, ext).astype(bf16)
        one = jnp.where(lane == F_HD, 1.0, 0.0)
        for p in range(F_H // 2):
            vv = v_ref[:, p * 128:(p + 1) * 128]
            vb_ref[:, (2 * p) * 128:(2 * p + 1) * 128] = jnp.where(lo_half, vv, one).astype(bf16)
            vb_ref[:, (2 * p + 1) * 128:(2 * p + 2) * 128] = jnp.where(lo_half, pltpu.roll(vv, F_HD, 1), one).astype(bf16)

    t1 = lambda c: pl.BlockSpec((tm, 1024), lambda i: (i, c))
    vec = lambda n: pl.BlockSpec((1, n), lambda i: (0, 0))
    sd = jax.ShapeDtypeStruct
    return pl.pallas_call(
        body, name="fox_pre", grid=(S // tm,),
        in_specs=[t1(0), t1(1), t1(2), pl.BlockSpec((tm, 128), lambda i: (i, F_F0 // 128)), vec(16), vec(128), vec(128)],
        out_specs=[pl.BlockSpec((tm, 2048), lambda i: (i, 0))] * 3,
        out_shape=[sd((S, 2048), bf16)] * 3,
        scratch_shapes=[pltpu.VMEM((8, 16), f32)],
        compiler_params=_params(("arbitrary",)),
    )(proj, proj, proj, proj, fbias, qw2, kw2)


FTQ = 512
FHS_FWD, FHS_BWD = 16, 8


def fox_attn(qa, ka, v, comm=None):
    S = qa.shape[0]
    nq = S // FTQ
    FHS = FHS_FWD

    live = [(i, j) for i in range(nq) for j in range(i + 1)]
    qi_tab = jnp.asarray([i for i, _ in live], jnp.int32)
    kj_tab = jnp.asarray([j for _, j in live], jnp.int32)

    def body(qi_ref, kj_ref, q_ref, k_ref, v_ref, o_ref, lse_ref, m_scr, acc_scr):
        t = pl.program_id(1)
        i, j = qi_ref[t], kj_ref[t]

        @pl.when(j == 0)
        def _():
            m_scr[...] = jnp.full_like(m_scr, NEG)
            acc_scr[...] = jnp.zeros_like(acc_scr)

        def step(diagonal):
            for e in range(FHS):
                es = slice(e * 128, (e + 1) * 128)
                s = _nt(q_ref[:, es], k_ref[:, es])
                if diagonal:
                    s = jnp.where(_iota((FTQ, FTQ), 0) >= _iota((FTQ, FTQ), 1), s, NEG)
                m_old = m_scr[e]
                m_new = jnp.maximum(m_old, jnp.max(s, axis=-1, keepdims=True))
                p = jnp.exp(s - m_new[:, 0:1])
                acc_scr[e] = acc_scr[e] * jnp.exp(m_old - m_new) + _nn(p.astype(bf16), v_ref[:, es])
                m_scr[e] = m_new

        pl.when(j < i)(functools.partial(step, False))

        @pl.when(j == i)
        def _():
            step(True)
            for e in range(FHS):
                vs = slice(e * F_HD, (e + 1) * F_HD)
                acc = acc_scr[e]
                l = acc[:, F_HD:F_HD + 1]
                o_ref[:, vs] = acc[:, 0:F_HD] / l
                lse_ref[:, vs] = m_scr[e][:, 0:F_HD] + jnp.log(l)

    sd = jax.ShapeDtypeStruct
    qo = pl.BlockSpec((FTQ, F_HD * FHS), lambda p, t, qi, kj: (qi[t], p))
    kv = pl.BlockSpec((FTQ, 128 * FHS), lambda p, t, qi, kj: (kj[t], p))
    return _call(
        body, name="fox_attn", grid=(F_H // FHS, len(live)),
        in_specs=[pl.BlockSpec((FTQ, 128 * FHS), lambda p, t, qi, kj: (qi[t], p)), kv, kv],
        out_specs=[qo, qo],
        out_shape=[sd((S, 1024), f32), sd((S, 1024), f32)],
        scratch_shapes=[pltpu.VMEM((FHS, FTQ, 128), f32), pltpu.VMEM((FHS, FTQ, 128), f32)],
        sem=("parallel", "arbitrary"), args=(qa, ka, v), comm=comm, prefetch=(qi_tab, kj_tab))


def fox_attn_bwd(qa, ka, v, do, lse, delta, comm=None):
    S = qa.shape[0]
    nq = S // FTQ
    FHS = FHS_BWD

    live = [(j, i) for j in range(nq) for i in range(j, nq)]
    kj_tab = jnp.asarray([j for j, _ in live], jnp.int32)
    qi_tab = jnp.asarray([i for _, i in live], jnp.int32)

    def body(kj_ref, qi_ref, q_ref, k_ref, v_ref, do_ref, lse_ref, dl_ref, dq_ref, dk_ref, dv_ref, dk_scr, dv_scr):
        t = pl.program_id(1)
        j, i = kj_ref[t], qi_ref[t]

        @pl.when(t == 0)
        def _():
            dq_ref[...] = jnp.zeros_like(dq_ref)

        @pl.when(i == j)
        def _():
            dk_scr[...] = jnp.zeros_like(dk_scr)
            dv_scr[...] = jnp.zeros_like(dv_scr)

        def step(diagonal):
            rows = pl.ds(pl.multiple_of(i * FTQ, FTQ), FTQ)
            for e in range(FHS):
                es, vs = slice(e * 128, (e + 1) * 128), slice(e * F_HD, (e + 1) * F_HD)
                qe, ke = q_ref[:, es], k_ref[:, es]
                dob = do_ref[:, vs]
                s = _nt(qe, ke)
                if diagonal:
                    s = jnp.where(_iota((FTQ, FTQ), 0) >= _iota((FTQ, FTQ), 1), s, NEG)
                p = jnp.exp(s - lse_ref[:, e * F_HD:e * F_HD + 1])
                ds = p * (_nt(dob, v_ref[:, e * 128:e * 128 + F_HD]) - dl_ref[:, e * F_HD:e * F_HD + 1])
                dsb = ds.astype(bf16)
                dv_scr[e] += _tn(dob, p.astype(bf16))
                dk_scr[e] += _tn(qe, dsb)
                dq_ref[rows, es] += _nn(dsb, ke)

        pl.when(i > j)(functools.partial(step, False))
        pl.when(i == j)(functools.partial(step, True))

        @pl.when(i == nq - 1)
        def _():
            for e in range(FHS):
                dk_ref[:, e * 128:(e + 1) * 128] = dk_scr[e].T
                dv_ref[:, e * F_HD:(e + 1) * F_HD] = dv_scr[e].T

    sd = jax.ShapeDtypeStruct
    qi = lambda w: pl.BlockSpec((FTQ, w * FHS), lambda p, t, kj_, qi_: (qi_[t], p))
    kj = lambda w: pl.BlockSpec((FTQ, w * FHS), lambda p, t, kj_, qi_: (kj_[t], p))
    return _call(
        body, name="fox_attn_bwd", grid=(F_H // FHS, len(live)),
        in_specs=[qi(128), kj(128), kj(128), qi(F_HD), qi(F_HD), qi(F_HD)],
        out_specs=[pl.BlockSpec((S, 128 * FHS), lambda p, t, kj_, qi_: (0, p)), kj(128), kj(F_HD)],
        out_shape=[sd((S, 2048), f32), sd((S, 2048), f32), sd((S, 1024), f32)],
        scratch_shapes=[pltpu.VMEM((FHS, 128, FTQ), f32), pltpu.VMEM((FHS, F_HD, FTQ), f32)],
        sem=("parallel", "arbitrary"), args=(qa, ka, v, do, lse, delta), comm=comm, prefetch=(kj_tab, qi_tab))


def fox_gate(o, proj):
    S = o.shape[0]
    tm = 512

    def body(o_ref, z_ref, o2_ref):
        o2_ref[...] = (o_ref[...] * _silu(z_ref[...])).astype(bf16)

    t = pl.BlockSpec((tm, 1024), lambda i: (i, 0))
    return pl.pallas_call(
        body, name="fox_gate", grid=(S // tm,),
        in_specs=[t, pl.BlockSpec((tm, 1024), lambda i: (i, 3))], out_specs=t,
        out_shape=jax.ShapeDtypeStruct((S, 1024), bf16),
        compiler_params=_params(("parallel",)),
    )(o, proj)


def fox_gate_bwd(dy, w_out, o, proj):
    S = o.shape[0]
    tm = 512

    def body(dy_ref, w_ref, o_ref, z_ref, do_ref, dz_ref, dl_ref):
        d_all = _nt(dy_ref[...], w_ref[...])
        lo_half = _iota((tm, 128), 1) < F_HD
        for p in range(F_H // 2):
            ps = slice(p * 128, (p + 1) * 128)
            d2, ov, z = d_all[:, ps], o_ref[:, ps], z_ref[:, ps]
            dov = d2 * _silu(z)
            do_ref[:, ps] = dov.astype(bf16)
            dz_ref[:, ps] = (d2 * ov * _dsilu(z)).astype(bf16)
            dl_ref[:, ps] = _half_mean(dov * ov, lo_half) * float(F_HD)

    t = pl.BlockSpec((tm, 1024), lambda i: (i, 0))
    sd = jax.ShapeDtypeStruct
    return pl.pallas_call(
        body, name="fox_gate_bwd", grid=(S // tm,),
        in_specs=[t, pl.BlockSpec(w_out.shape, lambda i: (0, 0)), t, pl.BlockSpec((tm, 1024), lambda i: (i, 3))],
        out_specs=[t, t, t],
        out_shape=[sd((S, 1024), bf16), sd((S, 1024), bf16), sd((S, 1024), f32)],
        compiler_params=_params(("parallel",)),
    )(dy, w_out, o, proj)


def fox_pre_bwd(proj, fbias, qw2, kw2, dqa, dka, dv, dz):
    S = proj.shape[0]
    tm = 256
    nb = S // tm

    def body(q_ref, k_ref, f_ref, fb_ref, qw_ref, kw_ref, dqa_ref, dka_ref, dv_ref, dz_ref, dp_ref, st_ref, carry):
        i = pl.program_id(0)

        @pl.when(i == 0)
        def _():
            carry[...] = jnp.zeros_like(carry)

        lane = _iota((tm, 128), 1)
        lo_half = lane < F_HD
        lane16 = _iota((tm, 16), 1)
        dcum = jnp.zeros((tm, 16), f32)
        dws = []
        for src, w_ref, dsrc, is_q, col0 in ((q_ref, qw_ref, dqa_ref, True, 0), (k_ref, kw_ref, dka_ref, False, 1024)):
            dw = jnp.zeros((1, 128), f32)
            for p in range(F_H // 2):
                ps = slice(p * 128, (p + 1) * 128)
                x = src[:, ps]
                r = lax.rsqrt(_half_mean(x * x, lo_half) + EPS)
                xh = x * r
                d0 = dsrc[:, (2 * p) * 128:(2 * p + 1) * 128]
                d1 = dsrc[:, (2 * p + 1) * 128:(2 * p + 2) * 128]
                dy = jnp.where(lo_half, d0, pltpu.roll(d1, F_HD, 1))
                if is_q:
                    dy = dy * (F_HD ** -0.5)
                dxh = dy * w_ref[...]
                dw = dw + jnp.sum(dy * xh, axis=0, keepdims=True)
                dp_ref[:, col0 + p * 128:col0 + (p + 1) * 128] = (r * (dxh - xh * _half_mean(dxh * xh, lo_half))).astype(bf16)
                for e, de in ((0, d0), (1, d1)):
                    col = de[:, 64:65] if is_q else -de[:, 67:68]
                    dcum = dcum + jnp.where(lane16 == 2 * p + e, col, 0.0)
            dws.append(dw)
        dp_ref[:, 2048:3072] = dv_ref[...].astype(bf16)
        dp_ref[:, 3072:4096] = dz_ref[...]
        utri = jnp.where(_iota((tm, tm), 1) >= _iota((tm, tm), 0), 1.0, 0.0).astype(f32)
        dlogf = _nn(utri, dcum, HI) + carry[0:1, :]
        carry[0:1, :] = dlogf[0:1, :]
        fl = f_ref[:, 0:16] + fb_ref[...]
        df = dlogf * _sigmoid(-fl)
        place = jnp.where(_iota((16, 128), 1) == _iota((16, 128), 0), 1.0, 0.0).astype(f32)
        dfw = _nn(df, place, HI)
        dp_ref[:, F_F0:F_INP] = dfw.astype(bf16)
        upd = jnp.concatenate(dws + [jnp.sum(dfw, axis=0, keepdims=True), jnp.zeros((5, 128), f32)], axis=0)

        @pl.when(i == 0)
        def _():
            st_ref[...] = upd

        @pl.when(i > 0)
        def _():
            st_ref[...] += upd

    rev = lambda w, c: pl.BlockSpec((tm, w), lambda i: (nb - 1 - i, c))
    vec = lambda n: pl.BlockSpec((1, n), lambda i: (0, 0))
    sd = jax.ShapeDtypeStruct
    return pl.pallas_call(
        body, name="fox_pre_bwd", grid=(nb,),
        in_specs=[rev(1024, 0), rev(1024, 1), rev(128, F_F0 // 128), vec(16), vec(128), vec(128),
                  rev(2048, 0), rev(2048, 0), rev(1024, 0), rev(1024, 0)],
        out_specs=[rev(F_INP, 0), pl.BlockSpec((8, 128), lambda i: (0, 0))],
        out_shape=[sd((S, F_INP), bf16), sd((8, 128), f32)],
        scratch_shapes=[pltpu.VMEM((8, 16), f32)],
        compiler_params=_params(("arbitrary",)),
    )(proj, proj, proj, fbias, qw2, kw2, dqa, dka, dv, dz)


def _me():
    return lax.axis_index("x"), lax.axis_index("y"), lax.axis_index("c")


def _other_chips(x, y):
    return [(1 - x, y), (x, 1 - y), (1 - x, 1 - y)]


def ag_small(xs):
    m_per, n = xs.shape

    def body(x_ref, out_ref, send_sems, recv_sems, local_sem):
        x, y, c = _me()
        me, sibling = (x, y, c), (x, y, 1 - c)
        chips = _other_chips(x, y)

        def rows(px, py, pc):
            return out_ref.at[pl.ds((4 * px + 2 * py + pc) * m_per, m_per), :]

        def copy(k, block, to, src=None):
            return pltpu.make_async_remote_copy(
                src_ref=rows(*block) if src is None else src, dst_ref=rows(*block),
                send_sem=send_sems.at[k], recv_sem=recv_sems.at[k], device_id=to, device_id_type=MESH)

        mine = pltpu.make_async_copy(x_ref, rows(*me), local_sem)
        mine.start()
        first = [copy(0, me, sibling, src=x_ref)]
        first += [copy(1 + j, me, (*chip, c), src=x_ref) for j, chip in enumerate(chips)]
        for cp in first:
            cp.start()
        passed = [copy(4 + j, (*chip, c), sibling) for j, chip in enumerate(chips)]
        for j, chip in enumerate(chips):
            copy(1 + j, (*chip, c), me).wait_recv()
            passed[j].start()
        copy(0, sibling, me).wait_recv()
        for j, chip in enumerate(chips):
            copy(4 + j, (*chip, 1 - c), me).wait_recv()
        for cp in first + passed:
            cp.wait_send()
        mine.wait()

    return pl.pallas_call(
        body, name="ag_small",
        out_shape=jax.ShapeDtypeStruct((8 * m_per, n), xs.dtype),
        in_specs=[pl.BlockSpec(memory_space=pltpu.VMEM)], out_specs=pl.BlockSpec(memory_space=pltpu.VMEM),
        scratch_shapes=[pltpu.SemaphoreType.DMA((7,)), pltpu.SemaphoreType.DMA((7,)), pltpu.SemaphoreType.DMA],
        compiler_params=pltpu.CompilerParams(vmem_limit_bytes=VMEM_LIMIT),
    )(xs)


_ANY = pl.BlockSpec(memory_space=pl.ANY)


def ag_chips(arrs):
    n = len(arrs)
    assert all(a.shape[0] == 2 for a in arrs)

    def body(*refs):
        ins, outs = refs[:n], refs[n:2 * n]
        send_sems, recv_sems, fwd_send, fwd_recv, local_sems = refs[2 * n:]
        x, y, c = _me()
        me = 2 * x + y
        chips = _other_chips(x, y)
        started = []
        for a in range(n):
            cp = pltpu.make_async_copy(ins[a], outs[a].at[me], local_sems.at[a])
            cp.start()
            started.append(cp)
        sends = []
        for a in range(n):
            for j, (px, py) in enumerate(chips):
                r = pltpu.make_async_remote_copy(
                    src_ref=ins[a].at[c], dst_ref=outs[a].at[me, c], send_sem=send_sems.at[3 * a + j],
                    recv_sem=recv_sems.at[3 * a + j], device_id=(px, py, c), device_id_type=MESH)
                r.start()
                sends.append(r)
        for a in range(n):
            for j, (px, py) in enumerate(chips):
                got = outs[a].at[2 * px + py, c]
                pltpu.make_async_remote_copy(
                    src_ref=ins[a].at[c], dst_ref=got, send_sem=send_sems.at[3 * a + j],
                    recv_sem=recv_sems.at[3 * a + j], device_id=(px, py, c), device_id_type=MESH).wait_recv()
                f = pltpu.make_async_remote_copy(
                    src_ref=got, dst_ref=got, send_sem=fwd_send.at[3 * a + j], recv_sem=fwd_recv.at[3 * a + j],
                    device_id=(x, y, 1 - c), device_id_type=MESH)
                f.start()
                sends.append(f)
        for a in range(n):
            for j, (px, py) in enumerate(chips):
                theirs = outs[a].at[2 * px + py, 1 - c]
                pltpu.make_async_remote_copy(
                    src_ref=theirs, dst_ref=theirs, send_sem=fwd_send.at[3 * a + j], recv_sem=fwd_recv.at[3 * a + j],
                    device_id=(x, y, 1 - c), device_id_type=MESH).wait_recv()
        for r in sends:
            r.wait_send()
        for cp in started:
            cp.wait()

    sems = pltpu.SemaphoreType.DMA((3 * n,))
    return pl.pallas_call(
        body, name="ag_chips",
        out_shape=[jax.ShapeDtypeStruct((4,) + a.shape, a.dtype) for a in arrs],
        in_specs=[_ANY] * n, out_specs=[_ANY] * n,
        scratch_shapes=[sems, sems, sems, sems, pltpu.SemaphoreType.DMA((n,))],
    )(*arrs)


def _ag_comm(arrs):
    n = len(arrs)

    def copies(ins, outs, sems, inbound):
        send_sems, recv_sems, local_sems = sems
        x, y, c = _me()
        me = 2 * x + y
        local = [pltpu.make_async_copy(ins[a], outs[a].at[me], local_sems.at[a]) for a in range(n)]
        out_cp, in_cp = [], []
        for a in range(n):
            for j, (px, py) in enumerate(_other_chips(x, y)):
                mk = functools.partial(pltpu.make_async_remote_copy, src_ref=ins[a], send_sem=send_sems.at[3 * a + j],
                                       recv_sem=recv_sems.at[3 * a + j], device_id=(px, py, c), device_id_type=MESH)
                out_cp.append(mk(dst_ref=outs[a].at[me]))
                if inbound:
                    in_cp.append(mk(dst_ref=outs[a].at[2 * px + py]))
        return local, out_cp, in_cp

    def start(ins, outs, sems):
        local, out_cp, _ = copies(ins, outs, sems, False)
        for cp in local + out_cp:
            cp.start()

    def wait(ins, outs, sems):
        local, out_cp, in_cp = copies(ins, outs, sems, True)
        for cp in in_cp:
            cp.wait_recv()
        for cp in out_cp:
            cp.wait_send()
        for cp in local:
            cp.wait()

    sems = [pltpu.SemaphoreType.DMA((3 * n,)), pltpu.SemaphoreType.DMA((3 * n,)), pltpu.SemaphoreType.DMA((n,))]
    return _Comm(arrs, [jax.ShapeDtypeStruct((4,) + a.shape, a.dtype) for a in arrs], sems, start, wait)


def _rs_comm(gs):
    n = len(gs)
    flips = [(fx, fy, fc) for fx in (0, 1) for fy in (0, 1) for fc in (0, 1)][1:]

    def copies(ins, outs, sems, inbound):
        send_sems, recv_sems, local_sems = sems
        x, y, c = _me()
        me = 4 * x + 2 * y + c
        local, out_cp, in_cp = [], [], []
        for a in range(n):
            rh = ins[a].shape[1] // 2
            mine = ins[a].at[2 * x + y, pl.ds(c * rh, rh), :]
            local.append(pltpu.make_async_copy(mine, outs[a].at[me], local_sems.at[a]))
            for j, (fx, fy, fc) in enumerate(flips):
                px, py, pc = (1 - x if fx else x), (1 - y if fy else y), (1 - c if fc else c)
                mk = functools.partial(pltpu.make_async_remote_copy, send_sem=send_sems.at[7 * a + j],
                                       recv_sem=recv_sems.at[7 * a + j], device_id=(px, py, pc), device_id_type=MESH)
                out_cp.append(mk(src_ref=ins[a].at[2 * px + py, pl.ds(pc * rh, rh), :], dst_ref=outs[a].at[me]))
                if inbound:
                    in_cp.append(mk(src_ref=mine, dst_ref=outs[a].at[4 * px + 2 * py + pc]))
        return local, out_cp, in_cp

    def start(ins, outs, sems):
        local, out_cp, _ = copies(ins, outs, sems, False)
        for cp in local + out_cp:
            cp.start()

    def wait(ins, outs, sems):
        local, out_cp, in_cp = copies(ins, outs, sems, True)
        for cp in in_cp:
            cp.wait_recv()
        for cp in out_cp:
            cp.wait_send()
        for cp in local:
            cp.wait()

    sems = [pltpu.SemaphoreType.DMA((7 * n,)), pltpu.SemaphoreType.DMA((7 * n,)), pltpu.SemaphoreType.DMA((n,))]
    return _Comm(gs, [jax.ShapeDtypeStruct((8, g.shape[1] // 2, g.shape[2]), g.dtype) for g in gs], sems, start, wait)


def sum_leading(q, name):
    K, R, C = q.shape
    tr = _pick(R, (256, 128, 64, 32, 16, 8))

    def body(q_ref, o_ref):
        acc = q_ref[0]
        for k in range(1, K):
            acc = acc + q_ref[k]
        o_ref[...] = acc

    return pl.pallas_call(
        body, name=name, grid=(R // tr,),
        in_specs=[pl.BlockSpec((K, tr, C), lambda i: (0, i, 0))], out_specs=pl.BlockSpec((tr, C), lambda i: (i, 0)),
        out_shape=jax.ShapeDtypeStruct((R, C), f32),
        compiler_params=_params(("parallel",)),
    )(q)


def rs_sum_devices(q, cidx, layer, n_layers, into=None):
    K, R, C = q.shape
    tr = _pick(R, (256, 128))

    def body(c_ref, q_ref, *rest):
        acc = q_ref[0].astype(f32)
        for k in range(1, K):
            acc = acc + q_ref[k].astype(f32)
        rest[-1][0, 0] = acc

    return pl.pallas_call(
        body, name="rs_sum_devices",
        grid_spec=pltpu.PrefetchScalarGridSpec(
            num_scalar_prefetch=1, grid=(R // tr,),
            in_specs=[pl.BlockSpec((K, tr, C), lambda i, c_ref: (0, i, 0))] + ([] if into is None else [_ANY]),
            out_specs=pl.BlockSpec((1, 1, tr, C), lambda i, c_ref: (layer, c_ref[0], i, 0))),
        out_shape=jax.ShapeDtypeStruct((n_layers, 2, R, C), f32),
        input_output_aliases={} if into is None else {2: 0},
        compiler_params=_params(("parallel",)),
    )(cidx, q, *([] if into is None else [into]))


def rs_share_halves(rs):
    n = len(rs)

    def body(*refs):
        bufs = refs[n:2 * n]
        send_sems, recv_sems = refs[2 * n:]
        x, y, c = _me()
        cps = []
        for a in range(n):
            mine = bufs[a].at[pl.ds(0, bufs[a].shape[0]), c]
            cp = pltpu.make_async_remote_copy(
                src_ref=mine, dst_ref=mine, send_sem=send_sems.at[a], recv_sem=recv_sems.at[a],
                device_id=(x, y, 1 - c), device_id_type=MESH)
            cp.start()
            cps.append(cp)
        for a, cp in enumerate(cps):
            theirs = bufs[a].at[pl.ds(0, bufs[a].shape[0]), 1 - c]
            pltpu.make_async_remote_copy(
                src_ref=theirs, dst_ref=theirs, send_sem=send_sems.at[a], recv_sem=recv_sems.at[a],
                device_id=(x, y, 1 - c), device_id_type=MESH).wait_recv()
            cp.wait_send()

    return pl.pallas_call(
        body, name="rs_share_halves",
        out_shape=[jax.ShapeDtypeStruct(r.shape, r.dtype) for r in rs],
        in_specs=[_ANY] * n, out_specs=[_ANY] * n, input_output_aliases={a: a for a in range(n)},
        scratch_shapes=[pltpu.SemaphoreType.DMA((n,)), pltpu.SemaphoreType.DMA((n,))],
    )(*rs)


def ada_mod(c_all, ada_w):
    L, _, n = ada_w.shape

    def body(c_ref, w_ref, o_ref):
        o_ref[0] = _nn(_silu(c_ref[...]), w_ref[0], HI)

    return pl.pallas_call(
        body, name="ada_mod", grid=(L,),
        in_specs=[pl.BlockSpec((8, D), lambda l: (0, 0)), pl.BlockSpec((1, D, n), lambda l: (l, 0, 0))],
        out_specs=pl.BlockSpec((1, 8, n), lambda l: (l, 0, 0)),
        out_shape=jax.ShapeDtypeStruct((L, 8, n), f32),
        compiler_params=_params(("parallel",)),
    )(c_all, ada_w)


def ada_w_grad(c_all, dmod):
    L, _, n = dmod.shape

    def body(c_ref, d_ref, o_ref):
        o_ref[0] = _tn(_silu(c_ref[...]), d_ref[0], HI)

    return pl.pallas_call(
        body, name="ada_w_grad", grid=(L,),
        in_specs=[pl.BlockSpec((8, D), lambda l: (0, 0)), pl.BlockSpec((1, 8, n), lambda l: (l, 0, 0))],
        out_specs=pl.BlockSpec((1, D, n), lambda l: (l, 0, 0)),
        out_shape=jax.ShapeDtypeStruct((L, D, n), f32),
        compiler_params=_params(("parallel",)),
    )(c_all, dmod)


def adamw(w, g, m, v, name):
    shp = w.shape
    two = lambda a: a.reshape(-1, shp[-1])
    R, C = two(w).shape
    tr = _pick(R, (256, 128, 64, 32, 16, 8))
    bc1, bc2 = 1.0 - B1 ** STEP, 1.0 - B2 ** STEP

    def body(w_ref, g_ref, m_ref, v_ref, d_ref, mo_ref, vo_ref):
        gv = g_ref[...]
        mn = B1 * m_ref[...] + (1.0 - B1) * gv
        vn = B2 * v_ref[...] + (1.0 - B2) * (gv * gv)
        d_ref[...] = -LR * ((mn / bc1) / (jnp.sqrt(vn / bc2) + AEPS) + WD * w_ref[...])
        mo_ref[...] = mn
        vo_ref[...] = vn

    t = pl.BlockSpec((tr, C), lambda i: (i, 0))
    outs = pl.pallas_call(
        body, name=name, grid=(R // tr,),
        in_specs=[t] * 4, out_specs=[t] * 3, out_shape=[jax.ShapeDtypeStruct((R, C), f32)] * 3,
        compiler_params=_params(("parallel",)),
    )(two(w), two(g), two(m), two(v))
    return [o.reshape(shp) for o in outs]


def _pack(arrs):
    parts, offs, r0 = [], [], 0
    for a in arrs:
        n = a.size
        rows = -(-n // 1024) * 8
        parts.append(jnp.pad(a.reshape(-1), (0, rows * 128 - n)).reshape(rows, 128))
        offs.append((r0, rows))
        r0 += rows
    return jnp.concatenate(parts, axis=0), offs


def _unpack(buf, offs, shapes):
    out = []
    for (r0, rows), shp in zip(offs, shapes):
        n = 1
        for d in shp:
            n *= d
        out.append(buf[..., r0:r0 + rows, :].reshape(buf.shape[:-2] + (rows * 128,))[..., :n].reshape(buf.shape[:-2] + tuple(shp)))
    return out


def kernel(x, c, norm_w, ada_w, ada_b, a_w_in, a_conv_w, a_A_log, a_dt_bias, a_norm_w, a_w_out, b_w_in, b_f_bias, b_qn_w, b_kn_w, b_w_out, final_norm_w, loss_target, m_norm_w, m_ada_w, m_ada_b, m_a_w_in, m_a_conv_w, m_a_A_log, m_a_dt_bias, m_a_norm_w, m_a_w_out, m_b_w_in, m_b_f_bias, m_b_qn_w, m_b_kn_w, m_b_w_out, m_final_norm_w, v_norm_w, v_ada_w, v_ada_b, v_a_w_in, v_a_conv_w, v_a_A_log, v_a_dt_bias, v_a_norm_w, v_a_w_out, v_b_w_in, v_b_f_bias, v_b_qn_w, v_b_kn_w, v_b_w_out, v_final_norm_w):
    weights = dict(norm_w=norm_w, ada_w=ada_w, ada_b=ada_b, a_w_in=a_w_in, a_conv_w=a_conv_w, a_A_log=a_A_log,
                   a_dt_bias=a_dt_bias, a_norm_w=a_norm_w, a_w_out=a_w_out, b_w_in=b_w_in, b_f_bias=b_f_bias,
                   b_qn_w=b_qn_w, b_kn_w=b_kn_w, b_w_out=b_w_out, final_norm_w=final_norm_w)
    m_in = dict(norm_w=m_norm_w, ada_w=m_ada_w, ada_b=m_ada_b, a_w_in=m_a_w_in, a_conv_w=m_a_conv_w, a_A_log=m_a_A_log,
                a_dt_bias=m_a_dt_bias, a_norm_w=m_a_norm_w, a_w_out=m_a_w_out, b_w_in=m_b_w_in, b_f_bias=m_b_f_bias,
                b_qn_w=m_b_qn_w, b_kn_w=m_b_kn_w, b_w_out=m_b_w_out, final_norm_w=m_final_norm_w)
    v_in = dict(norm_w=v_norm_w, ada_w=v_ada_w, ada_b=v_ada_b, a_w_in=v_a_w_in, a_conv_w=v_a_conv_w, a_A_log=v_a_A_log,
                a_dt_bias=v_a_dt_bias, a_norm_w=v_a_norm_w, a_w_out=v_a_w_out, b_w_in=v_b_w_in, b_f_bias=v_b_f_bias,
                b_qn_w=v_b_qn_w, b_kn_w=v_b_kn_w, b_w_out=v_b_w_out, final_norm_w=v_final_norm_w)
    xi, yi, ci = _me()
    me_b, me_k = 4 * xi + 2 * yi + ci, 2 * xi + yi
    cidx = ci.astype(jnp.int32).reshape(1)
    S = x.shape[1]
    depth, n_a, n_b = norm_w.shape[0], a_w_in.shape[0], b_w_in.shape[0]
    x0, tgt = x.reshape(S, D), loss_target.reshape(S, D)

    c_all = ag_small(jnp.pad(c, ((0, 7), (0, 0)))).reshape(8, 8, D)[:, 0]
    nloc = ada_w.shape[2]
    parts = ag_small(ada_mod(c_all, ada_w).reshape(depth * 8, nloc)).reshape(4, 2, depth, 8, nloc)[:, 0]
    mine = lax.dynamic_index_in_dim(parts, me_b, axis=2, keepdims=False)
    mod = jnp.transpose(mine, (1, 0, 2)).reshape(depth, 4 * nloc) + ada_b
    shift, scale, gate = (mod[:, k * D:(k + 1) * D] for k in range(3))

    w_loc = [(a_w_in[i // 2] if i % 2 == 0 else b_w_in[i // 2]).astype(bf16) for i in range(depth)]
    wo_loc = [(a_w_out[i // 2] if i % 2 == 0 else b_w_out[i // 2]).astype(bf16) for i in range(depth)]
    pad_in = [(G_INP - G_IN) if i % 2 == 0 else (F_INP - F_IN) for i in range(depth)]
    halves = lambda w: w.reshape((2, w.shape[0] // 2) + w.shape[1:])

    def cols_in_place(g_in, pad):
        w = jnp.transpose(g_in, (1, 0, 2)).reshape(g_in.shape[1], -1)
        return jnp.pad(w, ((0, 0), (0, pad)))

    g_in0, g_conv = ag_chips([halves(w_loc[0]), a_conv_w])
    w_in_full = [cols_in_place(g_in0.reshape((4,) + w_loc[0].shape), pad_in[0])]
    w_out_full = []
    conv = [jnp.transpose(g_conv[:, l], (1, 0, 2)).reshape(CONV_K, -1) for l in range(n_a)]
    qw2 = [_row(jnp.tile(b_qn_w[l], 2)) for l in range(n_b)]
    kw2 = [_row(jnp.tile(b_kn_w[l], 2)) for l in range(n_b)]

    saved, xc = [], x0
    for i in range(depth):
        l = i // 2
        nxt = _ag_comm([w_loc[i + 1], wo_loc[i + 1]]) if i + 1 < depth else None
        h = ln_mod(xc, _row(norm_w[i]), _row(scale[i]), _row(shift[i]))
        name = "mm_a_in" if i % 2 == 0 else "mm_b_in"
        if i == 0:
            proj, got = matmul(h, w_in_full[0], "nn", name, comm=_ag_comm([wo_loc[0]]))
            w_out_full.append(got[0].reshape(-1, D))
        else:
            proj = matmul(h, w_in_full[i], "nn", name)
        if i % 2 == 0:
            pre = gdn_pre(proj, conv[l], _row(a_A_log[l]), _row(a_dt_bias[l]))
            res, got = gdn_fwd(*pre, comm=nxt)
            o2 = gdn_onorm(res[0], proj, _row(a_norm_w[l]))
            y, xn = out_proj(o2, w_out_full[i], xc, _row(gate[i]), "out_proj_a")
        else:
            pre = fox_pre(proj, _row(b_f_bias[l]), qw2[l], kw2[l])
            res, got = fox_attn(*pre, comm=nxt)
            o2 = fox_gate(res[0], proj)
            y, xn = out_proj(o2, w_out_full[i], xc, _row(gate[i]), "out_proj_b")
        saved.append((xc, h, proj, o2, y, pre, res))
        if nxt is not None:
            w_in_full.append(cols_in_place(got[0], pad_in[i + 1]))
            w_out_full.append(got[1].reshape(-1, D))
        xc = xn
    dx, st_f = final_loss(xc, _row(final_norm_w), tgt)

    d_norm, d_mod = [None] * depth, [None] * depth
    d_conv, d_alog, d_dtb, d_anw = [None] * n_a, [None] * n_a, [None] * n_a, [None] * n_a
    d_fb, d_qn, d_kn = [None] * n_b, [None] * n_b, [None] * n_b
    ex_in, ex_out, pend_in = [None] * depth, [None] * depth, None
    for i in reversed(range(depth)):
        l = i // 2
        xin, h, proj, o2, y, pre, res = saved[i]
        ab = "a" if i % 2 == 0 else "b"
        dy, st_g = gate_bwd(dx, y, _row(gate[i]))
        d_out = matmul(o2, dy, "tn", f"mm_{ab}_dwo", out_dtype=bf16)
        ride = _rs_comm(([] if pend_in is None else [pend_in]) + [d_out.reshape(4, d_out.shape[0] // 4, D)])
        if i % 2 == 0:
            o, wv, at, tinv, vn, st = res
            do, dz, st_o = gdn_onorm_bwd(dy, w_out_full[i], o, proj, _row(a_norm_w[l]))
            grads, got = gdn_bwd(do, *pre, wv, at, tinv, vn, st, comm=ride)
            dcv, dba, st_s = gdn_pre_bwd(proj, conv[l], _row(a_A_log[l]), _row(a_dt_bias[l]), *grads)
            dproj, dcw = gdn_conv_bwd(proj, conv[l], dcv, dz, dba)
            d_conv[l], d_alog[l], d_dtb[l], d_anw[l] = dcw[:CONV_K], st_s[0], st_s[1], st_o[0]
        else:
            o, lse = res
            do, dz, delta = fox_gate_bwd(dy, w_out_full[i], o, proj)
            (dqa, dka, dv), got = fox_attn_bwd(*pre, do, lse, delta, comm=ride)
            dproj, st_b = fox_pre_bwd(proj, _row(b_f_bias[l]), qw2[l], kw2[l], dqa, dka, dv, dz)
            d_fb[l], d_qn[l], d_kn[l] = st_b[2, :F_H], st_b[0, :F_HD] + st_b[0, F_HD:], st_b[1, :F_HD] + st_b[1, F_HD:]
        ex_out[i] = got[-1]
        if pend_in is not None:
            ex_in[i + 1] = got[0]
        d_in = matmul(h, dproj, "tn", f"mm_{ab}_dw", out_dtype=bf16)
        cl = w_loc[i].shape[1]
        pend_in = jnp.transpose(d_in[:, :4 * cl].reshape(d_in.shape[0], 4, cl), (1, 0, 2))
        if i == 0:
            dh, got = matmul(dproj, w_in_full[i], "nt", f"mm_{ab}_dh", comm=_rs_comm([pend_in]))
            ex_in[0] = got[0]
        else:
            dh = matmul(dproj, w_in_full[i], "nt", f"mm_{ab}_dh")
        dx, st_n = ln_mod_bwd(xin, _row(norm_w[i]), _row(scale[i]), dh, dx)
        d_norm[i] = st_n[0]
        d_mod[i] = jnp.concatenate([st_n[2], st_n[1], st_g[0]])

    small = [jnp.stack(d_norm), jnp.stack(d_mod), jnp.stack(d_conv), jnp.stack(d_alog), jnp.stack(d_dtb), jnp.stack(d_anw),
             jnp.stack(d_fb), jnp.stack(d_qn), jnp.stack(d_kn), st_f[0], jnp.sum(st_f[1]).reshape(1)]
    shapes = [a.shape for a in small]
    buf, offs = _pack(small)
    gathered = ag_small(buf).reshape(8, buf.shape[0], 128)
    tot = _unpack(sum_leading(gathered, "sum_devices"), offs, shapes)
    g_norm, g_adab, g_convf, g_alog, g_dtb, g_anw, g_fb, g_qn, g_kn, g_fin, loss = tot
    dmod_all = _unpack(gathered, offs[1:2], shapes[1:2])[0]
    dmod_loc = lax.dynamic_slice_in_dim(dmod_all, me_k * nloc, nloc, axis=2)
    g_adaw = ada_w_grad(c_all, jnp.transpose(dmod_loc, (1, 0, 2)))
    g_conv_loc = lax.dynamic_slice_in_dim(g_convf, me_k * a_conv_w.shape[2], a_conv_w.shape[2], axis=2)

    bufs = {}
    for i in range(depth):
        for which, q in (("in", ex_in[i]), ("out", ex_out[i])):
            key = ("a" if i % 2 == 0 else "b", which)
            bufs[key] = rs_sum_devices(q, cidx, i // 2, depth // 2, into=bufs.get(key))
    keys = list(bufs)
    done = dict(zip(keys, rs_share_halves([bufs[k] for k in keys])))
    grads = dict(norm_w=g_norm, ada_w=g_adaw, ada_b=g_adab, a_w_in=done["a", "in"].reshape(a_w_in.shape),
                 a_conv_w=g_conv_loc, a_A_log=g_alog, a_dt_bias=g_dtb, a_norm_w=g_anw,
                 a_w_out=done["a", "out"].reshape(a_w_out.shape), b_w_in=done["b", "in"].reshape(b_w_in.shape),
                 b_f_bias=g_fb, b_qn_w=g_qn, b_kn_w=g_kn, b_w_out=done["b", "out"].reshape(b_w_out.shape),
                 final_norm_w=g_fin)
    names = list(weights)
    upd = {n: adamw(weights[n], grads[n], m_in[n], v_in[n], "adamw_" + n) for n in names}
    return (loss.reshape(()), dx.reshape(x.shape), *[grads[n] for n in names], *[upd[n][0] for n in names],
            *[upd[n][1] for n in names], *[upd[n][2] for n in names])
```

```python
import functools

import jax
import jax.numpy as jnp
from jax import lax
from jax.experimental import pallas as pl
from jax.experimental.pallas import tpu as pltpu

f32, bf16 = jnp.float32, jnp.bfloat16
HI = lax.Precision.HIGHEST
MESH = pl.DeviceIdType.MESH

EPS = 1e-6
D = 1024
CHUNK = 64
GQK_H, GV_H, GHD = 8, 16, 128
G_CONV = 4096
G_Z0 = 4096
G_BA0 = 6144
G_IN, G_INP = 6176, 6272
CONV_K = 4
F_H, F_HD = 16, 64
F_F0 = 4096
F_IN, F_INP = 4112, 4224
LR, B1, B2, AEPS, WD, STEP = 0.001, 0.9, 0.999, 1e-08, 0.01, 10
NEG = -1e30
VMEM_LIMIT = 56 * 1024 * 1024


def _nn(a, b, prec=None):
    return lax.dot_general(a, b, (((1,), (0,)), ((), ())), preferred_element_type=f32, precision=prec)


def _nt(a, b, prec=None):
    return lax.dot_general(a, b, (((1,), (1,)), ((), ())), preferred_element_type=f32, precision=prec)


def _tn(a, b, prec=None):
    return lax.dot_general(a, b, (((0,), (0,)), ((), ())), preferred_element_type=f32, precision=prec)


def _iota(shape, axis):
    return lax.broadcasted_iota(jnp.int32, shape, axis)


def _sigmoid(x):
    return 0.5 * jnp.tanh(0.5 * x) + 0.5


def _softplus(x):
    return jnp.maximum(x, 0.0) + jnp.log(1.0 + jnp.exp(-jnp.abs(x)))


def _silu(x):
    return x * _sigmoid(x)


def _dsilu(x):
    s = _sigmoid(x)
    return s * (1.0 + x * (1.0 - s))


def _params(sem=None, vmem=VMEM_LIMIT):
    return pltpu.CompilerParams(dimension_semantics=sem, vmem_limit_bytes=vmem)


def _row(v):
    return v.reshape(1, -1)


class _Comm:
    def __init__(self, ins, out_shapes, sems, start, wait):
        self.ins, self.out_shapes, self.sems, self.start, self.wait = list(ins), list(out_shapes), list(sems), start, wait


def _call(body, *, name, grid, in_specs, out_specs, out_shape, scratch_shapes, sem, args, comm=None, prefetch=()):
    n_pf, n_in, n_out, n_s = len(prefetch), len(in_specs), len(out_specs), len(scratch_shapes)
    n_ci, n_co = (len(comm.ins), len(comm.out_shapes)) if comm is not None else (0, 0)

    def wrapped(*refs):
        pf, refs = refs[:n_pf], refs[n_pf:]
        core_in, c_in = refs[:n_in], refs[n_in:n_in + n_ci]
        o0 = n_in + n_ci
        core_out, c_out = refs[o0:o0 + n_out], refs[o0 + n_out:o0 + n_out + n_co]
        s0 = o0 + n_out + n_co
        core_s, c_sem = refs[s0:s0 + n_s], refs[s0 + n_s:]
        if comm is not None:
            first = functools.reduce(jnp.logical_and, [pl.program_id(d) == 0 for d in range(len(grid))])
            pl.when(first)(functools.partial(comm.start, c_in, c_out, c_sem))
        body(*pf, *core_in, *core_out, *core_s)
        if comm is not None:
            last = functools.reduce(jnp.logical_and, [pl.program_id(d) == grid[d] - 1 for d in range(len(grid))])
            pl.when(last)(functools.partial(comm.wait, c_in, c_out, c_sem))

    extra = ([], [], [], []) if comm is None else ([_ANY] * n_ci, [_ANY] * n_co, comm.out_shapes, comm.sems)
    spec = pltpu.PrefetchScalarGridSpec(
        num_scalar_prefetch=n_pf, grid=grid, in_specs=list(in_specs) + extra[0], out_specs=list(out_specs) + extra[1],
        scratch_shapes=list(scratch_shapes) + extra[3])
    outs = pl.pallas_call(
        wrapped, name=name if comm is None else name + "_x", grid_spec=spec, out_shape=list(out_shape) + extra[2],
        compiler_params=_params(sem if comm is None else ("arbitrary",) * len(grid)),
    )(*prefetch, *args, *(comm.ins if comm is not None else []))
    return outs[:n_out], outs[n_out:]


def _pick(n, pref):
    for t in pref:
        if n % t == 0:
            return t
    return n


MM_VMEM_BUDGET = 44 * 1024 * 1024


def _mm_tiles(M, N, K):
    best = None
    for tk in [K] + [t for t in (2048, 1408, 1024, 896, 512, 384, 256, 128) if K % t == 0 and t < K]:
        for tm in (2048, 1024, 512, 256, 128):
            for tn in (1408, 1024, 896, 512, 384, 256, 128):
                if M % tm or N % tn:
                    continue
                nk = K // tk
                need = 2 * 2 * (tm * tk + tk * tn) + 2 * 4 * tm * tn + (4 * tm * tn if nk > 1 else 0)
                if need <= MM_VMEM_BUDGET:
                    cand = ((nk, -tm * tn), (tm, tn, tk))
                    best = cand if best is None or cand[0] < best[0] else best
    return best[1]


def matmul(a, b, mode, name, out_dtype=f32, comm=None):
    if mode == "nn":
        (M, K), (_, N) = a.shape, b.shape
    elif mode == "nt":
        (M, K), (N, _) = a.shape, b.shape
    else:
        (K, M), (_, N) = a.shape, b.shape
    tm, tn, tk = _mm_tiles(M, N, K)
    nk = K // tk
    dot = {"nn": _nn, "nt": _nt, "tn": _tn}[mode]

    def body(a_ref, b_ref, o_ref, *acc):
        k = pl.program_id(2)
        part = dot(a_ref[...], b_ref[...])
        if nk == 1:
            o_ref[...] = part.astype(out_dtype)
        else:
            acc_ref = acc[0]

            @pl.when(k == 0)
            def _():
                acc_ref[...] = part

            @pl.when(k > 0)
            def _():
                acc_ref[...] += part

            @pl.when(k == nk - 1)
            def _():
                o_ref[...] = acc_ref[...].astype(out_dtype)

    a_spec = pl.BlockSpec((tk, tm), lambda i, j, k: (k, i)) if mode == "tn" else pl.BlockSpec((tm, tk), lambda i, j, k: (i, k))
    b_spec = pl.BlockSpec((tn, tk), lambda i, j, k: (j, k)) if mode == "nt" else pl.BlockSpec((tk, tn), lambda i, j, k: (k, j))
    outs, got = _call(
        body, name=name, grid=(M // tm, N // tn, nk),
        in_specs=[a_spec, b_spec], out_specs=[pl.BlockSpec((tm, tn), lambda i, j, k: (i, j))],
        out_shape=[jax.ShapeDtypeStruct((M, N), out_dtype)],
        scratch_shapes=[] if nk == 1 else [pltpu.VMEM((tm, tn), f32)],
        sem=("parallel", "parallel", "arbitrary"), args=(a, b), comm=comm)
    return outs[0] if comm is None else (outs[0], got)


def out_proj(o2, w, x, gate, name):
    S, K = o2.shape
    N = w.shape[1]
    tm, tn = 1024, 1024

    def body(a_ref, b_ref, x_ref, g_ref, y_ref, xn_ref):
        y = _nn(a_ref[...], b_ref[...])
        y_ref[...] = y
        xn_ref[...] = x_ref[...] + g_ref[...] * y

    return pl.pallas_call(
        body, name=name, grid=(S // tm, N // tn),
        in_specs=[pl.BlockSpec((tm, K), lambda i, j: (i, 0)), pl.BlockSpec((K, tn), lambda i, j: (0, j)),
                  pl.BlockSpec((tm, tn), lambda i, j: (i, j)), pl.BlockSpec((1, tn), lambda i, j: (0, j))],
        out_specs=[pl.BlockSpec((tm, tn), lambda i, j: (i, j))] * 2,
        out_shape=[jax.ShapeDtypeStruct((S, N), f32)] * 2,
        compiler_params=_params(("parallel", "parallel")),
    )(o2, w, x, gate)


def ln_mod(x, nw, scale, shift):
    S = x.shape[0]
    tm = 512

    def body(x_ref, nw_ref, sc_ref, sh_ref, h_ref):
        xv = x_ref[...]
        r = lax.rsqrt(jnp.mean(xv * xv, axis=-1, keepdims=True) + EPS)
        h_ref[...] = ((xv * r) * nw_ref[...] * (1.0 + sc_ref[...]) + sh_ref[...]).astype(bf16)

    vec = pl.BlockSpec((1, D), lambda i: (0, 0))
    return pl.pallas_call(
        body, name="ln_mod", grid=(S // tm,),
        in_specs=[pl.BlockSpec((tm, D), lambda i: (i, 0)), vec, vec, vec],
        out_specs=pl.BlockSpec((tm, D), lambda i: (i, 0)),
        out_shape=jax.ShapeDtypeStruct((S, D), bf16),
        compiler_params=_params(("parallel",)),
    )(x, nw, scale, shift)


def ln_mod_bwd(x, nw, scale, dh, dxres):
    S = x.shape[0]
    tm = 512
    nb = S // tm

    def body(x_ref, nw_ref, sc_ref, dh_ref, dr_ref, dx_ref, st_ref):
        i = pl.program_id(0)
        xv = x_ref[...]
        r = lax.rsqrt(jnp.mean(xv * xv, axis=-1, keepdims=True) + EPS)
        xn = xv * r
        dh = dh_ref[...]
        dxn = dh * (nw_ref[...] * (1.0 + sc_ref[...]))
        dx_ref[...] = dr_ref[...] + r * (dxn - xn * jnp.mean(dxn * xn, axis=-1, keepdims=True))
        p1 = jnp.sum(dh * xn, axis=0, keepdims=True)
        p2 = jnp.sum(dh, axis=0, keepdims=True)
        upd = jnp.concatenate([p1, p1, p2, jnp.zeros((5, D), f32)], axis=0)

        @pl.when(i == 0)
        def _():
            st_ref[...] = upd

        @pl.when(i > 0)
        def _():
            st_ref[...] += upd

        @pl.when(i == nb - 1)
        def _():
            st_ref[0:1, :] = st_ref[0:1, :] * (1.0 + sc_ref[...])
            st_ref[1:2, :] = st_ref[1:2, :] * nw_ref[...]

    vec = pl.BlockSpec((1, D), lambda i: (0, 0))
    tile = pl.BlockSpec((tm, D), lambda i: (i, 0))
    return pl.pallas_call(
        body, name="ln_mod_bwd", grid=(S // tm,),
        in_specs=[tile, vec, vec, tile, tile],
        out_specs=[tile, pl.BlockSpec((8, D), lambda i: (0, 0))],
        out_shape=[jax.ShapeDtypeStruct((S, D), f32), jax.ShapeDtypeStruct((8, D), f32)],
        compiler_params=_params(("arbitrary",)),
    )(x, nw, scale, dh, dxres)


def final_loss(x, fw, tgt):
    S = x.shape[0]
    tm = 512

    def body(x_ref, w_ref, t_ref, dx_ref, st_ref):
        i = pl.program_id(0)
        xv = x_ref[...]
        r = lax.rsqrt(jnp.mean(xv * xv, axis=-1, keepdims=True) + EPS)
        xn = xv * r
        err = xn * w_ref[...] - t_ref[...]
        dy = err * (1.0 / D)
        dxn = dy * w_ref[...]
        dx_ref[...] = r * (dxn - xn * jnp.mean(dxn * xn, axis=-1, keepdims=True))
        p1 = jnp.sum(dy * xn, axis=0, keepdims=True)
        p2 = jnp.sum(err * err, axis=0, keepdims=True) * (0.5 / D)
        upd = jnp.concatenate([p1, p2, jnp.zeros((6, D), f32)], axis=0)

        @pl.when(i == 0)
        def _():
            st_ref[...] = upd

        @pl.when(i > 0)
        def _():
            st_ref[...] += upd

    tile = pl.BlockSpec((tm, D), lambda i: (i, 0))
    return pl.pallas_call(
        body, name="final_loss", grid=(S // tm,),
        in_specs=[tile, pl.BlockSpec((1, D), lambda i: (0, 0)), tile],
        out_specs=[tile, pl.BlockSpec((8, D), lambda i: (0, 0))],
        out_shape=[jax.ShapeDtypeStruct((S, D), f32), jax.ShapeDtypeStruct((8, D), f32)],
        compiler_params=_params(("arbitrary",)),
    )(x, fw, tgt)


def gate_bwd(dx, y, gate):
    S = dx.shape[0]
    tm = 512

    def body(dx_ref, y_ref, g_ref, dy_ref, st_ref):
        i = pl.program_id(0)
        dxv = dx_ref[...]
        dy_ref[...] = (g_ref[...] * dxv).astype(bf16)
        upd = jnp.concatenate([jnp.sum(dxv * y_ref[...], axis=0, keepdims=True), jnp.zeros((7, D), f32)], axis=0)

        @pl.when(i == 0)
        def _():
            st_ref[...] = upd

        @pl.when(i > 0)
        def _():
            st_ref[...] += upd

    tile = pl.BlockSpec((tm, D), lambda i: (i, 0))
    return pl.pallas_call(
        body, name="gate_bwd", grid=(S // tm,),
        in_specs=[tile, tile, pl.BlockSpec((1, D), lambda i: (0, 0))],
        out_specs=[tile, pl.BlockSpec((8, D), lambda i: (0, 0))],
        out_shape=[jax.ShapeDtypeStruct((S, D), bf16), jax.ShapeDtypeStruct((8, D), f32)],
        compiler_params=_params(("arbitrary",)),
    )(dx, y, gate)


def _chunk_mats(tm):
    r, c = _iota((tm, tm), 0), _iota((tm, tm), 1)
    same = jnp.right_shift(r, 6) == jnp.right_shift(c, 6)
    ltri = jnp.where(same & (c <= r), 1.0, 0.0).astype(f32)
    utri = jnp.where(same & (c >= r), 1.0, 0.0).astype(f32)
    bsame = jnp.where(same, 1.0, 0.0).astype(f32)
    return ltri, utri, bsame


def _gdn_scalars(ba, alog, dtb, ltri, bsame):
    beta = _sigmoid(ba[:, 0:16])
    u = ba[:, 16:32] + dtb
    neg_a = -jnp.exp(alog)
    g = neg_a * _softplus(u)
    gc = _nn(ltri, g, HI)
    glast = _nn(bsame, g, HI)
    return beta, u, neg_a, g, gc, glast


def _conv_taps(p_ref, halo_ref, first, gi, ext_scr):
    cs = slice(gi * 128, (gi + 1) * 128)
    tm = p_ref.shape[0]
    cur = p_ref[:, cs]
    ext_scr[gi, 0:8, :] = jnp.where(first, 0.0, halo_ref[:, cs])
    ext_scr[gi, 8:, :] = cur
    return [cur] + [ext_scr[gi, 8 - s:8 - s + tm, :] for s in range(1, CONV_K)]


def _conv_scratch(tm):
    return pltpu.VMEM((G_CONV // 128, tm + 8, 128), f32)


def _conv(taps, w):
    cv = taps[0] * w[3:4]
    for s in range(1, CONV_K):
        cv = cv + taps[s] * w[3 - s:4 - s]
    return cv


def _l2n(x):
    return x * lax.rsqrt(jnp.sum(x * x, axis=-1, keepdims=True) + EPS)


def _gdn_in_specs(tm, S):
    nb8 = tm // 8
    return [pl.BlockSpec((tm, G_CONV), lambda i: (i, 0)),
            pl.BlockSpec((8, G_CONV), lambda i: (jnp.maximum(i * nb8 - 1, 0), 0)),
            pl.BlockSpec((tm, 128), lambda i: (i, G_BA0 // 128))]


def gdn_pre(proj, conv_w, alog, dtb):
    S = proj.shape[0]
    tm = 256
    nch = tm // CHUNK

    def body(p_ref, halo_ref, ba_ref, w_ref, al_ref, dt_ref,
             q_ref, k_ref, kb_ref, kbg_ref, vb_ref, qd_ref, kd_ref, d_ref, gl_ref, ext_scr):
        first = pl.program_id(0) == 0
        ltri, _, bsame = _chunk_mats(tm)
        beta, _, _, _, gc, glast = _gdn_scalars(ba_ref[...], al_ref[...], dt_ref[...], ltri, bsame)
        eg, ek, egl = jnp.exp(gc), jnp.exp(glast - gc), jnp.exp(glast)
        eye = jnp.where(_iota((16, 16), 0) == _iota((16, 16), 1), 1.0, 0.0).astype(f32)
        gct = _nt(eye, gc, HI)
        low = _iota((CHUNK, CHUNK), 0) >= _iota((CHUNK, CHUNK), 1)

        def act(gi):
            return _silu(_conv(_conv_taps(p_ref, halo_ref, first, gi, ext_scr), w_ref[:, gi * 128:(gi + 1) * 128]))

        for j in range(GQK_H):
            js = slice(j * 128, (j + 1) * 128)
            qn = _l2n(act(j)) * (GHD ** -0.5)
            kn = _l2n(act(GQK_H + j))
            q_ref[:, js] = qn.astype(bf16)
            k_ref[:, js] = kn.astype(bf16)
            for e in range(2):
                h = 2 * j + e
                hs = slice(h * 128, (h + 1) * 128)
                v = act(2 * GQK_H + h)
                bh, egh, ekh = beta[:, h:h + 1], eg[:, h:h + 1], ek[:, h:h + 1]
                kbv = kn * bh
                kb_ref[:, hs] = kbv.astype(bf16)
                kbg_ref[:, hs] = (kbv * egh).astype(bf16)
                vb_ref[:, hs] = (v * bh).astype(bf16)
                qd_ref[:, hs] = (qn * egh).astype(bf16)
                kd_ref[:, hs] = (kn * ekh).astype(bf16)
                for c in range(nch):
                    rs = slice(c * CHUNK, (c + 1) * CHUNK)
                    diff = gc[rs, h:h + 1] - gct[h:h + 1, rs]
                    d_ref[rs, h * CHUNK:(h + 1) * CHUNK] = jnp.where(low, jnp.exp(jnp.where(low, diff, 0.0)), 0.0)
                    gl_ref[c * 8:(c + 1) * 8, hs] = jnp.broadcast_to(egl[c * CHUNK:c * CHUNK + 8, h:h + 1], (8, 128))

    full = lambda shape: pl.BlockSpec(shape, lambda i: (0, 0))
    t1 = pl.BlockSpec((tm, 1024), lambda i: (i, 0))
    t2 = pl.BlockSpec((tm, 2048), lambda i: (i, 0))
    sd = jax.ShapeDtypeStruct
    return pl.pallas_call(
        body, name="gdn_pre", grid=(S // tm,),
        in_specs=_gdn_in_specs(tm, S) + [full((CONV_K, G_CONV)), full((1, 16)), full((1, 16))],
        out_specs=[t1, t1, t2, t2, t2, t2, t2, t1, pl.BlockSpec((tm // 8, 2048), lambda i: (i, 0))],
        out_shape=[sd((S, 1024), bf16)] * 2 + [sd((S, 2048), bf16)] * 5 + [sd((S, 1024), f32), sd((S // 8, 2048), f32)],
        scratch_shapes=[_conv_scratch(tm)],
        compiler_params=_params(("parallel",)),
    )(proj, proj, proj, conv_w, alog, dtb)


def _bnn(a, b):
    return lax.dot_general(a, b, (((2,), (1,)), ((0,), (0,))), preferred_element_type=f32)


def _bnt(a, b):
    return lax.dot_general(a, b, (((2,), (2,)), ((0,), (0,))), preferred_element_type=f32)


def _btn(a, b):
    return lax.dot_general(a, b, (((1,), (1,)), ((0,), (0,))), preferred_element_type=f32)


def _split(a):
    hi = a.astype(bf16)
    return hi, (a - hi.astype(f32)).astype(bf16)


def _cat3(h, l, axis, lhs):
    return jnp.concatenate([h, h, l] if lhs else [h, l, h], axis=axis)


def _tri_inv_b(L):
    eye = jnp.where(_iota((1, CHUNK, CHUNK), 1) == _iota((1, CHUNK, CHUNK), 2), 1.0, 0.0).astype(f32)
    P = -L
    T = eye + P
    ph, pl_ = _split(P)
    for _ in range(5):
        P = _bnn(_cat3(ph, pl_, 2, True), _cat3(ph, pl_, 1, False))
        ph, pl_ = _split(P)
        th, tl = _split(T)
        T = T + _bnn(_cat3(th, tl, 2, True), _cat3(ph, pl_, 1, False))
    return T


GTB = 512
GQH_FWD, GQH_BWD = 1, 2


def _gdn_slices(ncb, gnv):
    pairs = [(c, e) for c in range(ncb) for e in range(gnv)]
    rs = lambda c: slice(c * CHUNK, (c + 1) * CHUNK)
    cs = lambda e: slice(e * 128, (e + 1) * 128)
    ds_ = lambda e: slice(e * CHUNK, (e + 1) * CHUNK)
    ks = lambda e: slice((e // 2) * 128, (e // 2 + 1) * 128)
    return pairs, rs, cs, ds_, ks


def gdn_fwd(q, k, kb, kbg, vb, qd, kd, dm, gl8, comm=None):
    S = q.shape[0]
    nb, ncb = S // GTB, GTB // CHUNK
    GQH, GNV = GQH_FWD, 2 * GQH_FWD
    pairs, rs, cs, ds_, ks = _gdn_slices(ncb, GNV)

    def body(q_ref, k_ref, kb_ref, kbg_ref, vb_ref, qd_ref, kd_ref, d_ref, gl_ref,
             o_ref, w_ref, at_ref, t_ref, vn_ref, st_ref, state, u_scr):
        @pl.when(pl.program_id(1) == 0)
        def _():
            state[...] = jnp.zeros_like(state)

        stk = lambda ref, lanes: jnp.stack([ref[rs(c), lanes(e)] for c, e in pairs])
        kq = stk(k_ref, ks)
        dmat = stk(d_ref, ds_)
        strict = _iota((1, CHUNK, CHUNK), 1) > _iota((1, CHUNK, CHUNK), 2)
        T = _tri_inv_b(jnp.where(strict, _bnt(stk(kb_ref, cs), kq) * dmat, 0.0))
        tb = T.astype(bf16)
        u_scr[...] = _bnn(tb, stk(vb_ref, cs))
        wb = _bnn(tb, stk(kbg_ref, cs)).astype(bf16)
        per_qk = lambda ref: jnp.stack([ref[rs(c), ks(e)] for c, e in pairs if e % 2 == 0])
        qk = _bnt(per_qk(q_ref), per_qk(k_ref))
        for b, (c, e) in enumerate(pairs):
            w_ref[rs(c), cs(e)] = wb[b]
            at_ref[rs(c), ds_(e)] = (qk[b // 2] * dmat[b]).astype(bf16)
            t_ref[rs(c), ds_(e)] = T[b]
        for b, (c, e) in enumerate(pairs):
            sb = state[e].astype(bf16)
            vnb = (u_scr[b] - _nn(w_ref[rs(c), cs(e)], sb)).astype(bf16)
            o_ref[rs(c), cs(e)] = _nn(qd_ref[rs(c), cs(e)], sb) + _nn(at_ref[rs(c), ds_(e)], vnb)
            st_ref[c * 128:(c + 1) * 128, cs(e)] = sb
            state[e] = state[e] * gl_ref[c * 8:c * 8 + 1, cs(e)] + _tn(kd_ref[rs(c), cs(e)], vnb)
            vn_ref[rs(c), cs(e)] = vnb

    b1 = pl.BlockSpec((GTB, 128 * GQH), lambda j, i: (i, j))
    b2 = pl.BlockSpec((GTB, 256 * GQH), lambda j, i: (i, j))
    sd = jax.ShapeDtypeStruct
    return _call(
        body, name="gdn_fwd", grid=(GQK_H // GQH, nb),
        in_specs=[b1, b1, b2, b2, b2, b2, b2, b1, pl.BlockSpec((GTB // 8, 256 * GQH), lambda j, i: (i, j))],
        out_specs=[b2, b2, b1, b1, b2, pl.BlockSpec((ncb * 128, 256 * GQH), lambda j, i: (i, j))],
        out_shape=[sd((S, 2048), f32), sd((S, 2048), bf16), sd((S, 1024), bf16), sd((S, 1024), f32),
                   sd((S, 2048), bf16), sd((S // CHUNK * 128, 2048), bf16)],
        scratch_shapes=[pltpu.VMEM((GNV, 128, 128), f32), pltpu.VMEM((GNV * ncb, CHUNK, 128), f32)],
        sem=("parallel", "arbitrary"), args=(q, k, kb, kbg, vb, qd, kd, dm, gl8), comm=comm)


def gdn_bwd(do, q, k, kb, kbg, vb, qd, kd, dm, gl8, w, at, T, vn, st, comm=None):
    S = q.shape[0]
    nb, ncb = S // GTB, GTB // CHUNK
    GQH, GNV = GQH_BWD, 2 * GQH_BWD
    pairs, rs, cs, ds_, ks = _gdn_slices(ncb, GNV)

    def body(do_ref, q_ref, k_ref, kb_ref, kbg_ref, vb_ref, qd_ref, kd_ref, d_ref, gl_ref, w_ref, at_ref, t_ref, vn_ref, st_ref,
             dq_ref, dk_ref, dkb_ref, dkbg_ref, dvb_ref, dqd_ref, dkd_ref, dgc_ref, dstate, dvn_scr, dw_scr, dat_scr, dgl_scr):
        @pl.when(pl.program_id(1) == 0)
        def _():
            dstate[...] = jnp.zeros_like(dstate)

        for b, (c, e) in reversed(list(enumerate(pairs))):
            dob = do_ref[rs(c), cs(e)].astype(bf16)
            sb = st_ref[c * 128:(c + 1) * 128, cs(e)]
            vnb = vn_ref[rs(c), cs(e)]
            gl = gl_ref[c * 8:c * 8 + 1, cs(e)]
            dS = dstate[e]
            dsb = dS.astype(bf16)
            dvnb = (_tn(at_ref[rs(c), ds_(e)], dob) + _nn(kd_ref[rs(c), cs(e)], dsb)).astype(bf16)
            dvn_scr[b] = dvnb
            dat_scr[b] = _nt(dob, vnb)
            dqd_ref[rs(c), cs(e)] = _nt(dob, sb)
            dkd_ref[rs(c), cs(e)] = _nt(vnb, dsb)
            dw_scr[b] = (-_nt(dvnb, sb)).astype(bf16)
            dgl = jnp.sum(jnp.sum(dS * sb.astype(f32), axis=1, keepdims=True), axis=0, keepdims=True)
            dgl_scr[b] = jnp.broadcast_to(dgl * gl, (8, 128))
            dstate[e] = gl * dS + _tn(qd_ref[rs(c), cs(e)], dob) - _tn(w_ref[rs(c), cs(e)], dvnb)

        stk = lambda ref, lanes: jnp.stack([ref[rs(c), lanes(e)] for c, e in pairs])
        kq, qq = stk(k_ref, ks), stk(q_ref, ks)
        kbb = stk(kb_ref, cs)
        Tm = stk(t_ref, ds_)
        tb = Tm.astype(bf16)
        dvn, dw = dvn_scr[...], dw_scr[...]
        dT = _bnt(dvn, stk(vb_ref, cs)) + _bnt(dw, stk(kbg_ref, cs))
        dvb, dkbg = _btn(tb, dvn), _btn(tb, dw)
        th, tl = _split(Tm)
        xh, xl = _split(_bnt(_cat3(*_split(dT), 2, True), _cat3(th, tl, 2, False)))
        dL = -_btn(_cat3(th, tl, 1, True), _cat3(xh, xl, 1, False))
        dmat = stk(d_ref, ds_)
        strict = _iota((1, CHUNK, CHUNK), 1) > _iota((1, CHUNK, CHUNK), 2)
        dA = jnp.where(strict, dL * dmat, 0.0)
        dB = dat_scr[...] * dmat
        dAb, dBb = dA.astype(bf16), dB.astype(bf16)
        dkb = _bnn(dAb, kq)
        dkc = _btn(dAb, kbb) + _btn(dBb, qq)
        dqc = _bnn(dBb, kq)
        M = dA * _bnt(kbb, kq) + dB * _bnt(qq, kq)
        mh, ml = _split(M)
        colsum = _btn(jnp.concatenate([mh, ml], axis=1), jnp.ones((GNV * ncb, 2 * CHUNK, 128), bf16))
        lastrow = _iota((1, CHUNK, 128), 1) == CHUNK - 1
        for b, (c, e) in enumerate(pairs):
            dvb_ref[rs(c), cs(e)] = dvb[b]
            dkbg_ref[rs(c), cs(e)] = dkbg[b]
            dkb_ref[rs(c), cs(e)] = dkb[b]
            dgc_ref[rs(c), cs(e)] = (jnp.sum(M[b], axis=1, keepdims=True) - colsum[b]
                                     + jnp.where(lastrow[0], dgl_scr[b][0:1, :], 0.0))
        for b, (c, e) in enumerate(pairs):
            if e % 2 == 0:
                dq_ref[rs(c), ks(e)] = dqc[b] + dqc[b + 1]
                dk_ref[rs(c), ks(e)] = dkc[b] + dkc[b + 1]

    b1 = pl.BlockSpec((GTB, 128 * GQH), lambda j, i: (nb - 1 - i, j))
    b2 = pl.BlockSpec((GTB, 256 * GQH), lambda j, i: (nb - 1 - i, j))
    sd = jax.ShapeDtypeStruct
    return _call(
        body, name="gdn_bwd", grid=(GQK_H // GQH, nb),
        in_specs=[b2, b1, b1, b2, b2, b2, b2, b2, b1, pl.BlockSpec((GTB // 8, 256 * GQH), lambda j, i: (nb - 1 - i, j)),
                  b2, b1, b1, b2, pl.BlockSpec((ncb * 128, 256 * GQH), lambda j, i: (nb - 1 - i, j))],
        out_specs=[b1, b1, b2, b2, b2, b2, b2, b2],
        out_shape=[sd((S, 1024), f32)] * 2 + [sd((S, 2048), f32)] * 6,
        scratch_shapes=[pltpu.VMEM((GNV, 128, 128), f32), pltpu.VMEM((GNV * ncb, CHUNK, 128), bf16),
                        pltpu.VMEM((GNV * ncb, CHUNK, 128), bf16), pltpu.VMEM((GNV * ncb, CHUNK, CHUNK), f32),
                        pltpu.VMEM((GNV * ncb, 8, 128), f32)],
        sem=("parallel", "arbitrary"), args=(do, q, k, kb, kbg, vb, qd, kd, dm, gl8, w, at, T, vn, st), comm=comm)


def gdn_onorm(o, proj, nw):
    S = o.shape[0]
    tm = 256

    def body(o_ref, z_ref, nw_ref, o2_ref):
        for h in range(GV_H):
            hs = slice(h * 128, (h + 1) * 128)
            oh = o_ref[:, hs]
            r = lax.rsqrt(jnp.mean(oh * oh, axis=-1, keepdims=True) + EPS)
            o2_ref[:, hs] = (((oh * r) * nw_ref[...]) * _silu(z_ref[:, hs])).astype(bf16)

    t2 = pl.BlockSpec((tm, 2048), lambda i: (i, 0))
    return pl.pallas_call(
        body, name="gdn_onorm", grid=(S // tm,),
        in_specs=[t2, pl.BlockSpec((tm, 2048), lambda i: (i, G_Z0 // 2048)), pl.BlockSpec((1, 128), lambda i: (0, 0))],
        out_specs=t2, out_shape=jax.ShapeDtypeStruct((S, 2048), bf16),
        compiler_params=_params(("parallel",)),
    )(o, proj, nw)


def gdn_onorm_bwd(dy, w_out, o, proj, nw):
    S = o.shape[0]
    tm = 512

    def body(dy_ref, w_ref, o_ref, z_ref, nw_ref, do_ref, dz_ref, st_ref):
        i = pl.program_id(0)
        d_all = _nt(dy_ref[...], w_ref[...])
        acc = jnp.zeros((1, 128), f32)
        for h in range(GV_H):
            hs = slice(h * 128, (h + 1) * 128)
            oh, z, d2 = o_ref[:, hs], z_ref[:, hs], d_all[:, hs]
            r = lax.rsqrt(jnp.mean(oh * oh, axis=-1, keepdims=True) + EPS)
            on = oh * r
            dt = d2 * _silu(z)
            dz_ref[:, hs] = (d2 * (on * nw_ref[...]) * _dsilu(z)).astype(bf16)
            don = dt * nw_ref[...]
            acc = acc + jnp.sum(dt * on, axis=0, keepdims=True)
            do_ref[:, hs] = r * (don - on * jnp.mean(don * on, axis=-1, keepdims=True))
        upd = jnp.concatenate([acc, jnp.zeros((7, 128), f32)], axis=0)

        @pl.when(i == 0)
        def _():
            st_ref[...] = upd

        @pl.when(i > 0)
        def _():
            st_ref[...] += upd

    t2 = pl.BlockSpec((tm, 2048), lambda i: (i, 0))
    sd = jax.ShapeDtypeStruct
    return pl.pallas_call(
        body, name="gdn_onorm_bwd", grid=(S // tm,),
        in_specs=[pl.BlockSpec((tm, D), lambda i: (i, 0)), pl.BlockSpec(w_out.shape, lambda i: (0, 0)), t2,
                  pl.BlockSpec((tm, 2048), lambda i: (i, G_Z0 // 2048)), pl.BlockSpec((1, 128), lambda i: (0, 0))],
        out_specs=[t2, t2, pl.BlockSpec((8, 128), lambda i: (0, 0))],
        out_shape=[sd((S, 2048), f32), sd((S, 2048), bf16), sd((8, 128), f32)],
        compiler_params=_params(("arbitrary",)),
    )(dy, w_out, o, proj, nw)


def gdn_pre_bwd(proj, conv_w, alog, dtb, dq, dk, dkb, dkbg, dvb, dqd, dkd, dgcd):
    S = proj.shape[0]
    tm = 128

    def body(p_ref, halo_ref, ba_ref, w_ref, al_ref, dt_ref, dq_ref, dk_ref, dkb_ref, dkbg_ref, dvb_ref, dqd_ref, dkd_ref, dgc_ref,
             dcv_ref, dba_ref, st_ref, ext_scr):
        i = pl.program_id(0)
        first = i == 0
        ltri, utri, bsame = _chunk_mats(tm)
        beta, u, neg_a, g, gc, glast = _gdn_scalars(ba_ref[...], al_ref[...], dt_ref[...], ltri, bsame)
        eg, ek = jnp.exp(gc), jnp.exp(glast - gc)
        lane16 = _iota((tm, 16), 1)
        dgc_all = jnp.zeros((tm, 16), f32)
        rkd_all = jnp.zeros((tm, 16), f32)
        dbeta_all = jnp.zeros((tm, 16), f32)

        def pre(gi):
            return _conv(_conv_taps(p_ref, halo_ref, first, gi, ext_scr), w_ref[:, gi * 128:(gi + 1) * 128])

        def l2n_bwd(xt, dy):
            r = lax.rsqrt(jnp.sum(xt * xt, axis=-1, keepdims=True) + EPS)
            y = xt * r
            return r * (dy - y * jnp.sum(dy * y, axis=-1, keepdims=True))

        for j in range(GQK_H):
            js = slice(j * 128, (j + 1) * 128)
            cvq, cvk = pre(j), pre(GQK_H + j)
            qt, kt = _silu(cvq), _silu(cvk)
            qn = _l2n(qt) * (GHD ** -0.5)
            kn = _l2n(kt)
            dq_tot, dk_tot = dq_ref[:, js], dk_ref[:, js]
            for e in range(2):
                h = 2 * j + e
                hs = slice(h * 128, (h + 1) * 128)
                gv = 2 * GQK_H + h
                cvv = pre(gv)
                v = _silu(cvv)
                bh, egh, ekh = beta[:, h:h + 1], eg[:, h:h + 1], ek[:, h:h + 1]
                dkbg, dkd, dqd, dvb = dkbg_ref[:, hs], dkd_ref[:, hs], dqd_ref[:, hs], dvb_ref[:, hs]
                dkb_t = dkb_ref[:, hs] + dkbg * egh
                dk_tot = dk_tot + dkb_t * bh + dkd * ekh
                dq_tot = dq_tot + dqd * egh
                dcv_ref[:, gv * 128:(gv + 1) * 128] = (dvb * bh) * _dsilu(cvv)
                dbeta = jnp.sum(dkb_t * kn, axis=-1, keepdims=True) + jnp.sum(dvb * v, axis=-1, keepdims=True)
                rkd = jnp.sum(dkd * (kn * ekh), axis=-1, keepdims=True)
                dgc = (dgc_ref[:, hs][:, 0:1] + jnp.sum(dkbg * (kn * bh * egh), axis=-1, keepdims=True)
                       + jnp.sum(dqd * (qn * egh), axis=-1, keepdims=True) - rkd)
                sel = lane16 == h
                dgc_all = dgc_all + jnp.where(sel, dgc, 0.0)
                rkd_all = rkd_all + jnp.where(sel, rkd, 0.0)
                dbeta_all = dbeta_all + jnp.where(sel, dbeta, 0.0)
            dcv_ref[:, js] = l2n_bwd(qt, dq_tot * (GHD ** -0.5)) * _dsilu(cvq)
            ks = slice((GQK_H + j) * 128, (GQK_H + j + 1) * 128)
            dcv_ref[:, ks] = l2n_bwd(kt, dk_tot) * _dsilu(cvk)

        islast = jnp.bitwise_and(_iota((tm, 16), 0), CHUNK - 1) == CHUNK - 1
        dgc_all = dgc_all + jnp.where(islast, _nn(bsame, rkd_all, HI), 0.0)
        dg = _nn(utri, dgc_all, HI)
        da = dg * neg_a * _sigmoid(u)
        db = dbeta_all * beta * (1.0 - beta)
        r16, c128 = _iota((16, 128), 0), _iota((16, 128), 1)
        pb = jnp.where(c128 == r16, 1.0, 0.0).astype(f32)
        pa = jnp.where(c128 == r16 + 16, 1.0, 0.0).astype(f32)
        dba_ref[...] = _nn(db, pb, HI) + _nn(da, pa, HI)
        upd = jnp.concatenate([jnp.sum(dg * g, axis=0, keepdims=True), jnp.sum(da, axis=0, keepdims=True),
                               jnp.zeros((6, 16), f32)], axis=0)

        @pl.when(i == 0)
        def _():
            st_ref[...] = upd

        @pl.when(i > 0)
        def _():
            st_ref[...] += upd

    full = lambda shape: pl.BlockSpec(shape, lambda i: (0, 0))
    t1 = pl.BlockSpec((tm, 1024), lambda i: (i, 0))
    t2 = pl.BlockSpec((tm, 2048), lambda i: (i, 0))
    sd = jax.ShapeDtypeStruct
    return pl.pallas_call(
        body, name="gdn_pre_bwd", grid=(S // tm,),
        in_specs=_gdn_in_specs(tm, S) + [full((CONV_K, G_CONV)), full((1, 16)), full((1, 16)), t1, t1] + [t2] * 6,
        out_specs=[pl.BlockSpec((tm, G_CONV), lambda i: (i, 0)), pl.BlockSpec((tm, 128), lambda i: (i, 0)), full((8, 16))],
        out_shape=[sd((S, G_CONV), f32), sd((S, 128), f32), sd((8, 16), f32)],
        scratch_shapes=[_conv_scratch(tm)],
        compiler_params=_params(("arbitrary",)),
    )(proj, proj, proj, conv_w, alog, dtb, dq, dk, dkb, dkbg, dvb, dqd, dkd, dgcd)


def gdn_conv_bwd(proj, conv_w, dcv, dz, dba):
    S = proj.shape[0]
    tm = 256
    nb, nb8 = S // tm, tm // 8

    def body(p_ref, halo_ref, w_ref, dcv_ref, nxt_ref, dz_ref, dba_ref, dp_ref, dw_ref, ext_scr, nxt_scr):
        i = pl.program_id(0)
        first, last = i == 0, i == nb - 1
        for gi in range(G_CONV // 128):
            cs = slice(gi * 128, (gi + 1) * 128)
            taps = _conv_taps(p_ref, halo_ref, first, gi, ext_scr)
            cur = dcv_ref[:, cs]
            nxt_scr[gi, 0:tm, :] = cur
            nxt_scr[gi, tm:, :] = jnp.where(last, 0.0, nxt_ref[:, cs])
            w = w_ref[:, cs]
            dp = cur * w[3:4]
            rows = [jnp.sum(cur * taps[3 - kk], axis=0, keepdims=True) for kk in range(CONV_K)]
            for s in range(1, CONV_K):
                dp = dp + nxt_scr[gi, s:s + tm, :] * w[3 - s:4 - s]
            dp_ref[:, cs] = dp.astype(bf16)
            upd = jnp.concatenate(rows + [jnp.zeros((4, 128), f32)], axis=0)

            @pl.when(first)
            def _():
                dw_ref[:, cs] = upd

            @pl.when(i > 0)
            def _():
                dw_ref[:, cs] += upd

        dp_ref[:, G_Z0:G_BA0] = dz_ref[...]
        dp_ref[:, G_BA0:G_INP] = dba_ref[...].astype(bf16)

    sd = jax.ShapeDtypeStruct
    return pl.pallas_call(
        body, name="gdn_conv_bwd", grid=(nb,),
        in_specs=[pl.BlockSpec((tm, G_CONV), lambda i: (i, 0)),
                  pl.BlockSpec((8, G_CONV), lambda i: (jnp.maximum(i * nb8 - 1, 0), 0)),
                  pl.BlockSpec((CONV_K, G_CONV), lambda i: (0, 0)),
                  pl.BlockSpec((tm, G_CONV), lambda i: (i, 0)),
                  pl.BlockSpec((8, G_CONV), lambda i: (jnp.minimum((i + 1) * nb8, S // 8 - 1), 0)),
                  pl.BlockSpec((tm, 2048), lambda i: (i, 0)), pl.BlockSpec((tm, 128), lambda i: (i, 0))],
        out_specs=[pl.BlockSpec((tm, G_INP), lambda i: (i, 0)), pl.BlockSpec((8, G_CONV), lambda i: (0, 0))],
        out_shape=[sd((S, G_INP), bf16), sd((8, G_CONV), f32)],
        scratch_shapes=[_conv_scratch(tm), _conv_scratch(tm)],
        compiler_params=_params(("arbitrary",)),
    )(proj, proj, conv_w, dcv, dcv, dz, dba)


def _half_mean(t, lo_half):
    m0 = jnp.sum(jnp.where(lo_half, t, 0.0), axis=-1, keepdims=True)
    m1 = jnp.sum(jnp.where(lo_half, 0.0, t), axis=-1, keepdims=True)
    return jnp.where(lo_half, m0, m1) * (1.0 / F_HD)


def _split3(c):
    hi = c.astype(bf16).astype(f32)
    mid = (c - hi).astype(bf16).astype(f32)
    lo = (c - hi - mid).astype(bf16).astype(f32)
    return hi, mid, lo


def fox_pre(proj, fbias, qw2, kw2):
    S = proj.shape[0]
    tm = 512

    def body(q_ref, k_ref, v_ref, f_ref, fb_ref, qw_ref, kw_ref, qa_ref, ka_ref, vb_ref, carry):
        @pl.when(pl.program_id(0) == 0)
        def _():
            carry[...] = jnp.zeros_like(carry)

        logf = -_softplus(-(f_ref[:, 0:16] + fb_ref[...]))
        ltri = jnp.where(_iota((tm, tm), 1) <= _iota((tm, tm), 0), 1.0, 0.0).astype(f32)
        cum = _nn(ltri, logf, HI) + carry[0:1, :]
        carry[0:1, :] = cum[tm - 1:tm, :]
        lane = _iota((tm, 128), 1)
        lo_half = lane < F_HD
        for p in range(F_H // 2):
            ps = slice(p * 128, (p + 1) * 128)
            for src, w_ref, dst, is_q in ((q_ref, qw_ref, qa_ref, True), (k_ref, kw_ref, ka_ref, False)):
                x = src[:, ps]
                xn = x * lax.rsqrt(_half_mean(x * x, lo_half) + EPS) * w_ref[...]
                if is_q:
                    xn = xn * (F_HD ** -0.5)
                for e in range(2):
                    h = 2 * p + e
                    base = xn if e == 0 else pltpu.roll(xn, F_HD, 1)
                    hi, mid, lo = _split3(cum[:, h:h + 1])
                    pieces = jnp.where(lane == 64, hi, 0.0) + jnp.where(lane == 65, mid, 0.0) + jnp.where(lane == 66, lo, 0.0)
                    if is_q:
                        ext = pieces + jnp.where((lane >= 67) & (lane <= 69), 1.0, 0.0)
                    else:
                        ext = jnp.where((lane >= 64) & (lane <= 66), 1.0, 0.0) - pltpu.roll(pieces, 3, 1)
                    dst[:, h * 128:(h + 1) * 128] = jnp.where(lo_half, base, ext).astype(bf16)
        one = jnp.where(lane == F_HD, 1.0, 0.0)
        for p in range(F_H // 2):
            vv = v_ref[:, p * 128:(p + 1) * 128]
            vb_ref[:, (2 * p) * 128:(2 * p + 1) * 128] = jnp.where(lo_half, vv, one).astype(bf16)
            vb_ref[:, (2 * p + 1) * 128:(2 * p + 2) * 128] = jnp.where(lo_half, pltpu.roll(vv, F_HD, 1), one).astype(bf16)

    t1 = lambda c: pl.BlockSpec((tm, 1024), lambda i: (i, c))
    vec = lambda n: pl.BlockSpec((1, n), lambda i: (0, 0))
    sd = jax.ShapeDtypeStruct
    return pl.pallas_call(
        body, name="fox_pre", grid=(S // tm,),
        in_specs=[t1(0), t1(1), t1(2), pl.BlockSpec((tm, 128), lambda i: (i, F_F0 // 128)), vec(16), vec(128), vec(128)],
        out_specs=[pl.BlockSpec((tm, 2048), lambda i: (i, 0))] * 3,
        out_shape=[sd((S, 2048), bf16)] * 3,
        scratch_shapes=[pltpu.VMEM((8, 16), f32)],
        compiler_params=_params(("arbitrary",)),
    )(proj, proj, proj, proj, fbias, qw2, kw2)


FTQ = 512
FHS_FWD, FHS_BWD = 16, 8


def fox_attn(qa, ka, v, comm=None):
    S = qa.shape[0]
    nq = S // FTQ
    FHS = FHS_FWD

    live = [(i, j) for i in range(nq) for j in range(i + 1)]
    qi_tab = jnp.asarray([i for i, _ in live], jnp.int32)
    kj_tab = jnp.asarray([j for _, j in live], jnp.int32)

    def body(qi_ref, kj_ref, q_ref, k_ref, v_ref, o_ref, lse_ref, m_scr, acc_scr):
        t = pl.program_id(1)
        i, j = qi_ref[t], kj_ref[t]

        @pl.when(j == 0)
        def _():
            m_scr[...] = jnp.full_like(m_scr, NEG)
            acc_scr[...] = jnp.zeros_like(acc_scr)

        def step(diagonal):
            for e in range(FHS):
                es = slice(e * 128, (e + 1) * 128)
                s = _nt(q_ref[:, es], k_ref[:, es])
                if diagonal:
                    s = jnp.where(_iota((FTQ, FTQ), 0) >= _iota((FTQ, FTQ), 1), s, NEG)
                m_old = m_scr[e]
                m_new = jnp.maximum(m_old, jnp.max(s, axis=-1, keepdims=True))
                p = jnp.exp(s - m_new[:, 0:1])
                acc_scr[e] = acc_scr[e] * jnp.exp(m_old - m_new) + _nn(p.astype(bf16), v_ref[:, es])
                m_scr[e] = m_new

        pl.when(j < i)(functools.partial(step, False))

        @pl.when(j == i)
        def _():
            step(True)
            for e in range(FHS):
                vs = slice(e * F_HD, (e + 1) * F_HD)
                acc = acc_scr[e]
                l = acc[:, F_HD:F_HD + 1]
                o_ref[:, vs] = acc[:, 0:F_HD] / l
                lse_ref[:, vs] = m_scr[e][:, 0:F_HD] + jnp.log(l)

    sd = jax.ShapeDtypeStruct
    qo = pl.BlockSpec((FTQ, F_HD * FHS), lambda p, t, qi, kj: (qi[t], p))
    kv = pl.BlockSpec((FTQ, 128 * FHS), lambda p, t, qi, kj: (kj[t], p))
    return _call(
        body, name="fox_attn", grid=(F_H // FHS, len(live)),
        in_specs=[pl.BlockSpec((FTQ, 128 * FHS), lambda p, t, qi, kj: (qi[t], p)), kv, kv],
        out_specs=[qo, qo],
        out_shape=[sd((S, 1024), f32), sd((S, 1024), f32)],
        scratch_shapes=[pltpu.VMEM((FHS, FTQ, 128), f32), pltpu.VMEM((FHS, FTQ, 128), f32)],
        sem=("parallel", "arbitrary"), args=(qa, ka, v), comm=comm, prefetch=(qi_tab, kj_tab))


def fox_attn_bwd(qa, ka, v, do, lse, delta, comm=None):
    S = qa.shape[0]
    nq = S // FTQ
    FHS = FHS_BWD

    live = [(j, i) for j in range(nq) for i in range(j, nq)]
    kj_tab = jnp.asarray([j for j, _ in live], jnp.int32)
    qi_tab = jnp.asarray([i for _, i in live], jnp.int32)

    def body(kj_ref, qi_ref, q_ref, k_ref, v_ref, do_ref, lse_ref, dl_ref, dq_ref, dk_ref, dv_ref, dk_scr, dv_scr):
        t = pl.program_id(1)
        j, i = kj_ref[t], qi_ref[t]

        @pl.when(t == 0)
        def _():
            dq_ref[...] = jnp.zeros_like(dq_ref)

        @pl.when(i == j)
        def _():
            dk_scr[...] = jnp.zeros_like(dk_scr)
            dv_scr[...] = jnp.zeros_like(dv_scr)

        def step(diagonal):
            rows = pl.ds(pl.multiple_of(i * FTQ, FTQ), FTQ)
            for e in range(FHS):
                es, vs = slice(e * 128, (e + 1) * 128), slice(e * F_HD, (e + 1) * F_HD)
                qe, ke = q_ref[:, es], k_ref[:, es]
                dob = do_ref[:, vs]
                s = _nt(qe, ke)
                if diagonal:
                    s = jnp.where(_iota((FTQ, FTQ), 0) >= _iota((FTQ, FTQ), 1), s, NEG)
                p = jnp.exp(s - lse_ref[:, e * F_HD:e * F_HD + 1])
                ds = p * (_nt(dob, v_ref[:, e * 128:e * 128 + F_HD]) - dl_ref[:, e * F_HD:e * F_HD + 1])
                dsb = ds.astype(bf16)
                dv_scr[e] += _tn(dob, p.astype(bf16))
                dk_scr[e] += _tn(qe, dsb)
                dq_ref[rows, es] += _nn(dsb, ke)

        pl.when(i > j)(functools.partial(step, False))
        pl.when(i == j)(functools.partial(step, True))

        @pl.when(i == nq - 1)
        def _():
            for e in range(FHS):
                dk_ref[:, e * 128:(e + 1) * 128] = dk_scr[e].T
                dv_ref[:, e * F_HD:(e + 1) * F_HD] = dv_scr[e].T

    sd = jax.ShapeDtypeStruct
    qi = lambda w: pl.BlockSpec((FTQ, w * FHS), lambda p, t, kj_, qi_: (qi_[t], p))
    kj = lambda w: pl.BlockSpec((FTQ, w * FHS), lambda p, t, kj_, qi_: (kj_[t], p))
    return _call(
        body, name="fox_attn_bwd", grid=(F_H // FHS, len(live)),
        in_specs=[qi(128), kj(128), kj(128), qi(F_HD), qi(F_HD), qi(F_HD)],
        out_specs=[pl.BlockSpec((S, 128 * FHS), lambda p, t, kj_, qi_: (0, p)), kj(128), kj(F_HD)],
        out_shape=[sd((S, 2048), f32), sd((S, 2048), f32), sd((S, 1024), f32)],
        scratch_shapes=[pltpu.VMEM((FHS, 128, FTQ), f32), pltpu.VMEM((FHS, F_HD, FTQ), f32)],
        sem=("parallel", "arbitrary"), args=(qa, ka, v, do, lse, delta), comm=comm, prefetch=(kj_tab, qi_tab))


def fox_gate(o, proj):
    S = o.shape[0]
    tm = 512

    def body(o_ref, z_ref, o2_ref):
        o2_ref[...] = (o_ref[...] * _silu(z_ref[...])).astype(bf16)

    t = pl.BlockSpec((tm, 1024), lambda i: (i, 0))
    return pl.pallas_call(
        body, name="fox_gate", grid=(S // tm,),
        in_specs=[t, pl.BlockSpec((tm, 1024), lambda i: (i, 3))], out_specs=t,
        out_shape=jax.ShapeDtypeStruct((S, 1024), bf16),
        compiler_params=_params(("parallel",)),
    )(o, proj)


def fox_gate_bwd(dy, w_out, o, proj):
    S = o.shape[0]
    tm = 512

    def body(dy_ref, w_ref, o_ref, z_ref, do_ref, dz_ref, dl_ref):
        d_all = _nt(dy_ref[...], w_ref[...])
        lo_half = _iota((tm, 128), 1) < F_HD
        for p in range(F_H // 2):
            ps = slice(p * 128, (p + 1) * 128)
            d2, ov, z = d_all[:, ps], o_ref[:, ps], z_ref[:, ps]
            dov = d2 * _silu(z)
            do_ref[:, ps] = dov.astype(bf16)
            dz_ref[:, ps] = (d2 * ov * _dsilu(z)).astype(bf16)
            dl_ref[:, ps] = _half_mean(dov * ov, lo_half) * float(F_HD)

    t = pl.BlockSpec((tm, 1024), lambda i: (i, 0))
    sd = jax.ShapeDtypeStruct
    return pl.pallas_call(
        body, name="fox_gate_bwd", grid=(S // tm,),
        in_specs=[t, pl.BlockSpec(w_out.shape, lambda i: (0, 0)), t, pl.BlockSpec((tm, 1024), lambda i: (i, 3))],
        out_specs=[t, t, t],
        out_shape=[sd((S, 1024), bf16), sd((S, 1024), bf16), sd((S, 1024), f32)],
        compiler_params=_params(("parallel",)),
    )(dy, w_out, o, proj)


def fox_pre_bwd(proj, fbias, qw2, kw2, dqa, dka, dv, dz):
    S = proj.shape[0]
    tm = 512
    nb = S // tm

    def body(q_ref, k_ref, f_ref, fb_ref, qw_ref, kw_ref, dqa_ref, dka_ref, dv_ref, dz_ref, dp_ref, st_ref, carry):
        i = pl.program_id(0)

        @pl.when(i == 0)
        def _():
            carry[...] = jnp.zeros_like(carry)

        lane = _iota((tm, 128), 1)
        lo_half = lane < F_HD
        lane16 = _iota((tm, 16), 1)
        dcum = jnp.zeros((tm, 16), f32)
        dws = []
        for src, w_ref, dsrc, is_q, col0 in ((q_ref, qw_ref, dqa_ref, True, 0), (k_ref, kw_ref, dka_ref, False, 1024)):
            dw = jnp.zeros((1, 128), f32)
            for p in range(F_H // 2):
                ps = slice(p * 128, (p + 1) * 128)
                x = src[:, ps]
                r = lax.rsqrt(_half_mean(x * x, lo_half) + EPS)
                xh = x * r
                d0 = dsrc[:, (2 * p) * 128:(2 * p + 1) * 128]
                d1 = dsrc[:, (2 * p + 1) * 128:(2 * p + 2) * 128]
                dy = jnp.where(lo_half, d0, pltpu.roll(d1, F_HD, 1))
                if is_q:
                    dy = dy * (F_HD ** -0.5)
                dxh = dy * w_ref[...]
                dw = dw + jnp.sum(dy * xh, axis=0, keepdims=True)
                dp_ref[:, col0 + p * 128:col0 + (p + 1) * 128] = (r * (dxh - xh * _half_mean(dxh * xh, lo_half))).astype(bf16)
                for e, de in ((0, d0), (1, d1)):
                    col = de[:, 64:65] if is_q else -de[:, 67:68]
                    dcum = dcum + jnp.where(lane16 == 2 * p + e, col, 0.0)
            dws.append(dw)
        dp_ref[:, 2048:3072] = dv_ref[...].astype(bf16)
        dp_ref[:, 3072:4096] = dz_ref[...]
        utri = jnp.where(_iota((tm, tm), 1) >= _iota((tm, tm), 0), 1.0, 0.0).astype(f32)
        dlogf = _nn(utri, dcum, HI) + carry[0:1, :]
        carry[0:1, :] = dlogf[0:1, :]
        fl = f_ref[:, 0:16] + fb_ref[...]
        df = dlogf * _sigmoid(-fl)
        place = jnp.where(_iota((16, 128), 1) == _iota((16, 128), 0), 1.0, 0.0).astype(f32)
        dfw = _nn(df, place, HI)
        dp_ref[:, F_F0:F_INP] = dfw.astype(bf16)
        upd = jnp.concatenate(dws + [jnp.sum(dfw, axis=0, keepdims=True), jnp.zeros((5, 128), f32)], axis=0)

        @pl.when(i == 0)
        def _():
            st_ref[...] = upd

        @pl.when(i > 0)
        def _():
            st_ref[...] += upd

    rev = lambda w, c: pl.BlockSpec((tm, w), lambda i: (nb - 1 - i, c))
    vec = lambda n: pl.BlockSpec((1, n), lambda i: (0, 0))
    sd = jax.ShapeDtypeStruct
    return pl.pallas_call(
        body, name="fox_pre_bwd", grid=(nb,),
        in_specs=[rev(1024, 0), rev(1024, 1), rev(128, F_F0 // 128), vec(16), vec(128), vec(128),
                  rev(2048, 0), rev(2048, 0), rev(1024, 0), rev(1024, 0)],
        out_specs=[rev(F_INP, 0), pl.BlockSpec((8, 128), lambda i: (0, 0))],
        out_shape=[sd((S, F_INP), bf16), sd((8, 128), f32)],
        scratch_shapes=[pltpu.VMEM((8, 16), f32)],
        compiler_params=_params(("arbitrary",)),
    )(proj, proj, proj, fbias, qw2, kw2, dqa, dka, dv, dz)


def _me():
    return lax.axis_index("x"), lax.axis_index("y"), lax.axis_index("c")


def _other_chips(x, y):
    return [(1 - x, y), (x, 1 - y), (1 - x, 1 - y)]


def ag_small(xs):
    m_per, n = xs.shape

    def body(x_ref, out_ref, send_sems, recv_sems, local_sem):
        x, y, c = _me()
        me, sibling = (x, y, c), (x, y, 1 - c)
        chips = _other_chips(x, y)

        def rows(px, py, pc):
            return out_ref.at[pl.ds((4 * px + 2 * py + pc) * m_per, m_per), :]

        def copy(k, block, to, src=None):
            return pltpu.make_async_remote_copy(
                src_ref=rows(*block) if src is None else src, dst_ref=rows(*block),
                send_sem=send_sems.at[k], recv_sem=recv_sems.at[k], device_id=to, device_id_type=MESH)

        mine = pltpu.make_async_copy(x_ref, rows(*me), local_sem)
        mine.start()
        first = [copy(0, me, sibling, src=x_ref)]
        first += [copy(1 + j, me, (*chip, c), src=x_ref) for j, chip in enumerate(chips)]
        for cp in first:
            cp.start()
        passed = [copy(4 + j, (*chip, c), sibling) for j, chip in enumerate(chips)]
        for j, chip in enumerate(chips):
            copy(1 + j, (*chip, c), me).wait_recv()
            passed[j].start()
        copy(0, sibling, me).wait_recv()
        for j, chip in enumerate(chips):
            copy(4 + j, (*chip, 1 - c), me).wait_recv()
        for cp in first + passed:
            cp.wait_send()
        mine.wait()

    return pl.pallas_call(
        body, name="ag_small",
        out_shape=jax.ShapeDtypeStruct((8 * m_per, n), xs.dtype),
        in_specs=[pl.BlockSpec(memory_space=pltpu.VMEM)], out_specs=pl.BlockSpec(memory_space=pltpu.VMEM),
        scratch_shapes=[pltpu.SemaphoreType.DMA((7,)), pltpu.SemaphoreType.DMA((7,)), pltpu.SemaphoreType.DMA],
        compiler_params=pltpu.CompilerParams(vmem_limit_bytes=VMEM_LIMIT),
    )(xs)


_ANY = pl.BlockSpec(memory_space=pl.ANY)


def ag_chips(arrs):
    n = len(arrs)
    assert all(a.shape[0] == 2 for a in arrs)

    def body(*refs):
        ins, outs = refs[:n], refs[n:2 * n]
        send_sems, recv_sems, fwd_send, fwd_recv, local_sems = refs[2 * n:]
        x, y, c = _me()
        me = 2 * x + y
        chips = _other_chips(x, y)
        started = []
        for a in range(n):
            cp = pltpu.make_async_copy(ins[a], outs[a].at[me], local_sems.at[a])
            cp.start()
            started.append(cp)
        sends = []
        for a in range(n):
            for j, (px, py) in enumerate(chips):
                r = pltpu.make_async_remote_copy(
                    src_ref=ins[a].at[c], dst_ref=outs[a].at[me, c], send_sem=send_sems.at[3 * a + j],
                    recv_sem=recv_sems.at[3 * a + j], device_id=(px, py, c), device_id_type=MESH)
                r.start()
                sends.append(r)
        for a in range(n):
            for j, (px, py) in enumerate(chips):
                got = outs[a].at[2 * px + py, c]
                pltpu.make_async_remote_copy(
                    src_ref=ins[a].at[c], dst_ref=got, send_sem=send_sems.at[3 * a + j],
                    recv_sem=recv_sems.at[3 * a + j], device_id=(px, py, c), device_id_type=MESH).wait_recv()
                f = pltpu.make_async_remote_copy(
                    src_ref=got, dst_ref=got, send_sem=fwd_send.at[3 * a + j], recv_sem=fwd_recv.at[3 * a + j],
                    device_id=(x, y, 1 - c), device_id_type=MESH)
                f.start()
                sends.append(f)
        for a in range(n):
            for j, (px, py) in enumerate(chips):
                theirs = outs[a].at[2 * px + py, 1 - c]
                pltpu.make_async_remote_copy(
                    src_ref=theirs, dst_ref=theirs, send_sem=fwd_send.at[3 * a + j], recv_sem=fwd_recv.at[3 * a + j],
                    device_id=(x, y, 1 - c), device_id_type=MESH).wait_recv()
        for r in sends:
            r.wait_send()
        for cp in started:
            cp.wait()

    sems = pltpu.SemaphoreType.DMA((3 * n,))
    return pl.pallas_call(
        body, name="ag_chips",
        out_shape=[jax.ShapeDtypeStruct((4,) + a.shape, a.dtype) for a in arrs],
        in_specs=[_ANY] * n, out_specs=[_ANY] * n,
        scratch_shapes=[sems, sems, sems, sems, pltpu.SemaphoreType.DMA((n,))],
    )(*arrs)


def _ag_comm(arrs):
    n = len(arrs)

    def copies(ins, outs, sems, inbound):
        send_sems, recv_sems, local_sems = sems
        x, y, c = _me()
        me = 2 * x + y
        local = [pltpu.make_async_copy(ins[a], outs[a].at[me], local_sems.at[a]) for a in range(n)]
        out_cp, in_cp = [], []
        for a in range(n):
            for j, (px, py) in enumerate(_other_chips(x, y)):
                mk = functools.partial(pltpu.make_async_remote_copy, src_ref=ins[a], send_sem=send_sems.at[3 * a + j],
                                       recv_sem=recv_sems.at[3 * a + j], device_id=(px, py, c), device_id_type=MESH)
                out_cp.append(mk(dst_ref=outs[a].at[me]))
                if inbound:
                    in_cp.append(mk(dst_ref=outs[a].at[2 * px + py]))
        return local, out_cp, in_cp

    def start(ins, outs, sems):
        local, out_cp, _ = copies(ins, outs, sems, False)
        for cp in local + out_cp:
            cp.start()

    def wait(ins, outs, sems):
        local, out_cp, in_cp = copies(ins, outs, sems, True)
        for cp in in_cp:
            cp.wait_recv()
        for cp in out_cp:
            cp.wait_send()
        for cp in local:
            cp.wait()

    sems = [pltpu.SemaphoreType.DMA((3 * n,)), pltpu.SemaphoreType.DMA((3 * n,)), pltpu.SemaphoreType.DMA((n,))]
    return _Comm(arrs, [jax.ShapeDtypeStruct((4,) + a.shape, a.dtype) for a in arrs], sems, start, wait)


def _rs_comm(gs):
    n = len(gs)
    flips = [(fx, fy, fc) for fx in (0, 1) for fy in (0, 1) for fc in (0, 1)][1:]

    def copies(ins, outs, sems, inbound):
        send_sems, recv_sems, local_sems = sems
        x, y, c = _me()
        me = 4 * x + 2 * y + c
        local, out_cp, in_cp = [], [], []
        for a in range(n):
            rh = ins[a].shape[1] // 2
            mine = ins[a].at[2 * x + y, pl.ds(c * rh, rh), :]
            local.append(pltpu.make_async_copy(mine, outs[a].at[me], local_sems.at[a]))
            for j, (fx, fy, fc) in enumerate(flips):
                px, py, pc = (1 - x if fx else x), (1 - y if fy else y), (1 - c if fc else c)
                mk = functools.partial(pltpu.make_async_remote_copy, send_sem=send_sems.at[7 * a + j],
                                       recv_sem=recv_sems.at[7 * a + j], device_id=(px, py, pc), device_id_type=MESH)
                out_cp.append(mk(src_ref=ins[a].at[2 * px + py, pl.ds(pc * rh, rh), :], dst_ref=outs[a].at[me]))
                if inbound:
                    in_cp.append(mk(src_ref=mine, dst_ref=outs[a].at[4 * px + 2 * py + pc]))
        return local, out_cp, in_cp

    def start(ins, outs, sems):
        local, out_cp, _ = copies(ins, outs, sems, False)
        for cp in local + out_cp:
            cp.start()

    def wait(ins, outs, sems):
        local, out_cp, in_cp = copies(ins, outs, sems, True)
        for cp in in_cp:
            cp.wait_recv()
        for cp in out_cp:
            cp.wait_send()
        for cp in local:
            cp.wait()

    sems = [pltpu.SemaphoreType.DMA((7 * n,)), pltpu.SemaphoreType.DMA((7 * n,)), pltpu.SemaphoreType.DMA((n,))]
    return _Comm(gs, [jax.ShapeDtypeStruct((8, g.shape[1] // 2, g.shape[2]), g.dtype) for g in gs], sems, start, wait)


def sum_leading(q, name):
    K, R, C = q.shape
    tr = _pick(R, (256, 128, 64, 32, 16, 8))

    def body(q_ref, o_ref):
        acc = q_ref[0]
        for k in range(1, K):
            acc = acc + q_ref[k]
        o_ref[...] = acc

    return pl.pallas_call(
        body, name=name, grid=(R // tr,),
        in_specs=[pl.BlockSpec((K, tr, C), lambda i: (0, i, 0))], out_specs=pl.BlockSpec((tr, C), lambda i: (i, 0)),
        out_shape=jax.ShapeDtypeStruct((R, C), f32),
        compiler_params=_params(("parallel",)),
    )(q)


def rs_sum_devices(q, cidx, layer, n_layers, into=None):
    K, R, C = q.shape
    tr = _pick(R, (256, 128))

    def body(c_ref, q_ref, *rest):
        acc = q_ref[0].astype(f32)
        for k in range(1, K):
            acc = acc + q_ref[k].astype(f32)
        rest[-1][0, 0] = acc

    return pl.pallas_call(
        body, name="rs_sum_devices",
        grid_spec=pltpu.PrefetchScalarGridSpec(
            num_scalar_prefetch=1, grid=(R // tr,),
            in_specs=[pl.BlockSpec((K, tr, C), lambda i, c_ref: (0, i, 0))] + ([] if into is None else [_ANY]),
            out_specs=pl.BlockSpec((1, 1, tr, C), lambda i, c_ref: (layer, c_ref[0], i, 0))),
        out_shape=jax.ShapeDtypeStruct((n_layers, 2, R, C), f32),
        input_output_aliases={} if into is None else {2: 0},
        compiler_params=_params(("parallel",)),
    )(cidx, q, *([] if into is None else [into]))


def rs_share_halves(rs):
    n = len(rs)

    def body(*refs):
        bufs = refs[n:2 * n]
        send_sems, recv_sems = refs[2 * n:]
        x, y, c = _me()
        cps = []
        for a in range(n):
            mine = bufs[a].at[pl.ds(0, bufs[a].shape[0]), c]
            cp = pltpu.make_async_remote_copy(
                src_ref=mine, dst_ref=mine, send_sem=send_sems.at[a], recv_sem=recv_sems.at[a],
                device_id=(x, y, 1 - c), device_id_type=MESH)
            cp.start()
            cps.append(cp)
        for a, cp in enumerate(cps):
            theirs = bufs[a].at[pl.ds(0, bufs[a].shape[0]), 1 - c]
            pltpu.make_async_remote_copy(
                src_ref=theirs, dst_ref=theirs, send_sem=send_sems.at[a], recv_sem=recv_sems.at[a],
                device_id=(x, y, 1 - c), device_id_type=MESH).wait_recv()
            cp.wait_send()

    return pl.pallas_call(
        body, name="rs_share_halves",
        out_shape=[jax.ShapeDtypeStruct(r.shape, r.dtype) for r in rs],
        in_specs=[_ANY] * n, out_specs=[_ANY] * n, input_output_aliases={a: a for a in range(n)},
        scratch_shapes=[pltpu.SemaphoreType.DMA((n,)), pltpu.SemaphoreType.DMA((n,))],
    )(*rs)


def ada_mod(c_all, ada_w):
    L, _, n = ada_w.shape

    def body(c_ref, w_ref, o_ref):
        o_ref[0] = _nn(_silu(c_ref[...]), w_ref[0], HI)

    return pl.pallas_call(
        body, name="ada_mod", grid=(L,),
        in_specs=[pl.BlockSpec((8, D), lambda l: (0, 0)), pl.BlockSpec((1, D, n), lambda l: (l, 0, 0))],
        out_specs=pl.BlockSpec((1, 8, n), lambda l: (l, 0, 0)),
        out_shape=jax.ShapeDtypeStruct((L, 8, n), f32),
        compiler_params=_params(("parallel",)),
    )(c_all, ada_w)


def ada_w_grad(c_all, dmod):
    L, _, n = dmod.shape

    def body(c_ref, d_ref, o_ref):
        o_ref[0] = _tn(_silu(c_ref[...]), d_ref[0], HI)

    return pl.pallas_call(
        body, name="ada_w_grad", grid=(L,),
        in_specs=[pl.BlockSpec((8, D), lambda l: (0, 0)), pl.BlockSpec((1, 8, n), lambda l: (l, 0, 0))],
        out_specs=pl.BlockSpec((1, D, n), lambda l: (l, 0, 0)),
        out_shape=jax.ShapeDtypeStruct((L, D, n), f32),
        compiler_params=_params(("parallel",)),
    )(c_all, dmod)


def adamw(w, g, m, v, name):
    shp = w.shape
    two = lambda a: a.reshape(-1, shp[-1])
    R, C = two(w).shape
    tr = _pick(R, (256, 128, 64, 32, 16, 8))
    bc1, bc2 = 1.0 - B1 ** STEP, 1.0 - B2 ** STEP

    def body(w_ref, g_ref, m_ref, v_ref, d_ref, mo_ref, vo_ref):
        gv = g_ref[...]
        mn = B1 * m_ref[...] + (1.0 - B1) * gv
        vn = B2 * v_ref[...] + (1.0 - B2) * (gv * gv)
        d_ref[...] = -LR * ((mn / bc1) / (jnp.sqrt(vn / bc2) + AEPS) + WD * w_ref[...])
        mo_ref[...] = mn
        vo_ref[...] = vn

    t = pl.BlockSpec((tr, C), lambda i: (i, 0))
    outs = pl.pallas_call(
        body, name=name, grid=(R // tr,),
        in_specs=[t] * 4, out_specs=[t] * 3, out_shape=[jax.ShapeDtypeStruct((R, C), f32)] * 3,
        compiler_params=_params(("parallel",)),
    )(two(w), two(g), two(m), two(v))
    return [o.reshape(shp) for o in outs]


def _pack(arrs):
    parts, offs, r0 = [], [], 0
    for a in arrs:
        n = a.size
        rows = -(-n // 1024) * 8
        parts.append(jnp.pad(a.reshape(-1), (0, rows * 128 - n)).reshape(rows, 128))
        offs.append((r0, rows))
        r0 += rows
    return jnp.concatenate(parts, axis=0), offs


def _unpack(buf, offs, shapes):
    out = []
    for (r0, rows), shp in zip(offs, shapes):
        n = 1
        for d in shp:
            n *= d
        out.append(buf[..., r0:r0 + rows, :].reshape(buf.shape[:-2] + (rows * 128,))[..., :n].reshape(buf.shape[:-2] + tuple(shp)))
    return out


def kernel(x, c, norm_w, ada_w, ada_b, a_w_in, a_conv_w, a_A_log, a_dt_bias, a_norm_w, a_w_out, b_w_in, b_f_bias, b_qn_w, b_kn_w, b_w_out, final_norm_w, loss_target, m_norm_w, m_ada_w, m_ada_b, m_a_w_in, m_a_conv_w, m_a_A_log, m_a_dt_bias, m_a_norm_w, m_a_w_out, m_b_w_in, m_b_f_bias, m_b_qn_w, m_b_kn_w, m_b_w_out, m_final_norm_w, v_norm_w, v_ada_w, v_ada_b, v_a_w_in, v_a_conv_w, v_a_A_log, v_a_dt_bias, v_a_norm_w, v_a_w_out, v_b_w_in, v_b_f_bias, v_b_qn_w, v_b_kn_w, v_b_w_out, v_final_norm_w):
    weights = dict(norm_w=norm_w, ada_w=ada_w, ada_b=ada_b, a_w_in=a_w_in, a_conv_w=a_conv_w, a_A_log=a_A_log,
                   a_dt_bias=a_dt_bias, a_norm_w=a_norm_w, a_w_out=a_w_out, b_w_in=b_w_in, b_f_bias=b_f_bias,
                   b_qn_w=b_qn_w, b_kn_w=b_kn_w, b_w_out=b_w_out, final_norm_w=final_norm_w)
    m_in = dict(norm_w=m_norm_w, ada_w=m_ada_w, ada_b=m_ada_b, a_w_in=m_a_w_in, a_conv_w=m_a_conv_w, a_A_log=m_a_A_log,
                a_dt_bias=m_a_dt_bias, a_norm_w=m_a_norm_w, a_w_out=m_a_w_out, b_w_in=m_b_w_in, b_f_bias=m_b_f_bias,
                b_qn_w=m_b_qn_w, b_kn_w=m_b_kn_w, b_w_out=m_b_w_out, final_norm_w=m_final_norm_w)
    v_in = dict(norm_w=v_norm_w, ada_w=v_ada_w, ada_b=v_ada_b, a_w_in=v_a_w_in, a_conv_w=v_a_conv_w, a_A_log=v_a_A_log,
                a_dt_bias=v_a_dt_bias, a_norm_w=v_a_norm_w, a_w_out=v_a_w_out, b_w_in=v_b_w_in, b_f_bias=v_b_f_bias,
                b_qn_w=v_b_qn_w, b_kn_w=v_b_kn_w, b_w_out=v_b_w_out, final_norm_w=v_final_norm_w)
    xi, yi, ci = _me()
    me_b, me_k = 4 * xi + 2 * yi + ci, 2 * xi + yi
    cidx = ci.astype(jnp.int32).reshape(1)
    S = x.shape[1]
    depth, n_a, n_b = norm_w.shape[0], a_w_in.shape[0], b_w_in.shape[0]
    x0, tgt = x.reshape(S, D), loss_target.reshape(S, D)

    c_all = ag_small(jnp.pad(c, ((0, 7), (0, 0)))).reshape(8, 8, D)[:, 0]
    nloc = ada_w.shape[2]
    parts = ag_small(ada_mod(c_all, ada_w).reshape(depth * 8, nloc)).reshape(4, 2, depth, 8, nloc)[:, 0]
    mine = lax.dynamic_index_in_dim(parts, me_b, axis=2, keepdims=False)
    mod = jnp.transpose(mine, (1, 0, 2)).reshape(depth, 4 * nloc) + ada_b
    shift, scale, gate = (mod[:, k * D:(k + 1) * D] for k in range(3))

    w_loc = [(a_w_in[i // 2] if i % 2 == 0 else b_w_in[i // 2]).astype(bf16) for i in range(depth)]
    wo_loc = [(a_w_out[i // 2] if i % 2 == 0 else b_w_out[i // 2]).astype(bf16) for i in range(depth)]
    pad_in = [(G_INP - G_IN) if i % 2 == 0 else (F_INP - F_IN) for i in range(depth)]
    halves = lambda w: w.reshape((2, w.shape[0] // 2) + w.shape[1:])

    def cols_in_place(g_in, pad):
        w = jnp.transpose(g_in, (1, 0, 2)).reshape(g_in.shape[1], -1)
        return jnp.pad(w, ((0, 0), (0, pad)))

    g_in0, g_conv = ag_chips([halves(w_loc[0]), a_conv_w])
    w_in_full = [cols_in_place(g_in0.reshape((4,) + w_loc[0].shape), pad_in[0])]
    w_out_full = []
    conv = [jnp.transpose(g_conv[:, l], (1, 0, 2)).reshape(CONV_K, -1) for l in range(n_a)]
    qw2 = [_row(jnp.tile(b_qn_w[l], 2)) for l in range(n_b)]
    kw2 = [_row(jnp.tile(b_kn_w[l], 2)) for l in range(n_b)]

    saved, xc = [], x0
    for i in range(depth):
        l = i // 2
        nxt = _ag_comm([w_loc[i + 1], wo_loc[i + 1]]) if i + 1 < depth else None
        h = ln_mod(xc, _row(norm_w[i]), _row(scale[i]), _row(shift[i]))
        name = "mm_a_in" if i % 2 == 0 else "mm_b_in"
        if i == 0:
            proj, got = matmul(h, w_in_full[0], "nn", name, comm=_ag_comm([wo_loc[0]]))
            w_out_full.append(got[0].reshape(-1, D))
        else:
            proj = matmul(h, w_in_full[i], "nn", name)
        if i % 2 == 0:
            pre = gdn_pre(proj, conv[l], _row(a_A_log[l]), _row(a_dt_bias[l]))
            res, got = gdn_fwd(*pre, comm=nxt)
            o2 = gdn_onorm(res[0], proj, _row(a_norm_w[l]))
            y, xn = out_proj(o2, w_out_full[i], xc, _row(gate[i]), "out_proj_a")
        else:
            pre = fox_pre(proj, _row(b_f_bias[l]), qw2[l], kw2[l])
            res, got = fox_attn(*pre, comm=nxt)
            o2 = fox_gate(res[0], proj)
            y, xn = out_proj(o2, w_out_full[i], xc, _row(gate[i]), "out_proj_b")
        saved.append((xc, h, proj, o2, y, pre, res))
        if nxt is not None:
            w_in_full.append(cols_in_place(got[0], pad_in[i + 1]))
            w_out_full.append(got[1].reshape(-1, D))
        xc = xn
    dx, st_f = final_loss(xc, _row(final_norm_w), tgt)

    d_norm, d_mod = [None] * depth, [None] * depth
    d_conv, d_alog, d_dtb, d_anw = [None] * n_a, [None] * n_a, [None] * n_a, [None] * n_a
    d_fb, d_qn, d_kn = [None] * n_b, [None] * n_b, [None] * n_b
    ex_in, ex_out, pend_in = [None] * depth, [None] * depth, None
    for i in reversed(range(depth)):
        l = i // 2
        xin, h, proj, o2, y, pre, res = saved[i]
        ab = "a" if i % 2 == 0 else "b"
        dy, st_g = gate_bwd(dx, y, _row(gate[i]))
        d_out = matmul(o2, dy, "tn", f"mm_{ab}_dwo", out_dtype=bf16)
        ride = _rs_comm(([] if pend_in is None else [pend_in]) + [d_out.reshape(4, d_out.shape[0] // 4, D)])
        if i % 2 == 0:
            o, wv, at, tinv, vn, st = res
            do, dz, st_o = gdn_onorm_bwd(dy, w_out_full[i], o, proj, _row(a_norm_w[l]))
            grads, got = gdn_bwd(do, *pre, wv, at, tinv, vn, st, comm=ride)
            dcv, dba, st_s = gdn_pre_bwd(proj, conv[l], _row(a_A_log[l]), _row(a_dt_bias[l]), *grads)
            dproj, dcw = gdn_conv_bwd(proj, conv[l], dcv, dz, dba)
            d_conv[l], d_alog[l], d_dtb[l], d_anw[l] = dcw[:CONV_K], st_s[0], st_s[1], st_o[0]
        else:
            o, lse = res
            do, dz, delta = fox_gate_bwd(dy, w_out_full[i], o, proj)
            (dqa, dka, dv), got = fox_attn_bwd(*pre, do, lse, delta, comm=ride)
            dproj, st_b = fox_pre_bwd(proj, _row(b_f_bias[l]), qw2[l], kw2[l], dqa, dka, dv, dz)
            d_fb[l], d_qn[l], d_kn[l] = st_b[2, :F_H], st_b[0, :F_HD] + st_b[0, F_HD:], st_b[1, :F_HD] + st_b[1, F_HD:]
        ex_out[i] = got[-1]
        if pend_in is not None:
            ex_in[i + 1] = got[0]
        d_in = matmul(h, dproj, "tn", f"mm_{ab}_dw", out_dtype=bf16)
        cl = w_loc[i].shape[1]
        pend_in = jnp.transpose(d_in[:, :4 * cl].reshape(d_in.shape[0], 4, cl), (1, 0, 2))
        if i == 0:
            dh, got = matmul(dproj, w_in_full[i], "nt", f"mm_{ab}_dh", comm=_rs_comm([pend_in]))
            ex_in[0] = got[0]
        else:
            dh = matmul(dproj, w_in_full[i], "nt", f"mm_{ab}_dh")
        dx, st_n = ln_mod_bwd(xin, _row(norm_w[i]), _row(scale[i]), dh, dx)
        d_norm[i] = st_n[0]
        d_mod[i] = jnp.concatenate([st_n[2], st_n[1], st_g[0]])

    small = [jnp.stack(d_norm), jnp.stack(d_mod), jnp.stack(d_conv), jnp.stack(d_alog), jnp.stack(d_dtb), jnp.stack(d_anw),
             jnp.stack(d_fb), jnp.stack(d_qn), jnp.stack(d_kn), st_f[0], jnp.sum(st_f[1]).reshape(1)]
    shapes = [a.shape for a in small]
    buf, offs = _pack(small)
    gathered = ag_small(buf).reshape(8, buf.shape[0], 128)
    tot = _unpack(sum_leading(gathered, "sum_devices"), offs, shapes)
    g_norm, g_adab, g_convf, g_alog, g_dtb, g_anw, g_fb, g_qn, g_kn, g_fin, loss = tot
    dmod_all = _unpack(gathered, offs[1:2], shapes[1:2])[0]
    dmod_loc = lax.dynamic_slice_in_dim(dmod_all, me_k * nloc, nloc, axis=2)
    g_adaw = ada_w_grad(c_all, jnp.transpose(dmod_loc, (1, 0, 2)))
    g_conv_loc = lax.dynamic_slice_in_dim(g_convf, me_k * a_conv_w.shape[2], a_conv_w.shape[2], axis=2)

    bufs = {}
    for i in range(depth):
        for which, q in (("in", ex_in[i]), ("out", ex_out[i])):
            key = ("a" if i % 2 == 0 else "b", which)
            bufs[key] = rs_sum_devices(q, cidx, i // 2, depth // 2, into=bufs.get(key))
    keys = list(bufs)
    done = dict(zip(keys, rs_share_halves([bufs[k] for k in keys])))
    grads = dict(norm_w=g_norm, ada_w=g_adaw, ada_b=g_adab, a_w_in=done["a", "in"].reshape(a_w_in.shape),
                 a_conv_w=g_conv_loc, a_A_log=g_alog, a_dt_bias=g_dtb, a_norm_w=g_anw,
                 a_w_out=done["a", "out"].reshape(a_w_out.shape), b_w_in=done["b", "in"].reshape(b_w_in.shape),
                 b_f_bias=g_fb, b_qn_w=g_qn, b_kn_w=g_kn, b_w_out=done["b", "out"].reshape(b_w_out.shape),
                 final_norm_w=g_fin)
    names = list(weights)
    upd = {n: adamw(weights[n], grads[n], m_in[n], v_in[n], "adamw_" + n) for n in names}
    return (loss.reshape(()), dx.reshape(x.shape), *[grads[n] for n in names], *[upd[n][0] for n in names],
            *[upd[n][1] for n in names], *[upd[n][2] for n in names])
```

```python
import functools

import jax
import jax.numpy as jnp
from jax import lax
from jax.experimental import pallas as pl
from jax.experimental.pallas import tpu as pltpu

f32, bf16 = jnp.float32, jnp.bfloat16
HI = lax.Precision.HIGHEST
MESH = pl.DeviceIdType.MESH

EPS = 1e-6
D = 1024
CHUNK = 64
GQK_H, GV_H, GHD = 8, 16, 128
G_CONV = 4096
G_Z0 = 4096
G_BA0 = 6144
G_IN, G_INP = 6176, 6272
CONV_K = 4
F_H, F_HD = 16, 64
F_F0 = 4096
F_IN, F_INP = 4112, 4224
LR, B1, B2, AEPS, WD, STEP = 0.001, 0.9, 0.999, 1e-08, 0.01, 10
NEG = -1e30
VMEM_LIMIT = 56 * 1024 * 1024


def _nn(a, b, prec=None):
    return lax.dot_general(a, b, (((1,), (0,)), ((), ())), preferred_element_type=f32, precision=prec)


def _nt(a, b, prec=None):
    return lax.dot_general(a, b, (((1,), (1,)), ((), ())), preferred_element_type=f32, precision=prec)


def _tn(a, b, prec=None):
    return lax.dot_general(a, b, (((0,), (0,)), ((), ())), preferred_element_type=f32, precision=prec)


def _iota(shape, axis):
    return lax.broadcasted_iota(jnp.int32, shape, axis)


def _sigmoid(x):
    return 0.5 * jnp.tanh(0.5 * x) + 0.5


def _softplus(x):
    return jnp.maximum(x, 0.0) + jnp.log(1.0 + jnp.exp(-jnp.abs(x)))


def _silu(x):
    return x * _sigmoid(x)


def _dsilu(x):
    s = _sigmoid(x)
    return s * (1.0 + x * (1.0 - s))


def _params(sem=None, vmem=VMEM_LIMIT):
    return pltpu.CompilerParams(dimension_semantics=sem, vmem_limit_bytes=vmem)


def _row(v):
    return v.reshape(1, -1)


class _Comm:
    def __init__(self, ins, out_shapes, sems, start, wait):
        self.ins, self.out_shapes, self.sems, self.start, self.wait = list(ins), list(out_shapes), list(sems), start, wait


def _call(body, *, name, grid, in_specs, out_specs, out_shape, scratch_shapes, sem, args, comm=None, prefetch=()):
    n_pf, n_in, n_out, n_s = len(prefetch), len(in_specs), len(out_specs), len(scratch_shapes)
    n_ci, n_co = (len(comm.ins), len(comm.out_shapes)) if comm is not None else (0, 0)

    def wrapped(*refs):
        pf, refs = refs[:n_pf], refs[n_pf:]
        core_in, c_in = refs[:n_in], refs[n_in:n_in + n_ci]
        o0 = n_in + n_ci
        core_out, c_out = refs[o0:o0 + n_out], refs[o0 + n_out:o0 + n_out + n_co]
        s0 = o0 + n_out + n_co
        core_s, c_sem = refs[s0:s0 + n_s], refs[s0 + n_s:]
        if comm is not None:
            first = functools.reduce(jnp.logical_and, [pl.program_id(d) == 0 for d in range(len(grid))])
            pl.when(first)(functools.partial(comm.start, c_in, c_out, c_sem))
        body(*pf, *core_in, *core_out, *core_s)
        if comm is not None:
            last = functools.reduce(jnp.logical_and, [pl.program_id(d) == grid[d] - 1 for d in range(len(grid))])
            pl.when(last)(functools.partial(comm.wait, c_in, c_out, c_sem))

    extra = ([], [], [], []) if comm is None else ([_ANY] * n_ci, [_ANY] * n_co, comm.out_shapes, comm.sems)
    spec = pltpu.PrefetchScalarGridSpec(
        num_scalar_prefetch=n_pf, grid=grid, in_specs=list(in_specs) + extra[0], out_specs=list(out_specs) + extra[1],
        scratch_shapes=list(scratch_shapes) + extra[3])
    outs = pl.pallas_call(
        wrapped, name=name if comm is None else name + "_x", grid_spec=spec, out_shape=list(out_shape) + extra[2],
        compiler_params=_params(sem if comm is None else ("arbitrary",) * len(grid)),
    )(*prefetch, *args, *(comm.ins if comm is not None else []))
    return outs[:n_out], outs[n_out:]


def _pick(n, pref):
    for t in pref:
        if n % t == 0:
            return t
    return n


MM_VMEM_BUDGET = 44 * 1024 * 1024


def _mm_tiles(M, N, K):
    best = None
    for tk in [K] + [t for t in (2048, 1408, 1024, 896, 512, 384, 256, 128) if K % t == 0 and t < K]:
        for tm in (2048, 1024, 512, 256, 128):
            for tn in (1408, 1024, 896, 512, 384, 256, 128):
                if M % tm or N % tn:
                    continue
                nk = K // tk
                need = 2 * 2 * (tm * tk + tk * tn) + 2 * 4 * tm * tn + (4 * tm * tn if nk > 1 else 0)
                if need <= MM_VMEM_BUDGET:
                    cand = ((nk, -tm * tn), (tm, tn, tk))
                    best = cand if best is None or cand[0] < best[0] else best
    return best[1]


def matmul(a, b, mode, name, out_dtype=f32, comm=None):
    if mode == "nn":
        (M, K), (_, N) = a.shape, b.shape
    elif mode == "nt":
        (M, K), (N, _) = a.shape, b.shape
    else:
        (K, M), (_, N) = a.shape, b.shape
    tm, tn, tk = _mm_tiles(M, N, K)
    nk = K // tk
    dot = {"nn": _nn, "nt": _nt, "tn": _tn}[mode]

    def body(a_ref, b_ref, o_ref, *acc):
        k = pl.program_id(2)
        part = dot(a_ref[...], b_ref[...])
        if nk == 1:
            o_ref[...] = part.astype(out_dtype)
        else:
            acc_ref = acc[0]

            @pl.when(k == 0)
            def _():
                acc_ref[...] = part

            @pl.when(k > 0)
            def _():
                acc_ref[...] += part

            @pl.when(k == nk - 1)
            def _():
                o_ref[...] = acc_ref[...].astype(out_dtype)

    a_spec = pl.BlockSpec((tk, tm), lambda i, j, k: (k, i)) if mode == "tn" else pl.BlockSpec((tm, tk), lambda i, j, k: (i, k))
    b_spec = pl.BlockSpec((tn, tk), lambda i, j, k: (j, k)) if mode == "nt" else pl.BlockSpec((tk, tn), lambda i, j, k: (k, j))
    outs, got = _call(
        body, name=name, grid=(M // tm, N // tn, nk),
        in_specs=[a_spec, b_spec], out_specs=[pl.BlockSpec((tm, tn), lambda i, j, k: (i, j))],
        out_shape=[jax.ShapeDtypeStruct((M, N), out_dtype)],
        scratch_shapes=[] if nk == 1 else [pltpu.VMEM((tm, tn), f32)],
        sem=("parallel", "parallel", "arbitrary"), args=(a, b), comm=comm)
    return outs[0] if comm is None else (outs[0], got)


def out_proj(o2, w, x, gate, name):
    S, K = o2.shape
    N = w.shape[1]
    tm, tn = 1024, 1024

    def body(a_ref, b_ref, x_ref, g_ref, y_ref, xn_ref):
        y = _nn(a_ref[...], b_ref[...])
        y_ref[...] = y
        xn_ref[...] = x_ref[...] + g_ref[...] * y

    return pl.pallas_call(
        body, name=name, grid=(S // tm, N // tn),
        in_specs=[pl.BlockSpec((tm, K), lambda i, j: (i, 0)), pl.BlockSpec((K, tn), lambda i, j: (0, j)),
                  pl.BlockSpec((tm, tn), lambda i, j: (i, j)), pl.BlockSpec((1, tn), lambda i, j: (0, j))],
        out_specs=[pl.BlockSpec((tm, tn), lambda i, j: (i, j))] * 2,
        out_shape=[jax.ShapeDtypeStruct((S, N), f32)] * 2,
        compiler_params=_params(("parallel", "parallel")),
    )(o2, w, x, gate)


def ln_mod(x, nw, scale, shift):
    S = x.shape[0]
    tm = 1024

    def body(x_ref, nw_ref, sc_ref, sh_ref, h_ref):
        xv = x_ref[...]
        r = lax.rsqrt(jnp.mean(xv * xv, axis=-1, keepdims=True) + EPS)
        h_ref[...] = ((xv * r) * nw_ref[...] * (1.0 + sc_ref[...]) + sh_ref[...]).astype(bf16)

    vec = pl.BlockSpec((1, D), lambda i: (0, 0))
    return pl.pallas_call(
        body, name="ln_mod", grid=(S // tm,),
        in_specs=[pl.BlockSpec((tm, D), lambda i: (i, 0)), vec, vec, vec],
        out_specs=pl.BlockSpec((tm, D), lambda i: (i, 0)),
        out_shape=jax.ShapeDtypeStruct((S, D), bf16),
        compiler_params=_params(("parallel",)),
    )(x, nw, scale, shift)


def ln_mod_bwd(x, nw, scale, dh, dxres):
    S = x.shape[0]
    tm = 1024
    nb = S // tm

    def body(x_ref, nw_ref, sc_ref, dh_ref, dr_ref, dx_ref, st_ref):
        i = pl.program_id(0)
        xv = x_ref[...]
        r = lax.rsqrt(jnp.mean(xv * xv, axis=-1, keepdims=True) + EPS)
        xn = xv * r
        dh = dh_ref[...]
        dxn = dh * (nw_ref[...] * (1.0 + sc_ref[...]))
        dx_ref[...] = dr_ref[...] + r * (dxn - xn * jnp.mean(dxn * xn, axis=-1, keepdims=True))
        p1 = jnp.sum(dh * xn, axis=0, keepdims=True)
        p2 = jnp.sum(dh, axis=0, keepdims=True)
        upd = jnp.concatenate([p1, p1, p2, jnp.zeros((5, D), f32)], axis=0)

        @pl.when(i == 0)
        def _():
            st_ref[...] = upd

        @pl.when(i > 0)
        def _():
            st_ref[...] += upd

        @pl.when(i == nb - 1)
        def _():
            st_ref[0:1, :] = st_ref[0:1, :] * (1.0 + sc_ref[...])
            st_ref[1:2, :] = st_ref[1:2, :] * nw_ref[...]

    vec = pl.BlockSpec((1, D), lambda i: (0, 0))
    tile = pl.BlockSpec((tm, D), lambda i: (i, 0))
    return pl.pallas_call(
        body, name="ln_mod_bwd", grid=(S // tm,),
        in_specs=[tile, vec, vec, tile, tile],
        out_specs=[tile, pl.BlockSpec((8, D), lambda i: (0, 0))],
        out_shape=[jax.ShapeDtypeStruct((S, D), f32), jax.ShapeDtypeStruct((8, D), f32)],
        compiler_params=_params(("arbitrary",)),
    )(x, nw, scale, dh, dxres)


def final_loss(x, fw, tgt):
    S = x.shape[0]
    tm = 1024

    def body(x_ref, w_ref, t_ref, dx_ref, st_ref):
        i = pl.program_id(0)
        xv = x_ref[...]
        r = lax.rsqrt(jnp.mean(xv * xv, axis=-1, keepdims=True) + EPS)
        xn = xv * r
        err = xn * w_ref[...] - t_ref[...]
        dy = err * (1.0 / D)
        dxn = dy * w_ref[...]
        dx_ref[...] = r * (dxn - xn * jnp.mean(dxn * xn, axis=-1, keepdims=True))
        p1 = jnp.sum(dy * xn, axis=0, keepdims=True)
        p2 = jnp.sum(err * err, axis=0, keepdims=True) * (0.5 / D)
        upd = jnp.concatenate([p1, p2, jnp.zeros((6, D), f32)], axis=0)

        @pl.when(i == 0)
        def _():
            st_ref[...] = upd

        @pl.when(i > 0)
        def _():
            st_ref[...] += upd

    tile = pl.BlockSpec((tm, D), lambda i: (i, 0))
    return pl.pallas_call(
        body, name="final_loss", grid=(S // tm,),
        in_specs=[tile, pl.BlockSpec((1, D), lambda i: (0, 0)), tile],
        out_specs=[tile, pl.BlockSpec((8, D), lambda i: (0, 0))],
        out_shape=[jax.ShapeDtypeStruct((S, D), f32), jax.ShapeDtypeStruct((8, D), f32)],
        compiler_params=_params(("arbitrary",)),
    )(x, fw, tgt)


def gate_bwd(dx, y, gate):
    S = dx.shape[0]
    tm = 1024

    def body(dx_ref, y_ref, g_ref, dy_ref, st_ref):
        i = pl.program_id(0)
        dxv = dx_ref[...]
        dy_ref[...] = (g_ref[...] * dxv).astype(bf16)
        upd = jnp.concatenate([jnp.sum(dxv * y_ref[...], axis=0, keepdims=True), jnp.zeros((7, D), f32)], axis=0)

        @pl.when(i == 0)
        def _():
            st_ref[...] = upd

        @pl.when(i > 0)
        def _():
            st_ref[...] += upd

    tile = pl.BlockSpec((tm, D), lambda i: (i, 0))
    return pl.pallas_call(
        body, name="gate_bwd", grid=(S // tm,),
        in_specs=[tile, tile, pl.BlockSpec((1, D), lambda i: (0, 0))],
        out_specs=[tile, pl.BlockSpec((8, D), lambda i: (0, 0))],
        out_shape=[jax.ShapeDtypeStruct((S, D), bf16), jax.ShapeDtypeStruct((8, D), f32)],
        compiler_params=_params(("arbitrary",)),
    )(dx, y, gate)


def _chunk_mats(tm):
    r, c = _iota((tm, tm), 0), _iota((tm, tm), 1)
    same = jnp.right_shift(r, 6) == jnp.right_shift(c, 6)
    ltri = jnp.where(same & (c <= r), 1.0, 0.0).astype(f32)
    utri = jnp.where(same & (c >= r), 1.0, 0.0).astype(f32)
    bsame = jnp.where(same, 1.0, 0.0).astype(f32)
    return ltri, utri, bsame


def _gdn_scalars(ba, alog, dtb, ltri, bsame):
    beta = _sigmoid(ba[:, 0:16])
    u = ba[:, 16:32] + dtb
    neg_a = -jnp.exp(alog)
    g = neg_a * _softplus(u)
    gc = _nn(ltri, g, HI)
    glast = _nn(bsame, g, HI)
    return beta, u, neg_a, g, gc, glast


def _conv_taps(p_ref, halo_ref, first, gi, ext_scr):
    cs = slice(gi * 128, (gi + 1) * 128)
    tm = p_ref.shape[0]
    cur = p_ref[:, cs]
    ext_scr[gi, 0:8, :] = jnp.where(first, 0.0, halo_ref[:, cs])
    ext_scr[gi, 8:, :] = cur
    return [cur] + [ext_scr[gi, 8 - s:8 - s + tm, :] for s in range(1, CONV_K)]


def _conv_scratch(tm):
    return pltpu.VMEM((G_CONV // 128, tm + 8, 128), f32)


def _conv(taps, w):
    cv = taps[0] * w[3:4]
    for s in range(1, CONV_K):
        cv = cv + taps[s] * w[3 - s:4 - s]
    return cv


def _l2n(x):
    return x * lax.rsqrt(jnp.sum(x * x, axis=-1, keepdims=True) + EPS)


def _gdn_in_specs(tm, S):
    nb8 = tm // 8
    return [pl.BlockSpec((tm, G_CONV), lambda i: (i, 0)),
            pl.BlockSpec((8, G_CONV), lambda i: (jnp.maximum(i * nb8 - 1, 0), 0)),
            pl.BlockSpec((tm, 128), lambda i: (i, G_BA0 // 128))]


def gdn_pre(proj, conv_w, alog, dtb):
    S = proj.shape[0]
    tm = 256
    nch = tm // CHUNK

    def body(p_ref, halo_ref, ba_ref, w_ref, al_ref, dt_ref,
             q_ref, k_ref, kb_ref, kbg_ref, vb_ref, qd_ref, kd_ref, d_ref, gl_ref, ext_scr):
        first = pl.program_id(0) == 0
        ltri, _, bsame = _chunk_mats(tm)
        beta, _, _, _, gc, glast = _gdn_scalars(ba_ref[...], al_ref[...], dt_ref[...], ltri, bsame)
        eg, ek, egl = jnp.exp(gc), jnp.exp(glast - gc), jnp.exp(glast)
        eye = jnp.where(_iota((16, 16), 0) == _iota((16, 16), 1), 1.0, 0.0).astype(f32)
        gct = _nt(eye, gc, HI)
        low = _iota((CHUNK, CHUNK), 0) >= _iota((CHUNK, CHUNK), 1)

        def act(gi):
            return _silu(_conv(_conv_taps(p_ref, halo_ref, first, gi, ext_scr), w_ref[:, gi * 128:(gi + 1) * 128]))

        for j in range(GQK_H):
            js = slice(j * 128, (j + 1) * 128)
            qn = _l2n(act(j)) * (GHD ** -0.5)
            kn = _l2n(act(GQK_H + j))
            q_ref[:, js] = qn.astype(bf16)
            k_ref[:, js] = kn.astype(bf16)
            for e in range(2):
                h = 2 * j + e
                hs = slice(h * 128, (h + 1) * 128)
                v = act(2 * GQK_H + h)
                bh, egh, ekh = beta[:, h:h + 1], eg[:, h:h + 1], ek[:, h:h + 1]
                kbv = kn * bh
                kb_ref[:, hs] = kbv.astype(bf16)
                kbg_ref[:, hs] = (kbv * egh).astype(bf16)
                vb_ref[:, hs] = (v * bh).astype(bf16)
                qd_ref[:, hs] = (qn * egh).astype(bf16)
                kd_ref[:, hs] = (kn * ekh).astype(bf16)
                for c in range(nch):
                    rs = slice(c * CHUNK, (c + 1) * CHUNK)
                    diff = gc[rs, h:h + 1] - gct[h:h + 1, rs]
                    d_ref[rs, h * CHUNK:(h + 1) * CHUNK] = jnp.where(low, jnp.exp(jnp.where(low, diff, 0.0)), 0.0)
                    gl_ref[c * 8:(c + 1) * 8, hs] = jnp.broadcast_to(egl[c * CHUNK:c * CHUNK + 8, h:h + 1], (8, 128))

    full = lambda shape: pl.BlockSpec(shape, lambda i: (0, 0))
    t1 = pl.BlockSpec((tm, 1024), lambda i: (i, 0))
    t2 = pl.BlockSpec((tm, 2048), lambda i: (i, 0))
    sd = jax.ShapeDtypeStruct
    return pl.pallas_call(
        body, name="gdn_pre", grid=(S // tm,),
        in_specs=_gdn_in_specs(tm, S) + [full((CONV_K, G_CONV)), full((1, 16)), full((1, 16))],
        out_specs=[t1, t1, t2, t2, t2, t2, t2, t1, pl.BlockSpec((tm // 8, 2048), lambda i: (i, 0))],
        out_shape=[sd((S, 1024), bf16)] * 2 + [sd((S, 2048), bf16)] * 5 + [sd((S, 1024), f32), sd((S // 8, 2048), f32)],
        scratch_shapes=[_conv_scratch(tm)],
        compiler_params=_params(("parallel",)),
    )(proj, proj, proj, conv_w, alog, dtb)


def _bnn(a, b):
    return lax.dot_general(a, b, (((2,), (1,)), ((0,), (0,))), preferred_element_type=f32)


def _bnt(a, b):
    return lax.dot_general(a, b, (((2,), (2,)), ((0,), (0,))), preferred_element_type=f32)


def _btn(a, b):
    return lax.dot_general(a, b, (((1,), (1,)), ((0,), (0,))), preferred_element_type=f32)


def _split(a):
    hi = a.astype(bf16)
    return hi, (a - hi.astype(f32)).astype(bf16)


def _cat3(h, l, axis, lhs):
    return jnp.concatenate([h, h, l] if lhs else [h, l, h], axis=axis)


def _tri_inv_b(L):
    eye = jnp.where(_iota((1, CHUNK, CHUNK), 1) == _iota((1, CHUNK, CHUNK), 2), 1.0, 0.0).astype(f32)
    P = -L
    T = eye + P
    ph, pl_ = _split(P)
    for _ in range(5):
        P = _bnn(_cat3(ph, pl_, 2, True), _cat3(ph, pl_, 1, False))
        ph, pl_ = _split(P)
        th, tl = _split(T)
        T = T + _bnn(_cat3(th, tl, 2, True), _cat3(ph, pl_, 1, False))
    return T


GTB = 512
GQH_FWD, GQH_BWD = 1, 2


def _gdn_slices(ncb, gnv):
    pairs = [(c, e) for c in range(ncb) for e in range(gnv)]
    rs = lambda c: slice(c * CHUNK, (c + 1) * CHUNK)
    cs = lambda e: slice(e * 128, (e + 1) * 128)
    ds_ = lambda e: slice(e * CHUNK, (e + 1) * CHUNK)
    ks = lambda e: slice((e // 2) * 128, (e // 2 + 1) * 128)
    return pairs, rs, cs, ds_, ks


def gdn_fwd(q, k, kb, kbg, vb, qd, kd, dm, gl8, comm=None):
    S = q.shape[0]
    nb, ncb = S // GTB, GTB // CHUNK
    GQH, GNV = GQH_FWD, 2 * GQH_FWD
    pairs, rs, cs, ds_, ks = _gdn_slices(ncb, GNV)

    def body(q_ref, k_ref, kb_ref, kbg_ref, vb_ref, qd_ref, kd_ref, d_ref, gl_ref,
             o_ref, w_ref, at_ref, t_ref, vn_ref, st_ref, state, u_scr):
        @pl.when(pl.program_id(1) == 0)
        def _():
            state[...] = jnp.zeros_like(state)

        stk = lambda ref, lanes: jnp.stack([ref[rs(c), lanes(e)] for c, e in pairs])
        kq = stk(k_ref, ks)
        dmat = stk(d_ref, ds_)
        strict = _iota((1, CHUNK, CHUNK), 1) > _iota((1, CHUNK, CHUNK), 2)
        T = _tri_inv_b(jnp.where(strict, _bnt(stk(kb_ref, cs), kq) * dmat, 0.0))
        tb = T.astype(bf16)
        u_scr[...] = _bnn(tb, stk(vb_ref, cs))
        wb = _bnn(tb, stk(kbg_ref, cs)).astype(bf16)
        per_qk = lambda ref: jnp.stack([ref[rs(c), ks(e)] for c, e in pairs if e % 2 == 0])
        qk = _bnt(per_qk(q_ref), per_qk(k_ref))
        for b, (c, e) in enumerate(pairs):
            w_ref[rs(c), cs(e)] = wb[b]
            at_ref[rs(c), ds_(e)] = (qk[b // 2] * dmat[b]).astype(bf16)
            t_ref[rs(c), ds_(e)] = T[b]
        for b, (c, e) in enumerate(pairs):
            sb = state[e].astype(bf16)
            vnb = (u_scr[b] - _nn(w_ref[rs(c), cs(e)], sb)).astype(bf16)
            o_ref[rs(c), cs(e)] = _nn(qd_ref[rs(c), cs(e)], sb) + _nn(at_ref[rs(c), ds_(e)], vnb)
            st_ref[c * 128:(c + 1) * 128, cs(e)] = sb
            state[e] = state[e] * gl_ref[c * 8:c * 8 + 1, cs(e)] + _tn(kd_ref[rs(c), cs(e)], vnb)
            vn_ref[rs(c), cs(e)] = vnb

    b1 = pl.BlockSpec((GTB, 128 * GQH), lambda j, i: (i, j))
    b2 = pl.BlockSpec((GTB, 256 * GQH), lambda j, i: (i, j))
    sd = jax.ShapeDtypeStruct
    return _call(
        body, name="gdn_fwd", grid=(GQK_H // GQH, nb),
        in_specs=[b1, b1, b2, b2, b2, b2, b2, b1, pl.BlockSpec((GTB // 8, 256 * GQH), lambda j, i: (i, j))],
        out_specs=[b2, b2, b1, b1, b2, pl.BlockSpec((ncb * 128, 256 * GQH), lambda j, i: (i, j))],
        out_shape=[sd((S, 2048), f32), sd((S, 2048), bf16), sd((S, 1024), bf16), sd((S, 1024), f32),
                   sd((S, 2048), bf16), sd((S // CHUNK * 128, 2048), bf16)],
        scratch_shapes=[pltpu.VMEM((GNV, 128, 128), f32), pltpu.VMEM((GNV * ncb, CHUNK, 128), f32)],
        sem=("parallel", "arbitrary"), args=(q, k, kb, kbg, vb, qd, kd, dm, gl8), comm=comm)


def gdn_bwd(do, q, k, kb, kbg, vb, qd, kd, dm, gl8, w, at, T, vn, st, comm=None):
    S = q.shape[0]
    nb, ncb = S // GTB, GTB // CHUNK
    GQH, GNV = GQH_BWD, 2 * GQH_BWD
    pairs, rs, cs, ds_, ks = _gdn_slices(ncb, GNV)

    def body(do_ref, q_ref, k_ref, kb_ref, kbg_ref, vb_ref, qd_ref, kd_ref, d_ref, gl_ref, w_ref, at_ref, t_ref, vn_ref, st_ref,
             dq_ref, dk_ref, dkb_ref, dkbg_ref, dvb_ref, dqd_ref, dkd_ref, dgc_ref, dstate, dvn_scr, dw_scr, dat_scr, dgl_scr):
        @pl.when(pl.program_id(1) == 0)
        def _():
            dstate[...] = jnp.zeros_like(dstate)

        for b, (c, e) in reversed(list(enumerate(pairs))):
            dob = do_ref[rs(c), cs(e)].astype(bf16)
            sb = st_ref[c * 128:(c + 1) * 128, cs(e)]
            vnb = vn_ref[rs(c), cs(e)]
            gl = gl_ref[c * 8:c * 8 + 1, cs(e)]
            dS = dstate[e]
            dsb = dS.astype(bf16)
            dvnb = (_tn(at_ref[rs(c), ds_(e)], dob) + _nn(kd_ref[rs(c), cs(e)], dsb)).astype(bf16)
            dvn_scr[b] = dvnb
            dat_scr[b] = _nt(dob, vnb)
            dqd_ref[rs(c), cs(e)] = _nt(dob, sb)
            dkd_ref[rs(c), cs(e)] = _nt(vnb, dsb)
            dw_scr[b] = (-_nt(dvnb, sb)).astype(bf16)
            dgl = jnp.sum(jnp.sum(dS * sb.astype(f32), axis=1, keepdims=True), axis=0, keepdims=True)
            dgl_scr[b] = jnp.broadcast_to(dgl * gl, (8, 128))
            dstate[e] = gl * dS + _tn(qd_ref[rs(c), cs(e)], dob) - _tn(w_ref[rs(c), cs(e)], dvnb)

        stk = lambda ref, lanes: jnp.stack([ref[rs(c), lanes(e)] for c, e in pairs])
        kq, qq = stk(k_ref, ks), stk(q_ref, ks)
        kbb = stk(kb_ref, cs)
        Tm = stk(t_ref, ds_)
        tb = Tm.astype(bf16)
        dvn, dw = dvn_scr[...], dw_scr[...]
        dT = _bnt(dvn, stk(vb_ref, cs)) + _bnt(dw, stk(kbg_ref, cs))
        dvb, dkbg = _btn(tb, dvn), _btn(tb, dw)
        th, tl = _split(Tm)
        xh, xl = _split(_bnt(_cat3(*_split(dT), 2, True), _cat3(th, tl, 2, False)))
        dL = -_btn(_cat3(th, tl, 1, True), _cat3(xh, xl, 1, False))
        dmat = stk(d_ref, ds_)
        strict = _iota((1, CHUNK, CHUNK), 1) > _iota((1, CHUNK, CHUNK), 2)
        dA = jnp.where(strict, dL * dmat, 0.0)
        dB = dat_scr[...] * dmat
        dAb, dBb = dA.astype(bf16), dB.astype(bf16)
        dkb = _bnn(dAb, kq)
        dkc = _btn(dAb, kbb) + _btn(dBb, qq)
        dqc = _bnn(dBb, kq)
        M = dA * _bnt(kbb, kq) + dB * _bnt(qq, kq)
        mh, ml = _split(M)
        colsum = _btn(jnp.concatenate([mh, ml], axis=1), jnp.ones((GNV * ncb, 2 * CHUNK, 128), bf16))
        lastrow = _iota((1, CHUNK, 128), 1) == CHUNK - 1
        for b, (c, e) in enumerate(pairs):
            dvb_ref[rs(c), cs(e)] = dvb[b]
            dkbg_ref[rs(c), cs(e)] = dkbg[b]
            dkb_ref[rs(c), cs(e)] = dkb[b]
            dgc_ref[rs(c), cs(e)] = (jnp.sum(M[b], axis=1, keepdims=True) - colsum[b]
                                     + jnp.where(lastrow[0], dgl_scr[b][0:1, :], 0.0))
        for b, (c, e) in enumerate(pairs):
            if e % 2 == 0:
                dq_ref[rs(c), ks(e)] = dqc[b] + dqc[b + 1]
                dk_ref[rs(c), ks(e)] = dkc[b] + dkc[b + 1]

    b1 = pl.BlockSpec((GTB, 128 * GQH), lambda j, i: (nb - 1 - i, j))
    b2 = pl.BlockSpec((GTB, 256 * GQH), lambda j, i: (nb - 1 - i, j))
    sd = jax.ShapeDtypeStruct
    return _call(
        body, name="gdn_bwd", grid=(GQK_H // GQH, nb),
        in_specs=[b2, b1, b1, b2, b2, b2, b2, b2, b1, pl.BlockSpec((GTB // 8, 256 * GQH), lambda j, i: (nb - 1 - i, j)),
                  b2, b1, b1, b2, pl.BlockSpec((ncb * 128, 256 * GQH), lambda j, i: (nb - 1 - i, j))],
        out_specs=[b1, b1, b2, b2, b2, b2, b2, b2],
        out_shape=[sd((S, 1024), f32)] * 2 + [sd((S, 2048), f32)] * 6,
        scratch_shapes=[pltpu.VMEM((GNV, 128, 128), f32), pltpu.VMEM((GNV * ncb, CHUNK, 128), bf16),
                        pltpu.VMEM((GNV * ncb, CHUNK, 128), bf16), pltpu.VMEM((GNV * ncb, CHUNK, CHUNK), f32),
                        pltpu.VMEM((GNV * ncb, 8, 128), f32)],
        sem=("parallel", "arbitrary"), args=(do, q, k, kb, kbg, vb, qd, kd, dm, gl8, w, at, T, vn, st), comm=comm)


def gdn_onorm(o, proj, nw):
    S = o.shape[0]
    tm = 256

    def body(o_ref, z_ref, nw_ref, o2_ref):
        for h in range(GV_H):
            hs = slice(h * 128, (h + 1) * 128)
            oh = o_ref[:, hs]
            r = lax.rsqrt(jnp.mean(oh * oh, axis=-1, keepdims=True) + EPS)
            o2_ref[:, hs] = (((oh * r) * nw_ref[...]) * _silu(z_ref[:, hs])).astype(bf16)

    t2 = pl.BlockSpec((tm, 2048), lambda i: (i, 0))
    return pl.pallas_call(
        body, name="gdn_onorm", grid=(S // tm,),
        in_specs=[t2, pl.BlockSpec((tm, 2048), lambda i: (i, G_Z0 // 2048)), pl.BlockSpec((1, 128), lambda i: (0, 0))],
        out_specs=t2, out_shape=jax.ShapeDtypeStruct((S, 2048), bf16),
        compiler_params=_params(("parallel",)),
    )(o, proj, nw)


def gdn_onorm_bwd(dy, w_out, o, proj, nw):
    S = o.shape[0]
    tm = 512

    def body(dy_ref, w_ref, o_ref, z_ref, nw_ref, do_ref, dz_ref, st_ref):
        i = pl.program_id(0)
        d_all = _nt(dy_ref[...], w_ref[...])
        acc = jnp.zeros((1, 128), f32)
        for h in range(GV_H):
            hs = slice(h * 128, (h + 1) * 128)
            oh, z, d2 = o_ref[:, hs], z_ref[:, hs], d_all[:, hs]
            r = lax.rsqrt(jnp.mean(oh * oh, axis=-1, keepdims=True) + EPS)
            on = oh * r
            dt = d2 * _silu(z)
            dz_ref[:, hs] = (d2 * (on * nw_ref[...]) * _dsilu(z)).astype(bf16)
            don = dt * nw_ref[...]
            acc = acc + jnp.sum(dt * on, axis=0, keepdims=True)
            do_ref[:, hs] = r * (don - on * jnp.mean(don * on, axis=-1, keepdims=True))
        upd = jnp.concatenate([acc, jnp.zeros((7, 128), f32)], axis=0)

        @pl.when(i == 0)
        def _():
            st_ref[...] = upd

        @pl.when(i > 0)
        def _():
            st_ref[...] += upd

    t2 = pl.BlockSpec((tm, 2048), lambda i: (i, 0))
    sd = jax.ShapeDtypeStruct
    return pl.pallas_call(
        body, name="gdn_onorm_bwd", grid=(S // tm,),
        in_specs=[pl.BlockSpec((tm, D), lambda i: (i, 0)), pl.BlockSpec(w_out.shape, lambda i: (0, 0)), t2,
                  pl.BlockSpec((tm, 2048), lambda i: (i, G_Z0 // 2048)), pl.BlockSpec((1, 128), lambda i: (0, 0))],
        out_specs=[t2, t2, pl.BlockSpec((8, 128), lambda i: (0, 0))],
        out_shape=[sd((S, 2048), f32), sd((S, 2048), bf16), sd((8, 128), f32)],
        compiler_params=_params(("arbitrary",)),
    )(dy, w_out, o, proj, nw)


def gdn_pre_bwd(proj, conv_w, alog, dtb, dq, dk, dkb, dkbg, dvb, dqd, dkd, dgcd):
    S = proj.shape[0]
    tm = 128

    def body(p_ref, halo_ref, ba_ref, w_ref, al_ref, dt_ref, dq_ref, dk_ref, dkb_ref, dkbg_ref, dvb_ref, dqd_ref, dkd_ref, dgc_ref,
             dcv_ref, dba_ref, st_ref, ext_scr):
        i = pl.program_id(0)
        first = i == 0
        ltri, utri, bsame = _chunk_mats(tm)
        beta, u, neg_a, g, gc, glast = _gdn_scalars(ba_ref[...], al_ref[...], dt_ref[...], ltri, bsame)
        eg, ek = jnp.exp(gc), jnp.exp(glast - gc)
        lane16 = _iota((tm, 16), 1)
        dgc_all = jnp.zeros((tm, 16), f32)
        rkd_all = jnp.zeros((tm, 16), f32)
        dbeta_all = jnp.zeros((tm, 16), f32)

        def pre(gi):
            return _conv(_conv_taps(p_ref, halo_ref, first, gi, ext_scr), w_ref[:, gi * 128:(gi + 1) * 128])

        def l2n_bwd(xt, dy):
            r = lax.rsqrt(jnp.sum(xt * xt, axis=-1, keepdims=True) + EPS)
            y = xt * r
            return r * (dy - y * jnp.sum(dy * y, axis=-1, keepdims=True))

        for j in range(GQK_H):
            js = slice(j * 128, (j + 1) * 128)
            cvq, cvk = pre(j), pre(GQK_H + j)
            qt, kt = _silu(cvq), _silu(cvk)
            qn = _l2n(qt) * (GHD ** -0.5)
            kn = _l2n(kt)
            dq_tot, dk_tot = dq_ref[:, js], dk_ref[:, js]
            for e in range(2):
                h = 2 * j + e
                hs = slice(h * 128, (h + 1) * 128)
                gv = 2 * GQK_H + h
                cvv = pre(gv)
                v = _silu(cvv)
                bh, egh, ekh = beta[:, h:h + 1], eg[:, h:h + 1], ek[:, h:h + 1]
                dkbg, dkd, dqd, dvb = dkbg_ref[:, hs], dkd_ref[:, hs], dqd_ref[:, hs], dvb_ref[:, hs]
                dkb_t = dkb_ref[:, hs] + dkbg * egh
                dk_tot = dk_tot + dkb_t * bh + dkd * ekh
                dq_tot = dq_tot + dqd * egh
                dcv_ref[:, gv * 128:(gv + 1) * 128] = (dvb * bh) * _dsilu(cvv)
                dbeta = jnp.sum(dkb_t * kn, axis=-1, keepdims=True) + jnp.sum(dvb * v, axis=-1, keepdims=True)
                rkd = jnp.sum(dkd * (kn * ekh), axis=-1, keepdims=True)
                dgc = (dgc_ref[:, hs][:, 0:1] + jnp.sum(dkbg * (kn * bh * egh), axis=-1, keepdims=True)
                       + jnp.sum(dqd * (qn * egh), axis=-1, keepdims=True) - rkd)
                sel = lane16 == h
                dgc_all = dgc_all + jnp.where(sel, dgc, 0.0)
                rkd_all = rkd_all + jnp.where(sel, rkd, 0.0)
                dbeta_all = dbeta_all + jnp.where(sel, dbeta, 0.0)
            dcv_ref[:, js] = l2n_bwd(qt, dq_tot * (GHD ** -0.5)) * _dsilu(cvq)
            ks = slice((GQK_H + j) * 128, (GQK_H + j + 1) * 128)
            dcv_ref[:, ks] = l2n_bwd(kt, dk_tot) * _dsilu(cvk)

        islast = jnp.bitwise_and(_iota((tm, 16), 0), CHUNK - 1) == CHUNK - 1
        dgc_all = dgc_all + jnp.where(islast, _nn(bsame, rkd_all, HI), 0.0)
        dg = _nn(utri, dgc_all, HI)
        da = dg * neg_a * _sigmoid(u)
        db = dbeta_all * beta * (1.0 - beta)
        r16, c128 = _iota((16, 128), 0), _iota((16, 128), 1)
        pb = jnp.where(c128 == r16, 1.0, 0.0).astype(f32)
        pa = jnp.where(c128 == r16 + 16, 1.0, 0.0).astype(f32)
        dba_ref[...] = _nn(db, pb, HI) + _nn(da, pa, HI)
        upd = jnp.concatenate([jnp.sum(dg * g, axis=0, keepdims=True), jnp.sum(da, axis=0, keepdims=True),
                               jnp.zeros((6, 16), f32)], axis=0)

        @pl.when(i == 0)
        def _():
            st_ref[...] = upd

        @pl.when(i > 0)
        def _():
            st_ref[...] += upd

    full = lambda shape: pl.BlockSpec(shape, lambda i: (0, 0))
    t1 = pl.BlockSpec((tm, 1024), lambda i: (i, 0))
    t2 = pl.BlockSpec((tm, 2048), lambda i: (i, 0))
    sd = jax.ShapeDtypeStruct
    return pl.pallas_call(
        body, name="gdn_pre_bwd", grid=(S // tm,),
        in_specs=_gdn_in_specs(tm, S) + [full((CONV_K, G_CONV)), full((1, 16)), full((1, 16)), t1, t1] + [t2] * 6,
        out_specs=[pl.BlockSpec((tm, G_CONV), lambda i: (i, 0)), pl.BlockSpec((tm, 128), lambda i: (i, 0)), full((8, 16))],
        out_shape=[sd((S, G_CONV), f32), sd((S, 128), f32), sd((8, 16), f32)],
        scratch_shapes=[_conv_scratch(tm)],
        compiler_params=_params(("arbitrary",)),
    )(proj, proj, proj, conv_w, alog, dtb, dq, dk, dkb, dkbg, dvb, dqd, dkd, dgcd)


def gdn_conv_bwd(proj, conv_w, dcv, dz, dba):
    S = proj.shape[0]
    tm = 256
    nb, nb8 = S // tm, tm // 8

    def body(p_ref, halo_ref, w_ref, dcv_ref, nxt_ref, dz_ref, dba_ref, dp_ref, dw_ref, ext_scr, nxt_scr):
        i = pl.program_id(0)
        first, last = i == 0, i == nb - 1
        for gi in range(G_CONV // 128):
            cs = slice(gi * 128, (gi + 1) * 128)
            taps = _conv_taps(p_ref, halo_ref, first, gi, ext_scr)
            cur = dcv_ref[:, cs]
            nxt_scr[gi, 0:tm, :] = cur
            nxt_scr[gi, tm:, :] = jnp.where(last, 0.0, nxt_ref[:, cs])
            w = w_ref[:, cs]
            dp = cur * w[3:4]
            rows = [jnp.sum(cur * taps[3 - kk], axis=0, keepdims=True) for kk in range(CONV_K)]
            for s in range(1, CONV_K):
                dp = dp + nxt_scr[gi, s:s + tm, :] * w[3 - s:4 - s]
            dp_ref[:, cs] = dp.astype(bf16)
            upd = jnp.concatenate(rows + [jnp.zeros((4, 128), f32)], axis=0)

            @pl.when(first)
            def _():
                dw_ref[:, cs] = upd

            @pl.when(i > 0)
            def _():
                dw_ref[:, cs] += upd

        dp_ref[:, G_Z0:G_BA0] = dz_ref[...]
        dp_ref[:, G_BA0:G_INP] = dba_ref[...].astype(bf16)

    sd = jax.ShapeDtypeStruct
    return pl.pallas_call(
        body, name="gdn_conv_bwd", grid=(nb,),
        in_specs=[pl.BlockSpec((tm, G_CONV), lambda i: (i, 0)),
                  pl.BlockSpec((8, G_CONV), lambda i: (jnp.maximum(i * nb8 - 1, 0), 0)),
                  pl.BlockSpec((CONV_K, G_CONV), lambda i: (0, 0)),
                  pl.BlockSpec((tm, G_CONV), lambda i: (i, 0)),
                  pl.BlockSpec((8, G_CONV), lambda i: (jnp.minimum((i + 1) * nb8, S // 8 - 1), 0)),
                  pl.BlockSpec((tm, 2048), lambda i: (i, 0)), pl.BlockSpec((tm, 128), lambda i: (i, 0))],
        out_specs=[pl.BlockSpec((tm, G_INP), lambda i: (i, 0)), pl.BlockSpec((8, G_CONV), lambda i: (0, 0))],
        out_shape=[sd((S, G_INP), bf16), sd((8, G_CONV), f32)],
        scratch_shapes=[_conv_scratch(tm), _conv_scratch(tm)],
        compiler_params=_params(("arbitrary",)),
    )(proj, proj, conv_w, dcv, dcv, dz, dba)


def _half_mean(t, lo_half):
    m0 = jnp.sum(jnp.where(lo_half, t, 0.0), axis=-1, keepdims=True)
    m1 = jnp.sum(jnp.where(lo_half, 0.0, t), axis=-1, keepdims=True)
    return jnp.where(lo_half, m0, m1) * (1.0 / F_HD)


def _split3(c):
    hi = c.astype(bf16).astype(f32)
    mid = (c - hi).astype(bf16).astype(f32)
    lo = (c - hi - mid).astype(bf16).astype(f32)
    return hi, mid, lo


def fox_pre(proj, fbias, qw2, kw2):
    S = proj.shape[0]
    tm = 512

    def body(q_ref, k_ref, v_ref, f_ref, fb_ref, qw_ref, kw_ref, qa_ref, ka_ref, vb_ref, carry):
        @pl.when(pl.program_id(0) == 0)
        def _():
            carry[...] = jnp.zeros_like(carry)

        logf = -_softplus(-(f_ref[:, 0:16] + fb_ref[...]))
        ltri = jnp.where(_iota((tm, tm), 1) <= _iota((tm, tm), 0), 1.0, 0.0).astype(f32)
        cum = _nn(ltri, logf, HI) + carry[0:1, :]
        carry[0:1, :] = cum[tm - 1:tm, :]
        lane = _iota((tm, 128), 1)
        lo_half = lane < F_HD
        for p in range(F_H // 2):
            ps = slice(p * 128, (p + 1) * 128)
            for src, w_ref, dst, is_q in ((q_ref, qw_ref, qa_ref, True), (k_ref, kw_ref, ka_ref, False)):
                x = src[:, ps]
                xn = x * lax.rsqrt(_half_mean(x * x, lo_half) + EPS) * w_ref[...]
                if is_q:
                    xn = xn * (F_HD ** -0.5)
                for e in range(2):
                    h = 2 * p + e
                    base = xn if e == 0 else pltpu.roll(xn, F_HD, 1)
                    hi, mid, lo = _split3(cum[:, h:h + 1])
                    pieces = jnp.where(lane == 64, hi, 0.0) + jnp.where(lane == 65, mid, 0.0) + jnp.where(lane == 66, lo, 0.0)
                    if is_q:
                        ext = pieces + jnp.where((lane >= 67) & (lane <= 69), 1.0, 0.0)
                    else:
                        ext = jnp.where((lane >= 64) & (lane <= 66), 1.0, 0.0) - pltpu.roll(pieces, 3, 1)
                    dst[:, h * 128:(h + 1) * 128] = jnp.where(lo_half, base, ext).astype(bf16)
        one = jnp.where(lane == F_HD, 1.0, 0.0)
        for p in range(F_H // 2):
            vv = v_ref[:, p * 128:(p + 1) * 128]
            vb_ref[:, (2 * p) * 128:(2 * p + 1) * 128] = jnp.where(lo_half, vv, one).astype(bf16)
            vb_ref[:, (2 * p + 1) * 128:(2 * p + 2) * 128] = jnp.where(lo_half, pltpu.roll(vv, F_HD, 1), one).astype(bf16)

    t1 = lambda c: pl.BlockSpec((tm, 1024), lambda i: (i, c))
    vec = lambda n: pl.BlockSpec((1, n), lambda i: (0, 0))
    sd = jax.ShapeDtypeStruct
    return pl.pallas_call(
        body, name="fox_pre", grid=(S // tm,),
        in_specs=[t1(0), t1(1), t1(2), pl.BlockSpec((tm, 128), lambda i: (i, F_F0 // 128)), vec(16), vec(128), vec(128)],
        out_specs=[pl.BlockSpec((tm, 2048), lambda i: (i, 0))] * 3,
        out_shape=[sd((S, 2048), bf16)] * 3,
        scratch_shapes=[pltpu.VMEM((8, 16), f32)],
        compiler_params=_params(("arbitrary",)),
    )(proj, proj, proj, proj, fbias, qw2, kw2)


FTQ = 512
FHS_FWD, FHS_BWD = 16, 8


def fox_attn(qa, ka, v, comm=None):
    S = qa.shape[0]
    nq = S // FTQ
    FHS = FHS_FWD

    live = [(i, j) for i in range(nq) for j in range(i + 1)]
    qi_tab = jnp.asarray([i for i, _ in live], jnp.int32)
    kj_tab = jnp.asarray([j for _, j in live], jnp.int32)

    def body(qi_ref, kj_ref, q_ref, k_ref, v_ref, o_ref, lse_ref, m_scr, acc_scr):
        t = pl.program_id(1)
        i, j = qi_ref[t], kj_ref[t]

        @pl.when(j == 0)
        def _():
            m_scr[...] = jnp.full_like(m_scr, NEG)
            acc_scr[...] = jnp.zeros_like(acc_scr)

        def step(diagonal):
            for e in range(FHS):
                es = slice(e * 128, (e + 1) * 128)
                s = _nt(q_ref[:, es], k_ref[:, es])
                if diagonal:
                    s = jnp.where(_iota((FTQ, FTQ), 0) >= _iota((FTQ, FTQ), 1), s, NEG)
                m_old = m_scr[e]
                m_new = jnp.maximum(m_old, jnp.max(s, axis=-1, keepdims=True))
                p = jnp.exp(s - m_new[:, 0:1])
                acc_scr[e] = acc_scr[e] * jnp.exp(m_old - m_new) + _nn(p.astype(bf16), v_ref[:, es])
                m_scr[e] = m_new

        pl.when(j < i)(functools.partial(step, False))

        @pl.when(j == i)
        def _():
            step(True)
            for e in range(FHS):
                vs = slice(e * F_HD, (e + 1) * F_HD)
                acc = acc_scr[e]
                l = acc[:, F_HD:F_HD + 1]
                o_ref[:, vs] = acc[:, 0:F_HD] / l
                lse_ref[:, vs] = m_scr[e][:, 0:F_HD] + jnp.log(l)

    sd = jax.ShapeDtypeStruct
    qo = pl.BlockSpec((FTQ, F_HD * FHS), lambda p, t, qi, kj: (qi[t], p))
    kv = pl.BlockSpec((FTQ, 128 * FHS), lambda p, t, qi, kj: (kj[t], p))
    return _call(
        body, name="fox_attn", grid=(F_H // FHS, len(live)),
        in_specs=[pl.BlockSpec((FTQ, 128 * FHS), lambda p, t, qi, kj: (qi[t], p)), kv, kv],
        out_specs=[qo, qo],
        out_shape=[sd((S, 1024), f32), sd((S, 1024), f32)],
        scratch_shapes=[pltpu.VMEM((FHS, FTQ, 128), f32), pltpu.VMEM((FHS, FTQ, 128), f32)],
        sem=("parallel", "arbitrary"), args=(qa, ka, v), comm=comm, prefetch=(qi_tab, kj_tab))


def fox_attn_bwd(qa, ka, v, do, lse, delta, comm=None):
    S = qa.shape[0]
    nq = S // FTQ
    FHS = FHS_BWD

    live = [(j, i) for j in range(nq) for i in range(j, nq)]
    kj_tab = jnp.asarray([j for j, _ in live], jnp.int32)
    qi_tab = jnp.asarray([i for _, i in live], jnp.int32)

    def body(kj_ref, qi_ref, q_ref, k_ref, v_ref, do_ref, lse_ref, dl_ref, dq_ref, dk_ref, dv_ref, dk_scr, dv_scr):
        t = pl.program_id(1)
        j, i = kj_ref[t], qi_ref[t]

        @pl.when(t == 0)
        def _():
            dq_ref[...] = jnp.zeros_like(dq_ref)

        @pl.when(i == j)
        def _():
            dk_scr[...] = jnp.zeros_like(dk_scr)
            dv_scr[...] = jnp.zeros_like(dv_scr)

        def step(diagonal):
            rows = pl.ds(pl.multiple_of(i * FTQ, FTQ), FTQ)
            for e in range(FHS):
                es, vs = slice(e * 128, (e + 1) * 128), slice(e * F_HD, (e + 1) * F_HD)
                qe, ke = q_ref[:, es], k_ref[:, es]
                dob = do_ref[:, vs]
                s = _nt(qe, ke)
                if diagonal:
                    s = jnp.where(_iota((FTQ, FTQ), 0) >= _iota((FTQ, FTQ), 1), s, NEG)
                p = jnp.exp(s - lse_ref[:, e * F_HD:e * F_HD + 1])
                ds = p * (_nt(dob, v_ref[:, e * 128:e * 128 + F_HD]) - dl_ref[:, e * F_HD:e * F_HD + 1])
                dsb = ds.astype(bf16)
                dv_scr[e] += _tn(dob, p.astype(bf16))
                dk_scr[e] += _tn(qe, dsb)
                dq_ref[rows, es] += _nn(dsb, ke)

        pl.when(i > j)(functools.partial(step, False))
        pl.when(i == j)(functools.partial(step, True))

        @pl.when(i == nq - 1)
        def _():
            for e in range(FHS):
                dk_ref[:, e * 128:(e + 1) * 128] = dk_scr[e].T
                dv_ref[:, e * F_HD:(e + 1) * F_HD] = dv_scr[e].T

    sd = jax.ShapeDtypeStruct
    qi = lambda w: pl.BlockSpec((FTQ, w * FHS), lambda p, t, kj_, qi_: (qi_[t], p))
    kj = lambda w: pl.BlockSpec((FTQ, w * FHS), lambda p, t, kj_, qi_: (kj_[t], p))
    return _call(
        body, name="fox_attn_bwd", grid=(F_H // FHS, len(live)),
        in_specs=[qi(128), kj(128), kj(128), qi(F_HD), qi(F_HD), qi(F_HD)],
        out_specs=[pl.BlockSpec((S, 128 * FHS), lambda p, t, kj_, qi_: (0, p)), kj(128), kj(F_HD)],
        out_shape=[sd((S, 2048), f32), sd((S, 2048), f32), sd((S, 1024), f32)],
        scratch_shapes=[pltpu.VMEM((FHS, 128, FTQ), f32), pltpu.VMEM((FHS, F_HD, FTQ), f32)],
        sem=("parallel", "arbitrary"), args=(qa, ka, v, do, lse, delta), comm=comm, prefetch=(kj_tab, qi_tab))


def fox_gate(o, proj):
    S = o.shape[0]
    tm = 512

    def body(o_ref, z_ref, o2_ref):
        o2_ref[...] = (o_ref[...] * _silu(z_ref[...])).astype(bf16)

    t = pl.BlockSpec((tm, 1024), lambda i: (i, 0))
    return pl.pallas_call(
        body, name="fox_gate", grid=(S // tm,),
        in_specs=[t, pl.BlockSpec((tm, 1024), lambda i: (i, 3))], out_specs=t,
        out_shape=jax.ShapeDtypeStruct((S, 1024), bf16),
        compiler_params=_params(("parallel",)),
    )(o, proj)


def fox_gate_bwd(dy, w_out, o, proj):
    S = o.shape[0]
    tm = 512

    def body(dy_ref, w_ref, o_ref, z_ref, do_ref, dz_ref, dl_ref):
        d_all = _nt(dy_ref[...], w_ref[...])
        lo_half = _iota((tm, 128), 1) < F_HD
        for p in range(F_H // 2):
            ps = slice(p * 128, (p + 1) * 128)
            d2, ov, z = d_all[:, ps], o_ref[:, ps], z_ref[:, ps]
            dov = d2 * _silu(z)
            do_ref[:, ps] = dov.astype(bf16)
            dz_ref[:, ps] = (d2 * ov * _dsilu(z)).astype(bf16)
            dl_ref[:, ps] = _half_mean(dov * ov, lo_half) * float(F_HD)

    t = pl.BlockSpec((tm, 1024), lambda i: (i, 0))
    sd = jax.ShapeDtypeStruct
    return pl.pallas_call(
        body, name="fox_gate_bwd", grid=(S // tm,),
        in_specs=[t, pl.BlockSpec(w_out.shape, lambda i: (0, 0)), t, pl.BlockSpec((tm, 1024), lambda i: (i, 3))],
        out_specs=[t, t, t],
        out_shape=[sd((S, 1024), bf16), sd((S, 1024), bf16), sd((S, 1024), f32)],
        compiler_params=_params(("parallel",)),
    )(dy, w_out, o, proj)


def fox_pre_bwd(proj, fbias, qw2, kw2, dqa, dka, dv, dz):
    S = proj.shape[0]
    tm = 512
    nb = S // tm

    def body(q_ref, k_ref, f_ref, fb_ref, qw_ref, kw_ref, dqa_ref, dka_ref, dv_ref, dz_ref, dp_ref, st_ref, carry):
        i = pl.program_id(0)

        @pl.when(i == 0)
        def _():
            carry[...] = jnp.zeros_like(carry)

        lane = _iota((tm, 128), 1)
        lo_half = lane < F_HD
        lane16 = _iota((tm, 16), 1)
        dcum = jnp.zeros((tm, 16), f32)
        dws = []
        for src, w_ref, dsrc, is_q, col0 in ((q_ref, qw_ref, dqa_ref, True, 0), (k_ref, kw_ref, dka_ref, False, 1024)):
            dw = jnp.zeros((1, 128), f32)
            for p in range(F_H // 2):
                ps = slice(p * 128, (p + 1) * 128)
                x = src[:, ps]
                r = lax.rsqrt(_half_mean(x * x, lo_half) + EPS)
                xh = x * r
                d0 = dsrc[:, (2 * p) * 128:(2 * p + 1) * 128]
                d1 = dsrc[:, (2 * p + 1) * 128:(2 * p + 2) * 128]
                dy = jnp.where(lo_half, d0, pltpu.roll(d1, F_HD, 1))
                if is_q:
                    dy = dy * (F_HD ** -0.5)
                dxh = dy * w_ref[...]
                dw = dw + jnp.sum(dy * xh, axis=0, keepdims=True)
                dp_ref[:, col0 + p * 128:col0 + (p + 1) * 128] = (r * (dxh - xh * _half_mean(dxh * xh, lo_half))).astype(bf16)
                for e, de in ((0, d0), (1, d1)):
                    col = de[:, 64:65] if is_q else -de[:, 67:68]
                    dcum = dcum + jnp.where(lane16 == 2 * p + e, col, 0.0)
            dws.append(dw)
        dp_ref[:, 2048:3072] = dv_ref[...].astype(bf16)
        dp_ref[:, 3072:4096] = dz_ref[...]
        utri = jnp.where(_iota((tm, tm), 1) >= _iota((tm, tm), 0), 1.0, 0.0).astype(f32)
        dlogf = _nn(utri, dcum, HI) + carry[0:1, :]
        carry[0:1, :] = dlogf[0:1, :]
        fl = f_ref[:, 0:16] + fb_ref[...]
        df = dlogf * _sigmoid(-fl)
        place = jnp.where(_iota((16, 128), 1) == _iota((16, 128), 0), 1.0, 0.0).astype(f32)
        dfw = _nn(df, place, HI)
        dp_ref[:, F_F0:F_INP] = dfw.astype(bf16)
        upd = jnp.concatenate(dws + [jnp.sum(dfw, axis=0, keepdims=True), jnp.zeros((5, 128), f32)], axis=0)

        @pl.when(i == 0)
        def _():
            st_ref[...] = upd

        @pl.when(i > 0)
        def _():
            st_ref[...] += upd

    rev = lambda w, c: pl.BlockSpec((tm, w), lambda i: (nb - 1 - i, c))
    vec = lambda n: pl.BlockSpec((1, n), lambda i: (0, 0))
    sd = jax.ShapeDtypeStruct
    return pl.pallas_call(
        body, name="fox_pre_bwd", grid=(nb,),
        in_specs=[rev(1024, 0), rev(1024, 1), rev(128, F_F0 // 128), vec(16), vec(128), vec(128),
                  rev(2048, 0), rev(2048, 0), rev(1024, 0), rev(1024, 0)],
        out_specs=[rev(F_INP, 0), pl.BlockSpec((8, 128), lambda i: (0, 0))],
        out_shape=[sd((S, F_INP), bf16), sd((8, 128), f32)],
        scratch_shapes=[pltpu.VMEM((8, 16), f32)],
        compiler_params=_params(("arbitrary",)),
    )(proj, proj, proj, fbias, qw2, kw2, dqa, dka, dv, dz)


def _me():
    return lax.axis_index("x"), lax.axis_index("y"), lax.axis_index("c")


def _other_chips(x, y):
    return [(1 - x, y), (x, 1 - y), (1 - x, 1 - y)]


def ag_small(xs):
    m_per, n = xs.shape

    def body(x_ref, out_ref, send_sems, recv_sems, local_sem):
        x, y, c = _me()
        me, sibling = (x, y, c), (x, y, 1 - c)
        chips = _other_chips(x, y)

        def rows(px, py, pc):
            return out_ref.at[pl.ds((4 * px + 2 * py + pc) * m_per, m_per), :]

        def copy(k, block, to, src=None):
            return pltpu.make_async_remote_copy(
                src_ref=rows(*block) if src is None else src, dst_ref=rows(*block),
                send_sem=send_sems.at[k], recv_sem=recv_sems.at[k], device_id=to, device_id_type=MESH)

        mine = pltpu.make_async_copy(x_ref, rows(*me), local_sem)
        mine.start()
        first = [copy(0, me, sibling, src=x_ref)]
        first += [copy(1 + j, me, (*chip, c), src=x_ref) for j, chip in enumerate(chips)]
        for cp in first:
            cp.start()
        passed = [copy(4 + j, (*chip, c), sibling) for j, chip in enumerate(chips)]
        for j, chip in enumerate(chips):
            copy(1 + j, (*chip, c), me).wait_recv()
            passed[j].start()
        copy(0, sibling, me).wait_recv()
        for j, chip in enumerate(chips):
            copy(4 + j, (*chip, 1 - c), me).wait_recv()
        for cp in first + passed:
            cp.wait_send()
        mine.wait()

    return pl.pallas_call(
        body, name="ag_small",
        out_shape=jax.ShapeDtypeStruct((8 * m_per, n), xs.dtype),
        in_specs=[pl.BlockSpec(memory_space=pltpu.VMEM)], out_specs=pl.BlockSpec(memory_space=pltpu.VMEM),
        scratch_shapes=[pltpu.SemaphoreType.DMA((7,)), pltpu.SemaphoreType.DMA((7,)), pltpu.SemaphoreType.DMA],
        compiler_params=pltpu.CompilerParams(vmem_limit_bytes=VMEM_LIMIT),
    )(xs)


_ANY = pl.BlockSpec(memory_space=pl.ANY)


def ag_chips(arrs):
    n = len(arrs)
    assert all(a.shape[0] == 2 for a in arrs)

    def body(*refs):
        ins, outs = refs[:n], refs[n:2 * n]
        send_sems, recv_sems, fwd_send, fwd_recv, local_sems = refs[2 * n:]
        x, y, c = _me()
        me = 2 * x + y
        chips = _other_chips(x, y)
        started = []
        for a in range(n):
            cp = pltpu.make_async_copy(ins[a], outs[a].at[me], local_sems.at[a])
            cp.start()
            started.append(cp)
        sends = []
        for a in range(n):
            for j, (px, py) in enumerate(chips):
                r = pltpu.make_async_remote_copy(
                    src_ref=ins[a].at[c], dst_ref=outs[a].at[me, c], send_sem=send_sems.at[3 * a + j],
                    recv_sem=recv_sems.at[3 * a + j], device_id=(px, py, c), device_id_type=MESH)
                r.start()
                sends.append(r)
        for a in range(n):
            for j, (px, py) in enumerate(chips):
                got = outs[a].at[2 * px + py, c]
                pltpu.make_async_remote_copy(
                    src_ref=ins[a].at[c], dst_ref=got, send_sem=send_sems.at[3 * a + j],
                    recv_sem=recv_sems.at[3 * a + j], device_id=(px, py, c), device_id_type=MESH).wait_recv()
                f = pltpu.make_async_remote_copy(
                    src_ref=got, dst_ref=got, send_sem=fwd_send.at[3 * a + j], recv_sem=fwd_recv.at[3 * a + j],
                    device_id=(x, y, 1 - c), device_id_type=MESH)
                f.start()
                sends.append(f)
        for a in range(n):
            for j, (px, py) in enumerate(chips):
                theirs = outs[a].at[2 * px + py, 1 - c]
                pltpu.make_async_remote_copy(
                    src_ref=theirs, dst_ref=theirs, send_sem=fwd_send.at[3 * a + j], recv_sem=fwd_recv.at[3 * a + j],
                    device_id=(x, y, 1 - c), device_id_type=MESH).wait_recv()
        for r in sends:
            r.wait_send()
        for cp in started:
            cp.wait()

    sems = pltpu.SemaphoreType.DMA((3 * n,))
    return pl.pallas_call(
        body, name="ag_chips",
        out_shape=[jax.ShapeDtypeStruct((4,) + a.shape, a.dtype) for a in arrs],
        in_specs=[_ANY] * n, out_specs=[_ANY] * n,
        scratch_shapes=[sems, sems, sems, sems, pltpu.SemaphoreType.DMA((n,))],
    )(*arrs)


def _ag_comm(arrs):
    n = len(arrs)

    def copies(ins, outs, sems, inbound):
        send_sems, recv_sems, local_sems = sems
        x, y, c = _me()
        me = 2 * x + y
        local = [pltpu.make_async_copy(ins[a], outs[a].at[me], local_sems.at[a]) for a in range(n)]
        out_cp, in_cp = [], []
        for a in range(n):
            for j, (px, py) in enumerate(_other_chips(x, y)):
                mk = functools.partial(pltpu.make_async_remote_copy, src_ref=ins[a], send_sem=send_sems.at[3 * a + j],
                                       recv_sem=recv_sems.at[3 * a + j], device_id=(px, py, c), device_id_type=MESH)
                out_cp.append(mk(dst_ref=outs[a].at[me]))
                if inbound:
                    in_cp.append(mk(dst_ref=outs[a].at[2 * px + py]))
        return local, out_cp, in_cp

    def start(ins, outs, sems):
        local, out_cp, _ = copies(ins, outs, sems, False)
        for cp in local + out_cp:
            cp.start()

    def wait(ins, outs, sems):
        local, out_cp, in_cp = copies(ins, outs, sems, True)
        for cp in in_cp:
            cp.wait_recv()
        for cp in out_cp:
            cp.wait_send()
        for cp in local:
            cp.wait()

    sems = [pltpu.SemaphoreType.DMA((3 * n,)), pltpu.SemaphoreType.DMA((3 * n,)), pltpu.SemaphoreType.DMA((n,))]
    return _Comm(arrs, [jax.ShapeDtypeStruct((4,) + a.shape, a.dtype) for a in arrs], sems, start, wait)


def _rs_comm(gs):
    n = len(gs)
    flips = [(fx, fy, fc) for fx in (0, 1) for fy in (0, 1) for fc in (0, 1)][1:]

    def copies(ins, outs, sems, inbound):
        send_sems, recv_sems, local_sems = sems
        x, y, c = _me()
        me = 4 * x + 2 * y + c
        local, out_cp, in_cp = [], [], []
        for a in range(n):
            rh = ins[a].shape[1] // 2
            mine = ins[a].at[2 * x + y, pl.ds(c * rh, rh), :]
            local.append(pltpu.make_async_copy(mine, outs[a].at[me], local_sems.at[a]))
            for j, (fx, fy, fc) in enumerate(flips):
                px, py, pc = (1 - x if fx else x), (1 - y if fy else y), (1 - c if fc else c)
                mk = functools.partial(pltpu.make_async_remote_copy, send_sem=send_sems.at[7 * a + j],
                                       recv_sem=recv_sems.at[7 * a + j], device_id=(px, py, pc), device_id_type=MESH)
                out_cp.append(mk(src_ref=ins[a].at[2 * px + py, pl.ds(pc * rh, rh), :], dst_ref=outs[a].at[me]))
                if inbound:
                    in_cp.append(mk(src_ref=mine, dst_ref=outs[a].at[4 * px + 2 * py + pc]))
        return local, out_cp, in_cp

    def start(ins, outs, sems):
        local, out_cp, _ = copies(ins, outs, sems, False)
        for cp in local + out_cp:
            cp.start()

    def wait(ins, outs, sems):
        local, out_cp, in_cp = copies(ins, outs, sems, True)
        for cp in in_cp:
            cp.wait_recv()
        for cp in out_cp:
            cp.wait_send()
        for cp in local:
            cp.wait()

    sems = [pltpu.SemaphoreType.DMA((7 * n,)), pltpu.SemaphoreType.DMA((7 * n,)), pltpu.SemaphoreType.DMA((n,))]
    return _Comm(gs, [jax.ShapeDtypeStruct((8, g.shape[1] // 2, g.shape[2]), g.dtype) for g in gs], sems, start, wait)


def sum_leading(q, name):
    K, R, C = q.shape
    tr = _pick(R, (256, 128, 64, 32, 16, 8))

    def body(q_ref, o_ref):
        acc = q_ref[0]
        for k in range(1, K):
            acc = acc + q_ref[k]
        o_ref[...] = acc

    return pl.pallas_call(
        body, name=name, grid=(R // tr,),
        in_specs=[pl.BlockSpec((K, tr, C), lambda i: (0, i, 0))], out_specs=pl.BlockSpec((tr, C), lambda i: (i, 0)),
        out_shape=jax.ShapeDtypeStruct((R, C), f32),
        compiler_params=_params(("parallel",)),
    )(q)


def rs_sum_devices(q, cidx, layer, n_layers, into=None):
    K, R, C = q.shape
    tr = _pick(R, (256, 128))

    def body(c_ref, q_ref, *rest):
        acc = q_ref[0].astype(f32)
        for k in range(1, K):
            acc = acc + q_ref[k].astype(f32)
        rest[-1][0, 0] = acc

    return pl.pallas_call(
        body, name="rs_sum_devices",
        grid_spec=pltpu.PrefetchScalarGridSpec(
            num_scalar_prefetch=1, grid=(R // tr,),
            in_specs=[pl.BlockSpec((K, tr, C), lambda i, c_ref: (0, i, 0))] + ([] if into is None else [_ANY]),
            out_specs=pl.BlockSpec((1, 1, tr, C), lambda i, c_ref: (layer, c_ref[0], i, 0))),
        out_shape=jax.ShapeDtypeStruct((n_layers, 2, R, C), f32),
        input_output_aliases={} if into is None else {2: 0},
        compiler_params=_params(("parallel",)),
    )(cidx, q, *([] if into is None else [into]))


def rs_share_halves(rs):
    n = len(rs)

    def body(*refs):
        bufs = refs[n:2 * n]
        send_sems, recv_sems = refs[2 * n:]
        x, y, c = _me()
        cps = []
        for a in range(n):
            mine = bufs[a].at[pl.ds(0, bufs[a].shape[0]), c]
            cp = pltpu.make_async_remote_copy(
                src_ref=mine, dst_ref=mine, send_sem=send_sems.at[a], recv_sem=recv_sems.at[a],
                device_id=(x, y, 1 - c), device_id_type=MESH)
            cp.start()
            cps.append(cp)
        for a, cp in enumerate(cps):
            theirs = bufs[a].at[pl.ds(0, bufs[a].shape[0]), 1 - c]
            pltpu.make_async_remote_copy(
                src_ref=theirs, dst_ref=theirs, send_sem=send_sems.at[a], recv_sem=recv_sems.at[a],
                device_id=(x, y, 1 - c), device_id_type=MESH).wait_recv()
            cp.wait_send()

    return pl.pallas_call(
        body, name="rs_share_halves",
        out_shape=[jax.ShapeDtypeStruct(r.shape, r.dtype) for r in rs],
        in_specs=[_ANY] * n, out_specs=[_ANY] * n, input_output_aliases={a: a for a in range(n)},
        scratch_shapes=[pltpu.SemaphoreType.DMA((n,)), pltpu.SemaphoreType.DMA((n,))],
    )(*rs)


def ada_mod(c_all, ada_w):
    L, _, n = ada_w.shape

    def body(c_ref, w_ref, o_ref):
        o_ref[0] = _nn(_silu(c_ref[...]), w_ref[0], HI)

    return pl.pallas_call(
        body, name="ada_mod", grid=(L,),
        in_specs=[pl.BlockSpec((8, D), lambda l: (0, 0)), pl.BlockSpec((1, D, n), lambda l: (l, 0, 0))],
        out_specs=pl.BlockSpec((1, 8, n), lambda l: (l, 0, 0)),
        out_shape=jax.ShapeDtypeStruct((L, 8, n), f32),
        compiler_params=_params(("parallel",)),
    )(c_all, ada_w)


def ada_w_grad(c_all, dmod):
    L, _, n = dmod.shape

    def body(c_ref, d_ref, o_ref):
        o_ref[0] = _tn(_silu(c_ref[...]), d_ref[0], HI)

    return pl.pallas_call(
        body, name="ada_w_grad", grid=(L,),
        in_specs=[pl.BlockSpec((8, D), lambda l: (0, 0)), pl.BlockSpec((1, 8, n), lambda l: (l, 0, 0))],
        out_specs=pl.BlockSpec((1, D, n), lambda l: (l, 0, 0)),
        out_shape=jax.ShapeDtypeStruct((L, D, n), f32),
        compiler_params=_params(("parallel",)),
    )(c_all, dmod)


def adamw(w, g, m, v, name):
    shp = w.shape
    two = lambda a: a.reshape(-1, shp[-1])
    R, C = two(w).shape
    tr = _pick(R, (256, 128, 64, 32, 16, 8))
    bc1, bc2 = 1.0 - B1 ** STEP, 1.0 - B2 ** STEP

    def body(w_ref, g_ref, m_ref, v_ref, d_ref, mo_ref, vo_ref):
        gv = g_ref[...]
        mn = B1 * m_ref[...] + (1.0 - B1) * gv
        vn = B2 * v_ref[...] + (1.0 - B2) * (gv * gv)
        d_ref[...] = -LR * ((mn / bc1) / (jnp.sqrt(vn / bc2) + AEPS) + WD * w_ref[...])
        mo_ref[...] = mn
        vo_ref[...] = vn

    t = pl.BlockSpec((tr, C), lambda i: (i, 0))
    outs = pl.pallas_call(
        body, name=name, grid=(R // tr,),
        in_specs=[t] * 4, out_specs=[t] * 3, out_shape=[jax.ShapeDtypeStruct((R, C), f32)] * 3,
        compiler_params=_params(("parallel",)),
    )(two(w), two(g), two(m), two(v))
    return [o.reshape(shp) for o in outs]


def _pack(arrs):
    parts, offs, r0 = [], [], 0
    for a in arrs:
        n = a.size
        rows = -(-n // 1024) * 8
        parts.append(jnp.pad(a.reshape(-1), (0, rows * 128 - n)).reshape(rows, 128))
        offs.append((r0, rows))
        r0 += rows
    return jnp.concatenate(parts, axis=0), offs


def _unpack(buf, offs, shapes):
    out = []
    for (r0, rows), shp in zip(offs, shapes):
        n = 1
        for d in shp:
            n *= d
        out.append(buf[..., r0:r0 + rows, :].reshape(buf.shape[:-2] + (rows * 128,))[..., :n].reshape(buf.shape[:-2] + tuple(shp)))
    return out


def kernel(x, c, norm_w, ada_w, ada_b, a_w_in, a_conv_w, a_A_log, a_dt_bias, a_norm_w, a_w_out, b_w_in, b_f_bias, b_qn_w, b_kn_w, b_w_out, final_norm_w, loss_target, m_norm_w, m_ada_w, m_ada_b, m_a_w_in, m_a_conv_w, m_a_A_log, m_a_dt_bias, m_a_norm_w, m_a_w_out, m_b_w_in, m_b_f_bias, m_b_qn_w, m_b_kn_w, m_b_w_out, m_final_norm_w, v_norm_w, v_ada_w, v_ada_b, v_a_w_in, v_a_conv_w, v_a_A_log, v_a_dt_bias, v_a_norm_w, v_a_w_out, v_b_w_in, v_b_f_bias, v_b_qn_w, v_b_kn_w, v_b_w_out, v_final_norm_w):
    weights = dict(norm_w=norm_w, ada_w=ada_w, ada_b=ada_b, a_w_in=a_w_in, a_conv_w=a_conv_w, a_A_log=a_A_log,
                   a_dt_bias=a_dt_bias, a_norm_w=a_norm_w, a_w_out=a_w_out, b_w_in=b_w_in, b_f_bias=b_f_bias,
                   b_qn_w=b_qn_w, b_kn_w=b_kn_w, b_w_out=b_w_out, final_norm_w=final_norm_w)
    m_in = dict(norm_w=m_norm_w, ada_w=m_ada_w, ada_b=m_ada_b, a_w_in=m_a_w_in, a_conv_w=m_a_conv_w, a_A_log=m_a_A_log,
                a_dt_bias=m_a_dt_bias, a_norm_w=m_a_norm_w, a_w_out=m_a_w_out, b_w_in=m_b_w_in, b_f_bias=m_b_f_bias,
                b_qn_w=m_b_qn_w, b_kn_w=m_b_kn_w, b_w_out=m_b_w_out, final_norm_w=m_final_norm_w)
    v_in = dict(norm_w=v_norm_w, ada_w=v_ada_w, ada_b=v_ada_b, a_w_in=v_a_w_in, a_conv_w=v_a_conv_w, a_A_log=v_a_A_log,
                a_dt_bias=v_a_dt_bias, a_norm_w=v_a_norm_w, a_w_out=v_a_w_out, b_w_in=v_b_w_in, b_f_bias=v_b_f_bias,
                b_qn_w=v_b_qn_w, b_kn_w=v_b_kn_w, b_w_out=v_b_w_out, final_norm_w=v_final_norm_w)
    xi, yi, ci = _me()
    me_b, me_k = 4 * xi + 2 * yi + ci, 2 * xi + yi
    cidx = ci.astype(jnp.int32).reshape(1)
    S = x.shape[1]
    depth, n_a, n_b = norm_w.shape[0], a_w_in.shape[0], b_w_in.shape[0]
    x0, tgt = x.reshape(S, D), loss_target.reshape(S, D)

    c_all = ag_small(jnp.pad(c, ((0, 7), (0, 0)))).reshape(8, 8, D)[:, 0]
    nloc = ada_w.shape[2]
    parts = ag_small(ada_mod(c_all, ada_w).reshape(depth * 8, nloc)).reshape(4, 2, depth, 8, nloc)[:, 0]
    mine = lax.dynamic_index_in_dim(parts, me_b, axis=2, keepdims=False)
    mod = jnp.transpose(mine, (1, 0, 2)).reshape(depth, 4 * nloc) + ada_b
    shift, scale, gate = (mod[:, k * D:(k + 1) * D] for k in range(3))

    w_loc = [(a_w_in[i // 2] if i % 2 == 0 else b_w_in[i // 2]).astype(bf16) for i in range(depth)]
    wo_loc = [(a_w_out[i // 2] if i % 2 == 0 else b_w_out[i // 2]).astype(bf16) for i in range(depth)]
    pad_in = [(G_INP - G_IN) if i % 2 == 0 else (F_INP - F_IN) for i in range(depth)]
    halves = lambda w: w.reshape((2, w.shape[0] // 2) + w.shape[1:])

    def cols_in_place(g_in, pad):
        w = jnp.transpose(g_in, (1, 0, 2)).reshape(g_in.shape[1], -1)
        return jnp.pad(w, ((0, 0), (0, pad)))

    g_in0, g_conv = ag_chips([halves(w_loc[0]), a_conv_w])
    w_in_full = [cols_in_place(g_in0.reshape((4,) + w_loc[0].shape), pad_in[0])]
    w_out_full = []
    conv = [jnp.transpose(g_conv[:, l], (1, 0, 2)).reshape(CONV_K, -1) for l in range(n_a)]
    qw2 = [_row(jnp.tile(b_qn_w[l], 2)) for l in range(n_b)]
    kw2 = [_row(jnp.tile(b_kn_w[l], 2)) for l in range(n_b)]

    saved, xc = [], x0
    for i in range(depth):
        l = i // 2
        nxt = _ag_comm([w_loc[i + 1], wo_loc[i + 1]]) if i + 1 < depth else None
        h = ln_mod(xc, _row(norm_w[i]), _row(scale[i]), _row(shift[i]))
        name = "mm_a_in" if i % 2 == 0 else "mm_b_in"
        if i == 0:
            proj, got = matmul(h, w_in_full[0], "nn", name, comm=_ag_comm([wo_loc[0]]))
            w_out_full.append(got[0].reshape(-1, D))
        else:
            proj = matmul(h, w_in_full[i], "nn", name)
        if i % 2 == 0:
            pre = gdn_pre(proj, conv[l], _row(a_A_log[l]), _row(a_dt_bias[l]))
            res, got = gdn_fwd(*pre, comm=nxt)
            o2 = gdn_onorm(res[0], proj, _row(a_norm_w[l]))
            y, xn = out_proj(o2, w_out_full[i], xc, _row(gate[i]), "out_proj_a")
        else:
            pre = fox_pre(proj, _row(b_f_bias[l]), qw2[l], kw2[l])
            res, got = fox_attn(*pre, comm=nxt)
            o2 = fox_gate(res[0], proj)
            y, xn = out_proj(o2, w_out_full[i], xc, _row(gate[i]), "out_proj_b")
        saved.append((xc, h, proj, o2, y, pre, res))
        if nxt is not None:
            w_in_full.append(cols_in_place(got[0], pad_in[i + 1]))
            w_out_full.append(got[1].reshape(-1, D))
        xc = xn
    dx, st_f = final_loss(xc, _row(final_norm_w), tgt)

    d_norm, d_mod = [None] * depth, [None] * depth
    d_conv, d_alog, d_dtb, d_anw = [None] * n_a, [None] * n_a, [None] * n_a, [None] * n_a
    d_fb, d_qn, d_kn = [None] * n_b, [None] * n_b, [None] * n_b
    ex_in, ex_out, pend_in = [None] * depth, [None] * depth, None
    for i in reversed(range(depth)):
        l = i // 2
        xin, h, proj, o2, y, pre, res = saved[i]
        ab = "a" if i % 2 == 0 else "b"
        dy, st_g = gate_bwd(dx, y, _row(gate[i]))
        d_out = matmul(o2, dy, "tn", f"mm_{ab}_dwo", out_dtype=bf16)
        ride = _rs_comm(([] if pend_in is None else [pend_in]) + [d_out.reshape(4, d_out.shape[0] // 4, D)])
        if i % 2 == 0:
            o, wv, at, tinv, vn, st = res
            do, dz, st_o = gdn_onorm_bwd(dy, w_out_full[i], o, proj, _row(a_norm_w[l]))
            grads, got = gdn_bwd(do, *pre, wv, at, tinv, vn, st, comm=ride)
            dcv, dba, st_s = gdn_pre_bwd(proj, conv[l], _row(a_A_log[l]), _row(a_dt_bias[l]), *grads)
            dproj, dcw = gdn_conv_bwd(proj, conv[l], dcv, dz, dba)
            d_conv[l], d_alog[l], d_dtb[l], d_anw[l] = dcw[:CONV_K], st_s[0], st_s[1], st_o[0]
        else:
            o, lse = res
            do, dz, delta = fox_gate_bwd(dy, w_out_full[i], o, proj)
            (dqa, dka, dv), got = fox_attn_bwd(*pre, do, lse, delta, comm=ride)
            dproj, st_b = fox_pre_bwd(proj, _row(b_f_bias[l]), qw2[l], kw2[l], dqa, dka, dv, dz)
            d_fb[l], d_qn[l], d_kn[l] = st_b[2, :F_H], st_b[0, :F_HD] + st_b[0, F_HD:], st_b[1, :F_HD] + st_b[1, F_HD:]
        ex_out[i] = got[-1]
        if pend_in is not None:
            ex_in[i + 1] = got[0]
        d_in = matmul(h, dproj, "tn", f"mm_{ab}_dw", out_dtype=bf16)
        cl = w_loc[i].shape[1]
        pend_in = jnp.transpose(d_in[:, :4 * cl].reshape(d_in.shape[0], 4, cl), (1, 0, 2))
        if i == 0:
            dh, got = matmul(dproj, w_in_full[i], "nt", f"mm_{ab}_dh", comm=_rs_comm([pend_in]))
            ex_in[0] = got[0]
        else:
            dh = matmul(dproj, w_in_full[i], "nt", f"mm_{ab}_dh")
        dx, st_n = ln_mod_bwd(xin, _row(norm_w[i]), _row(scale[i]), dh, dx)
        d_norm[i] = st_n[0]
        d_mod[i] = jnp.concatenate([st_n[2], st_n[1], st_g[0]])

    small = [jnp.stack(d_norm), jnp.stack(d_mod), jnp.stack(d_conv), jnp.stack(d_alog), jnp.stack(d_dtb), jnp.stack(d_anw),
             jnp.stack(d_fb), jnp.stack(d_qn), jnp.stack(d_kn), st_f[0], jnp.sum(st_f[1]).reshape(1)]
    shapes = [a.shape for a in small]
    buf, offs = _pack(small)
    gathered = ag_small(buf).reshape(8, buf.shape[0], 128)
    tot = _unpack(sum_leading(gathered, "sum_devices"), offs, shapes)
    g_norm, g_adab, g_convf, g_alog, g_dtb, g_anw, g_fb, g_qn, g_kn, g_fin, loss = tot
    dmod_all = _unpack(gathered, offs[1:2], shapes[1:2])[0]
    dmod_loc = lax.dynamic_slice_in_dim(dmod_all, me_k * nloc, nloc, axis=2)
    g_adaw = ada_w_grad(c_all, jnp.transpose(dmod_loc, (1, 0, 2)))
    g_conv_loc = lax.dynamic_slice_in_dim(g_convf, me_k * a_conv_w.shape[2], a_conv_w.shape[2], axis=2)

    bufs = {}
    for i in range(depth):
        for which, q in (("in", ex_in[i]), ("out", ex_out[i])):
            key = ("a" if i % 2 == 0 else "b", which)
            bufs[key] = rs_sum_devices(q, cidx, i // 2, depth // 2, into=bufs.get(key))
    keys = list(bufs)
    done = dict(zip(keys, rs_share_halves([bufs[k] for k in keys])))
    grads = dict(norm_w=g_norm, ada_w=g_adaw, ada_b=g_adab, a_w_in=done["a", "in"].reshape(a_w_in.shape),
                 a_conv_w=g_conv_loc, a_A_log=g_alog, a_dt_bias=g_dtb, a_norm_w=g_anw,
                 a_w_out=done["a", "out"].reshape(a_w_out.shape), b_w_in=done["b", "in"].reshape(b_w_in.shape),
                 b_f_bias=g_fb, b_qn_w=g_qn, b_kn_w=g_kn, b_w_out=done["b", "out"].reshape(b_w_out.shape),
                 final_norm_w=g_fin)
    names = list(weights)
    upd = {n: adamw(weights[n], grads[n], m_in[n], v_in[n], "adamw_" + n) for n in names}
    return (loss.reshape(()), dx.reshape(x.shape), *[grads[n] for n in names], *[upd[n][0] for n in names],
            *[upd[n][1] for n in names], *[upd[n][2] for n in names])
```
